```python
import math
import jax, jax.numpy as jnp
from jax import lax
import numpy as np

D_MODEL = 2048
BATCH = 8
SEQ = 4096
DEPTH = 1

HEAD_DIM = 128
N_ATTN_HEADS = 8
N_DELTA_HEADS = 8
ATTN_WIDTH = N_ATTN_HEADS * HEAD_DIM
DELTA_WIDTH = N_DELTA_HEADS * HEAD_DIM
MIX_WIDTH = ATTN_WIDTH + DELTA_WIDTH
DILATED_PATTERNS = ((128, 1), (512, 4), (2048, 16))
Q_BLOCK = 128
ROPE_THETA = 500000.0
ROPE_DIM = HEAD_DIM // 4
CONV_WIDTH = 4
CHUNK = 64
D_FF = 5632
NORM_EPS = 1e-6
N_MOD = 9
IN_PROJ_WIDTH = 3 * ATTN_WIDTH + 4 * DELTA_WIDTH + 2 * N_DELTA_HEADS

kernel_name = "hybrid_dilated_attn_gated_deltanet_macaron_layer"


def rms_norm(x, gain):
    xf = x.astype(jnp.float32)
    y = xf * lax.rsqrt(jnp.mean(xf * xf, axis=-1, keepdims=True) + NORM_EPS)
    return (y * gain.astype(jnp.float32)).astype(x.dtype)


def l2_norm(x):
    xf = x.astype(jnp.float32)
    return xf * lax.rsqrt(jnp.sum(xf * xf, axis=-1, keepdims=True) + NORM_EPS)


def modulate(x, gain, shift, scale):
    return rms_norm(x, gain) * (1.0 + scale[:, None, :]) + shift[:, None, :]


def swiglu(h, w_gate, w_up, w_down):
    return (jax.nn.silu(h @ w_gate) * (h @ w_up)) @ w_down


def partial_rope(x, positions):
    half = ROPE_DIM // 2
    inv_freq = ROPE_THETA ** (-jnp.arange(half, dtype=jnp.float32) / half)
    ang = positions.astype(jnp.float32)[..., None] * inv_freq
    cos = jnp.cos(ang)[:, :, None, :]
    sin = jnp.sin(ang)[:, :, None, :]
    x1 = x[..., :half].astype(jnp.float32)
    x2 = x[..., half:ROPE_DIM].astype(jnp.float32)
    rot = jnp.concatenate([x1 * cos - x2 * sin, x2 * cos + x1 * sin], axis=-1).astype(x.dtype)
    return jnp.concatenate([rot, x[..., ROPE_DIM:]], axis=-1)


def dilated_window_branch(q, k, v, dilation, span):
    B, Sp, H, Dh = q.shape
    L = Sp // dilation
    nblk = L // Q_BLOCK

    def residue_blocks(t):
        t = t.reshape(B, L, dilation, H, Dh).transpose(0, 2, 1, 3, 4)
        return t.reshape(B, dilation, nblk, Q_BLOCK, H, Dh)

    def with_previous_block(t):
        prev = jnp.pad(t[:, :, :-1], ((0, 0), (0, 0), (1, 0), (0, 0), (0, 0), (0, 0)))
        return jnp.concatenate([prev, t], axis=3)

    qb = residue_blocks(q)
    kw = with_previous_block(residue_blocks(k))
    vw = with_previous_block(residue_blocks(v))
    s = jnp.einsum('brnqhd,brnkhd->brnhqk', qb, kw,
                   preferred_element_type=jnp.float32) * (HEAD_DIM ** -0.5)
    qi = jnp.arange(Q_BLOCK)[:, None]
    kj = jnp.arange(2 * Q_BLOCK)[None, :]
    dist = qi + Q_BLOCK - kj
    key_index = jnp.arange(nblk)[:, None, None] * Q_BLOCK - Q_BLOCK + kj[None]
    mask = (dist >= 0) & (dist <= span) & (key_index >= 0)
    s = jnp.where(mask[None, None, :, None], s, -jnp.inf)
    m = jnp.max(s, axis=-1, keepdims=True)
    p = jnp.exp(s - m)
    denom = jnp.sum(p, axis=-1)
    o = jnp.einsum('brnhqk,brnkhd->brnqhd', p, vw.astype(jnp.float32))
    o = o / jnp.swapaxes(denom, 3, 4)[..., None]
    lse = jnp.swapaxes(m[..., 0] + jnp.log(denom), 3, 4)

    def to_sequence(t):
        t = t.reshape(B, dilation, L, *t.shape[4:])
        return jnp.swapaxes(t, 1, 2).reshape(B, Sp, *t.shape[3:])

    return to_sequence(o), to_sequence(lse)


def dilated_attention(q, k, v):
    B, S, H, Dh = q.shape
    unit = Q_BLOCK
    for _, d in DILATED_PATTERNS:
        unit = unit * d // math.gcd(unit, d * Q_BLOCK) if False else math.lcm(unit, d * Q_BLOCK)
    Sp = -(-S // unit) * unit
    pad = ((0, 0), (0, Sp - S), (0, 0), (0, 0))
    q, k, v = (jnp.pad(t, pad) for t in (q, k, v))
    outs, lses = [], []
    for window, dilation in DILATED_PATTERNS:
        o, lse = dilated_window_branch(q, k, v, dilation, window // dilation)
        outs.append(o)
        lses.append(lse)
    wts = jax.nn.softmax(jnp.stack(lses, axis=0), axis=0)
    o = jnp.einsum('pbsh,pbshd->bshd', wts, jnp.stack(outs, axis=0))
    return o[:, :S].astype(q.dtype)


def causal_depthwise_conv(x, w):
    C = x.shape[-1]
    y = lax.conv_general_dilated(x, w[:, None, :].astype(x.dtype), window_strides=(1,),
                                 padding=[(CONV_WIDTH - 1, 0)],
                                 dimension_numbers=('NWC', 'WIO', 'NWC'),
                                 feature_group_count=C)
    return jax.nn.silu(y)


def gated_delta_rule(q, k, v, g, beta):
    B, S, H, Dk = q.shape
    Dv = v.shape[-1]
    N = S // CHUNK
    f32 = jnp.float32

    def chunk_vec(t):
        return t.astype(f32).reshape(B, N, CHUNK, H, t.shape[-1]).transpose(0, 3, 1, 2, 4)

    def chunk_scalar(t):
        return t.astype(f32).reshape(B, N, CHUNK, H).transpose(0, 3, 1, 2)

    q = chunk_vec(q) * (Dk ** -0.5)
    k, v = chunk_vec(k), chunk_vec(v)
    g, beta = chunk_scalar(g), chunk_scalar(beta)
    gc = jnp.cumsum(g, axis=-1)
    causal = jnp.tril(jnp.ones((CHUNK, CHUNK), bool))
    strict = jnp.tril(jnp.ones((CHUNK, CHUNK), bool), k=-1)
    gamma = jnp.exp(jnp.where(causal, gc[..., :, None] - gc[..., None, :], -jnp.inf))
    kb = k * beta[..., None]
    a = jnp.where(strict, jnp.einsum('bhnid,bhnjd->bhnij', kb, k) * gamma, 0.0)
    eye = jnp.eye(CHUNK, dtype=f32)
    t_inv = lax.linalg.triangular_solve(eye + a, jnp.broadcast_to(eye, a.shape),
                                        left_side=True, lower=True, unit_diagonal=True)
    u = t_inv @ (v * beta[..., None])
    w = t_inv @ (kb * jnp.exp(gc)[..., None])
    q_decay = q * jnp.exp(gc)[..., None]
    k_tail = k * jnp.exp(gc[..., -1:] - gc)[..., None]
    chunk_decay = jnp.exp(gc[..., -1])
    intra = jnp.einsum('bhnid,bhnjd->bhnij', q, k) * gamma

    def step(state, xs):
        u_c, w_c, qd_c, kt_c, intra_c, dec_c = xs
        v_new = u_c - jnp.einsum('bhcd,bhde->bhce', w_c, state)
        o_c = (jnp.einsum('bhcd,bhde->bhce', qd_c, state)
               + jnp.einsum('bhij,bhje->bhie', intra_c, v_new))
        state = state * dec_c[..., None, None] + jnp.einsum('bhcd,bhce->bhde', kt_c, v_new)
        return state, o_c

    xs = tuple(jnp.moveaxis(t, 2, 0) for t in (u, w, q_decay, k_tail, intra, chunk_decay))
    _, o = lax.scan(step, jnp.zeros((B, H, Dk, Dv), f32), xs)
    return o.transpose(1, 0, 3, 2, 4).reshape(B, S, H, Dv)


def hybrid_mixer(h, positions, w_in, conv_w, q_norm, k_norm, a_log, dt_bias, delta_out_norm, w_out):
    B, S, _ = h.shape
    proj = h @ w_in
    cuts = [3 * ATTN_WIDTH, 3 * ATTN_WIDTH + 3 * DELTA_WIDTH,
            3 * ATTN_WIDTH + 4 * DELTA_WIDTH, 3 * ATTN_WIDTH + 4 * DELTA_WIDTH + N_DELTA_HEADS]
    attn_qkv, delta_qkv, z, a_in, b_in = jnp.split(proj, cuts, axis=-1)

    qa, ka, va = (t.reshape(B, S, N_ATTN_HEADS, HEAD_DIM) for t in jnp.split(attn_qkv, 3, axis=-1))
    qa = partial_rope(rms_norm(qa, q_norm), positions)
    ka = partial_rope(rms_norm(ka, k_norm), positions)
    oa = dilated_attention(qa, ka, va)

    dqkv = causal_depthwise_conv(delta_qkv, conv_w)
    qd, kd, vd = (t.reshape(B, S, N_DELTA_HEADS, HEAD_DIM) for t in jnp.split(dqkv, 3, axis=-1))
    g = -jnp.exp(a_log.astype(jnp.float32)) * jax.nn.softplus(
        a_in.astype(jnp.float32) + dt_bias.astype(jnp.float32))
    beta = jax.nn.sigmoid(b_in.astype(jnp.float32))
    od = gated_delta_rule(l2_norm(qd), l2_norm(kd), vd, g, beta).astype(h.dtype)
    od = rms_norm(od, delta_out_norm) * jax.nn.silu(z.reshape(B, S, N_DELTA_HEADS, HEAD_DIM))

    o = jnp.concatenate([oa.reshape(B, S, ATTN_WIDTH), od.reshape(B, S, DELTA_WIDTH)], axis=-1)
    return o @ w_out


def _fwd_setup_inputs(seed: int = 0) -> dict:
    key = jax.random.key(seed)
    ks = jax.random.split(key, 24)
    f32 = jnp.float32
    D = D_MODEL

    def nrm(k, shape, scale):
        return jax.random.normal(k, shape, f32) * scale

    def gain(k, shape):
        return 1.0 + 0.02 * jax.random.normal(k, shape, f32)

    start = jax.random.randint(ks[2], (BATCH, 1), 0, 1024, dtype=jnp.int32)
    positions = start + jnp.arange(SEQ, dtype=jnp.int32)[None, :]
    dt = jnp.exp(jax.random.uniform(ks[14], (DEPTH, N_DELTA_HEADS), f32,
                                    math.log(1e-3), math.log(1e-1)))
    return {
        "x": nrm(ks[0], (BATCH, SEQ, D), 1.0),
        "c": nrm(ks[1], (BATCH, D), 1.0),
        "positions": positions,
        "w_ada": nrm(ks[3], (DEPTH, D, N_MOD * D), 0.5 * D ** -0.5),
        "b_ada": nrm(ks[4], (DEPTH, N_MOD * D), 0.01),
        "ffn1_norm": gain(ks[5], (DEPTH, D)),
        "ffn1_w_gate": nrm(ks[6], (DEPTH, D, D_FF), D ** -0.5),
        "ffn1_w_up": nrm(ks[7], (DEPTH, D, D_FF), D ** -0.5),
        "ffn1_w_down": nrm(ks[8], (DEPTH, D_FF, D), D_FF ** -0.5),
        "mix_norm": gain(ks[9], (DEPTH, D)),
        "w_in": nrm(ks[10], (DEPTH, D, IN_PROJ_WIDTH), D ** -0.5),
        "conv_w": nrm(ks[11], (DEPTH, CONV_WIDTH, 3 * DELTA_WIDTH), CONV_WIDTH ** -0.5),
        "q_norm": gain(ks[12], (DEPTH, HEAD_DIM)),
        "k_norm": gain(ks[13], (DEPTH, HEAD_DIM)),
        "a_log": jnp.log(jax.random.uniform(ks[15], (DEPTH, N_DELTA_HEADS), f32, 1.0, 16.0)),
        "dt_bias": dt + jnp.log(-jnp.expm1(-dt)),
        "delta_out_norm": gain(ks[16], (DEPTH, HEAD_DIM)),
        "w_out": nrm(ks[17], (DEPTH, MIX_WIDTH, D), MIX_WIDTH ** -0.5),
        "ffn2_norm": gain(ks[18], (DEPTH, D)),
        "ffn2_w_gate": nrm(ks[19], (DEPTH, D, D_FF), D ** -0.5),
        "ffn2_w_up": nrm(ks[20], (DEPTH, D, D_FF), D ** -0.5),
        "ffn2_w_down": nrm(ks[21], (DEPTH, D_FF, D), D_FF ** -0.5),
    }


def _fwd_reference(x, c, positions, w_ada, b_ada, ffn1_norm, ffn1_w_gate, ffn1_w_up, ffn1_w_down,
              mix_norm, w_in, conv_w, q_norm, k_norm, a_log, dt_bias, delta_out_norm, w_out,
              ffn2_norm, ffn2_w_gate, ffn2_w_up, ffn2_w_down):
    c_act = jax.nn.silu(c)
    for l in range(DEPTH):
        mod = c_act @ w_ada[l] + b_ada[l]
        (sh1, sc1, gt1, sh2, sc2, gt2, sh3, sc3, gt3) = jnp.split(mod, N_MOD, axis=-1)
        h = modulate(x, ffn1_norm[l], sh1, sc1)
        x = x + 0.5 * gt1[:, None, :] * swiglu(h, ffn1_w_gate[l], ffn1_w_up[l], ffn1_w_down[l])
        h = modulate(x, mix_norm[l], sh2, sc2)
        x = x + gt2[:, None, :] * hybrid_mixer(h, positions, w_in[l], conv_w[l], q_norm[l], k_norm[l],
                                                a_log[l], dt_bias[l], delta_out_norm[l], w_out[l])
        h = modulate(x, ffn2_norm[l], sh3, sc3)
        x = x + 0.5 * gt3[:, None, :] * swiglu(h, ffn2_w_gate[l], ffn2_w_up[l], ffn2_w_down[l])
    return x


import jax as _jax
import jax.numpy as _jnp

TWIN_FORMAT = 'train_step'
FWD_PARAMS = ['x', 'c', 'positions', 'w_ada', 'b_ada', 'ffn1_norm', 'ffn1_w_gate', 'ffn1_w_up', 'ffn1_w_down', 'mix_norm', 'w_in', 'conv_w', 'q_norm', 'k_norm', 'a_log', 'dt_bias', 'delta_out_norm', 'w_out', 'ffn2_norm', 'ffn2_w_gate', 'ffn2_w_up', 'ffn2_w_down']
TWIN_WEIGHTS = ['w_ada', 'b_ada', 'ffn1_norm', 'ffn1_w_gate', 'ffn1_w_up', 'ffn1_w_down', 'mix_norm', 'w_in', 'conv_w', 'q_norm', 'k_norm', 'a_log', 'dt_bias', 'delta_out_norm', 'w_out', 'ffn2_norm', 'ffn2_w_gate', 'ffn2_w_up', 'ffn2_w_down']
TWIN_DIFF_INPUT = 'x'
TWIN_INPUTS = ['x', 'c', 'positions', 'w_ada', 'b_ada', 'ffn1_norm', 'ffn1_w_gate', 'ffn1_w_up', 'ffn1_w_down', 'mix_norm', 'w_in', 'conv_w', 'q_norm', 'k_norm', 'a_log', 'dt_bias', 'delta_out_norm', 'w_out', 'ffn2_norm', 'ffn2_w_gate', 'ffn2_w_up', 'ffn2_w_down', 'loss_target', 'm_w_ada', 'm_b_ada', 'm_ffn1_norm', 'm_ffn1_w_gate', 'm_ffn1_w_up', 'm_ffn1_w_down', 'm_mix_norm', 'm_w_in', 'm_conv_w', 'm_q_norm', 'm_k_norm', 'm_a_log', 'm_dt_bias', 'm_delta_out_norm', 'm_w_out', 'm_ffn2_norm', 'm_ffn2_w_gate', 'm_ffn2_w_up', 'm_ffn2_w_down', 'v_w_ada', 'v_b_ada', 'v_ffn1_norm', 'v_ffn1_w_gate', 'v_ffn1_w_up', 'v_ffn1_w_down', 'v_mix_norm', 'v_w_in', 'v_conv_w', 'v_q_norm', 'v_k_norm', 'v_a_log', 'v_dt_bias', 'v_delta_out_norm', 'v_w_out', 'v_ffn2_norm', 'v_ffn2_w_gate', 'v_ffn2_w_up', 'v_ffn2_w_down']
TWIN_OUTPUTS = ['loss', 'grad_x', 'grad_w_ada', 'grad_b_ada', 'grad_ffn1_norm', 'grad_ffn1_w_gate', 'grad_ffn1_w_up', 'grad_ffn1_w_down', 'grad_mix_norm', 'grad_w_in', 'grad_conv_w', 'grad_q_norm', 'grad_k_norm', 'grad_a_log', 'grad_dt_bias', 'grad_delta_out_norm', 'grad_w_out', 'grad_ffn2_norm', 'grad_ffn2_w_gate', 'grad_ffn2_w_up', 'grad_ffn2_w_down', 'delta_w_ada', 'delta_b_ada', 'delta_ffn1_norm', 'delta_ffn1_w_gate', 'delta_ffn1_w_up', 'delta_ffn1_w_down', 'delta_mix_norm', 'delta_w_in', 'delta_conv_w', 'delta_q_norm', 'delta_k_norm', 'delta_a_log', 'delta_dt_bias', 'delta_delta_out_norm', 'delta_w_out', 'delta_ffn2_norm', 'delta_ffn2_w_gate', 'delta_ffn2_w_up', 'delta_ffn2_w_down', 'new_m_w_ada', 'new_m_b_ada', 'new_m_ffn1_norm', 'new_m_ffn1_w_gate', 'new_m_ffn1_w_up', 'new_m_ffn1_w_down', 'new_m_mix_norm', 'new_m_w_in', 'new_m_conv_w', 'new_m_q_norm', 'new_m_k_norm', 'new_m_a_log', 'new_m_dt_bias', 'new_m_delta_out_norm', 'new_m_w_out', 'new_m_ffn2_norm', 'new_m_ffn2_w_gate', 'new_m_ffn2_w_up', 'new_m_ffn2_w_down', 'new_v_w_ada', 'new_v_b_ada', 'new_v_ffn1_norm', 'new_v_ffn1_w_gate', 'new_v_ffn1_w_up', 'new_v_ffn1_w_down', 'new_v_mix_norm', 'new_v_w_in', 'new_v_conv_w', 'new_v_q_norm', 'new_v_k_norm', 'new_v_a_log', 'new_v_dt_bias', 'new_v_delta_out_norm', 'new_v_w_out', 'new_v_ffn2_norm', 'new_v_ffn2_w_gate', 'new_v_ffn2_w_up', 'new_v_ffn2_w_down']
TWIN_LEAF_KINDS = {'loss': 'loss', 'grad_x': 'grad_x', 'grad_w_ada': 'grad_w', 'grad_b_ada': 'grad_w', 'grad_ffn1_norm': 'grad_w', 'grad_ffn1_w_gate': 'grad_w', 'grad_ffn1_w_up': 'grad_w', 'grad_ffn1_w_down': 'grad_w', 'grad_mix_norm': 'grad_w', 'grad_w_in': 'grad_w', 'grad_conv_w': 'grad_w', 'grad_q_norm': 'grad_w', 'grad_k_norm': 'grad_w', 'grad_a_log': 'grad_w', 'grad_dt_bias': 'grad_w', 'grad_delta_out_norm': 'grad_w', 'grad_w_out': 'grad_w', 'grad_ffn2_norm': 'grad_w', 'grad_ffn2_w_gate': 'grad_w', 'grad_ffn2_w_up': 'grad_w', 'grad_ffn2_w_down': 'grad_w', 'delta_w_ada': 'delta_w', 'delta_b_ada': 'delta_w', 'delta_ffn1_norm': 'delta_w', 'delta_ffn1_w_gate': 'delta_w', 'delta_ffn1_w_up': 'delta_w', 'delta_ffn1_w_down': 'delta_w', 'delta_mix_norm': 'delta_w', 'delta_w_in': 'delta_w', 'delta_conv_w': 'delta_w', 'delta_q_norm': 'delta_w', 'delta_k_norm': 'delta_w', 'delta_a_log': 'delta_w', 'delta_dt_bias': 'delta_w', 'delta_delta_out_norm': 'delta_w', 'delta_w_out': 'delta_w', 'delta_ffn2_norm': 'delta_w', 'delta_ffn2_w_gate': 'delta_w', 'delta_ffn2_w_up': 'delta_w', 'delta_ffn2_w_down': 'delta_w', 'new_m_w_ada': 'new_m', 'new_m_b_ada': 'new_m', 'new_m_ffn1_norm': 'new_m', 'new_m_ffn1_w_gate': 'new_m', 'new_m_ffn1_w_up': 'new_m', 'new_m_ffn1_w_down': 'new_m', 'new_m_mix_norm': 'new_m', 'new_m_w_in': 'new_m', 'new_m_conv_w': 'new_m', 'new_m_q_norm': 'new_m', 'new_m_k_norm': 'new_m', 'new_m_a_log': 'new_m', 'new_m_dt_bias': 'new_m', 'new_m_delta_out_norm': 'new_m', 'new_m_w_out': 'new_m', 'new_m_ffn2_norm': 'new_m', 'new_m_ffn2_w_gate': 'new_m', 'new_m_ffn2_w_up': 'new_m', 'new_m_ffn2_w_down': 'new_m', 'new_v_w_ada': 'new_v', 'new_v_b_ada': 'new_v', 'new_v_ffn1_norm': 'new_v', 'new_v_ffn1_w_gate': 'new_v', 'new_v_ffn1_w_up': 'new_v', 'new_v_ffn1_w_down': 'new_v', 'new_v_mix_norm': 'new_v', 'new_v_w_in': 'new_v', 'new_v_conv_w': 'new_v', 'new_v_q_norm': 'new_v', 'new_v_k_norm': 'new_v', 'new_v_a_log': 'new_v', 'new_v_dt_bias': 'new_v', 'new_v_delta_out_norm': 'new_v', 'new_v_w_out': 'new_v', 'new_v_ffn2_norm': 'new_v', 'new_v_ffn2_w_gate': 'new_v', 'new_v_ffn2_w_up': 'new_v', 'new_v_ffn2_w_down': 'new_v'}


def _forward(args):
    return _fwd_reference(*[args[k] for k in FWD_PARAMS])


def _output_shape():
    def fwd():
        inp = _fwd_setup_inputs(0)
        return _fwd_reference(*[inp[k] for k in FWD_PARAMS])
    out = _jax.eval_shape(fwd)
    return out.shape, out.dtype

N_MICROBATCH = 1
ADAM_LR = 0.001
ADAM_B1 = 0.9
ADAM_B2 = 0.999
ADAM_EPS = 1e-08
ADAM_WD = 0.01
ADAM_STEP = 10
PER_EXAMPLE_BATCH_AXIS = {'x': 0, 'c': 0, 'positions': 0, 'loss_target': 0}
SHARED_INPUTS = []
_WEIGHT_DTYPES = {'w_ada': _jnp.float32, 'b_ada': _jnp.float32, 'ffn1_norm': _jnp.float32, 'ffn1_w_gate': _jnp.float32, 'ffn1_w_up': _jnp.float32, 'ffn1_w_down': _jnp.float32, 'mix_norm': _jnp.float32, 'w_in': _jnp.float32, 'conv_w': _jnp.float32, 'q_norm': _jnp.float32, 'k_norm': _jnp.float32, 'a_log': _jnp.float32, 'dt_bias': _jnp.float32, 'delta_out_norm': _jnp.float32, 'w_out': _jnp.float32, 'ffn2_norm': _jnp.float32, 'ffn2_w_gate': _jnp.float32, 'ffn2_w_up': _jnp.float32, 'ffn2_w_down': _jnp.float32}
MOMENT_SCALE = {'w_ada': 1.444118e-01, 'b_ada': 3.707430e-01, 'ffn1_norm': 3.885041e-01, 'ffn1_w_gate': 1.083677e-02, 'ffn1_w_up': 9.995176e-03, 'ffn1_w_down': 1.628383e-02, 'mix_norm': 3.417437e-01, 'w_in': 5.251419e-02, 'conv_w': 6.219880e-02, 'q_norm': 1.057888e-01, 'k_norm': 1.057544e-01, 'a_log': 1.465389e+00, 'dt_bias': 1.391335e+00, 'delta_out_norm': 5.149612e+00, 'w_out': 5.506316e-02, 'ffn2_norm': 3.849648e-01, 'ffn2_w_gate': 1.043929e-02, 'ffn2_w_up': 9.513784e-03, 'ffn2_w_down': 1.539967e-02}


def _to_microbatches(a, axis):
    t = _jnp.moveaxis(a, axis, 0)
    t = t.reshape((N_MICROBATCH, t.shape[0] // N_MICROBATCH) + t.shape[1:])
    return _jnp.moveaxis(t, 1, axis + 1)


def setup_inputs(seed: int = 0) -> dict:
    inp = _fwd_setup_inputs(seed)
    key = _jax.random.fold_in(_jax.random.key(seed), 7919)
    shape, _ = _output_shape()
    out = dict(inp)
    out["loss_target"] = _jax.random.normal(_jax.random.fold_in(key, 0), shape, _jnp.float32)
    for i, name in enumerate(TWIN_WEIGHTS):
        w = inp[name].astype(_jnp.float32)
        if MOMENT_SCALE is None:
            s = _jnp.sqrt(_jnp.mean(_jnp.square(w)) + 1e-30)
        else:
            s = MOMENT_SCALE[name]
        km, kv = _jax.random.split(_jax.random.fold_in(key, i + 1))
        out[name] = w
        out["m_" + name] = s * _jax.random.normal(km, w.shape, _jnp.float32)
        out["v_" + name] = (s * s) * _jax.random.uniform(kv, w.shape, _jnp.float32, 0.5, 1.5)
    if N_MICROBATCH > 1:
        for name, axis in PER_EXAMPLE_BATCH_AXIS.items():
            out[name] = _to_microbatches(out[name], axis)
    return {'x': out['x'], 'c': out['c'], 'positions': out['positions'], 'w_ada': out['w_ada'], 'b_ada': out['b_ada'], 'ffn1_norm': out['ffn1_norm'], 'ffn1_w_gate': out['ffn1_w_gate'], 'ffn1_w_up': out['ffn1_w_up'], 'ffn1_w_down': out['ffn1_w_down'], 'mix_norm': out['mix_norm'], 'w_in': out['w_in'], 'conv_w': out['conv_w'], 'q_norm': out['q_norm'], 'k_norm': out['k_norm'], 'a_log': out['a_log'], 'dt_bias': out['dt_bias'], 'delta_out_norm': out['delta_out_norm'], 'w_out': out['w_out'], 'ffn2_norm': out['ffn2_norm'], 'ffn2_w_gate': out['ffn2_w_gate'], 'ffn2_w_up': out['ffn2_w_up'], 'ffn2_w_down': out['ffn2_w_down'], 'loss_target': out['loss_target'], 'm_w_ada': out['m_w_ada'], 'm_b_ada': out['m_b_ada'], 'm_ffn1_norm': out['m_ffn1_norm'], 'm_ffn1_w_gate': out['m_ffn1_w_gate'], 'm_ffn1_w_up': out['m_ffn1_w_up'], 'm_ffn1_w_down': out['m_ffn1_w_down'], 'm_mix_norm': out['m_mix_norm'], 'm_w_in': out['m_w_in'], 'm_conv_w': out['m_conv_w'], 'm_q_norm': out['m_q_norm'], 'm_k_norm': out['m_k_norm'], 'm_a_log': out['m_a_log'], 'm_dt_bias': out['m_dt_bias'], 'm_delta_out_norm': out['m_delta_out_norm'], 'm_w_out': out['m_w_out'], 'm_ffn2_norm': out['m_ffn2_norm'], 'm_ffn2_w_gate': out['m_ffn2_w_gate'], 'm_ffn2_w_up': out['m_ffn2_w_up'], 'm_ffn2_w_down': out['m_ffn2_w_down'], 'v_w_ada': out['v_w_ada'], 'v_b_ada': out['v_b_ada'], 'v_ffn1_norm': out['v_ffn1_norm'], 'v_ffn1_w_gate': out['v_ffn1_w_gate'], 'v_ffn1_w_up': out['v_ffn1_w_up'], 'v_ffn1_w_down': out['v_ffn1_w_down'], 'v_mix_norm': out['v_mix_norm'], 'v_w_in': out['v_w_in'], 'v_conv_w': out['v_conv_w'], 'v_q_norm': out['v_q_norm'], 'v_k_norm': out['v_k_norm'], 'v_a_log': out['v_a_log'], 'v_dt_bias': out['v_dt_bias'], 'v_delta_out_norm': out['v_delta_out_norm'], 'v_w_out': out['v_w_out'], 'v_ffn2_norm': out['v_ffn2_norm'], 'v_ffn2_w_gate': out['v_ffn2_w_gate'], 'v_ffn2_w_up': out['v_ffn2_w_up'], 'v_ffn2_w_down': out['v_ffn2_w_down']}


def _loss(weights, diff, rest, loss_target):
    with _jax.named_scope("forward"):
        args = {**rest, TWIN_DIFF_INPUT: diff, **{k: w.astype(_WEIGHT_DTYPES[k]) for k, w in weights.items()}}
        y = _forward(args)
    with _jax.named_scope("loss_head"):
        err = _jnp.square(y.astype(_jnp.float32) - loss_target)
        return 0.5 * _jnp.sum(_jnp.mean(err, axis=-1)) if err.ndim else 0.5 * err


def _adamw(w, g, m, v):
    m = ADAM_B1 * m + (1.0 - ADAM_B1) * g
    v = ADAM_B2 * v + (1.0 - ADAM_B2) * _jnp.square(g)
    m_hat = m / (1.0 - ADAM_B1 ** ADAM_STEP)
    v_hat = v / (1.0 - ADAM_B2 ** ADAM_STEP)
    delta = -ADAM_LR * (m_hat / (_jnp.sqrt(v_hat) + ADAM_EPS) + ADAM_WD * w)
    return delta, m, v


def reference(x, c, positions, w_ada, b_ada, ffn1_norm, ffn1_w_gate, ffn1_w_up, ffn1_w_down, mix_norm, w_in, conv_w, q_norm, k_norm, a_log, dt_bias, delta_out_norm, w_out, ffn2_norm, ffn2_w_gate, ffn2_w_up, ffn2_w_down, loss_target, m_w_ada, m_b_ada, m_ffn1_norm, m_ffn1_w_gate, m_ffn1_w_up, m_ffn1_w_down, m_mix_norm, m_w_in, m_conv_w, m_q_norm, m_k_norm, m_a_log, m_dt_bias, m_delta_out_norm, m_w_out, m_ffn2_norm, m_ffn2_w_gate, m_ffn2_w_up, m_ffn2_w_down, v_w_ada, v_b_ada, v_ffn1_norm, v_ffn1_w_gate, v_ffn1_w_up, v_ffn1_w_down, v_mix_norm, v_w_in, v_conv_w, v_q_norm, v_k_norm, v_a_log, v_dt_bias, v_delta_out_norm, v_w_out, v_ffn2_norm, v_ffn2_w_gate, v_ffn2_w_up, v_ffn2_w_down):
    given = dict(x=x, c=c, positions=positions, w_ada=w_ada, b_ada=b_ada, ffn1_norm=ffn1_norm, ffn1_w_gate=ffn1_w_gate, ffn1_w_up=ffn1_w_up, ffn1_w_down=ffn1_w_down, mix_norm=mix_norm, w_in=w_in, conv_w=conv_w, q_norm=q_norm, k_norm=k_norm, a_log=a_log, dt_bias=dt_bias, delta_out_norm=delta_out_norm, w_out=w_out, ffn2_norm=ffn2_norm, ffn2_w_gate=ffn2_w_gate, ffn2_w_up=ffn2_w_up, ffn2_w_down=ffn2_w_down, loss_target=loss_target, m_w_ada=m_w_ada, m_b_ada=m_b_ada, m_ffn1_norm=m_ffn1_norm, m_ffn1_w_gate=m_ffn1_w_gate, m_ffn1_w_up=m_ffn1_w_up, m_ffn1_w_down=m_ffn1_w_down, m_mix_norm=m_mix_norm, m_w_in=m_w_in, m_conv_w=m_conv_w, m_q_norm=m_q_norm, m_k_norm=m_k_norm, m_a_log=m_a_log, m_dt_bias=m_dt_bias, m_delta_out_norm=m_delta_out_norm, m_w_out=m_w_out, m_ffn2_norm=m_ffn2_norm, m_ffn2_w_gate=m_ffn2_w_gate, m_ffn2_w_up=m_ffn2_w_up, m_ffn2_w_down=m_ffn2_w_down, v_w_ada=v_w_ada, v_b_ada=v_b_ada, v_ffn1_norm=v_ffn1_norm, v_ffn1_w_gate=v_ffn1_w_gate, v_ffn1_w_up=v_ffn1_w_up, v_ffn1_w_down=v_ffn1_w_down, v_mix_norm=v_mix_norm, v_w_in=v_w_in, v_conv_w=v_conv_w, v_q_norm=v_q_norm, v_k_norm=v_k_norm, v_a_log=v_a_log, v_dt_bias=v_dt_bias, v_delta_out_norm=v_delta_out_norm, v_w_out=v_w_out, v_ffn2_norm=v_ffn2_norm, v_ffn2_w_gate=v_ffn2_w_gate, v_ffn2_w_up=v_ffn2_w_up, v_ffn2_w_down=v_ffn2_w_down)
    weights = {n: given[n] for n in TWIN_WEIGHTS}
    shared = {n: given[n] for n in SHARED_INPUTS}
    per_example = {n: given[n] for n in ['x', 'c', 'positions']}
    grad_fn = _jax.value_and_grad(_loss, argnums=(0, 1))

    def one_microbatch(ex, loss_target):
        ex = dict(ex)
        diff = ex.pop(TWIN_DIFF_INPUT)
        return grad_fn(weights, diff, {**shared, **ex}, loss_target)

    if N_MICROBATCH == 1:
        loss, (grad_w, grad_x) = one_microbatch(per_example, given["loss_target"])
    else:
        def body(carry, xs):
            loss_sum, grad_sum = carry
            l_k, (gw_k, gx_k) = one_microbatch(xs[0], xs[1])
            with _jax.named_scope("update"):
                return (loss_sum + l_k, _jax.tree.map(_jnp.add, grad_sum, gw_k)), gx_k

        init = (_jnp.zeros((), _jnp.float32), _jax.tree.map(_jnp.zeros_like, weights))
        (loss, grad_w), grad_x = _jax.lax.scan(body, init, (per_example, given["loss_target"]))
    with _jax.named_scope("update"):
        delta_w, new_m, new_v = {}, {}, {}
        for n in TWIN_WEIGHTS:
            delta_w[n], new_m[n], new_v[n] = _adamw(weights[n], grad_w[n], given["m_" + n], given["v_" + n])
    return (loss, grad_x, *[grad_w[n] for n in TWIN_WEIGHTS], *[delta_w[n] for n in TWIN_WEIGHTS],
            *[new_m[n] for n in TWIN_WEIGHTS], *[new_v[n] for n in TWIN_WEIGHTS])
```

```python
import functools
import math

import jax
import jax.numpy as jnp
from jax import lax
from jax.experimental import pallas as pl
from jax.experimental.pallas import tpu as pltpu

F32 = jnp.float32
BF16 = jnp.bfloat16
MESH = pl.DeviceIdType.MESH

HEAD_DIM = 128
N_ATTN_HEADS = 8
N_DELTA_HEADS = 8
DILATED_PATTERNS = ((128, 1), (512, 4), (2048, 16))
MAX_WINDOW = 2048
ROPE_THETA = 500000.0
ROPE_DIM = HEAD_DIM // 4
CONV_WIDTH = 4
CHUNK = 64
NORM_EPS = 1e-6
N_MOD = 9
ADAM_LR = 0.001
ADAM_B1 = 0.9
ADAM_B2 = 0.999
ADAM_EPS = 1e-08
ADAM_WD = 0.01
ADAM_STEP = 10

LANES = 128
VMEM_LIMIT = 56 * 1024 * 1024
ATTN_TILE = 512
HIGHEST = lax.Precision.HIGHEST


def _cparams(sem=None):
    return pltpu.CompilerParams(dimension_semantics=sem, vmem_limit_bytes=VMEM_LIMIT)


def _pick(dim, pref):
    if dim <= pref:
        return dim
    t = (pref // LANES) * LANES
    while t >= LANES:
        if dim % t == 0:
            return t
        t -= LANES
    return dim


def _sigmoid(x):
    return 1.0 / (1.0 + jnp.exp(-x))


def _silu(x):
    return x * _sigmoid(x)


def _softplus(x):
    return jnp.maximum(x, 0.0) + jnp.log(1.0 + jnp.exp(-jnp.abs(x)))


def _rms(x, gain):
    return x * lax.rsqrt(jnp.mean(x * x, axis=-1, keepdims=True) + NORM_EPS) * gain


def _l2(x):
    return x * lax.rsqrt(jnp.sum(x * x, axis=-1, keepdims=True) + NORM_EPS)


def _modulate(x, gain, shift, scale):
    return _rms(x, gain) * (1.0 + scale) + shift


def _dot(a, b):
    return lax.dot_general(a, b, (((1,), (0,)), ((), ())), precision=HIGHEST, preferred_element_type=F32)


def _dot_nt(a, b):
    return lax.dot_general(a, b, (((1,), (1,)), ((), ())), precision=HIGHEST, preferred_element_type=F32)


def _dot_tn(a, b):
    return lax.dot_general(a, b, (((0,), (0,)), ((), ())), precision=HIGHEST, preferred_element_type=F32)


def _bdot(a, b, dims):
    return lax.dot_general(a.astype(BF16), b.astype(BF16), (dims, ((), ())), preferred_element_type=F32)


_MM_DIMS = {"nn": ((1,), (0,)), "nt": ((1,), (1,)), "tn": ((0,), (0,))}


def _matmul(a, b, mode, out_dtype, name, tm=1024, tn=1024, tk=1024):
    if mode == "nn":
        (M, K), (_, N) = a.shape, b.shape
    elif mode == "nt":
        (M, K), (N, _) = a.shape, b.shape
    else:
        (K, M), (_, N) = a.shape, b.shape
    tm, tn, tk = _pick(M, tm), _pick(N, tn), _pick(K, tk)
    nk = K // tk
    dims = _MM_DIMS[mode]

    def body(a_ref, b_ref, o_ref, acc_ref):
        k = pl.program_id(2)
        p = _bdot(a_ref[...], b_ref[...], dims)

        @pl.when(k == 0)
        def _():
            acc_ref[...] = p

        @pl.when(k > 0)
        def _():
            acc_ref[...] += p

        @pl.when(k == nk - 1)
        def _():
            o_ref[...] = acc_ref[...].astype(out_dtype)

    a_spec = pl.BlockSpec((tk, tm), lambda i, j, k: (k, i)) if mode == "tn" else pl.BlockSpec((tm, tk), lambda i, j, k: (i, k))
    b_spec = pl.BlockSpec((tn, tk), lambda i, j, k: (j, k)) if mode == "nt" else pl.BlockSpec((tk, tn), lambda i, j, k: (k, j))
    return pl.pallas_call(
        body, name=name, grid=(M // tm, N // tn, nk),
        in_specs=[a_spec, b_spec], out_specs=pl.BlockSpec((tm, tn), lambda i, j, k: (i, j)),
        out_shape=jax.ShapeDtypeStruct((M, N), out_dtype),
        scratch_shapes=[pltpu.VMEM((tm, tn), F32)],
        compiler_params=_cparams(("parallel", "parallel", "arbitrary")),
    )(a, b)


def _row_spec(tr, d):
    return pl.BlockSpec((tr, d), lambda i: (i, 0))


def _vec_spec(d):
    return pl.BlockSpec((1, d), lambda i: (0, 0))


def _pre_fwd(x, gain, shift, scale, name):
    S, D = x.shape
    tr = _pick(S, 256)

    def body(x_ref, g_ref, sh_ref, sc_ref, h_ref):
        h_ref[...] = _modulate(x_ref[...], g_ref[...], sh_ref[...], sc_ref[...]).astype(BF16)

    return pl.pallas_call(
        body, name=name, grid=(S // tr,),
        in_specs=[_row_spec(tr, D), _vec_spec(D), _vec_spec(D), _vec_spec(D)],
        out_specs=_row_spec(tr, D), out_shape=jax.ShapeDtypeStruct((S, D), BF16),
        compiler_params=_cparams(("parallel",)),
    )(x, gain, shift, scale)


def _pre_bwd(x, gain, shift, scale, dh, dx_in, name):
    S, D = x.shape
    tr = _pick(S, 256)

    def body(x_ref, g_ref, sh_ref, sc_ref, dh_ref, dxin_ref, dx_ref, dg_ref, dsh_ref, dsc_ref):
        _, vjp = jax.vjp(_modulate, x_ref[...], g_ref[...], sh_ref[...], sc_ref[...])
        dx, dg, dsh, dsc = vjp(dh_ref[...])
        dx_ref[...] = dxin_ref[...] + dx

        @pl.when(pl.program_id(0) == 0)
        def _():
            dg_ref[...] = jnp.zeros_like(dg_ref)
            dsh_ref[...] = jnp.zeros_like(dsh_ref)
            dsc_ref[...] = jnp.zeros_like(dsc_ref)

        dg_ref[...] += dg
        dsh_ref[...] += dsh
        dsc_ref[...] += dsc

    vec = jax.ShapeDtypeStruct((1, D), F32)
    return pl.pallas_call(
        body, name=name, grid=(S // tr,),
        in_specs=[_row_spec(tr, D), _vec_spec(D), _vec_spec(D), _vec_spec(D), _row_spec(tr, D), _row_spec(tr, D)],
        out_specs=[_row_spec(tr, D), _vec_spec(D), _vec_spec(D), _vec_spec(D)],
        out_shape=[jax.ShapeDtypeStruct((S, D), F32), vec, vec, vec],
        compiler_params=_cparams(("arbitrary",)),
    )(x, gain, shift, scale, dh, dx_in)


def _residual_fwd(x, gate, f, coef, name):
    S, D = x.shape
    tr = _pick(S, 256)

    def body(x_ref, g_ref, f_ref, o_ref):
        o_ref[...] = x_ref[...] + coef * g_ref[...] * f_ref[...]

    return pl.pallas_call(
        body, name=name, grid=(S // tr,),
        in_specs=[_row_spec(tr, D), _vec_spec(D), _row_spec(tr, D)],
        out_specs=_row_spec(tr, D), out_shape=jax.ShapeDtypeStruct((S, D), F32),
        compiler_params=_cparams(("parallel",)),
    )(x, gate, f)


def _residual_bwd(gate, f, dxn, coef, name):
    S, D = f.shape
    tr = _pick(S, 256)

    def body(g_ref, f_ref, d_ref, df_ref, dg_ref):
        d = d_ref[...]
        df_ref[...] = (coef * g_ref[...] * d).astype(BF16)

        @pl.when(pl.program_id(0) == 0)
        def _():
            dg_ref[...] = jnp.zeros_like(dg_ref)

        dg_ref[...] += jnp.sum(coef * f_ref[...] * d, axis=0, keepdims=True)

    return pl.pallas_call(
        body, name=name, grid=(S // tr,),
        in_specs=[_vec_spec(D), _row_spec(tr, D), _row_spec(tr, D)],
        out_specs=[_row_spec(tr, D), _vec_spec(D)],
        out_shape=[jax.ShapeDtypeStruct((S, D), BF16), jax.ShapeDtypeStruct((1, D), F32)],
        compiler_params=_cparams(("arbitrary",)),
    )(gate, f, dxn)


def _swiglu_fn(a, b):
    return _silu(a) * b


def _swiglu_fwd(ab, fs, name):
    S, F2 = ab.shape
    tr = _pick(S, 256)

    def body(ab_ref, s_ref):
        s_ref[...] = _swiglu_fn(ab_ref[:, :fs], ab_ref[:, fs:]).astype(BF16)

    return pl.pallas_call(
        body, name=name, grid=(S // tr, F2 // (2 * fs)),
        in_specs=[pl.BlockSpec((tr, 2 * fs), lambda i, j: (i, j))],
        out_specs=pl.BlockSpec((tr, fs), lambda i, j: (i, j)),
        out_shape=jax.ShapeDtypeStruct((S, F2 // 2), BF16),
        compiler_params=_cparams(("parallel", "parallel")),
    )(ab)


def _swiglu_bwd(ab, ds, fs, name):
    S, F2 = ab.shape
    tr = _pick(S, 256)

    def body(ab_ref, ds_ref, dab_ref):
        _, vjp = jax.vjp(_swiglu_fn, ab_ref[:, :fs], ab_ref[:, fs:])
        da, db = vjp(ds_ref[...])
        dab_ref[:, :fs] = da.astype(BF16)
        dab_ref[:, fs:] = db.astype(BF16)

    return pl.pallas_call(
        body, name=name, grid=(S // tr, F2 // (2 * fs)),
        in_specs=[pl.BlockSpec((tr, 2 * fs), lambda i, j: (i, j)), pl.BlockSpec((tr, fs), lambda i, j: (i, j))],
        out_specs=pl.BlockSpec((tr, 2 * fs), lambda i, j: (i, j)),
        out_shape=jax.ShapeDtypeStruct((S, F2), BF16),
        compiler_params=_cparams(("parallel", "parallel")),
    )(ab, ds)


def _loss_head(y, target, name):
    S, D = y.shape
    tr = _pick(S, 256)

    def body(y_ref, t_ref, l_ref, dy_ref):
        e = y_ref[...] - t_ref[...]
        dy_ref[...] = e * (1.0 / D)

        @pl.when(pl.program_id(0) == 0)
        def _():
            l_ref[...] = jnp.zeros_like(l_ref)

        l_ref[...] += jnp.sum(jnp.sum(e * e, axis=-1, keepdims=True), axis=0, keepdims=True) * (0.5 / D)

    return pl.pallas_call(
        body, name=name, grid=(S // tr,),
        in_specs=[_row_spec(tr, D), _row_spec(tr, D)],
        out_specs=[pl.BlockSpec((1, 1), lambda i: (0, 0)), _row_spec(tr, D)],
        out_shape=[jax.ShapeDtypeStruct((1, 1), F32), jax.ShapeDtypeStruct((S, D), F32)],
        compiler_params=_cparams(("arbitrary",)),
    )(y, target)


def _rope(y, cc, s1, s2):
    return y * cc + pltpu.roll(y, LANES - ROPE_DIM // 2, 1) * s1 + pltpu.roll(y, ROPE_DIM // 2, 1) * s2


def _rope_t(d, cc, s1, s2):
    return d * cc + pltpu.roll(d * s1, ROPE_DIM // 2, 1) + pltpu.roll(d * s2, LANES - ROPE_DIM // 2, 1)


def _head_spec(tr, col0):
    return pl.BlockSpec((tr, HEAD_DIM), lambda i, h: (i, col0 + h))


def _tab_spec(tr):
    return pl.BlockSpec((tr, HEAD_DIM), lambda i, h: (i, 0))


def _gain_spec():
    return pl.BlockSpec((1, HEAD_DIM), lambda i, h: (0, 0))


def _attn_prep_fwd(proj, q_gain, k_gain, cc, s1, s2, name):
    S = proj.shape[0]
    H = N_ATTN_HEADS
    tr = _pick(S, 512)

    def body(q_ref, k_ref, qg_ref, kg_ref, cc_ref, s1_ref, s2_ref, qo_ref, ko_ref):
        cc, s1, s2 = cc_ref[...], s1_ref[...], s2_ref[...]
        qo_ref[...] = _rope(_rms(q_ref[...], qg_ref[...]), cc, s1, s2).astype(BF16)
        ko_ref[...] = _rope(_rms(k_ref[...], kg_ref[...]), cc, s1, s2).astype(BF16)

    out = jax.ShapeDtypeStruct((S, H * HEAD_DIM), BF16)
    return pl.pallas_call(
        body, name=name, grid=(S // tr, H),
        in_specs=[_head_spec(tr, 0), _head_spec(tr, H), _gain_spec(), _gain_spec(), _tab_spec(tr), _tab_spec(tr), _tab_spec(tr)],
        out_specs=[_head_spec(tr, 0), _head_spec(tr, 0)], out_shape=[out, out],
        compiler_params=_cparams(("parallel", "parallel")),
    )(proj, proj, q_gain, k_gain, cc, s1, s2)


def _attn_prep_bwd(proj, q_gain, k_gain, cc, s1, s2, dq, dk, name):
    S = proj.shape[0]
    H = N_ATTN_HEADS
    tr = _pick(S, 512)

    def body(q_ref, k_ref, qg_ref, kg_ref, cc_ref, s1_ref, s2_ref, dq_ref, dk_ref, dpq_ref, dpk_ref, dqg_ref, dkg_ref):
        cc, s1, s2 = cc_ref[...], s1_ref[...], s2_ref[...]

        @pl.when((pl.program_id(0) == 0) & (pl.program_id(1) == 0))
        def _():
            dqg_ref[...] = jnp.zeros_like(dqg_ref)
            dkg_ref[...] = jnp.zeros_like(dkg_ref)

        _, vjp_q = jax.vjp(_rms, q_ref[...], qg_ref[...])
        dxq, dgq = vjp_q(_rope_t(dq_ref[...], cc, s1, s2))
        _, vjp_k = jax.vjp(_rms, k_ref[...], kg_ref[...])
        dxk, dgk = vjp_k(_rope_t(dk_ref[...], cc, s1, s2))
        dpq_ref[...] = dxq.astype(BF16)
        dpk_ref[...] = dxk.astype(BF16)
        dqg_ref[...] += dgq
        dkg_ref[...] += dgk

    out = jax.ShapeDtypeStruct((S, H * HEAD_DIM), BF16)
    gout = jax.ShapeDtypeStruct((1, HEAD_DIM), F32)
    return pl.pallas_call(
        body, name=name, grid=(S // tr, H),
        in_specs=[_head_spec(tr, 0), _head_spec(tr, H), _gain_spec(), _gain_spec(), _tab_spec(tr), _tab_spec(tr), _tab_spec(tr),
                  _head_spec(tr, 0), _head_spec(tr, 0)],
        out_specs=[_head_spec(tr, 0), _head_spec(tr, 0), _gain_spec(), _gain_spec()], out_shape=[out, out, gout, gout],
        compiler_params=_cparams(("arbitrary", "arbitrary")),
    )(proj, proj, q_gain, k_gain, cc, s1, s2, dq, dk)


def _multiplicity(j, t):
    ti = lax.broadcasted_iota(jnp.int32, (t, t), 0)
    si = lax.broadcasted_iota(jnp.int32, (t, t), 1)
    delta = j * t + ti - si
    cnt = jnp.zeros((t, t), F32)
    for window, dil in DILATED_PATTERNS:
        ok = (delta >= 0) & ((delta & (dil - 1)) == 0) & (delta <= window)
        cnt = cnt + ok.astype(F32)
    return cnt


_NEG = -1e30


def _attn_fwd(q, k, proj, v_col0, name):
    S = q.shape[0]
    H = N_ATTN_HEADS
    t = _pick(S, ATTN_TILE)
    nq = S // t
    nj = MAX_WINDOW // t + 1
    scale = HEAD_DIM ** -0.5

    def body(q_ref, k_ref, v_ref, o_ref, ob_ref, lse_ref, m_sc, l_sc, acc_sc):
        qb, j = pl.program_id(1), pl.program_id(2)

        @pl.when(j == 0)
        def _():
            m_sc[...] = jnp.full_like(m_sc, _NEG)
            l_sc[...] = jnp.zeros_like(l_sc)
            acc_sc[...] = jnp.zeros_like(acc_sc)

        @pl.when(qb - j >= 0)
        def _():
            cnt = _multiplicity(j, t)
            s = _bdot(q_ref[...], k_ref[...], ((1,), (1,))) * scale
            s = jnp.where(cnt > 0.0, s, _NEG)
            m_prev = m_sc[...]
            m_new = jnp.maximum(m_prev, jnp.max(s, axis=-1, keepdims=True))
            alpha = jnp.exp(m_prev - m_new)
            p = cnt * jnp.exp(s - m_new)
            l_sc[...] = alpha * l_sc[...] + jnp.sum(p, axis=-1, keepdims=True)
            acc_sc[...] = alpha * acc_sc[...] + _bdot(p, v_ref[...], ((1,), (0,)))
            m_sc[...] = m_new

        @pl.when(j == nj - 1)
        def _():
            o = acc_sc[...] / l_sc[...]
            o_ref[...] = o
            ob_ref[...] = o.astype(BF16)
            lse_ref[...] = jnp.broadcast_to(m_sc[...] + jnp.log(l_sc[...]), (t, HEAD_DIM))

    qspec = pl.BlockSpec((t, HEAD_DIM), lambda h, i, j: (i, h))
    kspec = pl.BlockSpec((t, HEAD_DIM), lambda h, i, j: (jnp.maximum(i - j, 0), h))
    vspec = pl.BlockSpec((t, HEAD_DIM), lambda h, i, j: (jnp.maximum(i - j, 0), v_col0 + h))
    return pl.pallas_call(
        body, name=name, grid=(H, nq, nj),
        in_specs=[qspec, kspec, vspec], out_specs=[qspec, qspec, qspec],
        out_shape=[jax.ShapeDtypeStruct((S, H * HEAD_DIM), F32), jax.ShapeDtypeStruct((S, H * HEAD_DIM), BF16),
                   jax.ShapeDtypeStruct((S, H * HEAD_DIM), F32)],
        scratch_shapes=[pltpu.VMEM((t, 1), F32), pltpu.VMEM((t, 1), F32), pltpu.VMEM((t, HEAD_DIM), F32)],
        compiler_params=_cparams(("parallel", "parallel", "arbitrary")),
    )(q, k, proj)


def _attn_probs(q, k, lse, j, t, scale):
    cnt = _multiplicity(j, t)
    s = _bdot(q, k, ((1,), (1,))) * scale
    s = jnp.where(cnt > 0.0, s, _NEG)
    return cnt * jnp.exp(s - lse)


def _attn_bwd_dq(q, k, proj, v_col0, o, lse, do, do_col0, name):
    S = q.shape[0]
    H = N_ATTN_HEADS
    t = _pick(S, ATTN_TILE)
    nq = S // t
    nj = MAX_WINDOW // t + 1
    scale = HEAD_DIM ** -0.5

    def body(q_ref, k_ref, v_ref, o_ref, lse_ref, do_ref, dq_ref, acc_sc):
        qb, j = pl.program_id(1), pl.program_id(2)

        @pl.when(j == 0)
        def _():
            acc_sc[...] = jnp.zeros_like(acc_sc)

        @pl.when(qb - j >= 0)
        def _():
            do = do_ref[...]
            dsum = jnp.sum(do * o_ref[...], axis=-1, keepdims=True)
            lse = jnp.max(lse_ref[...], axis=-1, keepdims=True)
            p = _attn_probs(q_ref[...], k_ref[...], lse, j, t, scale)
            dp = _bdot(do, v_ref[...], ((1,), (1,)))
            ds = p * (dp - dsum)
            acc_sc[...] += _bdot(ds, k_ref[...], ((1,), (0,))) * scale

        @pl.when(j == nj - 1)
        def _():
            dq_ref[...] = acc_sc[...]

    qspec = pl.BlockSpec((t, HEAD_DIM), lambda h, i, j: (i, h))
    dospec = pl.BlockSpec((t, HEAD_DIM), lambda h, i, j: (i, do_col0 + h))
    kspec = pl.BlockSpec((t, HEAD_DIM), lambda h, i, j: (jnp.maximum(i - j, 0), h))
    vspec = pl.BlockSpec((t, HEAD_DIM), lambda h, i, j: (jnp.maximum(i - j, 0), v_col0 + h))
    return pl.pallas_call(
        body, name=name, grid=(H, nq, nj),
        in_specs=[qspec, kspec, vspec, qspec, qspec, dospec], out_specs=qspec,
        out_shape=jax.ShapeDtypeStruct((S, H * HEAD_DIM), F32),
        scratch_shapes=[pltpu.VMEM((t, HEAD_DIM), F32)],
        compiler_params=_cparams(("parallel", "parallel", "arbitrary")),
    )(q, k, proj, o, lse, do)


def _attn_bwd_dkv(q, k, proj, v_col0, o, lse, do, do_col0, name):
    S = q.shape[0]
    H = N_ATTN_HEADS
    t = _pick(S, ATTN_TILE)
    nq = S // t
    nj = MAX_WINDOW // t + 1
    scale = HEAD_DIM ** -0.5

    def body(q_ref, k_ref, v_ref, o_ref, lse_ref, do_ref, dk_ref, dv_ref, dk_sc, dv_sc):
        kb, j = pl.program_id(1), pl.program_id(2)

        @pl.when(j == 0)
        def _():
            dk_sc[...] = jnp.zeros_like(dk_sc)
            dv_sc[...] = jnp.zeros_like(dv_sc)

        @pl.when(kb + j < nq)
        def _():
            do = do_ref[...]
            dsum = jnp.sum(do * o_ref[...], axis=-1, keepdims=True)
            lse = jnp.max(lse_ref[...], axis=-1, keepdims=True)
            p = _attn_probs(q_ref[...], k_ref[...], lse, j, t, scale)
            dp = _bdot(do, v_ref[...], ((1,), (1,)))
            ds = p * (dp - dsum)
            dv_sc[...] += _bdot(p, do, ((0,), (0,)))
            dk_sc[...] += _bdot(ds, q_ref[...], ((0,), (0,))) * scale

        @pl.when(j == nj - 1)
        def _():
            dk_ref[...] = dk_sc[...]
            dv_ref[...] = dv_sc[...].astype(BF16)

    def qrow(h, i, j):
        return jnp.minimum(i + j, nq - 1)

    qspec = pl.BlockSpec((t, HEAD_DIM), lambda h, i, j: (qrow(h, i, j), h))
    dospec = pl.BlockSpec((t, HEAD_DIM), lambda h, i, j: (qrow(h, i, j), do_col0 + h))
    kspec = pl.BlockSpec((t, HEAD_DIM), lambda h, i, j: (i, h))
    vspec = pl.BlockSpec((t, HEAD_DIM), lambda h, i, j: (i, v_col0 + h))
    return pl.pallas_call(
        body, name=name, grid=(H, nq, nj),
        in_specs=[qspec, kspec, vspec, qspec, qspec, dospec], out_specs=[kspec, kspec],
        out_shape=[jax.ShapeDtypeStruct((S, H * HEAD_DIM), F32), jax.ShapeDtypeStruct((S, H * HEAD_DIM), BF16)],
        scratch_shapes=[pltpu.VMEM((t, HEAD_DIM), F32), pltpu.VMEM((t, HEAD_DIM), F32)],
        compiler_params=_cparams(("parallel", "parallel", "arbitrary")),
    )(q, k, proj, o, lse, do)


def _conv_pre(x_ref, w_ref):
    x = x_ref[...]
    rows = lax.broadcasted_iota(jnp.int32, x.shape, 0)
    shifted = [x]
    acc = x * w_ref[pl.ds(CONV_WIDTH - 1, 1), :]
    for sft in range(1, CONV_WIDTH):
        xs = jnp.where(rows >= sft, pltpu.roll(x, sft, 0), 0.0)
        shifted.append(xs)
        acc = acc + xs * w_ref[pl.ds(CONV_WIDTH - 1 - sft, 1), :]
    return acc, shifted


def _conv_fwd(proj, col0, width, w, name):
    S = proj.shape[0]

    def body(x_ref, w_ref, y_ref):
        acc, _ = _conv_pre(x_ref, w_ref)
        y_ref[...] = _silu(acc)

    return pl.pallas_call(
        body, name=name, grid=(width // LANES,),
        in_specs=[pl.BlockSpec((S, LANES), lambda c: (0, col0 + c)), pl.BlockSpec((CONV_WIDTH, LANES), lambda c: (0, c))],
        out_specs=pl.BlockSpec((S, LANES), lambda c: (0, c)),
        out_shape=jax.ShapeDtypeStruct((S, width), F32),
        compiler_params=_cparams(("parallel",)),
    )(proj, w)


def _conv_bwd(proj, col0, width, w, dy, name):
    S = proj.shape[0]

    def body(x_ref, w_ref, d_ref, dx_ref, dw_ref):
        acc, shifted = _conv_pre(x_ref, w_ref)
        sig = _sigmoid(acc)
        da = d_ref[...] * (sig * (1.0 + acc * (1.0 - sig)))
        rows = lax.broadcasted_iota(jnp.int32, da.shape, 0)
        dx = da * w_ref[pl.ds(CONV_WIDTH - 1, 1), :]
        dw_ref[pl.ds(CONV_WIDTH - 1, 1), :] = jnp.sum(da * shifted[0], axis=0, keepdims=True)
        for sft in range(1, CONV_WIDTH):
            back = jnp.where(rows < S - sft, pltpu.roll(da, S - sft, 0), 0.0)
            dx = dx + back * w_ref[pl.ds(CONV_WIDTH - 1 - sft, 1), :]
            dw_ref[pl.ds(CONV_WIDTH - 1 - sft, 1), :] = jnp.sum(da * shifted[sft], axis=0, keepdims=True)
        dx_ref[...] = dx.astype(BF16)

    return pl.pallas_call(
        body, name=name, grid=(width // LANES,),
        in_specs=[pl.BlockSpec((S, LANES), lambda c: (0, col0 + c)), pl.BlockSpec((CONV_WIDTH, LANES), lambda c: (0, c)),
                  pl.BlockSpec((S, LANES), lambda c: (0, c))],
        out_specs=[pl.BlockSpec((S, LANES), lambda c: (0, c)), pl.BlockSpec((CONV_WIDTH, LANES), lambda c: (0, c))],
        out_shape=[jax.ShapeDtypeStruct((S, width), BF16), jax.ShapeDtypeStruct((CONV_WIDTH, width), F32)],
        compiler_params=_cparams(("parallel",)),
    )(proj, w, dy)


PREP_CHUNKS = 4


def _chunk_prep(qraw, kraw, v, ab, alog_row, dtb_row, mask_g, mask_b):
    c = qraw.shape[0]
    q = _l2(qraw) * (HEAD_DIM ** -0.5)
    k = _l2(kraw)
    gfull = -jnp.exp(alog_row) * _softplus(ab + dtb_row)
    bfull = _sigmoid(ab)
    g = jnp.sum(jnp.where(mask_g, gfull, 0.0), axis=-1, keepdims=True)
    beta = jnp.sum(jnp.where(mask_b, bfull, 0.0), axis=-1, keepdims=True)
    row = lax.broadcasted_iota(jnp.int32, (c, c), 0)
    col = lax.broadcasted_iota(jnp.int32, (c, c), 1)
    tril, strict, eye = row >= col, row > col, row == col
    g_row = jnp.sum(jnp.where(eye, g, 0.0), axis=0, keepdims=True)
    gc_col = jnp.sum(jnp.where(tril, g_row, 0.0), axis=1, keepdims=True)
    gc_row = jnp.sum(jnp.where(row <= col, g, 0.0), axis=0, keepdims=True)
    gamma = jnp.where(tril, jnp.exp(jnp.where(tril, gc_col - gc_row, 0.0)), 0.0)
    kb = k * beta
    a = jnp.where(strict, _dot_nt(kb, k) * gamma, 0.0)
    eyef = eye.astype(F32)
    t_inv = eyef - a
    p = a
    for _ in range(int(math.log2(c)) - 1):
        p = _dot(p, p)
        t_inv = _dot(t_inv, eyef + p)
    egc = jnp.exp(gc_col)
    u = _dot(t_inv, v * beta)
    w = _dot(t_inv, kb * egc)
    qd = q * egc
    g_last = jnp.sum(g, axis=0, keepdims=True)
    kt = k * jnp.exp(g_last - gc_col)
    dec = jnp.broadcast_to(jnp.exp(g_last), (1, HEAD_DIM))
    intra = _dot_nt(q, k) * gamma
    return u, w, qd, kt, intra, dec


def _lane_masks(h):
    lane = lax.broadcasted_iota(jnp.int32, (1, LANES), 1)
    return lane == h, lane == N_DELTA_HEADS + h


def _prep_specs(tr, ab_col):
    H = N_DELTA_HEADS
    return [
        pl.BlockSpec((tr, HEAD_DIM), lambda i, h: (i, h)),
        pl.BlockSpec((tr, HEAD_DIM), lambda i, h: (i, H + h)),
        pl.BlockSpec((tr, HEAD_DIM), lambda i, h: (i, 2 * H + h)),
        pl.BlockSpec((tr, LANES), lambda i, h: (i, ab_col)),
        pl.BlockSpec((1, LANES), lambda i, h: (0, 0)),
        pl.BlockSpec((1, LANES), lambda i, h: (0, 0)),
    ]


def _prep_out_specs(tr):
    nc = tr // CHUNK
    hs = pl.BlockSpec((tr, HEAD_DIM), lambda i, h: (i, h))
    return [hs, hs, hs, hs,
            pl.BlockSpec((None, tr, CHUNK), lambda i, h: (h, i, 0)),
            pl.BlockSpec((None, nc, 1, HEAD_DIM), lambda i, h: (h, i, 0, 0))]


def _prep_out_shapes(S):
    H = N_DELTA_HEADS
    hs = jax.ShapeDtypeStruct((S, H * HEAD_DIM), F32)
    return [hs, hs, hs, hs, jax.ShapeDtypeStruct((H, S, CHUNK), F32), jax.ShapeDtypeStruct((H, S // CHUNK, 1, HEAD_DIM), F32)]


def _delta_prep_fwd(dqkv, proj, ab_col, alog_row, dtb_row, name):
    S = dqkv.shape[0]
    tr = min(S, PREP_CHUNKS * CHUNK)
    nc = tr // CHUNK

    def body(q_ref, k_ref, v_ref, ab_ref, al_ref, dt_ref, u_ref, w_ref, qd_ref, kt_ref, in_ref, dec_ref):
        mask_g, mask_b = _lane_masks(pl.program_id(1))
        for ci in range(nc):
            rs = pl.ds(ci * CHUNK, CHUNK)
            u, w, qd, kt, intra, dec = _chunk_prep(q_ref[rs, :], k_ref[rs, :], v_ref[rs, :], ab_ref[rs, :],
                                                   al_ref[...], dt_ref[...], mask_g, mask_b)
            u_ref[rs, :] = u
            w_ref[rs, :] = w
            qd_ref[rs, :] = qd
            kt_ref[rs, :] = kt
            in_ref[rs, :] = intra
            dec_ref[ci] = dec

    return pl.pallas_call(
        body, name=name, grid=(S // tr, N_DELTA_HEADS),
        in_specs=_prep_specs(tr, ab_col), out_specs=_prep_out_specs(tr), out_shape=_prep_out_shapes(S),
        compiler_params=_cparams(("parallel", "parallel")),
    )(dqkv, dqkv, dqkv, proj, alog_row, dtb_row)


def _delta_prep_bwd(dqkv, proj, ab_col, alog_row, dtb_row, cots, name):
    S = dqkv.shape[0]
    H = N_DELTA_HEADS
    tr = min(S, PREP_CHUNKS * CHUNK)
    nc = tr // CHUNK

    def body(q_ref, k_ref, v_ref, ab_ref, al_ref, dt_ref, du_ref, dw_ref, dqd_ref, dkt_ref, din_ref, ddec_ref,
             dq_ref, dk_ref, dv_ref, dab_ref, dal_ref, ddt_ref, dab_sc):
        h = pl.program_id(1)
        mask_g, mask_b = _lane_masks(h)

        @pl.when((pl.program_id(0) == 0) & (h == 0))
        def _():
            dal_ref[...] = jnp.zeros_like(dal_ref)
            ddt_ref[...] = jnp.zeros_like(ddt_ref)

        @pl.when(h == 0)
        def _():
            dab_sc[...] = jnp.zeros_like(dab_sc)

        for ci in range(nc):
            rs = pl.ds(ci * CHUNK, CHUNK)
            fn = functools.partial(_chunk_prep, mask_g=mask_g, mask_b=mask_b)
            _, vjp = jax.vjp(fn, q_ref[rs, :], k_ref[rs, :], v_ref[rs, :], ab_ref[rs, :], al_ref[...], dt_ref[...])
            dq, dk, dv, dab, dal, ddt = vjp((du_ref[rs, :], dw_ref[rs, :], dqd_ref[rs, :], dkt_ref[rs, :],
                                             din_ref[rs, :], ddec_ref[ci]))
            dq_ref[rs, :] = dq
            dk_ref[rs, :] = dk
            dv_ref[rs, :] = dv
            dab_sc[rs, :] += dab
            dal_ref[...] += dal
            ddt_ref[...] += ddt

        @pl.when(h == H - 1)
        def _():
            dab_ref[...] = dab_sc[...].astype(BF16)

    hs = pl.BlockSpec((tr, HEAD_DIM), lambda i, h: (i, h))
    hshape = jax.ShapeDtypeStruct((S, H * HEAD_DIM), F32)
    row = pl.BlockSpec((1, LANES), lambda i, h: (0, 0))
    rshape = jax.ShapeDtypeStruct((1, LANES), F32)
    return pl.pallas_call(
        body, name=name, grid=(S // tr, H),
        in_specs=_prep_specs(tr, ab_col) + _prep_out_specs(tr),
        out_specs=[hs, hs, hs, pl.BlockSpec((tr, LANES), lambda i, h: (i, 0)), row, row],
        out_shape=[hshape, hshape, hshape, jax.ShapeDtypeStruct((S, LANES), BF16), rshape, rshape],
        scratch_shapes=[pltpu.VMEM((tr, LANES), F32)],
        compiler_params=_cparams(("arbitrary", "arbitrary")),
    )(dqkv, dqkv, dqkv, proj, alog_row, dtb_row, *cots)


def _scan_step(state, u, w, qd, kt, intra, dec):
    v_new = u - _dot(w, state)
    o = _dot(qd, state) + _dot(intra, v_new)
    return o, state * dec + _dot_tn(kt, v_new)


def _scan_specs(rev, n):
    H = N_DELTA_HEADS

    def cix(i):
        return (n - 1 - i) if rev else i

    row = pl.BlockSpec((CHUNK, H * HEAD_DIM), lambda i: (cix(i), 0))
    return row, pl.BlockSpec((H, CHUNK, CHUNK), lambda i: (0, cix(i), 0)), \
        pl.BlockSpec((H, 1, 1, HEAD_DIM), lambda i: (0, cix(i), 0, 0)), \
        pl.BlockSpec((1, H, HEAD_DIM, HEAD_DIM), lambda i: (cix(i), 0, 0, 0))


def _delta_scan_fwd(u, w, qd, kt, intra, dec, name):
    S = u.shape[0]
    H = N_DELTA_HEADS
    n = S // CHUNK
    row, ispec, dspec, sspec = _scan_specs(False, n)

    def body(u_ref, w_ref, qd_ref, kt_ref, in_ref, dec_ref, o_ref, st_ref, s_sc):
        @pl.when(pl.program_id(0) == 0)
        def _():
            s_sc[...] = jnp.zeros_like(s_sc)

        for h in range(H):
            cs = pl.ds(h * HEAD_DIM, HEAD_DIM)
            state = s_sc[h]
            st_ref[0, h] = state
            o, new = _scan_step(state, u_ref[:, cs], w_ref[:, cs], qd_ref[:, cs], kt_ref[:, cs], in_ref[h], dec_ref[h, 0])
            o_ref[:, cs] = o
            s_sc[h] = new

    return pl.pallas_call(
        body, name=name, grid=(n,),
        in_specs=[row, row, row, row, ispec, dspec], out_specs=[row, sspec],
        out_shape=[jax.ShapeDtypeStruct((S, H * HEAD_DIM), F32), jax.ShapeDtypeStruct((n, H, HEAD_DIM, HEAD_DIM), F32)],
        scratch_shapes=[pltpu.VMEM((H, HEAD_DIM, HEAD_DIM), F32)],
        compiler_params=_cparams(("arbitrary",)),
    )(u, w, qd, kt, intra, dec)


def _delta_scan_bwd(u, w, qd, kt, intra, dec, states, do, name):
    S = u.shape[0]
    H = N_DELTA_HEADS
    n = S // CHUNK
    row, ispec, dspec, sspec = _scan_specs(True, n)

    def body(u_ref, w_ref, qd_ref, kt_ref, in_ref, dec_ref, st_ref, do_ref,
             du_ref, dw_ref, dqd_ref, dkt_ref, din_ref, ddec_ref, ds_sc):
        @pl.when(pl.program_id(0) == 0)
        def _():
            ds_sc[...] = jnp.zeros_like(ds_sc)

        for h in range(H):
            cs = pl.ds(h * HEAD_DIM, HEAD_DIM)
            _, vjp = jax.vjp(_scan_step, st_ref[0, h], u_ref[:, cs], w_ref[:, cs], qd_ref[:, cs], kt_ref[:, cs],
                             in_ref[h], dec_ref[h, 0])
            dstate, du, dw, dqd, dkt, din, ddec = vjp((do_ref[:, cs], ds_sc[h]))
            du_ref[:, cs] = du
            dw_ref[:, cs] = dw
            dqd_ref[:, cs] = dqd
            dkt_ref[:, cs] = dkt
            din_ref[h] = din
            ddec_ref[h, 0] = ddec
            ds_sc[h] = dstate

    hshape = jax.ShapeDtypeStruct((S, H * HEAD_DIM), F32)
    return pl.pallas_call(
        body, name=name, grid=(n,),
        in_specs=[row, row, row, row, ispec, dspec, sspec, row],
        out_specs=[row, row, row, row, ispec, dspec],
        out_shape=[hshape, hshape, hshape, hshape, jax.ShapeDtypeStruct((H, S, CHUNK), F32),
                   jax.ShapeDtypeStruct((H, n, 1, HEAD_DIM), F32)],
        scratch_shapes=[pltpu.VMEM((H, HEAD_DIM, HEAD_DIM), F32)],
        compiler_params=_cparams(("arbitrary",)),
    )(u, w, qd, kt, intra, dec, states, do)


def _gated_norm(od, z, gain):
    return _rms(od, gain) * _silu(z)


def _post_fwd(od, proj, z_col0, gain, name):
    S = od.shape[0]
    H = N_DELTA_HEADS
    tr = _pick(S, 512)

    def body(od_ref, z_ref, g_ref, o_ref):
        o_ref[...] = _gated_norm(od_ref[...], z_ref[...], g_ref[...]).astype(BF16)

    return pl.pallas_call(
        body, name=name, grid=(S // tr, H),
        in_specs=[_head_spec(tr, 0), _head_spec(tr, z_col0), _gain_spec()],
        out_specs=_head_spec(tr, 0), out_shape=jax.ShapeDtypeStruct((S, H * HEAD_DIM), BF16),
        compiler_params=_cparams(("parallel", "parallel")),
    )(od, proj, gain)


def _post_bwd(od, proj, z_col0, gain, do, do_col0, name):
    S = od.shape[0]
    H = N_DELTA_HEADS
    tr = _pick(S, 512)

    def body(od_ref, z_ref, g_ref, do_ref, dod_ref, dz_ref, dg_ref):
        @pl.when((pl.program_id(0) == 0) & (pl.program_id(1) == 0))
        def _():
            dg_ref[...] = jnp.zeros_like(dg_ref)

        _, vjp = jax.vjp(_gated_norm, od_ref[...], z_ref[...], g_ref[...])
        dod, dz, dg = vjp(do_ref[...])
        dod_ref[...] = dod
        dz_ref[...] = dz.astype(BF16)
        dg_ref[...] += dg

    return pl.pallas_call(
        body, name=name, grid=(S // tr, H),
        in_specs=[_head_spec(tr, 0), _head_spec(tr, z_col0), _gain_spec(), _head_spec(tr, do_col0)],
        out_specs=[_head_spec(tr, 0), _head_spec(tr, 0), _gain_spec()],
        out_shape=[jax.ShapeDtypeStruct((S, H * HEAD_DIM), F32), jax.ShapeDtypeStruct((S, H * HEAD_DIM), BF16),
                   jax.ShapeDtypeStruct((1, HEAD_DIM), F32)],
        compiler_params=_cparams(("arbitrary", "arbitrary")),
    )(od, proj, gain, do)


def _adam_math(w, g, m, v):
    m = ADAM_B1 * m + (1.0 - ADAM_B1) * g
    v = ADAM_B2 * v + (1.0 - ADAM_B2) * (g * g)
    m_hat = m / (1.0 - ADAM_B1 ** ADAM_STEP)
    v_hat = v / (1.0 - ADAM_B2 ** ADAM_STEP)
    delta = -ADAM_LR * (m_hat / (jnp.sqrt(v_hat) + ADAM_EPS) + ADAM_WD * w)
    return delta, m, v


def _adamw(w, g, m, v, name):
    R, C = w.shape
    tr = R if R * C * 4 <= (1 << 20) else _pick8(R, max(8, (1 << 20) // (C * 4)))

    def body(w_ref, g_ref, m_ref, v_ref, d_ref, nm_ref, nv_ref):
        d, nm, nv = _adam_math(w_ref[...], g_ref[...], m_ref[...], v_ref[...])
        d_ref[...] = d
        nm_ref[...] = nm
        nv_ref[...] = nv

    spec = pl.BlockSpec((tr, C), lambda i: (i, 0))
    shp = jax.ShapeDtypeStruct((R, C), F32)
    return pl.pallas_call(
        body, name=name, grid=(R // tr,), in_specs=[spec] * 4, out_specs=[spec] * 3, out_shape=[shp] * 3,
        compiler_params=_cparams(("parallel",)),
    )(w, g, m, v)


def _pick8(dim, pref):
    t = (min(dim, pref) // 8) * 8
    while t >= 8:
        if dim % t == 0:
            return t
        t -= 8
    return dim


def _adamw_outer(w, m, v, cond_t, rhs, name):
    R, C = w.shape
    tr = _pick8(R, 128)
    nb = cond_t.shape[1]
    lhs_t = cond_t

    def body(w_ref, m_ref, v_ref, a_ref, b_ref, g_ref, d_ref, nm_ref, nv_ref):
        g = _dot(_silu(a_ref[...]), b_ref[...])
        d, nm, nv = _adam_math(w_ref[...], g, m_ref[...], v_ref[...])
        g_ref[...] = g
        d_ref[...] = d
        nm_ref[...] = nm
        nv_ref[...] = nv

    spec = pl.BlockSpec((tr, C), lambda i: (i, 0))
    shp = jax.ShapeDtypeStruct((R, C), F32)
    return pl.pallas_call(
        body, name=name, grid=(R // tr,),
        in_specs=[spec, spec, spec, pl.BlockSpec((tr, nb), lambda i: (i, 0)), pl.BlockSpec((nb, C), lambda i: (0, 0))],
        out_specs=[spec] * 4, out_shape=[shp] * 4,
        compiler_params=_cparams(("parallel",)),
    )(w, m, v, lhs_t, rhs)


def _ada_fwd(cond, w, bias, name):
    a = cond
    nb, K = a.shape
    N = w.shape[1]
    tn = _pick(N, 512)

    def body(a_ref, w_ref, b_ref, o_ref):
        o_ref[...] = _dot(_silu(a_ref[...]), w_ref[...]) + b_ref[...]

    return pl.pallas_call(
        body, name=name, grid=(N // tn,),
        in_specs=[pl.BlockSpec((nb, K), lambda j: (0, 0)), pl.BlockSpec((K, tn), lambda j: (0, j)), pl.BlockSpec((1, tn), lambda j: (0, j))],
        out_specs=pl.BlockSpec((nb, tn), lambda j: (0, j)), out_shape=jax.ShapeDtypeStruct((nb, N), F32),
        compiler_params=_cparams(("parallel",)),
    )(a, w, bias)


def _add_cast(parts, out_dtypes, name):
    shape = parts[0].shape
    G, R, C = shape
    tr = _pick8(R, max(8, (1 << 20) // (C * 4)))
    n_in = len(parts)

    def body(*refs):
        acc = refs[0][...].astype(F32)
        for r in refs[1:n_in]:
            acc = acc + r[...].astype(F32)
        for o, dt in zip(refs[n_in:], out_dtypes):
            o[...] = acc.astype(dt)

    spec = pl.BlockSpec((1, tr, C), lambda g, i: (g, i, 0))
    outs = pl.pallas_call(
        body, name=name, grid=(G, R // tr), in_specs=[spec] * n_in, out_specs=[spec] * len(out_dtypes),
        out_shape=[jax.ShapeDtypeStruct(shape, dt) for dt in out_dtypes],
        compiler_params=_cparams(("parallel", "parallel")),
    )(*parts)
    return outs


def _me():
    return lax.axis_index("x"), lax.axis_index("y"), lax.axis_index("c")


def _xor_peer(k):
    x, y, c = _me()
    dx, dy, dc = (k >> 2) & 1, (k >> 1) & 1, k & 1
    return (x ^ dx if dx else x, y ^ dy if dy else y, c ^ dc if dc else c)


ANY = pl.BlockSpec(memory_space=pl.ANY)


def _all_gather_small(v, name):
    R, C = v.shape

    def body(v_ref, out_ref, send_sems, recv_sems):
        x, y, c = _me()
        mine = 4 * x + 2 * y + c
        out_ref[mine] = v_ref[...]
        copies = []
        for k in range(1, 8):
            cp = pltpu.make_async_remote_copy(src_ref=v_ref, dst_ref=out_ref.at[mine], send_sem=send_sems.at[k - 1],
                                              recv_sem=recv_sems.at[k - 1], device_id=_xor_peer(k), device_id_type=MESH)
            cp.start()
            copies.append(cp)
        for k in range(1, 8):
            px, py, pc = _xor_peer(k)
            pltpu.make_async_remote_copy(src_ref=v_ref, dst_ref=out_ref.at[4 * px + 2 * py + pc], send_sem=send_sems.at[k - 1],
                                         recv_sem=recv_sems.at[k - 1], device_id=_xor_peer(k), device_id_type=MESH).wait_recv()
        for cp in copies:
            cp.wait_send()

    return pl.pallas_call(
        body, name=name, out_shape=jax.ShapeDtypeStruct((8, R, C), F32),
        in_specs=[pl.BlockSpec(memory_space=pltpu.VMEM)], out_specs=pl.BlockSpec(memory_space=pltpu.VMEM),
        scratch_shapes=[pltpu.SemaphoreType.DMA((7,)), pltpu.SemaphoreType.DMA((7,))],
        compiler_params=pltpu.CompilerParams(vmem_limit_bytes=VMEM_LIMIT),
    )(v)


def _chip_peers():
    x, y, _ = _me()
    return [(1, (x, 1 - y)), (2, (1 - x, y)), (3, (1 - x, 1 - y))]


def _all_gather_shards(shards, name):
    n = len(shards)

    def body(*refs):
        ins, outs = refs[:n], refs[n:2 * n]
        send_sems, recv_sems, local_sems = refs[2 * n:]
        x, y, c = _me()
        chip = 2 * x + y
        sib = (x, y, 1 - c)
        peers = _chip_peers()
        locals_, sends = [], []
        for t in range(n):
            half = ins[t].shape[0] // 2
            lc = pltpu.make_async_copy(ins[t], outs[t].at[chip], local_sems.at[t])
            lc.start()
            locals_.append(lc)
            mine = pl.ds(c * half, half)
            for p, (k, (px, py)) in enumerate(peers):
                cp = pltpu.make_async_remote_copy(src_ref=ins[t].at[mine], dst_ref=outs[t].at[chip, mine],
                                                  send_sem=send_sems.at[6 * t + p], recv_sem=recv_sems.at[6 * t + p],
                                                  device_id=(px, py, c), device_id_type=MESH)
                cp.start()
                sends.append(cp)
        for t in range(n):
            half = ins[t].shape[0] // 2
            mine = pl.ds(c * half, half)
            for p, (k, (px, py)) in enumerate(peers):
                src_chip = 2 * px + py
                landed = outs[t].at[src_chip, mine]
                pltpu.make_async_remote_copy(src_ref=landed, dst_ref=landed, send_sem=send_sems.at[6 * t + p],
                                             recv_sem=recv_sems.at[6 * t + p], device_id=(px, py, c), device_id_type=MESH).wait_recv()
                fw = pltpu.make_async_remote_copy(src_ref=landed, dst_ref=landed, send_sem=send_sems.at[6 * t + 3 + p],
                                                  recv_sem=recv_sems.at[6 * t + 3 + p], device_id=sib, device_id_type=MESH)
                fw.start()
                sends.append(fw)
        for t in range(n):
            half = ins[t].shape[0] // 2
            theirs = pl.ds((1 - c) * half, half)
            for p, (k, (px, py)) in enumerate(peers):
                got = outs[t].at[2 * px + py, theirs]
                pltpu.make_async_remote_copy(src_ref=got, dst_ref=got, send_sem=send_sems.at[6 * t + 3 + p],
                                             recv_sem=recv_sems.at[6 * t + 3 + p], device_id=sib, device_id_type=MESH).wait_recv()
        for cp in sends:
            cp.wait_send()
        for lc in locals_:
            lc.wait()

    return pl.pallas_call(
        body, name=name,
        out_shape=[jax.ShapeDtypeStruct((4,) + s.shape, s.dtype) for s in shards],
        in_specs=[ANY] * n, out_specs=[ANY] * n,
        scratch_shapes=[pltpu.SemaphoreType.DMA((6 * n,)), pltpu.SemaphoreType.DMA((6 * n,)), pltpu.SemaphoreType.DMA((n,))],
    )(*shards)


def _swap_halves_with_sibling(slabs, name):
    n = len(slabs)

    def body(*refs):
        ins, outs = refs[:n], refs[n:2 * n]
        send_sems, recv_sems = refs[2 * n:]
        x, y, c = _me()
        sib = (x, y, 1 - c)
        cps = []
        for t in range(n):
            half = ins[t].shape[1] // 2
            cp = pltpu.make_async_remote_copy(src_ref=ins[t].at[:, pl.ds((1 - c) * half, half)], dst_ref=outs[t],
                                              send_sem=send_sems.at[t], recv_sem=recv_sems.at[t], device_id=sib, device_id_type=MESH)
            cp.start()
            cps.append(cp)
        for cp in cps:
            cp.wait()

    return pl.pallas_call(
        body, name=name,
        out_shape=[jax.ShapeDtypeStruct((4, s.shape[1] // 2, s.shape[2]), s.dtype) for s in slabs],
        in_specs=[ANY] * n, out_specs=[ANY] * n,
        scratch_shapes=[pltpu.SemaphoreType.DMA((n,)), pltpu.SemaphoreType.DMA((n,))],
    )(*slabs)


def _exchange_between_chips(slabs, name):
    n = len(slabs)

    def body(*refs):
        ins, outs = refs[:n], refs[n:2 * n]
        send_sems, recv_sems = refs[2 * n:]
        x, y, c = _me()
        chip = 2 * x + y
        cps = []
        for t in range(n):
            for p, (k, (px, py)) in enumerate(_chip_peers()):
                cp = pltpu.make_async_remote_copy(src_ref=ins[t].at[2 * px + py], dst_ref=outs[t].at[chip],
                                                  send_sem=send_sems.at[3 * t + p], recv_sem=recv_sems.at[3 * t + p],
                                                  device_id=(px, py, c), device_id_type=MESH)
                cp.start()
                cps.append(cp)
        for t in range(n):
            for p, (k, (px, py)) in enumerate(_chip_peers()):
                got = outs[t].at[2 * px + py]
                pltpu.make_async_remote_copy(src_ref=got, dst_ref=got, send_sem=send_sems.at[3 * t + p],
                                             recv_sem=recv_sems.at[3 * t + p], device_id=(px, py, c), device_id_type=MESH).wait_recv()
        for cp in cps:
            cp.wait_send()

    return pl.pallas_call(
        body, name=name,
        out_shape=[jax.ShapeDtypeStruct(s.shape, s.dtype) for s in slabs],
        in_specs=[ANY] * n, out_specs=[ANY] * n,
        scratch_shapes=[pltpu.SemaphoreType.DMA((3 * n,)), pltpu.SemaphoreType.DMA((3 * n,))],
    )(*slabs)


def _join_halves_with_sibling(halves, name):
    n = len(halves)

    def body(*refs):
        ins, outs = refs[:n], refs[n:2 * n]
        send_sems, recv_sems, local_sems = refs[2 * n:]
        x, y, c = _me()
        sib = (x, y, 1 - c)
        cps, lcs = [], []
        for t in range(n):
            lc = pltpu.make_async_copy(ins[t], outs[t].at[c], local_sems.at[t])
            lc.start()
            lcs.append(lc)
            cp = pltpu.make_async_remote_copy(src_ref=ins[t], dst_ref=outs[t].at[c], send_sem=send_sems.at[t],
                                              recv_sem=recv_sems.at[t], device_id=sib, device_id_type=MESH)
            cp.start()
            cps.append(cp)
        for t in range(n):
            got = outs[t].at[1 - c]
            pltpu.make_async_remote_copy(src_ref=got, dst_ref=got, send_sem=send_sems.at[t], recv_sem=recv_sems.at[t],
                                         device_id=sib, device_id_type=MESH).wait_recv()
        for cp in cps:
            cp.wait_send()
        for lc in lcs:
            lc.wait()

    return pl.pallas_call(
        body, name=name,
        out_shape=[jax.ShapeDtypeStruct((2,) + s.shape, s.dtype) for s in halves],
        in_specs=[ANY] * n, out_specs=[ANY] * n,
        scratch_shapes=[pltpu.SemaphoreType.DMA((n,)), pltpu.SemaphoreType.DMA((n,)), pltpu.SemaphoreType.DMA((n,))],
    )(*halves)


def _rope_tables(positions):
    half = ROPE_DIM // 2
    S = positions.shape[0]
    inv_freq = ROPE_THETA ** (-jnp.arange(half, dtype=F32) / half)
    ang = positions.astype(F32)[:, None] * inv_freq
    cos, sin = jnp.cos(ang), jnp.sin(ang)
    zeros = functools.partial(jnp.zeros, dtype=F32)
    cc = jnp.concatenate([cos, cos, jnp.ones((S, HEAD_DIM - ROPE_DIM), F32)], axis=1)
    s1 = jnp.concatenate([-sin, zeros((S, HEAD_DIM - half))], axis=1)
    s2 = jnp.concatenate([zeros((S, half)), sin, zeros((S, HEAD_DIM - ROPE_DIM))], axis=1)
    return cc, s1, s2


def _ffn_fwd(x, gain, shift, scale, gate, w_gu, w_d, fs, tag):
    h = _pre_fwd(x, gain, shift, scale, tag + "_pre")
    ab = _matmul(h, w_gu, "nn", F32, tag + "_gate_up", tm=512, tn=1024, tk=4096)
    s = _swiglu_fwd(ab, fs, tag + "_swiglu")
    f = _matmul(s, w_d, "nn", F32, tag + "_down", tm=512, tn=1024, tk=8192)
    xn = _residual_fwd(x, gate, f, 0.5, tag + "_res")
    return xn, (x, h, ab, s, f)


def _ffn_bwd(dxn, saved, gain, shift, scale, gate, w_gu, w_d, fs, tag):
    x, h, ab, s, f = saved
    df, dgate = _residual_bwd(gate, f, dxn, 0.5, tag + "_res_bwd")
    ds = _matmul(df, w_d, "nt", F32, tag + "_down_dx", tm=1024, tn=1408, tk=4096)
    dw_d = _matmul(s, df, "tn", F32, tag + "_down_dw", tm=1408, tn=1024, tk=1024)
    dab = _swiglu_bwd(ab, ds, fs, tag + "_swiglu_bwd")
    dh = _matmul(dab, w_gu, "nt", F32, tag + "_gate_up_dx", tm=1024, tn=1024, tk=2816)
    dw_gu = _matmul(h, dab, "tn", F32, tag + "_gate_up_dw", tm=1024, tn=1024, tk=1024)
    dx, dgain, dshift, dscale = _pre_bwd(x, gain, shift, scale, dh, dxn, tag + "_pre_bwd")
    return dx, dw_gu, dw_d, dgain, dshift, dscale, dgate


def _flat_pad(parts, rows, cols):
    flat = jnp.concatenate([p.reshape(-1).astype(F32) for p in parts])
    return jnp.pad(flat, (0, rows * cols - flat.shape[0])).reshape(rows, cols)


def _cols_to_slabs(w, n):
    R, NC = w.shape
    return jnp.transpose(w.reshape(R, n, NC // n), (1, 0, 2))


def kernel(x, c, positions, w_ada, b_ada, ffn1_norm, ffn1_w_gate, ffn1_w_up, ffn1_w_down, mix_norm, w_in, conv_w, q_norm, k_norm, a_log, dt_bias, delta_out_norm, w_out, ffn2_norm, ffn2_w_gate, ffn2_w_up, ffn2_w_down, loss_target, m_w_ada, m_b_ada, m_ffn1_norm, m_ffn1_w_gate, m_ffn1_w_up, m_ffn1_w_down, m_mix_norm, m_w_in, m_conv_w, m_q_norm, m_k_norm, m_a_log, m_dt_bias, m_delta_out_norm, m_w_out, m_ffn2_norm, m_ffn2_w_gate, m_ffn2_w_up, m_ffn2_w_down, v_w_ada, v_b_ada, v_ffn1_norm, v_ffn1_w_gate, v_ffn1_w_up, v_ffn1_w_down, v_mix_norm, v_w_in, v_conv_w, v_q_norm, v_k_norm, v_a_log, v_dt_bias, v_delta_out_norm, v_w_out, v_ffn2_norm, v_ffn2_w_gate, v_ffn2_w_up, v_ffn2_w_down):
    xi, yi, ci = _me()
    chip = 2 * xi + yi
    dev = 2 * chip + ci
    xs = x[0]
    S, D = xs.shape
    HA, HD = N_ATTN_HEADS, N_DELTA_HEADS
    fs = ffn1_w_gate.shape[2]
    n_mod_shard = w_ada.shape[2]
    in_shard = w_in.shape[2]
    in_width = 4 * in_shard
    in_pad = -(-in_width // LANES) * LANES
    conv_shard = conv_w.shape[2]
    conv_width = 4 * conv_shard

    pack0 = jnp.zeros((8, max(D, conv_shard)), F32)
    pack0 = pack0.at[0, :D].set(c[0]).at[1:1 + CONV_WIDTH, :conv_shard].set(conv_w[0])
    got0 = _all_gather_small(pack0, "gather_cond")
    c_all = got0[:, 0, :D]
    conv_full = jnp.transpose(got0[::2, 1:1 + CONV_WIDTH, :conv_shard], (1, 0, 2)).reshape(CONV_WIDTH, conv_width)
    b_ada_mine = lax.dynamic_slice(b_ada, (0, chip * n_mod_shard), (1, n_mod_shard))
    mod_part = _ada_fwd(c_all, w_ada[0], b_ada_mine, "ada_fwd")
    got1 = _all_gather_small(mod_part, "gather_mod")
    mod = lax.dynamic_index_in_dim(got1[::2], dev, axis=1, keepdims=False).reshape(1, 4 * n_mod_shard)
    sh1, sc1, gt1, sh2, sc2, gt2, sh3, sc3, gt3 = [mod[:, i * D:(i + 1) * D] for i in range(N_MOD)]

    shards = [w[0].astype(BF16) for w in (ffn1_w_gate, ffn1_w_up, ffn1_w_down, w_in, w_out, ffn2_w_gate, ffn2_w_up, ffn2_w_down)]
    g1g, g1u, g1d, gin, gout, g2g, g2u, g2d = _all_gather_shards(shards, "gather_weights")

    def gate_up(gg, gu):
        return jnp.transpose(jnp.concatenate([gg, gu], axis=2), (1, 0, 2)).reshape(D, 8 * fs)

    w_gu1, w_gu2 = gate_up(g1g, g1u), gate_up(g2g, g2u)
    w_d1, w_d2 = g1d.reshape(4 * fs, D), g2d.reshape(4 * fs, D)
    w_in_f = jnp.pad(jnp.transpose(gin, (1, 0, 2)).reshape(D, in_width), ((0, 0), (0, in_pad - in_width)))
    w_out_f = gout.reshape(-1, D)

    x1, saved1 = _ffn_fwd(xs, ffn1_norm, sh1, sc1, gt1, w_gu1, w_d1, fs, "ffn1")

    cc, s1, s2 = _rope_tables(positions[0])
    alog_row = jnp.pad(a_log, ((0, 0), (0, LANES - HD)))
    dtb_row = jnp.pad(dt_bias, ((0, 0), (0, LANES - HD)))
    col_k, col_v, col_d, col_z, col_ab = HA, 2 * HA, 3 * HA, 3 * HA + 3 * HD, 3 * HA + 4 * HD
    h2 = _pre_fwd(x1, mix_norm, sh2, sc2, "mix_pre")
    proj = _matmul(h2, w_in_f, "nn", F32, "mix_in_proj", tm=512, tn=1024, tk=4096)
    qa, ka = _attn_prep_fwd(proj, q_norm, k_norm, cc, s1, s2, "attn_prep")
    oa, oa_b, lse = _attn_fwd(qa, ka, proj, col_v, "attn_fwd")
    dqkv = _conv_fwd(proj, col_d, conv_width, conv_full, "conv_fwd")
    prep = _delta_prep_fwd(dqkv, proj, col_ab, alog_row, dtb_row, "delta_prep")
    od_raw, states = _delta_scan_fwd(*prep, "delta_scan")
    od = _post_fwd(od_raw, proj, col_z, delta_out_norm, "delta_post")
    o = jnp.concatenate([oa_b, od], axis=1)
    mo = _matmul(o, w_out_f, "nn", F32, "mix_out_proj", tm=512, tn=1024, tk=4096)
    x2 = _residual_fwd(x1, gt2, mo, 1.0, "mix_res")

    x3, saved3 = _ffn_fwd(x2, ffn2_norm, sh3, sc3, gt3, w_gu2, w_d2, fs, "ffn2")
    loss_part, dy = _loss_head(x3, loss_target[0], "loss_head")
    loss = lax.psum(loss_part[0, 0], ("x", "y", "c"))

    dx2, dw_gu2, dw_d2, dgain3, dsh3, dsc3, dgt3 = _ffn_bwd(dy, saved3, ffn2_norm, sh3, sc3, gt3, w_gu2, w_d2, fs, "ffn2")

    dmo, dgt2 = _residual_bwd(gt2, mo, dx2, 1.0, "mix_res_bwd")
    do = _matmul(dmo, w_out_f, "nt", F32, "mix_out_dx", tm=1024, tn=1024, tk=4096)
    dw_out = _matmul(o, dmo, "tn", F32, "mix_out_dw", tm=1024, tn=1024, tk=1024)
    dq = _attn_bwd_dq(qa, ka, proj, col_v, oa, lse, do, 0, "attn_bwd_dq")
    dk, dv = _attn_bwd_dkv(qa, ka, proj, col_v, oa, lse, do, 0, "attn_bwd_dkv")
    dpq, dpk, dq_gain, dk_gain = _attn_prep_bwd(proj, q_norm, k_norm, cc, s1, s2, dq, dk, "attn_prep_bwd")
    dod, dz, ddn_gain = _post_bwd(od_raw, proj, col_z, delta_out_norm, do, HA, "delta_post_bwd")
    cots = _delta_scan_bwd(*prep, states, dod, "delta_scan_bwd")
    ddq, ddk, ddv, dab, dalog, ddtb = _delta_prep_bwd(dqkv, proj, col_ab, alog_row, dtb_row, cots, "delta_prep_bwd")
    dconv_in, dconv_w = _conv_bwd(proj, col_d, conv_width, conv_full, jnp.concatenate([ddq, ddk, ddv], axis=1), "conv_bwd")
    dproj = jnp.concatenate([dpq, dpk, dv, dconv_in, dz, dab], axis=1)
    dh2 = _matmul(dproj, w_in_f, "nt", F32, "mix_in_dx", tm=1024, tn=1024, tk=2816)
    dw_in = _matmul(h2, dproj, "tn", F32, "mix_in_dw", tm=1024, tn=1024, tk=1024)
    dx1, dgain2, dsh2, dsc2 = _pre_bwd(x1, mix_norm, sh2, sc2, dh2, dx2, "mix_pre_bwd")

    dx0, dw_gu1, dw_d1, dgain1, dsh1, dsc1, dgt1 = _ffn_bwd(dx1, saved1, ffn1_norm, sh1, sc1, gt1, w_gu1, w_d1, fs, "ffn1")

    n_small = N_MOD * D + 3 * D + 5 * LANES + CONV_WIDTH * conv_width
    cols_small = -(-n_small // (8 * LANES)) * LANES
    small = _flat_pad([dsh1, dsc1, dgt1, dsh2, dsc2, dgt2, dsh3, dsc3, dgt3, dgain1, dgain2, dgain3,
                       dq_gain, dk_gain, dalog, ddtb, ddn_gain, dconv_w], 8, cols_small)
    got2 = _all_gather_small(small, "gather_small_grads")
    small_sum = _add_cast([got2[d:d + 1] for d in range(8)], [F32], "sum_small_grads")[0].reshape(-1)
    dmod_all = got2.reshape(8, -1)[:, :N_MOD * D]
    off = [0]

    def take(n):
        off[0] += n
        return small_sum[off[0] - n:off[0]]

    g_b_ada = take(N_MOD * D).reshape(1, -1)
    g_ffn1_norm, g_mix_norm, g_ffn2_norm = take(D).reshape(1, D), take(D).reshape(1, D), take(D).reshape(1, D)
    g_q_norm, g_k_norm = take(LANES).reshape(1, -1), take(LANES).reshape(1, -1)
    g_a_log, g_dt_bias = take(LANES)[:HD].reshape(1, HD), take(LANES)[:HD].reshape(1, HD)
    g_dn = take(LANES).reshape(1, -1)
    g_conv_full = take(CONV_WIDTH * conv_width).reshape(CONV_WIDTH, conv_width)
    g_conv = lax.dynamic_slice(g_conv_full, (0, chip * conv_shard), (CONV_WIDTH, conv_shard))

    slabs = [_cols_to_slabs(dw_gu1, 4), dw_d1.reshape(4, fs, D), _cols_to_slabs(dw_in[:, :in_width], 4),
             dw_out.reshape(4, -1, D), _cols_to_slabs(dw_gu2, 4), dw_d2.reshape(4, fs, D)]
    from_sibling = _swap_halves_with_sibling(slabs, "rs_sibling_swap")
    chip_sums32, chip_sums16 = [], []
    for t, (slab, other) in enumerate(zip(slabs, from_sibling)):
        half = slab.shape[1] // 2
        mine = lax.dynamic_slice_in_dim(slab, ci * half, half, axis=1)
        p32, p16 = _add_cast([mine, other], [F32, BF16], "rs_chip_sum_%d" % t)
        chip_sums32.append(p32)
        chip_sums16.append(p16)
    from_chips = _exchange_between_chips(chip_sums16, "rs_chip_exchange")
    halves = []
    for t, (p32, got) in enumerate(zip(chip_sums32, from_chips)):
        parts = [lax.dynamic_index_in_dim(p32, chip, axis=0, keepdims=True)]
        parts += [lax.dynamic_index_in_dim(got, (chip + k) % 4, axis=0, keepdims=True) for k in (1, 2, 3)]
        halves.append(_add_cast(parts, [F32], "rs_total_%d" % t)[0][0])
    joined = _join_halves_with_sibling(halves, "rs_sibling_join")
    g_gu1, g_d1, g_in, g_out, g_gu2, g_d2 = [j.reshape(-1, j.shape[2]) for j in joined]

    res = {}

    def upd(name, w, g, m, v):
        d, nm, nv = _adamw(w[0], g, m[0], v[0], "adamw_" + name)
        res[name] = (g[None], d[None], nm[None], nv[None])

    upd("ffn1_w_gate", ffn1_w_gate, g_gu1[:, :fs], m_ffn1_w_gate, v_ffn1_w_gate)
    upd("ffn1_w_up", ffn1_w_up, g_gu1[:, fs:], m_ffn1_w_up, v_ffn1_w_up)
    upd("ffn1_w_down", ffn1_w_down, g_d1, m_ffn1_w_down, v_ffn1_w_down)
    upd("w_in", w_in, g_in, m_w_in, v_w_in)
    upd("w_out", w_out, g_out, m_w_out, v_w_out)
    upd("ffn2_w_gate", ffn2_w_gate, g_gu2[:, :fs], m_ffn2_w_gate, v_ffn2_w_gate)
    upd("ffn2_w_up", ffn2_w_up, g_gu2[:, fs:], m_ffn2_w_up, v_ffn2_w_up)
    upd("ffn2_w_down", ffn2_w_down, g_d2, m_ffn2_w_down, v_ffn2_w_down)
    upd("conv_w", conv_w, g_conv, m_conv_w, v_conv_w)

    dmod_mine = lax.dynamic_slice(dmod_all, (0, chip * n_mod_shard), (8, n_mod_shard))
    g, d, nm, nv = _adamw_outer(w_ada[0], m_w_ada[0], v_w_ada[0], jnp.transpose(c_all), dmod_mine, "adamw_w_ada")
    res["w_ada"] = (g[None], d[None], nm[None], nv[None])

    rep = [("b_ada", b_ada, g_b_ada, m_b_ada, v_b_ada), ("ffn1_norm", ffn1_norm, g_ffn1_norm, m_ffn1_norm, v_ffn1_norm),
           ("mix_norm", mix_norm, g_mix_norm, m_mix_norm, v_mix_norm), ("ffn2_norm", ffn2_norm, g_ffn2_norm, m_ffn2_norm, v_ffn2_norm),
           ("q_norm", q_norm, g_q_norm, m_q_norm, v_q_norm), ("k_norm", k_norm, g_k_norm, m_k_norm, v_k_norm),
           ("a_log", a_log, g_a_log, m_a_log, v_a_log), ("dt_bias", dt_bias, g_dt_bias, m_dt_bias, v_dt_bias),
           ("delta_out_norm", delta_out_norm, g_dn, m_delta_out_norm, v_delta_out_norm)]
    n_rep = sum(-(-r[1].shape[1] // LANES) * LANES for r in rep)
    cols_rep = -(-n_rep // (8 * LANES)) * LANES

    def pack_rep(idx):
        return _flat_pad([jnp.pad(r[idx], ((0, 0), (0, -r[idx].shape[1] % LANES))) for r in rep], 8, cols_rep)

    d_rep, nm_rep, nv_rep = [a.reshape(-1) for a in _adamw(pack_rep(1), pack_rep(2), pack_rep(3), pack_rep(4), "adamw_small")]
    o2 = 0
    for name, w, g, _, _ in rep:
        n = w.shape[1]
        res[name] = (g, d_rep[o2:o2 + n].reshape(1, n), nm_rep[o2:o2 + n].reshape(1, n), nv_rep[o2:o2 + n].reshape(1, n))
        o2 += -(-n // LANES) * LANES

    order = ["w_ada", "b_ada", "ffn1_norm", "ffn1_w_gate", "ffn1_w_up", "ffn1_w_down", "mix_norm", "w_in", "conv_w", "q_norm",
             "k_norm", "a_log", "dt_bias", "delta_out_norm", "w_out", "ffn2_norm", "ffn2_w_gate", "ffn2_w_up", "ffn2_w_down"]
    return (loss, dx0[None], *[res[n][0] for n in order], *[res[n][1] for n in order],
            *[res[n][2] for n in order], *[res[n][3] for n in order])
```

```python
import functools
import math

import jax
import jax.numpy as jnp
from jax import lax
from jax.experimental import pallas as pl
from jax.experimental.pallas import tpu as pltpu

F32 = jnp.float32
BF16 = jnp.bfloat16
MESH = pl.DeviceIdType.MESH

HEAD_DIM = 128
N_ATTN_HEADS = 8
N_DELTA_HEADS = 8
DILATED_PATTERNS = ((128, 1), (512, 4), (2048, 16))
MAX_WINDOW = 2048
ROPE_THETA = 500000.0
ROPE_DIM = HEAD_DIM // 4
CONV_WIDTH = 4
CHUNK = 64
NORM_EPS = 1e-6
N_MOD = 9
ADAM_LR = 0.001
ADAM_B1 = 0.9
ADAM_B2 = 0.999
ADAM_EPS = 1e-08
ADAM_WD = 0.01
ADAM_STEP = 10

LANES = 128
VMEM_LIMIT = 56 * 1024 * 1024
ATTN_TILE = 512
HIGHEST = lax.Precision.HIGHEST


def _cparams(sem=None):
    return pltpu.CompilerParams(dimension_semantics=sem, vmem_limit_bytes=VMEM_LIMIT)


def _pick(dim, pref):
    if dim <= pref:
        return dim
    t = (pref // LANES) * LANES
    while t >= LANES:
        if dim % t == 0:
            return t
        t -= LANES
    return dim


def _sigmoid(x):
    return 1.0 / (1.0 + jnp.exp(-x))


def _silu(x):
    return x * _sigmoid(x)


def _softplus(x):
    return jnp.maximum(x, 0.0) + jnp.log(1.0 + jnp.exp(-jnp.abs(x)))


def _rms(x, gain):
    return x * lax.rsqrt(jnp.mean(x * x, axis=-1, keepdims=True) + NORM_EPS) * gain


def _l2(x):
    return x * lax.rsqrt(jnp.sum(x * x, axis=-1, keepdims=True) + NORM_EPS)


def _modulate(x, gain, shift, scale):
    return _rms(x, gain) * (1.0 + scale) + shift


def _dot(a, b):
    return lax.dot_general(a, b, (((1,), (0,)), ((), ())), precision=HIGHEST, preferred_element_type=F32)


def _bdot(a, b, dims):
    return lax.dot_general(a.astype(BF16), b.astype(BF16), (dims, ((), ())), preferred_element_type=F32)


_NN, _NT, _TN = ((1,), (0,)), ((1,), (1,)), ((0,), (0,))
HIGH = lax.Precision.HIGH


def _dot3(a, b, dims=_NN):
    return lax.dot_general(a, b, (dims, ((), ())), precision=HIGH, preferred_element_type=F32)


@jax.custom_vjp
def _mm_nn(a, b):
    return _bdot(a, b, _NN)


_mm_nn.defvjp(lambda a, b: (_bdot(a, b, _NN), (a, b)),
              lambda res, g: (_bdot(g, res[1], _NT), _bdot(res[0], g, _TN)))


@jax.custom_vjp
def _mm_nt(a, b):
    return _bdot(a, b, _NT)


_mm_nt.defvjp(lambda a, b: (_bdot(a, b, _NT), (a, b)),
              lambda res, g: (_bdot(g, res[1], _NN), _bdot(g, res[0], _TN)))


def _tri_inv(a, eyef):
    t_inv = eyef - a
    p = a
    for _ in range(int(math.log2(a.shape[0])) - 1):
        p = _dot3(p, p)
        t_inv = _dot3(t_inv, eyef + p)
    return t_inv


@jax.custom_vjp
def _tri_inv_saved(a, t_inv):
    return t_inv


_tri_inv_saved.defvjp(lambda a, t_inv: (t_inv, t_inv),
                      lambda t_inv, g: (-_dot3(t_inv, _dot3(g, t_inv, _NT), _TN), jnp.zeros_like(t_inv)))


_MM_DIMS = {"nn": ((1,), (0,)), "nt": ((1,), (1,)), "tn": ((0,), (0,))}


def _matmul(a, b, mode, out_dtype, name, tm=1024, tn=1024, tk=1024):
    if mode == "nn":
        (M, K), (_, N) = a.shape, b.shape
    elif mode == "nt":
        (M, K), (N, _) = a.shape, b.shape
    else:
        (K, M), (_, N) = a.shape, b.shape
    tm, tn, tk = _pick(M, tm), _pick(N, tn), _pick(K, tk)
    nk = K // tk
    dims = _MM_DIMS[mode]

    def body(a_ref, b_ref, o_ref, acc_ref):
        k = pl.program_id(2)
        p = _bdot(a_ref[...], b_ref[...], dims)

        @pl.when(k == 0)
        def _():
            acc_ref[...] = p

        @pl.when(k > 0)
        def _():
            acc_ref[...] += p

        @pl.when(k == nk - 1)
        def _():
            o_ref[...] = acc_ref[...].astype(out_dtype)

    a_spec = pl.BlockSpec((tk, tm), lambda i, j, k: (k, i)) if mode == "tn" else pl.BlockSpec((tm, tk), lambda i, j, k: (i, k))
    b_spec = pl.BlockSpec((tn, tk), lambda i, j, k: (j, k)) if mode == "nt" else pl.BlockSpec((tk, tn), lambda i, j, k: (k, j))
    return pl.pallas_call(
        body, name=name, grid=(M // tm, N // tn, nk),
        in_specs=[a_spec, b_spec], out_specs=pl.BlockSpec((tm, tn), lambda i, j, k: (i, j)),
        out_shape=jax.ShapeDtypeStruct((M, N), out_dtype),
        scratch_shapes=[pltpu.VMEM((tm, tn), F32)],
        compiler_params=_cparams(("parallel", "parallel", "arbitrary")),
    )(a, b)


def _row_spec(tr, d):
    return pl.BlockSpec((tr, d), lambda i: (i, 0))


def _vec_spec(d):
    return pl.BlockSpec((1, d), lambda i: (0, 0))


def _pre_fwd(x, gain, shift, scale, name):
    S, D = x.shape
    tr = _pick(S, 256)

    def body(x_ref, g_ref, sh_ref, sc_ref, h_ref):
        h_ref[...] = _modulate(x_ref[...], g_ref[...], sh_ref[...], sc_ref[...]).astype(BF16)

    return pl.pallas_call(
        body, name=name, grid=(S // tr,),
        in_specs=[_row_spec(tr, D), _vec_spec(D), _vec_spec(D), _vec_spec(D)],
        out_specs=_row_spec(tr, D), out_shape=jax.ShapeDtypeStruct((S, D), BF16),
        compiler_params=_cparams(("parallel",)),
    )(x, gain, shift, scale)


def _pre_bwd(x, gain, shift, scale, dh, dx_in, name):
    S, D = x.shape
    tr = _pick(S, 256)

    def body(x_ref, g_ref, sh_ref, sc_ref, dh_ref, dxin_ref, dx_ref, dg_ref, dsh_ref, dsc_ref):
        _, vjp = jax.vjp(_modulate, x_ref[...], g_ref[...], sh_ref[...], sc_ref[...])
        dx, dg, dsh, dsc = vjp(dh_ref[...])
        dx_ref[...] = dxin_ref[...] + dx

        @pl.when(pl.program_id(0) == 0)
        def _():
            dg_ref[...] = jnp.zeros_like(dg_ref)
            dsh_ref[...] = jnp.zeros_like(dsh_ref)
            dsc_ref[...] = jnp.zeros_like(dsc_ref)

        dg_ref[...] += dg
        dsh_ref[...] += dsh
        dsc_ref[...] += dsc

    vec = jax.ShapeDtypeStruct((1, D), F32)
    return pl.pallas_call(
        body, name=name, grid=(S // tr,),
        in_specs=[_row_spec(tr, D), _vec_spec(D), _vec_spec(D), _vec_spec(D), _row_spec(tr, D), _row_spec(tr, D)],
        out_specs=[_row_spec(tr, D), _vec_spec(D), _vec_spec(D), _vec_spec(D)],
        out_shape=[jax.ShapeDtypeStruct((S, D), F32), vec, vec, vec],
        compiler_params=_cparams(("arbitrary",)),
    )(x, gain, shift, scale, dh, dx_in)


def _residual_fwd(x, gate, f, coef, name):
    S, D = x.shape
    tr = _pick(S, 256)

    def body(x_ref, g_ref, f_ref, o_ref):
        o_ref[...] = x_ref[...] + coef * g_ref[...] * f_ref[...]

    return pl.pallas_call(
        body, name=name, grid=(S // tr,),
        in_specs=[_row_spec(tr, D), _vec_spec(D), _row_spec(tr, D)],
        out_specs=_row_spec(tr, D), out_shape=jax.ShapeDtypeStruct((S, D), F32),
        compiler_params=_cparams(("parallel",)),
    )(x, gate, f)


def _residual_bwd(gate, f, dxn, coef, name):
    S, D = f.shape
    tr = _pick(S, 256)

    def body(g_ref, f_ref, d_ref, df_ref, dg_ref):
        d = d_ref[...]
        df_ref[...] = (coef * g_ref[...] * d).astype(BF16)

        @pl.when(pl.program_id(0) == 0)
        def _():
            dg_ref[...] = jnp.zeros_like(dg_ref)

        dg_ref[...] += jnp.sum(coef * f_ref[...] * d, axis=0, keepdims=True)

    return pl.pallas_call(
        body, name=name, grid=(S // tr,),
        in_specs=[_vec_spec(D), _row_spec(tr, D), _row_spec(tr, D)],
        out_specs=[_row_spec(tr, D), _vec_spec(D)],
        out_shape=[jax.ShapeDtypeStruct((S, D), BF16), jax.ShapeDtypeStruct((1, D), F32)],
        compiler_params=_cparams(("arbitrary",)),
    )(gate, f, dxn)


def _swiglu_fn(a, b):
    return _silu(a) * b


def _swiglu_fwd(ab, fs, name):
    S, F2 = ab.shape
    tr = _pick(S, 256)

    def body(ab_ref, s_ref):
        s_ref[...] = _swiglu_fn(ab_ref[:, :fs], ab_ref[:, fs:]).astype(BF16)

    return pl.pallas_call(
        body, name=name, grid=(S // tr, F2 // (2 * fs)),
        in_specs=[pl.BlockSpec((tr, 2 * fs), lambda i, j: (i, j))],
        out_specs=pl.BlockSpec((tr, fs), lambda i, j: (i, j)),
        out_shape=jax.ShapeDtypeStruct((S, F2 // 2), BF16),
        compiler_params=_cparams(("parallel", "parallel")),
    )(ab)


def _swiglu_bwd(ab, ds, fs, name):
    S, F2 = ab.shape
    tr = _pick(S, 256)

    def body(ab_ref, ds_ref, dab_ref):
        _, vjp = jax.vjp(_swiglu_fn, ab_ref[:, :fs], ab_ref[:, fs:])
        da, db = vjp(ds_ref[...])
        dab_ref[:, :fs] = da.astype(BF16)
        dab_ref[:, fs:] = db.astype(BF16)

    return pl.pallas_call(
        body, name=name, grid=(S // tr, F2 // (2 * fs)),
        in_specs=[pl.BlockSpec((tr, 2 * fs), lambda i, j: (i, j)), pl.BlockSpec((tr, fs), lambda i, j: (i, j))],
        out_specs=pl.BlockSpec((tr, 2 * fs), lambda i, j: (i, j)),
        out_shape=jax.ShapeDtypeStruct((S, F2), BF16),
        compiler_params=_cparams(("parallel", "parallel")),
    )(ab, ds)


def _loss_head(y, target, name):
    S, D = y.shape
    tr = _pick(S, 256)

    def body(y_ref, t_ref, l_ref, dy_ref):
        e = y_ref[...] - t_ref[...]
        dy_ref[...] = e * (1.0 / D)

        @pl.when(pl.program_id(0) == 0)
        def _():
            l_ref[...] = jnp.zeros_like(l_ref)

        l_ref[...] += jnp.sum(jnp.sum(e * e, axis=-1, keepdims=True), axis=0, keepdims=True) * (0.5 / D)

    return pl.pallas_call(
        body, name=name, grid=(S // tr,),
        in_specs=[_row_spec(tr, D), _row_spec(tr, D)],
        out_specs=[pl.BlockSpec((1, 1), lambda i: (0, 0)), _row_spec(tr, D)],
        out_shape=[jax.ShapeDtypeStruct((1, 1), F32), jax.ShapeDtypeStruct((S, D), F32)],
        compiler_params=_cparams(("arbitrary",)),
    )(y, target)


def _rope(y, cc, s1, s2):
    return y * cc + pltpu.roll(y, LANES - ROPE_DIM // 2, 1) * s1 + pltpu.roll(y, ROPE_DIM // 2, 1) * s2


def _rope_t(d, cc, s1, s2):
    return d * cc + pltpu.roll(d * s1, ROPE_DIM // 2, 1) + pltpu.roll(d * s2, LANES - ROPE_DIM // 2, 1)


def _head_spec(tr, col0):
    return pl.BlockSpec((tr, HEAD_DIM), lambda i, h: (i, col0 + h))


def _tab_spec(tr):
    return pl.BlockSpec((tr, HEAD_DIM), lambda i, h: (i, 0))


def _gain_spec():
    return pl.BlockSpec((1, HEAD_DIM), lambda i, h: (0, 0))


def _attn_prep_fwd(proj, q_gain, k_gain, cc, s1, s2, name):
    S = proj.shape[0]
    H = N_ATTN_HEADS
    tr = _pick(S, 512)

    def body(q_ref, k_ref, qg_ref, kg_ref, cc_ref, s1_ref, s2_ref, qo_ref, ko_ref):
        cc, s1, s2 = cc_ref[...], s1_ref[...], s2_ref[...]
        qo_ref[...] = _rope(_rms(q_ref[...], qg_ref[...]), cc, s1, s2).astype(BF16)
        ko_ref[...] = _rope(_rms(k_ref[...], kg_ref[...]), cc, s1, s2).astype(BF16)

    out = jax.ShapeDtypeStruct((S, H * HEAD_DIM), BF16)
    return pl.pallas_call(
        body, name=name, grid=(S // tr, H),
        in_specs=[_head_spec(tr, 0), _head_spec(tr, H), _gain_spec(), _gain_spec(), _tab_spec(tr), _tab_spec(tr), _tab_spec(tr)],
        out_specs=[_head_spec(tr, 0), _head_spec(tr, 0)], out_shape=[out, out],
        compiler_params=_cparams(("parallel", "parallel")),
    )(proj, proj, q_gain, k_gain, cc, s1, s2)


def _attn_prep_bwd(proj, q_gain, k_gain, cc, s1, s2, dq, dk, name):
    S = proj.shape[0]
    H = N_ATTN_HEADS
    tr = _pick(S, 512)

    def body(q_ref, k_ref, qg_ref, kg_ref, cc_ref, s1_ref, s2_ref, dq_ref, dk_ref, dpq_ref, dpk_ref, dqg_ref, dkg_ref):
        cc, s1, s2 = cc_ref[...], s1_ref[...], s2_ref[...]

        @pl.when((pl.program_id(0) == 0) & (pl.program_id(1) == 0))
        def _():
            dqg_ref[...] = jnp.zeros_like(dqg_ref)
            dkg_ref[...] = jnp.zeros_like(dkg_ref)

        _, vjp_q = jax.vjp(_rms, q_ref[...], qg_ref[...])
        dxq, dgq = vjp_q(_rope_t(dq_ref[...], cc, s1, s2))
        _, vjp_k = jax.vjp(_rms, k_ref[...], kg_ref[...])
        dxk, dgk = vjp_k(_rope_t(dk_ref[...], cc, s1, s2))
        dpq_ref[...] = dxq.astype(BF16)
        dpk_ref[...] = dxk.astype(BF16)
        dqg_ref[...] += dgq
        dkg_ref[...] += dgk

    out = jax.ShapeDtypeStruct((S, H * HEAD_DIM), BF16)
    gout = jax.ShapeDtypeStruct((1, HEAD_DIM), F32)
    return pl.pallas_call(
        body, name=name, grid=(S // tr, H),
        in_specs=[_head_spec(tr, 0), _head_spec(tr, H), _gain_spec(), _gain_spec(), _tab_spec(tr), _tab_spec(tr), _tab_spec(tr),
                  _head_spec(tr, 0), _head_spec(tr, 0)],
        out_specs=[_head_spec(tr, 0), _head_spec(tr, 0), _gain_spec(), _gain_spec()], out_shape=[out, out, gout, gout],
        compiler_params=_cparams(("arbitrary", "arbitrary")),
    )(proj, proj, q_gain, k_gain, cc, s1, s2, dq, dk)


def _multiplicity(j, t):
    ti = lax.broadcasted_iota(jnp.int32, (t, t), 0)
    si = lax.broadcasted_iota(jnp.int32, (t, t), 1)
    delta = j * t + ti - si
    cnt = jnp.zeros((t, t), F32)
    for window, dil in DILATED_PATTERNS:
        ok = (delta >= 0) & ((delta & (dil - 1)) == 0) & (delta <= window)
        cnt = cnt + ok.astype(F32)
    return cnt


_NEG = -1e30


def _attn_fwd(q, k, proj, v_col0, name):
    S = q.shape[0]
    H = N_ATTN_HEADS
    t = _pick(S, ATTN_TILE)
    nq = S // t
    nj = MAX_WINDOW // t + 1
    scale = HEAD_DIM ** -0.5

    def body(q_ref, k_ref, v_ref, o_ref, ob_ref, lse_ref, m_sc, l_sc, acc_sc):
        qb, j = pl.program_id(1), pl.program_id(2)

        @pl.when(j == 0)
        def _():
            m_sc[...] = jnp.full_like(m_sc, _NEG)
            l_sc[...] = jnp.zeros_like(l_sc)
            acc_sc[...] = jnp.zeros_like(acc_sc)

        @pl.when(qb - j >= 0)
        def _():
            cnt = _multiplicity(j, t)
            s = _bdot(q_ref[...], k_ref[...], ((1,), (1,))) * scale
            s = jnp.where(cnt > 0.0, s, _NEG)
            m_prev = m_sc[...]
            m_new = jnp.maximum(m_prev, jnp.max(s, axis=-1, keepdims=True))
            alpha = jnp.exp(m_prev - m_new)
            p = cnt * jnp.exp(s - m_new)
            l_sc[...] = alpha * l_sc[...] + jnp.sum(p, axis=-1, keepdims=True)
            acc_sc[...] = alpha * acc_sc[...] + _bdot(p, v_ref[...], ((1,), (0,)))
            m_sc[...] = m_new

        @pl.when(j == nj - 1)
        def _():
            o = acc_sc[...] / l_sc[...]
            o_ref[...] = o
            ob_ref[...] = o.astype(BF16)
            lse_ref[...] = jnp.broadcast_to(m_sc[...] + jnp.log(l_sc[...]), (t, HEAD_DIM))

    qspec = pl.BlockSpec((t, HEAD_DIM), lambda h, i, j: (i, h))
    kspec = pl.BlockSpec((t, HEAD_DIM), lambda h, i, j: (jnp.maximum(i - j, 0), h))
    vspec = pl.BlockSpec((t, HEAD_DIM), lambda h, i, j: (jnp.maximum(i - j, 0), v_col0 + h))
    return pl.pallas_call(
        body, name=name, grid=(H, nq, nj),
        in_specs=[qspec, kspec, vspec], out_specs=[qspec, qspec, qspec],
        out_shape=[jax.ShapeDtypeStruct((S, H * HEAD_DIM), F32), jax.ShapeDtypeStruct((S, H * HEAD_DIM), BF16),
                   jax.ShapeDtypeStruct((S, H * HEAD_DIM), F32)],
        scratch_shapes=[pltpu.VMEM((t, 1), F32), pltpu.VMEM((t, 1), F32), pltpu.VMEM((t, HEAD_DIM), F32)],
        compiler_params=_cparams(("parallel", "parallel", "arbitrary")),
    )(q, k, proj)


def _attn_probs(q, k, lse, j, t, scale):
    cnt = _multiplicity(j, t)
    s = _bdot(q, k, ((1,), (1,))) * scale
    s = jnp.where(cnt > 0.0, s, _NEG)
    return cnt * jnp.exp(s - lse)


def _attn_bwd_dq(q, k, proj, v_col0, o, lse, do, do_col0, name):
    S = q.shape[0]
    H = N_ATTN_HEADS
    t = _pick(S, ATTN_TILE)
    nq = S // t
    nj = MAX_WINDOW // t + 1
    scale = HEAD_DIM ** -0.5

    def body(q_ref, k_ref, v_ref, o_ref, lse_ref, do_ref, dq_ref, acc_sc):
        qb, j = pl.program_id(1), pl.program_id(2)

        @pl.when(j == 0)
        def _():
            acc_sc[...] = jnp.zeros_like(acc_sc)

        @pl.when(qb - j >= 0)
        def _():
            do = do_ref[...]
            dsum = jnp.sum(do * o_ref[...], axis=-1, keepdims=True)
            lse = jnp.max(lse_ref[...], axis=-1, keepdims=True)
            p = _attn_probs(q_ref[...], k_ref[...], lse, j, t, scale)
            dp = _bdot(do, v_ref[...], ((1,), (1,)))
            ds = p * (dp - dsum)
            acc_sc[...] += _bdot(ds, k_ref[...], ((1,), (0,))) * scale

        @pl.when(j == nj - 1)
        def _():
            dq_ref[...] = acc_sc[...]

    qspec = pl.BlockSpec((t, HEAD_DIM), lambda h, i, j: (i, h))
    dospec = pl.BlockSpec((t, HEAD_DIM), lambda h, i, j: (i, do_col0 + h))
    kspec = pl.BlockSpec((t, HEAD_DIM), lambda h, i, j: (jnp.maximum(i - j, 0), h))
    vspec = pl.BlockSpec((t, HEAD_DIM), lambda h, i, j: (jnp.maximum(i - j, 0), v_col0 + h))
    return pl.pallas_call(
        body, name=name, grid=(H, nq, nj),
        in_specs=[qspec, kspec, vspec, qspec, qspec, dospec], out_specs=qspec,
        out_shape=jax.ShapeDtypeStruct((S, H * HEAD_DIM), F32),
        scratch_shapes=[pltpu.VMEM((t, HEAD_DIM), F32)],
        compiler_params=_cparams(("parallel", "parallel", "arbitrary")),
    )(q, k, proj, o, lse, do)


def _attn_bwd_dkv(q, k, proj, v_col0, o, lse, do, do_col0, name):
    S = q.shape[0]
    H = N_ATTN_HEADS
    t = _pick(S, ATTN_TILE)
    nq = S // t
    nj = MAX_WINDOW // t + 1
    scale = HEAD_DIM ** -0.5

    def body(q_ref, k_ref, v_ref, o_ref, lse_ref, do_ref, dk_ref, dv_ref, dk_sc, dv_sc):
        kb, j = pl.program_id(1), pl.program_id(2)

        @pl.when(j == 0)
        def _():
            dk_sc[...] = jnp.zeros_like(dk_sc)
            dv_sc[...] = jnp.zeros_like(dv_sc)

        @pl.when(kb + j < nq)
        def _():
            do = do_ref[...]
            dsum = jnp.sum(do * o_ref[...], axis=-1, keepdims=True)
            lse = jnp.max(lse_ref[...], axis=-1, keepdims=True)
            p = _attn_probs(q_ref[...], k_ref[...], lse, j, t, scale)
            dp = _bdot(do, v_ref[...], ((1,), (1,)))
            ds = p * (dp - dsum)
            dv_sc[...] += _bdot(p, do, ((0,), (0,)))
            dk_sc[...] += _bdot(ds, q_ref[...], ((0,), (0,))) * scale

        @pl.when(j == nj - 1)
        def _():
            dk_ref[...] = dk_sc[...]
            dv_ref[...] = dv_sc[...].astype(BF16)

    def qrow(h, i, j):
        return jnp.minimum(i + j, nq - 1)

    qspec = pl.BlockSpec((t, HEAD_DIM), lambda h, i, j: (qrow(h, i, j), h))
    dospec = pl.BlockSpec((t, HEAD_DIM), lambda h, i, j: (qrow(h, i, j), do_col0 + h))
    kspec = pl.BlockSpec((t, HEAD_DIM), lambda h, i, j: (i, h))
    vspec = pl.BlockSpec((t, HEAD_DIM), lambda h, i, j: (i, v_col0 + h))
    return pl.pallas_call(
        body, name=name, grid=(H, nq, nj),
        in_specs=[qspec, kspec, vspec, qspec, qspec, dospec], out_specs=[kspec, kspec],
        out_shape=[jax.ShapeDtypeStruct((S, H * HEAD_DIM), F32), jax.ShapeDtypeStruct((S, H * HEAD_DIM), BF16)],
        scratch_shapes=[pltpu.VMEM((t, HEAD_DIM), F32), pltpu.VMEM((t, HEAD_DIM), F32)],
        compiler_params=_cparams(("parallel", "parallel", "arbitrary")),
    )(q, k, proj, o, lse, do)


def _conv_pre(x_ref, w_ref):
    x = x_ref[...]
    rows = lax.broadcasted_iota(jnp.int32, x.shape, 0)
    shifted = [x]
    acc = x * w_ref[pl.ds(CONV_WIDTH - 1, 1), :]
    for sft in range(1, CONV_WIDTH):
        xs = jnp.where(rows >= sft, pltpu.roll(x, sft, 0), 0.0)
        shifted.append(xs)
        acc = acc + xs * w_ref[pl.ds(CONV_WIDTH - 1 - sft, 1), :]
    return acc, shifted


def _conv_fwd(proj, col0, width, w, name):
    S = proj.shape[0]

    def body(x_ref, w_ref, y_ref):
        acc, _ = _conv_pre(x_ref, w_ref)
        y_ref[...] = _silu(acc)

    return pl.pallas_call(
        body, name=name, grid=(width // LANES,),
        in_specs=[pl.BlockSpec((S, LANES), lambda c: (0, col0 + c)), pl.BlockSpec((CONV_WIDTH, LANES), lambda c: (0, c))],
        out_specs=pl.BlockSpec((S, LANES), lambda c: (0, c)),
        out_shape=jax.ShapeDtypeStruct((S, width), F32),
        compiler_params=_cparams(("parallel",)),
    )(proj, w)


def _conv_bwd(proj, col0, width, w, dy, name):
    S = proj.shape[0]

    def body(x_ref, w_ref, d_ref, dx_ref, dw_ref):
        acc, shifted = _conv_pre(x_ref, w_ref)
        sig = _sigmoid(acc)
        da = d_ref[...] * (sig * (1.0 + acc * (1.0 - sig)))
        rows = lax.broadcasted_iota(jnp.int32, da.shape, 0)
        dx = da * w_ref[pl.ds(CONV_WIDTH - 1, 1), :]
        dw_ref[pl.ds(CONV_WIDTH - 1, 1), :] = jnp.sum(da * shifted[0], axis=0, keepdims=True)
        for sft in range(1, CONV_WIDTH):
            back = jnp.where(rows < S - sft, pltpu.roll(da, S - sft, 0), 0.0)
            dx = dx + back * w_ref[pl.ds(CONV_WIDTH - 1 - sft, 1), :]
            dw_ref[pl.ds(CONV_WIDTH - 1 - sft, 1), :] = jnp.sum(da * shifted[sft], axis=0, keepdims=True)
        dx_ref[...] = dx.astype(BF16)

    return pl.pallas_call(
        body, name=name, grid=(width // LANES,),
        in_specs=[pl.BlockSpec((S, LANES), lambda c: (0, col0 + c)), pl.BlockSpec((CONV_WIDTH, LANES), lambda c: (0, c)),
                  pl.BlockSpec((S, LANES), lambda c: (0, c))],
        out_specs=[pl.BlockSpec((S, LANES), lambda c: (0, c)), pl.BlockSpec((CONV_WIDTH, LANES), lambda c: (0, c))],
        out_shape=[jax.ShapeDtypeStruct((S, width), BF16), jax.ShapeDtypeStruct((CONV_WIDTH, width), F32)],
        compiler_params=_cparams(("parallel",)),
    )(proj, w, dy)


PREP_CHUNKS = 4


def _chunk_prep(qraw, kraw, v, ab, alog_row, dtb_row, mask_g, mask_b, t_saved=None):
    c = qraw.shape[0]
    mm_nt, mm_nn = (_mm_nt, _mm_nn) if t_saved is not None else (lambda p, r: _bdot(p, r, _NT), lambda p, r: _bdot(p, r, _NN))
    q = _l2(qraw) * (HEAD_DIM ** -0.5)
    k = _l2(kraw)
    gfull = -jnp.exp(alog_row) * _softplus(ab + dtb_row)
    bfull = _sigmoid(ab)
    g = jnp.sum(jnp.where(mask_g, gfull, 0.0), axis=-1, keepdims=True)
    beta = jnp.sum(jnp.where(mask_b, bfull, 0.0), axis=-1, keepdims=True)
    row = lax.broadcasted_iota(jnp.int32, (c, c), 0)
    col = lax.broadcasted_iota(jnp.int32, (c, c), 1)
    tril, strict, eye = row >= col, row > col, row == col
    g_row = jnp.sum(jnp.where(eye, g, 0.0), axis=0, keepdims=True)
    gc_col = jnp.sum(jnp.where(tril, g_row, 0.0), axis=1, keepdims=True)
    gc_row = jnp.sum(jnp.where(row <= col, g, 0.0), axis=0, keepdims=True)
    gamma = jnp.where(tril, jnp.exp(jnp.where(tril, gc_col - gc_row, 0.0)), 0.0)
    kb = k * beta
    a = jnp.where(strict, mm_nt(kb, k) * gamma, 0.0)
    t_inv = _tri_inv(a, eye.astype(F32)) if t_saved is None else _tri_inv_saved(a, t_saved)
    egc = jnp.exp(gc_col)
    u = mm_nn(t_inv, v * beta)
    w = mm_nn(t_inv, kb * egc)
    qd = q * egc
    g_last = jnp.sum(g, axis=0, keepdims=True)
    kt = k * jnp.exp(g_last - gc_col)
    dec = jnp.broadcast_to(jnp.exp(g_last), (1, HEAD_DIM))
    intra = mm_nt(q, k) * gamma
    if t_saved is None:
        return u, w, qd, kt, intra, dec, t_inv
    return u, w, qd, kt, intra, dec


def _lane_masks(h):
    lane = lax.broadcasted_iota(jnp.int32, (1, LANES), 1)
    return lane == h, lane == N_DELTA_HEADS + h


def _prep_specs(tr, ab_col):
    H = N_DELTA_HEADS
    return [
        pl.BlockSpec((tr, HEAD_DIM), lambda i, h: (i, h)),
        pl.BlockSpec((tr, HEAD_DIM), lambda i, h: (i, H + h)),
        pl.BlockSpec((tr, HEAD_DIM), lambda i, h: (i, 2 * H + h)),
        pl.BlockSpec((tr, LANES), lambda i, h: (i, ab_col)),
        pl.BlockSpec((1, LANES), lambda i, h: (0, 0)),
        pl.BlockSpec((1, LANES), lambda i, h: (0, 0)),
    ]


def _prep_out_specs(tr):
    nc = tr // CHUNK
    hs = pl.BlockSpec((tr, HEAD_DIM), lambda i, h: (i, h))
    return [hs, hs, hs, hs,
            pl.BlockSpec((None, tr, CHUNK), lambda i, h: (h, i, 0)),
            pl.BlockSpec((None, nc, 1, HEAD_DIM), lambda i, h: (h, i, 0, 0)),
            pl.BlockSpec((None, tr, CHUNK), lambda i, h: (h, i, 0))]


def _prep_out_shapes(S):
    H = N_DELTA_HEADS
    hs = jax.ShapeDtypeStruct((S, H * HEAD_DIM), F32)
    sq = jax.ShapeDtypeStruct((H, S, CHUNK), F32)
    return [hs, hs, hs, hs, sq, jax.ShapeDtypeStruct((H, S // CHUNK, 1, HEAD_DIM), F32), sq]


def _delta_prep_fwd(dqkv, proj, ab_col, alog_row, dtb_row, name):
    S = dqkv.shape[0]
    tr = min(S, PREP_CHUNKS * CHUNK)
    nc = tr // CHUNK

    def body(q_ref, k_ref, v_ref, ab_ref, al_ref, dt_ref, u_ref, w_ref, qd_ref, kt_ref, in_ref, dec_ref, ti_ref):
        mask_g, mask_b = _lane_masks(pl.program_id(1))
        for ci in range(nc):
            rs = pl.ds(ci * CHUNK, CHUNK)
            u, w, qd, kt, intra, dec, t_inv = _chunk_prep(q_ref[rs, :], k_ref[rs, :], v_ref[rs, :], ab_ref[rs, :],
                                                          al_ref[...], dt_ref[...], mask_g, mask_b)
            u_ref[rs, :] = u
            w_ref[rs, :] = w
            qd_ref[rs, :] = qd
            kt_ref[rs, :] = kt
            in_ref[rs, :] = intra
            dec_ref[ci] = dec
            ti_ref[rs, :] = t_inv

    return pl.pallas_call(
        body, name=name, grid=(S // tr, N_DELTA_HEADS),
        in_specs=_prep_specs(tr, ab_col), out_specs=_prep_out_specs(tr), out_shape=_prep_out_shapes(S),
        compiler_params=_cparams(("parallel", "parallel")),
    )(dqkv, dqkv, dqkv, proj, alog_row, dtb_row)


def _delta_prep_bwd(dqkv, proj, ab_col, alog_row, dtb_row, cots, t_inv, name):
    S = dqkv.shape[0]
    H = N_DELTA_HEADS
    tr = min(S, PREP_CHUNKS * CHUNK)
    nc = tr // CHUNK

    def body(q_ref, k_ref, v_ref, ab_ref, al_ref, dt_ref, du_ref, dw_ref, dqd_ref, dkt_ref, din_ref, ddec_ref, ti_ref,
             dq_ref, dk_ref, dv_ref, dab_ref, dal_ref, ddt_ref, dab_sc):
        h = pl.program_id(1)
        mask_g, mask_b = _lane_masks(h)

        @pl.when((pl.program_id(0) == 0) & (h == 0))
        def _():
            dal_ref[...] = jnp.zeros_like(dal_ref)
            ddt_ref[...] = jnp.zeros_like(ddt_ref)

        @pl.when(h == 0)
        def _():
            dab_sc[...] = jnp.zeros_like(dab_sc)

        for ci in range(nc):
            rs = pl.ds(ci * CHUNK, CHUNK)
            fn = functools.partial(_chunk_prep, mask_g=mask_g, mask_b=mask_b, t_saved=ti_ref[rs, :])
            _, vjp = jax.vjp(fn, q_ref[rs, :], k_ref[rs, :], v_ref[rs, :], ab_ref[rs, :], al_ref[...], dt_ref[...])
            dq, dk, dv, dab, dal, ddt = vjp((du_ref[rs, :], dw_ref[rs, :], dqd_ref[rs, :], dkt_ref[rs, :],
                                             din_ref[rs, :], ddec_ref[ci]))
            dq_ref[rs, :] = dq
            dk_ref[rs, :] = dk
            dv_ref[rs, :] = dv
            dab_sc[rs, :] += dab
            dal_ref[...] += dal
            ddt_ref[...] += ddt

        @pl.when(h == H - 1)
        def _():
            dab_ref[...] = dab_sc[...].astype(BF16)

    hs = pl.BlockSpec((tr, HEAD_DIM), lambda i, h: (i, h))
    hshape = jax.ShapeDtypeStruct((S, H * HEAD_DIM), F32)
    row = pl.BlockSpec((1, LANES), lambda i, h: (0, 0))
    rshape = jax.ShapeDtypeStruct((1, LANES), F32)
    return pl.pallas_call(
        body, name=name, grid=(S // tr, H),
        in_specs=_prep_specs(tr, ab_col) + _prep_out_specs(tr),
        out_specs=[hs, hs, hs, pl.BlockSpec((tr, LANES), lambda i, h: (i, 0)), row, row],
        out_shape=[hshape, hshape, hshape, jax.ShapeDtypeStruct((S, LANES), BF16), rshape, rshape],
        scratch_shapes=[pltpu.VMEM((tr, LANES), F32)],
        compiler_params=_cparams(("arbitrary", "arbitrary")),
    )(dqkv, dqkv, dqkv, proj, alog_row, dtb_row, *cots, t_inv)


def _scan_step(state, u, w, qd, kt, intra, dec):
    v_new = u - _dot3(w, state)
    o = _dot3(qd, state) + _dot3(intra, v_new)
    return o, state * dec + _dot3(kt, v_new, _TN)


def _scan_specs(rev, n):
    H = N_DELTA_HEADS

    def cix(i):
        return (n - 1 - i) if rev else i

    row = pl.BlockSpec((CHUNK, H * HEAD_DIM), lambda i: (cix(i), 0))
    return row, pl.BlockSpec((H, CHUNK, CHUNK), lambda i: (0, cix(i), 0)), \
        pl.BlockSpec((H, 1, 1, HEAD_DIM), lambda i: (0, cix(i), 0, 0)), \
        pl.BlockSpec((1, H, HEAD_DIM, HEAD_DIM), lambda i: (cix(i), 0, 0, 0))


def _delta_scan_fwd(u, w, qd, kt, intra, dec, name):
    S = u.shape[0]
    H = N_DELTA_HEADS
    n = S // CHUNK
    row, ispec, dspec, sspec = _scan_specs(False, n)

    def body(u_ref, w_ref, qd_ref, kt_ref, in_ref, dec_ref, o_ref, st_ref, s_sc):
        @pl.when(pl.program_id(0) == 0)
        def _():
            s_sc[...] = jnp.zeros_like(s_sc)

        for h in range(H):
            cs = pl.ds(h * HEAD_DIM, HEAD_DIM)
            state = s_sc[h]
            st_ref[0, h] = state
            o, new = _scan_step(state, u_ref[:, cs], w_ref[:, cs], qd_ref[:, cs], kt_ref[:, cs], in_ref[h], dec_ref[h, 0])
            o_ref[:, cs] = o
            s_sc[h] = new

    return pl.pallas_call(
        body, name=name, grid=(n,),
        in_specs=[row, row, row, row, ispec, dspec], out_specs=[row, sspec],
        out_shape=[jax.ShapeDtypeStruct((S, H * HEAD_DIM), F32), jax.ShapeDtypeStruct((n, H, HEAD_DIM, HEAD_DIM), F32)],
        scratch_shapes=[pltpu.VMEM((H, HEAD_DIM, HEAD_DIM), F32)],
        compiler_params=_cparams(("arbitrary",)),
    )(u, w, qd, kt, intra, dec)


def _delta_scan_bwd(u, w, qd, kt, intra, dec, states, do, name):
    S = u.shape[0]
    H = N_DELTA_HEADS
    n = S // CHUNK
    row, ispec, dspec, sspec = _scan_specs(True, n)

    def body(u_ref, w_ref, qd_ref, kt_ref, in_ref, dec_ref, st_ref, do_ref,
             du_ref, dw_ref, dqd_ref, dkt_ref, din_ref, ddec_ref, ds_sc):
        @pl.when(pl.program_id(0) == 0)
        def _():
            ds_sc[...] = jnp.zeros_like(ds_sc)

        for h in range(H):
            cs = pl.ds(h * HEAD_DIM, HEAD_DIM)
            _, vjp = jax.vjp(_scan_step, st_ref[0, h], u_ref[:, cs], w_ref[:, cs], qd_ref[:, cs], kt_ref[:, cs],
                             in_ref[h], dec_ref[h, 0])
            dstate, du, dw, dqd, dkt, din, ddec = vjp((do_ref[:, cs], ds_sc[h]))
            du_ref[:, cs] = du
            dw_ref[:, cs] = dw
            dqd_ref[:, cs] = dqd
            dkt_ref[:, cs] = dkt
            din_ref[h] = din
            ddec_ref[h, 0] = ddec
            ds_sc[h] = dstate

    hshape = jax.ShapeDtypeStruct((S, H * HEAD_DIM), F32)
    return pl.pallas_call(
        body, name=name, grid=(n,),
        in_specs=[row, row, row, row, ispec, dspec, sspec, row],
        out_specs=[row, row, row, row, ispec, dspec],
        out_shape=[hshape, hshape, hshape, hshape, jax.ShapeDtypeStruct((H, S, CHUNK), F32),
                   jax.ShapeDtypeStruct((H, n, 1, HEAD_DIM), F32)],
        scratch_shapes=[pltpu.VMEM((H, HEAD_DIM, HEAD_DIM), F32)],
        compiler_params=_cparams(("arbitrary",)),
    )(u, w, qd, kt, intra, dec, states, do)


def _gated_norm(od, z, gain):
    return _rms(od, gain) * _silu(z)


def _post_fwd(od, proj, z_col0, gain, name):
    S = od.shape[0]
    H = N_DELTA_HEADS
    tr = _pick(S, 512)

    def body(od_ref, z_ref, g_ref, o_ref):
        o_ref[...] = _gated_norm(od_ref[...], z_ref[...], g_ref[...]).astype(BF16)

    return pl.pallas_call(
        body, name=name, grid=(S // tr, H),
        in_specs=[_head_spec(tr, 0), _head_spec(tr, z_col0), _gain_spec()],
        out_specs=_head_spec(tr, 0), out_shape=jax.ShapeDtypeStruct((S, H * HEAD_DIM), BF16),
        compiler_params=_cparams(("parallel", "parallel")),
    )(od, proj, gain)


def _post_bwd(od, proj, z_col0, gain, do, do_col0, name):
    S = od.shape[0]
    H = N_DELTA_HEADS
    tr = _pick(S, 512)

    def body(od_ref, z_ref, g_ref, do_ref, dod_ref, dz_ref, dg_ref):
        @pl.when((pl.program_id(0) == 0) & (pl.program_id(1) == 0))
        def _():
            dg_ref[...] = jnp.zeros_like(dg_ref)

        _, vjp = jax.vjp(_gated_norm, od_ref[...], z_ref[...], g_ref[...])
        dod, dz, dg = vjp(do_ref[...])
        dod_ref[...] = dod
        dz_ref[...] = dz.astype(BF16)
        dg_ref[...] += dg

    return pl.pallas_call(
        body, name=name, grid=(S // tr, H),
        in_specs=[_head_spec(tr, 0), _head_spec(tr, z_col0), _gain_spec(), _head_spec(tr, do_col0)],
        out_specs=[_head_spec(tr, 0), _head_spec(tr, 0), _gain_spec()],
        out_shape=[jax.ShapeDtypeStruct((S, H * HEAD_DIM), F32), jax.ShapeDtypeStruct((S, H * HEAD_DIM), BF16),
                   jax.ShapeDtypeStruct((1, HEAD_DIM), F32)],
        compiler_params=_cparams(("arbitrary", "arbitrary")),
    )(od, proj, gain, do)


def _adam_math(w, g, m, v):
    m = ADAM_B1 * m + (1.0 - ADAM_B1) * g
    v = ADAM_B2 * v + (1.0 - ADAM_B2) * (g * g)
    m_hat = m / (1.0 - ADAM_B1 ** ADAM_STEP)
    v_hat = v / (1.0 - ADAM_B2 ** ADAM_STEP)
    delta = -ADAM_LR * (m_hat / (jnp.sqrt(v_hat) + ADAM_EPS) + ADAM_WD * w)
    return delta, m, v


def _adamw(w, g, m, v, name):
    R, C = w.shape
    tr = R if R * C * 4 <= (1 << 20) else _pick8(R, max(8, (1 << 20) // (C * 4)))

    def body(w_ref, g_ref, m_ref, v_ref, d_ref, nm_ref, nv_ref):
        d, nm, nv = _adam_math(w_ref[...], g_ref[...], m_ref[...], v_ref[...])
        d_ref[...] = d
        nm_ref[...] = nm
        nv_ref[...] = nv

    spec = pl.BlockSpec((tr, C), lambda i: (i, 0))
    shp = jax.ShapeDtypeStruct((R, C), F32)
    return pl.pallas_call(
        body, name=name, grid=(R // tr,), in_specs=[spec] * 4, out_specs=[spec] * 3, out_shape=[shp] * 3,
        compiler_params=_cparams(("parallel",)),
    )(w, g, m, v)


def _pick8(dim, pref):
    t = (min(dim, pref) // 8) * 8
    while t >= 8:
        if dim % t == 0:
            return t
        t -= 8
    return dim


def _adamw_outer(w, m, v, cond_t, rhs, name):
    R, C = w.shape
    tr = _pick8(R, 128)
    nb = cond_t.shape[1]
    lhs_t = cond_t

    def body(w_ref, m_ref, v_ref, a_ref, b_ref, g_ref, d_ref, nm_ref, nv_ref):
        g = _dot(_silu(a_ref[...]), b_ref[...])
        d, nm, nv = _adam_math(w_ref[...], g, m_ref[...], v_ref[...])
        g_ref[...] = g
        d_ref[...] = d
        nm_ref[...] = nm
        nv_ref[...] = nv

    spec = pl.BlockSpec((tr, C), lambda i: (i, 0))
    shp = jax.ShapeDtypeStruct((R, C), F32)
    return pl.pallas_call(
        body, name=name, grid=(R // tr,),
        in_specs=[spec, spec, spec, pl.BlockSpec((tr, nb), lambda i: (i, 0)), pl.BlockSpec((nb, C), lambda i: (0, 0))],
        out_specs=[spec] * 4, out_shape=[shp] * 4,
        compiler_params=_cparams(("parallel",)),
    )(w, m, v, lhs_t, rhs)


def _ada_fwd(cond, w, bias, name):
    a = cond
    nb, K = a.shape
    N = w.shape[1]
    tn = _pick(N, 512)

    def body(a_ref, w_ref, b_ref, o_ref):
        o_ref[...] = _dot(_silu(a_ref[...]), w_ref[...]) + b_ref[...]

    return pl.pallas_call(
        body, name=name, grid=(N // tn,),
        in_specs=[pl.BlockSpec((nb, K), lambda j: (0, 0)), pl.BlockSpec((K, tn), lambda j: (0, j)), pl.BlockSpec((1, tn), lambda j: (0, j))],
        out_specs=pl.BlockSpec((nb, tn), lambda j: (0, j)), out_shape=jax.ShapeDtypeStruct((nb, N), F32),
        compiler_params=_cparams(("parallel",)),
    )(a, w, bias)


def _add_cast(parts, out_dtypes, name):
    shape = parts[0].shape
    G, R, C = shape
    tr = _pick8(R, max(8, (1 << 20) // (C * 4)))
    n_in = len(parts)

    def body(*refs):
        acc = refs[0][...].astype(F32)
        for r in refs[1:n_in]:
            acc = acc + r[...].astype(F32)
        for o, dt in zip(refs[n_in:], out_dtypes):
            o[...] = acc.astype(dt)

    spec = pl.BlockSpec((1, tr, C), lambda g, i: (g, i, 0))
    outs = pl.pallas_call(
        body, name=name, grid=(G, R // tr), in_specs=[spec] * n_in, out_specs=[spec] * len(out_dtypes),
        out_shape=[jax.ShapeDtypeStruct(shape, dt) for dt in out_dtypes],
        compiler_params=_cparams(("parallel", "parallel")),
    )(*parts)
    return outs


def _me():
    return lax.axis_index("x"), lax.axis_index("y"), lax.axis_index("c")


def _xor_peer(k):
    x, y, c = _me()
    dx, dy, dc = (k >> 2) & 1, (k >> 1) & 1, k & 1
    return (x ^ dx if dx else x, y ^ dy if dy else y, c ^ dc if dc else c)


ANY = pl.BlockSpec(memory_space=pl.ANY)


def _all_gather_small(v, name):
    R, C = v.shape

    def body(v_ref, out_ref, send_sems, recv_sems):
        x, y, c = _me()
        mine = 4 * x + 2 * y + c
        out_ref[mine] = v_ref[...]
        copies = []
        for k in range(1, 8):
            cp = pltpu.make_async_remote_copy(src_ref=v_ref, dst_ref=out_ref.at[mine], send_sem=send_sems.at[k - 1],
                                              recv_sem=recv_sems.at[k - 1], device_id=_xor_peer(k), device_id_type=MESH)
            cp.start()
            copies.append(cp)
        for k in range(1, 8):
            px, py, pc = _xor_peer(k)
            pltpu.make_async_remote_copy(src_ref=v_ref, dst_ref=out_ref.at[4 * px + 2 * py + pc], send_sem=send_sems.at[k - 1],
                                         recv_sem=recv_sems.at[k - 1], device_id=_xor_peer(k), device_id_type=MESH).wait_recv()
        for cp in copies:
            cp.wait_send()

    return pl.pallas_call(
        body, name=name, out_shape=jax.ShapeDtypeStruct((8, R, C), F32),
        in_specs=[pl.BlockSpec(memory_space=pltpu.VMEM)], out_specs=pl.BlockSpec(memory_space=pltpu.VMEM),
        scratch_shapes=[pltpu.SemaphoreType.DMA((7,)), pltpu.SemaphoreType.DMA((7,))],
        compiler_params=pltpu.CompilerParams(vmem_limit_bytes=VMEM_LIMIT),
    )(v)


def _chip_peers():
    x, y, _ = _me()
    return [(1, (x, 1 - y)), (2, (1 - x, y)), (3, (1 - x, 1 - y))]


def _all_gather_shards(shards, name):
    n = len(shards)

    def body(*refs):
        ins, outs = refs[:n], refs[n:2 * n]
        send_sems, recv_sems = refs[2 * n:]
        x, y, c = _me()
        chip = 2 * x + y
        sib = (x, y, 1 - c)
        peers = _chip_peers()
        sends = []
        for t in range(n):
            half = ins[t].shape[0] // 2
            mine = pl.ds(c * half, half)
            for p, (k, (px, py)) in enumerate(peers):
                cp = pltpu.make_async_remote_copy(src_ref=ins[t].at[mine], dst_ref=outs[t].at[chip, mine],
                                                  send_sem=send_sems.at[6 * t + p], recv_sem=recv_sems.at[6 * t + p],
                                                  device_id=(px, py, c), device_id_type=MESH)
                cp.start()
                sends.append(cp)
        for t in range(n):
            half = ins[t].shape[0] // 2
            mine = pl.ds(c * half, half)
            for p, (k, (px, py)) in enumerate(peers):
                src_chip = 2 * px + py
                landed = outs[t].at[src_chip, mine]
                pltpu.make_async_remote_copy(src_ref=landed, dst_ref=landed, send_sem=send_sems.at[6 * t + p],
                                             recv_sem=recv_sems.at[6 * t + p], device_id=(px, py, c), device_id_type=MESH).wait_recv()
                fw = pltpu.make_async_remote_copy(src_ref=landed, dst_ref=landed, send_sem=send_sems.at[6 * t + 3 + p],
                                                  recv_sem=recv_sems.at[6 * t + 3 + p], device_id=sib, device_id_type=MESH)
                fw.start()
                sends.append(fw)
        for t in range(n):
            half = ins[t].shape[0] // 2
            theirs = pl.ds((1 - c) * half, half)
            for p, (k, (px, py)) in enumerate(peers):
                got = outs[t].at[2 * px + py, theirs]
                pltpu.make_async_remote_copy(src_ref=got, dst_ref=got, send_sem=send_sems.at[6 * t + 3 + p],
                                             recv_sem=recv_sems.at[6 * t + 3 + p], device_id=sib, device_id_type=MESH).wait_recv()
        for cp in sends:
            cp.wait_send()

    return pl.pallas_call(
        body, name=name,
        out_shape=[jax.ShapeDtypeStruct((4,) + s.shape, s.dtype) for s in shards],
        in_specs=[ANY] * n, out_specs=[ANY] * n,
        scratch_shapes=[pltpu.SemaphoreType.DMA((6 * n,)), pltpu.SemaphoreType.DMA((6 * n,))],
    )(*shards)


def _swap_halves_with_sibling(slabs, name):
    n = len(slabs)

    def body(*refs):
        ins, outs = refs[:n], refs[n:2 * n]
        send_sems, recv_sems = refs[2 * n:]
        x, y, c = _me()
        sib = (x, y, 1 - c)
        cps = []
        for t in range(n):
            half = ins[t].shape[1] // 2
            cp = pltpu.make_async_remote_copy(src_ref=ins[t].at[:, pl.ds((1 - c) * half, half)], dst_ref=outs[t],
                                              send_sem=send_sems.at[t], recv_sem=recv_sems.at[t], device_id=sib, device_id_type=MESH)
            cp.start()
            cps.append(cp)
        for cp in cps:
            cp.wait()

    return pl.pallas_call(
        body, name=name,
        out_shape=[jax.ShapeDtypeStruct((4, s.shape[1] // 2, s.shape[2]), s.dtype) for s in slabs],
        in_specs=[ANY] * n, out_specs=[ANY] * n,
        scratch_shapes=[pltpu.SemaphoreType.DMA((n,)), pltpu.SemaphoreType.DMA((n,))],
    )(*slabs)


def _exchange_between_chips(slabs, name):
    n = len(slabs)

    def body(*refs):
        ins, outs = refs[:n], refs[n:2 * n]
        send_sems, recv_sems = refs[2 * n:]
        x, y, c = _me()
        chip = 2 * x + y
        cps = []
        for t in range(n):
            for p, (k, (px, py)) in enumerate(_chip_peers()):
                cp = pltpu.make_async_remote_copy(src_ref=ins[t].at[2 * px + py], dst_ref=outs[t].at[chip],
                                                  send_sem=send_sems.at[3 * t + p], recv_sem=recv_sems.at[3 * t + p],
                                                  device_id=(px, py, c), device_id_type=MESH)
                cp.start()
                cps.append(cp)
        for t in range(n):
            for p, (k, (px, py)) in enumerate(_chip_peers()):
                got = outs[t].at[2 * px + py]
                pltpu.make_async_remote_copy(src_ref=got, dst_ref=got, send_sem=send_sems.at[3 * t + p],
                                             recv_sem=recv_sems.at[3 * t + p], device_id=(px, py, c), device_id_type=MESH).wait_recv()
        for cp in cps:
            cp.wait_send()

    return pl.pallas_call(
        body, name=name,
        out_shape=[jax.ShapeDtypeStruct(s.shape, s.dtype) for s in slabs],
        in_specs=[ANY] * n, out_specs=[ANY] * n,
        scratch_shapes=[pltpu.SemaphoreType.DMA((3 * n,)), pltpu.SemaphoreType.DMA((3 * n,))],
    )(*slabs)


def _send_halves_to_sibling(halves, name):
    n = len(halves)

    def body(*refs):
        ins, outs = refs[:n], refs[n:2 * n]
        send_sems, recv_sems = refs[2 * n:]
        x, y, c = _me()
        sib = (x, y, 1 - c)
        cps = []
        for t in range(n):
            cp = pltpu.make_async_remote_copy(src_ref=ins[t], dst_ref=outs[t], send_sem=send_sems.at[t],
                                              recv_sem=recv_sems.at[t], device_id=sib, device_id_type=MESH)
            cp.start()
            cps.append(cp)
        for cp in cps:
            cp.wait()

    return pl.pallas_call(
        body, name=name,
        out_shape=[jax.ShapeDtypeStruct(s.shape, s.dtype) for s in halves],
        in_specs=[ANY] * n, out_specs=[ANY] * n,
        scratch_shapes=[pltpu.SemaphoreType.DMA((n,)), pltpu.SemaphoreType.DMA((n,))],
    )(*halves)


def _rope_tables(positions):
    half = ROPE_DIM // 2
    S = positions.shape[0]
    inv_freq = ROPE_THETA ** (-jnp.arange(half, dtype=F32) / half)
    ang = positions.astype(F32)[:, None] * inv_freq
    cos, sin = jnp.cos(ang), jnp.sin(ang)
    zeros = functools.partial(jnp.zeros, dtype=F32)
    cc = jnp.concatenate([cos, cos, jnp.ones((S, HEAD_DIM - ROPE_DIM), F32)], axis=1)
    s1 = jnp.concatenate([-sin, zeros((S, HEAD_DIM - half))], axis=1)
    s2 = jnp.concatenate([zeros((S, half)), sin, zeros((S, HEAD_DIM - ROPE_DIM))], axis=1)
    return cc, s1, s2


def _ffn_fwd(x, gain, shift, scale, gate, w_gu, w_d, fs, tag):
    h = _pre_fwd(x, gain, shift, scale, tag + "_pre")
    ab = _matmul(h, w_gu, "nn", F32, tag + "_gate_up", tm=512, tn=1024, tk=4096)
    s = _swiglu_fwd(ab, fs, tag + "_swiglu")
    f = _matmul(s, w_d, "nn", F32, tag + "_down", tm=512, tn=1024, tk=8192)
    xn = _residual_fwd(x, gate, f, 0.5, tag + "_res")
    return xn, (x, h, ab, s, f)


def _ffn_bwd(dxn, saved, gain, shift, scale, gate, w_gu, w_d, fs, tag):
    x, h, ab, s, f = saved
    df, dgate = _residual_bwd(gate, f, dxn, 0.5, tag + "_res_bwd")
    ds = _matmul(df, w_d, "nt", F32, tag + "_down_dx", tm=1024, tn=1408, tk=4096)
    dw_d = _matmul(s, df, "tn", F32, tag + "_down_dw", tm=1408, tn=1024, tk=1024)
    dab = _swiglu_bwd(ab, ds, fs, tag + "_swiglu_bwd")
    dh = _matmul(dab, w_gu, "nt", F32, tag + "_gate_up_dx", tm=1024, tn=1024, tk=2816)
    dw_gu = _matmul(h, dab, "tn", F32, tag + "_gate_up_dw", tm=1024, tn=1024, tk=1024)
    dx, dgain, dshift, dscale = _pre_bwd(x, gain, shift, scale, dh, dxn, tag + "_pre_bwd")
    return dx, dw_gu, dw_d, dgain, dshift, dscale, dgate


def _flat_pad(parts, rows, cols):
    flat = jnp.concatenate([p.reshape(-1).astype(F32) for p in parts])
    return jnp.pad(flat, (0, rows * cols - flat.shape[0])).reshape(rows, cols)


def _cols_to_slabs(w, n):
    R, NC = w.shape
    return jnp.transpose(w.reshape(R, n, NC // n), (1, 0, 2))


def kernel(x, c, positions, w_ada, b_ada, ffn1_norm, ffn1_w_gate, ffn1_w_up, ffn1_w_down, mix_norm, w_in, conv_w, q_norm, k_norm, a_log, dt_bias, delta_out_norm, w_out, ffn2_norm, ffn2_w_gate, ffn2_w_up, ffn2_w_down, loss_target, m_w_ada, m_b_ada, m_ffn1_norm, m_ffn1_w_gate, m_ffn1_w_up, m_ffn1_w_down, m_mix_norm, m_w_in, m_conv_w, m_q_norm, m_k_norm, m_a_log, m_dt_bias, m_delta_out_norm, m_w_out, m_ffn2_norm, m_ffn2_w_gate, m_ffn2_w_up, m_ffn2_w_down, v_w_ada, v_b_ada, v_ffn1_norm, v_ffn1_w_gate, v_ffn1_w_up, v_ffn1_w_down, v_mix_norm, v_w_in, v_conv_w, v_q_norm, v_k_norm, v_a_log, v_dt_bias, v_delta_out_norm, v_w_out, v_ffn2_norm, v_ffn2_w_gate, v_ffn2_w_up, v_ffn2_w_down):
    xi, yi, ci = _me()
    chip = 2 * xi + yi
    dev = 2 * chip + ci
    xs = x[0]
    S, D = xs.shape
    HA, HD = N_ATTN_HEADS, N_DELTA_HEADS
    fs = ffn1_w_gate.shape[2]
    n_mod_shard = w_ada.shape[2]
    in_shard = w_in.shape[2]
    in_width = 4 * in_shard
    in_pad = -(-in_width // LANES) * LANES
    conv_shard = conv_w.shape[2]
    conv_width = 4 * conv_shard

    pack0 = jnp.zeros((8, max(D, conv_shard)), F32)
    pack0 = pack0.at[0, :D].set(c[0]).at[1:1 + CONV_WIDTH, :conv_shard].set(conv_w[0])
    got0 = _all_gather_small(pack0, "gather_cond")
    c_all = got0[:, 0, :D]
    conv_full = jnp.transpose(got0[::2, 1:1 + CONV_WIDTH, :conv_shard], (1, 0, 2)).reshape(CONV_WIDTH, conv_width)
    b_ada_mine = lax.dynamic_slice(b_ada, (0, chip * n_mod_shard), (1, n_mod_shard))
    mod_part = _ada_fwd(c_all, w_ada[0], b_ada_mine, "ada_fwd")
    got1 = _all_gather_small(mod_part, "gather_mod")
    mod = lax.dynamic_index_in_dim(got1[::2], dev, axis=1, keepdims=False).reshape(1, 4 * n_mod_shard)
    sh1, sc1, gt1, sh2, sc2, gt2, sh3, sc3, gt3 = [mod[:, i * D:(i + 1) * D] for i in range(N_MOD)]

    shards = [w[0].astype(BF16) for w in (ffn1_w_gate, ffn1_w_up, ffn1_w_down, w_in, w_out, ffn2_w_gate, ffn2_w_up, ffn2_w_down)]
    gathered = _all_gather_shards(shards, "gather_weights")
    g1g, g1u, g1d, gin, gout, g2g, g2u, g2d = [lax.dynamic_update_index_in_dim(g, s, chip, 0) for g, s in zip(gathered, shards)]

    def gate_up(gg, gu):
        return jnp.transpose(jnp.concatenate([gg, gu], axis=2), (1, 0, 2)).reshape(D, 8 * fs)

    w_gu1, w_gu2 = gate_up(g1g, g1u), gate_up(g2g, g2u)
    w_d1, w_d2 = g1d.reshape(4 * fs, D), g2d.reshape(4 * fs, D)
    w_in_f = jnp.pad(jnp.transpose(gin, (1, 0, 2)).reshape(D, in_width), ((0, 0), (0, in_pad - in_width)))
    w_out_f = gout.reshape(-1, D)

    x1, saved1 = _ffn_fwd(xs, ffn1_norm, sh1, sc1, gt1, w_gu1, w_d1, fs, "ffn1")

    cc, s1, s2 = _rope_tables(positions[0])
    alog_row = jnp.pad(a_log, ((0, 0), (0, LANES - HD)))
    dtb_row = jnp.pad(dt_bias, ((0, 0), (0, LANES - HD)))
    col_k, col_v, col_d, col_z, col_ab = HA, 2 * HA, 3 * HA, 3 * HA + 3 * HD, 3 * HA + 4 * HD
    h2 = _pre_fwd(x1, mix_norm, sh2, sc2, "mix_pre")
    proj = _matmul(h2, w_in_f, "nn", F32, "mix_in_proj", tm=512, tn=1024, tk=4096)
    qa, ka = _attn_prep_fwd(proj, q_norm, k_norm, cc, s1, s2, "attn_prep")
    oa, oa_b, lse = _attn_fwd(qa, ka, proj, col_v, "attn_fwd")
    dqkv = _conv_fwd(proj, col_d, conv_width, conv_full, "conv_fwd")
    *prep, t_inv = _delta_prep_fwd(dqkv, proj, col_ab, alog_row, dtb_row, "delta_prep")
    od_raw, states = _delta_scan_fwd(*prep, "delta_scan")
    od = _post_fwd(od_raw, proj, col_z, delta_out_norm, "delta_post")
    o = jnp.concatenate([oa_b, od], axis=1)
    mo = _matmul(o, w_out_f, "nn", F32, "mix_out_proj", tm=512, tn=1024, tk=4096)
    x2 = _residual_fwd(x1, gt2, mo, 1.0, "mix_res")

    x3, saved3 = _ffn_fwd(x2, ffn2_norm, sh3, sc3, gt3, w_gu2, w_d2, fs, "ffn2")
    loss_part, dy = _loss_head(x3, loss_target[0], "loss_head")
    loss = lax.psum(loss_part[0, 0], ("x", "y", "c"))

    dx2, dw_gu2, dw_d2, dgain3, dsh3, dsc3, dgt3 = _ffn_bwd(dy, saved3, ffn2_norm, sh3, sc3, gt3, w_gu2, w_d2, fs, "ffn2")

    dmo, dgt2 = _residual_bwd(gt2, mo, dx2, 1.0, "mix_res_bwd")
    do = _matmul(dmo, w_out_f, "nt", F32, "mix_out_dx", tm=1024, tn=1024, tk=4096)
    dw_out = _matmul(o, dmo, "tn", F32, "mix_out_dw", tm=1024, tn=1024, tk=1024)
    dq = _attn_bwd_dq(qa, ka, proj, col_v, oa, lse, do, 0, "attn_bwd_dq")
    dk, dv = _attn_bwd_dkv(qa, ka, proj, col_v, oa, lse, do, 0, "attn_bwd_dkv")
    dpq, dpk, dq_gain, dk_gain = _attn_prep_bwd(proj, q_norm, k_norm, cc, s1, s2, dq, dk, "attn_prep_bwd")
    dod, dz, ddn_gain = _post_bwd(od_raw, proj, col_z, delta_out_norm, do, HA, "delta_post_bwd")
    cots = _delta_scan_bwd(*prep, states, dod, "delta_scan_bwd")
    ddq, ddk, ddv, dab, dalog, ddtb = _delta_prep_bwd(dqkv, proj, col_ab, alog_row, dtb_row, cots, t_inv, "delta_prep_bwd")
    dconv_in, dconv_w = _conv_bwd(proj, col_d, conv_width, conv_full, jnp.concatenate([ddq, ddk, ddv], axis=1), "conv_bwd")
    dproj = jnp.concatenate([dpq, dpk, dv, dconv_in, dz, dab], axis=1)
    dh2 = _matmul(dproj, w_in_f, "nt", F32, "mix_in_dx", tm=1024, tn=1024, tk=2816)
    dw_in = _matmul(h2, dproj, "tn", F32, "mix_in_dw", tm=1024, tn=1024, tk=1024)
    dx1, dgain2, dsh2, dsc2 = _pre_bwd(x1, mix_norm, sh2, sc2, dh2, dx2, "mix_pre_bwd")

    dx0, dw_gu1, dw_d1, dgain1, dsh1, dsc1, dgt1 = _ffn_bwd(dx1, saved1, ffn1_norm, sh1, sc1, gt1, w_gu1, w_d1, fs, "ffn1")

    n_small = N_MOD * D + 3 * D + 5 * LANES + CONV_WIDTH * conv_width
    cols_small = -(-n_small // (8 * LANES)) * LANES
    small = _flat_pad([dsh1, dsc1, dgt1, dsh2, dsc2, dgt2, dsh3, dsc3, dgt3, dgain1, dgain2, dgain3,
                       dq_gain, dk_gain, dalog, ddtb, ddn_gain, dconv_w], 8, cols_small)
    got2 = _all_gather_small(small, "gather_small_grads")
    small_sum = _add_cast([got2[d:d + 1] for d in range(8)], [F32], "sum_small_grads")[0].reshape(-1)
    dmod_all = got2.reshape(8, -1)[:, :N_MOD * D]
    off = [0]

    def take(n):
        off[0] += n
        return small_sum[off[0] - n:off[0]]

    g_b_ada = take(N_MOD * D).reshape(1, -1)
    g_ffn1_norm, g_mix_norm, g_ffn2_norm = take(D).reshape(1, D), take(D).reshape(1, D), take(D).reshape(1, D)
    g_q_norm, g_k_norm = take(LANES).reshape(1, -1), take(LANES).reshape(1, -1)
    g_a_log, g_dt_bias = take(LANES)[:HD].reshape(1, HD), take(LANES)[:HD].reshape(1, HD)
    g_dn = take(LANES).reshape(1, -1)
    g_conv_full = take(CONV_WIDTH * conv_width).reshape(CONV_WIDTH, conv_width)
    g_conv = lax.dynamic_slice(g_conv_full, (0, chip * conv_shard), (CONV_WIDTH, conv_shard))

    slabs = [_cols_to_slabs(dw_gu1, 4), dw_d1.reshape(4, fs, D), _cols_to_slabs(dw_in[:, :in_width], 4),
             dw_out.reshape(4, -1, D), _cols_to_slabs(dw_gu2, 4), dw_d2.reshape(4, fs, D)]
    from_sibling = _swap_halves_with_sibling(slabs, "rs_sibling_swap")
    chip_sums32, chip_sums16 = [], []
    for t, (slab, other) in enumerate(zip(slabs, from_sibling)):
        half = slab.shape[1] // 2
        mine = lax.dynamic_slice_in_dim(slab, ci * half, half, axis=1)
        p32, p16 = _add_cast([mine, other], [F32, BF16], "rs_chip_sum_%d" % t)
        chip_sums32.append(p32)
        chip_sums16.append(p16)
    from_chips = _exchange_between_chips(chip_sums16, "rs_chip_exchange")
    halves = []
    for t, (p32, got) in enumerate(zip(chip_sums32, from_chips)):
        parts = [lax.dynamic_index_in_dim(p32, chip, axis=0, keepdims=True)]
        parts += [lax.dynamic_index_in_dim(got, (chip + k) % 4, axis=0, keepdims=True) for k in (1, 2, 3)]
        halves.append(_add_cast(parts, [F32], "rs_total_%d" % t)[0][0])
    theirs = _send_halves_to_sibling(halves, "rs_sibling_join")
    g_gu1, g_d1, g_in, g_out, g_gu2, g_d2 = [
        jnp.where(ci == 0, jnp.concatenate([mine, other], axis=0), jnp.concatenate([other, mine], axis=0))
        for mine, other in zip(halves, theirs)]

    res = {}

    def upd(name, w, g, m, v):
        d, nm, nv = _adamw(w[0], g, m[0], v[0], "adamw_" + name)
        res[name] = (g[None], d[None], nm[None], nv[None])

    upd("ffn1_w_gate", ffn1_w_gate, g_gu1[:, :fs], m_ffn1_w_gate, v_ffn1_w_gate)
    upd("ffn1_w_up", ffn1_w_up, g_gu1[:, fs:], m_ffn1_w_up, v_ffn1_w_up)
    upd("ffn1_w_down", ffn1_w_down, g_d1, m_ffn1_w_down, v_ffn1_w_down)
    upd("w_in", w_in, g_in, m_w_in, v_w_in)
    upd("w_out", w_out, g_out, m_w_out, v_w_out)
    upd("ffn2_w_gate", ffn2_w_gate, g_gu2[:, :fs], m_ffn2_w_gate, v_ffn2_w_gate)
    upd("ffn2_w_up", ffn2_w_up, g_gu2[:, fs:], m_ffn2_w_up, v_ffn2_w_up)
    upd("ffn2_w_down", ffn2_w_down, g_d2, m_ffn2_w_down, v_ffn2_w_down)
    upd("conv_w", conv_w, g_conv, m_conv_w, v_conv_w)

    dmod_mine = lax.dynamic_slice(dmod_all, (0, chip * n_mod_shard), (8, n_mod_shard))
    g, d, nm, nv = _adamw_outer(w_ada[0], m_w_ada[0], v_w_ada[0], jnp.transpose(c_all), dmod_mine, "adamw_w_ada")
    res["w_ada"] = (g[None], d[None], nm[None], nv[None])

    rep = [("b_ada", b_ada, g_b_ada, m_b_ada, v_b_ada), ("ffn1_norm", ffn1_norm, g_ffn1_norm, m_ffn1_norm, v_ffn1_norm),
           ("mix_norm", mix_norm, g_mix_norm, m_mix_norm, v_mix_norm), ("ffn2_norm", ffn2_norm, g_ffn2_norm, m_ffn2_norm, v_ffn2_norm),
           ("q_norm", q_norm, g_q_norm, m_q_norm, v_q_norm), ("k_norm", k_norm, g_k_norm, m_k_norm, v_k_norm),
           ("a_log", a_log, g_a_log, m_a_log, v_a_log), ("dt_bias", dt_bias, g_dt_bias, m_dt_bias, v_dt_bias),
           ("delta_out_norm", delta_out_norm, g_dn, m_delta_out_norm, v_delta_out_norm)]
    n_rep = sum(-(-r[1].shape[1] // LANES) * LANES for r in rep)
    cols_rep = -(-n_rep // (8 * LANES)) * LANES

    def pack_rep(idx):
        return _flat_pad([jnp.pad(r[idx], ((0, 0), (0, -r[idx].shape[1] % LANES))) for r in rep], 8, cols_rep)

    d_rep, nm_rep, nv_rep = [a.reshape(-1) for a in _adamw(pack_rep(1), pack_rep(2), pack_rep(3), pack_rep(4), "adamw_small")]
    o2 = 0
    for name, w, g, _, _ in rep:
        n = w.shape[1]
        res[name] = (g, d_rep[o2:o2 + n].reshape(1, n), nm_rep[o2:o2 + n].reshape(1, n), nv_rep[o2:o2 + n].reshape(1, n))
        o2 += -(-n // LANES) * LANES

    order = ["w_ada", "b_ada", "ffn1_norm", "ffn1_w_gate", "ffn1_w_up", "ffn1_w_down", "mix_norm", "w_in", "conv_w", "q_norm",
             "k_norm", "a_log", "dt_bias", "delta_out_norm", "w_out", "ffn2_norm", "ffn2_w_gate", "ffn2_w_up", "ffn2_w_down"]
    return (loss, dx0[None], *[res[n][0] for n in order], *[res[n][1] for n in order],
            *[res[n][2] for n in order], *[res[n][3] for n in order])
```

```python
import functools
import math

import jax
import jax.numpy as jnp
from jax import lax
from jax.experimental import pallas as pl
from jax.experimental.pallas import tpu as pltpu

F32 = jnp.float32
BF16 = jnp.bfloat16
MESH = pl.DeviceIdType.MESH

HEAD_DIM = 128
N_ATTN_HEADS = 8
N_DELTA_HEADS = 8
DILATED_PATTERNS = ((128, 1), (512, 4), (2048, 16))
MAX_WINDOW = 2048
ROPE_THETA = 500000.0
ROPE_DIM = HEAD_DIM // 4
CONV_WIDTH = 4
CHUNK = 64
NORM_EPS = 1e-6
N_MOD = 9
ADAM_LR = 0.001
ADAM_B1 = 0.9
ADAM_B2 = 0.999
ADAM_EPS = 1e-08
ADAM_WD = 0.01
ADAM_STEP = 10

LANES = 128
VMEM_LIMIT = 56 * 1024 * 1024
ATTN_TILE = 512
HIGHEST = lax.Precision.HIGHEST


def _cparams(sem=None):
    return pltpu.CompilerParams(dimension_semantics=sem, vmem_limit_bytes=VMEM_LIMIT)


def _pick(dim, pref):
    if dim <= pref:
        return dim
    t = (pref // LANES) * LANES
    while t >= LANES:
        if dim % t == 0:
            return t
        t -= LANES
    return dim


def _sigmoid(x):
    return 1.0 / (1.0 + jnp.exp(-x))


def _silu(x):
    return x * _sigmoid(x)


def _softplus(x):
    return jnp.maximum(x, 0.0) + jnp.log(1.0 + jnp.exp(-jnp.abs(x)))


def _rms(x, gain):
    return x * lax.rsqrt(jnp.mean(x * x, axis=-1, keepdims=True) + NORM_EPS) * gain


def _l2(x):
    return x * lax.rsqrt(jnp.sum(x * x, axis=-1, keepdims=True) + NORM_EPS)


def _modulate(x, gain, shift, scale):
    return _rms(x, gain) * (1.0 + scale) + shift


def _dot(a, b):
    return lax.dot_general(a, b, (((1,), (0,)), ((), ())), precision=HIGHEST, preferred_element_type=F32)


def _bdot(a, b, dims):
    return lax.dot_general(a.astype(BF16), b.astype(BF16), (dims, ((), ())), preferred_element_type=F32)


_NN, _NT, _TN = ((1,), (0,)), ((1,), (1,)), ((0,), (0,))
HIGH = lax.Precision.HIGH


def _dot3(a, b, dims=_NN):
    return lax.dot_general(a, b, (dims, ((), ())), precision=HIGH, preferred_element_type=F32)


@jax.custom_vjp
def _mm_nn(a, b):
    return _bdot(a, b, _NN)


_mm_nn.defvjp(lambda a, b: (_bdot(a, b, _NN), (a, b)),
              lambda res, g: (_bdot(g, res[1], _NT), _bdot(res[0], g, _TN)))


@jax.custom_vjp
def _mm_nt(a, b):
    return _bdot(a, b, _NT)


_mm_nt.defvjp(lambda a, b: (_bdot(a, b, _NT), (a, b)),
              lambda res, g: (_bdot(g, res[1], _NN), _bdot(g, res[0], _TN)))


def _tri_inv(a, eyef):
    t_inv = eyef - a
    p = a
    for _ in range(int(math.log2(a.shape[0])) - 1):
        p = _dot3(p, p)
        t_inv = _dot3(t_inv, eyef + p)
    return t_inv


@jax.custom_vjp
def _tri_inv_saved(a, t_inv):
    return t_inv


_tri_inv_saved.defvjp(lambda a, t_inv: (t_inv, t_inv),
                      lambda t_inv, g: (-_dot3(t_inv, _dot3(g, t_inv, _NT), _TN), jnp.zeros_like(t_inv)))


_MM_DIMS = {"nn": ((1,), (0,)), "nt": ((1,), (1,)), "tn": ((0,), (0,))}


def _matmul(a, b, mode, out_dtype, name, tm=1024, tn=1024, tk=1024):
    if mode == "nn":
        (M, K), (_, N) = a.shape, b.shape
    elif mode == "nt":
        (M, K), (N, _) = a.shape, b.shape
    else:
        (K, M), (_, N) = a.shape, b.shape
    tm, tn, tk = _pick(M, tm), _pick(N, tn), _pick(K, tk)
    nk = K // tk
    dims = _MM_DIMS[mode]

    def body(a_ref, b_ref, o_ref, acc_ref):
        k = pl.program_id(2)
        p = _bdot(a_ref[...], b_ref[...], dims)

        @pl.when(k == 0)
        def _():
            acc_ref[...] = p

        @pl.when(k > 0)
        def _():
            acc_ref[...] += p

        @pl.when(k == nk - 1)
        def _():
            o_ref[...] = acc_ref[...].astype(out_dtype)

    a_spec = pl.BlockSpec((tk, tm), lambda i, j, k: (k, i)) if mode == "tn" else pl.BlockSpec((tm, tk), lambda i, j, k: (i, k))
    b_spec = pl.BlockSpec((tn, tk), lambda i, j, k: (j, k)) if mode == "nt" else pl.BlockSpec((tk, tn), lambda i, j, k: (k, j))
    return pl.pallas_call(
        body, name=name, grid=(M // tm, N // tn, nk),
        in_specs=[a_spec, b_spec], out_specs=pl.BlockSpec((tm, tn), lambda i, j, k: (i, j)),
        out_shape=jax.ShapeDtypeStruct((M, N), out_dtype),
        scratch_shapes=[pltpu.VMEM((tm, tn), F32)],
        compiler_params=_cparams(("parallel", "parallel", "arbitrary")),
    )(a, b)


def _row_spec(tr, d):
    return pl.BlockSpec((tr, d), lambda i: (i, 0))


def _vec_spec(d):
    return pl.BlockSpec((1, d), lambda i: (0, 0))


def _pre_fwd(x, gain, shift, scale, name):
    S, D = x.shape
    tr = _pick(S, 256)

    def body(x_ref, g_ref, sh_ref, sc_ref, h_ref):
        h_ref[...] = _modulate(x_ref[...], g_ref[...], sh_ref[...], sc_ref[...]).astype(BF16)

    return pl.pallas_call(
        body, name=name, grid=(S // tr,),
        in_specs=[_row_spec(tr, D), _vec_spec(D), _vec_spec(D), _vec_spec(D)],
        out_specs=_row_spec(tr, D), out_shape=jax.ShapeDtypeStruct((S, D), BF16),
        compiler_params=_cparams(("parallel",)),
    )(x, gain, shift, scale)


def _pre_bwd(x, gain, shift, scale, dh, dx_in, name):
    S, D = x.shape
    tr = _pick(S, 256)

    def body(x_ref, g_ref, sh_ref, sc_ref, dh_ref, dxin_ref, dx_ref, dg_ref, dsh_ref, dsc_ref):
        _, vjp = jax.vjp(_modulate, x_ref[...], g_ref[...], sh_ref[...], sc_ref[...])
        dx, dg, dsh, dsc = vjp(dh_ref[...])
        dx_ref[...] = dxin_ref[...] + dx

        @pl.when(pl.program_id(0) == 0)
        def _():
            dg_ref[...] = jnp.zeros_like(dg_ref)
            dsh_ref[...] = jnp.zeros_like(dsh_ref)
            dsc_ref[...] = jnp.zeros_like(dsc_ref)

        dg_ref[...] += dg
        dsh_ref[...] += dsh
        dsc_ref[...] += dsc

    vec = jax.ShapeDtypeStruct((1, D), F32)
    return pl.pallas_call(
        body, name=name, grid=(S // tr,),
        in_specs=[_row_spec(tr, D), _vec_spec(D), _vec_spec(D), _vec_spec(D), _row_spec(tr, D), _row_spec(tr, D)],
        out_specs=[_row_spec(tr, D), _vec_spec(D), _vec_spec(D), _vec_spec(D)],
        out_shape=[jax.ShapeDtypeStruct((S, D), F32), vec, vec, vec],
        compiler_params=_cparams(("arbitrary",)),
    )(x, gain, shift, scale, dh, dx_in)


def _residual_fwd(x, gate, f, coef, name):
    S, D = x.shape
    tr = _pick(S, 256)

    def body(x_ref, g_ref, f_ref, o_ref):
        o_ref[...] = x_ref[...] + coef * g_ref[...] * f_ref[...]

    return pl.pallas_call(
        body, name=name, grid=(S // tr,),
        in_specs=[_row_spec(tr, D), _vec_spec(D), _row_spec(tr, D)],
        out_specs=_row_spec(tr, D), out_shape=jax.ShapeDtypeStruct((S, D), F32),
        compiler_params=_cparams(("parallel",)),
    )(x, gate, f)


def _residual_bwd(gate, f, dxn, coef, name):
    S, D = f.shape
    tr = _pick(S, 256)

    def body(g_ref, f_ref, d_ref, df_ref, dg_ref):
        d = d_ref[...]
        df_ref[...] = (coef * g_ref[...] * d).astype(BF16)

        @pl.when(pl.program_id(0) == 0)
        def _():
            dg_ref[...] = jnp.zeros_like(dg_ref)

        dg_ref[...] += jnp.sum(coef * f_ref[...] * d, axis=0, keepdims=True)

    return pl.pallas_call(
        body, name=name, grid=(S // tr,),
        in_specs=[_vec_spec(D), _row_spec(tr, D), _row_spec(tr, D)],
        out_specs=[_row_spec(tr, D), _vec_spec(D)],
        out_shape=[jax.ShapeDtypeStruct((S, D), BF16), jax.ShapeDtypeStruct((1, D), F32)],
        compiler_params=_cparams(("arbitrary",)),
    )(gate, f, dxn)


def _swiglu_fn(a, b):
    return _silu(a) * b


def _swiglu_fwd(ab, fs, name):
    S, F2 = ab.shape
    tr = _pick(S, 256)

    def body(ab_ref, s_ref):
        s_ref[...] = _swiglu_fn(ab_ref[:, :fs], ab_ref[:, fs:]).astype(BF16)

    return pl.pallas_call(
        body, name=name, grid=(S // tr, F2 // (2 * fs)),
        in_specs=[pl.BlockSpec((tr, 2 * fs), lambda i, j: (i, j))],
        out_specs=pl.BlockSpec((tr, fs), lambda i, j: (i, j)),
        out_shape=jax.ShapeDtypeStruct((S, F2 // 2), BF16),
        compiler_params=_cparams(("parallel", "parallel")),
    )(ab)


def _swiglu_bwd(ab, ds, fs, name):
    S, F2 = ab.shape
    tr = _pick(S, 256)

    def body(ab_ref, ds_ref, dab_ref):
        _, vjp = jax.vjp(_swiglu_fn, ab_ref[:, :fs], ab_ref[:, fs:])
        da, db = vjp(ds_ref[...])
        dab_ref[:, :fs] = da.astype(BF16)
        dab_ref[:, fs:] = db.astype(BF16)

    return pl.pallas_call(
        body, name=name, grid=(S // tr, F2 // (2 * fs)),
        in_specs=[pl.BlockSpec((tr, 2 * fs), lambda i, j: (i, j)), pl.BlockSpec((tr, fs), lambda i, j: (i, j))],
        out_specs=pl.BlockSpec((tr, 2 * fs), lambda i, j: (i, j)),
        out_shape=jax.ShapeDtypeStruct((S, F2), BF16),
        compiler_params=_cparams(("parallel", "parallel")),
    )(ab, ds)


def _loss_head(y, target, name):
    S, D = y.shape
    tr = _pick(S, 256)

    def body(y_ref, t_ref, l_ref, dy_ref):
        e = y_ref[...] - t_ref[...]
        dy_ref[...] = e * (1.0 / D)

        @pl.when(pl.program_id(0) == 0)
        def _():
            l_ref[...] = jnp.zeros_like(l_ref)

        l_ref[...] += jnp.sum(jnp.sum(e * e, axis=-1, keepdims=True), axis=0, keepdims=True) * (0.5 / D)

    return pl.pallas_call(
        body, name=name, grid=(S // tr,),
        in_specs=[_row_spec(tr, D), _row_spec(tr, D)],
        out_specs=[pl.BlockSpec((1, 1), lambda i: (0, 0)), _row_spec(tr, D)],
        out_shape=[jax.ShapeDtypeStruct((1, 1), F32), jax.ShapeDtypeStruct((S, D), F32)],
        compiler_params=_cparams(("arbitrary",)),
    )(y, target)


def _rope(y, cc, s1, s2):
    return y * cc + pltpu.roll(y, LANES - ROPE_DIM // 2, 1) * s1 + pltpu.roll(y, ROPE_DIM // 2, 1) * s2


def _rope_t(d, cc, s1, s2):
    return d * cc + pltpu.roll(d * s1, ROPE_DIM // 2, 1) + pltpu.roll(d * s2, LANES - ROPE_DIM // 2, 1)


def _head_spec(tr, col0):
    return pl.BlockSpec((tr, HEAD_DIM), lambda i, h: (i, col0 + h))


def _tab_spec(tr):
    return pl.BlockSpec((tr, HEAD_DIM), lambda i, h: (i, 0))


def _gain_spec():
    return pl.BlockSpec((1, HEAD_DIM), lambda i, h: (0, 0))


def _attn_prep_fwd(proj, q_gain, k_gain, cc, s1, s2, name):
    S = proj.shape[0]
    H = N_ATTN_HEADS
    tr = _pick(S, 512)

    def body(q_ref, k_ref, qg_ref, kg_ref, cc_ref, s1_ref, s2_ref, qo_ref, ko_ref):
        cc, s1, s2 = cc_ref[...], s1_ref[...], s2_ref[...]
        qo_ref[...] = _rope(_rms(q_ref[...], qg_ref[...]), cc, s1, s2).astype(BF16)
        ko_ref[...] = _rope(_rms(k_ref[...], kg_ref[...]), cc, s1, s2).astype(BF16)

    out = jax.ShapeDtypeStruct((S, H * HEAD_DIM), BF16)
    return pl.pallas_call(
        body, name=name, grid=(S // tr, H),
        in_specs=[_head_spec(tr, 0), _head_spec(tr, H), _gain_spec(), _gain_spec(), _tab_spec(tr), _tab_spec(tr), _tab_spec(tr)],
        out_specs=[_head_spec(tr, 0), _head_spec(tr, 0)], out_shape=[out, out],
        compiler_params=_cparams(("parallel", "parallel")),
    )(proj, proj, q_gain, k_gain, cc, s1, s2)


def _attn_prep_bwd(proj, q_gain, k_gain, cc, s1, s2, dq, dk, name):
    S = proj.shape[0]
    H = N_ATTN_HEADS
    tr = _pick(S, 512)

    def body(q_ref, k_ref, qg_ref, kg_ref, cc_ref, s1_ref, s2_ref, dq_ref, dk_ref, dpq_ref, dpk_ref, dqg_ref, dkg_ref):
        cc, s1, s2 = cc_ref[...], s1_ref[...], s2_ref[...]

        @pl.when((pl.program_id(0) == 0) & (pl.program_id(1) == 0))
        def _():
            dqg_ref[...] = jnp.zeros_like(dqg_ref)
            dkg_ref[...] = jnp.zeros_like(dkg_ref)

        _, vjp_q = jax.vjp(_rms, q_ref[...], qg_ref[...])
        dxq, dgq = vjp_q(_rope_t(dq_ref[...], cc, s1, s2))
        _, vjp_k = jax.vjp(_rms, k_ref[...], kg_ref[...])
        dxk, dgk = vjp_k(_rope_t(dk_ref[...], cc, s1, s2))
        dpq_ref[...] = dxq.astype(BF16)
        dpk_ref[...] = dxk.astype(BF16)
        dqg_ref[...] += dgq
        dkg_ref[...] += dgk

    out = jax.ShapeDtypeStruct((S, H * HEAD_DIM), BF16)
    gout = jax.ShapeDtypeStruct((1, HEAD_DIM), F32)
    return pl.pallas_call(
        body, name=name, grid=(S // tr, H),
        in_specs=[_head_spec(tr, 0), _head_spec(tr, H), _gain_spec(), _gain_spec(), _tab_spec(tr), _tab_spec(tr), _tab_spec(tr),
                  _head_spec(tr, 0), _head_spec(tr, 0)],
        out_specs=[_head_spec(tr, 0), _head_spec(tr, 0), _gain_spec(), _gain_spec()], out_shape=[out, out, gout, gout],
        compiler_params=_cparams(("arbitrary", "arbitrary")),
    )(proj, proj, q_gain, k_gain, cc, s1, s2, dq, dk)


def _multiplicity(j, t):
    ti = lax.broadcasted_iota(jnp.int32, (t, t), 0)
    si = lax.broadcasted_iota(jnp.int32, (t, t), 1)
    delta = j * t + ti - si
    cnt = jnp.zeros((t, t), F32)
    for window, dil in DILATED_PATTERNS:
        ok = (delta >= 0) & ((delta & (dil - 1)) == 0) & (delta <= window)
        cnt = cnt + ok.astype(F32)
    return cnt


_NEG = -1e30


def _attn_fwd(q, k, proj, v_col0, name):
    S = q.shape[0]
    H = N_ATTN_HEADS
    t = _pick(S, ATTN_TILE)
    nq = S // t
    nj = MAX_WINDOW // t + 1
    scale = HEAD_DIM ** -0.5

    def body(q_ref, k_ref, v_ref, o_ref, ob_ref, lse_ref, m_sc, l_sc, acc_sc):
        qb, j = pl.program_id(1), pl.program_id(2)

        @pl.when(j == 0)
        def _():
            m_sc[...] = jnp.full_like(m_sc, _NEG)
            l_sc[...] = jnp.zeros_like(l_sc)
            acc_sc[...] = jnp.zeros_like(acc_sc)

        @pl.when(qb - j >= 0)
        def _():
            cnt = _multiplicity(j, t)
            s = _bdot(q_ref[...], k_ref[...], ((1,), (1,))) * scale
            s = jnp.where(cnt > 0.0, s, _NEG)
            m_prev = m_sc[...]
            m_new = jnp.maximum(m_prev, jnp.max(s, axis=-1, keepdims=True))
            alpha = jnp.exp(m_prev - m_new)
            p = cnt * jnp.exp(s - m_new)
            l_sc[...] = alpha * l_sc[...] + jnp.sum(p, axis=-1, keepdims=True)
            acc_sc[...] = alpha * acc_sc[...] + _bdot(p, v_ref[...], ((1,), (0,)))
            m_sc[...] = m_new

        @pl.when(j == nj - 1)
        def _():
            o = acc_sc[...] / l_sc[...]
            o_ref[...] = o
            ob_ref[...] = o.astype(BF16)
            lse_ref[...] = jnp.broadcast_to(m_sc[...] + jnp.log(l_sc[...]), (t, HEAD_DIM))

    qspec = pl.BlockSpec((t, HEAD_DIM), lambda h, i, j: (i, h))
    kspec = pl.BlockSpec((t, HEAD_DIM), lambda h, i, j: (jnp.maximum(i - j, 0), h))
    vspec = pl.BlockSpec((t, HEAD_DIM), lambda h, i, j: (jnp.maximum(i - j, 0), v_col0 + h))
    return pl.pallas_call(
        body, name=name, grid=(H, nq, nj),
        in_specs=[qspec, kspec, vspec], out_specs=[qspec, qspec, qspec],
        out_shape=[jax.ShapeDtypeStruct((S, H * HEAD_DIM), F32), jax.ShapeDtypeStruct((S, H * HEAD_DIM), BF16),
                   jax.ShapeDtypeStruct((S, H * HEAD_DIM), F32)],
        scratch_shapes=[pltpu.VMEM((t, 1), F32), pltpu.VMEM((t, 1), F32), pltpu.VMEM((t, HEAD_DIM), F32)],
        compiler_params=_cparams(("parallel", "parallel", "arbitrary")),
    )(q, k, proj)


def _attn_probs(q, k, lse, j, t, scale):
    cnt = _multiplicity(j, t)
    s = _bdot(q, k, ((1,), (1,))) * scale
    s = jnp.where(cnt > 0.0, s, _NEG)
    return cnt * jnp.exp(s - lse)


def _attn_bwd_dq(q, k, proj, v_col0, o, lse, do, do_col0, name):
    S = q.shape[0]
    H = N_ATTN_HEADS
    t = _pick(S, ATTN_TILE)
    nq = S // t
    nj = MAX_WINDOW // t + 1
    scale = HEAD_DIM ** -0.5

    def body(q_ref, k_ref, v_ref, o_ref, lse_ref, do_ref, dq_ref, acc_sc):
        qb, j = pl.program_id(1), pl.program_id(2)

        @pl.when(j == 0)
        def _():
            acc_sc[...] = jnp.zeros_like(acc_sc)

        @pl.when(qb - j >= 0)
        def _():
            do = do_ref[...]
            dsum = jnp.sum(do * o_ref[...], axis=-1, keepdims=True)
            lse = jnp.max(lse_ref[...], axis=-1, keepdims=True)
            p = _attn_probs(q_ref[...], k_ref[...], lse, j, t, scale)
            dp = _bdot(do, v_ref[...], ((1,), (1,)))
            ds = p * (dp - dsum)
            acc_sc[...] += _bdot(ds, k_ref[...], ((1,), (0,))) * scale

        @pl.when(j == nj - 1)
        def _():
            dq_ref[...] = acc_sc[...]

    qspec = pl.BlockSpec((t, HEAD_DIM), lambda h, i, j: (i, h))
    dospec = pl.BlockSpec((t, HEAD_DIM), lambda h, i, j: (i, do_col0 + h))
    kspec = pl.BlockSpec((t, HEAD_DIM), lambda h, i, j: (jnp.maximum(i - j, 0), h))
    vspec = pl.BlockSpec((t, HEAD_DIM), lambda h, i, j: (jnp.maximum(i - j, 0), v_col0 + h))
    return pl.pallas_call(
        body, name=name, grid=(H, nq, nj),
        in_specs=[qspec, kspec, vspec, qspec, qspec, dospec], out_specs=qspec,
        out_shape=jax.ShapeDtypeStruct((S, H * HEAD_DIM), F32),
        scratch_shapes=[pltpu.VMEM((t, HEAD_DIM), F32)],
        compiler_params=_cparams(("parallel", "parallel", "arbitrary")),
    )(q, k, proj, o, lse, do)


def _attn_bwd_dkv(q, k, proj, v_col0, o, lse, do, do_col0, name):
    S = q.shape[0]
    H = N_ATTN_HEADS
    t = _pick(S, ATTN_TILE)
    nq = S // t
    nj = MAX_WINDOW // t + 1
    scale = HEAD_DIM ** -0.5

    def body(q_ref, k_ref, v_ref, o_ref, lse_ref, do_ref, dk_ref, dv_ref, dk_sc, dv_sc):
        kb, j = pl.program_id(1), pl.program_id(2)

        @pl.when(j == 0)
        def _():
            dk_sc[...] = jnp.zeros_like(dk_sc)
            dv_sc[...] = jnp.zeros_like(dv_sc)

        @pl.when(kb + j < nq)
        def _():
            do = do_ref[...]
            dsum = jnp.sum(do * o_ref[...], axis=-1, keepdims=True)
            lse = jnp.max(lse_ref[...], axis=-1, keepdims=True)
            p = _attn_probs(q_ref[...], k_ref[...], lse, j, t, scale)
            dp = _bdot(do, v_ref[...], ((1,), (1,)))
            ds = p * (dp - dsum)
            dv_sc[...] += _bdot(p, do, ((0,), (0,)))
            dk_sc[...] += _bdot(ds, q_ref[...], ((0,), (0,))) * scale

        @pl.when(j == nj - 1)
        def _():
            dk_ref[...] = dk_sc[...]
            dv_ref[...] = dv_sc[...].astype(BF16)

    def qrow(h, i, j):
        return jnp.minimum(i + j, nq - 1)

    qspec = pl.BlockSpec((t, HEAD_DIM), lambda h, i, j: (qrow(h, i, j), h))
    dospec = pl.BlockSpec((t, HEAD_DIM), lambda h, i, j: (qrow(h, i, j), do_col0 + h))
    kspec = pl.BlockSpec((t, HEAD_DIM), lambda h, i, j: (i, h))
    vspec = pl.BlockSpec((t, HEAD_DIM), lambda h, i, j: (i, v_col0 + h))
    return pl.pallas_call(
        body, name=name, grid=(H, nq, nj),
        in_specs=[qspec, kspec, vspec, qspec, qspec, dospec], out_specs=[kspec, kspec],
        out_shape=[jax.ShapeDtypeStruct((S, H * HEAD_DIM), F32), jax.ShapeDtypeStruct((S, H * HEAD_DIM), BF16)],
        scratch_shapes=[pltpu.VMEM((t, HEAD_DIM), F32), pltpu.VMEM((t, HEAD_DIM), F32)],
        compiler_params=_cparams(("parallel", "parallel", "arbitrary")),
    )(q, k, proj, o, lse, do)


def _conv_pre(x_ref, w_ref):
    x = x_ref[...]
    rows = lax.broadcasted_iota(jnp.int32, x.shape, 0)
    shifted = [x]
    acc = x * w_ref[pl.ds(CONV_WIDTH - 1, 1), :]
    for sft in range(1, CONV_WIDTH):
        xs = jnp.where(rows >= sft, pltpu.roll(x, sft, 0), 0.0)
        shifted.append(xs)
        acc = acc + xs * w_ref[pl.ds(CONV_WIDTH - 1 - sft, 1), :]
    return acc, shifted


def _conv_fwd(proj, col0, width, w, name):
    S = proj.shape[0]

    def body(x_ref, w_ref, y_ref):
        acc, _ = _conv_pre(x_ref, w_ref)
        y_ref[...] = _silu(acc)

    return pl.pallas_call(
        body, name=name, grid=(width // LANES,),
        in_specs=[pl.BlockSpec((S, LANES), lambda c: (0, col0 + c)), pl.BlockSpec((CONV_WIDTH, LANES), lambda c: (0, c))],
        out_specs=pl.BlockSpec((S, LANES), lambda c: (0, c)),
        out_shape=jax.ShapeDtypeStruct((S, width), F32),
        compiler_params=_cparams(("parallel",)),
    )(proj, w)


def _conv_bwd(proj, col0, width, w, dy, name):
    S = proj.shape[0]

    def body(x_ref, w_ref, d_ref, dx_ref, dw_ref):
        acc, shifted = _conv_pre(x_ref, w_ref)
        sig = _sigmoid(acc)
        da = d_ref[...] * (sig * (1.0 + acc * (1.0 - sig)))
        rows = lax.broadcasted_iota(jnp.int32, da.shape, 0)
        dx = da * w_ref[pl.ds(CONV_WIDTH - 1, 1), :]
        dw_ref[pl.ds(CONV_WIDTH - 1, 1), :] = jnp.sum(da * shifted[0], axis=0, keepdims=True)
        for sft in range(1, CONV_WIDTH):
            back = jnp.where(rows < S - sft, pltpu.roll(da, S - sft, 0), 0.0)
            dx = dx + back * w_ref[pl.ds(CONV_WIDTH - 1 - sft, 1), :]
            dw_ref[pl.ds(CONV_WIDTH - 1 - sft, 1), :] = jnp.sum(da * shifted[sft], axis=0, keepdims=True)
        dx_ref[...] = dx.astype(BF16)

    return pl.pallas_call(
        body, name=name, grid=(width // LANES,),
        in_specs=[pl.BlockSpec((S, LANES), lambda c: (0, col0 + c)), pl.BlockSpec((CONV_WIDTH, LANES), lambda c: (0, c)),
                  pl.BlockSpec((S, LANES), lambda c: (0, c))],
        out_specs=[pl.BlockSpec((S, LANES), lambda c: (0, c)), pl.BlockSpec((CONV_WIDTH, LANES), lambda c: (0, c))],
        out_shape=[jax.ShapeDtypeStruct((S, width), BF16), jax.ShapeDtypeStruct((CONV_WIDTH, width), F32)],
        compiler_params=_cparams(("parallel",)),
    )(proj, w, dy)


PREP_CHUNKS = 4


def _chunk_prep(qraw, kraw, v, ab, alog_row, dtb_row, mask_g, mask_b, t_saved=None):
    c = qraw.shape[0]
    mm_nt, mm_nn = (_mm_nt, _mm_nn) if t_saved is not None else (lambda p, r: _bdot(p, r, _NT), lambda p, r: _bdot(p, r, _NN))
    q = _l2(qraw) * (HEAD_DIM ** -0.5)
    k = _l2(kraw)
    gfull = -jnp.exp(alog_row) * _softplus(ab + dtb_row)
    bfull = _sigmoid(ab)
    g = jnp.sum(jnp.where(mask_g, gfull, 0.0), axis=-1, keepdims=True)
    beta = jnp.sum(jnp.where(mask_b, bfull, 0.0), axis=-1, keepdims=True)
    row = lax.broadcasted_iota(jnp.int32, (c, c), 0)
    col = lax.broadcasted_iota(jnp.int32, (c, c), 1)
    tril, strict, eye = row >= col, row > col, row == col
    g_row = jnp.sum(jnp.where(eye, g, 0.0), axis=0, keepdims=True)
    gc_col = jnp.sum(jnp.where(tril, g_row, 0.0), axis=1, keepdims=True)
    gc_row = jnp.sum(jnp.where(row <= col, g, 0.0), axis=0, keepdims=True)
    gamma = jnp.where(tril, jnp.exp(jnp.where(tril, gc_col - gc_row, 0.0)), 0.0)
    kb = k * beta
    a = jnp.where(strict, mm_nt(kb, k) * gamma, 0.0)
    t_inv = _tri_inv(a, eye.astype(F32)) if t_saved is None else _tri_inv_saved(a, t_saved)
    egc = jnp.exp(gc_col)
    u = mm_nn(t_inv, v * beta)
    w = mm_nn(t_inv, kb * egc)
    qd = q * egc
    g_last = jnp.sum(g, axis=0, keepdims=True)
    kt = k * jnp.exp(g_last - gc_col)
    dec = jnp.broadcast_to(jnp.exp(g_last), (1, HEAD_DIM))
    intra = mm_nt(q, k) * gamma
    if t_saved is None:
        return u, w, qd, kt, intra, dec, t_inv
    return u, w, qd, kt, intra, dec


def _lane_masks(h):
    lane = lax.broadcasted_iota(jnp.int32, (1, LANES), 1)
    return lane == h, lane == N_DELTA_HEADS + h


def _prep_specs(tr, ab_col):
    H = N_DELTA_HEADS
    return [
        pl.BlockSpec((tr, HEAD_DIM), lambda i, h: (i, h)),
        pl.BlockSpec((tr, HEAD_DIM), lambda i, h: (i, H + h)),
        pl.BlockSpec((tr, HEAD_DIM), lambda i, h: (i, 2 * H + h)),
        pl.BlockSpec((tr, LANES), lambda i, h: (i, ab_col)),
        pl.BlockSpec((1, LANES), lambda i, h: (0, 0)),
        pl.BlockSpec((1, LANES), lambda i, h: (0, 0)),
    ]


def _prep_out_specs(tr):
    nc = tr // CHUNK
    hs = pl.BlockSpec((tr, HEAD_DIM), lambda i, h: (i, h))
    return [hs, hs, hs, hs,
            pl.BlockSpec((None, tr, CHUNK), lambda i, h: (h, i, 0)),
            pl.BlockSpec((None, nc, 1, HEAD_DIM), lambda i, h: (h, i, 0, 0)),
            pl.BlockSpec((None, tr, CHUNK), lambda i, h: (h, i, 0))]


def _prep_out_shapes(S):
    H = N_DELTA_HEADS
    hs = jax.ShapeDtypeStruct((S, H * HEAD_DIM), F32)
    sq = jax.ShapeDtypeStruct((H, S, CHUNK), F32)
    return [hs, hs, hs, hs, sq, jax.ShapeDtypeStruct((H, S // CHUNK, 1, HEAD_DIM), F32), sq]


def _delta_prep_fwd(dqkv, proj, ab_col, alog_row, dtb_row, name):
    S = dqkv.shape[0]
    tr = min(S, PREP_CHUNKS * CHUNK)
    nc = tr // CHUNK

    def body(q_ref, k_ref, v_ref, ab_ref, al_ref, dt_ref, u_ref, w_ref, qd_ref, kt_ref, in_ref, dec_ref, ti_ref):
        mask_g, mask_b = _lane_masks(pl.program_id(1))
        for ci in range(nc):
            rs = pl.ds(ci * CHUNK, CHUNK)
            u, w, qd, kt, intra, dec, t_inv = _chunk_prep(q_ref[rs, :], k_ref[rs, :], v_ref[rs, :], ab_ref[rs, :],
                                                          al_ref[...], dt_ref[...], mask_g, mask_b)
            u_ref[rs, :] = u
            w_ref[rs, :] = w
            qd_ref[rs, :] = qd
            kt_ref[rs, :] = kt
            in_ref[rs, :] = intra
            dec_ref[ci] = dec
            ti_ref[rs, :] = t_inv

    return pl.pallas_call(
        body, name=name, grid=(S // tr, N_DELTA_HEADS),
        in_specs=_prep_specs(tr, ab_col), out_specs=_prep_out_specs(tr), out_shape=_prep_out_shapes(S),
        compiler_params=_cparams(("parallel", "parallel")),
    )(dqkv, dqkv, dqkv, proj, alog_row, dtb_row)


def _delta_prep_bwd(dqkv, proj, ab_col, alog_row, dtb_row, cots, t_inv, name):
    S = dqkv.shape[0]
    H = N_DELTA_HEADS
    tr = min(S, PREP_CHUNKS * CHUNK)
    nc = tr // CHUNK

    def body(q_ref, k_ref, v_ref, ab_ref, al_ref, dt_ref, du_ref, dw_ref, dqd_ref, dkt_ref, din_ref, ddec_ref, ti_ref,
             dq_ref, dk_ref, dv_ref, dab_ref, dal_ref, ddt_ref, dab_sc):
        h = pl.program_id(1)
        mask_g, mask_b = _lane_masks(h)

        @pl.when((pl.program_id(0) == 0) & (h == 0))
        def _():
            dal_ref[...] = jnp.zeros_like(dal_ref)
            ddt_ref[...] = jnp.zeros_like(ddt_ref)

        @pl.when(h == 0)
        def _():
            dab_sc[...] = jnp.zeros_like(dab_sc)

        for ci in range(nc):
            rs = pl.ds(ci * CHUNK, CHUNK)
            fn = functools.partial(_chunk_prep, mask_g=mask_g, mask_b=mask_b, t_saved=ti_ref[rs, :])
            _, vjp = jax.vjp(fn, q_ref[rs, :], k_ref[rs, :], v_ref[rs, :], ab_ref[rs, :], al_ref[...], dt_ref[...])
            dq, dk, dv, dab, dal, ddt = vjp((du_ref[rs, :], dw_ref[rs, :], dqd_ref[rs, :], dkt_ref[rs, :],
                                             din_ref[rs, :], ddec_ref[ci]))
            dq_ref[rs, :] = dq
            dk_ref[rs, :] = dk
            dv_ref[rs, :] = dv
            dab_sc[rs, :] += dab
            dal_ref[...] += dal
            ddt_ref[...] += ddt

        @pl.when(h == H - 1)
        def _():
            dab_ref[...] = dab_sc[...].astype(BF16)

    hs = pl.BlockSpec((tr, HEAD_DIM), lambda i, h: (i, h))
    hshape = jax.ShapeDtypeStruct((S, H * HEAD_DIM), F32)
    row = pl.BlockSpec((1, LANES), lambda i, h: (0, 0))
    rshape = jax.ShapeDtypeStruct((1, LANES), F32)
    return pl.pallas_call(
        body, name=name, grid=(S // tr, H),
        in_specs=_prep_specs(tr, ab_col) + _prep_out_specs(tr),
        out_specs=[hs, hs, hs, pl.BlockSpec((tr, LANES), lambda i, h: (i, 0)), row, row],
        out_shape=[hshape, hshape, hshape, jax.ShapeDtypeStruct((S, LANES), BF16), rshape, rshape],
        scratch_shapes=[pltpu.VMEM((tr, LANES), F32)],
        compiler_params=_cparams(("arbitrary", "arbitrary")),
    )(dqkv, dqkv, dqkv, proj, alog_row, dtb_row, *cots, t_inv)


def _scan_step(state, u, w, qd, kt, intra, dec):
    v_new = u - _dot3(w, state)
    o = _dot3(qd, state) + _dot3(intra, v_new)
    return o, state * dec + _dot3(kt, v_new, _TN)


def _scan_specs(rev, n):
    H = N_DELTA_HEADS

    def cix(i):
        return (n - 1 - i) if rev else i

    row = pl.BlockSpec((CHUNK, H * HEAD_DIM), lambda i: (cix(i), 0))
    return row, pl.BlockSpec((H, CHUNK, CHUNK), lambda i: (0, cix(i), 0)), \
        pl.BlockSpec((H, 1, 1, HEAD_DIM), lambda i: (0, cix(i), 0, 0)), \
        pl.BlockSpec((1, H, HEAD_DIM, HEAD_DIM), lambda i: (cix(i), 0, 0, 0))


def _delta_scan_fwd(u, w, qd, kt, intra, dec, name):
    S = u.shape[0]
    H = N_DELTA_HEADS
    n = S // CHUNK
    row, ispec, dspec, sspec = _scan_specs(False, n)

    def body(u_ref, w_ref, qd_ref, kt_ref, in_ref, dec_ref, o_ref, st_ref, s_sc):
        @pl.when(pl.program_id(0) == 0)
        def _():
            s_sc[...] = jnp.zeros_like(s_sc)

        for h in range(H):
            cs = pl.ds(h * HEAD_DIM, HEAD_DIM)
            state = s_sc[h]
            st_ref[0, h] = state
            o, new = _scan_step(state, u_ref[:, cs], w_ref[:, cs], qd_ref[:, cs], kt_ref[:, cs], in_ref[h], dec_ref[h, 0])
            o_ref[:, cs] = o
            s_sc[h] = new

    return pl.pallas_call(
        body, name=name, grid=(n,),
        in_specs=[row, row, row, row, ispec, dspec], out_specs=[row, sspec],
        out_shape=[jax.ShapeDtypeStruct((S, H * HEAD_DIM), F32), jax.ShapeDtypeStruct((n, H, HEAD_DIM, HEAD_DIM), F32)],
        scratch_shapes=[pltpu.VMEM((H, HEAD_DIM, HEAD_DIM), F32)],
        compiler_params=_cparams(("arbitrary",)),
    )(u, w, qd, kt, intra, dec)


def _delta_scan_bwd(u, w, qd, kt, intra, dec, states, do, name):
    S = u.shape[0]
    H = N_DELTA_HEADS
    n = S // CHUNK
    row, ispec, dspec, sspec = _scan_specs(True, n)

    def body(u_ref, w_ref, qd_ref, kt_ref, in_ref, dec_ref, st_ref, do_ref,
             du_ref, dw_ref, dqd_ref, dkt_ref, din_ref, ddec_ref, ds_sc):
        @pl.when(pl.program_id(0) == 0)
        def _():
            ds_sc[...] = jnp.zeros_like(ds_sc)

        for h in range(H):
            cs = pl.ds(h * HEAD_DIM, HEAD_DIM)
            _, vjp = jax.vjp(_scan_step, st_ref[0, h], u_ref[:, cs], w_ref[:, cs], qd_ref[:, cs], kt_ref[:, cs],
                             in_ref[h], dec_ref[h, 0])
            dstate, du, dw, dqd, dkt, din, ddec = vjp((do_ref[:, cs], ds_sc[h]))
            du_ref[:, cs] = du
            dw_ref[:, cs] = dw
            dqd_ref[:, cs] = dqd
            dkt_ref[:, cs] = dkt
            din_ref[h] = din
            ddec_ref[h, 0] = ddec
            ds_sc[h] = dstate

    hshape = jax.ShapeDtypeStruct((S, H * HEAD_DIM), F32)
    return pl.pallas_call(
        body, name=name, grid=(n,),
        in_specs=[row, row, row, row, ispec, dspec, sspec, row],
        out_specs=[row, row, row, row, ispec, dspec],
        out_shape=[hshape, hshape, hshape, hshape, jax.ShapeDtypeStruct((H, S, CHUNK), F32),
                   jax.ShapeDtypeStruct((H, n, 1, HEAD_DIM), F32)],
        scratch_shapes=[pltpu.VMEM((H, HEAD_DIM, HEAD_DIM), F32)],
        compiler_params=_cparams(("arbitrary",)),
    )(u, w, qd, kt, intra, dec, states, do)


def _gated_norm(od, z, gain):
    return _rms(od, gain) * _silu(z)


def _post_fwd(od, proj, z_col0, gain, name):
    S = od.shape[0]
    H = N_DELTA_HEADS
    tr = _pick(S, 512)

    def body(od_ref, z_ref, g_ref, o_ref):
        o_ref[...] = _gated_norm(od_ref[...], z_ref[...], g_ref[...]).astype(BF16)

    return pl.pallas_call(
        body, name=name, grid=(S // tr, H),
        in_specs=[_head_spec(tr, 0), _head_spec(tr, z_col0), _gain_spec()],
        out_specs=_head_spec(tr, 0), out_shape=jax.ShapeDtypeStruct((S, H * HEAD_DIM), BF16),
        compiler_params=_cparams(("parallel", "parallel")),
    )(od, proj, gain)


def _post_bwd(od, proj, z_col0, gain, do, do_col0, name):
    S = od.shape[0]
    H = N_DELTA_HEADS
    tr = _pick(S, 512)

    def body(od_ref, z_ref, g_ref, do_ref, dod_ref, dz_ref, dg_ref):
        @pl.when((pl.program_id(0) == 0) & (pl.program_id(1) == 0))
        def _():
            dg_ref[...] = jnp.zeros_like(dg_ref)

        _, vjp = jax.vjp(_gated_norm, od_ref[...], z_ref[...], g_ref[...])
        dod, dz, dg = vjp(do_ref[...])
        dod_ref[...] = dod
        dz_ref[...] = dz.astype(BF16)
        dg_ref[...] += dg

    return pl.pallas_call(
        body, name=name, grid=(S // tr, H),
        in_specs=[_head_spec(tr, 0), _head_spec(tr, z_col0), _gain_spec(), _head_spec(tr, do_col0)],
        out_specs=[_head_spec(tr, 0), _head_spec(tr, 0), _gain_spec()],
        out_shape=[jax.ShapeDtypeStruct((S, H * HEAD_DIM), F32), jax.ShapeDtypeStruct((S, H * HEAD_DIM), BF16),
                   jax.ShapeDtypeStruct((1, HEAD_DIM), F32)],
        compiler_params=_cparams(("arbitrary", "arbitrary")),
    )(od, proj, gain, do)


def _adam_math(w, g, m, v):
    m = ADAM_B1 * m + (1.0 - ADAM_B1) * g
    v = ADAM_B2 * v + (1.0 - ADAM_B2) * (g * g)
    m_hat = m / (1.0 - ADAM_B1 ** ADAM_STEP)
    v_hat = v / (1.0 - ADAM_B2 ** ADAM_STEP)
    delta = -ADAM_LR * (m_hat / (jnp.sqrt(v_hat) + ADAM_EPS) + ADAM_WD * w)
    return delta, m, v


def _adamw(w, g, m, v, name):
    R, C = w.shape
    tr = R if R * C * 4 <= (1 << 20) else _pick8(R, max(8, (1 << 20) // (C * 4)))

    def body(w_ref, g_ref, m_ref, v_ref, d_ref, nm_ref, nv_ref):
        d, nm, nv = _adam_math(w_ref[...], g_ref[...], m_ref[...], v_ref[...])
        d_ref[...] = d
        nm_ref[...] = nm
        nv_ref[...] = nv

    spec = pl.BlockSpec((tr, C), lambda i: (i, 0))
    shp = jax.ShapeDtypeStruct((R, C), F32)
    return pl.pallas_call(
        body, name=name, grid=(R // tr,), in_specs=[spec] * 4, out_specs=[spec] * 3, out_shape=[shp] * 3,
        compiler_params=_cparams(("parallel",)),
    )(w, g, m, v)


def _pick8(dim, pref):
    t = (min(dim, pref) // 8) * 8
    while t >= 8:
        if dim % t == 0:
            return t
        t -= 8
    return dim


def _adamw_outer(w, m, v, cond_t, rhs, name):
    R, C = w.shape
    tr = _pick8(R, 128)
    nb = cond_t.shape[1]
    lhs_t = cond_t

    def body(w_ref, m_ref, v_ref, a_ref, b_ref, g_ref, d_ref, nm_ref, nv_ref):
        g = _dot(_silu(a_ref[...]), b_ref[...])
        d, nm, nv = _adam_math(w_ref[...], g, m_ref[...], v_ref[...])
        g_ref[...] = g
        d_ref[...] = d
        nm_ref[...] = nm
        nv_ref[...] = nv

    spec = pl.BlockSpec((tr, C), lambda i: (i, 0))
    shp = jax.ShapeDtypeStruct((R, C), F32)
    return pl.pallas_call(
        body, name=name, grid=(R // tr,),
        in_specs=[spec, spec, spec, pl.BlockSpec((tr, nb), lambda i: (i, 0)), pl.BlockSpec((nb, C), lambda i: (0, 0))],
        out_specs=[spec] * 4, out_shape=[shp] * 4,
        compiler_params=_cparams(("parallel",)),
    )(w, m, v, lhs_t, rhs)


def _ada_fwd(cond, w, bias, name):
    a = cond
    nb, K = a.shape
    N = w.shape[1]
    tn = _pick(N, 512)

    def body(a_ref, w_ref, b_ref, o_ref):
        o_ref[...] = _dot(_silu(a_ref[...]), w_ref[...]) + b_ref[...]

    return pl.pallas_call(
        body, name=name, grid=(N // tn,),
        in_specs=[pl.BlockSpec((nb, K), lambda j: (0, 0)), pl.BlockSpec((K, tn), lambda j: (0, j)), pl.BlockSpec((1, tn), lambda j: (0, j))],
        out_specs=pl.BlockSpec((nb, tn), lambda j: (0, j)), out_shape=jax.ShapeDtypeStruct((nb, N), F32),
        compiler_params=_cparams(("parallel",)),
    )(a, w, bias)


def _add_cast(parts, out_dtypes, name):
    shape = parts[0].shape
    G, R, C = shape
    tr = _pick8(R, max(8, (1 << 20) // (C * 4)))
    n_in = len(parts)

    def body(*refs):
        acc = refs[0][...].astype(F32)
        for r in refs[1:n_in]:
            acc = acc + r[...].astype(F32)
        for o, dt in zip(refs[n_in:], out_dtypes):
            o[...] = acc.astype(dt)

    spec = pl.BlockSpec((1, tr, C), lambda g, i: (g, i, 0))
    outs = pl.pallas_call(
        body, name=name, grid=(G, R // tr), in_specs=[spec] * n_in, out_specs=[spec] * len(out_dtypes),
        out_shape=[jax.ShapeDtypeStruct(shape, dt) for dt in out_dtypes],
        compiler_params=_cparams(("parallel", "parallel")),
    )(*parts)
    return outs


def _me():
    return lax.axis_index("x"), lax.axis_index("y"), lax.axis_index("c")


def _xor_peer(k):
    x, y, c = _me()
    dx, dy, dc = (k >> 2) & 1, (k >> 1) & 1, k & 1
    return (x ^ dx if dx else x, y ^ dy if dy else y, c ^ dc if dc else c)


ANY = pl.BlockSpec(memory_space=pl.ANY)


def _all_gather_small(v, name):
    R, C = v.shape

    def body(v_ref, out_ref, send_sems, recv_sems):
        x, y, c = _me()
        mine = 4 * x + 2 * y + c
        out_ref[mine] = v_ref[...]
        copies = []
        for k in range(1, 8):
            cp = pltpu.make_async_remote_copy(src_ref=v_ref, dst_ref=out_ref.at[mine], send_sem=send_sems.at[k - 1],
                                              recv_sem=recv_sems.at[k - 1], device_id=_xor_peer(k), device_id_type=MESH)
            cp.start()
            copies.append(cp)
        for k in range(1, 8):
            px, py, pc = _xor_peer(k)
            pltpu.make_async_remote_copy(src_ref=v_ref, dst_ref=out_ref.at[4 * px + 2 * py + pc], send_sem=send_sems.at[k - 1],
                                         recv_sem=recv_sems.at[k - 1], device_id=_xor_peer(k), device_id_type=MESH).wait_recv()
        for cp in copies:
            cp.wait_send()

    return pl.pallas_call(
        body, name=name, out_shape=jax.ShapeDtypeStruct((8, R, C), F32),
        in_specs=[pl.BlockSpec(memory_space=pltpu.VMEM)], out_specs=pl.BlockSpec(memory_space=pltpu.VMEM),
        scratch_shapes=[pltpu.SemaphoreType.DMA((7,)), pltpu.SemaphoreType.DMA((7,))],
        compiler_params=pltpu.CompilerParams(vmem_limit_bytes=VMEM_LIMIT),
    )(v)


def _chip_peers():
    x, y, _ = _me()
    return [(1, (x, 1 - y)), (2, (1 - x, y)), (3, (1 - x, 1 - y))]


def _all_gather_shards(shards, name):
    n = len(shards)

    def body(*refs):
        ins, outs = refs[:n], refs[n:2 * n]
        send_sems, recv_sems = refs[2 * n:]
        x, y, c = _me()
        chip = 2 * x + y
        sib = (x, y, 1 - c)
        peers = _chip_peers()
        sends = []
        for t in range(n):
            half = ins[t].shape[0] // 2
            mine = pl.ds(c * half, half)
            for p, (k, (px, py)) in enumerate(peers):
                cp = pltpu.make_async_remote_copy(src_ref=ins[t].at[mine], dst_ref=outs[t].at[chip, mine],
                                                  send_sem=send_sems.at[6 * t + p], recv_sem=recv_sems.at[6 * t + p],
                                                  device_id=(px, py, c), device_id_type=MESH)
                cp.start()
                sends.append(cp)
        for t in range(n):
            half = ins[t].shape[0] // 2
            mine = pl.ds(c * half, half)
            for p, (k, (px, py)) in enumerate(peers):
                src_chip = 2 * px + py
                landed = outs[t].at[src_chip, mine]
                pltpu.make_async_remote_copy(src_ref=landed, dst_ref=landed, send_sem=send_sems.at[6 * t + p],
                                             recv_sem=recv_sems.at[6 * t + p], device_id=(px, py, c), device_id_type=MESH).wait_recv()
                fw = pltpu.make_async_remote_copy(src_ref=landed, dst_ref=landed, send_sem=send_sems.at[6 * t + 3 + p],
                                                  recv_sem=recv_sems.at[6 * t + 3 + p], device_id=sib, device_id_type=MESH)
                fw.start()
                sends.append(fw)
        for t in range(n):
            half = ins[t].shape[0] // 2
            theirs = pl.ds((1 - c) * half, half)
            for p, (k, (px, py)) in enumerate(peers):
                got = outs[t].at[2 * px + py, theirs]
                pltpu.make_async_remote_copy(src_ref=got, dst_ref=got, send_sem=send_sems.at[6 * t + 3 + p],
                                             recv_sem=recv_sems.at[6 * t + 3 + p], device_id=sib, device_id_type=MESH).wait_recv()
        for cp in sends:
            cp.wait_send()

    return pl.pallas_call(
        body, name=name,
        out_shape=[jax.ShapeDtypeStruct((4,) + s.shape, s.dtype) for s in shards],
        in_specs=[ANY] * n, out_specs=[ANY] * n,
        scratch_shapes=[pltpu.SemaphoreType.DMA((6 * n,)), pltpu.SemaphoreType.DMA((6 * n,))],
    )(*shards)


def _swap_halves_with_sibling(slabs, name):
    n = len(slabs)

    def body(*refs):
        ins, outs = refs[:n], refs[n:2 * n]
        send_sems, recv_sems = refs[2 * n:]
        x, y, c = _me()
        sib = (x, y, 1 - c)
        cps = []
        for t in range(n):
            half = ins[t].shape[1] // 2
            cp = pltpu.make_async_remote_copy(src_ref=ins[t].at[:, pl.ds((1 - c) * half, half)], dst_ref=outs[t],
                                              send_sem=send_sems.at[t], recv_sem=recv_sems.at[t], device_id=sib, device_id_type=MESH)
            cp.start()
            cps.append(cp)
        for cp in cps:
            cp.wait()

    return pl.pallas_call(
        body, name=name,
        out_shape=[jax.ShapeDtypeStruct((4, s.shape[1] // 2, s.shape[2]), s.dtype) for s in slabs],
        in_specs=[ANY] * n, out_specs=[ANY] * n,
        scratch_shapes=[pltpu.SemaphoreType.DMA((n,)), pltpu.SemaphoreType.DMA((n,))],
    )(*slabs)


def _exchange_between_chips(slabs, name):
    n = len(slabs)

    def body(*refs):
        ins, outs = refs[:n], refs[n:2 * n]
        send_sems, recv_sems = refs[2 * n:]
        x, y, c = _me()
        chip = 2 * x + y
        cps = []
        for t in range(n):
            for p, (k, (px, py)) in enumerate(_chip_peers()):
                cp = pltpu.make_async_remote_copy(src_ref=ins[t].at[2 * px + py], dst_ref=outs[t].at[chip],
                                                  send_sem=send_sems.at[3 * t + p], recv_sem=recv_sems.at[3 * t + p],
                                                  device_id=(px, py, c), device_id_type=MESH)
                cp.start()
                cps.append(cp)
        for t in range(n):
            for p, (k, (px, py)) in enumerate(_chip_peers()):
                got = outs[t].at[2 * px + py]
                pltpu.make_async_remote_copy(src_ref=got, dst_ref=got, send_sem=send_sems.at[3 * t + p],
                                             recv_sem=recv_sems.at[3 * t + p], device_id=(px, py, c), device_id_type=MESH).wait_recv()
        for cp in cps:
            cp.wait_send()

    return pl.pallas_call(
        body, name=name,
        out_shape=[jax.ShapeDtypeStruct(s.shape, s.dtype) for s in slabs],
        in_specs=[ANY] * n, out_specs=[ANY] * n,
        scratch_shapes=[pltpu.SemaphoreType.DMA((3 * n,)), pltpu.SemaphoreType.DMA((3 * n,))],
    )(*slabs)


def _send_halves_to_sibling(halves, name):
    n = len(halves)

    def body(*refs):
        ins, outs = refs[:n], refs[n:2 * n]
        send_sems, recv_sems = refs[2 * n:]
        x, y, c = _me()
        sib = (x, y, 1 - c)
        cps = []
        for t in range(n):
            cp = pltpu.make_async_remote_copy(src_ref=ins[t], dst_ref=outs[t], send_sem=send_sems.at[t],
                                              recv_sem=recv_sems.at[t], device_id=sib, device_id_type=MESH)
            cp.start()
            cps.append(cp)
        for cp in cps:
            cp.wait()

    return pl.pallas_call(
        body, name=name,
        out_shape=[jax.ShapeDtypeStruct(s.shape, s.dtype) for s in halves],
        in_specs=[ANY] * n, out_specs=[ANY] * n,
        scratch_shapes=[pltpu.SemaphoreType.DMA((n,)), pltpu.SemaphoreType.DMA((n,))],
    )(*halves)


HBM_SPEC = pl.BlockSpec(memory_space=pltpu.HBM)
SEM_SPEC = pl.BlockSpec(memory_space=pltpu.SEMAPHORE)
DATAFLOW = pltpu.SideEffectType.DATAFLOW_SIDE_EFFECTING


def _plan_gather_direct(src_refs, land_refs):
    x, y, c = _me()
    chip = 2 * x + y
    plan = []
    for s, land in zip(src_refs, land_refs):
        half = s.shape[0] // 2
        for _, (px, py) in _chip_peers():
            for pc in (c, 1 - c):
                plan.append((s.at[pl.ds(c * half, half)], land.at[chip, pl.ds(c * half, half)],
                             land.at[2 * px + py, pl.ds(pc * half, half)], (px, py, pc)))
    return plan


def _plan_scatter_direct(src_refs, land_refs):
    x, y, c = _me()
    plan = []
    for s, land in zip(src_refs, land_refs):
        half = s.shape[1] // 2
        for k in range(1, 8):
            px, py, pc = _xor_peer(k)
            plan.append((s.at[2 * px + py, pl.ds(pc * half, half)], land.at[4 * x + 2 * y + c],
                         land.at[4 * px + 2 * py + pc], (px, py, pc)))
    return plan


def _start_copies(srcs, lands, plan_fn, name):
    n = len(srcs)
    n_copies = len(srcs) * (7 if plan_fn is _plan_scatter_direct else 6)

    def body(*refs):
        send_sems, recv_sems, token = refs[2 * n], refs[2 * n + 1], refs[-1]
        for i, (src, dst, _, peer) in enumerate(plan_fn(refs[:n], refs[n:2 * n])):
            pltpu.make_async_remote_copy(src_ref=src, dst_ref=dst, send_sem=send_sems.at[i], recv_sem=recv_sems.at[i],
                                         device_id=peer, device_id_type=MESH).start()
        token[...] = jnp.zeros_like(token)

    arrays = list(srcs) + list(lands)
    outs = pl.pallas_call(
        body, name=name,
        out_shape=(pltpu.SemaphoreType.DMA((n_copies,)), pltpu.SemaphoreType.DMA((n_copies,)),
                   *[pltpu.HBM(a.shape, a.dtype) for a in arrays], jax.ShapeDtypeStruct((8, LANES), F32)),
        in_specs=[HBM_SPEC] * (2 * n),
        out_specs=(SEM_SPEC, SEM_SPEC, *[HBM_SPEC] * (2 * n), pl.BlockSpec(memory_space=pltpu.VMEM)),
        input_output_aliases={i: 2 + i for i in range(2 * n)},
        compiler_params=pltpu.CompilerParams(has_side_effects=DATAFLOW),
    )(*[pltpu.with_memory_space_constraint(a, pltpu.HBM) for a in arrays])
    return outs[0], outs[1], list(outs[2:2 + n]), list(outs[2 + n:2 + 2 * n]), outs[-1]


def _wait_copies(send_sems, recv_sems, srcs, lands, after, plan_fn, name):
    n = len(srcs)

    def body(*refs):
        send_sems, recv_sems = refs[2 * n], refs[2 * n + 1]
        for i, (src, _, arrival, peer) in enumerate(plan_fn(refs[:n], refs[n:2 * n])):
            cp = pltpu.make_async_remote_copy(src_ref=src, dst_ref=arrival, send_sem=send_sems.at[i], recv_sem=recv_sems.at[i],
                                              device_id=peer, device_id_type=MESH)
            cp.wait_send()
            cp.wait_recv()

    arrays = list(srcs) + list(lands)
    outs = pl.pallas_call(
        body, name=name,
        out_shape=tuple(pltpu.HBM(a.shape, a.dtype) for a in arrays),
        in_specs=[HBM_SPEC] * (2 * n) + [SEM_SPEC, SEM_SPEC, ANY],
        out_specs=tuple([HBM_SPEC] * (2 * n)),
        input_output_aliases={i: i for i in range(2 * n)},
        compiler_params=pltpu.CompilerParams(has_side_effects=DATAFLOW),
    )(*arrays, send_sems, recv_sems, after)
    return list(outs[n:])


def _rope_tables(positions):
    half = ROPE_DIM // 2
    S = positions.shape[0]
    inv_freq = ROPE_THETA ** (-jnp.arange(half, dtype=F32) / half)
    ang = positions.astype(F32)[:, None] * inv_freq
    cos, sin = jnp.cos(ang), jnp.sin(ang)
    zeros = functools.partial(jnp.zeros, dtype=F32)
    cc = jnp.concatenate([cos, cos, jnp.ones((S, HEAD_DIM - ROPE_DIM), F32)], axis=1)
    s1 = jnp.concatenate([-sin, zeros((S, HEAD_DIM - half))], axis=1)
    s2 = jnp.concatenate([zeros((S, half)), sin, zeros((S, HEAD_DIM - ROPE_DIM))], axis=1)
    return cc, s1, s2


def _ffn_fwd(x, gain, shift, scale, gate, w_gu, w_d, fs, tag):
    h = _pre_fwd(x, gain, shift, scale, tag + "_pre")
    ab = _matmul(h, w_gu, "nn", F32, tag + "_gate_up", tm=512, tn=1024, tk=4096)
    s = _swiglu_fwd(ab, fs, tag + "_swiglu")
    f = _matmul(s, w_d, "nn", F32, tag + "_down", tm=512, tn=1024, tk=8192)
    xn = _residual_fwd(x, gate, f, 0.5, tag + "_res")
    return xn, (x, h, ab, s, f)


def _ffn_bwd(dxn, saved, gain, shift, scale, gate, w_gu, w_d, fs, tag):
    x, h, ab, s, f = saved
    df, dgate = _residual_bwd(gate, f, dxn, 0.5, tag + "_res_bwd")
    ds = _matmul(df, w_d, "nt", F32, tag + "_down_dx", tm=1024, tn=1408, tk=4096)
    dw_d = _matmul(s, df, "tn", F32, tag + "_down_dw", tm=1408, tn=1024, tk=1024)
    dab = _swiglu_bwd(ab, ds, fs, tag + "_swiglu_bwd")
    dh = _matmul(dab, w_gu, "nt", F32, tag + "_gate_up_dx", tm=1024, tn=1024, tk=2816)
    dw_gu = _matmul(h, dab, "tn", F32, tag + "_gate_up_dw", tm=1024, tn=1024, tk=1024)
    dx, dgain, dshift, dscale = _pre_bwd(x, gain, shift, scale, dh, dxn, tag + "_pre_bwd")
    return dx, dw_gu, dw_d, dgain, dshift, dscale, dgate


def _flat_pad(parts, rows, cols):
    flat = jnp.concatenate([p.reshape(-1).astype(F32) for p in parts])
    return jnp.pad(flat, (0, rows * cols - flat.shape[0])).reshape(rows, cols)


def _cols_to_slabs(w, n):
    R, NC = w.shape
    return jnp.transpose(w.reshape(R, n, NC // n), (1, 0, 2))


def kernel(x, c, positions, w_ada, b_ada, ffn1_norm, ffn1_w_gate, ffn1_w_up, ffn1_w_down, mix_norm, w_in, conv_w, q_norm, k_norm, a_log, dt_bias, delta_out_norm, w_out, ffn2_norm, ffn2_w_gate, ffn2_w_up, ffn2_w_down, loss_target, m_w_ada, m_b_ada, m_ffn1_norm, m_ffn1_w_gate, m_ffn1_w_up, m_ffn1_w_down, m_mix_norm, m_w_in, m_conv_w, m_q_norm, m_k_norm, m_a_log, m_dt_bias, m_delta_out_norm, m_w_out, m_ffn2_norm, m_ffn2_w_gate, m_ffn2_w_up, m_ffn2_w_down, v_w_ada, v_b_ada, v_ffn1_norm, v_ffn1_w_gate, v_ffn1_w_up, v_ffn1_w_down, v_mix_norm, v_w_in, v_conv_w, v_q_norm, v_k_norm, v_a_log, v_dt_bias, v_delta_out_norm, v_w_out, v_ffn2_norm, v_ffn2_w_gate, v_ffn2_w_up, v_ffn2_w_down):
    xi, yi, ci = _me()
    chip = 2 * xi + yi
    dev = 2 * chip + ci
    xs = x[0]
    S, D = xs.shape
    HA, HD = N_ATTN_HEADS, N_DELTA_HEADS
    fs = ffn1_w_gate.shape[2]
    n_mod_shard = w_ada.shape[2]
    in_shard = w_in.shape[2]
    in_width = 4 * in_shard
    in_pad = -(-in_width // LANES) * LANES
    conv_shard = conv_w.shape[2]
    conv_width = 4 * conv_shard

    pack0 = jnp.zeros((8, max(D, conv_shard)), F32)
    pack0 = pack0.at[0, :D].set(c[0]).at[1:1 + CONV_WIDTH, :conv_shard].set(conv_w[0])
    got0 = _all_gather_small(pack0, "gather_cond")
    c_all = got0[:, 0, :D]
    conv_full = jnp.transpose(got0[::2, 1:1 + CONV_WIDTH, :conv_shard], (1, 0, 2)).reshape(CONV_WIDTH, conv_width)
    b_ada_mine = lax.dynamic_slice(b_ada, (0, chip * n_mod_shard), (1, n_mod_shard))
    mod_part = _ada_fwd(c_all, w_ada[0], b_ada_mine, "ada_fwd")
    got1 = _all_gather_small(mod_part, "gather_mod")
    mod = lax.dynamic_index_in_dim(got1[::2], dev, axis=1, keepdims=False).reshape(1, 4 * n_mod_shard)
    sh1, sc1, gt1, sh2, sc2, gt2, sh3, sc3, gt3 = [mod[:, i * D:(i + 1) * D] for i in range(N_MOD)]

    shards = [w[0].astype(BF16) for w in (ffn1_w_gate, ffn1_w_up, ffn1_w_down, w_in, w_out, ffn2_w_gate, ffn2_w_up, ffn2_w_down)]
    gathered = _all_gather_shards(shards[:3], "gather_weights")
    g1g, g1u, g1d = [lax.dynamic_update_index_in_dim(g, s, chip, 0) for g, s in zip(gathered, shards[:3])]
    zones = [lax.dynamic_update_index_in_dim(lax.empty((4,) + s.shape, BF16), s, chip, 0) for s in shards[3:]]
    ag_in = _start_copies(shards[3:4], zones[:1], _plan_gather_direct, "gather_in_start")
    ag_rest = _start_copies(shards[4:], zones[1:], _plan_gather_direct, "gather_rest_start")
    sh1 = sh1 + ag_in[4][0, 0] + ag_rest[4][0, 0]

    def gate_up(gg, gu):
        return jnp.transpose(jnp.concatenate([gg, gu], axis=2), (1, 0, 2)).reshape(D, 8 * fs)

    w_gu1, w_d1 = gate_up(g1g, g1u), g1d.reshape(4 * fs, D)

    x1, saved1 = _ffn_fwd(xs, ffn1_norm, sh1, sc1, gt1, w_gu1, w_d1, fs, "ffn1")
    (gin,) = _wait_copies(ag_in[0], ag_in[1], ag_in[2], ag_in[3], x1, _plan_gather_direct, "gather_in_wait")
    w_in_f = jnp.pad(jnp.transpose(gin, (1, 0, 2)).reshape(D, in_width), ((0, 0), (0, in_pad - in_width)))

    cc, s1, s2 = _rope_tables(positions[0])
    alog_row = jnp.pad(a_log, ((0, 0), (0, LANES - HD)))
    dtb_row = jnp.pad(dt_bias, ((0, 0), (0, LANES - HD)))
    col_k, col_v, col_d, col_z, col_ab = HA, 2 * HA, 3 * HA, 3 * HA + 3 * HD, 3 * HA + 4 * HD
    h2 = _pre_fwd(x1, mix_norm, sh2, sc2, "mix_pre")
    proj = _matmul(h2, w_in_f, "nn", F32, "mix_in_proj", tm=512, tn=1024, tk=4096)
    qa, ka = _attn_prep_fwd(proj, q_norm, k_norm, cc, s1, s2, "attn_prep")
    oa, oa_b, lse = _attn_fwd(qa, ka, proj, col_v, "attn_fwd")
    dqkv = _conv_fwd(proj, col_d, conv_width, conv_full, "conv_fwd")
    *prep, t_inv = _delta_prep_fwd(dqkv, proj, col_ab, alog_row, dtb_row, "delta_prep")
    od_raw, states = _delta_scan_fwd(*prep, "delta_scan")
    od = _post_fwd(od_raw, proj, col_z, delta_out_norm, "delta_post")
    o = jnp.concatenate([oa_b, od], axis=1)
    gout, g2g, g2u, g2d = _wait_copies(ag_rest[0], ag_rest[1], ag_rest[2], ag_rest[3], o, _plan_gather_direct, "gather_rest_wait")
    w_out_f = gout.reshape(-1, D)
    w_gu2, w_d2 = gate_up(g2g, g2u), g2d.reshape(4 * fs, D)
    mo = _matmul(o, w_out_f, "nn", F32, "mix_out_proj", tm=512, tn=1024, tk=4096)
    x2 = _residual_fwd(x1, gt2, mo, 1.0, "mix_res")

    x3, saved3 = _ffn_fwd(x2, ffn2_norm, sh3, sc3, gt3, w_gu2, w_d2, fs, "ffn2")
    loss_part, dy = _loss_head(x3, loss_target[0], "loss_head")
    loss = lax.psum(loss_part[0, 0], ("x", "y", "c"))

    dx2, dw_gu2, dw_d2, dgain3, dsh3, dsc3, dgt3 = _ffn_bwd(dy, saved3, ffn2_norm, sh3, sc3, gt3, w_gu2, w_d2, fs, "ffn2")

    def scatter_start(slabs32, name):
        slabs16 = [s.astype(BF16) for s in slabs32]
        zones = []
        for s in slabs16:
            half = s.shape[1] // 2
            own = lax.dynamic_slice(s, (chip, ci * half, 0), (1, half, s.shape[2]))
            zones.append(lax.dynamic_update_slice(lax.empty((8, half, s.shape[2]), BF16), own, (dev, 0, 0)))
        return _start_copies(slabs16, zones, _plan_scatter_direct, name)

    rs_ffn2 = scatter_start([_cols_to_slabs(dw_gu2, 4), dw_d2.reshape(4, fs, D)], "rs_ffn2_start")
    dmo, dgt2 = _residual_bwd(gt2 + rs_ffn2[4][0, 0], mo, dx2, 1.0, "mix_res_bwd")
    do = _matmul(dmo, w_out_f, "nt", F32, "mix_out_dx", tm=1024, tn=1024, tk=4096)
    dw_out = _matmul(o, dmo, "tn", F32, "mix_out_dw", tm=1024, tn=1024, tk=1024)
    dq = _attn_bwd_dq(qa, ka, proj, col_v, oa, lse, do, 0, "attn_bwd_dq")
    dk, dv = _attn_bwd_dkv(qa, ka, proj, col_v, oa, lse, do, 0, "attn_bwd_dkv")
    dpq, dpk, dq_gain, dk_gain = _attn_prep_bwd(proj, q_norm, k_norm, cc, s1, s2, dq, dk, "attn_prep_bwd")
    dod, dz, ddn_gain = _post_bwd(od_raw, proj, col_z, delta_out_norm, do, HA, "delta_post_bwd")
    cots = _delta_scan_bwd(*prep, states, dod, "delta_scan_bwd")
    ddq, ddk, ddv, dab, dalog, ddtb = _delta_prep_bwd(dqkv, proj, col_ab, alog_row, dtb_row, cots, t_inv, "delta_prep_bwd")
    dconv_in, dconv_w = _conv_bwd(proj, col_d, conv_width, conv_full, jnp.concatenate([ddq, ddk, ddv], axis=1), "conv_bwd")
    dproj = jnp.concatenate([dpq, dpk, dv, dconv_in, dz, dab], axis=1)
    dh2 = _matmul(dproj, w_in_f, "nt", F32, "mix_in_dx", tm=1024, tn=1024, tk=2816)
    dw_in = _matmul(h2, dproj, "tn", F32, "mix_in_dw", tm=1024, tn=1024, tk=1024)
    rs_mix = scatter_start([_cols_to_slabs(dw_in[:, :in_width], 4), dw_out.reshape(4, -1, D)], "rs_mix_start")
    dx1, dgain2, dsh2, dsc2 = _pre_bwd(x1, mix_norm, sh2, sc2, dh2, dx2, "mix_pre_bwd")

    dx0, dw_gu1, dw_d1, dgain1, dsh1, dsc1, dgt1 = _ffn_bwd(dx1, saved1, ffn1_norm, sh1, sc1, gt1 + rs_mix[4][0, 0], w_gu1, w_d1, fs, "ffn1")

    n_small = N_MOD * D + 3 * D + 5 * LANES + CONV_WIDTH * conv_width
    cols_small = -(-n_small // (8 * LANES)) * LANES
    small = _flat_pad([dsh1, dsc1, dgt1, dsh2, dsc2, dgt2, dsh3, dsc3, dgt3, dgain1, dgain2, dgain3,
                       dq_gain, dk_gain, dalog, ddtb, ddn_gain, dconv_w], 8, cols_small)
    got2 = _all_gather_small(small, "gather_small_grads")
    small_sum = _add_cast([got2[d:d + 1] for d in range(8)], [F32], "sum_small_grads")[0].reshape(-1)
    dmod_all = got2.reshape(8, -1)[:, :N_MOD * D]
    off = [0]

    def take(n):
        off[0] += n
        return small_sum[off[0] - n:off[0]]

    g_b_ada = take(N_MOD * D).reshape(1, -1)
    g_ffn1_norm, g_mix_norm, g_ffn2_norm = take(D).reshape(1, D), take(D).reshape(1, D), take(D).reshape(1, D)
    g_q_norm, g_k_norm = take(LANES).reshape(1, -1), take(LANES).reshape(1, -1)
    g_a_log, g_dt_bias = take(LANES)[:HD].reshape(1, HD), take(LANES)[:HD].reshape(1, HD)
    g_dn = take(LANES).reshape(1, -1)
    g_conv_full = take(CONV_WIDTH * conv_width).reshape(CONV_WIDTH, conv_width)
    g_conv = lax.dynamic_slice(g_conv_full, (0, chip * conv_shard), (CONV_WIDTH, conv_shard))

    slabs = [_cols_to_slabs(dw_gu1, 4), dw_d1.reshape(4, fs, D)]
    from_sibling = _swap_halves_with_sibling(slabs, "rs_sibling_swap")
    chip_sums32, chip_sums16 = [], []
    for t, (slab, other) in enumerate(zip(slabs, from_sibling)):
        half = slab.shape[1] // 2
        mine = lax.dynamic_slice_in_dim(slab, ci * half, half, axis=1)
        p32, p16 = _add_cast([mine, other], [F32, BF16], "rs_chip_sum_%d" % t)
        chip_sums32.append(p32)
        chip_sums16.append(p16)
    from_chips = _exchange_between_chips(chip_sums16, "rs_chip_exchange")
    halves = []
    for t, (p32, got) in enumerate(zip(chip_sums32, from_chips)):
        parts = [lax.dynamic_index_in_dim(p32, chip, axis=0, keepdims=True)]
        parts += [lax.dynamic_index_in_dim(got, (chip + k) % 4, axis=0, keepdims=True) for k in (1, 2, 3)]
        halves.append(_add_cast(parts, [F32], "rs_total_%d" % t)[0][0])
    arrived = _wait_copies(rs_mix[0], rs_mix[1], rs_mix[2], rs_mix[3], dx0, _plan_scatter_direct, "rs_mix_wait")
    arrived += _wait_copies(rs_ffn2[0], rs_ffn2[1], rs_ffn2[2], rs_ffn2[3], dx0, _plan_scatter_direct, "rs_ffn2_wait")
    for t, zone in enumerate(arrived):
        halves.append(_add_cast([zone[d:d + 1] for d in range(8)], [F32], "rs_total_%d" % (t + 2))[0][0])
    theirs = _send_halves_to_sibling(halves, "rs_sibling_join")
    g_gu1, g_d1, g_in, g_out, g_gu2, g_d2 = [
        jnp.where(ci == 0, jnp.concatenate([mine, other], axis=0), jnp.concatenate([other, mine], axis=0))
        for mine, other in zip(halves, theirs)]

    res = {}

    def upd(name, w, g, m, v):
        d, nm, nv = _adamw(w[0], g, m[0], v[0], "adamw_" + name)
        res[name] = (g[None], d[None], nm[None], nv[None])

    upd("ffn1_w_gate", ffn1_w_gate, g_gu1[:, :fs], m_ffn1_w_gate, v_ffn1_w_gate)
    upd("ffn1_w_up", ffn1_w_up, g_gu1[:, fs:], m_ffn1_w_up, v_ffn1_w_up)
    upd("ffn1_w_down", ffn1_w_down, g_d1, m_ffn1_w_down, v_ffn1_w_down)
    upd("w_in", w_in, g_in, m_w_in, v_w_in)
    upd("w_out", w_out, g_out, m_w_out, v_w_out)
    upd("ffn2_w_gate", ffn2_w_gate, g_gu2[:, :fs], m_ffn2_w_gate, v_ffn2_w_gate)
    upd("ffn2_w_up", ffn2_w_up, g_gu2[:, fs:], m_ffn2_w_up, v_ffn2_w_up)
    upd("ffn2_w_down", ffn2_w_down, g_d2, m_ffn2_w_down, v_ffn2_w_down)
    upd("conv_w", conv_w, g_conv, m_conv_w, v_conv_w)

    dmod_mine = lax.dynamic_slice(dmod_all, (0, chip * n_mod_shard), (8, n_mod_shard))
    g, d, nm, nv = _adamw_outer(w_ada[0], m_w_ada[0], v_w_ada[0], jnp.transpose(c_all), dmod_mine, "adamw_w_ada")
    res["w_ada"] = (g[None], d[None], nm[None], nv[None])

    rep = [("b_ada", b_ada, g_b_ada, m_b_ada, v_b_ada), ("ffn1_norm", ffn1_norm, g_ffn1_norm, m_ffn1_norm, v_ffn1_norm),
           ("mix_norm", mix_norm, g_mix_norm, m_mix_norm, v_mix_norm), ("ffn2_norm", ffn2_norm, g_ffn2_norm, m_ffn2_norm, v_ffn2_norm),
           ("q_norm", q_norm, g_q_norm, m_q_norm, v_q_norm), ("k_norm", k_norm, g_k_norm, m_k_norm, v_k_norm),
           ("a_log", a_log, g_a_log, m_a_log, v_a_log), ("dt_bias", dt_bias, g_dt_bias, m_dt_bias, v_dt_bias),
           ("delta_out_norm", delta_out_norm, g_dn, m_delta_out_norm, v_delta_out_norm)]
    n_rep = sum(-(-r[1].shape[1] // LANES) * LANES for r in rep)
    cols_rep = -(-n_rep // (8 * LANES)) * LANES

    def pack_rep(idx):
        return _flat_pad([jnp.pad(r[idx], ((0, 0), (0, -r[idx].shape[1] % LANES))) for r in rep], 8, cols_rep)

    d_rep, nm_rep, nv_rep = [a.reshape(-1) for a in _adamw(pack_rep(1), pack_rep(2), pack_rep(3), pack_rep(4), "adamw_small")]
    o2 = 0
    for name, w, g, _, _ in rep:
        n = w.shape[1]
        res[name] = (g, d_rep[o2:o2 + n].reshape(1, n), nm_rep[o2:o2 + n].reshape(1, n), nv_rep[o2:o2 + n].reshape(1, n))
        o2 += -(-n // LANES) * LANES

    order = ["w_ada", "b_ada", "ffn1_norm", "ffn1_w_gate", "ffn1_w_up", "ffn1_w_down", "mix_norm", "w_in", "conv_w", "q_norm",
             "k_norm", "a_log", "dt_bias", "delta_out_norm", "w_out", "ffn2_norm", "ffn2_w_gate", "ffn2_w_up", "ffn2_w_down"]
    return (loss, dx0[None], *[res[n][0] for n in order], *[res[n][1] for n in order],
            *[res[n][2] for n in order], *[res[n][3] for n in order])
```

```python
import functools
import math

import jax
import jax.numpy as jnp
from jax import lax
from jax.experimental import pallas as pl
from jax.experimental.pallas import tpu as pltpu

F32 = jnp.float32
BF16 = jnp.bfloat16
MESH = pl.DeviceIdType.MESH

HEAD_DIM = 128
N_ATTN_HEADS = 8
N_DELTA_HEADS = 8
DILATED_PATTERNS = ((128, 1), (512, 4), (2048, 16))
MAX_WINDOW = 2048
ROPE_THETA = 500000.0
ROPE_DIM = HEAD_DIM // 4
CONV_WIDTH = 4
CHUNK = 64
NORM_EPS = 1e-6
N_MOD = 9
ADAM_LR = 0.001
ADAM_B1 = 0.9
ADAM_B2 = 0.999
ADAM_EPS = 1e-08
ADAM_WD = 0.01
ADAM_STEP = 10

LANES = 128
VMEM_LIMIT = 56 * 1024 * 1024
ATTN_TILE = 512
HIGHEST = lax.Precision.HIGHEST


def _cparams(sem=None):
    return pltpu.CompilerParams(dimension_semantics=sem, vmem_limit_bytes=VMEM_LIMIT)


def _pick(dim, pref):
    if dim <= pref:
        return dim
    t = (pref // LANES) * LANES
    while t >= LANES:
        if dim % t == 0:
            return t
        t -= LANES
    return dim


def _sigmoid(x):
    return 1.0 / (1.0 + jnp.exp(-x))


def _silu(x):
    return x * _sigmoid(x)


def _softplus(x):
    return jnp.maximum(x, 0.0) + jnp.log(1.0 + jnp.exp(-jnp.abs(x)))


def _rms(x, gain):
    return x * lax.rsqrt(jnp.mean(x * x, axis=-1, keepdims=True) + NORM_EPS) * gain


def _l2(x):
    return x * lax.rsqrt(jnp.sum(x * x, axis=-1, keepdims=True) + NORM_EPS)


def _modulate(x, gain, shift, scale):
    return _rms(x, gain) * (1.0 + scale) + shift


def _dot(a, b):
    return lax.dot_general(a, b, (((1,), (0,)), ((), ())), precision=HIGHEST, preferred_element_type=F32)


def _bdot(a, b, dims):
    return lax.dot_general(a.astype(BF16), b.astype(BF16), (dims, ((), ())), preferred_element_type=F32)


_NN, _NT, _TN = ((1,), (0,)), ((1,), (1,)), ((0,), (0,))
HIGH = lax.Precision.HIGH


def _dot3(a, b, dims=_NN):
    return lax.dot_general(a, b, (dims, ((), ())), precision=HIGH, preferred_element_type=F32)


@jax.custom_vjp
def _mm_nn(a, b):
    return _bdot(a, b, _NN)


_mm_nn.defvjp(lambda a, b: (_bdot(a, b, _NN), (a, b)),
              lambda res, g: (_bdot(g, res[1], _NT), _bdot(res[0], g, _TN)))


@jax.custom_vjp
def _mm_nt(a, b):
    return _bdot(a, b, _NT)


_mm_nt.defvjp(lambda a, b: (_bdot(a, b, _NT), (a, b)),
              lambda res, g: (_bdot(g, res[1], _NN), _bdot(g, res[0], _TN)))


@jax.custom_vjp
def _tri_inv_saved(a, t_inv):
    return t_inv


_tri_inv_saved.defvjp(lambda a, t_inv: (t_inv, t_inv),
                      lambda t_inv, g: (-_dot3(t_inv, _dot3(g, t_inv, _NT), _TN), jnp.zeros_like(t_inv)))


_MM_DIMS = {"nn": ((1,), (0,)), "nt": ((1,), (1,)), "tn": ((0,), (0,))}


def _matmul(a, b, mode, out_dtype, name, tm=1024, tn=1024, tk=1024):
    if mode == "nn":
        (M, K), (_, N) = a.shape, b.shape
    elif mode == "nt":
        (M, K), (N, _) = a.shape, b.shape
    else:
        (K, M), (_, N) = a.shape, b.shape
    tm, tn, tk = _pick(M, tm), _pick(N, tn), _pick(K, tk)
    nk = K // tk
    dims = _MM_DIMS[mode]

    def body(a_ref, b_ref, o_ref, acc_ref):
        k = pl.program_id(2)
        p = _bdot(a_ref[...], b_ref[...], dims)

        @pl.when(k == 0)
        def _():
            acc_ref[...] = p

        @pl.when(k > 0)
        def _():
            acc_ref[...] += p

        @pl.when(k == nk - 1)
        def _():
            o_ref[...] = acc_ref[...].astype(out_dtype)

    a_spec = pl.BlockSpec((tk, tm), lambda i, j, k: (k, i)) if mode == "tn" else pl.BlockSpec((tm, tk), lambda i, j, k: (i, k))
    b_spec = pl.BlockSpec((tn, tk), lambda i, j, k: (j, k)) if mode == "nt" else pl.BlockSpec((tk, tn), lambda i, j, k: (k, j))
    return pl.pallas_call(
        body, name=name, grid=(M // tm, N // tn, nk),
        in_specs=[a_spec, b_spec], out_specs=pl.BlockSpec((tm, tn), lambda i, j, k: (i, j)),
        out_shape=jax.ShapeDtypeStruct((M, N), out_dtype),
        scratch_shapes=[pltpu.VMEM((tm, tn), F32)],
        compiler_params=_cparams(("parallel", "parallel", "arbitrary")),
    )(a, b)


def _row_spec(tr, d):
    return pl.BlockSpec((tr, d), lambda i: (i, 0))


def _vec_spec(d):
    return pl.BlockSpec((1, d), lambda i: (0, 0))


def _pre_fwd(x, gain, shift, scale, name):
    S, D = x.shape
    tr = _pick(S, 256)

    def body(x_ref, g_ref, sh_ref, sc_ref, h_ref):
        h_ref[...] = _modulate(x_ref[...], g_ref[...], sh_ref[...], sc_ref[...]).astype(BF16)

    return pl.pallas_call(
        body, name=name, grid=(S // tr,),
        in_specs=[_row_spec(tr, D), _vec_spec(D), _vec_spec(D), _vec_spec(D)],
        out_specs=_row_spec(tr, D), out_shape=jax.ShapeDtypeStruct((S, D), BF16),
        compiler_params=_cparams(("parallel",)),
    )(x, gain, shift, scale)


def _pre_bwd(x, gain, shift, scale, dh, dx_in, name):
    S, D = x.shape
    tr = _pick(S, 256)

    def body(x_ref, g_ref, sh_ref, sc_ref, dh_ref, dxin_ref, dx_ref, dg_ref, dsh_ref, dsc_ref):
        _, vjp = jax.vjp(_modulate, x_ref[...], g_ref[...], sh_ref[...], sc_ref[...])
        dx, dg, dsh, dsc = vjp(dh_ref[...])
        dx_ref[...] = dxin_ref[...] + dx

        @pl.when(pl.program_id(0) == 0)
        def _():
            dg_ref[...] = jnp.zeros_like(dg_ref)
            dsh_ref[...] = jnp.zeros_like(dsh_ref)
            dsc_ref[...] = jnp.zeros_like(dsc_ref)

        dg_ref[...] += dg
        dsh_ref[...] += dsh
        dsc_ref[...] += dsc

    vec = jax.ShapeDtypeStruct((1, D), F32)
    return pl.pallas_call(
        body, name=name, grid=(S // tr,),
        in_specs=[_row_spec(tr, D), _vec_spec(D), _vec_spec(D), _vec_spec(D), _row_spec(tr, D), _row_spec(tr, D)],
        out_specs=[_row_spec(tr, D), _vec_spec(D), _vec_spec(D), _vec_spec(D)],
        out_shape=[jax.ShapeDtypeStruct((S, D), F32), vec, vec, vec],
        compiler_params=_cparams(("arbitrary",)),
    )(x, gain, shift, scale, dh, dx_in)


def _residual_fwd(x, gate, f, coef, name):
    S, D = x.shape
    tr = _pick(S, 256)

    def body(x_ref, g_ref, f_ref, o_ref):
        o_ref[...] = x_ref[...] + coef * g_ref[...] * f_ref[...]

    return pl.pallas_call(
        body, name=name, grid=(S // tr,),
        in_specs=[_row_spec(tr, D), _vec_spec(D), _row_spec(tr, D)],
        out_specs=_row_spec(tr, D), out_shape=jax.ShapeDtypeStruct((S, D), F32),
        compiler_params=_cparams(("parallel",)),
    )(x, gate, f)


def _residual_bwd(gate, f, dxn, coef, name):
    S, D = f.shape
    tr = _pick(S, 256)

    def body(g_ref, f_ref, d_ref, df_ref, dg_ref):
        d = d_ref[...]
        df_ref[...] = (coef * g_ref[...] * d).astype(BF16)

        @pl.when(pl.program_id(0) == 0)
        def _():
            dg_ref[...] = jnp.zeros_like(dg_ref)

        dg_ref[...] += jnp.sum(coef * f_ref[...] * d, axis=0, keepdims=True)

    return pl.pallas_call(
        body, name=name, grid=(S // tr,),
        in_specs=[_vec_spec(D), _row_spec(tr, D), _row_spec(tr, D)],
        out_specs=[_row_spec(tr, D), _vec_spec(D)],
        out_shape=[jax.ShapeDtypeStruct((S, D), BF16), jax.ShapeDtypeStruct((1, D), F32)],
        compiler_params=_cparams(("arbitrary",)),
    )(gate, f, dxn)


def _swiglu_fn(a, b):
    return _silu(a) * b


def _swiglu_fwd(ab, fs, name):
    S, F2 = ab.shape
    tr = _pick(S, 256)

    def body(ab_ref, s_ref):
        s_ref[...] = _swiglu_fn(ab_ref[:, :fs], ab_ref[:, fs:]).astype(BF16)

    return pl.pallas_call(
        body, name=name, grid=(S // tr, F2 // (2 * fs)),
        in_specs=[pl.BlockSpec((tr, 2 * fs), lambda i, j: (i, j))],
        out_specs=pl.BlockSpec((tr, fs), lambda i, j: (i, j)),
        out_shape=jax.ShapeDtypeStruct((S, F2 // 2), BF16),
        compiler_params=_cparams(("parallel", "parallel")),
    )(ab)


def _swiglu_bwd(ab, ds, fs, name):
    S, F2 = ab.shape
    tr = _pick(S, 256)

    def body(ab_ref, ds_ref, dab_ref):
        _, vjp = jax.vjp(_swiglu_fn, ab_ref[:, :fs], ab_ref[:, fs:])
        da, db = vjp(ds_ref[...])
        dab_ref[:, :fs] = da.astype(BF16)
        dab_ref[:, fs:] = db.astype(BF16)

    return pl.pallas_call(
        body, name=name, grid=(S // tr, F2 // (2 * fs)),
        in_specs=[pl.BlockSpec((tr, 2 * fs), lambda i, j: (i, j)), pl.BlockSpec((tr, fs), lambda i, j: (i, j))],
        out_specs=pl.BlockSpec((tr, 2 * fs), lambda i, j: (i, j)),
        out_shape=jax.ShapeDtypeStruct((S, F2), BF16),
        compiler_params=_cparams(("parallel", "parallel")),
    )(ab, ds)


def _loss_head(y, target, name):
    S, D = y.shape
    tr = _pick(S, 256)

    def body(y_ref, t_ref, l_ref, dy_ref):
        e = y_ref[...] - t_ref[...]
        dy_ref[...] = e * (1.0 / D)

        @pl.when(pl.program_id(0) == 0)
        def _():
            l_ref[...] = jnp.zeros_like(l_ref)

        l_ref[...] += jnp.sum(jnp.sum(e * e, axis=-1, keepdims=True), axis=0, keepdims=True) * (0.5 / D)

    return pl.pallas_call(
        body, name=name, grid=(S // tr,),
        in_specs=[_row_spec(tr, D), _row_spec(tr, D)],
        out_specs=[pl.BlockSpec((1, 1), lambda i: (0, 0)), _row_spec(tr, D)],
        out_shape=[jax.ShapeDtypeStruct((1, 1), F32), jax.ShapeDtypeStruct((S, D), F32)],
        compiler_params=_cparams(("arbitrary",)),
    )(y, target)


def _rope(y, cc, s1, s2):
    return y * cc + pltpu.roll(y, LANES - ROPE_DIM // 2, 1) * s1 + pltpu.roll(y, ROPE_DIM // 2, 1) * s2


def _rope_t(d, cc, s1, s2):
    return d * cc + pltpu.roll(d * s1, ROPE_DIM // 2, 1) + pltpu.roll(d * s2, LANES - ROPE_DIM // 2, 1)


def _head_spec(tr, col0):
    return pl.BlockSpec((tr, HEAD_DIM), lambda i, h: (i, col0 + h))


def _tab_spec(tr):
    return pl.BlockSpec((tr, HEAD_DIM), lambda i, h: (i, 0))


def _gain_spec():
    return pl.BlockSpec((1, HEAD_DIM), lambda i, h: (0, 0))


def _attn_prep_fwd(proj, q_gain, k_gain, cc, s1, s2, name):
    S = proj.shape[0]
    H = N_ATTN_HEADS
    tr = _pick(S, 512)

    def body(q_ref, k_ref, qg_ref, kg_ref, cc_ref, s1_ref, s2_ref, qo_ref, ko_ref):
        cc, s1, s2 = cc_ref[...], s1_ref[...], s2_ref[...]
        qo_ref[...] = _rope(_rms(q_ref[...], qg_ref[...]), cc, s1, s2).astype(BF16)
        ko_ref[...] = _rope(_rms(k_ref[...], kg_ref[...]), cc, s1, s2).astype(BF16)

    out = jax.ShapeDtypeStruct((S, H * HEAD_DIM), BF16)
    return pl.pallas_call(
        body, name=name, grid=(S // tr, H),
        in_specs=[_head_spec(tr, 0), _head_spec(tr, H), _gain_spec(), _gain_spec(), _tab_spec(tr), _tab_spec(tr), _tab_spec(tr)],
        out_specs=[_head_spec(tr, 0), _head_spec(tr, 0)], out_shape=[out, out],
        compiler_params=_cparams(("parallel", "parallel")),
    )(proj, proj, q_gain, k_gain, cc, s1, s2)


def _attn_prep_bwd(proj, q_gain, k_gain, cc, s1, s2, dq, dk, name):
    S = proj.shape[0]
    H = N_ATTN_HEADS
    tr = _pick(S, 512)

    def body(q_ref, k_ref, qg_ref, kg_ref, cc_ref, s1_ref, s2_ref, dq_ref, dk_ref, dpq_ref, dpk_ref, dqg_ref, dkg_ref):
        cc, s1, s2 = cc_ref[...], s1_ref[...], s2_ref[...]

        @pl.when((pl.program_id(0) == 0) & (pl.program_id(1) == 0))
        def _():
            dqg_ref[...] = jnp.zeros_like(dqg_ref)
            dkg_ref[...] = jnp.zeros_like(dkg_ref)

        _, vjp_q = jax.vjp(_rms, q_ref[...], qg_ref[...])
        dxq, dgq = vjp_q(_rope_t(dq_ref[...], cc, s1, s2))
        _, vjp_k = jax.vjp(_rms, k_ref[...], kg_ref[...])
        dxk, dgk = vjp_k(_rope_t(dk_ref[...], cc, s1, s2))
        dpq_ref[...] = dxq.astype(BF16)
        dpk_ref[...] = dxk.astype(BF16)
        dqg_ref[...] += dgq
        dkg_ref[...] += dgk

    out = jax.ShapeDtypeStruct((S, H * HEAD_DIM), BF16)
    gout = jax.ShapeDtypeStruct((1, HEAD_DIM), F32)
    return pl.pallas_call(
        body, name=name, grid=(S // tr, H),
        in_specs=[_head_spec(tr, 0), _head_spec(tr, H), _gain_spec(), _gain_spec(), _tab_spec(tr), _tab_spec(tr), _tab_spec(tr),
                  _head_spec(tr, 0), _head_spec(tr, 0)],
        out_specs=[_head_spec(tr, 0), _head_spec(tr, 0), _gain_spec(), _gain_spec()], out_shape=[out, out, gout, gout],
        compiler_params=_cparams(("arbitrary", "arbitrary")),
    )(proj, proj, q_gain, k_gain, cc, s1, s2, dq, dk)


def _multiplicity(j, t):
    ti = lax.broadcasted_iota(jnp.int32, (t, t), 0)
    si = lax.broadcasted_iota(jnp.int32, (t, t), 1)
    delta = j * t + ti - si
    cnt = jnp.zeros((t, t), F32)
    for window, dil in DILATED_PATTERNS:
        ok = (delta >= 0) & ((delta & (dil - 1)) == 0) & (delta <= window)
        cnt = cnt + ok.astype(F32)
    return cnt


_NEG = -1e30


def _attn_fwd(q, k, proj, v_col0, name):
    S = q.shape[0]
    H = N_ATTN_HEADS
    t = _pick(S, ATTN_TILE)
    nq = S // t
    nj = MAX_WINDOW // t + 1
    scale = HEAD_DIM ** -0.5

    def body(q_ref, k_ref, v_ref, o_ref, ob_ref, lse_ref, m_sc, l_sc, acc_sc):
        qb, j = pl.program_id(1), pl.program_id(2)

        @pl.when(j == 0)
        def _():
            m_sc[...] = jnp.full_like(m_sc, _NEG)
            l_sc[...] = jnp.zeros_like(l_sc)
            acc_sc[...] = jnp.zeros_like(acc_sc)

        @pl.when(qb - j >= 0)
        def _():
            cnt = _multiplicity(j, t)
            s = _bdot(q_ref[...], k_ref[...], ((1,), (1,))) * scale
            s = jnp.where(cnt > 0.0, s, _NEG)
            m_prev = m_sc[...]
            m_new = jnp.maximum(m_prev, jnp.max(s, axis=-1, keepdims=True))
            alpha = jnp.exp(m_prev - m_new)
            p = cnt * jnp.exp(s - m_new)
            l_sc[...] = alpha * l_sc[...] + jnp.sum(p, axis=-1, keepdims=True)
            acc_sc[...] = alpha * acc_sc[...] + _bdot(p, v_ref[...], ((1,), (0,)))
            m_sc[...] = m_new

        @pl.when(j == nj - 1)
        def _():
            o = acc_sc[...] / l_sc[...]
            o_ref[...] = o
            ob_ref[...] = o.astype(BF16)
            lse_ref[...] = jnp.broadcast_to(m_sc[...] + jnp.log(l_sc[...]), (t, HEAD_DIM))

    qspec = pl.BlockSpec((t, HEAD_DIM), lambda h, i, j: (i, h))
    kspec = pl.BlockSpec((t, HEAD_DIM), lambda h, i, j: (jnp.maximum(i - j, 0), h))
    vspec = pl.BlockSpec((t, HEAD_DIM), lambda h, i, j: (jnp.maximum(i - j, 0), v_col0 + h))
    return pl.pallas_call(
        body, name=name, grid=(H, nq, nj),
        in_specs=[qspec, kspec, vspec], out_specs=[qspec, qspec, qspec],
        out_shape=[jax.ShapeDtypeStruct((S, H * HEAD_DIM), F32), jax.ShapeDtypeStruct((S, H * HEAD_DIM), BF16),
                   jax.ShapeDtypeStruct((S, H * HEAD_DIM), F32)],
        scratch_shapes=[pltpu.VMEM((t, 1), F32), pltpu.VMEM((t, 1), F32), pltpu.VMEM((t, HEAD_DIM), F32)],
        compiler_params=_cparams(("parallel", "parallel", "arbitrary")),
    )(q, k, proj)


def _attn_probs(q, k, lse, j, t, scale):
    cnt = _multiplicity(j, t)
    s = _bdot(q, k, ((1,), (1,))) * scale
    s = jnp.where(cnt > 0.0, s, _NEG)
    return cnt * jnp.exp(s - lse)


def _attn_bwd_dq(q, k, proj, v_col0, o, lse, do, do_col0, name):
    S = q.shape[0]
    H = N_ATTN_HEADS
    t = _pick(S, ATTN_TILE)
    nq = S // t
    nj = MAX_WINDOW // t + 1
    scale = HEAD_DIM ** -0.5

    def body(q_ref, k_ref, v_ref, o_ref, lse_ref, do_ref, dq_ref, acc_sc):
        qb, j = pl.program_id(1), pl.program_id(2)

        @pl.when(j == 0)
        def _():
            acc_sc[...] = jnp.zeros_like(acc_sc)

        @pl.when(qb - j >= 0)
        def _():
            do = do_ref[...]
            dsum = jnp.sum(do * o_ref[...], axis=-1, keepdims=True)
            lse = jnp.max(lse_ref[...], axis=-1, keepdims=True)
            p = _attn_probs(q_ref[...], k_ref[...], lse, j, t, scale)
            dp = _bdot(do, v_ref[...], ((1,), (1,)))
            ds = p * (dp - dsum)
            acc_sc[...] += _bdot(ds, k_ref[...], ((1,), (0,))) * scale

        @pl.when(j == nj - 1)
        def _():
            dq_ref[...] = acc_sc[...]

    qspec = pl.BlockSpec((t, HEAD_DIM), lambda h, i, j: (i, h))
    dospec = pl.BlockSpec((t, HEAD_DIM), lambda h, i, j: (i, do_col0 + h))
    kspec = pl.BlockSpec((t, HEAD_DIM), lambda h, i, j: (jnp.maximum(i - j, 0), h))
    vspec = pl.BlockSpec((t, HEAD_DIM), lambda h, i, j: (jnp.maximum(i - j, 0), v_col0 + h))
    return pl.pallas_call(
        body, name=name, grid=(H, nq, nj),
        in_specs=[qspec, kspec, vspec, qspec, qspec, dospec], out_specs=qspec,
        out_shape=jax.ShapeDtypeStruct((S, H * HEAD_DIM), F32),
        scratch_shapes=[pltpu.VMEM((t, HEAD_DIM), F32)],
        compiler_params=_cparams(("parallel", "parallel", "arbitrary")),
    )(q, k, proj, o, lse, do)


def _attn_bwd_dkv(q, k, proj, v_col0, o, lse, do, do_col0, name):
    S = q.shape[0]
    H = N_ATTN_HEADS
    t = _pick(S, ATTN_TILE)
    nq = S // t
    nj = MAX_WINDOW // t + 1
    scale = HEAD_DIM ** -0.5

    def body(q_ref, k_ref, v_ref, o_ref, lse_ref, do_ref, dk_ref, dv_ref, dk_sc, dv_sc):
        kb, j = pl.program_id(1), pl.program_id(2)

        @pl.when(j == 0)
        def _():
            dk_sc[...] = jnp.zeros_like(dk_sc)
            dv_sc[...] = jnp.zeros_like(dv_sc)

        @pl.when(kb + j < nq)
        def _():
            do = do_ref[...]
            dsum = jnp.sum(do * o_ref[...], axis=-1, keepdims=True)
            lse = jnp.max(lse_ref[...], axis=-1, keepdims=True)
            p = _attn_probs(q_ref[...], k_ref[...], lse, j, t, scale)
            dp = _bdot(do, v_ref[...], ((1,), (1,)))
            ds = p * (dp - dsum)
            dv_sc[...] += _bdot(p, do, ((0,), (0,)))
            dk_sc[...] += _bdot(ds, q_ref[...], ((0,), (0,))) * scale

        @pl.when(j == nj - 1)
        def _():
            dk_ref[...] = dk_sc[...]
            dv_ref[...] = dv_sc[...].astype(BF16)

    def qrow(h, i, j):
        return jnp.minimum(i + j, nq - 1)

    qspec = pl.BlockSpec((t, HEAD_DIM), lambda h, i, j: (qrow(h, i, j), h))
    dospec = pl.BlockSpec((t, HEAD_DIM), lambda h, i, j: (qrow(h, i, j), do_col0 + h))
    kspec = pl.BlockSpec((t, HEAD_DIM), lambda h, i, j: (i, h))
    vspec = pl.BlockSpec((t, HEAD_DIM), lambda h, i, j: (i, v_col0 + h))
    return pl.pallas_call(
        body, name=name, grid=(H, nq, nj),
        in_specs=[qspec, kspec, vspec, qspec, qspec, dospec], out_specs=[kspec, kspec],
        out_shape=[jax.ShapeDtypeStruct((S, H * HEAD_DIM), F32), jax.ShapeDtypeStruct((S, H * HEAD_DIM), BF16)],
        scratch_shapes=[pltpu.VMEM((t, HEAD_DIM), F32), pltpu.VMEM((t, HEAD_DIM), F32)],
        compiler_params=_cparams(("parallel", "parallel", "arbitrary")),
    )(q, k, proj, o, lse, do)


def _conv_pre(x_ref, w_ref):
    x = x_ref[...]
    rows = lax.broadcasted_iota(jnp.int32, x.shape, 0)
    shifted = [x]
    acc = x * w_ref[pl.ds(CONV_WIDTH - 1, 1), :]
    for sft in range(1, CONV_WIDTH):
        xs = jnp.where(rows >= sft, pltpu.roll(x, sft, 0), 0.0)
        shifted.append(xs)
        acc = acc + xs * w_ref[pl.ds(CONV_WIDTH - 1 - sft, 1), :]
    return acc, shifted


def _conv_fwd(proj, col0, width, w, name):
    S = proj.shape[0]

    def body(x_ref, w_ref, y_ref):
        acc, _ = _conv_pre(x_ref, w_ref)
        y_ref[...] = _silu(acc)

    return pl.pallas_call(
        body, name=name, grid=(width // LANES,),
        in_specs=[pl.BlockSpec((S, LANES), lambda c: (0, col0 + c)), pl.BlockSpec((CONV_WIDTH, LANES), lambda c: (0, c))],
        out_specs=pl.BlockSpec((S, LANES), lambda c: (0, c)),
        out_shape=jax.ShapeDtypeStruct((S, width), F32),
        compiler_params=_cparams(("parallel",)),
    )(proj, w)


def _conv_bwd(proj, col0, width, w, dy, name):
    S = proj.shape[0]

    def body(x_ref, w_ref, d_ref, dx_ref, dw_ref):
        acc, shifted = _conv_pre(x_ref, w_ref)
        sig = _sigmoid(acc)
        da = d_ref[...] * (sig * (1.0 + acc * (1.0 - sig)))
        rows = lax.broadcasted_iota(jnp.int32, da.shape, 0)
        dx = da * w_ref[pl.ds(CONV_WIDTH - 1, 1), :]
        dw_ref[pl.ds(CONV_WIDTH - 1, 1), :] = jnp.sum(da * shifted[0], axis=0, keepdims=True)
        for sft in range(1, CONV_WIDTH):
            back = jnp.where(rows < S - sft, pltpu.roll(da, S - sft, 0), 0.0)
            dx = dx + back * w_ref[pl.ds(CONV_WIDTH - 1 - sft, 1), :]
            dw_ref[pl.ds(CONV_WIDTH - 1 - sft, 1), :] = jnp.sum(da * shifted[sft], axis=0, keepdims=True)
        dx_ref[...] = dx.astype(BF16)

    return pl.pallas_call(
        body, name=name, grid=(width // LANES,),
        in_specs=[pl.BlockSpec((S, LANES), lambda c: (0, col0 + c)), pl.BlockSpec((CONV_WIDTH, LANES), lambda c: (0, c)),
                  pl.BlockSpec((S, LANES), lambda c: (0, c))],
        out_specs=[pl.BlockSpec((S, LANES), lambda c: (0, c)), pl.BlockSpec((CONV_WIDTH, LANES), lambda c: (0, c))],
        out_shape=[jax.ShapeDtypeStruct((S, width), BF16), jax.ShapeDtypeStruct((CONV_WIDTH, width), F32)],
        compiler_params=_cparams(("parallel",)),
    )(proj, w, dy)


PREP_CHUNKS = 8


def _chunks_prep(qraws, kraws, vs, abs_, alog_row, dtb_row, mask_g, mask_b, t_saved=None):
    n = len(qraws)
    c = qraws[0].shape[0]
    mm_nt, mm_nn = (_mm_nt, _mm_nn) if t_saved is not None else (lambda p, r: _bdot(p, r, _NT), lambda p, r: _bdot(p, r, _NN))
    row = lax.broadcasted_iota(jnp.int32, (c, c), 0)
    col = lax.broadcasted_iota(jnp.int32, (c, c), 1)
    tril, strict, eye = row >= col, row > col, row == col
    eyef = eye.astype(F32)
    neg_rate = -jnp.exp(alog_row)
    q, k, beta, gc_col, gamma, kb, g_last = [], [], [], [], [], [], []
    for i in range(n):
        q.append(_l2(qraws[i]) * (HEAD_DIM ** -0.5))
        k.append(_l2(kraws[i]))
        gfull = neg_rate * _softplus(abs_[i] + dtb_row)
        g = jnp.sum(jnp.where(mask_g, gfull, 0.0), axis=-1, keepdims=True)
        beta.append(jnp.sum(jnp.where(mask_b, _sigmoid(abs_[i]), 0.0), axis=-1, keepdims=True))
        g_row = jnp.sum(jnp.where(eye, g, 0.0), axis=0, keepdims=True)
        gc_col.append(jnp.sum(jnp.where(tril, g_row, 0.0), axis=1, keepdims=True))
        gc_row = jnp.sum(jnp.where(row <= col, g, 0.0), axis=0, keepdims=True)
        gamma.append(jnp.where(tril, jnp.exp(jnp.where(tril, gc_col[i] - gc_row, 0.0)), 0.0))
        kb.append(k[i] * beta[i])
        g_last.append(jnp.sum(g, axis=0, keepdims=True))
    a = [jnp.where(strict, mm_nt(kb[i], k[i]) * gamma[i], 0.0) for i in range(n)]
    if t_saved is None:
        t_inv = [eyef - a[i] for i in range(n)]
        p = a
        for _ in range(int(math.log2(c)) - 1):
            p = [_dot3(p[i], p[i]) for i in range(n)]
            t_inv = [_dot3(t_inv[i], eyef + p[i]) for i in range(n)]
    else:
        t_inv = [_tri_inv_saved(a[i], t_saved[i]) for i in range(n)]
    egc = [jnp.exp(gc_col[i]) for i in range(n)]
    u = [mm_nn(t_inv[i], vs[i] * beta[i]) for i in range(n)]
    w = [mm_nn(t_inv[i], kb[i] * egc[i]) for i in range(n)]
    intra = [mm_nt(q[i], k[i]) * gamma[i] for i in range(n)]
    out = []
    for i in range(n):
        kt = k[i] * jnp.exp(g_last[i] - gc_col[i])
        dec = jnp.broadcast_to(jnp.exp(g_last[i]), (1, HEAD_DIM))
        one = (u[i], w[i], q[i] * egc[i], kt, intra[i], dec)
        out.append(one + (t_inv[i],) if t_saved is None else one)
    return out


def _lane_masks(h):
    lane = lax.broadcasted_iota(jnp.int32, (1, LANES), 1)
    return lane == h, lane == N_DELTA_HEADS + h


def _prep_specs(tr, ab_col):
    H = N_DELTA_HEADS
    return [
        pl.BlockSpec((tr, HEAD_DIM), lambda i, h: (i, h)),
        pl.BlockSpec((tr, HEAD_DIM), lambda i, h: (i, H + h)),
        pl.BlockSpec((tr, HEAD_DIM), lambda i, h: (i, 2 * H + h)),
        pl.BlockSpec((tr, LANES), lambda i, h: (i, ab_col)),
        pl.BlockSpec((1, LANES), lambda i, h: (0, 0)),
        pl.BlockSpec((1, LANES), lambda i, h: (0, 0)),
    ]


def _prep_out_specs(tr):
    nc = tr // CHUNK
    hs = pl.BlockSpec((tr, HEAD_DIM), lambda i, h: (i, h))
    return [hs, hs, hs, hs,
            pl.BlockSpec((None, tr, CHUNK), lambda i, h: (h, i, 0)),
            pl.BlockSpec((None, nc, 1, HEAD_DIM), lambda i, h: (h, i, 0, 0)),
            pl.BlockSpec((None, tr, CHUNK), lambda i, h: (h, i, 0))]


def _prep_out_shapes(S):
    H = N_DELTA_HEADS
    hs = jax.ShapeDtypeStruct((S, H * HEAD_DIM), F32)
    sq = jax.ShapeDtypeStruct((H, S, CHUNK), F32)
    return [hs, hs, hs, hs, sq, jax.ShapeDtypeStruct((H, S // CHUNK, 1, HEAD_DIM), F32), sq]


def _delta_prep_fwd(dqkv, proj, ab_col, alog_row, dtb_row, name):
    S = dqkv.shape[0]
    tr = min(S, PREP_CHUNKS * CHUNK)
    nc = tr // CHUNK

    def body(q_ref, k_ref, v_ref, ab_ref, al_ref, dt_ref, u_ref, w_ref, qd_ref, kt_ref, in_ref, dec_ref, ti_ref):
        mask_g, mask_b = _lane_masks(pl.program_id(1))
        rows = [pl.ds(ci * CHUNK, CHUNK) for ci in range(nc)]
        outs = _chunks_prep([q_ref[rs, :] for rs in rows], [k_ref[rs, :] for rs in rows], [v_ref[rs, :] for rs in rows],
                            [ab_ref[rs, :] for rs in rows], al_ref[...], dt_ref[...], mask_g, mask_b)
        for ci, rs in enumerate(rows):
            u, w, qd, kt, intra, dec, t_inv = outs[ci]
            u_ref[rs, :] = u
            w_ref[rs, :] = w
            qd_ref[rs, :] = qd
            kt_ref[rs, :] = kt
            in_ref[rs, :] = intra
            dec_ref[ci] = dec
            ti_ref[rs, :] = t_inv

    return pl.pallas_call(
        body, name=name, grid=(S // tr, N_DELTA_HEADS),
        in_specs=_prep_specs(tr, ab_col), out_specs=_prep_out_specs(tr), out_shape=_prep_out_shapes(S),
        compiler_params=_cparams(("parallel", "parallel")),
    )(dqkv, dqkv, dqkv, proj, alog_row, dtb_row)


def _delta_prep_bwd(dqkv, proj, ab_col, alog_row, dtb_row, cots, t_inv, name):
    S = dqkv.shape[0]
    H = N_DELTA_HEADS
    tr = min(S, PREP_CHUNKS * CHUNK)
    nc = tr // CHUNK

    def body(q_ref, k_ref, v_ref, ab_ref, al_ref, dt_ref, du_ref, dw_ref, dqd_ref, dkt_ref, din_ref, ddec_ref, ti_ref,
             dq_ref, dk_ref, dv_ref, dab_ref, dal_ref, ddt_ref, dab_sc):
        h = pl.program_id(1)
        mask_g, mask_b = _lane_masks(h)

        @pl.when((pl.program_id(0) == 0) & (h == 0))
        def _():
            dal_ref[...] = jnp.zeros_like(dal_ref)
            ddt_ref[...] = jnp.zeros_like(ddt_ref)

        @pl.when(h == 0)
        def _():
            dab_sc[...] = jnp.zeros_like(dab_sc)

        rows = [pl.ds(ci * CHUNK, CHUNK) for ci in range(nc)]
        fn = functools.partial(_chunks_prep, mask_g=mask_g, mask_b=mask_b, t_saved=[ti_ref[rs, :] for rs in rows])
        _, vjp = jax.vjp(fn, [q_ref[rs, :] for rs in rows], [k_ref[rs, :] for rs in rows], [v_ref[rs, :] for rs in rows],
                         [ab_ref[rs, :] for rs in rows], al_ref[...], dt_ref[...])
        dqs, dks, dvs, dabs, dal, ddt = vjp([(du_ref[rs, :], dw_ref[rs, :], dqd_ref[rs, :], dkt_ref[rs, :], din_ref[rs, :],
                                              ddec_ref[ci]) for ci, rs in enumerate(rows)])
        for ci, rs in enumerate(rows):
            dq_ref[rs, :] = dqs[ci]
            dk_ref[rs, :] = dks[ci]
            dv_ref[rs, :] = dvs[ci]
            dab_sc[rs, :] += dabs[ci]
        dal_ref[...] += dal
        ddt_ref[...] += ddt

        @pl.when(h == H - 1)
        def _():
            dab_ref[...] = dab_sc[...].astype(BF16)

    hs = pl.BlockSpec((tr, HEAD_DIM), lambda i, h: (i, h))
    hshape = jax.ShapeDtypeStruct((S, H * HEAD_DIM), F32)
    row = pl.BlockSpec((1, LANES), lambda i, h: (0, 0))
    rshape = jax.ShapeDtypeStruct((1, LANES), F32)
    return pl.pallas_call(
        body, name=name, grid=(S // tr, H),
        in_specs=_prep_specs(tr, ab_col) + _prep_out_specs(tr),
        out_specs=[hs, hs, hs, pl.BlockSpec((tr, LANES), lambda i, h: (i, 0)), row, row],
        out_shape=[hshape, hshape, hshape, jax.ShapeDtypeStruct((S, LANES), BF16), rshape, rshape],
        scratch_shapes=[pltpu.VMEM((tr, LANES), F32)],
        compiler_params=_cparams(("arbitrary", "arbitrary")),
    )(dqkv, dqkv, dqkv, proj, alog_row, dtb_row, *cots, t_inv)


def _scan_steps(states, us, ws, qds, kts, intras, decs):
    hs = range(len(states))
    v_new = [us[h] - _dot3(ws[h], states[h]) for h in hs]
    o_state = [_dot3(qds[h], states[h]) for h in hs]
    o_intra = [_dot3(intras[h], v_new[h]) for h in hs]
    grown = [_dot3(kts[h], v_new[h], _TN) for h in hs]
    return [o_state[h] + o_intra[h] for h in hs], [states[h] * decs[h] + grown[h] for h in hs]


def _scan_specs(rev, n):
    H = N_DELTA_HEADS

    def cix(i):
        return (n - 1 - i) if rev else i

    row = pl.BlockSpec((CHUNK, H * HEAD_DIM), lambda i: (cix(i), 0))
    return row, pl.BlockSpec((H, CHUNK, CHUNK), lambda i: (0, cix(i), 0)), \
        pl.BlockSpec((H, 1, 1, HEAD_DIM), lambda i: (0, cix(i), 0, 0)), \
        pl.BlockSpec((1, H, HEAD_DIM, HEAD_DIM), lambda i: (cix(i), 0, 0, 0))


def _delta_scan_fwd(u, w, qd, kt, intra, dec, name):
    S = u.shape[0]
    H = N_DELTA_HEADS
    n = S // CHUNK
    row, ispec, dspec, sspec = _scan_specs(False, n)

    def body(u_ref, w_ref, qd_ref, kt_ref, in_ref, dec_ref, o_ref, st_ref, s_sc):
        @pl.when(pl.program_id(0) == 0)
        def _():
            s_sc[...] = jnp.zeros_like(s_sc)

        cols = [pl.ds(h * HEAD_DIM, HEAD_DIM) for h in range(H)]
        states = [s_sc[h] for h in range(H)]
        outs, new = _scan_steps(states, [u_ref[:, cs] for cs in cols], [w_ref[:, cs] for cs in cols],
                                [qd_ref[:, cs] for cs in cols], [kt_ref[:, cs] for cs in cols],
                                [in_ref[h] for h in range(H)], [dec_ref[h, 0] for h in range(H)])
        for h, cs in enumerate(cols):
            st_ref[0, h] = states[h]
            o_ref[:, cs] = outs[h]
            s_sc[h] = new[h]

    return pl.pallas_call(
        body, name=name, grid=(n,),
        in_specs=[row, row, row, row, ispec, dspec], out_specs=[row, sspec],
        out_shape=[jax.ShapeDtypeStruct((S, H * HEAD_DIM), F32), jax.ShapeDtypeStruct((n, H, HEAD_DIM, HEAD_DIM), F32)],
        scratch_shapes=[pltpu.VMEM((H, HEAD_DIM, HEAD_DIM), F32)],
        compiler_params=_cparams(("arbitrary",)),
    )(u, w, qd, kt, intra, dec)


def _delta_scan_bwd(u, w, qd, kt, intra, dec, states, do, name):
    S = u.shape[0]
    H = N_DELTA_HEADS
    n = S // CHUNK
    row, ispec, dspec, sspec = _scan_specs(True, n)

    def body(u_ref, w_ref, qd_ref, kt_ref, in_ref, dec_ref, st_ref, do_ref,
             du_ref, dw_ref, dqd_ref, dkt_ref, din_ref, ddec_ref, ds_sc):
        @pl.when(pl.program_id(0) == 0)
        def _():
            ds_sc[...] = jnp.zeros_like(ds_sc)

        cols = [pl.ds(h * HEAD_DIM, HEAD_DIM) for h in range(H)]
        _, vjp = jax.vjp(_scan_steps, [st_ref[0, h] for h in range(H)], [u_ref[:, cs] for cs in cols],
                         [w_ref[:, cs] for cs in cols], [qd_ref[:, cs] for cs in cols], [kt_ref[:, cs] for cs in cols],
                         [in_ref[h] for h in range(H)], [dec_ref[h, 0] for h in range(H)])
        dstate, du, dw, dqd, dkt, din, ddec = vjp(([do_ref[:, cs] for cs in cols], [ds_sc[h] for h in range(H)]))
        for h, cs in enumerate(cols):
            du_ref[:, cs] = du[h]
            dw_ref[:, cs] = dw[h]
            dqd_ref[:, cs] = dqd[h]
            dkt_ref[:, cs] = dkt[h]
            din_ref[h] = din[h]
            ddec_ref[h, 0] = ddec[h]
            ds_sc[h] = dstate[h]

    hshape = jax.ShapeDtypeStruct((S, H * HEAD_DIM), F32)
    return pl.pallas_call(
        body, name=name, grid=(n,),
        in_specs=[row, row, row, row, ispec, dspec, sspec, row],
        out_specs=[row, row, row, row, ispec, dspec],
        out_shape=[hshape, hshape, hshape, hshape, jax.ShapeDtypeStruct((H, S, CHUNK), F32),
                   jax.ShapeDtypeStruct((H, n, 1, HEAD_DIM), F32)],
        scratch_shapes=[pltpu.VMEM((H, HEAD_DIM, HEAD_DIM), F32)],
        compiler_params=_cparams(("arbitrary",)),
    )(u, w, qd, kt, intra, dec, states, do)


def _gated_norm(od, z, gain):
    return _rms(od, gain) * _silu(z)


def _post_fwd(od, proj, z_col0, gain, name):
    S = od.shape[0]
    H = N_DELTA_HEADS
    tr = _pick(S, 512)

    def body(od_ref, z_ref, g_ref, o_ref):
        o_ref[...] = _gated_norm(od_ref[...], z_ref[...], g_ref[...]).astype(BF16)

    return pl.pallas_call(
        body, name=name, grid=(S // tr, H),
        in_specs=[_head_spec(tr, 0), _head_spec(tr, z_col0), _gain_spec()],
        out_specs=_head_spec(tr, 0), out_shape=jax.ShapeDtypeStruct((S, H * HEAD_DIM), BF16),
        compiler_params=_cparams(("parallel", "parallel")),
    )(od, proj, gain)


def _post_bwd(od, proj, z_col0, gain, do, do_col0, name):
    S = od.shape[0]
    H = N_DELTA_HEADS
    tr = _pick(S, 512)

    def body(od_ref, z_ref, g_ref, do_ref, dod_ref, dz_ref, dg_ref):
        @pl.when((pl.program_id(0) == 0) & (pl.program_id(1) == 0))
        def _():
            dg_ref[...] = jnp.zeros_like(dg_ref)

        _, vjp = jax.vjp(_gated_norm, od_ref[...], z_ref[...], g_ref[...])
        dod, dz, dg = vjp(do_ref[...])
        dod_ref[...] = dod
        dz_ref[...] = dz.astype(BF16)
        dg_ref[...] += dg

    return pl.pallas_call(
        body, name=name, grid=(S // tr, H),
        in_specs=[_head_spec(tr, 0), _head_spec(tr, z_col0), _gain_spec(), _head_spec(tr, do_col0)],
        out_specs=[_head_spec(tr, 0), _head_spec(tr, 0), _gain_spec()],
        out_shape=[jax.ShapeDtypeStruct((S, H * HEAD_DIM), F32), jax.ShapeDtypeStruct((S, H * HEAD_DIM), BF16),
                   jax.ShapeDtypeStruct((1, HEAD_DIM), F32)],
        compiler_params=_cparams(("arbitrary", "arbitrary")),
    )(od, proj, gain, do)


def _adam_math(w, g, m, v):
    m = ADAM_B1 * m + (1.0 - ADAM_B1) * g
    v = ADAM_B2 * v + (1.0 - ADAM_B2) * (g * g)
    m_hat = m / (1.0 - ADAM_B1 ** ADAM_STEP)
    v_hat = v / (1.0 - ADAM_B2 ** ADAM_STEP)
    delta = -ADAM_LR * (m_hat / (jnp.sqrt(v_hat) + ADAM_EPS) + ADAM_WD * w)
    return delta, m, v


def _adamw(w, g, m, v, name):
    R, C = w.shape
    tr = R if R * C * 4 <= (1 << 20) else _pick8(R, max(8, (1 << 20) // (C * 4)))

    def body(w_ref, g_ref, m_ref, v_ref, d_ref, nm_ref, nv_ref):
        d, nm, nv = _adam_math(w_ref[...], g_ref[...], m_ref[...], v_ref[...])
        d_ref[...] = d
        nm_ref[...] = nm
        nv_ref[...] = nv

    spec = pl.BlockSpec((tr, C), lambda i: (i, 0))
    shp = jax.ShapeDtypeStruct((R, C), F32)
    return pl.pallas_call(
        body, name=name, grid=(R // tr,), in_specs=[spec] * 4, out_specs=[spec] * 3, out_shape=[shp] * 3,
        compiler_params=_cparams(("parallel",)),
    )(w, g, m, v)


def _pick8(dim, pref):
    t = (min(dim, pref) // 8) * 8
    while t >= 8:
        if dim % t == 0:
            return t
        t -= 8
    return dim


def _adamw_outer(w, m, v, cond_t, rhs, name):
    R, C = w.shape
    tr = _pick8(R, 128)
    nb = cond_t.shape[1]
    lhs_t = cond_t

    def body(w_ref, m_ref, v_ref, a_ref, b_ref, g_ref, d_ref, nm_ref, nv_ref):
        g = _dot(_silu(a_ref[...]), b_ref[...])
        d, nm, nv = _adam_math(w_ref[...], g, m_ref[...], v_ref[...])
        g_ref[...] = g
        d_ref[...] = d
        nm_ref[...] = nm
        nv_ref[...] = nv

    spec = pl.BlockSpec((tr, C), lambda i: (i, 0))
    shp = jax.ShapeDtypeStruct((R, C), F32)
    return pl.pallas_call(
        body, name=name, grid=(R // tr,),
        in_specs=[spec, spec, spec, pl.BlockSpec((tr, nb), lambda i: (i, 0)), pl.BlockSpec((nb, C), lambda i: (0, 0))],
        out_specs=[spec] * 4, out_shape=[shp] * 4,
        compiler_params=_cparams(("parallel",)),
    )(w, m, v, lhs_t, rhs)


def _ada_fwd(cond, w, bias, name):
    a = cond
    nb, K = a.shape
    N = w.shape[1]
    tn = _pick(N, 512)

    def body(a_ref, w_ref, b_ref, o_ref):
        o_ref[...] = _dot(_silu(a_ref[...]), w_ref[...]) + b_ref[...]

    return pl.pallas_call(
        body, name=name, grid=(N // tn,),
        in_specs=[pl.BlockSpec((nb, K), lambda j: (0, 0)), pl.BlockSpec((K, tn), lambda j: (0, j)), pl.BlockSpec((1, tn), lambda j: (0, j))],
        out_specs=pl.BlockSpec((nb, tn), lambda j: (0, j)), out_shape=jax.ShapeDtypeStruct((nb, N), F32),
        compiler_params=_cparams(("parallel",)),
    )(a, w, bias)


def _add_cast(parts, out_dtypes, name):
    shape = parts[0].shape
    G, R, C = shape
    tr = _pick8(R, max(8, (1 << 20) // (C * 4)))
    n_in = len(parts)

    def body(*refs):
        acc = refs[0][...].astype(F32)
        for r in refs[1:n_in]:
            acc = acc + r[...].astype(F32)
        for o, dt in zip(refs[n_in:], out_dtypes):
            o[...] = acc.astype(dt)

    spec = pl.BlockSpec((1, tr, C), lambda g, i: (g, i, 0))
    outs = pl.pallas_call(
        body, name=name, grid=(G, R // tr), in_specs=[spec] * n_in, out_specs=[spec] * len(out_dtypes),
        out_shape=[jax.ShapeDtypeStruct(shape, dt) for dt in out_dtypes],
        compiler_params=_cparams(("parallel", "parallel")),
    )(*parts)
    return outs


def _me():
    return lax.axis_index("x"), lax.axis_index("y"), lax.axis_index("c")


def _xor_peer(k):
    x, y, c = _me()
    dx, dy, dc = (k >> 2) & 1, (k >> 1) & 1, k & 1
    return (x ^ dx if dx else x, y ^ dy if dy else y, c ^ dc if dc else c)


ANY = pl.BlockSpec(memory_space=pl.ANY)


def _all_gather_small(v, name):
    R, C = v.shape

    def body(v_ref, out_ref, send_sems, recv_sems):
        x, y, c = _me()
        mine = 4 * x + 2 * y + c
        out_ref[mine] = v_ref[...]
        copies = []
        for k in range(1, 8):
            cp = pltpu.make_async_remote_copy(src_ref=v_ref, dst_ref=out_ref.at[mine], send_sem=send_sems.at[k - 1],
                                              recv_sem=recv_sems.at[k - 1], device_id=_xor_peer(k), device_id_type=MESH)
            cp.start()
            copies.append(cp)
        for k in range(1, 8):
            px, py, pc = _xor_peer(k)
            pltpu.make_async_remote_copy(src_ref=v_ref, dst_ref=out_ref.at[4 * px + 2 * py + pc], send_sem=send_sems.at[k - 1],
                                         recv_sem=recv_sems.at[k - 1], device_id=_xor_peer(k), device_id_type=MESH).wait_recv()
        for cp in copies:
            cp.wait_send()

    return pl.pallas_call(
        body, name=name, out_shape=jax.ShapeDtypeStruct((8, R, C), F32),
        in_specs=[pl.BlockSpec(memory_space=pltpu.VMEM)], out_specs=pl.BlockSpec(memory_space=pltpu.VMEM),
        scratch_shapes=[pltpu.SemaphoreType.DMA((7,)), pltpu.SemaphoreType.DMA((7,))],
        compiler_params=pltpu.CompilerParams(vmem_limit_bytes=VMEM_LIMIT),
    )(v)


def _chip_peers():
    x, y, _ = _me()
    return [(1, (x, 1 - y)), (2, (1 - x, y)), (3, (1 - x, 1 - y))]


def _all_gather_shards(shards, name):
    n = len(shards)

    def body(*refs):
        ins, outs = refs[:n], refs[n:2 * n]
        send_sems, recv_sems = refs[2 * n:]
        x, y, c = _me()
        chip = 2 * x + y
        sib = (x, y, 1 - c)
        peers = _chip_peers()
        sends = []
        for t in range(n):
            half = ins[t].shape[0] // 2
            mine = pl.ds(c * half, half)
            for p, (k, (px, py)) in enumerate(peers):
                cp = pltpu.make_async_remote_copy(src_ref=ins[t].at[mine], dst_ref=outs[t].at[chip, mine],
                                                  send_sem=send_sems.at[6 * t + p], recv_sem=recv_sems.at[6 * t + p],
                                                  device_id=(px, py, c), device_id_type=MESH)
                cp.start()
                sends.append(cp)
        for t in range(n):
            half = ins[t].shape[0] // 2
            mine = pl.ds(c * half, half)
            for p, (k, (px, py)) in enumerate(peers):
                src_chip = 2 * px + py
                landed = outs[t].at[src_chip, mine]
                pltpu.make_async_remote_copy(src_ref=landed, dst_ref=landed, send_sem=send_sems.at[6 * t + p],
                                             recv_sem=recv_sems.at[6 * t + p], device_id=(px, py, c), device_id_type=MESH).wait_recv()
                fw = pltpu.make_async_remote_copy(src_ref=landed, dst_ref=landed, send_sem=send_sems.at[6 * t + 3 + p],
                                                  recv_sem=recv_sems.at[6 * t + 3 + p], device_id=sib, device_id_type=MESH)
                fw.start()
                sends.append(fw)
        for t in range(n):
            half = ins[t].shape[0] // 2
            theirs = pl.ds((1 - c) * half, half)
            for p, (k, (px, py)) in enumerate(peers):
                got = outs[t].at[2 * px + py, theirs]
                pltpu.make_async_remote_copy(src_ref=got, dst_ref=got, send_sem=send_sems.at[6 * t + 3 + p],
                                             recv_sem=recv_sems.at[6 * t + 3 + p], device_id=sib, device_id_type=MESH).wait_recv()
        for cp in sends:
            cp.wait_send()

    return pl.pallas_call(
        body, name=name,
        out_shape=[jax.ShapeDtypeStruct((4,) + s.shape, s.dtype) for s in shards],
        in_specs=[ANY] * n, out_specs=[ANY] * n,
        scratch_shapes=[pltpu.SemaphoreType.DMA((6 * n,)), pltpu.SemaphoreType.DMA((6 * n,))],
    )(*shards)


def _swap_halves_with_sibling(slabs, name):
    n = len(slabs)

    def body(*refs):
        ins, outs = refs[:n], refs[n:2 * n]
        send_sems, recv_sems = refs[2 * n:]
        x, y, c = _me()
        sib = (x, y, 1 - c)
        cps = []
        for t in range(n):
            half = ins[t].shape[1] // 2
            cp = pltpu.make_async_remote_copy(src_ref=ins[t].at[:, pl.ds((1 - c) * half, half)], dst_ref=outs[t],
                                              send_sem=send_sems.at[t], recv_sem=recv_sems.at[t], device_id=sib, device_id_type=MESH)
            cp.start()
            cps.append(cp)
        for cp in cps:
            cp.wait()

    return pl.pallas_call(
        body, name=name,
        out_shape=[jax.ShapeDtypeStruct((4, s.shape[1] // 2, s.shape[2]), s.dtype) for s in slabs],
        in_specs=[ANY] * n, out_specs=[ANY] * n,
        scratch_shapes=[pltpu.SemaphoreType.DMA((n,)), pltpu.SemaphoreType.DMA((n,))],
    )(*slabs)


def _exchange_between_chips(slabs, name):
    n = len(slabs)

    def body(*refs):
        ins, outs = refs[:n], refs[n:2 * n]
        send_sems, recv_sems = refs[2 * n:]
        x, y, c = _me()
        chip = 2 * x + y
        cps = []
        for t in range(n):
            for p, (k, (px, py)) in enumerate(_chip_peers()):
                cp = pltpu.make_async_remote_copy(src_ref=ins[t].at[2 * px + py], dst_ref=outs[t].at[chip],
                                                  send_sem=send_sems.at[3 * t + p], recv_sem=recv_sems.at[3 * t + p],
                                                  device_id=(px, py, c), device_id_type=MESH)
                cp.start()
                cps.append(cp)
        for t in range(n):
            for p, (k, (px, py)) in enumerate(_chip_peers()):
                got = outs[t].at[2 * px + py]
                pltpu.make_async_remote_copy(src_ref=got, dst_ref=got, send_sem=send_sems.at[3 * t + p],
                                             recv_sem=recv_sems.at[3 * t + p], device_id=(px, py, c), device_id_type=MESH).wait_recv()
        for cp in cps:
            cp.wait_send()

    return pl.pallas_call(
        body, name=name,
        out_shape=[jax.ShapeDtypeStruct(s.shape, s.dtype) for s in slabs],
        in_specs=[ANY] * n, out_specs=[ANY] * n,
        scratch_shapes=[pltpu.SemaphoreType.DMA((3 * n,)), pltpu.SemaphoreType.DMA((3 * n,))],
    )(*slabs)


def _send_halves_to_sibling(halves, name):
    n = len(halves)

    def body(*refs):
        ins, outs = refs[:n], refs[n:2 * n]
        send_sems, recv_sems = refs[2 * n:]
        x, y, c = _me()
        sib = (x, y, 1 - c)
        cps = []
        for t in range(n):
            cp = pltpu.make_async_remote_copy(src_ref=ins[t], dst_ref=outs[t], send_sem=send_sems.at[t],
                                              recv_sem=recv_sems.at[t], device_id=sib, device_id_type=MESH)
            cp.start()
            cps.append(cp)
        for cp in cps:
            cp.wait()

    return pl.pallas_call(
        body, name=name,
        out_shape=[jax.ShapeDtypeStruct(s.shape, s.dtype) for s in halves],
        in_specs=[ANY] * n, out_specs=[ANY] * n,
        scratch_shapes=[pltpu.SemaphoreType.DMA((n,)), pltpu.SemaphoreType.DMA((n,))],
    )(*halves)


HBM_SPEC = pl.BlockSpec(memory_space=pltpu.HBM)
SEM_SPEC = pl.BlockSpec(memory_space=pltpu.SEMAPHORE)
DATAFLOW = pltpu.SideEffectType.DATAFLOW_SIDE_EFFECTING


def _plan_gather_direct(src_refs, land_refs):
    x, y, c = _me()
    chip = 2 * x + y
    plan = []
    for s, land in zip(src_refs, land_refs):
        half = s.shape[0] // 2
        for _, (px, py) in _chip_peers():
            for pc in (c, 1 - c):
                plan.append((s.at[pl.ds(c * half, half)], land.at[chip, pl.ds(c * half, half)],
                             land.at[2 * px + py, pl.ds(pc * half, half)], (px, py, pc)))
    return plan


def _plan_scatter_direct(src_refs, land_refs):
    x, y, c = _me()
    plan = []
    for s, land in zip(src_refs, land_refs):
        half = s.shape[1] // 2
        for k in range(1, 8):
            px, py, pc = _xor_peer(k)
            plan.append((s.at[2 * px + py, pl.ds(pc * half, half)], land.at[4 * x + 2 * y + c],
                         land.at[4 * px + 2 * py + pc], (px, py, pc)))
    return plan


def _start_copies(srcs, lands, plan_fn, name, after=None):
    n = len(srcs)
    n_copies = len(srcs) * (7 if plan_fn is _plan_scatter_direct else 6)
    extra = [] if after is None else [after]

    def body(*refs):
        refs = refs[:2 * n] + refs[2 * n + len(extra):]
        send_sems, recv_sems, token = refs[2 * n], refs[2 * n + 1], refs[-1]
        for i, (src, dst, _, peer) in enumerate(plan_fn(refs[:n], refs[n:2 * n])):
            pltpu.make_async_remote_copy(src_ref=src, dst_ref=dst, send_sem=send_sems.at[i], recv_sem=recv_sems.at[i],
                                         device_id=peer, device_id_type=MESH).start()
        token[...] = jnp.zeros_like(token)

    arrays = list(srcs) + list(lands)
    outs = pl.pallas_call(
        body, name=name,
        out_shape=(pltpu.SemaphoreType.DMA((n_copies,)), pltpu.SemaphoreType.DMA((n_copies,)),
                   *[pltpu.HBM(a.shape, a.dtype) for a in arrays], jax.ShapeDtypeStruct((8, LANES), F32)),
        in_specs=[HBM_SPEC] * (2 * n) + [ANY] * len(extra),
        out_specs=(SEM_SPEC, SEM_SPEC, *[HBM_SPEC] * (2 * n), pl.BlockSpec(memory_space=pltpu.VMEM)),
        input_output_aliases={i: 2 + i for i in range(2 * n)},
        compiler_params=pltpu.CompilerParams(has_side_effects=DATAFLOW),
    )(*[pltpu.with_memory_space_constraint(a, pltpu.HBM) for a in arrays], *extra)
    return outs[0], outs[1], list(outs[2:2 + n]), list(outs[2 + n:2 + 2 * n]), outs[-1]


def _wait_copies(send_sems, recv_sems, srcs, lands, after, plan_fn, name):
    n = len(srcs)

    def body(*refs):
        send_sems, recv_sems = refs[2 * n], refs[2 * n + 1]
        for i, (src, _, arrival, peer) in enumerate(plan_fn(refs[:n], refs[n:2 * n])):
            cp = pltpu.make_async_remote_copy(src_ref=src, dst_ref=arrival, send_sem=send_sems.at[i], recv_sem=recv_sems.at[i],
                                              device_id=peer, device_id_type=MESH)
            cp.wait_send()
            cp.wait_recv()

    arrays = list(srcs) + list(lands)
    outs = pl.pallas_call(
        body, name=name,
        out_shape=tuple(pltpu.HBM(a.shape, a.dtype) for a in arrays),
        in_specs=[HBM_SPEC] * (2 * n) + [SEM_SPEC, SEM_SPEC, ANY],
        out_specs=tuple([HBM_SPEC] * (2 * n)),
        input_output_aliases={i: i for i in range(2 * n)},
        compiler_params=pltpu.CompilerParams(has_side_effects=DATAFLOW),
    )(*arrays, send_sems, recv_sems, after)
    return list(outs[n:])


def _rope_tables(positions):
    half = ROPE_DIM // 2
    S = positions.shape[0]
    inv_freq = ROPE_THETA ** (-jnp.arange(half, dtype=F32) / half)
    ang = positions.astype(F32)[:, None] * inv_freq
    cos, sin = jnp.cos(ang), jnp.sin(ang)
    zeros = functools.partial(jnp.zeros, dtype=F32)
    cc = jnp.concatenate([cos, cos, jnp.ones((S, HEAD_DIM - ROPE_DIM), F32)], axis=1)
    s1 = jnp.concatenate([-sin, zeros((S, HEAD_DIM - half))], axis=1)
    s2 = jnp.concatenate([zeros((S, half)), sin, zeros((S, HEAD_DIM - ROPE_DIM))], axis=1)
    return cc, s1, s2


def _ffn_fwd(x, gain, shift, scale, gate, w_gu, w_d, fs, tag):
    h = _pre_fwd(x, gain, shift, scale, tag + "_pre")
    ab = _matmul(h, w_gu, "nn", F32, tag + "_gate_up", tm=512, tn=1024, tk=4096)
    s = _swiglu_fwd(ab, fs, tag + "_swiglu")
    f = _matmul(s, w_d, "nn", F32, tag + "_down", tm=512, tn=1024, tk=8192)
    xn = _residual_fwd(x, gate, f, 0.5, tag + "_res")
    return xn, (x, h, ab, s, f)


def _ffn_bwd(dxn, saved, gain, shift, scale, gate, w_gu, w_d, fs, tag):
    x, h, ab, s, f = saved
    df, dgate = _residual_bwd(gate, f, dxn, 0.5, tag + "_res_bwd")
    ds = _matmul(df, w_d, "nt", F32, tag + "_down_dx", tm=1024, tn=1408, tk=4096)
    dw_d = _matmul(s, df, "tn", F32, tag + "_down_dw", tm=1408, tn=1024, tk=1024)
    dab = _swiglu_bwd(ab, ds, fs, tag + "_swiglu_bwd")
    dh = _matmul(dab, w_gu, "nt", F32, tag + "_gate_up_dx", tm=1024, tn=1024, tk=2816)
    dw_gu = _matmul(h, dab, "tn", F32, tag + "_gate_up_dw", tm=1024, tn=1024, tk=1024)
    dx, dgain, dshift, dscale = _pre_bwd(x, gain, shift, scale, dh, dxn, tag + "_pre_bwd")
    return dx, dw_gu, dw_d, dgain, dshift, dscale, dgate


def _flat_pad(parts, rows, cols):
    flat = jnp.concatenate([p.reshape(-1).astype(F32) for p in parts])
    return jnp.pad(flat, (0, rows * cols - flat.shape[0])).reshape(rows, cols)


def _cols_to_slabs(w, n):
    R, NC = w.shape
    return jnp.transpose(w.reshape(R, n, NC // n), (1, 0, 2))


def kernel(x, c, positions, w_ada, b_ada, ffn1_norm, ffn1_w_gate, ffn1_w_up, ffn1_w_down, mix_norm, w_in, conv_w, q_norm, k_norm, a_log, dt_bias, delta_out_norm, w_out, ffn2_norm, ffn2_w_gate, ffn2_w_up, ffn2_w_down, loss_target, m_w_ada, m_b_ada, m_ffn1_norm, m_ffn1_w_gate, m_ffn1_w_up, m_ffn1_w_down, m_mix_norm, m_w_in, m_conv_w, m_q_norm, m_k_norm, m_a_log, m_dt_bias, m_delta_out_norm, m_w_out, m_ffn2_norm, m_ffn2_w_gate, m_ffn2_w_up, m_ffn2_w_down, v_w_ada, v_b_ada, v_ffn1_norm, v_ffn1_w_gate, v_ffn1_w_up, v_ffn1_w_down, v_mix_norm, v_w_in, v_conv_w, v_q_norm, v_k_norm, v_a_log, v_dt_bias, v_delta_out_norm, v_w_out, v_ffn2_norm, v_ffn2_w_gate, v_ffn2_w_up, v_ffn2_w_down):
    xi, yi, ci = _me()
    chip = 2 * xi + yi
    dev = 2 * chip + ci
    xs = x[0]
    S, D = xs.shape
    HA, HD = N_ATTN_HEADS, N_DELTA_HEADS
    fs = ffn1_w_gate.shape[2]
    n_mod_shard = w_ada.shape[2]
    in_shard = w_in.shape[2]
    in_width = 4 * in_shard
    in_pad = -(-in_width // LANES) * LANES
    conv_shard = conv_w.shape[2]
    conv_width = 4 * conv_shard

    pack0 = jnp.zeros((8, max(D, conv_shard)), F32)
    pack0 = pack0.at[0, :D].set(c[0]).at[1:1 + CONV_WIDTH, :conv_shard].set(conv_w[0])
    got0 = _all_gather_small(pack0, "gather_cond")
    c_all = got0[:, 0, :D]
    conv_full = jnp.transpose(got0[::2, 1:1 + CONV_WIDTH, :conv_shard], (1, 0, 2)).reshape(CONV_WIDTH, conv_width)
    b_ada_mine = lax.dynamic_slice(b_ada, (0, chip * n_mod_shard), (1, n_mod_shard))
    mod_part = _ada_fwd(c_all, w_ada[0], b_ada_mine, "ada_fwd")
    got1 = _all_gather_small(mod_part, "gather_mod")
    mod = lax.dynamic_index_in_dim(got1[::2], dev, axis=1, keepdims=False).reshape(1, 4 * n_mod_shard)
    sh1, sc1, gt1, sh2, sc2, gt2, sh3, sc3, gt3 = [mod[:, i * D:(i + 1) * D] for i in range(N_MOD)]

    shards = [w[0].astype(BF16) for w in (ffn1_w_gate, ffn1_w_up, ffn1_w_down, w_in, w_out, ffn2_w_gate, ffn2_w_up, ffn2_w_down)]
    gathered = _all_gather_shards(shards[:3], "gather_weights")
    g1g, g1u, g1d = [lax.dynamic_update_index_in_dim(g, s, chip, 0) for g, s in zip(gathered, shards[:3])]
    zones = [lax.dynamic_update_index_in_dim(lax.empty((4,) + s.shape, BF16), s, chip, 0) for s in shards[3:]]
    ag_in = _start_copies(shards[3:4], zones[:1], _plan_gather_direct, "gather_in_start")
    sh1 = sh1 + ag_in[4][0, 0]

    def gate_up(gg, gu):
        return jnp.transpose(jnp.concatenate([gg, gu], axis=2), (1, 0, 2)).reshape(D, 8 * fs)

    w_gu1, w_d1 = gate_up(g1g, g1u), g1d.reshape(4 * fs, D)

    x1, saved1 = _ffn_fwd(xs, ffn1_norm, sh1, sc1, gt1, w_gu1, w_d1, fs, "ffn1")
    (gin,) = _wait_copies(ag_in[0], ag_in[1], ag_in[2], ag_in[3], x1, _plan_gather_direct, "gather_in_wait")
    w_in_f = jnp.pad(jnp.transpose(gin, (1, 0, 2)).reshape(D, in_width), ((0, 0), (0, in_pad - in_width)))
    ag_rest = _start_copies(shards[4:], zones[1:], _plan_gather_direct, "gather_rest_start", after=gin)
    sh2 = sh2 + ag_rest[4][0, 0]

    cc, s1, s2 = _rope_tables(positions[0])
    alog_row = jnp.pad(a_log, ((0, 0), (0, LANES - HD)))
    dtb_row = jnp.pad(dt_bias, ((0, 0), (0, LANES - HD)))
    col_k, col_v, col_d, col_z, col_ab = HA, 2 * HA, 3 * HA, 3 * HA + 3 * HD, 3 * HA + 4 * HD
    h2 = _pre_fwd(x1, mix_norm, sh2, sc2, "mix_pre")
    proj = _matmul(h2, w_in_f, "nn", F32, "mix_in_proj", tm=512, tn=1024, tk=4096)
    qa, ka = _attn_prep_fwd(proj, q_norm, k_norm, cc, s1, s2, "attn_prep")
    oa, oa_b, lse = _attn_fwd(qa, ka, proj, col_v, "attn_fwd")
    dqkv = _conv_fwd(proj, col_d, conv_width, conv_full, "conv_fwd")
    *prep, t_inv = _delta_prep_fwd(dqkv, proj, col_ab, alog_row, dtb_row, "delta_prep")
    od_raw, states = _delta_scan_fwd(*prep, "delta_scan")
    od = _post_fwd(od_raw, proj, col_z, delta_out_norm, "delta_post")
    o = jnp.concatenate([oa_b, od], axis=1)
    gout, g2g, g2u, g2d = _wait_copies(ag_rest[0], ag_rest[1], ag_rest[2], ag_rest[3], o, _plan_gather_direct, "gather_rest_wait")
    w_out_f = gout.reshape(-1, D)
    w_gu2, w_d2 = gate_up(g2g, g2u), g2d.reshape(4 * fs, D)
    mo = _matmul(o, w_out_f, "nn", F32, "mix_out_proj", tm=512, tn=1024, tk=4096)
    x2 = _residual_fwd(x1, gt2, mo, 1.0, "mix_res")

    x3, saved3 = _ffn_fwd(x2, ffn2_norm, sh3, sc3, gt3, w_gu2, w_d2, fs, "ffn2")
    loss_part, dy = _loss_head(x3, loss_target[0], "loss_head")
    loss = lax.psum(loss_part[0, 0], ("x", "y", "c"))

    dx2, dw_gu2, dw_d2, dgain3, dsh3, dsc3, dgt3 = _ffn_bwd(dy, saved3, ffn2_norm, sh3, sc3, gt3, w_gu2, w_d2, fs, "ffn2")

    def scatter_start(slabs32, name):
        slabs16 = [s.astype(BF16) for s in slabs32]
        zones = []
        for s in slabs16:
            half = s.shape[1] // 2
            own = lax.dynamic_slice(s, (chip, ci * half, 0), (1, half, s.shape[2]))
            zones.append(lax.dynamic_update_slice(lax.empty((8, half, s.shape[2]), BF16), own, (dev, 0, 0)))
        return _start_copies(slabs16, zones, _plan_scatter_direct, name)

    rs_ffn2 = scatter_start([_cols_to_slabs(dw_gu2, 4), dw_d2.reshape(4, fs, D)], "rs_ffn2_start")
    dmo, dgt2 = _residual_bwd(gt2 + rs_ffn2[4][0, 0], mo, dx2, 1.0, "mix_res_bwd")
    do = _matmul(dmo, w_out_f, "nt", F32, "mix_out_dx", tm=1024, tn=1024, tk=4096)
    dw_out = _matmul(o, dmo, "tn", F32, "mix_out_dw", tm=1024, tn=1024, tk=1024)
    dq = _attn_bwd_dq(qa, ka, proj, col_v, oa, lse, do, 0, "attn_bwd_dq")
    dk, dv = _attn_bwd_dkv(qa, ka, proj, col_v, oa, lse, do, 0, "attn_bwd_dkv")
    dpq, dpk, dq_gain, dk_gain = _attn_prep_bwd(proj, q_norm, k_norm, cc, s1, s2, dq, dk, "attn_prep_bwd")
    dod, dz, ddn_gain = _post_bwd(od_raw, proj, col_z, delta_out_norm, do, HA, "delta_post_bwd")
    cots = _delta_scan_bwd(*prep, states, dod, "delta_scan_bwd")
    ddq, ddk, ddv, dab, dalog, ddtb = _delta_prep_bwd(dqkv, proj, col_ab, alog_row, dtb_row, cots, t_inv, "delta_prep_bwd")
    dconv_in, dconv_w = _conv_bwd(proj, col_d, conv_width, conv_full, jnp.concatenate([ddq, ddk, ddv], axis=1), "conv_bwd")
    dproj = jnp.concatenate([dpq, dpk, dv, dconv_in, dz, dab], axis=1)
    dh2 = _matmul(dproj, w_in_f, "nt", F32, "mix_in_dx", tm=1024, tn=1024, tk=2816)
    dw_in = _matmul(h2, dproj, "tn", F32, "mix_in_dw", tm=1024, tn=1024, tk=1024)
    rs_mix = scatter_start([_cols_to_slabs(dw_in[:, :in_width], 4), dw_out.reshape(4, -1, D)], "rs_mix_start")
    dx1, dgain2, dsh2, dsc2 = _pre_bwd(x1, mix_norm, sh2, sc2, dh2, dx2, "mix_pre_bwd")

    dx0, dw_gu1, dw_d1, dgain1, dsh1, dsc1, dgt1 = _ffn_bwd(dx1, saved1, ffn1_norm, sh1, sc1, gt1 + rs_mix[4][0, 0], w_gu1, w_d1, fs, "ffn1")

    n_small = N_MOD * D + 3 * D + 5 * LANES + CONV_WIDTH * conv_width
    cols_small = -(-n_small // (8 * LANES)) * LANES
    small = _flat_pad([dsh1, dsc1, dgt1, dsh2, dsc2, dgt2, dsh3, dsc3, dgt3, dgain1, dgain2, dgain3,
                       dq_gain, dk_gain, dalog, ddtb, ddn_gain, dconv_w], 8, cols_small)
    got2 = _all_gather_small(small, "gather_small_grads")
    small_sum = _add_cast([got2[d:d + 1] for d in range(8)], [F32], "sum_small_grads")[0].reshape(-1)
    dmod_all = got2.reshape(8, -1)[:, :N_MOD * D]
    off = [0]

    def take(n):
        off[0] += n
        return small_sum[off[0] - n:off[0]]

    g_b_ada = take(N_MOD * D).reshape(1, -1)
    g_ffn1_norm, g_mix_norm, g_ffn2_norm = take(D).reshape(1, D), take(D).reshape(1, D), take(D).reshape(1, D)
    g_q_norm, g_k_norm = take(LANES).reshape(1, -1), take(LANES).reshape(1, -1)
    g_a_log, g_dt_bias = take(LANES)[:HD].reshape(1, HD), take(LANES)[:HD].reshape(1, HD)
    g_dn = take(LANES).reshape(1, -1)
    g_conv_full = take(CONV_WIDTH * conv_width).reshape(CONV_WIDTH, conv_width)
    g_conv = lax.dynamic_slice(g_conv_full, (0, chip * conv_shard), (CONV_WIDTH, conv_shard))

    slabs = [_cols_to_slabs(dw_gu1, 4), dw_d1.reshape(4, fs, D)]
    from_sibling = _swap_halves_with_sibling(slabs, "rs_sibling_swap")
    chip_sums32, chip_sums16 = [], []
    for t, (slab, other) in enumerate(zip(slabs, from_sibling)):
        half = slab.shape[1] // 2
        mine = lax.dynamic_slice_in_dim(slab, ci * half, half, axis=1)
        p32, p16 = _add_cast([mine, other], [F32, BF16], "rs_chip_sum_%d" % t)
        chip_sums32.append(p32)
        chip_sums16.append(p16)
    from_chips = _exchange_between_chips(chip_sums16, "rs_chip_exchange")
    halves = []
    for t, (p32, got) in enumerate(zip(chip_sums32, from_chips)):
        parts = [lax.dynamic_index_in_dim(p32, chip, axis=0, keepdims=True)]
        parts += [lax.dynamic_index_in_dim(got, (chip + k) % 4, axis=0, keepdims=True) for k in (1, 2, 3)]
        halves.append(_add_cast(parts, [F32], "rs_total_%d" % t)[0][0])
    arrived = _wait_copies(rs_mix[0], rs_mix[1], rs_mix[2], rs_mix[3], dx0, _plan_scatter_direct, "rs_mix_wait")
    arrived += _wait_copies(rs_ffn2[0], rs_ffn2[1], rs_ffn2[2], rs_ffn2[3], dx0, _plan_scatter_direct, "rs_ffn2_wait")
    for t, zone in enumerate(arrived):
        halves.append(_add_cast([zone[d:d + 1] for d in range(8)], [F32], "rs_total_%d" % (t + 2))[0][0])
    theirs = _send_halves_to_sibling(halves, "rs_sibling_join")
    g_gu1, g_d1, g_in, g_out, g_gu2, g_d2 = [
        jnp.where(ci == 0, jnp.concatenate([mine, other], axis=0), jnp.concatenate([other, mine], axis=0))
        for mine, other in zip(halves, theirs)]

    res = {}

    def upd(name, w, g, m, v):
        d, nm, nv = _adamw(w[0], g, m[0], v[0], "adamw_" + name)
        res[name] = (g[None], d[None], nm[None], nv[None])

    upd("ffn1_w_gate", ffn1_w_gate, g_gu1[:, :fs], m_ffn1_w_gate, v_ffn1_w_gate)
    upd("ffn1_w_up", ffn1_w_up, g_gu1[:, fs:], m_ffn1_w_up, v_ffn1_w_up)
    upd("ffn1_w_down", ffn1_w_down, g_d1, m_ffn1_w_down, v_ffn1_w_down)
    upd("w_in", w_in, g_in, m_w_in, v_w_in)
    upd("w_out", w_out, g_out, m_w_out, v_w_out)
    upd("ffn2_w_gate", ffn2_w_gate, g_gu2[:, :fs], m_ffn2_w_gate, v_ffn2_w_gate)
    upd("ffn2_w_up", ffn2_w_up, g_gu2[:, fs:], m_ffn2_w_up, v_ffn2_w_up)
    upd("ffn2_w_down", ffn2_w_down, g_d2, m_ffn2_w_down, v_ffn2_w_down)
    upd("conv_w", conv_w, g_conv, m_conv_w, v_conv_w)

    dmod_mine = lax.dynamic_slice(dmod_all, (0, chip * n_mod_shard), (8, n_mod_shard))
    g, d, nm, nv = _adamw_outer(w_ada[0], m_w_ada[0], v_w_ada[0], jnp.transpose(c_all), dmod_mine, "adamw_w_ada")
    res["w_ada"] = (g[None], d[None], nm[None], nv[None])

    rep = [("b_ada", b_ada, g_b_ada, m_b_ada, v_b_ada), ("ffn1_norm", ffn1_norm, g_ffn1_norm, m_ffn1_norm, v_ffn1_norm),
           ("mix_norm", mix_norm, g_mix_norm, m_mix_norm, v_mix_norm), ("ffn2_norm", ffn2_norm, g_ffn2_norm, m_ffn2_norm, v_ffn2_norm),
           ("q_norm", q_norm, g_q_norm, m_q_norm, v_q_norm), ("k_norm", k_norm, g_k_norm, m_k_norm, v_k_norm),
           ("a_log", a_log, g_a_log, m_a_log, v_a_log), ("dt_bias", dt_bias, g_dt_bias, m_dt_bias, v_dt_bias),
           ("delta_out_norm", delta_out_norm, g_dn, m_delta_out_norm, v_delta_out_norm)]
    n_rep = sum(-(-r[1].shape[1] // LANES) * LANES for r in rep)
    cols_rep = -(-n_rep // (8 * LANES)) * LANES

    def pack_rep(idx):
        return _flat_pad([jnp.pad(r[idx], ((0, 0), (0, -r[idx].shape[1] % LANES))) for r in rep], 8, cols_rep)

    d_rep, nm_rep, nv_rep = [a.reshape(-1) for a in _adamw(pack_rep(1), pack_rep(2), pack_rep(3), pack_rep(4), "adamw_small")]
    o2 = 0
    for name, w, g, _, _ in rep:
        n = w.shape[1]
        res[name] = (g, d_rep[o2:o2 + n].reshape(1, n), nm_rep[o2:o2 + n].reshape(1, n), nv_rep[o2:o2 + n].reshape(1, n))
        o2 += -(-n // LANES) * LANES

    order = ["w_ada", "b_ada", "ffn1_norm", "ffn1_w_gate", "ffn1_w_up", "ffn1_w_down", "mix_norm", "w_in", "conv_w", "q_norm",
             "k_norm", "a_log", "dt_bias", "delta_out_norm", "w_out", "ffn2_norm", "ffn2_w_gate", "ffn2_w_up", "ffn2_w_down"]
    return (loss, dx0[None], *[res[n][0] for n in order], *[res[n][1] for n in order],
            *[res[n][2] for n in order], *[res[n][3] for n in order])
```

```python
import functools
import math

import jax
import jax.numpy as jnp
from jax import lax
from jax.experimental import pallas as pl
from jax.experimental.pallas import tpu as pltpu

F32 = jnp.float32
BF16 = jnp.bfloat16
MESH = pl.DeviceIdType.MESH

HEAD_DIM = 128
N_ATTN_HEADS = 8
N_DELTA_HEADS = 8
DILATED_PATTERNS = ((128, 1), (512, 4), (2048, 16))
MAX_WINDOW = 2048
ROPE_THETA = 500000.0
ROPE_DIM = HEAD_DIM // 4
CONV_WIDTH = 4
CHUNK = 64
NORM_EPS = 1e-6
N_MOD = 9
ADAM_LR = 0.001
ADAM_B1 = 0.9
ADAM_B2 = 0.999
ADAM_EPS = 1e-08
ADAM_WD = 0.01
ADAM_STEP = 10

LANES = 128
VMEM_LIMIT = 56 * 1024 * 1024
ATTN_TILE = 512
HIGHEST = lax.Precision.HIGHEST


def _cparams(sem=None):
    return pltpu.CompilerParams(dimension_semantics=sem, vmem_limit_bytes=VMEM_LIMIT)


def _pick(dim, pref):
    if dim <= pref:
        return dim
    t = (pref // LANES) * LANES
    while t >= LANES:
        if dim % t == 0:
            return t
        t -= LANES
    return dim


def _sigmoid(x):
    return 1.0 / (1.0 + jnp.exp(-x))


def _silu(x):
    return x * _sigmoid(x)


def _softplus(x):
    return jnp.maximum(x, 0.0) + jnp.log(1.0 + jnp.exp(-jnp.abs(x)))


def _rms(x, gain):
    return x * lax.rsqrt(jnp.mean(x * x, axis=-1, keepdims=True) + NORM_EPS) * gain


def _l2(x):
    return x * lax.rsqrt(jnp.sum(x * x, axis=-1, keepdims=True) + NORM_EPS)


def _modulate(x, gain, shift, scale):
    return _rms(x, gain) * (1.0 + scale) + shift


def _dot(a, b):
    return lax.dot_general(a, b, (((1,), (0,)), ((), ())), precision=HIGHEST, preferred_element_type=F32)


def _bdot(a, b, dims):
    return lax.dot_general(a.astype(BF16), b.astype(BF16), (dims, ((), ())), preferred_element_type=F32)


_NN, _NT, _TN = ((1,), (0,)), ((1,), (1,)), ((0,), (0,))
HIGH = lax.Precision.HIGH


def _dot3(a, b, dims=_NN):
    return lax.dot_general(a, b, (dims, ((), ())), precision=HIGH, preferred_element_type=F32)


@jax.custom_vjp
def _mm_nn(a, b):
    return _bdot(a, b, _NN)


_mm_nn.defvjp(lambda a, b: (_bdot(a, b, _NN), (a, b)),
              lambda res, g: (_bdot(g, res[1], _NT), _bdot(res[0], g, _TN)))


@jax.custom_vjp
def _mm_nt(a, b):
    return _bdot(a, b, _NT)


_mm_nt.defvjp(lambda a, b: (_bdot(a, b, _NT), (a, b)),
              lambda res, g: (_bdot(g, res[1], _NN), _bdot(g, res[0], _TN)))


@jax.custom_vjp
def _tri_inv_saved(a, t_inv):
    return t_inv


_tri_inv_saved.defvjp(lambda a, t_inv: (t_inv, t_inv),
                      lambda t_inv, g: (-_dot3(t_inv, _dot3(g, t_inv, _NT), _TN), jnp.zeros_like(t_inv)))


_MM_DIMS = {"nn": ((1,), (0,)), "nt": ((1,), (1,)), "tn": ((0,), (0,))}


def _matmul(a, b, mode, out_dtype, name, tm=1024, tn=1024, tk=1024, col_slabs=False):
    if mode == "nn":
        (M, K), (_, N) = a.shape, b.shape
    elif mode == "nt":
        (M, K), (N, _) = a.shape, b.shape
    else:
        (K, M), (_, N) = a.shape, b.shape
    tm, tn, tk = _pick(M, tm), _pick(N, tn), _pick(K, tk)
    nk = K // tk
    dims = _MM_DIMS[mode]

    def body(a_ref, b_ref, o_ref, acc_ref):
        k = pl.program_id(2)
        p = _bdot(a_ref[...], b_ref[...], dims)

        @pl.when(k == 0)
        def _():
            acc_ref[...] = p

        @pl.when(k > 0)
        def _():
            acc_ref[...] += p

        @pl.when(k == nk - 1)
        def _():
            o_ref[...] = acc_ref[...].astype(out_dtype)

    a_spec = pl.BlockSpec((tk, tm), lambda i, j, k: (k, i)) if mode == "tn" else pl.BlockSpec((tm, tk), lambda i, j, k: (i, k))
    b_spec = pl.BlockSpec((tn, tk), lambda i, j, k: (j, k)) if mode == "nt" else pl.BlockSpec((tk, tn), lambda i, j, k: (k, j))
    if col_slabs:
        out_spec = pl.BlockSpec((None, tm, tn), lambda i, j, k: (j, i, 0))
        out_shape = jax.ShapeDtypeStruct((N // tn, M, tn), out_dtype)
    else:
        out_spec = pl.BlockSpec((tm, tn), lambda i, j, k: (i, j))
        out_shape = jax.ShapeDtypeStruct((M, N), out_dtype)
    return pl.pallas_call(
        body, name=name, grid=(M // tm, N // tn, nk),
        in_specs=[a_spec, b_spec], out_specs=out_spec, out_shape=out_shape,
        scratch_shapes=[pltpu.VMEM((tm, tn), F32)],
        compiler_params=_cparams(("parallel", "parallel", "arbitrary")),
    )(a, b)


def _row_spec(tr, d):
    return pl.BlockSpec((tr, d), lambda i: (i, 0))


def _vec_spec(d):
    return pl.BlockSpec((1, d), lambda i: (0, 0))


def _pre_fwd(x, gain, shift, scale, name):
    S, D = x.shape
    tr = _pick(S, 256)

    def body(x_ref, g_ref, sh_ref, sc_ref, h_ref):
        h_ref[...] = _modulate(x_ref[...], g_ref[...], sh_ref[...], sc_ref[...]).astype(BF16)

    return pl.pallas_call(
        body, name=name, grid=(S // tr,),
        in_specs=[_row_spec(tr, D), _vec_spec(D), _vec_spec(D), _vec_spec(D)],
        out_specs=_row_spec(tr, D), out_shape=jax.ShapeDtypeStruct((S, D), BF16),
        compiler_params=_cparams(("parallel",)),
    )(x, gain, shift, scale)


def _pre_bwd(x, gain, shift, scale, dh, dx_in, name):
    S, D = x.shape
    tr = _pick(S, 256)

    def body(x_ref, g_ref, sh_ref, sc_ref, dh_ref, dxin_ref, dx_ref, dg_ref, dsh_ref, dsc_ref):
        _, vjp = jax.vjp(_modulate, x_ref[...], g_ref[...], sh_ref[...], sc_ref[...])
        dx, dg, dsh, dsc = vjp(dh_ref[...])
        dx_ref[...] = dxin_ref[...] + dx

        @pl.when(pl.program_id(0) == 0)
        def _():
            dg_ref[...] = jnp.zeros_like(dg_ref)
            dsh_ref[...] = jnp.zeros_like(dsh_ref)
            dsc_ref[...] = jnp.zeros_like(dsc_ref)

        dg_ref[...] += dg
        dsh_ref[...] += dsh
        dsc_ref[...] += dsc

    vec = jax.ShapeDtypeStruct((1, D), F32)
    return pl.pallas_call(
        body, name=name, grid=(S // tr,),
        in_specs=[_row_spec(tr, D), _vec_spec(D), _vec_spec(D), _vec_spec(D), _row_spec(tr, D), _row_spec(tr, D)],
        out_specs=[_row_spec(tr, D), _vec_spec(D), _vec_spec(D), _vec_spec(D)],
        out_shape=[jax.ShapeDtypeStruct((S, D), F32), vec, vec, vec],
        compiler_params=_cparams(("arbitrary",)),
    )(x, gain, shift, scale, dh, dx_in)


def _residual_fwd(x, gate, f, coef, name):
    S, D = x.shape
    tr = _pick(S, 256)

    def body(x_ref, g_ref, f_ref, o_ref):
        o_ref[...] = x_ref[...] + coef * g_ref[...] * f_ref[...]

    return pl.pallas_call(
        body, name=name, grid=(S // tr,),
        in_specs=[_row_spec(tr, D), _vec_spec(D), _row_spec(tr, D)],
        out_specs=_row_spec(tr, D), out_shape=jax.ShapeDtypeStruct((S, D), F32),
        compiler_params=_cparams(("parallel",)),
    )(x, gate, f)


def _residual_bwd(gate, f, dxn, coef, name):
    S, D = f.shape
    tr = _pick(S, 256)

    def body(g_ref, f_ref, d_ref, df_ref, dg_ref):
        d = d_ref[...]
        df_ref[...] = (coef * g_ref[...] * d).astype(BF16)

        @pl.when(pl.program_id(0) == 0)
        def _():
            dg_ref[...] = jnp.zeros_like(dg_ref)

        dg_ref[...] += jnp.sum(coef * f_ref[...] * d, axis=0, keepdims=True)

    return pl.pallas_call(
        body, name=name, grid=(S // tr,),
        in_specs=[_vec_spec(D), _row_spec(tr, D), _row_spec(tr, D)],
        out_specs=[_row_spec(tr, D), _vec_spec(D)],
        out_shape=[jax.ShapeDtypeStruct((S, D), BF16), jax.ShapeDtypeStruct((1, D), F32)],
        compiler_params=_cparams(("arbitrary",)),
    )(gate, f, dxn)


def _swiglu_fn(a, b):
    return _silu(a) * b


def _swiglu_fwd(ab, fs, name):
    S, F2 = ab.shape
    tr = _pick(S, 256)

    def body(ab_ref, s_ref):
        s_ref[...] = _swiglu_fn(ab_ref[:, :fs], ab_ref[:, fs:]).astype(BF16)

    return pl.pallas_call(
        body, name=name, grid=(S // tr, F2 // (2 * fs)),
        in_specs=[pl.BlockSpec((tr, 2 * fs), lambda i, j: (i, j))],
        out_specs=pl.BlockSpec((tr, fs), lambda i, j: (i, j)),
        out_shape=jax.ShapeDtypeStruct((S, F2 // 2), BF16),
        compiler_params=_cparams(("parallel", "parallel")),
    )(ab)


def _swiglu_bwd(ab, ds, fs, name):
    S, F2 = ab.shape
    tr = _pick(S, 256)

    def body(ab_ref, ds_ref, dab_ref):
        _, vjp = jax.vjp(_swiglu_fn, ab_ref[:, :fs], ab_ref[:, fs:])
        da, db = vjp(ds_ref[...])
        dab_ref[:, :fs] = da.astype(BF16)
        dab_ref[:, fs:] = db.astype(BF16)

    return pl.pallas_call(
        body, name=name, grid=(S // tr, F2 // (2 * fs)),
        in_specs=[pl.BlockSpec((tr, 2 * fs), lambda i, j: (i, j)), pl.BlockSpec((tr, fs), lambda i, j: (i, j))],
        out_specs=pl.BlockSpec((tr, 2 * fs), lambda i, j: (i, j)),
        out_shape=jax.ShapeDtypeStruct((S, F2), BF16),
        compiler_params=_cparams(("parallel", "parallel")),
    )(ab, ds)


def _loss_head(y, target, name):
    S, D = y.shape
    tr = _pick(S, 256)

    def body(y_ref, t_ref, l_ref, dy_ref):
        e = y_ref[...] - t_ref[...]
        dy_ref[...] = e * (1.0 / D)

        @pl.when(pl.program_id(0) == 0)
        def _():
            l_ref[...] = jnp.zeros_like(l_ref)

        l_ref[...] += jnp.sum(jnp.sum(e * e, axis=-1, keepdims=True), axis=0, keepdims=True) * (0.5 / D)

    return pl.pallas_call(
        body, name=name, grid=(S // tr,),
        in_specs=[_row_spec(tr, D), _row_spec(tr, D)],
        out_specs=[pl.BlockSpec((1, 1), lambda i: (0, 0)), _row_spec(tr, D)],
        out_shape=[jax.ShapeDtypeStruct((1, 1), F32), jax.ShapeDtypeStruct((S, D), F32)],
        compiler_params=_cparams(("arbitrary",)),
    )(y, target)


def _rope(y, cc, s1, s2):
    return y * cc + pltpu.roll(y, LANES - ROPE_DIM // 2, 1) * s1 + pltpu.roll(y, ROPE_DIM // 2, 1) * s2


def _rope_t(d, cc, s1, s2):
    return d * cc + pltpu.roll(d * s1, ROPE_DIM // 2, 1) + pltpu.roll(d * s2, LANES - ROPE_DIM // 2, 1)


def _head_spec(tr, col0):
    return pl.BlockSpec((tr, HEAD_DIM), lambda i, h: (i, col0 + h))


def _tab_spec(tr):
    return pl.BlockSpec((tr, HEAD_DIM), lambda i, h: (i, 0))


def _gain_spec():
    return pl.BlockSpec((1, HEAD_DIM), lambda i, h: (0, 0))


def _attn_prep_fwd(proj, q_gain, k_gain, cc, s1, s2, name):
    S = proj.shape[0]
    H = N_ATTN_HEADS
    tr = _pick(S, 512)

    def body(q_ref, k_ref, qg_ref, kg_ref, cc_ref, s1_ref, s2_ref, qo_ref, ko_ref):
        cc, s1, s2 = cc_ref[...], s1_ref[...], s2_ref[...]
        qo_ref[...] = _rope(_rms(q_ref[...], qg_ref[...]), cc, s1, s2).astype(BF16)
        ko_ref[...] = _rope(_rms(k_ref[...], kg_ref[...]), cc, s1, s2).astype(BF16)

    out = jax.ShapeDtypeStruct((S, H * HEAD_DIM), BF16)
    return pl.pallas_call(
        body, name=name, grid=(S // tr, H),
        in_specs=[_head_spec(tr, 0), _head_spec(tr, H), _gain_spec(), _gain_spec(), _tab_spec(tr), _tab_spec(tr), _tab_spec(tr)],
        out_specs=[_head_spec(tr, 0), _head_spec(tr, 0)], out_shape=[out, out],
        compiler_params=_cparams(("parallel", "parallel")),
    )(proj, proj, q_gain, k_gain, cc, s1, s2)


def _attn_prep_bwd(proj, q_gain, k_gain, cc, s1, s2, dq, dk, name):
    S = proj.shape[0]
    H = N_ATTN_HEADS
    tr = _pick(S, 512)

    def body(q_ref, k_ref, qg_ref, kg_ref, cc_ref, s1_ref, s2_ref, dq_ref, dk_ref, dpq_ref, dpk_ref, dqg_ref, dkg_ref):
        cc, s1, s2 = cc_ref[...], s1_ref[...], s2_ref[...]

        @pl.when((pl.program_id(0) == 0) & (pl.program_id(1) == 0))
        def _():
            dqg_ref[...] = jnp.zeros_like(dqg_ref)
            dkg_ref[...] = jnp.zeros_like(dkg_ref)

        _, vjp_q = jax.vjp(_rms, q_ref[...], qg_ref[...])
        dxq, dgq = vjp_q(_rope_t(dq_ref[...], cc, s1, s2))
        _, vjp_k = jax.vjp(_rms, k_ref[...], kg_ref[...])
        dxk, dgk = vjp_k(_rope_t(dk_ref[...], cc, s1, s2))
        dpq_ref[...] = dxq.astype(BF16)
        dpk_ref[...] = dxk.astype(BF16)
        dqg_ref[...] += dgq
        dkg_ref[...] += dgk

    out = jax.ShapeDtypeStruct((S, H * HEAD_DIM), BF16)
    gout = jax.ShapeDtypeStruct((1, HEAD_DIM), F32)
    return pl.pallas_call(
        body, name=name, grid=(S // tr, H),
        in_specs=[_head_spec(tr, 0), _head_spec(tr, H), _gain_spec(), _gain_spec(), _tab_spec(tr), _tab_spec(tr), _tab_spec(tr),
                  _head_spec(tr, 0), _head_spec(tr, 0)],
        out_specs=[_head_spec(tr, 0), _head_spec(tr, 0), _gain_spec(), _gain_spec()], out_shape=[out, out, gout, gout],
        compiler_params=_cparams(("arbitrary", "arbitrary")),
    )(proj, proj, q_gain, k_gain, cc, s1, s2, dq, dk)


def _multiplicity(j, t):
    ti = lax.broadcasted_iota(jnp.int32, (t, t), 0)
    si = lax.broadcasted_iota(jnp.int32, (t, t), 1)
    delta = j * t + ti - si
    cnt = jnp.zeros((t, t), F32)
    for window, dil in DILATED_PATTERNS:
        ok = (delta >= 0) & ((delta & (dil - 1)) == 0) & (delta <= window)
        cnt = cnt + ok.astype(F32)
    return cnt


_NEG = -1e30


def _log_multiplicity_table(t):
    cnt = jnp.stack([_multiplicity(j, t) for j in range(MAX_WINDOW // t + 1)])
    return jnp.where(cnt > 0.0, jnp.log(jnp.maximum(cnt, 1.0)), _NEG)


def _bias_spec(t):
    return pl.BlockSpec((MAX_WINDOW // t + 1, t, t), lambda h, i, j: (0, 0, 0))


def _attn_fwd(q, k, proj, v_col0, bias, name):
    S = q.shape[0]
    H = N_ATTN_HEADS
    t = _pick(S, ATTN_TILE)
    nq = S // t
    nj = MAX_WINDOW // t + 1
    scale = HEAD_DIM ** -0.5

    def body(q_ref, k_ref, v_ref, b_ref, o_ref, ob_ref, lse_ref, m_sc, l_sc, acc_sc):
        qb, j = pl.program_id(1), pl.program_id(2)

        @pl.when(j == 0)
        def _():
            m_sc[...] = jnp.full_like(m_sc, _NEG)
            l_sc[...] = jnp.zeros_like(l_sc)
            acc_sc[...] = jnp.zeros_like(acc_sc)

        @pl.when(qb - j >= 0)
        def _():
            s = _bdot(q_ref[...], k_ref[...], ((1,), (1,))) * scale + b_ref[j]
            m_prev = m_sc[...]
            m_new = jnp.maximum(m_prev, jnp.max(s, axis=-1, keepdims=True))
            alpha = jnp.exp(m_prev - m_new)
            p = jnp.exp(s - m_new)
            l_sc[...] = alpha * l_sc[...] + jnp.sum(p, axis=-1, keepdims=True)
            acc_sc[...] = alpha * acc_sc[...] + _bdot(p, v_ref[...], ((1,), (0,)))
            m_sc[...] = m_new

        @pl.when(j == nj - 1)
        def _():
            o = acc_sc[...] / l_sc[...]
            o_ref[...] = o
            ob_ref[...] = o.astype(BF16)
            lse_ref[...] = jnp.broadcast_to(m_sc[...] + jnp.log(l_sc[...]), (t, HEAD_DIM))

    qspec = pl.BlockSpec((t, HEAD_DIM), lambda h, i, j: (i, h))
    kspec = pl.BlockSpec((t, HEAD_DIM), lambda h, i, j: (jnp.maximum(i - j, 0), h))
    vspec = pl.BlockSpec((t, HEAD_DIM), lambda h, i, j: (jnp.maximum(i - j, 0), v_col0 + h))
    return pl.pallas_call(
        body, name=name, grid=(H, nq, nj),
        in_specs=[qspec, kspec, vspec, _bias_spec(t)], out_specs=[qspec, qspec, qspec],
        out_shape=[jax.ShapeDtypeStruct((S, H * HEAD_DIM), F32), jax.ShapeDtypeStruct((S, H * HEAD_DIM), BF16),
                   jax.ShapeDtypeStruct((S, H * HEAD_DIM), F32)],
        scratch_shapes=[pltpu.VMEM((t, 1), F32), pltpu.VMEM((t, 1), F32), pltpu.VMEM((t, HEAD_DIM), F32)],
        compiler_params=_cparams(("parallel", "parallel", "arbitrary")),
    )(q, k, proj, bias)


def _attn_probs(q, k, lse, bias_tile, scale):
    return jnp.exp(_bdot(q, k, ((1,), (1,))) * scale + bias_tile - lse)


def _attn_bwd_dq(q, k, proj, v_col0, o, lse, do, do_col0, bias, name):
    S = q.shape[0]
    H = N_ATTN_HEADS
    t = _pick(S, ATTN_TILE)
    nq = S // t
    nj = MAX_WINDOW // t + 1
    scale = HEAD_DIM ** -0.5

    def body(q_ref, k_ref, v_ref, o_ref, lse_ref, do_ref, b_ref, dq_ref, acc_sc):
        qb, j = pl.program_id(1), pl.program_id(2)

        @pl.when(j == 0)
        def _():
            acc_sc[...] = jnp.zeros_like(acc_sc)

        @pl.when(qb - j >= 0)
        def _():
            do = do_ref[...]
            dsum = jnp.sum(do * o_ref[...], axis=-1, keepdims=True)
            lse = jnp.max(lse_ref[...], axis=-1, keepdims=True)
            p = _attn_probs(q_ref[...], k_ref[...], lse, b_ref[j], scale)
            dp = _bdot(do, v_ref[...], ((1,), (1,)))
            ds = p * (dp - dsum)
            acc_sc[...] += _bdot(ds, k_ref[...], ((1,), (0,))) * scale

        @pl.when(j == nj - 1)
        def _():
            dq_ref[...] = acc_sc[...]

    qspec = pl.BlockSpec((t, HEAD_DIM), lambda h, i, j: (i, h))
    dospec = pl.BlockSpec((t, HEAD_DIM), lambda h, i, j: (i, do_col0 + h))
    kspec = pl.BlockSpec((t, HEAD_DIM), lambda h, i, j: (jnp.maximum(i - j, 0), h))
    vspec = pl.BlockSpec((t, HEAD_DIM), lambda h, i, j: (jnp.maximum(i - j, 0), v_col0 + h))
    return pl.pallas_call(
        body, name=name, grid=(H, nq, nj),
        in_specs=[qspec, kspec, vspec, qspec, qspec, dospec, _bias_spec(t)], out_specs=qspec,
        out_shape=jax.ShapeDtypeStruct((S, H * HEAD_DIM), F32),
        scratch_shapes=[pltpu.VMEM((t, HEAD_DIM), F32)],
        compiler_params=_cparams(("parallel", "parallel", "arbitrary")),
    )(q, k, proj, o, lse, do, bias)


def _attn_bwd_dkv(q, k, proj, v_col0, o, lse, do, do_col0, bias, name):
    S = q.shape[0]
    H = N_ATTN_HEADS
    t = _pick(S, ATTN_TILE)
    nq = S // t
    nj = MAX_WINDOW // t + 1
    scale = HEAD_DIM ** -0.5

    def body(q_ref, k_ref, v_ref, o_ref, lse_ref, do_ref, b_ref, dk_ref, dv_ref, dk_sc, dv_sc):
        kb, j = pl.program_id(1), pl.program_id(2)

        @pl.when(j == 0)
        def _():
            dk_sc[...] = jnp.zeros_like(dk_sc)
            dv_sc[...] = jnp.zeros_like(dv_sc)

        @pl.when(kb + j < nq)
        def _():
            do = do_ref[...]
            dsum = jnp.sum(do * o_ref[...], axis=-1, keepdims=True)
            lse = jnp.max(lse_ref[...], axis=-1, keepdims=True)
            p = _attn_probs(q_ref[...], k_ref[...], lse, b_ref[j], scale)
            dp = _bdot(do, v_ref[...], ((1,), (1,)))
            ds = p * (dp - dsum)
            dv_sc[...] += _bdot(p, do, ((0,), (0,)))
            dk_sc[...] += _bdot(ds, q_ref[...], ((0,), (0,))) * scale

        @pl.when(j == nj - 1)
        def _():
            dk_ref[...] = dk_sc[...]
            dv_ref[...] = dv_sc[...].astype(BF16)

    def qrow(h, i, j):
        return jnp.minimum(i + j, nq - 1)

    qspec = pl.BlockSpec((t, HEAD_DIM), lambda h, i, j: (qrow(h, i, j), h))
    dospec = pl.BlockSpec((t, HEAD_DIM), lambda h, i, j: (qrow(h, i, j), do_col0 + h))
    kspec = pl.BlockSpec((t, HEAD_DIM), lambda h, i, j: (i, h))
    vspec = pl.BlockSpec((t, HEAD_DIM), lambda h, i, j: (i, v_col0 + h))
    return pl.pallas_call(
        body, name=name, grid=(H, nq, nj),
        in_specs=[qspec, kspec, vspec, qspec, qspec, dospec, _bias_spec(t)], out_specs=[kspec, kspec],
        out_shape=[jax.ShapeDtypeStruct((S, H * HEAD_DIM), F32), jax.ShapeDtypeStruct((S, H * HEAD_DIM), BF16)],
        scratch_shapes=[pltpu.VMEM((t, HEAD_DIM), F32), pltpu.VMEM((t, HEAD_DIM), F32)],
        compiler_params=_cparams(("parallel", "parallel", "arbitrary")),
    )(q, k, proj, o, lse, do, bias)


def _conv_pre(x_ref, w_ref):
    x = x_ref[...]
    rows = lax.broadcasted_iota(jnp.int32, x.shape, 0)
    shifted = [x]
    acc = x * w_ref[pl.ds(CONV_WIDTH - 1, 1), :]
    for sft in range(1, CONV_WIDTH):
        xs = jnp.where(rows >= sft, pltpu.roll(x, sft, 0), 0.0)
        shifted.append(xs)
        acc = acc + xs * w_ref[pl.ds(CONV_WIDTH - 1 - sft, 1), :]
    return acc, shifted


def _conv_fwd(proj, col0, width, w, name):
    S = proj.shape[0]

    def body(x_ref, w_ref, y_ref):
        acc, _ = _conv_pre(x_ref, w_ref)
        y_ref[...] = _silu(acc)

    return pl.pallas_call(
        body, name=name, grid=(width // LANES,),
        in_specs=[pl.BlockSpec((S, LANES), lambda c: (0, col0 + c)), pl.BlockSpec((CONV_WIDTH, LANES), lambda c: (0, c))],
        out_specs=pl.BlockSpec((S, LANES), lambda c: (0, c)),
        out_shape=jax.ShapeDtypeStruct((S, width), F32),
        compiler_params=_cparams(("parallel",)),
    )(proj, w)


def _conv_bwd(proj, col0, width, w, dy, name):
    S = proj.shape[0]

    def body(x_ref, w_ref, d_ref, dx_ref, dw_ref):
        acc, shifted = _conv_pre(x_ref, w_ref)
        sig = _sigmoid(acc)
        da = d_ref[...] * (sig * (1.0 + acc * (1.0 - sig)))
        rows = lax.broadcasted_iota(jnp.int32, da.shape, 0)
        dx = da * w_ref[pl.ds(CONV_WIDTH - 1, 1), :]
        dw_ref[pl.ds(CONV_WIDTH - 1, 1), :] = jnp.sum(da * shifted[0], axis=0, keepdims=True)
        for sft in range(1, CONV_WIDTH):
            back = jnp.where(rows < S - sft, pltpu.roll(da, S - sft, 0), 0.0)
            dx = dx + back * w_ref[pl.ds(CONV_WIDTH - 1 - sft, 1), :]
            dw_ref[pl.ds(CONV_WIDTH - 1 - sft, 1), :] = jnp.sum(da * shifted[sft], axis=0, keepdims=True)
        dx_ref[...] = dx.astype(BF16)

    return pl.pallas_call(
        body, name=name, grid=(width // LANES,),
        in_specs=[pl.BlockSpec((S, LANES), lambda c: (0, col0 + c)), pl.BlockSpec((CONV_WIDTH, LANES), lambda c: (0, c)),
                  pl.BlockSpec((S, LANES), lambda c: (0, c))],
        out_specs=[pl.BlockSpec((S, LANES), lambda c: (0, c)), pl.BlockSpec((CONV_WIDTH, LANES), lambda c: (0, c))],
        out_shape=[jax.ShapeDtypeStruct((S, width), BF16), jax.ShapeDtypeStruct((CONV_WIDTH, width), F32)],
        compiler_params=_cparams(("parallel",)),
    )(proj, w, dy)


PREP_CHUNKS = 8


def _chunks_prep(qraws, kraws, vs, abs_, alog_row, dtb_row, mask_g, mask_b, t_saved=None):
    n = len(qraws)
    c = qraws[0].shape[0]
    mm_nt, mm_nn = (_mm_nt, _mm_nn) if t_saved is not None else (lambda p, r: _bdot(p, r, _NT), lambda p, r: _bdot(p, r, _NN))
    row = lax.broadcasted_iota(jnp.int32, (c, c), 0)
    col = lax.broadcasted_iota(jnp.int32, (c, c), 1)
    tril, strict, eye = row >= col, row > col, row == col
    eyef = eye.astype(F32)
    neg_rate = -jnp.exp(alog_row)
    q, k, beta, gc_col, gamma, kb, g_last = [], [], [], [], [], [], []
    for i in range(n):
        q.append(_l2(qraws[i]) * (HEAD_DIM ** -0.5))
        k.append(_l2(kraws[i]))
        gfull = neg_rate * _softplus(abs_[i] + dtb_row)
        g = jnp.sum(jnp.where(mask_g, gfull, 0.0), axis=-1, keepdims=True)
        beta.append(jnp.sum(jnp.where(mask_b, _sigmoid(abs_[i]), 0.0), axis=-1, keepdims=True))
        g_row = jnp.sum(jnp.where(eye, g, 0.0), axis=0, keepdims=True)
        gc_col.append(jnp.sum(jnp.where(tril, g_row, 0.0), axis=1, keepdims=True))
        gc_row = jnp.sum(jnp.where(row <= col, g, 0.0), axis=0, keepdims=True)
        gamma.append(jnp.where(tril, jnp.exp(jnp.where(tril, gc_col[i] - gc_row, 0.0)), 0.0))
        kb.append(k[i] * beta[i])
        g_last.append(jnp.sum(g, axis=0, keepdims=True))
    a = [jnp.where(strict, mm_nt(kb[i], k[i]) * gamma[i], 0.0) for i in range(n)]
    if t_saved is None:
        t_inv = [eyef - a[i] for i in range(n)]
        p = a
        for _ in range(int(math.log2(c)) - 1):
            p = [_dot3(p[i], p[i]) for i in range(n)]
            t_inv = [_dot3(t_inv[i], eyef + p[i]) for i in range(n)]
    else:
        t_inv = [_tri_inv_saved(a[i], t_saved[i]) for i in range(n)]
    egc = [jnp.exp(gc_col[i]) for i in range(n)]
    u = [mm_nn(t_inv[i], vs[i] * beta[i]) for i in range(n)]
    w = [mm_nn(t_inv[i], kb[i] * egc[i]) for i in range(n)]
    intra = [mm_nt(q[i], k[i]) * gamma[i] for i in range(n)]
    out = []
    for i in range(n):
        kt = k[i] * jnp.exp(g_last[i] - gc_col[i])
        dec = jnp.broadcast_to(jnp.exp(g_last[i]), (1, HEAD_DIM))
        one = (u[i], w[i], q[i] * egc[i], kt, intra[i], dec)
        out.append(one + (t_inv[i],) if t_saved is None else one)
    return out


def _lane_masks(h):
    lane = lax.broadcasted_iota(jnp.int32, (1, LANES), 1)
    return lane == h, lane == N_DELTA_HEADS + h


def _prep_specs(tr, ab_col):
    H = N_DELTA_HEADS
    return [
        pl.BlockSpec((tr, HEAD_DIM), lambda i, h: (i, h)),
        pl.BlockSpec((tr, HEAD_DIM), lambda i, h: (i, H + h)),
        pl.BlockSpec((tr, HEAD_DIM), lambda i, h: (i, 2 * H + h)),
        pl.BlockSpec((tr, LANES), lambda i, h: (i, ab_col)),
        pl.BlockSpec((1, LANES), lambda i, h: (0, 0)),
        pl.BlockSpec((1, LANES), lambda i, h: (0, 0)),
    ]


def _prep_out_specs(tr):
    nc = tr // CHUNK
    hs = pl.BlockSpec((tr, HEAD_DIM), lambda i, h: (i, h))
    return [hs, hs, hs, hs,
            pl.BlockSpec((None, tr, CHUNK), lambda i, h: (h, i, 0)),
            pl.BlockSpec((None, nc, 1, HEAD_DIM), lambda i, h: (h, i, 0, 0)),
            pl.BlockSpec((None, tr, CHUNK), lambda i, h: (h, i, 0))]


def _prep_out_shapes(S):
    H = N_DELTA_HEADS
    hs = jax.ShapeDtypeStruct((S, H * HEAD_DIM), F32)
    sq = jax.ShapeDtypeStruct((H, S, CHUNK), F32)
    return [hs, hs, hs, hs, sq, jax.ShapeDtypeStruct((H, S // CHUNK, 1, HEAD_DIM), F32), sq]


def _delta_prep_fwd(dqkv, proj, ab_col, alog_row, dtb_row, name):
    S = dqkv.shape[0]
    tr = min(S, PREP_CHUNKS * CHUNK)
    nc = tr // CHUNK

    def body(q_ref, k_ref, v_ref, ab_ref, al_ref, dt_ref, u_ref, w_ref, qd_ref, kt_ref, in_ref, dec_ref, ti_ref):
        mask_g, mask_b = _lane_masks(pl.program_id(1))
        rows = [pl.ds(ci * CHUNK, CHUNK) for ci in range(nc)]
        outs = _chunks_prep([q_ref[rs, :] for rs in rows], [k_ref[rs, :] for rs in rows], [v_ref[rs, :] for rs in rows],
                            [ab_ref[rs, :] for rs in rows], al_ref[...], dt_ref[...], mask_g, mask_b)
        for ci, rs in enumerate(rows):
            u, w, qd, kt, intra, dec, t_inv = outs[ci]
            u_ref[rs, :] = u
            w_ref[rs, :] = w
            qd_ref[rs, :] = qd
            kt_ref[rs, :] = kt
            in_ref[rs, :] = intra
            dec_ref[ci] = dec
            ti_ref[rs, :] = t_inv

    return pl.pallas_call(
        body, name=name, grid=(S // tr, N_DELTA_HEADS),
        in_specs=_prep_specs(tr, ab_col), out_specs=_prep_out_specs(tr), out_shape=_prep_out_shapes(S),
        compiler_params=_cparams(("parallel", "parallel")),
    )(dqkv, dqkv, dqkv, proj, alog_row, dtb_row)


def _delta_prep_bwd(dqkv, proj, ab_col, alog_row, dtb_row, cots, t_inv, name):
    S = dqkv.shape[0]
    H = N_DELTA_HEADS
    tr = min(S, PREP_CHUNKS * CHUNK)
    nc = tr // CHUNK

    def body(q_ref, k_ref, v_ref, ab_ref, al_ref, dt_ref, du_ref, dw_ref, dqd_ref, dkt_ref, din_ref, ddec_ref, ti_ref,
             dq_ref, dk_ref, dv_ref, dab_ref, dal_ref, ddt_ref, dab_sc):
        h = pl.program_id(1)
        mask_g, mask_b = _lane_masks(h)

        @pl.when((pl.program_id(0) == 0) & (h == 0))
        def _():
            dal_ref[...] = jnp.zeros_like(dal_ref)
            ddt_ref[...] = jnp.zeros_like(ddt_ref)

        @pl.when(h == 0)
        def _():
            dab_sc[...] = jnp.zeros_like(dab_sc)

        rows = [pl.ds(ci * CHUNK, CHUNK) for ci in range(nc)]
        fn = functools.partial(_chunks_prep, mask_g=mask_g, mask_b=mask_b, t_saved=[ti_ref[rs, :] for rs in rows])
        _, vjp = jax.vjp(fn, [q_ref[rs, :] for rs in rows], [k_ref[rs, :] for rs in rows], [v_ref[rs, :] for rs in rows],
                         [ab_ref[rs, :] for rs in rows], al_ref[...], dt_ref[...])
        dqs, dks, dvs, dabs, dal, ddt = vjp([(du_ref[rs, :], dw_ref[rs, :], dqd_ref[rs, :], dkt_ref[rs, :], din_ref[rs, :],
                                              ddec_ref[ci]) for ci, rs in enumerate(rows)])
        for ci, rs in enumerate(rows):
            dq_ref[rs, :] = dqs[ci]
            dk_ref[rs, :] = dks[ci]
            dv_ref[rs, :] = dvs[ci]
            dab_sc[rs, :] += dabs[ci]
        dal_ref[...] += dal
        ddt_ref[...] += ddt

        @pl.when(h == H - 1)
        def _():
            dab_ref[...] = dab_sc[...].astype(BF16)

    hs = pl.BlockSpec((tr, HEAD_DIM), lambda i, h: (i, h))
    hshape = jax.ShapeDtypeStruct((S, H * HEAD_DIM), F32)
    row = pl.BlockSpec((1, LANES), lambda i, h: (0, 0))
    rshape = jax.ShapeDtypeStruct((1, LANES), F32)
    return pl.pallas_call(
        body, name=name, grid=(S // tr, H),
        in_specs=_prep_specs(tr, ab_col) + _prep_out_specs(tr),
        out_specs=[hs, hs, hs, pl.BlockSpec((tr, LANES), lambda i, h: (i, 0)), row, row],
        out_shape=[hshape, hshape, hshape, jax.ShapeDtypeStruct((S, LANES), BF16), rshape, rshape],
        scratch_shapes=[pltpu.VMEM((tr, LANES), F32)],
        compiler_params=_cparams(("arbitrary", "arbitrary")),
    )(dqkv, dqkv, dqkv, proj, alog_row, dtb_row, *cots, t_inv)


def _scan_steps(states, us, ws, qds, kts, intras, decs):
    hs = range(len(states))
    v_new = [us[h] - _dot3(ws[h], states[h]) for h in hs]
    o_state = [_dot3(qds[h], states[h]) for h in hs]
    o_intra = [_dot3(intras[h], v_new[h]) for h in hs]
    grown = [_dot3(kts[h], v_new[h], _TN) for h in hs]
    return [o_state[h] + o_intra[h] for h in hs], [states[h] * decs[h] + grown[h] for h in hs]


def _scan_specs(rev, n):
    H = N_DELTA_HEADS

    def cix(i):
        return (n - 1 - i) if rev else i

    row = pl.BlockSpec((CHUNK, H * HEAD_DIM), lambda i: (cix(i), 0))
    return row, pl.BlockSpec((H, CHUNK, CHUNK), lambda i: (0, cix(i), 0)), \
        pl.BlockSpec((H, 1, 1, HEAD_DIM), lambda i: (0, cix(i), 0, 0)), \
        pl.BlockSpec((1, H, HEAD_DIM, HEAD_DIM), lambda i: (cix(i), 0, 0, 0))


def _delta_scan_fwd(u, w, qd, kt, intra, dec, name):
    S = u.shape[0]
    H = N_DELTA_HEADS
    n = S // CHUNK
    row, ispec, dspec, sspec = _scan_specs(False, n)

    def body(u_ref, w_ref, qd_ref, kt_ref, in_ref, dec_ref, o_ref, st_ref, s_sc):
        @pl.when(pl.program_id(0) == 0)
        def _():
            s_sc[...] = jnp.zeros_like(s_sc)

        cols = [pl.ds(h * HEAD_DIM, HEAD_DIM) for h in range(H)]
        states = [s_sc[h] for h in range(H)]
        outs, new = _scan_steps(states, [u_ref[:, cs] for cs in cols], [w_ref[:, cs] for cs in cols],
                                [qd_ref[:, cs] for cs in cols], [kt_ref[:, cs] for cs in cols],
                                [in_ref[h] for h in range(H)], [dec_ref[h, 0] for h in range(H)])
        for h, cs in enumerate(cols):
            st_ref[0, h] = states[h]
            o_ref[:, cs] = outs[h]
            s_sc[h] = new[h]

    return pl.pallas_call(
        body, name=name, grid=(n,),
        in_specs=[row, row, row, row, ispec, dspec], out_specs=[row, sspec],
        out_shape=[jax.ShapeDtypeStruct((S, H * HEAD_DIM), F32), jax.ShapeDtypeStruct((n, H, HEAD_DIM, HEAD_DIM), F32)],
        scratch_shapes=[pltpu.VMEM((H, HEAD_DIM, HEAD_DIM), F32)],
        compiler_params=_cparams(("arbitrary",)),
    )(u, w, qd, kt, intra, dec)


def _delta_scan_bwd(u, w, qd, kt, intra, dec, states, do, name):
    S = u.shape[0]
    H = N_DELTA_HEADS
    n = S // CHUNK
    row, ispec, dspec, sspec = _scan_specs(True, n)

    def body(u_ref, w_ref, qd_ref, kt_ref, in_ref, dec_ref, st_ref, do_ref,
             du_ref, dw_ref, dqd_ref, dkt_ref, din_ref, ddec_ref, ds_sc):
        @pl.when(pl.program_id(0) == 0)
        def _():
            ds_sc[...] = jnp.zeros_like(ds_sc)

        cols = [pl.ds(h * HEAD_DIM, HEAD_DIM) for h in range(H)]
        _, vjp = jax.vjp(_scan_steps, [st_ref[0, h] for h in range(H)], [u_ref[:, cs] for cs in cols],
                         [w_ref[:, cs] for cs in cols], [qd_ref[:, cs] for cs in cols], [kt_ref[:, cs] for cs in cols],
                         [in_ref[h] for h in range(H)], [dec_ref[h, 0] for h in range(H)])
        dstate, du, dw, dqd, dkt, din, ddec = vjp(([do_ref[:, cs] for cs in cols], [ds_sc[h] for h in range(H)]))
        for h, cs in enumerate(cols):
            du_ref[:, cs] = du[h]
            dw_ref[:, cs] = dw[h]
            dqd_ref[:, cs] = dqd[h]
            dkt_ref[:, cs] = dkt[h]
            din_ref[h] = din[h]
            ddec_ref[h, 0] = ddec[h]
            ds_sc[h] = dstate[h]

    hshape = jax.ShapeDtypeStruct((S, H * HEAD_DIM), F32)
    return pl.pallas_call(
        body, name=name, grid=(n,),
        in_specs=[row, row, row, row, ispec, dspec, sspec, row],
        out_specs=[row, row, row, row, ispec, dspec],
        out_shape=[hshape, hshape, hshape, hshape, jax.ShapeDtypeStruct((H, S, CHUNK), F32),
                   jax.ShapeDtypeStruct((H, n, 1, HEAD_DIM), F32)],
        scratch_shapes=[pltpu.VMEM((H, HEAD_DIM, HEAD_DIM), F32)],
        compiler_params=_cparams(("arbitrary",)),
    )(u, w, qd, kt, intra, dec, states, do)


def _gated_norm(od, z, gain):
    return _rms(od, gain) * _silu(z)


def _post_fwd(od, proj, z_col0, gain, name):
    S = od.shape[0]
    H = N_DELTA_HEADS
    tr = _pick(S, 512)

    def body(od_ref, z_ref, g_ref, o_ref):
        o_ref[...] = _gated_norm(od_ref[...], z_ref[...], g_ref[...]).astype(BF16)

    return pl.pallas_call(
        body, name=name, grid=(S // tr, H),
        in_specs=[_head_spec(tr, 0), _head_spec(tr, z_col0), _gain_spec()],
        out_specs=_head_spec(tr, 0), out_shape=jax.ShapeDtypeStruct((S, H * HEAD_DIM), BF16),
        compiler_params=_cparams(("parallel", "parallel")),
    )(od, proj, gain)


def _post_bwd(od, proj, z_col0, gain, do, do_col0, name):
    S = od.shape[0]
    H = N_DELTA_HEADS
    tr = _pick(S, 512)

    def body(od_ref, z_ref, g_ref, do_ref, dod_ref, dz_ref, dg_ref):
        @pl.when((pl.program_id(0) == 0) & (pl.program_id(1) == 0))
        def _():
            dg_ref[...] = jnp.zeros_like(dg_ref)

        _, vjp = jax.vjp(_gated_norm, od_ref[...], z_ref[...], g_ref[...])
        dod, dz, dg = vjp(do_ref[...])
        dod_ref[...] = dod
        dz_ref[...] = dz.astype(BF16)
        dg_ref[...] += dg

    return pl.pallas_call(
        body, name=name, grid=(S // tr, H),
        in_specs=[_head_spec(tr, 0), _head_spec(tr, z_col0), _gain_spec(), _head_spec(tr, do_col0)],
        out_specs=[_head_spec(tr, 0), _head_spec(tr, 0), _gain_spec()],
        out_shape=[jax.ShapeDtypeStruct((S, H * HEAD_DIM), F32), jax.ShapeDtypeStruct((S, H * HEAD_DIM), BF16),
                   jax.ShapeDtypeStruct((1, HEAD_DIM), F32)],
        compiler_params=_cparams(("arbitrary", "arbitrary")),
    )(od, proj, gain, do)


def _adam_math(w, g, m, v):
    m = ADAM_B1 * m + (1.0 - ADAM_B1) * g
    v = ADAM_B2 * v + (1.0 - ADAM_B2) * (g * g)
    m_hat = m / (1.0 - ADAM_B1 ** ADAM_STEP)
    v_hat = v / (1.0 - ADAM_B2 ** ADAM_STEP)
    delta = -ADAM_LR * (m_hat / (jnp.sqrt(v_hat) + ADAM_EPS) + ADAM_WD * w)
    return delta, m, v


def _adamw(w, g, m, v, name):
    R, C = w.shape
    tr = R if R * C * 4 <= (1 << 20) else _pick8(R, max(8, (1 << 20) // (C * 4)))

    def body(w_ref, g_ref, m_ref, v_ref, d_ref, nm_ref, nv_ref):
        d, nm, nv = _adam_math(w_ref[...], g_ref[...], m_ref[...], v_ref[...])
        d_ref[...] = d
        nm_ref[...] = nm
        nv_ref[...] = nv

    spec = pl.BlockSpec((tr, C), lambda i: (i, 0))
    shp = jax.ShapeDtypeStruct((R, C), F32)
    return pl.pallas_call(
        body, name=name, grid=(R // tr,), in_specs=[spec] * 4, out_specs=[spec] * 3, out_shape=[shp] * 3,
        compiler_params=_cparams(("parallel",)),
    )(w, g, m, v)


def _pick8(dim, pref):
    t = (min(dim, pref) // 8) * 8
    while t >= 8:
        if dim % t == 0:
            return t
        t -= 8
    return dim


def _adamw_outer(w, m, v, cond_t, rhs, name):
    R, C = w.shape
    tr = _pick8(R, 128)
    nb = cond_t.shape[1]
    lhs_t = cond_t

    def body(w_ref, m_ref, v_ref, a_ref, b_ref, g_ref, d_ref, nm_ref, nv_ref):
        g = _dot(_silu(a_ref[...]), b_ref[...])
        d, nm, nv = _adam_math(w_ref[...], g, m_ref[...], v_ref[...])
        g_ref[...] = g
        d_ref[...] = d
        nm_ref[...] = nm
        nv_ref[...] = nv

    spec = pl.BlockSpec((tr, C), lambda i: (i, 0))
    shp = jax.ShapeDtypeStruct((R, C), F32)
    return pl.pallas_call(
        body, name=name, grid=(R // tr,),
        in_specs=[spec, spec, spec, pl.BlockSpec((tr, nb), lambda i: (i, 0)), pl.BlockSpec((nb, C), lambda i: (0, 0))],
        out_specs=[spec] * 4, out_shape=[shp] * 4,
        compiler_params=_cparams(("parallel",)),
    )(w, m, v, lhs_t, rhs)


def _ada_fwd(cond, w, bias, name):
    a = cond
    nb, K = a.shape
    N = w.shape[1]
    tn = _pick(N, 512)

    def body(a_ref, w_ref, b_ref, o_ref):
        o_ref[...] = _dot(_silu(a_ref[...]), w_ref[...]) + b_ref[...]

    return pl.pallas_call(
        body, name=name, grid=(N // tn,),
        in_specs=[pl.BlockSpec((nb, K), lambda j: (0, 0)), pl.BlockSpec((K, tn), lambda j: (0, j)), pl.BlockSpec((1, tn), lambda j: (0, j))],
        out_specs=pl.BlockSpec((nb, tn), lambda j: (0, j)), out_shape=jax.ShapeDtypeStruct((nb, N), F32),
        compiler_params=_cparams(("parallel",)),
    )(a, w, bias)


def _add_cast(parts, out_dtypes, name):
    shape = parts[0].shape
    G, R, C = shape
    tr = _pick8(R, max(8, (1 << 20) // (C * 4)))
    n_in = len(parts)

    def body(*refs):
        acc = refs[0][...].astype(F32)
        for r in refs[1:n_in]:
            acc = acc + r[...].astype(F32)
        for o, dt in zip(refs[n_in:], out_dtypes):
            o[...] = acc.astype(dt)

    spec = pl.BlockSpec((1, tr, C), lambda g, i: (g, i, 0))
    outs = pl.pallas_call(
        body, name=name, grid=(G, R // tr), in_specs=[spec] * n_in, out_specs=[spec] * len(out_dtypes),
        out_shape=[jax.ShapeDtypeStruct(shape, dt) for dt in out_dtypes],
        compiler_params=_cparams(("parallel", "parallel")),
    )(*parts)
    return outs


def _me():
    return lax.axis_index("x"), lax.axis_index("y"), lax.axis_index("c")


def _xor_peer(k):
    x, y, c = _me()
    dx, dy, dc = (k >> 2) & 1, (k >> 1) & 1, k & 1
    return (x ^ dx if dx else x, y ^ dy if dy else y, c ^ dc if dc else c)


ANY = pl.BlockSpec(memory_space=pl.ANY)


def _all_gather_small(v, name):
    R, C = v.shape

    def body(v_ref, out_ref, send_sems, recv_sems):
        x, y, c = _me()
        mine = 4 * x + 2 * y + c
        out_ref[mine] = v_ref[...]
        copies = []
        for k in range(1, 8):
            cp = pltpu.make_async_remote_copy(src_ref=v_ref, dst_ref=out_ref.at[mine], send_sem=send_sems.at[k - 1],
                                              recv_sem=recv_sems.at[k - 1], device_id=_xor_peer(k), device_id_type=MESH)
            cp.start()
            copies.append(cp)
        for k in range(1, 8):
            px, py, pc = _xor_peer(k)
            pltpu.make_async_remote_copy(src_ref=v_ref, dst_ref=out_ref.at[4 * px + 2 * py + pc], send_sem=send_sems.at[k - 1],
                                         recv_sem=recv_sems.at[k - 1], device_id=_xor_peer(k), device_id_type=MESH).wait_recv()
        for cp in copies:
            cp.wait_send()

    return pl.pallas_call(
        body, name=name, out_shape=jax.ShapeDtypeStruct((8, R, C), F32),
        in_specs=[pl.BlockSpec(memory_space=pltpu.VMEM)], out_specs=pl.BlockSpec(memory_space=pltpu.VMEM),
        scratch_shapes=[pltpu.SemaphoreType.DMA((7,)), pltpu.SemaphoreType.DMA((7,))],
        compiler_params=pltpu.CompilerParams(vmem_limit_bytes=VMEM_LIMIT),
    )(v)


def _chip_peers():
    x, y, _ = _me()
    return [(1, (x, 1 - y)), (2, (1 - x, y)), (3, (1 - x, 1 - y))]


def _all_gather_shards(shards, name):
    n = len(shards)

    def body(*refs):
        ins, outs = refs[:n], refs[n:2 * n]
        send_sems, recv_sems = refs[2 * n:]
        x, y, c = _me()
        chip = 2 * x + y
        sib = (x, y, 1 - c)
        peers = _chip_peers()
        sends = []
        for t in range(n):
            half = ins[t].shape[0] // 2
            mine = pl.ds(c * half, half)
            for p, (k, (px, py)) in enumerate(peers):
                cp = pltpu.make_async_remote_copy(src_ref=ins[t].at[mine], dst_ref=outs[t].at[chip, mine],
                                                  send_sem=send_sems.at[6 * t + p], recv_sem=recv_sems.at[6 * t + p],
                                                  device_id=(px, py, c), device_id_type=MESH)
                cp.start()
                sends.append(cp)
        for t in range(n):
            half = ins[t].shape[0] // 2
            mine = pl.ds(c * half, half)
            for p, (k, (px, py)) in enumerate(peers):
                src_chip = 2 * px + py
                landed = outs[t].at[src_chip, mine]
                pltpu.make_async_remote_copy(src_ref=landed, dst_ref=landed, send_sem=send_sems.at[6 * t + p],
                                             recv_sem=recv_sems.at[6 * t + p], device_id=(px, py, c), device_id_type=MESH).wait_recv()
                fw = pltpu.make_async_remote_copy(src_ref=landed, dst_ref=landed, send_sem=send_sems.at[6 * t + 3 + p],
                                                  recv_sem=recv_sems.at[6 * t + 3 + p], device_id=sib, device_id_type=MESH)
                fw.start()
                sends.append(fw)
        for t in range(n):
            half = ins[t].shape[0] // 2
            theirs = pl.ds((1 - c) * half, half)
            for p, (k, (px, py)) in enumerate(peers):
                got = outs[t].at[2 * px + py, theirs]
                pltpu.make_async_remote_copy(src_ref=got, dst_ref=got, send_sem=send_sems.at[6 * t + 3 + p],
                                             recv_sem=recv_sems.at[6 * t + 3 + p], device_id=sib, device_id_type=MESH).wait_recv()
        for cp in sends:
            cp.wait_send()

    return pl.pallas_call(
        body, name=name,
        out_shape=[jax.ShapeDtypeStruct((4,) + s.shape, s.dtype) for s in shards],
        in_specs=[ANY] * n, out_specs=[ANY] * n,
        scratch_shapes=[pltpu.SemaphoreType.DMA((6 * n,)), pltpu.SemaphoreType.DMA((6 * n,))],
    )(*shards)


def _swap_halves_with_sibling(slabs, name):
    n = len(slabs)

    def body(*refs):
        ins, outs = refs[:n], refs[n:2 * n]
        send_sems, recv_sems = refs[2 * n:]
        x, y, c = _me()
        sib = (x, y, 1 - c)
        cps = []
        for t in range(n):
            half = ins[t].shape[1] // 2
            cp = pltpu.make_async_remote_copy(src_ref=ins[t].at[:, pl.ds((1 - c) * half, half)], dst_ref=outs[t],
                                              send_sem=send_sems.at[t], recv_sem=recv_sems.at[t], device_id=sib, device_id_type=MESH)
            cp.start()
            cps.append(cp)
        for cp in cps:
            cp.wait()

    return pl.pallas_call(
        body, name=name,
        out_shape=[jax.ShapeDtypeStruct((4, s.shape[1] // 2, s.shape[2]), s.dtype) for s in slabs],
        in_specs=[ANY] * n, out_specs=[ANY] * n,
        scratch_shapes=[pltpu.SemaphoreType.DMA((n,)), pltpu.SemaphoreType.DMA((n,))],
    )(*slabs)


def _exchange_between_chips(slabs, name):
    n = len(slabs)

    def body(*refs):
        ins, outs = refs[:n], refs[n:2 * n]
        send_sems, recv_sems = refs[2 * n:]
        x, y, c = _me()
        chip = 2 * x + y
        cps = []
        for t in range(n):
            for p, (k, (px, py)) in enumerate(_chip_peers()):
                cp = pltpu.make_async_remote_copy(src_ref=ins[t].at[2 * px + py], dst_ref=outs[t].at[chip],
                                                  send_sem=send_sems.at[3 * t + p], recv_sem=recv_sems.at[3 * t + p],
                                                  device_id=(px, py, c), device_id_type=MESH)
                cp.start()
                cps.append(cp)
        for t in range(n):
            for p, (k, (px, py)) in enumerate(_chip_peers()):
                got = outs[t].at[2 * px + py]
                pltpu.make_async_remote_copy(src_ref=got, dst_ref=got, send_sem=send_sems.at[3 * t + p],
                                             recv_sem=recv_sems.at[3 * t + p], device_id=(px, py, c), device_id_type=MESH).wait_recv()
        for cp in cps:
            cp.wait_send()

    return pl.pallas_call(
        body, name=name,
        out_shape=[jax.ShapeDtypeStruct(s.shape, s.dtype) for s in slabs],
        in_specs=[ANY] * n, out_specs=[ANY] * n,
        scratch_shapes=[pltpu.SemaphoreType.DMA((3 * n,)), pltpu.SemaphoreType.DMA((3 * n,))],
    )(*slabs)


def _send_halves_to_sibling(halves, name):
    n = len(halves)

    def body(*refs):
        ins, outs = refs[:n], refs[n:2 * n]
        send_sems, recv_sems = refs[2 * n:]
        x, y, c = _me()
        sib = (x, y, 1 - c)
        cps = []
        for t in range(n):
            cp = pltpu.make_async_remote_copy(src_ref=ins[t], dst_ref=outs[t], send_sem=send_sems.at[t],
                                              recv_sem=recv_sems.at[t], device_id=sib, device_id_type=MESH)
            cp.start()
            cps.append(cp)
        for cp in cps:
            cp.wait()

    return pl.pallas_call(
        body, name=name,
        out_shape=[jax.ShapeDtypeStruct(s.shape, s.dtype) for s in halves],
        in_specs=[ANY] * n, out_specs=[ANY] * n,
        scratch_shapes=[pltpu.SemaphoreType.DMA((n,)), pltpu.SemaphoreType.DMA((n,))],
    )(*halves)


HBM_SPEC = pl.BlockSpec(memory_space=pltpu.HBM)
SEM_SPEC = pl.BlockSpec(memory_space=pltpu.SEMAPHORE)
DATAFLOW = pltpu.SideEffectType.DATAFLOW_SIDE_EFFECTING


def _plan_gather_direct(src_refs, land_refs):
    x, y, c = _me()
    chip = 2 * x + y
    plan = []
    for s, land in zip(src_refs, land_refs):
        half = s.shape[0] // 2
        for _, (px, py) in _chip_peers():
            for pc in (c, 1 - c):
                plan.append((s.at[pl.ds(c * half, half)], land.at[chip, pl.ds(c * half, half)],
                             land.at[2 * px + py, pl.ds(pc * half, half)], (px, py, pc)))
    return plan


def _plan_scatter_direct(src_refs, land_refs):
    x, y, c = _me()
    plan = []
    for s, land in zip(src_refs, land_refs):
        half = s.shape[1] // 2
        for k in range(1, 8):
            px, py, pc = _xor_peer(k)
            plan.append((s.at[2 * px + py, pl.ds(pc * half, half)], land.at[4 * x + 2 * y + c],
                         land.at[4 * px + 2 * py + pc], (px, py, pc)))
    return plan


def _start_copies(srcs, lands, plan_fn, name, after=None):
    n = len(srcs)
    n_copies = len(srcs) * (7 if plan_fn is _plan_scatter_direct else 6)
    extra = [] if after is None else [after]

    def body(*refs):
        refs = refs[:2 * n] + refs[2 * n + len(extra):]
        send_sems, recv_sems, token = refs[2 * n], refs[2 * n + 1], refs[-1]
        for i, (src, dst, _, peer) in enumerate(plan_fn(refs[:n], refs[n:2 * n])):
            pltpu.make_async_remote_copy(src_ref=src, dst_ref=dst, send_sem=send_sems.at[i], recv_sem=recv_sems.at[i],
                                         device_id=peer, device_id_type=MESH).start()
        token[...] = jnp.zeros_like(token)

    arrays = list(srcs) + list(lands)
    outs = pl.pallas_call(
        body, name=name,
        out_shape=(pltpu.SemaphoreType.DMA((n_copies,)), pltpu.SemaphoreType.DMA((n_copies,)),
                   *[pltpu.HBM(a.shape, a.dtype) for a in arrays], jax.ShapeDtypeStruct((8, LANES), F32)),
        in_specs=[HBM_SPEC] * (2 * n) + [ANY] * len(extra),
        out_specs=(SEM_SPEC, SEM_SPEC, *[HBM_SPEC] * (2 * n), pl.BlockSpec(memory_space=pltpu.VMEM)),
        input_output_aliases={i: 2 + i for i in range(2 * n)},
        compiler_params=pltpu.CompilerParams(has_side_effects=DATAFLOW),
    )(*[pltpu.with_memory_space_constraint(a, pltpu.HBM) for a in arrays], *extra)
    return outs[0], outs[1], list(outs[2:2 + n]), list(outs[2 + n:2 + 2 * n]), outs[-1]


def _wait_copies(send_sems, recv_sems, srcs, lands, after, plan_fn, name):
    n = len(srcs)

    def body(*refs):
        send_sems, recv_sems = refs[2 * n], refs[2 * n + 1]
        for i, (src, _, arrival, peer) in enumerate(plan_fn(refs[:n], refs[n:2 * n])):
            cp = pltpu.make_async_remote_copy(src_ref=src, dst_ref=arrival, send_sem=send_sems.at[i], recv_sem=recv_sems.at[i],
                                              device_id=peer, device_id_type=MESH)
            cp.wait_send()
            cp.wait_recv()

    arrays = list(srcs) + list(lands)
    outs = pl.pallas_call(
        body, name=name,
        out_shape=tuple(pltpu.HBM(a.shape, a.dtype) for a in arrays),
        in_specs=[HBM_SPEC] * (2 * n) + [SEM_SPEC, SEM_SPEC, ANY],
        out_specs=tuple([HBM_SPEC] * (2 * n)),
        input_output_aliases={i: i for i in range(2 * n)},
        compiler_params=pltpu.CompilerParams(has_side_effects=DATAFLOW),
    )(*arrays, send_sems, recv_sems, after)
    return list(outs[n:])


def _rope_tables(positions):
    half = ROPE_DIM // 2
    S = positions.shape[0]
    inv_freq = ROPE_THETA ** (-jnp.arange(half, dtype=F32) / half)
    ang = positions.astype(F32)[:, None] * inv_freq
    cos, sin = jnp.cos(ang), jnp.sin(ang)
    zeros = functools.partial(jnp.zeros, dtype=F32)
    cc = jnp.concatenate([cos, cos, jnp.ones((S, HEAD_DIM - ROPE_DIM), F32)], axis=1)
    s1 = jnp.concatenate([-sin, zeros((S, HEAD_DIM - half))], axis=1)
    s2 = jnp.concatenate([zeros((S, half)), sin, zeros((S, HEAD_DIM - ROPE_DIM))], axis=1)
    return cc, s1, s2


def _ffn_fwd(x, gain, shift, scale, gate, w_gu, w_d, fs, tag):
    h = _pre_fwd(x, gain, shift, scale, tag + "_pre")
    ab = _matmul(h, w_gu, "nn", F32, tag + "_gate_up", tm=512, tn=1024, tk=4096)
    s = _swiglu_fwd(ab, fs, tag + "_swiglu")
    f = _matmul(s, w_d, "nn", F32, tag + "_down", tm=512, tn=1024, tk=8192)
    xn = _residual_fwd(x, gate, f, 0.5, tag + "_res")
    return xn, (x, h, ab, s, f)


def _ffn_bwd(dxn, saved, gain, shift, scale, gate, w_gu, w_d, fs, tag, on_dw_d=None):
    x, h, ab, s, f = saved
    df, dgate = _residual_bwd(gate, f, dxn, 0.5, tag + "_res_bwd")
    ds = _matmul(df, w_d, "nt", F32, tag + "_down_dx", tm=1024, tn=1408, tk=4096)
    dw_d = _matmul(s, df, "tn", F32, tag + "_down_dw", tm=1408, tn=1024, tk=1024)
    if on_dw_d is not None:
        shift = shift + on_dw_d(dw_d)
    dab = _swiglu_bwd(ab, ds, fs, tag + "_swiglu_bwd")
    dh = _matmul(dab, w_gu, "nt", F32, tag + "_gate_up_dx", tm=1024, tn=1024, tk=2816)
    dw_gu = _matmul(h, dab, "tn", F32, tag + "_gate_up_dw", tm=512, tn=2 * fs, tk=1024, col_slabs=True)
    dx, dgain, dshift, dscale = _pre_bwd(x, gain, shift, scale, dh, dxn, tag + "_pre_bwd")
    return dx, dw_gu, dw_d, dgain, dshift, dscale, dgate


def _flat_pad(parts, rows, cols):
    flat = jnp.concatenate([p.reshape(-1).astype(F32) for p in parts])
    return jnp.pad(flat, (0, rows * cols - flat.shape[0])).reshape(rows, cols)


def _cols_to_slabs(w, n):
    R, NC = w.shape
    return jnp.transpose(w.reshape(R, n, NC // n), (1, 0, 2))


def kernel(x, c, positions, w_ada, b_ada, ffn1_norm, ffn1_w_gate, ffn1_w_up, ffn1_w_down, mix_norm, w_in, conv_w, q_norm, k_norm, a_log, dt_bias, delta_out_norm, w_out, ffn2_norm, ffn2_w_gate, ffn2_w_up, ffn2_w_down, loss_target, m_w_ada, m_b_ada, m_ffn1_norm, m_ffn1_w_gate, m_ffn1_w_up, m_ffn1_w_down, m_mix_norm, m_w_in, m_conv_w, m_q_norm, m_k_norm, m_a_log, m_dt_bias, m_delta_out_norm, m_w_out, m_ffn2_norm, m_ffn2_w_gate, m_ffn2_w_up, m_ffn2_w_down, v_w_ada, v_b_ada, v_ffn1_norm, v_ffn1_w_gate, v_ffn1_w_up, v_ffn1_w_down, v_mix_norm, v_w_in, v_conv_w, v_q_norm, v_k_norm, v_a_log, v_dt_bias, v_delta_out_norm, v_w_out, v_ffn2_norm, v_ffn2_w_gate, v_ffn2_w_up, v_ffn2_w_down):
    xi, yi, ci = _me()
    chip = 2 * xi + yi
    dev = 2 * chip + ci
    xs = x[0]
    S, D = xs.shape
    HA, HD = N_ATTN_HEADS, N_DELTA_HEADS
    fs = ffn1_w_gate.shape[2]
    n_mod_shard = w_ada.shape[2]
    in_shard = w_in.shape[2]
    in_width = 4 * in_shard
    in_pad = -(-in_width // LANES) * LANES
    conv_shard = conv_w.shape[2]
    conv_width = 4 * conv_shard

    pack0 = jnp.zeros((8, max(D, conv_shard)), F32)
    pack0 = pack0.at[0, :D].set(c[0]).at[1:1 + CONV_WIDTH, :conv_shard].set(conv_w[0])
    got0 = _all_gather_small(pack0, "gather_cond")
    c_all = got0[:, 0, :D]
    conv_full = jnp.transpose(got0[::2, 1:1 + CONV_WIDTH, :conv_shard], (1, 0, 2)).reshape(CONV_WIDTH, conv_width)
    b_ada_mine = lax.dynamic_slice(b_ada, (0, chip * n_mod_shard), (1, n_mod_shard))
    mod_part = _ada_fwd(c_all, w_ada[0], b_ada_mine, "ada_fwd")
    got1 = _all_gather_small(mod_part, "gather_mod")
    mod = lax.dynamic_index_in_dim(got1[::2], dev, axis=1, keepdims=False).reshape(1, 4 * n_mod_shard)
    sh1, sc1, gt1, sh2, sc2, gt2, sh3, sc3, gt3 = [mod[:, i * D:(i + 1) * D] for i in range(N_MOD)]

    shards = [w[0].astype(BF16) for w in (ffn1_w_gate, ffn1_w_up, ffn1_w_down, w_in, w_out, ffn2_w_gate, ffn2_w_up, ffn2_w_down)]
    gathered = _all_gather_shards(shards[:3], "gather_weights")
    g1g, g1u, g1d = [lax.dynamic_update_index_in_dim(g, s, chip, 0) for g, s in zip(gathered, shards[:3])]
    zones = [lax.dynamic_update_index_in_dim(lax.empty((4,) + s.shape, BF16), s, chip, 0) for s in shards[3:]]
    ag_in = _start_copies(shards[3:4], zones[:1], _plan_gather_direct, "gather_in_start")
    sh1 = sh1 + ag_in[4][0, 0]

    def gate_up(gg, gu):
        return jnp.transpose(jnp.concatenate([gg, gu], axis=2), (1, 0, 2)).reshape(D, 8 * fs)

    w_gu1, w_d1 = gate_up(g1g, g1u), g1d.reshape(4 * fs, D)

    x1, saved1 = _ffn_fwd(xs, ffn1_norm, sh1, sc1, gt1, w_gu1, w_d1, fs, "ffn1")
    (gin,) = _wait_copies(ag_in[0], ag_in[1], ag_in[2], ag_in[3], x1, _plan_gather_direct, "gather_in_wait")
    w_in_f = jnp.pad(jnp.transpose(gin, (1, 0, 2)).reshape(D, in_width), ((0, 0), (0, in_pad - in_width)))
    ag_rest = _start_copies(shards[4:], zones[1:], _plan_gather_direct, "gather_rest_start", after=gin)
    sh2 = sh2 + ag_rest[4][0, 0]

    cc, s1, s2 = _rope_tables(positions[0])
    alog_row = jnp.pad(a_log, ((0, 0), (0, LANES - HD)))
    dtb_row = jnp.pad(dt_bias, ((0, 0), (0, LANES - HD)))
    col_k, col_v, col_d, col_z, col_ab = HA, 2 * HA, 3 * HA, 3 * HA + 3 * HD, 3 * HA + 4 * HD
    h2 = _pre_fwd(x1, mix_norm, sh2, sc2, "mix_pre")
    proj = _matmul(h2, w_in_f, "nn", F32, "mix_in_proj", tm=512, tn=2432, tk=4096)
    attn_bias = _log_multiplicity_table(_pick(S, ATTN_TILE))
    qa, ka = _attn_prep_fwd(proj, q_norm, k_norm, cc, s1, s2, "attn_prep")
    oa, oa_b, lse = _attn_fwd(qa, ka, proj, col_v, attn_bias, "attn_fwd")
    dqkv = _conv_fwd(proj, col_d, conv_width, conv_full, "conv_fwd")
    *prep, t_inv = _delta_prep_fwd(dqkv, proj, col_ab, alog_row, dtb_row, "delta_prep")
    od_raw, states = _delta_scan_fwd(*prep, "delta_scan")
    od = _post_fwd(od_raw, proj, col_z, delta_out_norm, "delta_post")
    o = jnp.concatenate([oa_b, od], axis=1)
    gout, g2g, g2u, g2d = _wait_copies(ag_rest[0], ag_rest[1], ag_rest[2], ag_rest[3], o, _plan_gather_direct, "gather_rest_wait")
    w_out_f = gout.reshape(-1, D)
    w_gu2, w_d2 = gate_up(g2g, g2u), g2d.reshape(4 * fs, D)
    mo = _matmul(o, w_out_f, "nn", F32, "mix_out_proj", tm=512, tn=1024, tk=4096)
    x2 = _residual_fwd(x1, gt2, mo, 1.0, "mix_res")

    x3, saved3 = _ffn_fwd(x2, ffn2_norm, sh3, sc3, gt3, w_gu2, w_d2, fs, "ffn2")
    loss_part, dy = _loss_head(x3, loss_target[0], "loss_head")
    loss = lax.psum(loss_part[0, 0], ("x", "y", "c"))

    dx2, dw_gu2, dw_d2, dgain3, dsh3, dsc3, dgt3 = _ffn_bwd(dy, saved3, ffn2_norm, sh3, sc3, gt3, w_gu2, w_d2, fs, "ffn2")

    def scatter_start(slabs32, name):
        slabs16 = [s.astype(BF16) for s in slabs32]
        zones = []
        for s in slabs16:
            half = s.shape[1] // 2
            own = lax.dynamic_slice(s, (chip, ci * half, 0), (1, half, s.shape[2]))
            zones.append(lax.dynamic_update_slice(lax.empty((8, half, s.shape[2]), BF16), own, (dev, 0, 0)))
        return _start_copies(slabs16, zones, _plan_scatter_direct, name)

    rs_ffn2 = scatter_start([dw_gu2, dw_d2.reshape(4, fs, D)], "rs_ffn2_start")
    dmo, dgt2 = _residual_bwd(gt2 + rs_ffn2[4][0, 0], mo, dx2, 1.0, "mix_res_bwd")
    do = _matmul(dmo, w_out_f, "nt", F32, "mix_out_dx", tm=1024, tn=1024, tk=4096)
    dw_out = _matmul(o, dmo, "tn", F32, "mix_out_dw", tm=1024, tn=1024, tk=1024)
    dq = _attn_bwd_dq(qa, ka, proj, col_v, oa, lse, do, 0, attn_bias, "attn_bwd_dq")
    dk, dv = _attn_bwd_dkv(qa, ka, proj, col_v, oa, lse, do, 0, attn_bias, "attn_bwd_dkv")
    dpq, dpk, dq_gain, dk_gain = _attn_prep_bwd(proj, q_norm, k_norm, cc, s1, s2, dq, dk, "attn_prep_bwd")
    dod, dz, ddn_gain = _post_bwd(od_raw, proj, col_z, delta_out_norm, do, HA, "delta_post_bwd")
    cots = _delta_scan_bwd(*prep, states, dod, "delta_scan_bwd")
    ddq, ddk, ddv, dab, dalog, ddtb = _delta_prep_bwd(dqkv, proj, col_ab, alog_row, dtb_row, cots, t_inv, "delta_prep_bwd")
    dconv_in, dconv_w = _conv_bwd(proj, col_d, conv_width, conv_full, jnp.concatenate([ddq, ddk, ddv], axis=1), "conv_bwd")
    dproj = jnp.concatenate([dpq, dpk, dv, dconv_in, dz, dab], axis=1)
    dh2 = _matmul(dproj, w_in_f, "nt", F32, "mix_in_dx", tm=1024, tn=1024, tk=2816)
    dw_in = _matmul(h2, dproj, "tn", F32, "mix_in_dw", tm=512, tn=2432, tk=1024)
    rs_mix = scatter_start([_cols_to_slabs(dw_in[:, :in_width], 4), dw_out.reshape(4, -1, D)], "rs_mix_start")
    dx1, dgain2, dsh2, dsc2 = _pre_bwd(x1, mix_norm, sh2, sc2, dh2, dx2, "mix_pre_bwd")

    rs_ffn1d = []

    def send_dw_d1(dw_d):
        rs_ffn1d.extend(scatter_start([dw_d.reshape(4, fs, D)], "rs_ffn1d_start"))
        return rs_ffn1d[4][0, 0]

    dx0, dw_gu1, dw_d1, dgain1, dsh1, dsc1, dgt1 = _ffn_bwd(dx1, saved1, ffn1_norm, sh1, sc1, gt1 + rs_mix[4][0, 0], w_gu1, w_d1, fs,
                                                            "ffn1", on_dw_d=send_dw_d1)

    n_small = N_MOD * D + 3 * D + 5 * LANES + CONV_WIDTH * conv_width
    cols_small = -(-n_small // (8 * LANES)) * LANES
    small = _flat_pad([dsh1, dsc1, dgt1, dsh2, dsc2, dgt2, dsh3, dsc3, dgt3, dgain1, dgain2, dgain3,
                       dq_gain, dk_gain, dalog, ddtb, ddn_gain, dconv_w], 8, cols_small)
    got2 = _all_gather_small(small, "gather_small_grads")
    small_sum = _add_cast([got2[d:d + 1] for d in range(8)], [F32], "sum_small_grads")[0].reshape(-1)
    dmod_all = got2.reshape(8, -1)[:, :N_MOD * D]
    off = [0]

    def take(n):
        off[0] += n
        return small_sum[off[0] - n:off[0]]

    g_b_ada = take(N_MOD * D).reshape(1, -1)
    g_ffn1_norm, g_mix_norm, g_ffn2_norm = take(D).reshape(1, D), take(D).reshape(1, D), take(D).reshape(1, D)
    g_q_norm, g_k_norm = take(LANES).reshape(1, -1), take(LANES).reshape(1, -1)
    g_a_log, g_dt_bias = take(LANES)[:HD].reshape(1, HD), take(LANES)[:HD].reshape(1, HD)
    g_dn = take(LANES).reshape(1, -1)
    g_conv_full = take(CONV_WIDTH * conv_width).reshape(CONV_WIDTH, conv_width)
    g_conv = lax.dynamic_slice(g_conv_full, (0, chip * conv_shard), (CONV_WIDTH, conv_shard))

    slabs = [dw_gu1]
    from_sibling = _swap_halves_with_sibling(slabs, "rs_sibling_swap")
    chip_sums32, chip_sums16 = [], []
    for t, (slab, other) in enumerate(zip(slabs, from_sibling)):
        half = slab.shape[1] // 2
        mine = lax.dynamic_slice_in_dim(slab, ci * half, half, axis=1)
        p32, p16 = _add_cast([mine, other], [F32, BF16], "rs_chip_sum_%d" % t)
        chip_sums32.append(p32)
        chip_sums16.append(p16)
    from_chips = _exchange_between_chips(chip_sums16, "rs_chip_exchange")
    halves = []
    for t, (p32, got) in enumerate(zip(chip_sums32, from_chips)):
        parts = [lax.dynamic_index_in_dim(p32, chip, axis=0, keepdims=True)]
        parts += [lax.dynamic_index_in_dim(got, (chip + k) % 4, axis=0, keepdims=True) for k in (1, 2, 3)]
        halves.append(_add_cast(parts, [F32], "rs_total_%d" % t)[0][0])
    arrived = _wait_copies(rs_ffn1d[0], rs_ffn1d[1], rs_ffn1d[2], rs_ffn1d[3], dx0, _plan_scatter_direct, "rs_ffn1d_wait")
    arrived += _wait_copies(rs_mix[0], rs_mix[1], rs_mix[2], rs_mix[3], dx0, _plan_scatter_direct, "rs_mix_wait")
    arrived += _wait_copies(rs_ffn2[0], rs_ffn2[1], rs_ffn2[2], rs_ffn2[3], dx0, _plan_scatter_direct, "rs_ffn2_wait")
    for t, zone in enumerate(arrived):
        halves.append(_add_cast([zone[d:d + 1] for d in range(8)], [F32], "rs_total_%d" % (t + 2))[0][0])
    theirs = _send_halves_to_sibling(halves, "rs_sibling_join")
    g_gu1, g_d1, g_in, g_out, g_gu2, g_d2 = [
        jnp.where(ci == 0, jnp.concatenate([mine, other], axis=0), jnp.concatenate([other, mine], axis=0))
        for mine, other in zip(halves, theirs)]

    res = {}

    def upd(name, w, g, m, v):
        d, nm, nv = _adamw(w[0], g, m[0], v[0], "adamw_" + name)
        res[name] = (g[None], d[None], nm[None], nv[None])

    upd("ffn1_w_gate", ffn1_w_gate, g_gu1[:, :fs], m_ffn1_w_gate, v_ffn1_w_gate)
    upd("ffn1_w_up", ffn1_w_up, g_gu1[:, fs:], m_ffn1_w_up, v_ffn1_w_up)
    upd("ffn1_w_down", ffn1_w_down, g_d1, m_ffn1_w_down, v_ffn1_w_down)
    upd("w_in", w_in, g_in, m_w_in, v_w_in)
    upd("w_out", w_out, g_out, m_w_out, v_w_out)
    upd("ffn2_w_gate", ffn2_w_gate, g_gu2[:, :fs], m_ffn2_w_gate, v_ffn2_w_gate)
    upd("ffn2_w_up", ffn2_w_up, g_gu2[:, fs:], m_ffn2_w_up, v_ffn2_w_up)
    upd("ffn2_w_down", ffn2_w_down, g_d2, m_ffn2_w_down, v_ffn2_w_down)
    upd("conv_w", conv_w, g_conv, m_conv_w, v_conv_w)

    dmod_mine = lax.dynamic_slice(dmod_all, (0, chip * n_mod_shard), (8, n_mod_shard))
    g, d, nm, nv = _adamw_outer(w_ada[0], m_w_ada[0], v_w_ada[0], jnp.transpose(c_all), dmod_mine, "adamw_w_ada")
    res["w_ada"] = (g[None], d[None], nm[None], nv[None])

    rep = [("b_ada", b_ada, g_b_ada, m_b_ada, v_b_ada), ("ffn1_norm", ffn1_norm, g_ffn1_norm, m_ffn1_norm, v_ffn1_norm),
           ("mix_norm", mix_norm, g_mix_norm, m_mix_norm, v_mix_norm), ("ffn2_norm", ffn2_norm, g_ffn2_norm, m_ffn2_norm, v_ffn2_norm),
           ("q_norm", q_norm, g_q_norm, m_q_norm, v_q_norm), ("k_norm", k_norm, g_k_norm, m_k_norm, v_k_norm),
           ("a_log", a_log, g_a_log, m_a_log, v_a_log), ("dt_bias", dt_bias, g_dt_bias, m_dt_bias, v_dt_bias),
           ("delta_out_norm", delta_out_norm, g_dn, m_delta_out_norm, v_delta_out_norm)]
    n_rep = sum(-(-r[1].shape[1] // LANES) * LANES for r in rep)
    cols_rep = -(-n_rep // (8 * LANES)) * LANES

    def pack_rep(idx):
        return _flat_pad([jnp.pad(r[idx], ((0, 0), (0, -r[idx].shape[1] % LANES))) for r in rep], 8, cols_rep)

    d_rep, nm_rep, nv_rep = [a.reshape(-1) for a in _adamw(pack_rep(1), pack_rep(2), pack_rep(3), pack_rep(4), "adamw_small")]
    o2 = 0
    for name, w, g, _, _ in rep:
        n = w.shape[1]
        res[name] = (g, d_rep[o2:o2 + n].reshape(1, n), nm_rep[o2:o2 + n].reshape(1, n), nv_rep[o2:o2 + n].reshape(1, n))
        o2 += -(-n // LANES) * LANES

    order = ["w_ada", "b_ada", "ffn1_norm", "ffn1_w_gate", "ffn1_w_up", "ffn1_w_down", "mix_norm", "w_in", "conv_w", "q_norm",
             "k_norm", "a_log", "dt_bias", "delta_out_norm", "w_out", "ffn2_norm", "ffn2_w_gate", "ffn2_w_up", "ffn2_w_down"]
    return (loss, dx0[None], *[res[n][0] for n in order], *[res[n][1] for n in order],
            *[res[n][2] for n in order], *[res[n][3] for n in order])
```

```python
import functools
import math

import jax
import jax.numpy as jnp
from jax import lax
from jax.experimental import pallas as pl
from jax.experimental.pallas import tpu as pltpu

F32 = jnp.float32
BF16 = jnp.bfloat16
MESH = pl.DeviceIdType.MESH

HEAD_DIM = 128
N_ATTN_HEADS = 8
N_DELTA_HEADS = 8
DILATED_PATTERNS = ((128, 1), (512, 4), (2048, 16))
MAX_WINDOW = 2048
ROPE_THETA = 500000.0
ROPE_DIM = HEAD_DIM // 4
CONV_WIDTH = 4
CHUNK = 64
NORM_EPS = 1e-6
N_MOD = 9
ADAM_LR = 0.001
ADAM_B1 = 0.9
ADAM_B2 = 0.999
ADAM_EPS = 1e-08
ADAM_WD = 0.01
ADAM_STEP = 10

LANES = 128
VMEM_LIMIT = 56 * 1024 * 1024
ATTN_TILE = 512
HIGHEST = lax.Precision.HIGHEST


def _cparams(sem=None):
    return pltpu.CompilerParams(dimension_semantics=sem, vmem_limit_bytes=VMEM_LIMIT)


def _pick(dim, pref):
    if dim <= pref:
        return dim
    t = (pref // LANES) * LANES
    while t >= LANES:
        if dim % t == 0:
            return t
        t -= LANES
    return dim


def _sigmoid(x):
    return 1.0 / (1.0 + jnp.exp(-x))


def _silu(x):
    return x * _sigmoid(x)


def _softplus(x):
    return jnp.maximum(x, 0.0) + jnp.log(1.0 + jnp.exp(-jnp.abs(x)))


def _rms(x, gain):
    return x * lax.rsqrt(jnp.mean(x * x, axis=-1, keepdims=True) + NORM_EPS) * gain


def _l2(x):
    return x * lax.rsqrt(jnp.sum(x * x, axis=-1, keepdims=True) + NORM_EPS)


def _modulate(x, gain, shift, scale):
    return _rms(x, gain) * (1.0 + scale) + shift


def _dot(a, b):
    return lax.dot_general(a, b, (((1,), (0,)), ((), ())), precision=HIGHEST, preferred_element_type=F32)


def _bdot(a, b, dims):
    return lax.dot_general(a.astype(BF16), b.astype(BF16), (dims, ((), ())), preferred_element_type=F32)


_NN, _NT, _TN = ((1,), (0,)), ((1,), (1,)), ((0,), (0,))
HIGH = lax.Precision.HIGH


def _dot3(a, b, dims=_NN):
    return lax.dot_general(a, b, (dims, ((), ())), precision=HIGH, preferred_element_type=F32)


@jax.custom_vjp
def _mm_nn(a, b):
    return _bdot(a, b, _NN)


_mm_nn.defvjp(lambda a, b: (_bdot(a, b, _NN), (a, b)),
              lambda res, g: (_bdot(g, res[1], _NT), _bdot(res[0], g, _TN)))


@jax.custom_vjp
def _mm_nt(a, b):
    return _bdot(a, b, _NT)


_mm_nt.defvjp(lambda a, b: (_bdot(a, b, _NT), (a, b)),
              lambda res, g: (_bdot(g, res[1], _NN), _bdot(g, res[0], _TN)))


@jax.custom_vjp
def _tri_inv_saved(a, t_inv):
    return t_inv


_tri_inv_saved.defvjp(lambda a, t_inv: (t_inv, t_inv),
                      lambda t_inv, g: (-_dot3(t_inv, _dot3(g, t_inv, _NT), _TN), jnp.zeros_like(t_inv)))


_MM_DIMS = {"nn": ((1,), (0,)), "nt": ((1,), (1,)), "tn": ((0,), (0,))}


def _matmul(a, b, mode, out_dtype, name, tm=1024, tn=1024, tk=1024, col_slabs=False, epilogue=None):
    if mode == "nn":
        (M, K), (_, N) = a.shape, b.shape
    elif mode == "nt":
        (M, K), (N, _) = a.shape, b.shape
    else:
        (K, M), (_, N) = a.shape, b.shape
    tm, tn, tk = _pick(M, tm), _pick(N, tn), _pick(K, tk)
    nk = K // tk
    dims = _MM_DIMS[mode]
    epi_fn, extra_in, outs = epilogue if epilogue is not None else (None, [], None)
    if outs is None:
        if col_slabs:
            outs = [((N // tn, M, tn), out_dtype, (None, tm, tn), lambda i, j: (j, i, 0))]
        else:
            outs = [((M, N), out_dtype, (tm, tn), lambda i, j: (i, j))]
    n_in, n_out = len(extra_in), len(outs)

    def body(a_ref, b_ref, *rest):
        in_refs, out_refs = rest[:n_in], rest[n_in:n_in + n_out]
        k = pl.program_id(2)
        p = _bdot(a_ref[...], b_ref[...], dims)

        def finish(tile):
            if epi_fn is None:
                out_refs[0][...] = tile.astype(out_dtype)
            else:
                epi_fn(tile, in_refs, out_refs)

        if nk == 1:
            finish(p)
        else:
            acc_ref = rest[-1]

            @pl.when(k == 0)
            def _():
                acc_ref[...] = p

            @pl.when((k > 0) & (k < nk - 1))
            def _():
                acc_ref[...] += p

            @pl.when(k == nk - 1)
            def _():
                finish(acc_ref[...] + p)

    def ij(index_map):
        return lambda i, j, k: index_map(i, j)

    a_spec = pl.BlockSpec((tk, tm), lambda i, j, k: (k, i)) if mode == "tn" else pl.BlockSpec((tm, tk), lambda i, j, k: (i, k))
    b_spec = pl.BlockSpec((tn, tk), lambda i, j, k: (j, k)) if mode == "nt" else pl.BlockSpec((tk, tn), lambda i, j, k: (k, j))
    res = pl.pallas_call(
        body, name=name, grid=(M // tm, N // tn, nk),
        in_specs=[a_spec, b_spec] + [pl.BlockSpec(blk, ij(im)) for _, blk, im in extra_in],
        out_specs=[pl.BlockSpec(blk, ij(im)) for _, _, blk, im in outs],
        out_shape=[jax.ShapeDtypeStruct(shp, dt) for shp, dt, _, _ in outs],
        scratch_shapes=[pltpu.VMEM((tm, tn), F32)] if nk > 1 else [],
        compiler_params=_cparams(("parallel", "parallel", "arbitrary")),
    )(a, b, *[arr for arr, _, _ in extra_in])
    return res if epilogue is not None else res[0]


def _row_spec(tr, d):
    return pl.BlockSpec((tr, d), lambda i: (i, 0))


def _vec_spec(d):
    return pl.BlockSpec((1, d), lambda i: (0, 0))


def _pre_fwd(x, gain, shift, scale, name):
    S, D = x.shape
    tr = _pick(S, 256)

    def body(x_ref, g_ref, sh_ref, sc_ref, h_ref):
        h_ref[...] = _modulate(x_ref[...], g_ref[...], sh_ref[...], sc_ref[...]).astype(BF16)

    return pl.pallas_call(
        body, name=name, grid=(S // tr,),
        in_specs=[_row_spec(tr, D), _vec_spec(D), _vec_spec(D), _vec_spec(D)],
        out_specs=_row_spec(tr, D), out_shape=jax.ShapeDtypeStruct((S, D), BF16),
        compiler_params=_cparams(("parallel",)),
    )(x, gain, shift, scale)


def _pre_bwd(x, gain, shift, scale, dh, dx_in, name):
    S, D = x.shape
    tr = _pick(S, 256)

    def body(x_ref, g_ref, sh_ref, sc_ref, dh_ref, dxin_ref, dx_ref, dg_ref, dsh_ref, dsc_ref):
        _, vjp = jax.vjp(_modulate, x_ref[...], g_ref[...], sh_ref[...], sc_ref[...])
        dx, dg, dsh, dsc = vjp(dh_ref[...])
        dx_ref[...] = dxin_ref[...] + dx

        @pl.when(pl.program_id(0) == 0)
        def _():
            dg_ref[...] = jnp.zeros_like(dg_ref)
            dsh_ref[...] = jnp.zeros_like(dsh_ref)
            dsc_ref[...] = jnp.zeros_like(dsc_ref)

        dg_ref[...] += dg
        dsh_ref[...] += dsh
        dsc_ref[...] += dsc

    vec = jax.ShapeDtypeStruct((1, D), F32)
    return pl.pallas_call(
        body, name=name, grid=(S // tr,),
        in_specs=[_row_spec(tr, D), _vec_spec(D), _vec_spec(D), _vec_spec(D), _row_spec(tr, D), _row_spec(tr, D)],
        out_specs=[_row_spec(tr, D), _vec_spec(D), _vec_spec(D), _vec_spec(D)],
        out_shape=[jax.ShapeDtypeStruct((S, D), F32), vec, vec, vec],
        compiler_params=_cparams(("arbitrary",)),
    )(x, gain, shift, scale, dh, dx_in)


def _residual_fwd(x, gate, f, coef, name):
    S, D = x.shape
    tr = _pick(S, 256)

    def body(x_ref, g_ref, f_ref, o_ref):
        o_ref[...] = x_ref[...] + coef * g_ref[...] * f_ref[...]

    return pl.pallas_call(
        body, name=name, grid=(S // tr,),
        in_specs=[_row_spec(tr, D), _vec_spec(D), _row_spec(tr, D)],
        out_specs=_row_spec(tr, D), out_shape=jax.ShapeDtypeStruct((S, D), F32),
        compiler_params=_cparams(("parallel",)),
    )(x, gate, f)


def _residual_bwd(gate, f, dxn, coef, name):
    S, D = f.shape
    tr = _pick(S, 256)

    def body(g_ref, f_ref, d_ref, df_ref, dg_ref):
        d = d_ref[...]
        df_ref[...] = (coef * g_ref[...] * d).astype(BF16)

        @pl.when(pl.program_id(0) == 0)
        def _():
            dg_ref[...] = jnp.zeros_like(dg_ref)

        dg_ref[...] += jnp.sum(coef * f_ref[...] * d, axis=0, keepdims=True)

    return pl.pallas_call(
        body, name=name, grid=(S // tr,),
        in_specs=[_vec_spec(D), _row_spec(tr, D), _row_spec(tr, D)],
        out_specs=[_row_spec(tr, D), _vec_spec(D)],
        out_shape=[jax.ShapeDtypeStruct((S, D), BF16), jax.ShapeDtypeStruct((1, D), F32)],
        compiler_params=_cparams(("arbitrary",)),
    )(gate, f, dxn)


def _swiglu_fn(a, b):
    return _silu(a) * b


def _loss_head(y, target, name):
    S, D = y.shape
    tr = _pick(S, 256)

    def body(y_ref, t_ref, l_ref, dy_ref):
        e = y_ref[...] - t_ref[...]
        dy_ref[...] = e * (1.0 / D)

        @pl.when(pl.program_id(0) == 0)
        def _():
            l_ref[...] = jnp.zeros_like(l_ref)

        l_ref[...] += jnp.sum(jnp.sum(e * e, axis=-1, keepdims=True), axis=0, keepdims=True) * (0.5 / D)

    return pl.pallas_call(
        body, name=name, grid=(S // tr,),
        in_specs=[_row_spec(tr, D), _row_spec(tr, D)],
        out_specs=[pl.BlockSpec((1, 1), lambda i: (0, 0)), _row_spec(tr, D)],
        out_shape=[jax.ShapeDtypeStruct((1, 1), F32), jax.ShapeDtypeStruct((S, D), F32)],
        compiler_params=_cparams(("arbitrary",)),
    )(y, target)


def _rope(y, cc, s1, s2):
    return y * cc + pltpu.roll(y, LANES - ROPE_DIM // 2, 1) * s1 + pltpu.roll(y, ROPE_DIM // 2, 1) * s2


def _rope_t(d, cc, s1, s2):
    return d * cc + pltpu.roll(d * s1, ROPE_DIM // 2, 1) + pltpu.roll(d * s2, LANES - ROPE_DIM // 2, 1)


def _head_spec(tr, col0):
    return pl.BlockSpec((tr, HEAD_DIM), lambda i, h: (i, col0 + h))


def _tab_spec(tr):
    return pl.BlockSpec((tr, HEAD_DIM), lambda i, h: (i, 0))


def _gain_spec():
    return pl.BlockSpec((1, HEAD_DIM), lambda i, h: (0, 0))


def _attn_prep_fwd(proj, q_gain, k_gain, cc, s1, s2, name):
    S = proj.shape[0]
    H = N_ATTN_HEADS
    tr = _pick(S, 512)

    def body(q_ref, k_ref, qg_ref, kg_ref, cc_ref, s1_ref, s2_ref, qo_ref, ko_ref):
        cc, s1, s2 = cc_ref[...], s1_ref[...], s2_ref[...]
        qo_ref[...] = _rope(_rms(q_ref[...], qg_ref[...]), cc, s1, s2).astype(BF16)
        ko_ref[...] = _rope(_rms(k_ref[...], kg_ref[...]), cc, s1, s2).astype(BF16)

    out = jax.ShapeDtypeStruct((S, H * HEAD_DIM), BF16)
    return pl.pallas_call(
        body, name=name, grid=(S // tr, H),
        in_specs=[_head_spec(tr, 0), _head_spec(tr, H), _gain_spec(), _gain_spec(), _tab_spec(tr), _tab_spec(tr), _tab_spec(tr)],
        out_specs=[_head_spec(tr, 0), _head_spec(tr, 0)], out_shape=[out, out],
        compiler_params=_cparams(("parallel", "parallel")),
    )(proj, proj, q_gain, k_gain, cc, s1, s2)


def _attn_prep_bwd(proj, q_gain, k_gain, cc, s1, s2, dq, dk, name):
    S = proj.shape[0]
    H = N_ATTN_HEADS
    tr = _pick(S, 512)

    def body(q_ref, k_ref, qg_ref, kg_ref, cc_ref, s1_ref, s2_ref, dq_ref, dk_ref, dpq_ref, dpk_ref, dqg_ref, dkg_ref):
        cc, s1, s2 = cc_ref[...], s1_ref[...], s2_ref[...]

        @pl.when((pl.program_id(0) == 0) & (pl.program_id(1) == 0))
        def _():
            dqg_ref[...] = jnp.zeros_like(dqg_ref)
            dkg_ref[...] = jnp.zeros_like(dkg_ref)

        _, vjp_q = jax.vjp(_rms, q_ref[...], qg_ref[...])
        dxq, dgq = vjp_q(_rope_t(dq_ref[...], cc, s1, s2))
        _, vjp_k = jax.vjp(_rms, k_ref[...], kg_ref[...])
        dxk, dgk = vjp_k(_rope_t(dk_ref[...], cc, s1, s2))
        dpq_ref[...] = dxq.astype(BF16)
        dpk_ref[...] = dxk.astype(BF16)
        dqg_ref[...] += dgq
        dkg_ref[...] += dgk

    out = jax.ShapeDtypeStruct((S, H * HEAD_DIM), BF16)
    gout = jax.ShapeDtypeStruct((1, HEAD_DIM), F32)
    return pl.pallas_call(
        body, name=name, grid=(S // tr, H),
        in_specs=[_head_spec(tr, 0), _head_spec(tr, H), _gain_spec(), _gain_spec(), _tab_spec(tr), _tab_spec(tr), _tab_spec(tr),
                  _head_spec(tr, 0), _head_spec(tr, 0)],
        out_specs=[_head_spec(tr, 0), _head_spec(tr, 0), _gain_spec(), _gain_spec()], out_shape=[out, out, gout, gout],
        compiler_params=_cparams(("arbitrary", "arbitrary")),
    )(proj, proj, q_gain, k_gain, cc, s1, s2, dq, dk)


def _multiplicity(j, t):
    ti = lax.broadcasted_iota(jnp.int32, (t, t), 0)
    si = lax.broadcasted_iota(jnp.int32, (t, t), 1)
    delta = j * t + ti - si
    cnt = jnp.zeros((t, t), F32)
    for window, dil in DILATED_PATTERNS:
        ok = (delta >= 0) & ((delta & (dil - 1)) == 0) & (delta <= window)
        cnt = cnt + ok.astype(F32)
    return cnt


_NEG = -1e30


def _log_multiplicity_table(t):
    cnt = jnp.stack([_multiplicity(j, t) for j in range(MAX_WINDOW // t + 1)])
    return jnp.where(cnt > 0.0, jnp.log(jnp.maximum(cnt, 1.0)), _NEG)


def _bias_spec(t):
    return pl.BlockSpec((MAX_WINDOW // t + 1, t, t), lambda h, i, j: (0, 0, 0))


def _attn_fwd(q, k, proj, v_col0, bias, name):
    S = q.shape[0]
    H = N_ATTN_HEADS
    t = _pick(S, ATTN_TILE)
    nq = S // t
    nj = MAX_WINDOW // t + 1
    scale = HEAD_DIM ** -0.5

    def body(q_ref, k_ref, v_ref, b_ref, o_ref, ob_ref, lse_ref, m_sc, l_sc, acc_sc):
        qb, j = pl.program_id(1), pl.program_id(2)

        @pl.when(j == 0)
        def _():
            m_sc[...] = jnp.full_like(m_sc, _NEG)
            l_sc[...] = jnp.zeros_like(l_sc)
            acc_sc[...] = jnp.zeros_like(acc_sc)

        @pl.when(qb - j >= 0)
        def _():
            s = _bdot(q_ref[...], k_ref[...], ((1,), (1,))) * scale + b_ref[j]
            m_prev = m_sc[...]
            m_new = jnp.maximum(m_prev, jnp.max(s, axis=-1, keepdims=True))
            alpha = jnp.exp(m_prev - m_new)
            p = jnp.exp(s - m_new)
            l_sc[...] = alpha * l_sc[...] + jnp.sum(p, axis=-1, keepdims=True)
            acc_sc[...] = alpha * acc_sc[...] + _bdot(p, v_ref[...], ((1,), (0,)))
            m_sc[...] = m_new

        @pl.when(j == nj - 1)
        def _():
            o = acc_sc[...] / l_sc[...]
            o_ref[...] = o
            ob_ref[...] = o.astype(BF16)
            lse_ref[...] = jnp.broadcast_to(m_sc[...] + jnp.log(l_sc[...]), (t, HEAD_DIM))

    qspec = pl.BlockSpec((t, HEAD_DIM), lambda h, i, j: (i, h))
    kspec = pl.BlockSpec((t, HEAD_DIM), lambda h, i, j: (jnp.maximum(i - j, 0), h))
    vspec = pl.BlockSpec((t, HEAD_DIM), lambda h, i, j: (jnp.maximum(i - j, 0), v_col0 + h))
    return pl.pallas_call(
        body, name=name, grid=(H, nq, nj),
        in_specs=[qspec, kspec, vspec, _bias_spec(t)], out_specs=[qspec, qspec, qspec],
        out_shape=[jax.ShapeDtypeStruct((S, H * HEAD_DIM), F32), jax.ShapeDtypeStruct((S, H * HEAD_DIM), BF16),
                   jax.ShapeDtypeStruct((S, H * HEAD_DIM), F32)],
        scratch_shapes=[pltpu.VMEM((t, 1), F32), pltpu.VMEM((t, 1), F32), pltpu.VMEM((t, HEAD_DIM), F32)],
        compiler_params=_cparams(("parallel", "parallel", "arbitrary")),
    )(q, k, proj, bias)


def _attn_probs(q, k, lse, bias_tile, scale):
    return jnp.exp(_bdot(q, k, ((1,), (1,))) * scale + bias_tile - lse)


def _attn_bwd_dq(q, k, proj, v_col0, o, lse, do, do_col0, bias, name):
    S = q.shape[0]
    H = N_ATTN_HEADS
    t = _pick(S, ATTN_TILE)
    nq = S // t
    nj = MAX_WINDOW // t + 1
    scale = HEAD_DIM ** -0.5

    def body(q_ref, k_ref, v_ref, o_ref, lse_ref, do_ref, b_ref, dq_ref, acc_sc):
        qb, j = pl.program_id(1), pl.program_id(2)

        @pl.when(j == 0)
        def _():
            acc_sc[...] = jnp.zeros_like(acc_sc)

        @pl.when(qb - j >= 0)
        def _():
            do = do_ref[...]
            dsum = jnp.sum(do * o_ref[...], axis=-1, keepdims=True)
            lse = jnp.max(lse_ref[...], axis=-1, keepdims=True)
            p = _attn_probs(q_ref[...], k_ref[...], lse, b_ref[j], scale)
            dp = _bdot(do, v_ref[...], ((1,), (1,)))
            ds = p * (dp - dsum)
            acc_sc[...] += _bdot(ds, k_ref[...], ((1,), (0,))) * scale

        @pl.when(j == nj - 1)
        def _():
            dq_ref[...] = acc_sc[...]

    qspec = pl.BlockSpec((t, HEAD_DIM), lambda h, i, j: (i, h))
    dospec = pl.BlockSpec((t, HEAD_DIM), lambda h, i, j: (i, do_col0 + h))
    kspec = pl.BlockSpec((t, HEAD_DIM), lambda h, i, j: (jnp.maximum(i - j, 0), h))
    vspec = pl.BlockSpec((t, HEAD_DIM), lambda h, i, j: (jnp.maximum(i - j, 0), v_col0 + h))
    return pl.pallas_call(
        body, name=name, grid=(H, nq, nj),
        in_specs=[qspec, kspec, vspec, qspec, qspec, dospec, _bias_spec(t)], out_specs=qspec,
        out_shape=jax.ShapeDtypeStruct((S, H * HEAD_DIM), F32),
        scratch_shapes=[pltpu.VMEM((t, HEAD_DIM), F32)],
        compiler_params=_cparams(("parallel", "parallel", "arbitrary")),
    )(q, k, proj, o, lse, do, bias)


def _attn_bwd_dkv(q, k, proj, v_col0, o, lse, do, do_col0, bias, name):
    S = q.shape[0]
    H = N_ATTN_HEADS
    t = _pick(S, ATTN_TILE)
    nq = S // t
    nj = MAX_WINDOW // t + 1
    scale = HEAD_DIM ** -0.5

    def body(q_ref, k_ref, v_ref, o_ref, lse_ref, do_ref, b_ref, dk_ref, dv_ref, dk_sc, dv_sc):
        kb, j = pl.program_id(1), pl.program_id(2)

        @pl.when(j == 0)
        def _():
            dk_sc[...] = jnp.zeros_like(dk_sc)
            dv_sc[...] = jnp.zeros_like(dv_sc)

        @pl.when(kb + j < nq)
        def _():
            do = do_ref[...]
            dsum = jnp.sum(do * o_ref[...], axis=-1, keepdims=True)
            lse = jnp.max(lse_ref[...], axis=-1, keepdims=True)
            p = _attn_probs(q_ref[...], k_ref[...], lse, b_ref[j], scale)
            dp = _bdot(do, v_ref[...], ((1,), (1,)))
            ds = p * (dp - dsum)
            dv_sc[...] += _bdot(p, do, ((0,), (0,)))
            dk_sc[...] += _bdot(ds, q_ref[...], ((0,), (0,))) * scale

        @pl.when(j == nj - 1)
        def _():
            dk_ref[...] = dk_sc[...]
            dv_ref[...] = dv_sc[...].astype(BF16)

    def qrow(h, i, j):
        return jnp.minimum(i + j, nq - 1)

    qspec = pl.BlockSpec((t, HEAD_DIM), lambda h, i, j: (qrow(h, i, j), h))
    dospec = pl.BlockSpec((t, HEAD_DIM), lambda h, i, j: (qrow(h, i, j), do_col0 + h))
    kspec = pl.BlockSpec((t, HEAD_DIM), lambda h, i, j: (i, h))
    vspec = pl.BlockSpec((t, HEAD_DIM), lambda h, i, j: (i, v_col0 + h))
    return pl.pallas_call(
        body, name=name, grid=(H, nq, nj),
        in_specs=[qspec, kspec, vspec, qspec, qspec, dospec, _bias_spec(t)], out_specs=[kspec, kspec],
        out_shape=[jax.ShapeDtypeStruct((S, H * HEAD_DIM), F32), jax.ShapeDtypeStruct((S, H * HEAD_DIM), BF16)],
        scratch_shapes=[pltpu.VMEM((t, HEAD_DIM), F32), pltpu.VMEM((t, HEAD_DIM), F32)],
        compiler_params=_cparams(("parallel", "parallel", "arbitrary")),
    )(q, k, proj, o, lse, do, bias)


def _conv_pre(x_ref, w_ref):
    x = x_ref[...]
    rows = lax.broadcasted_iota(jnp.int32, x.shape, 0)
    shifted = [x]
    acc = x * w_ref[pl.ds(CONV_WIDTH - 1, 1), :]
    for sft in range(1, CONV_WIDTH):
        xs = jnp.where(rows >= sft, pltpu.roll(x, sft, 0), 0.0)
        shifted.append(xs)
        acc = acc + xs * w_ref[pl.ds(CONV_WIDTH - 1 - sft, 1), :]
    return acc, shifted


def _conv_fwd(proj, col0, width, w, name):
    S = proj.shape[0]

    def body(x_ref, w_ref, y_ref):
        acc, _ = _conv_pre(x_ref, w_ref)
        y_ref[...] = _silu(acc)

    return pl.pallas_call(
        body, name=name, grid=(width // LANES,),
        in_specs=[pl.BlockSpec((S, LANES), lambda c: (0, col0 + c)), pl.BlockSpec((CONV_WIDTH, LANES), lambda c: (0, c))],
        out_specs=pl.BlockSpec((S, LANES), lambda c: (0, c)),
        out_shape=jax.ShapeDtypeStruct((S, width), F32),
        compiler_params=_cparams(("parallel",)),
    )(proj, w)


def _conv_bwd(proj, col0, width, w, dy, name):
    S = proj.shape[0]

    def body(x_ref, w_ref, d_ref, dx_ref, dw_ref):
        acc, shifted = _conv_pre(x_ref, w_ref)
        sig = _sigmoid(acc)
        da = d_ref[...] * (sig * (1.0 + acc * (1.0 - sig)))
        rows = lax.broadcasted_iota(jnp.int32, da.shape, 0)
        dx = da * w_ref[pl.ds(CONV_WIDTH - 1, 1), :]
        dw_ref[pl.ds(CONV_WIDTH - 1, 1), :] = jnp.sum(da * shifted[0], axis=0, keepdims=True)
        for sft in range(1, CONV_WIDTH):
            back = jnp.where(rows < S - sft, pltpu.roll(da, S - sft, 0), 0.0)
            dx = dx + back * w_ref[pl.ds(CONV_WIDTH - 1 - sft, 1), :]
            dw_ref[pl.ds(CONV_WIDTH - 1 - sft, 1), :] = jnp.sum(da * shifted[sft], axis=0, keepdims=True)
        dx_ref[...] = dx.astype(BF16)

    return pl.pallas_call(
        body, name=name, grid=(width // LANES,),
        in_specs=[pl.BlockSpec((S, LANES), lambda c: (0, col0 + c)), pl.BlockSpec((CONV_WIDTH, LANES), lambda c: (0, c)),
                  pl.BlockSpec((S, LANES), lambda c: (0, c))],
        out_specs=[pl.BlockSpec((S, LANES), lambda c: (0, c)), pl.BlockSpec((CONV_WIDTH, LANES), lambda c: (0, c))],
        out_shape=[jax.ShapeDtypeStruct((S, width), BF16), jax.ShapeDtypeStruct((CONV_WIDTH, width), F32)],
        compiler_params=_cparams(("parallel",)),
    )(proj, w, dy)


PREP_CHUNKS = 8


def _chunks_prep(qraws, kraws, vs, abs_, alog_row, dtb_row, mask_g, mask_b, t_saved=None):
    n = len(qraws)
    c = qraws[0].shape[0]
    mm_nt, mm_nn = (_mm_nt, _mm_nn) if t_saved is not None else (lambda p, r: _bdot(p, r, _NT), lambda p, r: _bdot(p, r, _NN))
    row = lax.broadcasted_iota(jnp.int32, (c, c), 0)
    col = lax.broadcasted_iota(jnp.int32, (c, c), 1)
    tril, strict, eye = row >= col, row > col, row == col
    eyef = eye.astype(F32)
    neg_rate = -jnp.exp(alog_row)
    q, k, beta, gc_col, gamma, kb, g_last = [], [], [], [], [], [], []
    for i in range(n):
        q.append(_l2(qraws[i]) * (HEAD_DIM ** -0.5))
        k.append(_l2(kraws[i]))
        gfull = neg_rate * _softplus(abs_[i] + dtb_row)
        g = jnp.sum(jnp.where(mask_g, gfull, 0.0), axis=-1, keepdims=True)
        beta.append(jnp.sum(jnp.where(mask_b, _sigmoid(abs_[i]), 0.0), axis=-1, keepdims=True))
        g_row = jnp.sum(jnp.where(eye, g, 0.0), axis=0, keepdims=True)
        gc_col.append(jnp.sum(jnp.where(tril, g_row, 0.0), axis=1, keepdims=True))
        gc_row = jnp.sum(jnp.where(row <= col, g, 0.0), axis=0, keepdims=True)
        gamma.append(jnp.where(tril, jnp.exp(jnp.where(tril, gc_col[i] - gc_row, 0.0)), 0.0))
        kb.append(k[i] * beta[i])
        g_last.append(jnp.sum(g, axis=0, keepdims=True))
    a = [jnp.where(strict, mm_nt(kb[i], k[i]) * gamma[i], 0.0) for i in range(n)]
    if t_saved is None:
        t_inv = [eyef - a[i] for i in range(n)]
        p = a
        for _ in range(int(math.log2(c)) - 1):
            p = [_dot3(p[i], p[i]) for i in range(n)]
            t_inv = [_dot3(t_inv[i], eyef + p[i]) for i in range(n)]
    else:
        t_inv = [_tri_inv_saved(a[i], t_saved[i]) for i in range(n)]
    egc = [jnp.exp(gc_col[i]) for i in range(n)]
    u = [mm_nn(t_inv[i], vs[i] * beta[i]) for i in range(n)]
    w = [mm_nn(t_inv[i], kb[i] * egc[i]) for i in range(n)]
    intra = [mm_nt(q[i], k[i]) * gamma[i] for i in range(n)]
    out = []
    for i in range(n):
        kt = k[i] * jnp.exp(g_last[i] - gc_col[i])
        dec = jnp.broadcast_to(jnp.exp(g_last[i]), (1, HEAD_DIM))
        one = (u[i], w[i], q[i] * egc[i], kt, intra[i], dec)
        out.append(one + (t_inv[i],) if t_saved is None else one)
    return out


def _lane_masks(h):
    lane = lax.broadcasted_iota(jnp.int32, (1, LANES), 1)
    return lane == h, lane == N_DELTA_HEADS + h


def _prep_specs(tr, ab_col):
    H = N_DELTA_HEADS
    return [
        pl.BlockSpec((tr, HEAD_DIM), lambda i, h: (i, h)),
        pl.BlockSpec((tr, HEAD_DIM), lambda i, h: (i, H + h)),
        pl.BlockSpec((tr, HEAD_DIM), lambda i, h: (i, 2 * H + h)),
        pl.BlockSpec((tr, LANES), lambda i, h: (i, ab_col)),
        pl.BlockSpec((1, LANES), lambda i, h: (0, 0)),
        pl.BlockSpec((1, LANES), lambda i, h: (0, 0)),
    ]


def _prep_out_specs(tr):
    nc = tr // CHUNK
    hs = pl.BlockSpec((tr, HEAD_DIM), lambda i, h: (i, h))
    return [hs, hs, hs, hs,
            pl.BlockSpec((None, tr, CHUNK), lambda i, h: (h, i, 0)),
            pl.BlockSpec((None, nc, 1, HEAD_DIM), lambda i, h: (h, i, 0, 0)),
            pl.BlockSpec((None, tr, CHUNK), lambda i, h: (h, i, 0))]


def _prep_out_shapes(S):
    H = N_DELTA_HEADS
    hs = jax.ShapeDtypeStruct((S, H * HEAD_DIM), F32)
    sq = jax.ShapeDtypeStruct((H, S, CHUNK), F32)
    return [hs, hs, hs, hs, sq, jax.ShapeDtypeStruct((H, S // CHUNK, 1, HEAD_DIM), F32), sq]


def _delta_prep_fwd(dqkv, proj, ab_col, alog_row, dtb_row, name):
    S = dqkv.shape[0]
    tr = min(S, PREP_CHUNKS * CHUNK)
    nc = tr // CHUNK

    def body(q_ref, k_ref, v_ref, ab_ref, al_ref, dt_ref, u_ref, w_ref, qd_ref, kt_ref, in_ref, dec_ref, ti_ref):
        mask_g, mask_b = _lane_masks(pl.program_id(1))
        rows = [pl.ds(ci * CHUNK, CHUNK) for ci in range(nc)]
        outs = _chunks_prep([q_ref[rs, :] for rs in rows], [k_ref[rs, :] for rs in rows], [v_ref[rs, :] for rs in rows],
                            [ab_ref[rs, :] for rs in rows], al_ref[...], dt_ref[...], mask_g, mask_b)
        for ci, rs in enumerate(rows):
            u, w, qd, kt, intra, dec, t_inv = outs[ci]
            u_ref[rs, :] = u
            w_ref[rs, :] = w
            qd_ref[rs, :] = qd
            kt_ref[rs, :] = kt
            in_ref[rs, :] = intra
            dec_ref[ci] = dec
            ti_ref[rs, :] = t_inv

    return pl.pallas_call(
        body, name=name, grid=(S // tr, N_DELTA_HEADS),
        in_specs=_prep_specs(tr, ab_col), out_specs=_prep_out_specs(tr), out_shape=_prep_out_shapes(S),
        compiler_params=_cparams(("parallel", "parallel")),
    )(dqkv, dqkv, dqkv, proj, alog_row, dtb_row)


def _delta_prep_bwd(dqkv, proj, ab_col, alog_row, dtb_row, cots, t_inv, name):
    S = dqkv.shape[0]
    H = N_DELTA_HEADS
    tr = min(S, PREP_CHUNKS * CHUNK)
    nc = tr // CHUNK

    def body(q_ref, k_ref, v_ref, ab_ref, al_ref, dt_ref, du_ref, dw_ref, dqd_ref, dkt_ref, din_ref, ddec_ref, ti_ref,
             dq_ref, dk_ref, dv_ref, dab_ref, dal_ref, ddt_ref, dab_sc):
        h = pl.program_id(1)
        mask_g, mask_b = _lane_masks(h)

        @pl.when((pl.program_id(0) == 0) & (h == 0))
        def _():
            dal_ref[...] = jnp.zeros_like(dal_ref)
            ddt_ref[...] = jnp.zeros_like(ddt_ref)

        @pl.when(h == 0)
        def _():
            dab_sc[...] = jnp.zeros_like(dab_sc)

        rows = [pl.ds(ci * CHUNK, CHUNK) for ci in range(nc)]
        fn = functools.partial(_chunks_prep, mask_g=mask_g, mask_b=mask_b, t_saved=[ti_ref[rs, :] for rs in rows])
        _, vjp = jax.vjp(fn, [q_ref[rs, :] for rs in rows], [k_ref[rs, :] for rs in rows], [v_ref[rs, :] for rs in rows],
                         [ab_ref[rs, :] for rs in rows], al_ref[...], dt_ref[...])
        dqs, dks, dvs, dabs, dal, ddt = vjp([(du_ref[rs, :], dw_ref[rs, :], dqd_ref[rs, :], dkt_ref[rs, :], din_ref[rs, :],
                                              ddec_ref[ci]) for ci, rs in enumerate(rows)])
        for ci, rs in enumerate(rows):
            dq_ref[rs, :] = dqs[ci]
            dk_ref[rs, :] = dks[ci]
            dv_ref[rs, :] = dvs[ci]
            dab_sc[rs, :] += dabs[ci]
        dal_ref[...] += dal
        ddt_ref[...] += ddt

        @pl.when(h == H - 1)
        def _():
            dab_ref[...] = dab_sc[...].astype(BF16)

    hs = pl.BlockSpec((tr, HEAD_DIM), lambda i, h: (i, h))
    hshape = jax.ShapeDtypeStruct((S, H * HEAD_DIM), F32)
    row = pl.BlockSpec((1, LANES), lambda i, h: (0, 0))
    rshape = jax.ShapeDtypeStruct((1, LANES), F32)
    return pl.pallas_call(
        body, name=name, grid=(S // tr, H),
        in_specs=_prep_specs(tr, ab_col) + _prep_out_specs(tr),
        out_specs=[hs, hs, hs, pl.BlockSpec((tr, LANES), lambda i, h: (i, 0)), row, row],
        out_shape=[hshape, hshape, hshape, jax.ShapeDtypeStruct((S, LANES), BF16), rshape, rshape],
        scratch_shapes=[pltpu.VMEM((tr, LANES), F32)],
        compiler_params=_cparams(("arbitrary", "arbitrary")),
    )(dqkv, dqkv, dqkv, proj, alog_row, dtb_row, *cots, t_inv)


def _scan_steps(states, us, ws, qds, kts, intras, decs):
    hs = range(len(states))
    v_new = [us[h] - _dot3(ws[h], states[h]) for h in hs]
    o_state = [_dot3(qds[h], states[h]) for h in hs]
    o_intra = [_dot3(intras[h], v_new[h]) for h in hs]
    grown = [_dot3(kts[h], v_new[h], _TN) for h in hs]
    return [o_state[h] + o_intra[h] for h in hs], [states[h] * decs[h] + grown[h] for h in hs]


def _scan_specs(rev, n):
    H = N_DELTA_HEADS

    def cix(i):
        return (n - 1 - i) if rev else i

    row = pl.BlockSpec((CHUNK, H * HEAD_DIM), lambda i: (cix(i), 0))
    return row, pl.BlockSpec((H, CHUNK, CHUNK), lambda i: (0, cix(i), 0)), \
        pl.BlockSpec((H, 1, 1, HEAD_DIM), lambda i: (0, cix(i), 0, 0)), \
        pl.BlockSpec((1, H, HEAD_DIM, HEAD_DIM), lambda i: (cix(i), 0, 0, 0))


def _delta_scan_fwd(u, w, qd, kt, intra, dec, name):
    S = u.shape[0]
    H = N_DELTA_HEADS
    n = S // CHUNK
    row, ispec, dspec, sspec = _scan_specs(False, n)

    def body(u_ref, w_ref, qd_ref, kt_ref, in_ref, dec_ref, o_ref, st_ref, s_sc):
        @pl.when(pl.program_id(0) == 0)
        def _():
            s_sc[...] = jnp.zeros_like(s_sc)

        cols = [pl.ds(h * HEAD_DIM, HEAD_DIM) for h in range(H)]
        states = [s_sc[h] for h in range(H)]
        outs, new = _scan_steps(states, [u_ref[:, cs] for cs in cols], [w_ref[:, cs] for cs in cols],
                                [qd_ref[:, cs] for cs in cols], [kt_ref[:, cs] for cs in cols],
                                [in_ref[h] for h in range(H)], [dec_ref[h, 0] for h in range(H)])
        for h, cs in enumerate(cols):
            st_ref[0, h] = states[h]
            o_ref[:, cs] = outs[h]
            s_sc[h] = new[h]

    return pl.pallas_call(
        body, name=name, grid=(n,),
        in_specs=[row, row, row, row, ispec, dspec], out_specs=[row, sspec],
        out_shape=[jax.ShapeDtypeStruct((S, H * HEAD_DIM), F32), jax.ShapeDtypeStruct((n, H, HEAD_DIM, HEAD_DIM), F32)],
        scratch_shapes=[pltpu.VMEM((H, HEAD_DIM, HEAD_DIM), F32)],
        compiler_params=_cparams(("arbitrary",)),
    )(u, w, qd, kt, intra, dec)


def _delta_scan_bwd(u, w, qd, kt, intra, dec, states, do, name):
    S = u.shape[0]
    H = N_DELTA_HEADS
    n = S // CHUNK
    row, ispec, dspec, sspec = _scan_specs(True, n)

    def body(u_ref, w_ref, qd_ref, kt_ref, in_ref, dec_ref, st_ref, do_ref,
             du_ref, dw_ref, dqd_ref, dkt_ref, din_ref, ddec_ref, ds_sc):
        @pl.when(pl.program_id(0) == 0)
        def _():
            ds_sc[...] = jnp.zeros_like(ds_sc)

        cols = [pl.ds(h * HEAD_DIM, HEAD_DIM) for h in range(H)]
        _, vjp = jax.vjp(_scan_steps, [st_ref[0, h] for h in range(H)], [u_ref[:, cs] for cs in cols],
                         [w_ref[:, cs] for cs in cols], [qd_ref[:, cs] for cs in cols], [kt_ref[:, cs] for cs in cols],
                         [in_ref[h] for h in range(H)], [dec_ref[h, 0] for h in range(H)])
        dstate, du, dw, dqd, dkt, din, ddec = vjp(([do_ref[:, cs] for cs in cols], [ds_sc[h] for h in range(H)]))
        for h, cs in enumerate(cols):
            du_ref[:, cs] = du[h]
            dw_ref[:, cs] = dw[h]
            dqd_ref[:, cs] = dqd[h]
            dkt_ref[:, cs] = dkt[h]
            din_ref[h] = din[h]
            ddec_ref[h, 0] = ddec[h]
            ds_sc[h] = dstate[h]

    hshape = jax.ShapeDtypeStruct((S, H * HEAD_DIM), F32)
    return pl.pallas_call(
        body, name=name, grid=(n,),
        in_specs=[row, row, row, row, ispec, dspec, sspec, row],
        out_specs=[row, row, row, row, ispec, dspec],
        out_shape=[hshape, hshape, hshape, hshape, jax.ShapeDtypeStruct((H, S, CHUNK), F32),
                   jax.ShapeDtypeStruct((H, n, 1, HEAD_DIM), F32)],
        scratch_shapes=[pltpu.VMEM((H, HEAD_DIM, HEAD_DIM), F32)],
        compiler_params=_cparams(("arbitrary",)),
    )(u, w, qd, kt, intra, dec, states, do)


def _gated_norm(od, z, gain):
    return _rms(od, gain) * _silu(z)


def _post_fwd(od, proj, z_col0, gain, name):
    S = od.shape[0]
    H = N_DELTA_HEADS
    tr = _pick(S, 512)

    def body(od_ref, z_ref, g_ref, o_ref):
        o_ref[...] = _gated_norm(od_ref[...], z_ref[...], g_ref[...]).astype(BF16)

    return pl.pallas_call(
        body, name=name, grid=(S // tr, H),
        in_specs=[_head_spec(tr, 0), _head_spec(tr, z_col0), _gain_spec()],
        out_specs=_head_spec(tr, 0), out_shape=jax.ShapeDtypeStruct((S, H * HEAD_DIM), BF16),
        compiler_params=_cparams(("parallel", "parallel")),
    )(od, proj, gain)


def _post_bwd(od, proj, z_col0, gain, do, do_col0, name):
    S = od.shape[0]
    H = N_DELTA_HEADS
    tr = _pick(S, 512)

    def body(od_ref, z_ref, g_ref, do_ref, dod_ref, dz_ref, dg_ref):
        @pl.when((pl.program_id(0) == 0) & (pl.program_id(1) == 0))
        def _():
            dg_ref[...] = jnp.zeros_like(dg_ref)

        _, vjp = jax.vjp(_gated_norm, od_ref[...], z_ref[...], g_ref[...])
        dod, dz, dg = vjp(do_ref[...])
        dod_ref[...] = dod
        dz_ref[...] = dz.astype(BF16)
        dg_ref[...] += dg

    return pl.pallas_call(
        body, name=name, grid=(S // tr, H),
        in_specs=[_head_spec(tr, 0), _head_spec(tr, z_col0), _gain_spec(), _head_spec(tr, do_col0)],
        out_specs=[_head_spec(tr, 0), _head_spec(tr, 0), _gain_spec()],
        out_shape=[jax.ShapeDtypeStruct((S, H * HEAD_DIM), F32), jax.ShapeDtypeStruct((S, H * HEAD_DIM), BF16),
                   jax.ShapeDtypeStruct((1, HEAD_DIM), F32)],
        compiler_params=_cparams(("arbitrary", "arbitrary")),
    )(od, proj, gain, do)


def _adam_math(w, g, m, v):
    m = ADAM_B1 * m + (1.0 - ADAM_B1) * g
    v = ADAM_B2 * v + (1.0 - ADAM_B2) * (g * g)
    m_hat = m / (1.0 - ADAM_B1 ** ADAM_STEP)
    v_hat = v / (1.0 - ADAM_B2 ** ADAM_STEP)
    delta = -ADAM_LR * (m_hat / (jnp.sqrt(v_hat) + ADAM_EPS) + ADAM_WD * w)
    return delta, m, v


def _adamw(w, g, m, v, name):
    R, C = w.shape
    tr = R if R * C * 4 <= (1 << 20) else _pick8(R, max(8, (1 << 20) // (C * 4)))

    def body(w_ref, g_ref, m_ref, v_ref, d_ref, nm_ref, nv_ref):
        d, nm, nv = _adam_math(w_ref[...], g_ref[...], m_ref[...], v_ref[...])
        d_ref[...] = d
        nm_ref[...] = nm
        nv_ref[...] = nv

    spec = pl.BlockSpec((tr, C), lambda i: (i, 0))
    shp = jax.ShapeDtypeStruct((R, C), F32)
    return pl.pallas_call(
        body, name=name, grid=(R // tr,), in_specs=[spec] * 4, out_specs=[spec] * 3, out_shape=[shp] * 3,
        compiler_params=_cparams(("parallel",)),
    )(w, g, m, v)


def _pick8(dim, pref):
    t = (min(dim, pref) // 8) * 8
    while t >= 8:
        if dim % t == 0:
            return t
        t -= 8
    return dim


def _adamw_outer(w, m, v, cond_t, rhs, name):
    R, C = w.shape
    tr = _pick8(R, 128)
    nb = cond_t.shape[1]
    lhs_t = cond_t

    def body(w_ref, m_ref, v_ref, a_ref, b_ref, g_ref, d_ref, nm_ref, nv_ref):
        g = _dot(_silu(a_ref[...]), b_ref[...])
        d, nm, nv = _adam_math(w_ref[...], g, m_ref[...], v_ref[...])
        g_ref[...] = g
        d_ref[...] = d
        nm_ref[...] = nm
        nv_ref[...] = nv

    spec = pl.BlockSpec((tr, C), lambda i: (i, 0))
    shp = jax.ShapeDtypeStruct((R, C), F32)
    return pl.pallas_call(
        body, name=name, grid=(R // tr,),
        in_specs=[spec, spec, spec, pl.BlockSpec((tr, nb), lambda i: (i, 0)), pl.BlockSpec((nb, C), lambda i: (0, 0))],
        out_specs=[spec] * 4, out_shape=[shp] * 4,
        compiler_params=_cparams(("parallel",)),
    )(w, m, v, lhs_t, rhs)


def _ada_fwd(cond, w, bias, name):
    a = cond
    nb, K = a.shape
    N = w.shape[1]
    tn = _pick(N, 512)

    def body(a_ref, w_ref, b_ref, o_ref):
        o_ref[...] = _dot(_silu(a_ref[...]), w_ref[...]) + b_ref[...]

    return pl.pallas_call(
        body, name=name, grid=(N // tn,),
        in_specs=[pl.BlockSpec((nb, K), lambda j: (0, 0)), pl.BlockSpec((K, tn), lambda j: (0, j)), pl.BlockSpec((1, tn), lambda j: (0, j))],
        out_specs=pl.BlockSpec((nb, tn), lambda j: (0, j)), out_shape=jax.ShapeDtypeStruct((nb, N), F32),
        compiler_params=_cparams(("parallel",)),
    )(a, w, bias)


def _add_cast(parts, out_dtypes, name):
    shape = parts[0].shape
    G, R, C = shape
    tr = _pick8(R, max(8, (1 << 20) // (C * 4)))
    n_in = len(parts)

    def body(*refs):
        acc = refs[0][...].astype(F32)
        for r in refs[1:n_in]:
            acc = acc + r[...].astype(F32)
        for o, dt in zip(refs[n_in:], out_dtypes):
            o[...] = acc.astype(dt)

    spec = pl.BlockSpec((1, tr, C), lambda g, i: (g, i, 0))
    outs = pl.pallas_call(
        body, name=name, grid=(G, R // tr), in_specs=[spec] * n_in, out_specs=[spec] * len(out_dtypes),
        out_shape=[jax.ShapeDtypeStruct(shape, dt) for dt in out_dtypes],
        compiler_params=_cparams(("parallel", "parallel")),
    )(*parts)
    return outs


def _me():
    return lax.axis_index("x"), lax.axis_index("y"), lax.axis_index("c")


def _xor_peer(k):
    x, y, c = _me()
    dx, dy, dc = (k >> 2) & 1, (k >> 1) & 1, k & 1
    return (x ^ dx if dx else x, y ^ dy if dy else y, c ^ dc if dc else c)


ANY = pl.BlockSpec(memory_space=pl.ANY)


def _all_gather_small(v, name, after=None):
    R, C = v.shape
    extra = [] if after is None else [after]

    def body(v_ref, *rest):
        out_ref, send_sems, recv_sems = rest[len(extra):]
        x, y, c = _me()
        mine = 4 * x + 2 * y + c
        out_ref[mine] = v_ref[...]
        copies = []
        for k in range(1, 8):
            cp = pltpu.make_async_remote_copy(src_ref=v_ref, dst_ref=out_ref.at[mine], send_sem=send_sems.at[k - 1],
                                              recv_sem=recv_sems.at[k - 1], device_id=_xor_peer(k), device_id_type=MESH)
            cp.start()
            copies.append(cp)
        for k in range(1, 8):
            px, py, pc = _xor_peer(k)
            pltpu.make_async_remote_copy(src_ref=v_ref, dst_ref=out_ref.at[4 * px + 2 * py + pc], send_sem=send_sems.at[k - 1],
                                         recv_sem=recv_sems.at[k - 1], device_id=_xor_peer(k), device_id_type=MESH).wait_recv()
        for cp in copies:
            cp.wait_send()

    return pl.pallas_call(
        body, name=name, out_shape=jax.ShapeDtypeStruct((8, R, C), F32),
        in_specs=[pl.BlockSpec(memory_space=pltpu.VMEM)] + [ANY] * len(extra), out_specs=pl.BlockSpec(memory_space=pltpu.VMEM),
        scratch_shapes=[pltpu.SemaphoreType.DMA((7,)), pltpu.SemaphoreType.DMA((7,))],
        compiler_params=pltpu.CompilerParams(vmem_limit_bytes=VMEM_LIMIT),
    )(v, *extra)


def _chip_peers():
    x, y, _ = _me()
    return [(1, (x, 1 - y)), (2, (1 - x, y)), (3, (1 - x, 1 - y))]


def _all_gather_shards(shards, name):
    n = len(shards)

    def body(*refs):
        ins, outs = refs[:n], refs[n:2 * n]
        send_sems, recv_sems = refs[2 * n:]
        x, y, c = _me()
        chip = 2 * x + y
        sib = (x, y, 1 - c)
        peers = _chip_peers()
        sends = []
        for t in range(n):
            half = ins[t].shape[0] // 2
            mine = pl.ds(c * half, half)
            for p, (k, (px, py)) in enumerate(peers):
                cp = pltpu.make_async_remote_copy(src_ref=ins[t].at[mine], dst_ref=outs[t].at[chip, mine],
                                                  send_sem=send_sems.at[6 * t + p], recv_sem=recv_sems.at[6 * t + p],
                                                  device_id=(px, py, c), device_id_type=MESH)
                cp.start()
                sends.append(cp)
        for t in range(n):
            half = ins[t].shape[0] // 2
            mine = pl.ds(c * half, half)
            for p, (k, (px, py)) in enumerate(peers):
                src_chip = 2 * px + py
                landed = outs[t].at[src_chip, mine]
                pltpu.make_async_remote_copy(src_ref=landed, dst_ref=landed, send_sem=send_sems.at[6 * t + p],
                                             recv_sem=recv_sems.at[6 * t + p], device_id=(px, py, c), device_id_type=MESH).wait_recv()
                fw = pltpu.make_async_remote_copy(src_ref=landed, dst_ref=landed, send_sem=send_sems.at[6 * t + 3 + p],
                                                  recv_sem=recv_sems.at[6 * t + 3 + p], device_id=sib, device_id_type=MESH)
                fw.start()
                sends.append(fw)
        for t in range(n):
            half = ins[t].shape[0] // 2
            theirs = pl.ds((1 - c) * half, half)
            for p, (k, (px, py)) in enumerate(peers):
                got = outs[t].at[2 * px + py, theirs]
                pltpu.make_async_remote_copy(src_ref=got, dst_ref=got, send_sem=send_sems.at[6 * t + 3 + p],
                                             recv_sem=recv_sems.at[6 * t + 3 + p], device_id=sib, device_id_type=MESH).wait_recv()
        for cp in sends:
            cp.wait_send()

    return pl.pallas_call(
        body, name=name,
        out_shape=[jax.ShapeDtypeStruct((4,) + s.shape, s.dtype) for s in shards],
        in_specs=[ANY] * n, out_specs=[ANY] * n,
        scratch_shapes=[pltpu.SemaphoreType.DMA((6 * n,)), pltpu.SemaphoreType.DMA((6 * n,))],
    )(*shards)


def _swap_halves_with_sibling(slabs, name):
    n = len(slabs)

    def body(*refs):
        ins, outs = refs[:n], refs[n:2 * n]
        send_sems, recv_sems = refs[2 * n:]
        x, y, c = _me()
        sib = (x, y, 1 - c)
        cps = []
        for t in range(n):
            half = ins[t].shape[1] // 2
            cp = pltpu.make_async_remote_copy(src_ref=ins[t].at[:, pl.ds((1 - c) * half, half)], dst_ref=outs[t],
                                              send_sem=send_sems.at[t], recv_sem=recv_sems.at[t], device_id=sib, device_id_type=MESH)
            cp.start()
            cps.append(cp)
        for cp in cps:
            cp.wait()

    return pl.pallas_call(
        body, name=name,
        out_shape=[jax.ShapeDtypeStruct((4, s.shape[1] // 2, s.shape[2]), s.dtype) for s in slabs],
        in_specs=[ANY] * n, out_specs=[ANY] * n,
        scratch_shapes=[pltpu.SemaphoreType.DMA((n,)), pltpu.SemaphoreType.DMA((n,))],
    )(*slabs)


def _exchange_between_chips(slabs, name):
    n = len(slabs)

    def body(*refs):
        ins, outs = refs[:n], refs[n:2 * n]
        send_sems, recv_sems = refs[2 * n:]
        x, y, c = _me()
        chip = 2 * x + y
        cps = []
        for t in range(n):
            for p, (k, (px, py)) in enumerate(_chip_peers()):
                cp = pltpu.make_async_remote_copy(src_ref=ins[t].at[2 * px + py], dst_ref=outs[t].at[chip],
                                                  send_sem=send_sems.at[3 * t + p], recv_sem=recv_sems.at[3 * t + p],
                                                  device_id=(px, py, c), device_id_type=MESH)
                cp.start()
                cps.append(cp)
        for t in range(n):
            for p, (k, (px, py)) in enumerate(_chip_peers()):
                got = outs[t].at[2 * px + py]
                pltpu.make_async_remote_copy(src_ref=got, dst_ref=got, send_sem=send_sems.at[3 * t + p],
                                             recv_sem=recv_sems.at[3 * t + p], device_id=(px, py, c), device_id_type=MESH).wait_recv()
        for cp in cps:
            cp.wait_send()

    return pl.pallas_call(
        body, name=name,
        out_shape=[jax.ShapeDtypeStruct(s.shape, s.dtype) for s in slabs],
        in_specs=[ANY] * n, out_specs=[ANY] * n,
        scratch_shapes=[pltpu.SemaphoreType.DMA((3 * n,)), pltpu.SemaphoreType.DMA((3 * n,))],
    )(*slabs)


def _send_halves_to_sibling(halves, name):
    n = len(halves)

    def body(*refs):
        ins, outs = refs[:n], refs[n:2 * n]
        send_sems, recv_sems = refs[2 * n:]
        x, y, c = _me()
        sib = (x, y, 1 - c)
        cps = []
        for t in range(n):
            cp = pltpu.make_async_remote_copy(src_ref=ins[t], dst_ref=outs[t], send_sem=send_sems.at[t],
                                              recv_sem=recv_sems.at[t], device_id=sib, device_id_type=MESH)
            cp.start()
            cps.append(cp)
        for cp in cps:
            cp.wait()

    return pl.pallas_call(
        body, name=name,
        out_shape=[jax.ShapeDtypeStruct(s.shape, s.dtype) for s in halves],
        in_specs=[ANY] * n, out_specs=[ANY] * n,
        scratch_shapes=[pltpu.SemaphoreType.DMA((n,)), pltpu.SemaphoreType.DMA((n,))],
    )(*halves)


HBM_SPEC = pl.BlockSpec(memory_space=pltpu.HBM)
SEM_SPEC = pl.BlockSpec(memory_space=pltpu.SEMAPHORE)
DATAFLOW = pltpu.SideEffectType.DATAFLOW_SIDE_EFFECTING


def _plan_gather_direct(src_refs, land_refs):
    x, y, c = _me()
    chip = 2 * x + y
    plan = []
    for s, land in zip(src_refs, land_refs):
        half = s.shape[0] // 2
        for _, (px, py) in _chip_peers():
            for pc in (c, 1 - c):
                plan.append((s.at[pl.ds(c * half, half)], land.at[chip, pl.ds(c * half, half)],
                             land.at[2 * px + py, pl.ds(pc * half, half)], (px, py, pc)))
    return plan


def _plan_scatter_direct(src_refs, land_refs):
    x, y, c = _me()
    plan = []
    for s, land in zip(src_refs, land_refs):
        half = s.shape[1] // 2
        for k in range(1, 8):
            px, py, pc = _xor_peer(k)
            plan.append((s.at[2 * px + py, pl.ds(pc * half, half)], land.at[4 * x + 2 * y + c],
                         land.at[4 * px + 2 * py + pc], (px, py, pc)))
    return plan


def _start_copies(srcs, lands, plan_fn, name, after=None):
    n = len(srcs)
    n_copies = len(srcs) * (7 if plan_fn is _plan_scatter_direct else 6)
    extra = [] if after is None else [after]

    def body(*refs):
        refs = refs[:2 * n] + refs[2 * n + len(extra):]
        send_sems, recv_sems, token = refs[2 * n], refs[2 * n + 1], refs[-1]
        for i, (src, dst, _, peer) in enumerate(plan_fn(refs[:n], refs[n:2 * n])):
            pltpu.make_async_remote_copy(src_ref=src, dst_ref=dst, send_sem=send_sems.at[i], recv_sem=recv_sems.at[i],
                                         device_id=peer, device_id_type=MESH).start()
        token[...] = jnp.zeros_like(token)

    arrays = list(srcs) + list(lands)
    outs = pl.pallas_call(
        body, name=name,
        out_shape=(pltpu.SemaphoreType.DMA((n_copies,)), pltpu.SemaphoreType.DMA((n_copies,)),
                   *[pltpu.HBM(a.shape, a.dtype) for a in arrays], jax.ShapeDtypeStruct((8, LANES), F32)),
        in_specs=[HBM_SPEC] * (2 * n) + [ANY] * len(extra),
        out_specs=(SEM_SPEC, SEM_SPEC, *[HBM_SPEC] * (2 * n), pl.BlockSpec(memory_space=pltpu.VMEM)),
        input_output_aliases={i: 2 + i for i in range(2 * n)},
        compiler_params=pltpu.CompilerParams(has_side_effects=DATAFLOW),
    )(*[pltpu.with_memory_space_constraint(a, pltpu.HBM) for a in arrays], *extra)
    return outs[0], outs[1], list(outs[2:2 + n]), list(outs[2 + n:2 + 2 * n]), outs[-1]


def _wait_copies(send_sems, recv_sems, srcs, lands, after, plan_fn, name):
    n = len(srcs)

    def body(*refs):
        send_sems, recv_sems = refs[2 * n], refs[2 * n + 1]
        for i, (src, _, arrival, peer) in enumerate(plan_fn(refs[:n], refs[n:2 * n])):
            cp = pltpu.make_async_remote_copy(src_ref=src, dst_ref=arrival, send_sem=send_sems.at[i], recv_sem=recv_sems.at[i],
                                              device_id=peer, device_id_type=MESH)
            cp.wait_send()
            cp.wait_recv()

    arrays = list(srcs) + list(lands)
    outs = pl.pallas_call(
        body, name=name,
        out_shape=tuple(pltpu.HBM(a.shape, a.dtype) for a in arrays),
        in_specs=[HBM_SPEC] * (2 * n) + [SEM_SPEC, SEM_SPEC, ANY],
        out_specs=tuple([HBM_SPEC] * (2 * n)),
        input_output_aliases={i: i for i in range(2 * n)},
        compiler_params=pltpu.CompilerParams(has_side_effects=DATAFLOW),
    )(*arrays, send_sems, recv_sems, after)
    return list(outs[n:])


def _rope_tables(positions):
    half = ROPE_DIM // 2
    S = positions.shape[0]
    inv_freq = ROPE_THETA ** (-jnp.arange(half, dtype=F32) / half)
    ang = positions.astype(F32)[:, None] * inv_freq
    cos, sin = jnp.cos(ang), jnp.sin(ang)
    zeros = functools.partial(jnp.zeros, dtype=F32)
    cc = jnp.concatenate([cos, cos, jnp.ones((S, HEAD_DIM - ROPE_DIM), F32)], axis=1)
    s1 = jnp.concatenate([-sin, zeros((S, HEAD_DIM - half))], axis=1)
    s2 = jnp.concatenate([zeros((S, half)), sin, zeros((S, HEAD_DIM - ROPE_DIM))], axis=1)
    return cc, s1, s2


def _ffn_fwd(x, gain, shift, scale, gate, w_gu, w_d, fs, tag):
    S = x.shape[0]
    tm = _pick(S, 512)
    h = _pre_fwd(x, gain, shift, scale, tag + "_pre")

    def swiglu_out(tile, ins, outs):
        outs[0][...] = tile
        outs[1][...] = _swiglu_fn(tile[:, :fs], tile[:, fs:]).astype(BF16)

    ab, s = _matmul(h, w_gu, "nn", F32, tag + "_gate_up", tm=tm, tn=2 * fs, tk=4096, epilogue=(swiglu_out, [], [
        ((S, 8 * fs), F32, (tm, 2 * fs), lambda i, j: (i, j)), ((S, 4 * fs), BF16, (tm, fs), lambda i, j: (i, j))]))
    f = _matmul(s, w_d, "nn", F32, tag + "_down", tm=512, tn=1024, tk=8192)
    xn = _residual_fwd(x, gate, f, 0.5, tag + "_res")
    return xn, (x, h, ab, s, f)


def _ffn_bwd(dxn, saved, gain, shift, scale, gate, w_gu, w_d, fs, tag, on_dw_d=None):
    x, h, ab, s, f = saved
    df, dgate = _residual_bwd(gate, f, dxn, 0.5, tag + "_res_bwd")
    S = x.shape[0]
    tm = _pick(S, 512)
    dw_d = _matmul(s, df, "tn", F32, tag + "_down_dw", tm=1408, tn=1024, tk=2048)
    if on_dw_d is not None:
        shift = shift + on_dw_d(dw_d)

    def swiglu_back(tile, ins, outs):
        _, vjp = jax.vjp(_swiglu_fn, ins[0][:, :fs], ins[0][:, fs:])
        da, db = vjp(tile)
        outs[0][:, :fs] = da.astype(BF16)
        outs[0][:, fs:] = db.astype(BF16)

    (dab,) = _matmul(df, w_d, "nt", F32, tag + "_down_dx", tm=tm, tn=fs, tk=4096, epilogue=(
        swiglu_back, [(ab, (tm, 2 * fs), lambda i, j: (i, j))], [((S, 8 * fs), BF16, (tm, 2 * fs), lambda i, j: (i, j))]))
    dh = _matmul(dab, w_gu, "nt", F32, tag + "_gate_up_dx", tm=1024, tn=1024, tk=2816)
    dw_gu = _matmul(h, dab, "tn", F32, tag + "_gate_up_dw", tm=512, tn=2 * fs, tk=2048, col_slabs=True)
    dx, dgain, dshift, dscale = _pre_bwd(x, gain, shift, scale, dh, dxn, tag + "_pre_bwd")
    return dx, dw_gu, dw_d, dgain, dshift, dscale, dgate


def _flat_pad(parts, rows, cols):
    flat = jnp.concatenate([p.reshape(-1).astype(F32) for p in parts])
    return jnp.pad(flat, (0, rows * cols - flat.shape[0])).reshape(rows, cols)


def _cols_to_slabs(w, n):
    R, NC = w.shape
    return jnp.transpose(w.reshape(R, n, NC // n), (1, 0, 2))


def kernel(x, c, positions, w_ada, b_ada, ffn1_norm, ffn1_w_gate, ffn1_w_up, ffn1_w_down, mix_norm, w_in, conv_w, q_norm, k_norm, a_log, dt_bias, delta_out_norm, w_out, ffn2_norm, ffn2_w_gate, ffn2_w_up, ffn2_w_down, loss_target, m_w_ada, m_b_ada, m_ffn1_norm, m_ffn1_w_gate, m_ffn1_w_up, m_ffn1_w_down, m_mix_norm, m_w_in, m_conv_w, m_q_norm, m_k_norm, m_a_log, m_dt_bias, m_delta_out_norm, m_w_out, m_ffn2_norm, m_ffn2_w_gate, m_ffn2_w_up, m_ffn2_w_down, v_w_ada, v_b_ada, v_ffn1_norm, v_ffn1_w_gate, v_ffn1_w_up, v_ffn1_w_down, v_mix_norm, v_w_in, v_conv_w, v_q_norm, v_k_norm, v_a_log, v_dt_bias, v_delta_out_norm, v_w_out, v_ffn2_norm, v_ffn2_w_gate, v_ffn2_w_up, v_ffn2_w_down):
    xi, yi, ci = _me()
    chip = 2 * xi + yi
    dev = 2 * chip + ci
    xs = x[0]
    S, D = xs.shape
    HA, HD = N_ATTN_HEADS, N_DELTA_HEADS
    fs = ffn1_w_gate.shape[2]
    n_mod_shard = w_ada.shape[2]
    in_shard = w_in.shape[2]
    in_width = 4 * in_shard
    in_pad = -(-in_width // LANES) * LANES
    conv_shard = conv_w.shape[2]
    conv_width = 4 * conv_shard

    pack0 = jnp.zeros((8, max(D, conv_shard)), F32)
    pack0 = pack0.at[0, :D].set(c[0]).at[1:1 + CONV_WIDTH, :conv_shard].set(conv_w[0])
    got0 = _all_gather_small(pack0, "gather_cond")
    c_all = got0[:, 0, :D]
    conv_full = jnp.transpose(got0[::2, 1:1 + CONV_WIDTH, :conv_shard], (1, 0, 2)).reshape(CONV_WIDTH, conv_width)
    b_ada_mine = lax.dynamic_slice(b_ada, (0, chip * n_mod_shard), (1, n_mod_shard))
    mod_part = _ada_fwd(c_all, w_ada[0], b_ada_mine, "ada_fwd")
    got1 = _all_gather_small(mod_part, "gather_mod")
    mod = lax.dynamic_index_in_dim(got1[::2], dev, axis=1, keepdims=False).reshape(1, 4 * n_mod_shard)
    sh1, sc1, gt1, sh2, sc2, gt2, sh3, sc3, gt3 = [mod[:, i * D:(i + 1) * D] for i in range(N_MOD)]

    shards = [w[0].astype(BF16) for w in (ffn1_w_gate, ffn1_w_up, ffn1_w_down, w_in, w_out, ffn2_w_gate, ffn2_w_up, ffn2_w_down)]
    gathered = _all_gather_shards(shards[:3], "gather_weights")
    g1g, g1u, g1d = [lax.dynamic_update_index_in_dim(g, s, chip, 0) for g, s in zip(gathered, shards[:3])]
    zones = [lax.dynamic_update_index_in_dim(lax.empty((4,) + s.shape, BF16), s, chip, 0) for s in shards[3:]]
    ag_in = _start_copies(shards[3:4], zones[:1], _plan_gather_direct, "gather_in_start")
    sh1 = sh1 + ag_in[4][0, 0]

    def gate_up(gg, gu):
        return jnp.transpose(jnp.concatenate([gg, gu], axis=2), (1, 0, 2)).reshape(D, 8 * fs)

    w_gu1, w_d1 = gate_up(g1g, g1u), g1d.reshape(4 * fs, D)

    x1, saved1 = _ffn_fwd(xs, ffn1_norm, sh1, sc1, gt1, w_gu1, w_d1, fs, "ffn1")
    (gin,) = _wait_copies(ag_in[0], ag_in[1], ag_in[2], ag_in[3], x1, _plan_gather_direct, "gather_in_wait")
    w_in_f = jnp.pad(jnp.transpose(gin, (1, 0, 2)).reshape(D, in_width), ((0, 0), (0, in_pad - in_width)))
    ag_rest = _start_copies(shards[4:], zones[1:], _plan_gather_direct, "gather_rest_start", after=gin)
    sh2 = sh2 + ag_rest[4][0, 0]

    cc, s1, s2 = _rope_tables(positions[0])
    alog_row = jnp.pad(a_log, ((0, 0), (0, LANES - HD)))
    dtb_row = jnp.pad(dt_bias, ((0, 0), (0, LANES - HD)))
    col_k, col_v, col_d, col_z, col_ab = HA, 2 * HA, 3 * HA, 3 * HA + 3 * HD, 3 * HA + 4 * HD
    h2 = _pre_fwd(x1, mix_norm, sh2, sc2, "mix_pre")
    proj = _matmul(h2, w_in_f, "nn", F32, "mix_in_proj", tm=512, tn=2432, tk=4096)
    attn_bias = _log_multiplicity_table(_pick(S, ATTN_TILE))
    qa, ka = _attn_prep_fwd(proj, q_norm, k_norm, cc, s1, s2, "attn_prep")
    oa, oa_b, lse = _attn_fwd(qa, ka, proj, col_v, attn_bias, "attn_fwd")
    dqkv = _conv_fwd(proj, col_d, conv_width, conv_full, "conv_fwd")
    *prep, t_inv = _delta_prep_fwd(dqkv, proj, col_ab, alog_row, dtb_row, "delta_prep")
    od_raw, states = _delta_scan_fwd(*prep, "delta_scan")
    od = _post_fwd(od_raw, proj, col_z, delta_out_norm, "delta_post")
    o = jnp.concatenate([oa_b, od], axis=1)
    gout, g2g, g2u, g2d = _wait_copies(ag_rest[0], ag_rest[1], ag_rest[2], ag_rest[3], o, _plan_gather_direct, "gather_rest_wait")
    w_out_f = gout.reshape(-1, D)
    w_gu2, w_d2 = gate_up(g2g, g2u), g2d.reshape(4 * fs, D)
    mo = _matmul(o, w_out_f, "nn", F32, "mix_out_proj", tm=512, tn=1024, tk=4096)
    x2 = _residual_fwd(x1, gt2, mo, 1.0, "mix_res")

    x3, saved3 = _ffn_fwd(x2, ffn2_norm, sh3, sc3, gt3, w_gu2, w_d2, fs, "ffn2")
    loss_part, dy = _loss_head(x3, loss_target[0], "loss_head")
    loss = lax.psum(loss_part[0, 0], ("x", "y", "c"))

    dx2, dw_gu2, dw_d2, dgain3, dsh3, dsc3, dgt3 = _ffn_bwd(dy, saved3, ffn2_norm, sh3, sc3, gt3, w_gu2, w_d2, fs, "ffn2")

    def scatter_start(slabs32, name):
        slabs16 = [s.astype(BF16) for s in slabs32]
        zones = []
        for s in slabs16:
            half = s.shape[1] // 2
            own = lax.dynamic_slice(s, (chip, ci * half, 0), (1, half, s.shape[2]))
            zones.append(lax.dynamic_update_slice(lax.empty((8, half, s.shape[2]), BF16), own, (dev, 0, 0)))
        return _start_copies(slabs16, zones, _plan_scatter_direct, name)

    rs_ffn2 = scatter_start([dw_gu2, dw_d2.reshape(4, fs, D)], "rs_ffn2_start")
    dmo, dgt2 = _residual_bwd(gt2 + rs_ffn2[4][0, 0], mo, dx2, 1.0, "mix_res_bwd")
    do = _matmul(dmo, w_out_f, "nt", F32, "mix_out_dx", tm=1024, tn=1024, tk=4096)
    dw_out = _matmul(o, dmo, "tn", F32, "mix_out_dw", tm=1024, tn=1024, tk=1024)
    dq = _attn_bwd_dq(qa, ka, proj, col_v, oa, lse, do, 0, attn_bias, "attn_bwd_dq")
    dk, dv = _attn_bwd_dkv(qa, ka, proj, col_v, oa, lse, do, 0, attn_bias, "attn_bwd_dkv")
    dpq, dpk, dq_gain, dk_gain = _attn_prep_bwd(proj, q_norm, k_norm, cc, s1, s2, dq, dk, "attn_prep_bwd")
    dod, dz, ddn_gain = _post_bwd(od_raw, proj, col_z, delta_out_norm, do, HA, "delta_post_bwd")
    cots = _delta_scan_bwd(*prep, states, dod, "delta_scan_bwd")
    ddq, ddk, ddv, dab, dalog, ddtb = _delta_prep_bwd(dqkv, proj, col_ab, alog_row, dtb_row, cots, t_inv, "delta_prep_bwd")
    dconv_in, dconv_w = _conv_bwd(proj, col_d, conv_width, conv_full, jnp.concatenate([ddq, ddk, ddv], axis=1), "conv_bwd")
    dproj = jnp.concatenate([dpq, dpk, dv, dconv_in, dz, dab], axis=1)
    dh2 = _matmul(dproj, w_in_f, "nt", F32, "mix_in_dx", tm=1024, tn=1024, tk=2816)
    dw_in = _matmul(h2, dproj, "tn", F32, "mix_in_dw", tm=512, tn=2432, tk=2048)
    rs_mix = scatter_start([_cols_to_slabs(dw_in[:, :in_width], 4), dw_out.reshape(4, -1, D)], "rs_mix_start")
    dx1, dgain2, dsh2, dsc2 = _pre_bwd(x1, mix_norm, sh2, sc2, dh2, dx2, "mix_pre_bwd")

    rs_ffn1d = []

    def send_dw_d1(dw_d):
        rs_ffn1d.extend(scatter_start([dw_d.reshape(4, fs, D)], "rs_ffn1d_start"))
        return rs_ffn1d[4][0, 0]

    dx0, dw_gu1, dw_d1, dgain1, dsh1, dsc1, dgt1 = _ffn_bwd(dx1, saved1, ffn1_norm, sh1, sc1, gt1 + rs_mix[4][0, 0], w_gu1, w_d1, fs,
                                                            "ffn1", on_dw_d=send_dw_d1)

    slabs = [dw_gu1]
    from_sibling = _swap_halves_with_sibling(slabs, "rs_sibling_swap")
    chip_sums32, chip_sums16 = [], []
    for t, (slab, other) in enumerate(zip(slabs, from_sibling)):
        half = slab.shape[1] // 2
        mine = lax.dynamic_slice_in_dim(slab, ci * half, half, axis=1)
        p32, p16 = _add_cast([mine, other], [F32, BF16], "rs_chip_sum_%d" % t)
        chip_sums32.append(p32)
        chip_sums16.append(p16)
    from_chips = _exchange_between_chips(chip_sums16, "rs_chip_exchange")
    halves = []
    for t, (p32, got) in enumerate(zip(chip_sums32, from_chips)):
        parts = [lax.dynamic_index_in_dim(p32, chip, axis=0, keepdims=True)]
        parts += [lax.dynamic_index_in_dim(got, (chip + k) % 4, axis=0, keepdims=True) for k in (1, 2, 3)]
        halves.append(_add_cast(parts, [F32], "rs_total_%d" % t)[0][0])
    arrived = _wait_copies(rs_ffn1d[0], rs_ffn1d[1], rs_ffn1d[2], rs_ffn1d[3], dx0, _plan_scatter_direct, "rs_ffn1d_wait")
    arrived += _wait_copies(rs_mix[0], rs_mix[1], rs_mix[2], rs_mix[3], dx0, _plan_scatter_direct, "rs_mix_wait")
    arrived += _wait_copies(rs_ffn2[0], rs_ffn2[1], rs_ffn2[2], rs_ffn2[3], dx0, _plan_scatter_direct, "rs_ffn2_wait")
    for t, zone in enumerate(arrived):
        halves.append(_add_cast([zone[d:d + 1] for d in range(8)], [F32], "rs_total_%d" % (t + 2))[0][0])
    theirs = _send_halves_to_sibling(halves, "rs_sibling_join")
    g_gu1, g_d1, g_in, g_out, g_gu2, g_d2 = [
        jnp.where(ci == 0, jnp.concatenate([mine, other], axis=0), jnp.concatenate([other, mine], axis=0))
        for mine, other in zip(halves, theirs)]

    n_small = N_MOD * D + 3 * D + 5 * LANES + CONV_WIDTH * conv_width
    cols_small = -(-n_small // (8 * LANES)) * LANES
    small = _flat_pad([dsh1, dsc1, dgt1, dsh2, dsc2, dgt2, dsh3, dsc3, dgt3, dgain1, dgain2, dgain3,
                       dq_gain, dk_gain, dalog, ddtb, ddn_gain, dconv_w], 8, cols_small)
    got2 = _all_gather_small(small, "gather_small_grads", after=theirs[0])
    small_sum = _add_cast([got2[d:d + 1] for d in range(8)], [F32], "sum_small_grads")[0].reshape(-1)
    dmod_all = got2.reshape(8, -1)[:, :N_MOD * D]
    off = [0]

    def take(n):
        off[0] += n
        return small_sum[off[0] - n:off[0]]

    g_b_ada = take(N_MOD * D).reshape(1, -1)
    g_ffn1_norm, g_mix_norm, g_ffn2_norm = take(D).reshape(1, D), take(D).reshape(1, D), take(D).reshape(1, D)
    g_q_norm, g_k_norm = take(LANES).reshape(1, -1), take(LANES).reshape(1, -1)
    g_a_log, g_dt_bias = take(LANES)[:HD].reshape(1, HD), take(LANES)[:HD].reshape(1, HD)
    g_dn = take(LANES).reshape(1, -1)
    g_conv_full = take(CONV_WIDTH * conv_width).reshape(CONV_WIDTH, conv_width)
    g_conv = lax.dynamic_slice(g_conv_full, (0, chip * conv_shard), (CONV_WIDTH, conv_shard))

    res = {}

    def upd(name, w, g, m, v):
        d, nm, nv = _adamw(w[0], g, m[0], v[0], "adamw_" + name)
        res[name] = (g[None], d[None], nm[None], nv[None])

    upd("ffn1_w_gate", ffn1_w_gate, g_gu1[:, :fs], m_ffn1_w_gate, v_ffn1_w_gate)
    upd("ffn1_w_up", ffn1_w_up, g_gu1[:, fs:], m_ffn1_w_up, v_ffn1_w_up)
    upd("ffn1_w_down", ffn1_w_down, g_d1, m_ffn1_w_down, v_ffn1_w_down)
    upd("w_in", w_in, g_in, m_w_in, v_w_in)
    upd("w_out", w_out, g_out, m_w_out, v_w_out)
    upd("ffn2_w_gate", ffn2_w_gate, g_gu2[:, :fs], m_ffn2_w_gate, v_ffn2_w_gate)
    upd("ffn2_w_up", ffn2_w_up, g_gu2[:, fs:], m_ffn2_w_up, v_ffn2_w_up)
    upd("ffn2_w_down", ffn2_w_down, g_d2, m_ffn2_w_down, v_ffn2_w_down)
    upd("conv_w", conv_w, g_conv, m_conv_w, v_conv_w)

    dmod_mine = lax.dynamic_slice(dmod_all, (0, chip * n_mod_shard), (8, n_mod_shard))
    g, d, nm, nv = _adamw_outer(w_ada[0], m_w_ada[0], v_w_ada[0], jnp.transpose(c_all), dmod_mine, "adamw_w_ada")
    res["w_ada"] = (g[None], d[None], nm[None], nv[None])

    rep = [("b_ada", b_ada, g_b_ada, m_b_ada, v_b_ada), ("ffn1_norm", ffn1_norm, g_ffn1_norm, m_ffn1_norm, v_ffn1_norm),
           ("mix_norm", mix_norm, g_mix_norm, m_mix_norm, v_mix_norm), ("ffn2_norm", ffn2_norm, g_ffn2_norm, m_ffn2_norm, v_ffn2_norm),
           ("q_norm", q_norm, g_q_norm, m_q_norm, v_q_norm), ("k_norm", k_norm, g_k_norm, m_k_norm, v_k_norm),
           ("a_log", a_log, g_a_log, m_a_log, v_a_log), ("dt_bias", dt_bias, g_dt_bias, m_dt_bias, v_dt_bias),
           ("delta_out_norm", delta_out_norm, g_dn, m_delta_out_norm, v_delta_out_norm)]
    n_rep = sum(-(-r[1].shape[1] // LANES) * LANES for r in rep)
    cols_rep = -(-n_rep // (8 * LANES)) * LANES

    def pack_rep(idx):
        return _flat_pad([jnp.pad(r[idx], ((0, 0), (0, -r[idx].shape[1] % LANES))) for r in rep], 8, cols_rep)

    d_rep, nm_rep, nv_rep = [a.reshape(-1) for a in _adamw(pack_rep(1), pack_rep(2), pack_rep(3), pack_rep(4), "adamw_small")]
    o2 = 0
    for name, w, g, _, _ in rep:
        n = w.shape[1]
        res[name] = (g, d_rep[o2:o2 + n].reshape(1, n), nm_rep[o2:o2 + n].reshape(1, n), nv_rep[o2:o2 + n].reshape(1, n))
        o2 += -(-n // LANES) * LANES

    order = ["w_ada", "b_ada", "ffn1_norm", "ffn1_w_gate", "ffn1_w_up", "ffn1_w_down", "mix_norm", "w_in", "conv_w", "q_norm",
             "k_norm", "a_log", "dt_bias", "delta_out_norm", "w_out", "ffn2_norm", "ffn2_w_gate", "ffn2_w_up", "ffn2_w_down"]
    return (loss, dx0[None], *[res[n][0] for n in order], *[res[n][1] for n in order],
            *[res[n][2] for n in order], *[res[n][3] for n in order])
```

```python
import functools
import math

import jax
import jax.numpy as jnp
from jax import lax
from jax.experimental import pallas as pl
from jax.experimental.pallas import tpu as pltpu

F32 = jnp.float32
BF16 = jnp.bfloat16
MESH = pl.DeviceIdType.MESH

HEAD_DIM = 128
N_ATTN_HEADS = 8
N_DELTA_HEADS = 8
DILATED_PATTERNS = ((128, 1), (512, 4), (2048, 16))
MAX_WINDOW = 2048
ROPE_THETA = 500000.0
ROPE_DIM = HEAD_DIM // 4
CONV_WIDTH = 4
CHUNK = 64
NORM_EPS = 1e-6
N_MOD = 9
ADAM_LR = 0.001
ADAM_B1 = 0.9
ADAM_B2 = 0.999
ADAM_EPS = 1e-08
ADAM_WD = 0.01
ADAM_STEP = 10

LANES = 128
VMEM_LIMIT = 56 * 1024 * 1024
ATTN_TILE = 512
HIGHEST = lax.Precision.HIGHEST


def _cparams(sem=None):
    return pltpu.CompilerParams(dimension_semantics=sem, vmem_limit_bytes=VMEM_LIMIT)


def _pick(dim, pref):
    if dim <= pref:
        return dim
    t = (pref // LANES) * LANES
    while t >= LANES:
        if dim % t == 0:
            return t
        t -= LANES
    return dim


def _sigmoid(x):
    return 1.0 / (1.0 + jnp.exp(-x))


def _silu(x):
    return x * _sigmoid(x)


def _softplus(x):
    return jnp.maximum(x, 0.0) + jnp.log(1.0 + jnp.exp(-jnp.abs(x)))


def _rms(x, gain):
    return x * lax.rsqrt(jnp.mean(x * x, axis=-1, keepdims=True) + NORM_EPS) * gain


def _l2(x):
    return x * lax.rsqrt(jnp.sum(x * x, axis=-1, keepdims=True) + NORM_EPS)


def _modulate(x, gain, shift, scale):
    return _rms(x, gain) * (1.0 + scale) + shift


def _dot(a, b):
    return lax.dot_general(a, b, (((1,), (0,)), ((), ())), precision=HIGHEST, preferred_element_type=F32)


def _bdot(a, b, dims):
    return lax.dot_general(a.astype(BF16), b.astype(BF16), (dims, ((), ())), preferred_element_type=F32)


_NN, _NT, _TN = ((1,), (0,)), ((1,), (1,)), ((0,), (0,))
HIGH = lax.Precision.HIGH


def _dot3(a, b, dims=_NN):
    return lax.dot_general(a, b, (dims, ((), ())), precision=HIGH, preferred_element_type=F32)


@jax.custom_vjp
def _mm_nn(a, b):
    return _bdot(a, b, _NN)


_mm_nn.defvjp(lambda a, b: (_bdot(a, b, _NN), (a, b)),
              lambda res, g: (_bdot(g, res[1], _NT), _bdot(res[0], g, _TN)))


@jax.custom_vjp
def _mm_nt(a, b):
    return _bdot(a, b, _NT)


_mm_nt.defvjp(lambda a, b: (_bdot(a, b, _NT), (a, b)),
              lambda res, g: (_bdot(g, res[1], _NN), _bdot(g, res[0], _TN)))


@jax.custom_vjp
def _tri_inv_saved(a, t_inv):
    return t_inv


_tri_inv_saved.defvjp(lambda a, t_inv: (t_inv, t_inv),
                      lambda t_inv, g: (-_dot3(t_inv, _dot3(g, t_inv, _NT), _TN), jnp.zeros_like(t_inv)))


_MM_DIMS = {"nn": ((1,), (0,)), "nt": ((1,), (1,)), "tn": ((0,), (0,))}


def _matmul(a, b, mode, out_dtype, name, tm=1024, tn=1024, tk=1024, col_slabs=False, epilogue=None, after=None):
    if mode == "nn":
        (M, K), (_, N) = a.shape, b.shape
    elif mode == "nt":
        (M, K), (N, _) = a.shape, b.shape
    else:
        (K, M), (_, N) = a.shape, b.shape
    tm, tn, tk = _pick(M, tm), _pick(N, tn), _pick(K, tk)
    nk = K // tk
    dims = _MM_DIMS[mode]
    epi_fn, extra_in, outs = epilogue if epilogue is not None else (None, [], None)
    if outs is None:
        if col_slabs:
            outs = [((N // tn, M, tn), out_dtype, (None, tm, tn), lambda i, j: (j, i, 0))]
        else:
            outs = [((M, N), out_dtype, (tm, tn), lambda i, j: (i, j))]
    n_in, n_out = len(extra_in), len(outs)

    tied = [] if after is None else [after]

    def body(a_ref, b_ref, *rest):
        rest = rest[len(tied):]
        in_refs, out_refs = rest[:n_in], rest[n_in:n_in + n_out]
        k = pl.program_id(2)
        p = _bdot(a_ref[...], b_ref[...], dims)

        def finish(tile):
            if epi_fn is None:
                out_refs[0][...] = tile.astype(out_dtype)
            else:
                epi_fn(tile, in_refs, out_refs)

        if nk == 1:
            finish(p)
        else:
            acc_ref = rest[-1]

            @pl.when(k == 0)
            def _():
                acc_ref[...] = p

            @pl.when((k > 0) & (k < nk - 1))
            def _():
                acc_ref[...] += p

            @pl.when(k == nk - 1)
            def _():
                finish(acc_ref[...] + p)

    def ij(index_map):
        return lambda i, j, k: index_map(i, j)

    a_spec = pl.BlockSpec((tk, tm), lambda i, j, k: (k, i)) if mode == "tn" else pl.BlockSpec((tm, tk), lambda i, j, k: (i, k))
    b_spec = pl.BlockSpec((tn, tk), lambda i, j, k: (j, k)) if mode == "nt" else pl.BlockSpec((tk, tn), lambda i, j, k: (k, j))
    res = pl.pallas_call(
        body, name=name, grid=(M // tm, N // tn, nk),
        in_specs=[a_spec, b_spec] + [pl.BlockSpec(memory_space=pl.ANY)] * len(tied) + [pl.BlockSpec(blk, ij(im)) for _, blk, im in extra_in],
        out_specs=[pl.BlockSpec(blk, ij(im)) for _, _, blk, im in outs],
        out_shape=[jax.ShapeDtypeStruct(shp, dt) for shp, dt, _, _ in outs],
        scratch_shapes=[pltpu.VMEM((tm, tn), F32)] if nk > 1 else [],
        compiler_params=_cparams(("parallel", "parallel", "arbitrary")),
    )(a, b, *tied, *[arr for arr, _, _ in extra_in])
    return res if epilogue is not None else res[0]


def _row_spec(tr, d):
    return pl.BlockSpec((tr, d), lambda i: (i, 0))


def _vec_spec(d):
    return pl.BlockSpec((1, d), lambda i: (0, 0))


def _pre_fwd(x, gain, shift, scale, name):
    S, D = x.shape
    tr = _pick(S, 256)

    def body(x_ref, g_ref, sh_ref, sc_ref, h_ref):
        h_ref[...] = _modulate(x_ref[...], g_ref[...], sh_ref[...], sc_ref[...]).astype(BF16)

    return pl.pallas_call(
        body, name=name, grid=(S // tr,),
        in_specs=[_row_spec(tr, D), _vec_spec(D), _vec_spec(D), _vec_spec(D)],
        out_specs=_row_spec(tr, D), out_shape=jax.ShapeDtypeStruct((S, D), BF16),
        compiler_params=_cparams(("parallel",)),
    )(x, gain, shift, scale)


def _pre_bwd(x, gain, shift, scale, dh, dx_in, name):
    S, D = x.shape
    tr = _pick(S, 256)

    def body(x_ref, g_ref, sh_ref, sc_ref, dh_ref, dxin_ref, dx_ref, dg_ref, dsh_ref, dsc_ref):
        _, vjp = jax.vjp(_modulate, x_ref[...], g_ref[...], sh_ref[...], sc_ref[...])
        dx, dg, dsh, dsc = vjp(dh_ref[...])
        dx_ref[...] = dxin_ref[...] + dx

        @pl.when(pl.program_id(0) == 0)
        def _():
            dg_ref[...] = jnp.zeros_like(dg_ref)
            dsh_ref[...] = jnp.zeros_like(dsh_ref)
            dsc_ref[...] = jnp.zeros_like(dsc_ref)

        dg_ref[...] += dg
        dsh_ref[...] += dsh
        dsc_ref[...] += dsc

    vec = jax.ShapeDtypeStruct((1, D), F32)
    return pl.pallas_call(
        body, name=name, grid=(S // tr,),
        in_specs=[_row_spec(tr, D), _vec_spec(D), _vec_spec(D), _vec_spec(D), _row_spec(tr, D), _row_spec(tr, D)],
        out_specs=[_row_spec(tr, D), _vec_spec(D), _vec_spec(D), _vec_spec(D)],
        out_shape=[jax.ShapeDtypeStruct((S, D), F32), vec, vec, vec],
        compiler_params=_cparams(("arbitrary",)),
    )(x, gain, shift, scale, dh, dx_in)


def _residual_fwd(x, gate, f, coef, name):
    S, D = x.shape
    tr = _pick(S, 256)

    def body(x_ref, g_ref, f_ref, o_ref):
        o_ref[...] = x_ref[...] + coef * g_ref[...] * f_ref[...]

    return pl.pallas_call(
        body, name=name, grid=(S // tr,),
        in_specs=[_row_spec(tr, D), _vec_spec(D), _row_spec(tr, D)],
        out_specs=_row_spec(tr, D), out_shape=jax.ShapeDtypeStruct((S, D), F32),
        compiler_params=_cparams(("parallel",)),
    )(x, gate, f)


def _residual_bwd(gate, f, dxn, coef, name):
    S, D = f.shape
    tr = _pick(S, 256)

    def body(g_ref, f_ref, d_ref, df_ref, dg_ref):
        d = d_ref[...]
        df_ref[...] = (coef * g_ref[...] * d).astype(BF16)

        @pl.when(pl.program_id(0) == 0)
        def _():
            dg_ref[...] = jnp.zeros_like(dg_ref)

        dg_ref[...] += jnp.sum(coef * f_ref[...] * d, axis=0, keepdims=True)

    return pl.pallas_call(
        body, name=name, grid=(S // tr,),
        in_specs=[_vec_spec(D), _row_spec(tr, D), _row_spec(tr, D)],
        out_specs=[_row_spec(tr, D), _vec_spec(D)],
        out_shape=[jax.ShapeDtypeStruct((S, D), BF16), jax.ShapeDtypeStruct((1, D), F32)],
        compiler_params=_cparams(("arbitrary",)),
    )(gate, f, dxn)


def _swiglu_fn(a, b):
    return _silu(a) * b


def _loss_head(y, target, name):
    S, D = y.shape
    tr = _pick(S, 256)

    def body(y_ref, t_ref, l_ref, dy_ref):
        e = y_ref[...] - t_ref[...]
        dy_ref[...] = e * (1.0 / D)

        @pl.when(pl.program_id(0) == 0)
        def _():
            l_ref[...] = jnp.zeros_like(l_ref)

        l_ref[...] += jnp.sum(jnp.sum(e * e, axis=-1, keepdims=True), axis=0, keepdims=True) * (0.5 / D)

    return pl.pallas_call(
        body, name=name, grid=(S // tr,),
        in_specs=[_row_spec(tr, D), _row_spec(tr, D)],
        out_specs=[pl.BlockSpec((1, 1), lambda i: (0, 0)), _row_spec(tr, D)],
        out_shape=[jax.ShapeDtypeStruct((1, 1), F32), jax.ShapeDtypeStruct((S, D), F32)],
        compiler_params=_cparams(("arbitrary",)),
    )(y, target)


def _rope(y, cc, s1, s2):
    return y * cc + pltpu.roll(y, LANES - ROPE_DIM // 2, 1) * s1 + pltpu.roll(y, ROPE_DIM // 2, 1) * s2


def _rope_t(d, cc, s1, s2):
    return d * cc + pltpu.roll(d * s1, ROPE_DIM // 2, 1) + pltpu.roll(d * s2, LANES - ROPE_DIM // 2, 1)


def _head_spec(tr, col0):
    return pl.BlockSpec((tr, HEAD_DIM), lambda i, h: (i, col0 + h))


def _tab_spec(tr):
    return pl.BlockSpec((tr, HEAD_DIM), lambda i, h: (i, 0))


def _gain_spec():
    return pl.BlockSpec((1, HEAD_DIM), lambda i, h: (0, 0))


def _attn_prep_fwd(proj, q_gain, k_gain, cc, s1, s2, name):
    S = proj.shape[0]
    H = N_ATTN_HEADS
    tr = _pick(S, 512)

    def body(q_ref, k_ref, qg_ref, kg_ref, cc_ref, s1_ref, s2_ref, qo_ref, ko_ref):
        cc, s1, s2 = cc_ref[...], s1_ref[...], s2_ref[...]
        qo_ref[...] = _rope(_rms(q_ref[...], qg_ref[...]), cc, s1, s2).astype(BF16)
        ko_ref[...] = _rope(_rms(k_ref[...], kg_ref[...]), cc, s1, s2).astype(BF16)

    out = jax.ShapeDtypeStruct((S, H * HEAD_DIM), BF16)
    return pl.pallas_call(
        body, name=name, grid=(S // tr, H),
        in_specs=[_head_spec(tr, 0), _head_spec(tr, H), _gain_spec(), _gain_spec(), _tab_spec(tr), _tab_spec(tr), _tab_spec(tr)],
        out_specs=[_head_spec(tr, 0), _head_spec(tr, 0)], out_shape=[out, out],
        compiler_params=_cparams(("parallel", "parallel")),
    )(proj, proj, q_gain, k_gain, cc, s1, s2)


def _attn_prep_bwd(proj, q_gain, k_gain, cc, s1, s2, dq, dk, name):
    S = proj.shape[0]
    H = N_ATTN_HEADS
    tr = _pick(S, 512)

    def body(q_ref, k_ref, qg_ref, kg_ref, cc_ref, s1_ref, s2_ref, dq_ref, dk_ref, dpq_ref, dpk_ref, dqg_ref, dkg_ref):
        cc, s1, s2 = cc_ref[...], s1_ref[...], s2_ref[...]

        @pl.when((pl.program_id(0) == 0) & (pl.program_id(1) == 0))
        def _():
            dqg_ref[...] = jnp.zeros_like(dqg_ref)
            dkg_ref[...] = jnp.zeros_like(dkg_ref)

        _, vjp_q = jax.vjp(_rms, q_ref[...], qg_ref[...])
        dxq, dgq = vjp_q(_rope_t(dq_ref[...], cc, s1, s2))
        _, vjp_k = jax.vjp(_rms, k_ref[...], kg_ref[...])
        dxk, dgk = vjp_k(_rope_t(dk_ref[...], cc, s1, s2))
        dpq_ref[...] = dxq.astype(BF16)
        dpk_ref[...] = dxk.astype(BF16)
        dqg_ref[...] += dgq
        dkg_ref[...] += dgk

    out = jax.ShapeDtypeStruct((S, H * HEAD_DIM), BF16)
    gout = jax.ShapeDtypeStruct((1, HEAD_DIM), F32)
    return pl.pallas_call(
        body, name=name, grid=(S // tr, H),
        in_specs=[_head_spec(tr, 0), _head_spec(tr, H), _gain_spec(), _gain_spec(), _tab_spec(tr), _tab_spec(tr), _tab_spec(tr),
                  _head_spec(tr, 0), _head_spec(tr, 0)],
        out_specs=[_head_spec(tr, 0), _head_spec(tr, 0), _gain_spec(), _gain_spec()], out_shape=[out, out, gout, gout],
        compiler_params=_cparams(("arbitrary", "arbitrary")),
    )(proj, proj, q_gain, k_gain, cc, s1, s2, dq, dk)


def _multiplicity(j, t):
    ti = lax.broadcasted_iota(jnp.int32, (t, t), 0)
    si = lax.broadcasted_iota(jnp.int32, (t, t), 1)
    delta = j * t + ti - si
    cnt = jnp.zeros((t, t), F32)
    for window, dil in DILATED_PATTERNS:
        ok = (delta >= 0) & ((delta & (dil - 1)) == 0) & (delta <= window)
        cnt = cnt + ok.astype(F32)
    return cnt


_NEG = -1e30


def _log_multiplicity_table(t):
    cnt = jnp.stack([_multiplicity(j, t) for j in range(MAX_WINDOW // t + 1)])
    return jnp.where(cnt > 0.0, jnp.log(jnp.maximum(cnt, 1.0)), _NEG)


def _bias_spec(t):
    return pl.BlockSpec((MAX_WINDOW // t + 1, t, t), lambda h, i, j: (0, 0, 0))


def _attn_fwd(q, k, proj, v_col0, bias, name):
    S = q.shape[0]
    H = N_ATTN_HEADS
    t = _pick(S, ATTN_TILE)
    nq = S // t
    nj = MAX_WINDOW // t + 1
    scale = HEAD_DIM ** -0.5

    def body(q_ref, k_ref, v_ref, b_ref, o_ref, ob_ref, lse_ref, m_sc, l_sc, acc_sc):
        qb, j = pl.program_id(1), pl.program_id(2)

        @pl.when(j == 0)
        def _():
            m_sc[...] = jnp.full_like(m_sc, _NEG)
            l_sc[...] = jnp.zeros_like(l_sc)
            acc_sc[...] = jnp.zeros_like(acc_sc)

        @pl.when(qb - j >= 0)
        def _():
            s = _bdot(q_ref[...], k_ref[...], ((1,), (1,))) * scale + b_ref[j]
            m_prev = m_sc[...]
            m_new = jnp.maximum(m_prev, jnp.max(s, axis=-1, keepdims=True))
            alpha = jnp.exp(m_prev - m_new)
            p = jnp.exp(s - m_new)
            l_sc[...] = alpha * l_sc[...] + jnp.sum(p, axis=-1, keepdims=True)
            acc_sc[...] = alpha * acc_sc[...] + _bdot(p, v_ref[...], ((1,), (0,)))
            m_sc[...] = m_new

        @pl.when(j == nj - 1)
        def _():
            o = acc_sc[...] / l_sc[...]
            o_ref[...] = o
            ob_ref[...] = o.astype(BF16)
            lse_ref[...] = jnp.broadcast_to(m_sc[...] + jnp.log(l_sc[...]), (t, HEAD_DIM))

    qspec = pl.BlockSpec((t, HEAD_DIM), lambda h, i, j: (i, h))
    kspec = pl.BlockSpec((t, HEAD_DIM), lambda h, i, j: (jnp.maximum(i - j, 0), h))
    vspec = pl.BlockSpec((t, HEAD_DIM), lambda h, i, j: (jnp.maximum(i - j, 0), v_col0 + h))
    return pl.pallas_call(
        body, name=name, grid=(H, nq, nj),
        in_specs=[qspec, kspec, vspec, _bias_spec(t)], out_specs=[qspec, qspec, qspec],
        out_shape=[jax.ShapeDtypeStruct((S, H * HEAD_DIM), F32), jax.ShapeDtypeStruct((S, H * HEAD_DIM), BF16),
                   jax.ShapeDtypeStruct((S, H * HEAD_DIM), F32)],
        scratch_shapes=[pltpu.VMEM((t, 1), F32), pltpu.VMEM((t, 1), F32), pltpu.VMEM((t, HEAD_DIM), F32)],
        compiler_params=_cparams(("parallel", "parallel", "arbitrary")),
    )(q, k, proj, bias)


def _attn_probs(q, k, lse, bias_tile, scale):
    return jnp.exp(_bdot(q, k, ((1,), (1,))) * scale + bias_tile - lse)


def _attn_bwd_dq(q, k, proj, v_col0, o, lse, do, do_col0, bias, name):
    S = q.shape[0]
    H = N_ATTN_HEADS
    t = _pick(S, ATTN_TILE)
    nq = S // t
    nj = MAX_WINDOW // t + 1
    scale = HEAD_DIM ** -0.5

    def body(q_ref, k_ref, v_ref, o_ref, lse_ref, do_ref, b_ref, dq_ref, acc_sc):
        qb, j = pl.program_id(1), pl.program_id(2)

        @pl.when(j == 0)
        def _():
            acc_sc[...] = jnp.zeros_like(acc_sc)

        @pl.when(qb - j >= 0)
        def _():
            do = do_ref[...]
            dsum = jnp.sum(do * o_ref[...], axis=-1, keepdims=True)
            lse = jnp.max(lse_ref[...], axis=-1, keepdims=True)
            p = _attn_probs(q_ref[...], k_ref[...], lse, b_ref[j], scale)
            dp = _bdot(do, v_ref[...], ((1,), (1,)))
            ds = p * (dp - dsum)
            acc_sc[...] += _bdot(ds, k_ref[...], ((1,), (0,))) * scale

        @pl.when(j == nj - 1)
        def _():
            dq_ref[...] = acc_sc[...]

    qspec = pl.BlockSpec((t, HEAD_DIM), lambda h, i, j: (i, h))
    dospec = pl.BlockSpec((t, HEAD_DIM), lambda h, i, j: (i, do_col0 + h))
    kspec = pl.BlockSpec((t, HEAD_DIM), lambda h, i, j: (jnp.maximum(i - j, 0), h))
    vspec = pl.BlockSpec((t, HEAD_DIM), lambda h, i, j: (jnp.maximum(i - j, 0), v_col0 + h))
    return pl.pallas_call(
        body, name=name, grid=(H, nq, nj),
        in_specs=[qspec, kspec, vspec, qspec, qspec, dospec, _bias_spec(t)], out_specs=qspec,
        out_shape=jax.ShapeDtypeStruct((S, H * HEAD_DIM), F32),
        scratch_shapes=[pltpu.VMEM((t, HEAD_DIM), F32)],
        compiler_params=_cparams(("parallel", "parallel", "arbitrary")),
    )(q, k, proj, o, lse, do, bias)


def _attn_bwd_dkv(q, k, proj, v_col0, o, lse, do, do_col0, bias, name):
    S = q.shape[0]
    H = N_ATTN_HEADS
    t = _pick(S, ATTN_TILE)
    nq = S // t
    nj = MAX_WINDOW // t + 1
    scale = HEAD_DIM ** -0.5

    def body(q_ref, k_ref, v_ref, o_ref, lse_ref, do_ref, b_ref, dk_ref, dv_ref, dk_sc, dv_sc):
        kb, j = pl.program_id(1), pl.program_id(2)

        @pl.when(j == 0)
        def _():
            dk_sc[...] = jnp.zeros_like(dk_sc)
            dv_sc[...] = jnp.zeros_like(dv_sc)

        @pl.when(kb + j < nq)
        def _():
            do = do_ref[...]
            dsum = jnp.sum(do * o_ref[...], axis=-1, keepdims=True)
            lse = jnp.max(lse_ref[...], axis=-1, keepdims=True)
            p = _attn_probs(q_ref[...], k_ref[...], lse, b_ref[j], scale)
            dp = _bdot(do, v_ref[...], ((1,), (1,)))
            ds = p * (dp - dsum)
            dv_sc[...] += _bdot(p, do, ((0,), (0,)))
            dk_sc[...] += _bdot(ds, q_ref[...], ((0,), (0,))) * scale

        @pl.when(j == nj - 1)
        def _():
            dk_ref[...] = dk_sc[...]
            dv_ref[...] = dv_sc[...].astype(BF16)

    def qrow(h, i, j):
        return jnp.minimum(i + j, nq - 1)

    qspec = pl.BlockSpec((t, HEAD_DIM), lambda h, i, j: (qrow(h, i, j), h))
    dospec = pl.BlockSpec((t, HEAD_DIM), lambda h, i, j: (qrow(h, i, j), do_col0 + h))
    kspec = pl.BlockSpec((t, HEAD_DIM), lambda h, i, j: (i, h))
    vspec = pl.BlockSpec((t, HEAD_DIM), lambda h, i, j: (i, v_col0 + h))
    return pl.pallas_call(
        body, name=name, grid=(H, nq, nj),
        in_specs=[qspec, kspec, vspec, qspec, qspec, dospec, _bias_spec(t)], out_specs=[kspec, kspec],
        out_shape=[jax.ShapeDtypeStruct((S, H * HEAD_DIM), F32), jax.ShapeDtypeStruct((S, H * HEAD_DIM), BF16)],
        scratch_shapes=[pltpu.VMEM((t, HEAD_DIM), F32), pltpu.VMEM((t, HEAD_DIM), F32)],
        compiler_params=_cparams(("parallel", "parallel", "arbitrary")),
    )(q, k, proj, o, lse, do, bias)


def _conv_pre(x_ref, w_ref):
    x = x_ref[...]
    rows = lax.broadcasted_iota(jnp.int32, x.shape, 0)
    shifted = [x]
    acc = x * w_ref[pl.ds(CONV_WIDTH - 1, 1), :]
    for sft in range(1, CONV_WIDTH):
        xs = jnp.where(rows >= sft, pltpu.roll(x, sft, 0), 0.0)
        shifted.append(xs)
        acc = acc + xs * w_ref[pl.ds(CONV_WIDTH - 1 - sft, 1), :]
    return acc, shifted


def _conv_fwd(proj, col0, width, w, name):
    S = proj.shape[0]

    def body(x_ref, w_ref, y_ref):
        acc, _ = _conv_pre(x_ref, w_ref)
        y_ref[...] = _silu(acc)

    return pl.pallas_call(
        body, name=name, grid=(width // LANES,),
        in_specs=[pl.BlockSpec((S, LANES), lambda c: (0, col0 + c)), pl.BlockSpec((CONV_WIDTH, LANES), lambda c: (0, c))],
        out_specs=pl.BlockSpec((S, LANES), lambda c: (0, c)),
        out_shape=jax.ShapeDtypeStruct((S, width), F32),
        compiler_params=_cparams(("parallel",)),
    )(proj, w)


def _conv_bwd(proj, col0, width, w, dy, name):
    S = proj.shape[0]

    def body(x_ref, w_ref, d_ref, dx_ref, dw_ref):
        acc, shifted = _conv_pre(x_ref, w_ref)
        sig = _sigmoid(acc)
        da = d_ref[...] * (sig * (1.0 + acc * (1.0 - sig)))
        rows = lax.broadcasted_iota(jnp.int32, da.shape, 0)
        dx = da * w_ref[pl.ds(CONV_WIDTH - 1, 1), :]
        dw_ref[pl.ds(CONV_WIDTH - 1, 1), :] = jnp.sum(da * shifted[0], axis=0, keepdims=True)
        for sft in range(1, CONV_WIDTH):
            back = jnp.where(rows < S - sft, pltpu.roll(da, S - sft, 0), 0.0)
            dx = dx + back * w_ref[pl.ds(CONV_WIDTH - 1 - sft, 1), :]
            dw_ref[pl.ds(CONV_WIDTH - 1 - sft, 1), :] = jnp.sum(da * shifted[sft], axis=0, keepdims=True)
        dx_ref[...] = dx.astype(BF16)

    return pl.pallas_call(
        body, name=name, grid=(width // LANES,),
        in_specs=[pl.BlockSpec((S, LANES), lambda c: (0, col0 + c)), pl.BlockSpec((CONV_WIDTH, LANES), lambda c: (0, c)),
                  pl.BlockSpec((S, LANES), lambda c: (0, c))],
        out_specs=[pl.BlockSpec((S, LANES), lambda c: (0, c)), pl.BlockSpec((CONV_WIDTH, LANES), lambda c: (0, c))],
        out_shape=[jax.ShapeDtypeStruct((S, width), BF16), jax.ShapeDtypeStruct((CONV_WIDTH, width), F32)],
        compiler_params=_cparams(("parallel",)),
    )(proj, w, dy)


PREP_CHUNKS = 8


def _chunks_prep(qraws, kraws, vs, abs_, alog_row, dtb_row, mask_g, mask_b, t_saved=None):
    n = len(qraws)
    c = qraws[0].shape[0]
    mm_nt, mm_nn = (_mm_nt, _mm_nn) if t_saved is not None else (lambda p, r: _bdot(p, r, _NT), lambda p, r: _bdot(p, r, _NN))
    row = lax.broadcasted_iota(jnp.int32, (c, c), 0)
    col = lax.broadcasted_iota(jnp.int32, (c, c), 1)
    tril, strict, eye = row >= col, row > col, row == col
    eyef = eye.astype(F32)
    neg_rate = -jnp.exp(alog_row)
    q, k, beta, gc_col, gamma, kb, g_last = [], [], [], [], [], [], []
    for i in range(n):
        q.append(_l2(qraws[i]) * (HEAD_DIM ** -0.5))
        k.append(_l2(kraws[i]))
        gfull = neg_rate * _softplus(abs_[i] + dtb_row)
        g = jnp.sum(jnp.where(mask_g, gfull, 0.0), axis=-1, keepdims=True)
        beta.append(jnp.sum(jnp.where(mask_b, _sigmoid(abs_[i]), 0.0), axis=-1, keepdims=True))
        g_row = jnp.sum(jnp.where(eye, g, 0.0), axis=0, keepdims=True)
        gc_col.append(jnp.sum(jnp.where(tril, g_row, 0.0), axis=1, keepdims=True))
        gc_row = jnp.sum(jnp.where(row <= col, g, 0.0), axis=0, keepdims=True)
        gamma.append(jnp.where(tril, jnp.exp(jnp.where(tril, gc_col[i] - gc_row, 0.0)), 0.0))
        kb.append(k[i] * beta[i])
        g_last.append(jnp.sum(g, axis=0, keepdims=True))
    a = [jnp.where(strict, mm_nt(kb[i], k[i]) * gamma[i], 0.0) for i in range(n)]
    if t_saved is None:
        t_inv = [eyef - a[i] for i in range(n)]
        p = a
        for _ in range(int(math.log2(c)) - 1):
            p = [_dot3(p[i], p[i]) for i in range(n)]
            t_inv = [_dot3(t_inv[i], eyef + p[i]) for i in range(n)]
    else:
        t_inv = [_tri_inv_saved(a[i], t_saved[i]) for i in range(n)]
    egc = [jnp.exp(gc_col[i]) for i in range(n)]
    u = [mm_nn(t_inv[i], vs[i] * beta[i]) for i in range(n)]
    w = [mm_nn(t_inv[i], kb[i] * egc[i]) for i in range(n)]
    intra = [mm_nt(q[i], k[i]) * gamma[i] for i in range(n)]
    out = []
    for i in range(n):
        kt = k[i] * jnp.exp(g_last[i] - gc_col[i])
        dec = jnp.broadcast_to(jnp.exp(g_last[i]), (1, HEAD_DIM))
        one = (u[i], w[i], q[i] * egc[i], kt, intra[i], dec)
        out.append(one + (t_inv[i],) if t_saved is None else one)
    return out


def _lane_masks(h):
    lane = lax.broadcasted_iota(jnp.int32, (1, LANES), 1)
    return lane == h, lane == N_DELTA_HEADS + h


def _prep_specs(tr, ab_col):
    H = N_DELTA_HEADS
    return [
        pl.BlockSpec((tr, HEAD_DIM), lambda i, h: (i, h)),
        pl.BlockSpec((tr, HEAD_DIM), lambda i, h: (i, H + h)),
        pl.BlockSpec((tr, HEAD_DIM), lambda i, h: (i, 2 * H + h)),
        pl.BlockSpec((tr, LANES), lambda i, h: (i, ab_col)),
        pl.BlockSpec((1, LANES), lambda i, h: (0, 0)),
        pl.BlockSpec((1, LANES), lambda i, h: (0, 0)),
    ]


def _prep_out_specs(tr):
    nc = tr // CHUNK
    hs = pl.BlockSpec((tr, HEAD_DIM), lambda i, h: (i, h))
    return [hs, hs, hs, hs,
            pl.BlockSpec((None, tr, CHUNK), lambda i, h: (h, i, 0)),
            pl.BlockSpec((None, nc, 1, HEAD_DIM), lambda i, h: (h, i, 0, 0)),
            pl.BlockSpec((None, tr, CHUNK), lambda i, h: (h, i, 0))]


def _prep_out_shapes(S):
    H = N_DELTA_HEADS
    hs = jax.ShapeDtypeStruct((S, H * HEAD_DIM), F32)
    sq = jax.ShapeDtypeStruct((H, S, CHUNK), F32)
    return [hs, hs, hs, hs, sq, jax.ShapeDtypeStruct((H, S // CHUNK, 1, HEAD_DIM), F32), sq]


def _delta_prep_fwd(dqkv, proj, ab_col, alog_row, dtb_row, name):
    S = dqkv.shape[0]
    tr = min(S, PREP_CHUNKS * CHUNK)
    nc = tr // CHUNK

    def body(q_ref, k_ref, v_ref, ab_ref, al_ref, dt_ref, u_ref, w_ref, qd_ref, kt_ref, in_ref, dec_ref, ti_ref):
        mask_g, mask_b = _lane_masks(pl.program_id(1))
        rows = [pl.ds(ci * CHUNK, CHUNK) for ci in range(nc)]
        outs = _chunks_prep([q_ref[rs, :] for rs in rows], [k_ref[rs, :] for rs in rows], [v_ref[rs, :] for rs in rows],
                            [ab_ref[rs, :] for rs in rows], al_ref[...], dt_ref[...], mask_g, mask_b)
        for ci, rs in enumerate(rows):
            u, w, qd, kt, intra, dec, t_inv = outs[ci]
            u_ref[rs, :] = u
            w_ref[rs, :] = w
            qd_ref[rs, :] = qd
            kt_ref[rs, :] = kt
            in_ref[rs, :] = intra
            dec_ref[ci] = dec
            ti_ref[rs, :] = t_inv

    return pl.pallas_call(
        body, name=name, grid=(S // tr, N_DELTA_HEADS),
        in_specs=_prep_specs(tr, ab_col), out_specs=_prep_out_specs(tr), out_shape=_prep_out_shapes(S),
        compiler_params=_cparams(("parallel", "parallel")),
    )(dqkv, dqkv, dqkv, proj, alog_row, dtb_row)


def _delta_prep_bwd(dqkv, proj, ab_col, alog_row, dtb_row, cots, t_inv, name):
    S = dqkv.shape[0]
    H = N_DELTA_HEADS
    tr = min(S, PREP_CHUNKS * CHUNK)
    nc = tr // CHUNK

    def body(q_ref, k_ref, v_ref, ab_ref, al_ref, dt_ref, du_ref, dw_ref, dqd_ref, dkt_ref, din_ref, ddec_ref, ti_ref,
             dq_ref, dk_ref, dv_ref, dab_ref, dal_ref, ddt_ref, dab_sc):
        h = pl.program_id(1)
        mask_g, mask_b = _lane_masks(h)

        @pl.when((pl.program_id(0) == 0) & (h == 0))
        def _():
            dal_ref[...] = jnp.zeros_like(dal_ref)
            ddt_ref[...] = jnp.zeros_like(ddt_ref)

        @pl.when(h == 0)
        def _():
            dab_sc[...] = jnp.zeros_like(dab_sc)

        rows = [pl.ds(ci * CHUNK, CHUNK) for ci in range(nc)]
        fn = functools.partial(_chunks_prep, mask_g=mask_g, mask_b=mask_b, t_saved=[ti_ref[rs, :] for rs in rows])
        _, vjp = jax.vjp(fn, [q_ref[rs, :] for rs in rows], [k_ref[rs, :] for rs in rows], [v_ref[rs, :] for rs in rows],
                         [ab_ref[rs, :] for rs in rows], al_ref[...], dt_ref[...])
        dqs, dks, dvs, dabs, dal, ddt = vjp([(du_ref[rs, :], dw_ref[rs, :], dqd_ref[rs, :], dkt_ref[rs, :], din_ref[rs, :],
                                              ddec_ref[ci]) for ci, rs in enumerate(rows)])
        for ci, rs in enumerate(rows):
            dq_ref[rs, :] = dqs[ci]
            dk_ref[rs, :] = dks[ci]
            dv_ref[rs, :] = dvs[ci]
            dab_sc[rs, :] += dabs[ci]
        dal_ref[...] += dal
        ddt_ref[...] += ddt

        @pl.when(h == H - 1)
        def _():
            dab_ref[...] = dab_sc[...].astype(BF16)

    hs = pl.BlockSpec((tr, HEAD_DIM), lambda i, h: (i, h))
    hshape = jax.ShapeDtypeStruct((S, H * HEAD_DIM), F32)
    row = pl.BlockSpec((1, LANES), lambda i, h: (0, 0))
    rshape = jax.ShapeDtypeStruct((1, LANES), F32)
    return pl.pallas_call(
        body, name=name, grid=(S // tr, H),
        in_specs=_prep_specs(tr, ab_col) + _prep_out_specs(tr),
        out_specs=[hs, hs, hs, pl.BlockSpec((tr, LANES), lambda i, h: (i, 0)), row, row],
        out_shape=[hshape, hshape, hshape, jax.ShapeDtypeStruct((S, LANES), BF16), rshape, rshape],
        scratch_shapes=[pltpu.VMEM((tr, LANES), F32)],
        compiler_params=_cparams(("arbitrary", "arbitrary")),
    )(dqkv, dqkv, dqkv, proj, alog_row, dtb_row, *cots, t_inv)


def _scan_steps(states, us, ws, qds, kts, intras, decs):
    hs = range(len(states))
    v_new = [us[h] - _dot3(ws[h], states[h]) for h in hs]
    o_state = [_dot3(qds[h], states[h]) for h in hs]
    o_intra = [_dot3(intras[h], v_new[h]) for h in hs]
    grown = [_dot3(kts[h], v_new[h], _TN) for h in hs]
    return [o_state[h] + o_intra[h] for h in hs], [states[h] * decs[h] + grown[h] for h in hs]


def _scan_specs(rev, n):
    H = N_DELTA_HEADS

    def cix(i):
        return (n - 1 - i) if rev else i

    row = pl.BlockSpec((CHUNK, H * HEAD_DIM), lambda i: (cix(i), 0))
    return row, pl.BlockSpec((H, CHUNK, CHUNK), lambda i: (0, cix(i), 0)), \
        pl.BlockSpec((H, 1, 1, HEAD_DIM), lambda i: (0, cix(i), 0, 0)), \
        pl.BlockSpec((1, H, HEAD_DIM, HEAD_DIM), lambda i: (cix(i), 0, 0, 0))


def _delta_scan_fwd(u, w, qd, kt, intra, dec, name):
    S = u.shape[0]
    H = N_DELTA_HEADS
    n = S // CHUNK
    row, ispec, dspec, sspec = _scan_specs(False, n)

    def body(u_ref, w_ref, qd_ref, kt_ref, in_ref, dec_ref, o_ref, st_ref, s_sc):
        @pl.when(pl.program_id(0) == 0)
        def _():
            s_sc[...] = jnp.zeros_like(s_sc)

        cols = [pl.ds(h * HEAD_DIM, HEAD_DIM) for h in range(H)]
        states = [s_sc[h] for h in range(H)]
        outs, new = _scan_steps(states, [u_ref[:, cs] for cs in cols], [w_ref[:, cs] for cs in cols],
                                [qd_ref[:, cs] for cs in cols], [kt_ref[:, cs] for cs in cols],
                                [in_ref[h] for h in range(H)], [dec_ref[h, 0] for h in range(H)])
        for h, cs in enumerate(cols):
            st_ref[0, h] = states[h]
            o_ref[:, cs] = outs[h]
            s_sc[h] = new[h]

    return pl.pallas_call(
        body, name=name, grid=(n,),
        in_specs=[row, row, row, row, ispec, dspec], out_specs=[row, sspec],
        out_shape=[jax.ShapeDtypeStruct((S, H * HEAD_DIM), F32), jax.ShapeDtypeStruct((n, H, HEAD_DIM, HEAD_DIM), F32)],
        scratch_shapes=[pltpu.VMEM((H, HEAD_DIM, HEAD_DIM), F32)],
        compiler_params=_cparams(("arbitrary",)),
    )(u, w, qd, kt, intra, dec)


def _delta_scan_bwd(u, w, qd, kt, intra, dec, states, do, name):
    S = u.shape[0]
    H = N_DELTA_HEADS
    n = S // CHUNK
    row, ispec, dspec, sspec = _scan_specs(True, n)

    def body(u_ref, w_ref, qd_ref, kt_ref, in_ref, dec_ref, st_ref, do_ref,
             du_ref, dw_ref, dqd_ref, dkt_ref, din_ref, ddec_ref, ds_sc):
        @pl.when(pl.program_id(0) == 0)
        def _():
            ds_sc[...] = jnp.zeros_like(ds_sc)

        cols = [pl.ds(h * HEAD_DIM, HEAD_DIM) for h in range(H)]
        _, vjp = jax.vjp(_scan_steps, [st_ref[0, h] for h in range(H)], [u_ref[:, cs] for cs in cols],
                         [w_ref[:, cs] for cs in cols], [qd_ref[:, cs] for cs in cols], [kt_ref[:, cs] for cs in cols],
                         [in_ref[h] for h in range(H)], [dec_ref[h, 0] for h in range(H)])
        dstate, du, dw, dqd, dkt, din, ddec = vjp(([do_ref[:, cs] for cs in cols], [ds_sc[h] for h in range(H)]))
        for h, cs in enumerate(cols):
            du_ref[:, cs] = du[h]
            dw_ref[:, cs] = dw[h]
            dqd_ref[:, cs] = dqd[h]
            dkt_ref[:, cs] = dkt[h]
            din_ref[h] = din[h]
            ddec_ref[h, 0] = ddec[h]
            ds_sc[h] = dstate[h]

    hshape = jax.ShapeDtypeStruct((S, H * HEAD_DIM), F32)
    return pl.pallas_call(
        body, name=name, grid=(n,),
        in_specs=[row, row, row, row, ispec, dspec, sspec, row],
        out_specs=[row, row, row, row, ispec, dspec],
        out_shape=[hshape, hshape, hshape, hshape, jax.ShapeDtypeStruct((H, S, CHUNK), F32),
                   jax.ShapeDtypeStruct((H, n, 1, HEAD_DIM), F32)],
        scratch_shapes=[pltpu.VMEM((H, HEAD_DIM, HEAD_DIM), F32)],
        compiler_params=_cparams(("arbitrary",)),
    )(u, w, qd, kt, intra, dec, states, do)


def _gated_norm(od, z, gain):
    return _rms(od, gain) * _silu(z)


def _post_fwd(od, proj, z_col0, gain, name):
    S = od.shape[0]
    H = N_DELTA_HEADS
    tr = _pick(S, 512)

    def body(od_ref, z_ref, g_ref, o_ref):
        o_ref[...] = _gated_norm(od_ref[...], z_ref[...], g_ref[...]).astype(BF16)

    return pl.pallas_call(
        body, name=name, grid=(S // tr, H),
        in_specs=[_head_spec(tr, 0), _head_spec(tr, z_col0), _gain_spec()],
        out_specs=_head_spec(tr, 0), out_shape=jax.ShapeDtypeStruct((S, H * HEAD_DIM), BF16),
        compiler_params=_cparams(("parallel", "parallel")),
    )(od, proj, gain)


def _post_bwd(od, proj, z_col0, gain, do, do_col0, name):
    S = od.shape[0]
    H = N_DELTA_HEADS
    tr = _pick(S, 512)

    def body(od_ref, z_ref, g_ref, do_ref, dod_ref, dz_ref, dg_ref):
        @pl.when((pl.program_id(0) == 0) & (pl.program_id(1) == 0))
        def _():
            dg_ref[...] = jnp.zeros_like(dg_ref)

        _, vjp = jax.vjp(_gated_norm, od_ref[...], z_ref[...], g_ref[...])
        dod, dz, dg = vjp(do_ref[...])
        dod_ref[...] = dod
        dz_ref[...] = dz.astype(BF16)
        dg_ref[...] += dg

    return pl.pallas_call(
        body, name=name, grid=(S // tr, H),
        in_specs=[_head_spec(tr, 0), _head_spec(tr, z_col0), _gain_spec(), _head_spec(tr, do_col0)],
        out_specs=[_head_spec(tr, 0), _head_spec(tr, 0), _gain_spec()],
        out_shape=[jax.ShapeDtypeStruct((S, H * HEAD_DIM), F32), jax.ShapeDtypeStruct((S, H * HEAD_DIM), BF16),
                   jax.ShapeDtypeStruct((1, HEAD_DIM), F32)],
        compiler_params=_cparams(("arbitrary", "arbitrary")),
    )(od, proj, gain, do)


def _adam_math(w, g, m, v):
    m = ADAM_B1 * m + (1.0 - ADAM_B1) * g
    v = ADAM_B2 * v + (1.0 - ADAM_B2) * (g * g)
    m_hat = m / (1.0 - ADAM_B1 ** ADAM_STEP)
    v_hat = v / (1.0 - ADAM_B2 ** ADAM_STEP)
    delta = -ADAM_LR * (m_hat / (jnp.sqrt(v_hat) + ADAM_EPS) + ADAM_WD * w)
    return delta, m, v


def _adamw(w, g, m, v, name):
    R, C = w.shape
    tr = R if R * C * 4 <= (1 << 20) else _pick8(R, max(8, (1 << 20) // (C * 4)))

    def body(w_ref, g_ref, m_ref, v_ref, d_ref, nm_ref, nv_ref):
        d, nm, nv = _adam_math(w_ref[...], g_ref[...], m_ref[...], v_ref[...])
        d_ref[...] = d
        nm_ref[...] = nm
        nv_ref[...] = nv

    spec = pl.BlockSpec((tr, C), lambda i: (i, 0))
    shp = jax.ShapeDtypeStruct((R, C), F32)
    return pl.pallas_call(
        body, name=name, grid=(R // tr,), in_specs=[spec] * 4, out_specs=[spec] * 3, out_shape=[shp] * 3,
        compiler_params=_cparams(("parallel",)),
    )(w, g, m, v)


def _adamw_halves(w, mine, theirs, col, m, v, name):
    R, C = w.shape
    tr = _pick8(R // 2, max(8, (1 << 20) // (C * 4)))
    nb2 = (R // 2) // tr

    def body(w_ref, a_ref, b_ref, m_ref, v_ref, g_ref, d_ref, nm_ref, nv_ref):
        top = pl.program_id(0) < nb2
        g = jnp.where(top == (lax.axis_index("c") == 0), a_ref[...], b_ref[...])
        d, nm, nv = _adam_math(w_ref[...], g, m_ref[...], v_ref[...])
        g_ref[...] = g
        d_ref[...] = d
        nm_ref[...] = nm
        nv_ref[...] = nv

    spec = pl.BlockSpec((tr, C), lambda i: (i, 0))
    half = pl.BlockSpec((tr, C), lambda i: (i % nb2, col))
    shp = jax.ShapeDtypeStruct((R, C), F32)
    return pl.pallas_call(
        body, name=name, grid=(2 * nb2,), in_specs=[spec, half, half, spec, spec], out_specs=[spec] * 4, out_shape=[shp] * 4,
        compiler_params=_cparams(("parallel",)),
    )(w, mine, theirs, m, v)


def _pick8(dim, pref):
    t = (min(dim, pref) // 8) * 8
    while t >= 8:
        if dim % t == 0:
            return t
        t -= 8
    return dim


def _adamw_outer(w, m, v, cond_t, rhs, name):
    R, C = w.shape
    tr = _pick8(R, 128)
    nb = cond_t.shape[1]
    lhs_t = cond_t

    def body(w_ref, m_ref, v_ref, a_ref, b_ref, g_ref, d_ref, nm_ref, nv_ref):
        g = _dot(_silu(a_ref[...]), b_ref[...])
        d, nm, nv = _adam_math(w_ref[...], g, m_ref[...], v_ref[...])
        g_ref[...] = g
        d_ref[...] = d
        nm_ref[...] = nm
        nv_ref[...] = nv

    spec = pl.BlockSpec((tr, C), lambda i: (i, 0))
    shp = jax.ShapeDtypeStruct((R, C), F32)
    return pl.pallas_call(
        body, name=name, grid=(R // tr,),
        in_specs=[spec, spec, spec, pl.BlockSpec((tr, nb), lambda i: (i, 0)), pl.BlockSpec((nb, C), lambda i: (0, 0))],
        out_specs=[spec] * 4, out_shape=[shp] * 4,
        compiler_params=_cparams(("parallel",)),
    )(w, m, v, lhs_t, rhs)


def _ada_fwd(cond, w, bias, name):
    a = cond
    nb, K = a.shape
    N = w.shape[1]
    tn = _pick(N, 512)

    def body(a_ref, w_ref, b_ref, o_ref):
        o_ref[...] = _dot(_silu(a_ref[...]), w_ref[...]) + b_ref[...]

    return pl.pallas_call(
        body, name=name, grid=(N // tn,),
        in_specs=[pl.BlockSpec((nb, K), lambda j: (0, 0)), pl.BlockSpec((K, tn), lambda j: (0, j)), pl.BlockSpec((1, tn), lambda j: (0, j))],
        out_specs=pl.BlockSpec((nb, tn), lambda j: (0, j)), out_shape=jax.ShapeDtypeStruct((nb, N), F32),
        compiler_params=_cparams(("parallel",)),
    )(a, w, bias)


def _add_cast(parts, out_dtypes, name):
    shape = parts[0].shape
    G, R, C = shape
    tr = _pick8(R, max(8, (1 << 20) // (C * 4)))
    n_in = len(parts)

    def body(*refs):
        acc = refs[0][...].astype(F32)
        for r in refs[1:n_in]:
            acc = acc + r[...].astype(F32)
        for o, dt in zip(refs[n_in:], out_dtypes):
            o[...] = acc.astype(dt)

    spec = pl.BlockSpec((1, tr, C), lambda g, i: (g, i, 0))
    outs = pl.pallas_call(
        body, name=name, grid=(G, R // tr), in_specs=[spec] * n_in, out_specs=[spec] * len(out_dtypes),
        out_shape=[jax.ShapeDtypeStruct(shape, dt) for dt in out_dtypes],
        compiler_params=_cparams(("parallel", "parallel")),
    )(*parts)
    return outs


def _me():
    return lax.axis_index("x"), lax.axis_index("y"), lax.axis_index("c")


def _xor_peer(k):
    x, y, c = _me()
    dx, dy, dc = (k >> 2) & 1, (k >> 1) & 1, k & 1
    return (x ^ dx if dx else x, y ^ dy if dy else y, c ^ dc if dc else c)


ANY = pl.BlockSpec(memory_space=pl.ANY)


def _all_gather_small(v, name, after=None):
    R, C = v.shape
    extra = [] if after is None else [after]

    def body(v_ref, *rest):
        out_ref, send_sems, recv_sems = rest[len(extra):]
        x, y, c = _me()
        mine = 4 * x + 2 * y + c
        out_ref[mine] = v_ref[...]
        copies = []
        for k in range(1, 8):
            cp = pltpu.make_async_remote_copy(src_ref=v_ref, dst_ref=out_ref.at[mine], send_sem=send_sems.at[k - 1],
                                              recv_sem=recv_sems.at[k - 1], device_id=_xor_peer(k), device_id_type=MESH)
            cp.start()
            copies.append(cp)
        for k in range(1, 8):
            px, py, pc = _xor_peer(k)
            pltpu.make_async_remote_copy(src_ref=v_ref, dst_ref=out_ref.at[4 * px + 2 * py + pc], send_sem=send_sems.at[k - 1],
                                         recv_sem=recv_sems.at[k - 1], device_id=_xor_peer(k), device_id_type=MESH).wait_recv()
        for cp in copies:
            cp.wait_send()

    return pl.pallas_call(
        body, name=name, out_shape=jax.ShapeDtypeStruct((8, R, C), F32),
        in_specs=[pl.BlockSpec(memory_space=pltpu.VMEM)] + [ANY] * len(extra), out_specs=pl.BlockSpec(memory_space=pltpu.VMEM),
        scratch_shapes=[pltpu.SemaphoreType.DMA((7,)), pltpu.SemaphoreType.DMA((7,))],
        compiler_params=pltpu.CompilerParams(vmem_limit_bytes=VMEM_LIMIT),
    )(v, *extra)


def _chip_peers():
    x, y, _ = _me()
    return [(1, (x, 1 - y)), (2, (1 - x, y)), (3, (1 - x, 1 - y))]


def _all_gather_shards(shards, name):
    n = len(shards)

    def body(*refs):
        ins, outs = refs[:n], refs[n:2 * n]
        send_sems, recv_sems = refs[2 * n:]
        x, y, c = _me()
        chip = 2 * x + y
        sib = (x, y, 1 - c)
        peers = _chip_peers()
        sends = []
        for t in range(n):
            half = ins[t].shape[0] // 2
            mine = pl.ds(c * half, half)
            for p, (k, (px, py)) in enumerate(peers):
                cp = pltpu.make_async_remote_copy(src_ref=ins[t].at[mine], dst_ref=outs[t].at[chip, mine],
                                                  send_sem=send_sems.at[6 * t + p], recv_sem=recv_sems.at[6 * t + p],
                                                  device_id=(px, py, c), device_id_type=MESH)
                cp.start()
                sends.append(cp)
        for t in range(n):
            half = ins[t].shape[0] // 2
            mine = pl.ds(c * half, half)
            for p, (k, (px, py)) in enumerate(peers):
                src_chip = 2 * px + py
                landed = outs[t].at[src_chip, mine]
                pltpu.make_async_remote_copy(src_ref=landed, dst_ref=landed, send_sem=send_sems.at[6 * t + p],
                                             recv_sem=recv_sems.at[6 * t + p], device_id=(px, py, c), device_id_type=MESH).wait_recv()
                fw = pltpu.make_async_remote_copy(src_ref=landed, dst_ref=landed, send_sem=send_sems.at[6 * t + 3 + p],
                                                  recv_sem=recv_sems.at[6 * t + 3 + p], device_id=sib, device_id_type=MESH)
                fw.start()
                sends.append(fw)
        for t in range(n):
            half = ins[t].shape[0] // 2
            theirs = pl.ds((1 - c) * half, half)
            for p, (k, (px, py)) in enumerate(peers):
                got = outs[t].at[2 * px + py, theirs]
                pltpu.make_async_remote_copy(src_ref=got, dst_ref=got, send_sem=send_sems.at[6 * t + 3 + p],
                                             recv_sem=recv_sems.at[6 * t + 3 + p], device_id=sib, device_id_type=MESH).wait_recv()
        for cp in sends:
            cp.wait_send()

    return pl.pallas_call(
        body, name=name,
        out_shape=[jax.ShapeDtypeStruct((4,) + s.shape, s.dtype) for s in shards],
        in_specs=[ANY] * n, out_specs=[ANY] * n,
        scratch_shapes=[pltpu.SemaphoreType.DMA((6 * n,)), pltpu.SemaphoreType.DMA((6 * n,))],
    )(*shards)


def _swap_halves_with_sibling(slabs, name):
    n = len(slabs)

    def body(*refs):
        ins, outs = refs[:n], refs[n:2 * n]
        send_sems, recv_sems = refs[2 * n:]
        x, y, c = _me()
        sib = (x, y, 1 - c)
        cps = []
        for t in range(n):
            half = ins[t].shape[1] // 2
            cp = pltpu.make_async_remote_copy(src_ref=ins[t].at[:, pl.ds((1 - c) * half, half)], dst_ref=outs[t],
                                              send_sem=send_sems.at[t], recv_sem=recv_sems.at[t], device_id=sib, device_id_type=MESH)
            cp.start()
            cps.append(cp)
        for cp in cps:
            cp.wait()

    return pl.pallas_call(
        body, name=name,
        out_shape=[jax.ShapeDtypeStruct((4, s.shape[1] // 2, s.shape[2]), s.dtype) for s in slabs],
        in_specs=[ANY] * n, out_specs=[ANY] * n,
        scratch_shapes=[pltpu.SemaphoreType.DMA((n,)), pltpu.SemaphoreType.DMA((n,))],
    )(*slabs)


def _send_halves_to_sibling(halves, name):
    n = len(halves)

    def body(*refs):
        ins, outs = refs[:n], refs[n:2 * n]
        send_sems, recv_sems = refs[2 * n:]
        x, y, c = _me()
        sib = (x, y, 1 - c)
        cps = []
        for t in range(n):
            cp = pltpu.make_async_remote_copy(src_ref=ins[t], dst_ref=outs[t], send_sem=send_sems.at[t],
                                              recv_sem=recv_sems.at[t], device_id=sib, device_id_type=MESH)
            cp.start()
            cps.append(cp)
        for cp in cps:
            cp.wait()

    return pl.pallas_call(
        body, name=name,
        out_shape=[jax.ShapeDtypeStruct(s.shape, s.dtype) for s in halves],
        in_specs=[ANY] * n, out_specs=[ANY] * n,
        scratch_shapes=[pltpu.SemaphoreType.DMA((n,)), pltpu.SemaphoreType.DMA((n,))],
    )(*halves)


HBM_SPEC = pl.BlockSpec(memory_space=pltpu.HBM)
SEM_SPEC = pl.BlockSpec(memory_space=pltpu.SEMAPHORE)
DATAFLOW = pltpu.SideEffectType.DATAFLOW_SIDE_EFFECTING


def _plan_gather_direct(src_refs, land_refs):
    x, y, c = _me()
    chip = 2 * x + y
    plan = []
    for s, land in zip(src_refs, land_refs):
        half = s.shape[0] // 2
        for _, (px, py) in _chip_peers():
            for pc in (c, 1 - c):
                plan.append((s.at[pl.ds(c * half, half)], land.at[chip, pl.ds(c * half, half)],
                             land.at[2 * px + py, pl.ds(pc * half, half)], (px, py, pc)))
    return plan


def _plan_scatter_direct(src_refs, land_refs):
    x, y, c = _me()
    plan = []
    for s, land in zip(src_refs, land_refs):
        half = s.shape[1] // 2
        for k in range(1, 8):
            px, py, pc = _xor_peer(k)
            plan.append((s.at[2 * px + py, pl.ds(pc * half, half)], land.at[4 * x + 2 * y + c],
                         land.at[4 * px + 2 * py + pc], (px, py, pc)))
    return plan


def _plan_exchange_chips(src_refs, land_refs):
    x, y, c = _me()
    chip = 2 * x + y
    plan = []
    for s, land in zip(src_refs, land_refs):
        for _, (px, py) in _chip_peers():
            plan.append((s.at[2 * px + py], land.at[chip], land.at[2 * px + py], (px, py, c)))
    return plan


_plan_gather_direct.per_tensor = 6
_plan_scatter_direct.per_tensor = 7
_plan_exchange_chips.per_tensor = 3


def _start_copies(srcs, lands, plan_fn, name, after=None):
    n = len(srcs)
    n_copies = len(srcs) * plan_fn.per_tensor
    extra = [] if after is None else [after]

    def body(*refs):
        refs = refs[:2 * n] + refs[2 * n + len(extra):]
        send_sems, recv_sems, token = refs[2 * n], refs[2 * n + 1], refs[-1]
        for i, (src, dst, _, peer) in enumerate(plan_fn(refs[:n], refs[n:2 * n])):
            pltpu.make_async_remote_copy(src_ref=src, dst_ref=dst, send_sem=send_sems.at[i], recv_sem=recv_sems.at[i],
                                         device_id=peer, device_id_type=MESH).start()
        token[...] = jnp.zeros_like(token)

    arrays = list(srcs) + list(lands)
    outs = pl.pallas_call(
        body, name=name,
        out_shape=(pltpu.SemaphoreType.DMA((n_copies,)), pltpu.SemaphoreType.DMA((n_copies,)),
                   *[pltpu.HBM(a.shape, a.dtype) for a in arrays], jax.ShapeDtypeStruct((8, LANES), F32)),
        in_specs=[HBM_SPEC] * (2 * n) + [ANY] * len(extra),
        out_specs=(SEM_SPEC, SEM_SPEC, *[HBM_SPEC] * (2 * n), pl.BlockSpec(memory_space=pltpu.VMEM)),
        input_output_aliases={i: 2 + i for i in range(2 * n)},
        compiler_params=pltpu.CompilerParams(has_side_effects=DATAFLOW),
    )(*[pltpu.with_memory_space_constraint(a, pltpu.HBM) for a in arrays], *extra)
    return outs[0], outs[1], list(outs[2:2 + n]), list(outs[2 + n:2 + 2 * n]), outs[-1]


def _wait_copies(send_sems, recv_sems, srcs, lands, after, plan_fn, name):
    n = len(srcs)

    def body(*refs):
        send_sems, recv_sems = refs[2 * n], refs[2 * n + 1]
        for i, (src, _, arrival, peer) in enumerate(plan_fn(refs[:n], refs[n:2 * n])):
            cp = pltpu.make_async_remote_copy(src_ref=src, dst_ref=arrival, send_sem=send_sems.at[i], recv_sem=recv_sems.at[i],
                                              device_id=peer, device_id_type=MESH)
            cp.wait_send()
            cp.wait_recv()

    arrays = list(srcs) + list(lands)
    outs = pl.pallas_call(
        body, name=name,
        out_shape=tuple(pltpu.HBM(a.shape, a.dtype) for a in arrays),
        in_specs=[HBM_SPEC] * (2 * n) + [SEM_SPEC, SEM_SPEC, ANY],
        out_specs=tuple([HBM_SPEC] * (2 * n)),
        input_output_aliases={i: i for i in range(2 * n)},
        compiler_params=pltpu.CompilerParams(has_side_effects=DATAFLOW),
    )(*arrays, send_sems, recv_sems, after)
    return list(outs[n:])


def _rope_tables(positions):
    half = ROPE_DIM // 2
    S = positions.shape[0]
    inv_freq = ROPE_THETA ** (-jnp.arange(half, dtype=F32) / half)
    ang = positions.astype(F32)[:, None] * inv_freq
    cos, sin = jnp.cos(ang), jnp.sin(ang)
    zeros = functools.partial(jnp.zeros, dtype=F32)
    cc = jnp.concatenate([cos, cos, jnp.ones((S, HEAD_DIM - ROPE_DIM), F32)], axis=1)
    s1 = jnp.concatenate([-sin, zeros((S, HEAD_DIM - half))], axis=1)
    s2 = jnp.concatenate([zeros((S, half)), sin, zeros((S, HEAD_DIM - ROPE_DIM))], axis=1)
    return cc, s1, s2


def _ffn_fwd(x, gain, shift, scale, gate, w_gu, w_d, fs, tag):
    S = x.shape[0]
    tm = _pick(S, 512)
    h = _pre_fwd(x, gain, shift, scale, tag + "_pre")

    def swiglu_out(tile, ins, outs):
        outs[0][...] = tile
        outs[1][...] = _swiglu_fn(tile[:, :fs], tile[:, fs:]).astype(BF16)

    ab, s = _matmul(h, w_gu, "nn", F32, tag + "_gate_up", tm=tm, tn=2 * fs, tk=4096, epilogue=(swiglu_out, [], [
        ((S, 8 * fs), F32, (tm, 2 * fs), lambda i, j: (i, j)), ((S, 4 * fs), BF16, (tm, fs), lambda i, j: (i, j))]))
    f = _matmul(s, w_d, "nn", F32, tag + "_down", tm=512, tn=1024, tk=8192)
    xn = _residual_fwd(x, gate, f, 0.5, tag + "_res")
    return xn, (x, h, ab, s, f)


def _ffn_bwd(dxn, saved, gain, shift, scale, gate, w_gu, w_d, fs, tag, on_dw_d=None, on_dw_gu=None):
    x, h, ab, s, f = saved
    df, dgate = _residual_bwd(gate, f, dxn, 0.5, tag + "_res_bwd")
    S = x.shape[0]
    tm = _pick(S, 512)
    dw_d = _matmul(s, df, "tn", F32, tag + "_down_dw", tm=1408, tn=1024, tk=2048)
    if on_dw_d is not None:
        shift = shift + on_dw_d(dw_d)

    def swiglu_back(tile, ins, outs):
        _, vjp = jax.vjp(_swiglu_fn, ins[0][:, :fs], ins[0][:, fs:])
        da, db = vjp(tile)
        outs[0][:, :fs] = da.astype(BF16)
        outs[0][:, fs:] = db.astype(BF16)

    (dab,) = _matmul(df, w_d, "nt", F32, tag + "_down_dx", tm=tm, tn=fs, tk=4096, epilogue=(
        swiglu_back, [(ab, (tm, 2 * fs), lambda i, j: (i, j))], [((S, 8 * fs), BF16, (tm, 2 * fs), lambda i, j: (i, j))]))
    dw_gu = _matmul(h, dab, "tn", F32, tag + "_gate_up_dw", tm=512, tn=2 * fs, tk=2048, col_slabs=True)
    tie = on_dw_gu(dw_gu) if on_dw_gu is not None else None
    dh = _matmul(dab, w_gu, "nt", F32, tag + "_gate_up_dx", tm=1024, tn=1024, tk=2816, after=tie)
    dx, dgain, dshift, dscale = _pre_bwd(x, gain, shift, scale, dh, dxn, tag + "_pre_bwd")
    return dx, dw_gu, dw_d, dgain, dshift, dscale, dgate


def _flat_pad(parts, rows, cols):
    flat = jnp.concatenate([p.reshape(-1).astype(F32) for p in parts])
    return jnp.pad(flat, (0, rows * cols - flat.shape[0])).reshape(rows, cols)


def _cols_to_slabs(w, n):
    R, NC = w.shape
    return jnp.transpose(w.reshape(R, n, NC // n), (1, 0, 2))


def kernel(x, c, positions, w_ada, b_ada, ffn1_norm, ffn1_w_gate, ffn1_w_up, ffn1_w_down, mix_norm, w_in, conv_w, q_norm, k_norm, a_log, dt_bias, delta_out_norm, w_out, ffn2_norm, ffn2_w_gate, ffn2_w_up, ffn2_w_down, loss_target, m_w_ada, m_b_ada, m_ffn1_norm, m_ffn1_w_gate, m_ffn1_w_up, m_ffn1_w_down, m_mix_norm, m_w_in, m_conv_w, m_q_norm, m_k_norm, m_a_log, m_dt_bias, m_delta_out_norm, m_w_out, m_ffn2_norm, m_ffn2_w_gate, m_ffn2_w_up, m_ffn2_w_down, v_w_ada, v_b_ada, v_ffn1_norm, v_ffn1_w_gate, v_ffn1_w_up, v_ffn1_w_down, v_mix_norm, v_w_in, v_conv_w, v_q_norm, v_k_norm, v_a_log, v_dt_bias, v_delta_out_norm, v_w_out, v_ffn2_norm, v_ffn2_w_gate, v_ffn2_w_up, v_ffn2_w_down):
    xi, yi, ci = _me()
    chip = 2 * xi + yi
    dev = 2 * chip + ci
    xs = x[0]
    S, D = xs.shape
    HA, HD = N_ATTN_HEADS, N_DELTA_HEADS
    fs = ffn1_w_gate.shape[2]
    n_mod_shard = w_ada.shape[2]
    in_shard = w_in.shape[2]
    in_width = 4 * in_shard
    in_pad = -(-in_width // LANES) * LANES
    conv_shard = conv_w.shape[2]
    conv_width = 4 * conv_shard

    pack0 = jnp.zeros((8, max(D, conv_shard)), F32)
    pack0 = pack0.at[0, :D].set(c[0]).at[1:1 + CONV_WIDTH, :conv_shard].set(conv_w[0])
    got0 = _all_gather_small(pack0, "gather_cond")
    c_all = got0[:, 0, :D]
    conv_full = jnp.transpose(got0[::2, 1:1 + CONV_WIDTH, :conv_shard], (1, 0, 2)).reshape(CONV_WIDTH, conv_width)
    b_ada_mine = lax.dynamic_slice(b_ada, (0, chip * n_mod_shard), (1, n_mod_shard))
    mod_part = _ada_fwd(c_all, w_ada[0], b_ada_mine, "ada_fwd")
    got1 = _all_gather_small(mod_part, "gather_mod")
    mod = lax.dynamic_index_in_dim(got1[::2], dev, axis=1, keepdims=False).reshape(1, 4 * n_mod_shard)
    sh1, sc1, gt1, sh2, sc2, gt2, sh3, sc3, gt3 = [mod[:, i * D:(i + 1) * D] for i in range(N_MOD)]

    shards = [w[0].astype(BF16) for w in (ffn1_w_gate, ffn1_w_up, ffn1_w_down, w_in, w_out, ffn2_w_gate, ffn2_w_up, ffn2_w_down)]
    gathered = _all_gather_shards(shards[:3], "gather_weights")
    g1g, g1u, g1d = [lax.dynamic_update_index_in_dim(g, s, chip, 0) for g, s in zip(gathered, shards[:3])]
    zones = [lax.dynamic_update_index_in_dim(lax.empty((4,) + s.shape, BF16), s, chip, 0) for s in shards[3:]]
    ag_in = _start_copies(shards[3:4], zones[:1], _plan_gather_direct, "gather_in_start")
    sh1 = sh1 + ag_in[4][0, 0]

    def gate_up(gg, gu):
        return jnp.transpose(jnp.concatenate([gg, gu], axis=2), (1, 0, 2)).reshape(D, 8 * fs)

    w_gu1, w_d1 = gate_up(g1g, g1u), g1d.reshape(4 * fs, D)

    x1, saved1 = _ffn_fwd(xs, ffn1_norm, sh1, sc1, gt1, w_gu1, w_d1, fs, "ffn1")
    (gin,) = _wait_copies(ag_in[0], ag_in[1], ag_in[2], ag_in[3], x1, _plan_gather_direct, "gather_in_wait")
    w_in_f = jnp.pad(jnp.transpose(gin, (1, 0, 2)).reshape(D, in_width), ((0, 0), (0, in_pad - in_width)))
    ag_rest = _start_copies(shards[4:], zones[1:], _plan_gather_direct, "gather_rest_start", after=gin)
    sh2 = sh2 + ag_rest[4][0, 0]

    cc, s1, s2 = _rope_tables(positions[0])
    alog_row = jnp.pad(a_log, ((0, 0), (0, LANES - HD)))
    dtb_row = jnp.pad(dt_bias, ((0, 0), (0, LANES - HD)))
    col_k, col_v, col_d, col_z, col_ab = HA, 2 * HA, 3 * HA, 3 * HA + 3 * HD, 3 * HA + 4 * HD
    h2 = _pre_fwd(x1, mix_norm, sh2, sc2, "mix_pre")
    proj = _matmul(h2, w_in_f, "nn", F32, "mix_in_proj", tm=512, tn=2432, tk=4096)
    attn_bias = _log_multiplicity_table(_pick(S, ATTN_TILE))
    qa, ka = _attn_prep_fwd(proj, q_norm, k_norm, cc, s1, s2, "attn_prep")
    oa, oa_b, lse = _attn_fwd(qa, ka, proj, col_v, attn_bias, "attn_fwd")
    dqkv = _conv_fwd(proj, col_d, conv_width, conv_full, "conv_fwd")
    *prep, t_inv = _delta_prep_fwd(dqkv, proj, col_ab, alog_row, dtb_row, "delta_prep")
    od_raw, states = _delta_scan_fwd(*prep, "delta_scan")
    od = _post_fwd(od_raw, proj, col_z, delta_out_norm, "delta_post")
    o = jnp.concatenate([oa_b, od], axis=1)
    gout, g2g, g2u, g2d = _wait_copies(ag_rest[0], ag_rest[1], ag_rest[2], ag_rest[3], o, _plan_gather_direct, "gather_rest_wait")
    w_out_f = gout.reshape(-1, D)
    w_gu2, w_d2 = gate_up(g2g, g2u), g2d.reshape(4 * fs, D)
    mo = _matmul(o, w_out_f, "nn", F32, "mix_out_proj", tm=512, tn=1024, tk=4096)
    x2 = _residual_fwd(x1, gt2, mo, 1.0, "mix_res")

    x3, saved3 = _ffn_fwd(x2, ffn2_norm, sh3, sc3, gt3, w_gu2, w_d2, fs, "ffn2")
    loss_part, dy = _loss_head(x3, loss_target[0], "loss_head")
    loss = lax.psum(loss_part[0, 0], ("x", "y", "c"))

    dx2, dw_gu2, dw_d2, dgain3, dsh3, dsc3, dgt3 = _ffn_bwd(dy, saved3, ffn2_norm, sh3, sc3, gt3, w_gu2, w_d2, fs, "ffn2")

    def scatter_start(slabs32, name):
        slabs16 = [s.astype(BF16) for s in slabs32]
        zones = []
        for s in slabs16:
            half = s.shape[1] // 2
            own = lax.dynamic_slice(s, (chip, ci * half, 0), (1, half, s.shape[2]))
            zones.append(lax.dynamic_update_slice(lax.empty((8, half, s.shape[2]), BF16), own, (dev, 0, 0)))
        return _start_copies(slabs16, zones, _plan_scatter_direct, name)

    rs_ffn2 = scatter_start([dw_gu2, dw_d2.reshape(4, fs, D)], "rs_ffn2_start")
    dmo, dgt2 = _residual_bwd(gt2 + rs_ffn2[4][0, 0], mo, dx2, 1.0, "mix_res_bwd")
    do = _matmul(dmo, w_out_f, "nt", F32, "mix_out_dx", tm=1024, tn=1024, tk=4096)
    dw_out = _matmul(o, dmo, "tn", F32, "mix_out_dw", tm=1024, tn=1024, tk=1024)
    dq = _attn_bwd_dq(qa, ka, proj, col_v, oa, lse, do, 0, attn_bias, "attn_bwd_dq")
    dk, dv = _attn_bwd_dkv(qa, ka, proj, col_v, oa, lse, do, 0, attn_bias, "attn_bwd_dkv")
    dpq, dpk, dq_gain, dk_gain = _attn_prep_bwd(proj, q_norm, k_norm, cc, s1, s2, dq, dk, "attn_prep_bwd")
    dod, dz, ddn_gain = _post_bwd(od_raw, proj, col_z, delta_out_norm, do, HA, "delta_post_bwd")
    cots = _delta_scan_bwd(*prep, states, dod, "delta_scan_bwd")
    ddq, ddk, ddv, dab, dalog, ddtb = _delta_prep_bwd(dqkv, proj, col_ab, alog_row, dtb_row, cots, t_inv, "delta_prep_bwd")
    dconv_in, dconv_w = _conv_bwd(proj, col_d, conv_width, conv_full, jnp.concatenate([ddq, ddk, ddv], axis=1), "conv_bwd")
    dproj = jnp.concatenate([dpq, dpk, dv, dconv_in, dz, dab], axis=1)
    dh2 = _matmul(dproj, w_in_f, "nt", F32, "mix_in_dx", tm=1024, tn=1024, tk=2816)
    dw_in = _matmul(h2, dproj, "tn", F32, "mix_in_dw", tm=512, tn=2432, tk=2048)
    rs_mix = scatter_start([_cols_to_slabs(dw_in[:, :in_width], 4), dw_out.reshape(4, -1, D)], "rs_mix_start")
    dx1, dgain2, dsh2, dsc2 = _pre_bwd(x1, mix_norm, sh2, sc2, dh2, dx2, "mix_pre_bwd")

    rs_ffn1d = []

    def send_dw_d1(dw_d):
        rs_ffn1d.extend(scatter_start([dw_d.reshape(4, fs, D)], "rs_ffn1d_start"))
        return rs_ffn1d[4][0, 0]

    rs_gu1 = []

    def send_dw_gu1(slab):
        (other,) = _swap_halves_with_sibling([slab], "rs_sibling_swap")
        half = slab.shape[1] // 2
        mine = lax.dynamic_slice_in_dim(slab, ci * half, half, axis=1)
        p32, p16 = _add_cast([mine, other], [F32, BF16], "rs_chip_sum")
        rs_gu1.append(p32)
        rs_gu1.extend(_start_copies([p16], [lax.empty(p16.shape, BF16)], _plan_exchange_chips, "rs_chip_exchange_start"))
        return rs_gu1[5]

    dx0, dw_gu1, dw_d1, dgain1, dsh1, dsc1, dgt1 = _ffn_bwd(dx1, saved1, ffn1_norm, sh1, sc1, gt1 + rs_mix[4][0, 0], w_gu1, w_d1, fs,
                                                            "ffn1", on_dw_d=send_dw_d1, on_dw_gu=send_dw_gu1)

    (got,) = _wait_copies(rs_gu1[1], rs_gu1[2], rs_gu1[3], rs_gu1[4], dx0, _plan_exchange_chips, "rs_chip_exchange_wait")
    parts = [lax.dynamic_index_in_dim(rs_gu1[0], chip, axis=0, keepdims=True)]
    parts += [lax.dynamic_index_in_dim(got, (chip + k) % 4, axis=0, keepdims=True) for k in (1, 2, 3)]
    halves = [_add_cast(parts, [F32], "rs_total_0")[0][0]]
    arrived = _wait_copies(rs_ffn1d[0], rs_ffn1d[1], rs_ffn1d[2], rs_ffn1d[3], dx0, _plan_scatter_direct, "rs_ffn1d_wait")
    arrived += _wait_copies(rs_mix[0], rs_mix[1], rs_mix[2], rs_mix[3], dx0, _plan_scatter_direct, "rs_mix_wait")
    arrived += _wait_copies(rs_ffn2[0], rs_ffn2[1], rs_ffn2[2], rs_ffn2[3], dx0, _plan_scatter_direct, "rs_ffn2_wait")
    for t, zone in enumerate(arrived):
        halves.append(_add_cast([zone[d:d + 1] for d in range(8)], [F32], "rs_total_%d" % (t + 2))[0][0])
    theirs = _send_halves_to_sibling(halves, "rs_sibling_join")
    h_gu1, h_d1, h_in, h_out, h_gu2, h_d2 = zip(halves, theirs)

    n_small = N_MOD * D + 3 * D + 5 * LANES + CONV_WIDTH * conv_width
    cols_small = -(-n_small // (8 * LANES)) * LANES
    small = _flat_pad([dsh1, dsc1, dgt1, dsh2, dsc2, dgt2, dsh3, dsc3, dgt3, dgain1, dgain2, dgain3,
                       dq_gain, dk_gain, dalog, ddtb, ddn_gain, dconv_w], 8, cols_small)
    got2 = _all_gather_small(small, "gather_small_grads", after=theirs[0])
    small_sum = _add_cast([got2[d:d + 1] for d in range(8)], [F32], "sum_small_grads")[0].reshape(-1)
    dmod_all = got2.reshape(8, -1)[:, :N_MOD * D]
    off = [0]

    def take(n):
        off[0] += n
        return small_sum[off[0] - n:off[0]]

    g_b_ada = take(N_MOD * D).reshape(1, -1)
    g_ffn1_norm, g_mix_norm, g_ffn2_norm = take(D).reshape(1, D), take(D).reshape(1, D), take(D).reshape(1, D)
    g_q_norm, g_k_norm = take(LANES).reshape(1, -1), take(LANES).reshape(1, -1)
    g_a_log, g_dt_bias = take(LANES)[:HD].reshape(1, HD), take(LANES)[:HD].reshape(1, HD)
    g_dn = take(LANES).reshape(1, -1)
    g_conv_full = take(CONV_WIDTH * conv_width).reshape(CONV_WIDTH, conv_width)
    g_conv = lax.dynamic_slice(g_conv_full, (0, chip * conv_shard), (CONV_WIDTH, conv_shard))

    res = {}

    def upd(name, w, pair, col, m, v):
        g, d, nm, nv = _adamw_halves(w[0], pair[0], pair[1], col, m[0], v[0], "adamw_" + name)
        res[name] = (g[None], d[None], nm[None], nv[None])

    upd("ffn1_w_gate", ffn1_w_gate, h_gu1, 0, m_ffn1_w_gate, v_ffn1_w_gate)
    upd("ffn1_w_up", ffn1_w_up, h_gu1, 1, m_ffn1_w_up, v_ffn1_w_up)
    upd("ffn1_w_down", ffn1_w_down, h_d1, 0, m_ffn1_w_down, v_ffn1_w_down)
    upd("w_in", w_in, h_in, 0, m_w_in, v_w_in)
    upd("w_out", w_out, h_out, 0, m_w_out, v_w_out)
    upd("ffn2_w_gate", ffn2_w_gate, h_gu2, 0, m_ffn2_w_gate, v_ffn2_w_gate)
    upd("ffn2_w_up", ffn2_w_up, h_gu2, 1, m_ffn2_w_up, v_ffn2_w_up)
    upd("ffn2_w_down", ffn2_w_down, h_d2, 0, m_ffn2_w_down, v_ffn2_w_down)
    d_cv, nm_cv, nv_cv = _adamw(conv_w[0], g_conv, m_conv_w[0], v_conv_w[0], "adamw_conv_w")
    res["conv_w"] = (g_conv[None], d_cv[None], nm_cv[None], nv_cv[None])

    dmod_mine = lax.dynamic_slice(dmod_all, (0, chip * n_mod_shard), (8, n_mod_shard))
    g, d, nm, nv = _adamw_outer(w_ada[0], m_w_ada[0], v_w_ada[0], jnp.transpose(c_all), dmod_mine, "adamw_w_ada")
    res["w_ada"] = (g[None], d[None], nm[None], nv[None])

    rep = [("b_ada", b_ada, g_b_ada, m_b_ada, v_b_ada), ("ffn1_norm", ffn1_norm, g_ffn1_norm, m_ffn1_norm, v_ffn1_norm),
           ("mix_norm", mix_norm, g_mix_norm, m_mix_norm, v_mix_norm), ("ffn2_norm", ffn2_norm, g_ffn2_norm, m_ffn2_norm, v_ffn2_norm),
           ("q_norm", q_norm, g_q_norm, m_q_norm, v_q_norm), ("k_norm", k_norm, g_k_norm, m_k_norm, v_k_norm),
           ("a_log", a_log, g_a_log, m_a_log, v_a_log), ("dt_bias", dt_bias, g_dt_bias, m_dt_bias, v_dt_bias),
           ("delta_out_norm", delta_out_norm, g_dn, m_delta_out_norm, v_delta_out_norm)]
    n_rep = sum(-(-r[1].shape[1] // LANES) * LANES for r in rep)
    cols_rep = -(-n_rep // (8 * LANES)) * LANES

    def pack_rep(idx):
        return _flat_pad([jnp.pad(r[idx], ((0, 0), (0, -r[idx].shape[1] % LANES))) for r in rep], 8, cols_rep)

    d_rep, nm_rep, nv_rep = [a.reshape(-1) for a in _adamw(pack_rep(1), pack_rep(2), pack_rep(3), pack_rep(4), "adamw_small")]
    o2 = 0
    for name, w, g, _, _ in rep:
        n = w.shape[1]
        res[name] = (g, d_rep[o2:o2 + n].reshape(1, n), nm_rep[o2:o2 + n].reshape(1, n), nv_rep[o2:o2 + n].reshape(1, n))
        o2 += -(-n // LANES) * LANES

    order = ["w_ada", "b_ada", "ffn1_norm", "ffn1_w_gate", "ffn1_w_up", "ffn1_w_down", "mix_norm", "w_in", "conv_w", "q_norm",
             "k_norm", "a_log", "dt_bias", "delta_out_norm", "w_out", "ffn2_norm", "ffn2_w_gate", "ffn2_w_up", "ffn2_w_down"]
    return (loss, dx0[None], *[res[n][0] for n in order], *[res[n][1] for n in order],
            *[res[n][2] for n in order], *[res[n][3] for n in order])
```

```python
import functools
import math

import jax
import jax.numpy as jnp
from jax import lax
from jax.experimental import pallas as pl
from jax.experimental.pallas import tpu as pltpu

F32 = jnp.float32
BF16 = jnp.bfloat16
MESH = pl.DeviceIdType.MESH

HEAD_DIM = 128
N_ATTN_HEADS = 8
N_DELTA_HEADS = 8
DILATED_PATTERNS = ((128, 1), (512, 4), (2048, 16))
ROPE_THETA = 500000.0
ROPE_DIM = HEAD_DIM // 4
CONV_WIDTH = 4
CHUNK = 64
NORM_EPS = 1e-6
N_MOD = 9
ADAM_LR = 0.001
ADAM_B1 = 0.9
ADAM_B2 = 0.999
ADAM_EPS = 1e-08
ADAM_WD = 0.01
ADAM_STEP = 10

LANES = 128
VMEM_LIMIT = 56 * 1024 * 1024
ATTN_TILE = 512
HIGHEST = lax.Precision.HIGHEST


def _cparams(sem=None):
    return pltpu.CompilerParams(dimension_semantics=sem, vmem_limit_bytes=VMEM_LIMIT)


def _pick(dim, pref):
    if dim <= pref:
        return dim
    t = (pref // LANES) * LANES
    while t >= LANES:
        if dim % t == 0:
            return t
        t -= LANES
    return dim


def _sigmoid(x):
    return 1.0 / (1.0 + jnp.exp(-x))


def _silu(x):
    return x * _sigmoid(x)


def _softplus(x):
    return jnp.maximum(x, 0.0) + jnp.log(1.0 + jnp.exp(-jnp.abs(x)))


def _rms(x, gain):
    return x * lax.rsqrt(jnp.mean(x * x, axis=-1, keepdims=True) + NORM_EPS) * gain


def _l2(x):
    return x * lax.rsqrt(jnp.sum(x * x, axis=-1, keepdims=True) + NORM_EPS)


def _modulate(x, gain, shift, scale):
    return _rms(x, gain) * (1.0 + scale) + shift


def _dot(a, b):
    return lax.dot_general(a, b, (((1,), (0,)), ((), ())), precision=HIGHEST, preferred_element_type=F32)


def _bdot(a, b, dims):
    return lax.dot_general(a.astype(BF16), b.astype(BF16), (dims, ((), ())), preferred_element_type=F32)


_NN, _NT, _TN = ((1,), (0,)), ((1,), (1,)), ((0,), (0,))
HIGH = lax.Precision.HIGH


def _dot3(a, b, dims=_NN):
    return lax.dot_general(a, b, (dims, ((), ())), precision=HIGH, preferred_element_type=F32)


@jax.custom_vjp
def _mm_nn(a, b):
    return _bdot(a, b, _NN)


_mm_nn.defvjp(lambda a, b: (_bdot(a, b, _NN), (a, b)),
              lambda res, g: (_bdot(g, res[1], _NT), _bdot(res[0], g, _TN)))


@jax.custom_vjp
def _mm_nt(a, b):
    return _bdot(a, b, _NT)


_mm_nt.defvjp(lambda a, b: (_bdot(a, b, _NT), (a, b)),
              lambda res, g: (_bdot(g, res[1], _NN), _bdot(g, res[0], _TN)))


@jax.custom_vjp
def _mm_tn(a, b):
    return _bdot(a, b, _TN)


_mm_tn.defvjp(lambda a, b: (_bdot(a, b, _TN), (a, b)),
              lambda res, g: (_bdot(res[1], g, _NT), _bdot(res[0], g, _NN)))


@jax.custom_vjp
def _tri_inv_saved(a, t_inv):
    return t_inv


_tri_inv_saved.defvjp(lambda a, t_inv: (t_inv, t_inv),
                      lambda t_inv, g: (-_dot3(t_inv, _dot3(g, t_inv, _NT), _TN), jnp.zeros_like(t_inv)))


_MM_DIMS = {"nn": ((1,), (0,)), "nt": ((1,), (1,)), "tn": ((0,), (0,))}


def _matmul(a, b, mode, out_dtype, name, tm=1024, tn=1024, tk=1024, col_slabs=False, epilogue=None, after=None):
    if mode == "nn":
        (M, K), (_, N) = a.shape, b.shape
    elif mode == "nt":
        (M, K), (N, _) = a.shape, b.shape
    else:
        (K, M), (_, N) = a.shape, b.shape
    tm, tn, tk = _pick(M, tm), _pick(N, tn), _pick(K, tk)
    nk = K // tk
    dims = _MM_DIMS[mode]
    epi_fn, extra_in, outs = epilogue if epilogue is not None else (None, [], None)
    if outs is None:
        if col_slabs:
            outs = [((N // tn, M, tn), out_dtype, (None, tm, tn), lambda i, j: (j, i, 0))]
        else:
            outs = [((M, N), out_dtype, (tm, tn), lambda i, j: (i, j))]
    n_in, n_out = len(extra_in), len(outs)

    tied = [] if after is None else [after]

    def body(a_ref, b_ref, *rest):
        rest = rest[len(tied):]
        in_refs, out_refs = rest[:n_in], rest[n_in:n_in + n_out]
        k = pl.program_id(2)
        p = _bdot(a_ref[...], b_ref[...], dims)

        def finish(tile):
            if epi_fn is None:
                out_refs[0][...] = tile.astype(out_dtype)
            else:
                epi_fn(tile, in_refs, out_refs)

        if nk == 1:
            finish(p)
        else:
            acc_ref = rest[-1]

            @pl.when(k == 0)
            def _():
                acc_ref[...] = p

            @pl.when((k > 0) & (k < nk - 1))
            def _():
                acc_ref[...] += p

            @pl.when(k == nk - 1)
            def _():
                finish(acc_ref[...] + p)

    def ij(index_map):
        return lambda i, j, k: index_map(i, j)

    a_spec = pl.BlockSpec((tk, tm), lambda i, j, k: (k, i)) if mode == "tn" else pl.BlockSpec((tm, tk), lambda i, j, k: (i, k))
    b_spec = pl.BlockSpec((tn, tk), lambda i, j, k: (j, k)) if mode == "nt" else pl.BlockSpec((tk, tn), lambda i, j, k: (k, j))
    res = pl.pallas_call(
        body, name=name, grid=(M // tm, N // tn, nk),
        in_specs=[a_spec, b_spec] + [pl.BlockSpec(memory_space=pl.ANY)] * len(tied) + [pl.BlockSpec(blk, ij(im)) for _, blk, im in extra_in],
        out_specs=[pl.BlockSpec(blk, ij(im)) for _, _, blk, im in outs],
        out_shape=[jax.ShapeDtypeStruct(shp, dt) for shp, dt, _, _ in outs],
        scratch_shapes=[pltpu.VMEM((tm, tn), F32)] if nk > 1 else [],
        compiler_params=_cparams(("parallel", "parallel", "arbitrary")),
    )(a, b, *tied, *[arr for arr, _, _ in extra_in])
    return res if epilogue is not None else res[0]


def _row_spec(tr, d):
    return pl.BlockSpec((tr, d), lambda i: (i, 0))


def _vec_spec(d):
    return pl.BlockSpec((1, d), lambda i: (0, 0))


def _pre_fwd(x, gain, shift, scale, name):
    S, D = x.shape
    tr = _pick(S, 256)

    def body(x_ref, g_ref, sh_ref, sc_ref, h_ref):
        h_ref[...] = _modulate(x_ref[...], g_ref[...], sh_ref[...], sc_ref[...]).astype(BF16)

    return pl.pallas_call(
        body, name=name, grid=(S // tr,),
        in_specs=[_row_spec(tr, D), _vec_spec(D), _vec_spec(D), _vec_spec(D)],
        out_specs=_row_spec(tr, D), out_shape=jax.ShapeDtypeStruct((S, D), BF16),
        compiler_params=_cparams(("parallel",)),
    )(x, gain, shift, scale)


def _pre_bwd(x, gain, shift, scale, dh, dx_in, name):
    S, D = x.shape
    tr = _pick(S, 256)

    def body(x_ref, g_ref, sh_ref, sc_ref, dh_ref, dxin_ref, dx_ref, dg_ref, dsh_ref, dsc_ref):
        _, vjp = jax.vjp(_modulate, x_ref[...], g_ref[...], sh_ref[...], sc_ref[...])
        dx, dg, dsh, dsc = vjp(dh_ref[...])
        dx_ref[...] = dxin_ref[...] + dx

        @pl.when(pl.program_id(0) == 0)
        def _():
            dg_ref[...] = jnp.zeros_like(dg_ref)
            dsh_ref[...] = jnp.zeros_like(dsh_ref)
            dsc_ref[...] = jnp.zeros_like(dsc_ref)

        dg_ref[...] += dg
        dsh_ref[...] += dsh
        dsc_ref[...] += dsc

    vec = jax.ShapeDtypeStruct((1, D), F32)
    return pl.pallas_call(
        body, name=name, grid=(S // tr,),
        in_specs=[_row_spec(tr, D), _vec_spec(D), _vec_spec(D), _vec_spec(D), _row_spec(tr, D), _row_spec(tr, D)],
        out_specs=[_row_spec(tr, D), _vec_spec(D), _vec_spec(D), _vec_spec(D)],
        out_shape=[jax.ShapeDtypeStruct((S, D), F32), vec, vec, vec],
        compiler_params=_cparams(("arbitrary",)),
    )(x, gain, shift, scale, dh, dx_in)


def _residual_fwd(x, gate, f, coef, name):
    S, D = x.shape
    tr = _pick(S, 256)

    def body(x_ref, g_ref, f_ref, o_ref):
        o_ref[...] = x_ref[...] + coef * g_ref[...] * f_ref[...]

    return pl.pallas_call(
        body, name=name, grid=(S // tr,),
        in_specs=[_row_spec(tr, D), _vec_spec(D), _row_spec(tr, D)],
        out_specs=_row_spec(tr, D), out_shape=jax.ShapeDtypeStruct((S, D), F32),
        compiler_params=_cparams(("parallel",)),
    )(x, gate, f)


def _residual_bwd(gate, f, dxn, coef, name):
    S, D = f.shape
    tr = _pick(S, 256)

    def body(g_ref, f_ref, d_ref, df_ref, dg_ref):
        d = d_ref[...]
        df_ref[...] = (coef * g_ref[...] * d).astype(BF16)

        @pl.when(pl.program_id(0) == 0)
        def _():
            dg_ref[...] = jnp.zeros_like(dg_ref)

        dg_ref[...] += jnp.sum(coef * f_ref[...] * d, axis=0, keepdims=True)

    return pl.pallas_call(
        body, name=name, grid=(S // tr,),
        in_specs=[_vec_spec(D), _row_spec(tr, D), _row_spec(tr, D)],
        out_specs=[_row_spec(tr, D), _vec_spec(D)],
        out_shape=[jax.ShapeDtypeStruct((S, D), BF16), jax.ShapeDtypeStruct((1, D), F32)],
        compiler_params=_cparams(("arbitrary",)),
    )(gate, f, dxn)


def _swiglu_fn(a, b):
    return _silu(a) * b


def _loss_head(y, target, name):
    S, D = y.shape
    tr = _pick(S, 256)

    def body(y_ref, t_ref, l_ref, dy_ref):
        e = y_ref[...] - t_ref[...]
        dy_ref[...] = e * (1.0 / D)

        @pl.when(pl.program_id(0) == 0)
        def _():
            l_ref[...] = jnp.zeros_like(l_ref)

        l_ref[...] += jnp.sum(jnp.sum(e * e, axis=-1, keepdims=True), axis=0, keepdims=True) * (0.5 / D)

    return pl.pallas_call(
        body, name=name, grid=(S // tr,),
        in_specs=[_row_spec(tr, D), _row_spec(tr, D)],
        out_specs=[pl.BlockSpec((1, 1), lambda i: (0, 0)), _row_spec(tr, D)],
        out_shape=[jax.ShapeDtypeStruct((1, 1), F32), jax.ShapeDtypeStruct((S, D), F32)],
        compiler_params=_cparams(("arbitrary",)),
    )(y, target)


def _rope(y, cc, s1, s2):
    return y * cc + pltpu.roll(y, LANES - ROPE_DIM // 2, 1) * s1 + pltpu.roll(y, ROPE_DIM // 2, 1) * s2


def _rope_t(d, cc, s1, s2):
    return d * cc + pltpu.roll(d * s1, ROPE_DIM // 2, 1) + pltpu.roll(d * s2, LANES - ROPE_DIM // 2, 1)


def _head_spec(tr, col0):
    return pl.BlockSpec((tr, HEAD_DIM), lambda i, h: (i, col0 + h))


def _tab_spec(tr):
    return pl.BlockSpec((tr, HEAD_DIM), lambda i, h: (i, 0))


def _gain_spec():
    return pl.BlockSpec((1, HEAD_DIM), lambda i, h: (0, 0))


def _attn_prep_fwd(proj, q_gain, k_gain, cc, s1, s2, name):
    S = proj.shape[0]
    H = N_ATTN_HEADS
    tr = _pick(S, 512)

    def body(q_ref, k_ref, qg_ref, kg_ref, cc_ref, s1_ref, s2_ref, qo_ref, ko_ref):
        cc, s1, s2 = cc_ref[...], s1_ref[...], s2_ref[...]
        qo_ref[...] = _rope(_rms(q_ref[...], qg_ref[...]), cc, s1, s2)
        ko_ref[...] = _rope(_rms(k_ref[...], kg_ref[...]), cc, s1, s2)

    out = jax.ShapeDtypeStruct((S, H * HEAD_DIM), F32)
    return pl.pallas_call(
        body, name=name, grid=(S // tr, H),
        in_specs=[_head_spec(tr, 0), _head_spec(tr, H), _gain_spec(), _gain_spec(), _tab_spec(tr), _tab_spec(tr), _tab_spec(tr)],
        out_specs=[_head_spec(tr, 0), _head_spec(tr, 0)], out_shape=[out, out],
        compiler_params=_cparams(("parallel", "parallel")),
    )(proj, proj, q_gain, k_gain, cc, s1, s2)


def _attn_prep_bwd(proj, q_gain, k_gain, cc, s1, s2, near, far, name):
    S = proj.shape[0]
    H = N_ATTN_HEADS
    tr = _pick(S, 512)

    def body(q_ref, k_ref, qg_ref, kg_ref, cc_ref, s1_ref, s2_ref, dqn_ref, dkn_ref, dvn_ref, dqf_ref, dkf_ref, dvf_ref,
             dpq_ref, dpk_ref, dpv_ref, dqg_ref, dkg_ref):
        cc, s1, s2 = cc_ref[...], s1_ref[...], s2_ref[...]
        dq_ref = dqn_ref[...] + dqf_ref[...]
        dk_ref = dkn_ref[...] + dkf_ref[...]
        dpv_ref[...] = (dvn_ref[...] + dvf_ref[...]).astype(BF16)

        @pl.when((pl.program_id(0) == 0) & (pl.program_id(1) == 0))
        def _():
            dqg_ref[...] = jnp.zeros_like(dqg_ref)
            dkg_ref[...] = jnp.zeros_like(dkg_ref)

        _, vjp_q = jax.vjp(_rms, q_ref[...], qg_ref[...])
        dxq, dgq = vjp_q(_rope_t(dq_ref[...], cc, s1, s2))
        _, vjp_k = jax.vjp(_rms, k_ref[...], kg_ref[...])
        dxk, dgk = vjp_k(_rope_t(dk_ref[...], cc, s1, s2))
        dpq_ref[...] = dxq.astype(BF16)
        dpk_ref[...] = dxk.astype(BF16)
        dqg_ref[...] += dgq
        dkg_ref[...] += dgk

    out = jax.ShapeDtypeStruct((S, H * HEAD_DIM), BF16)
    gout = jax.ShapeDtypeStruct((1, HEAD_DIM), F32)
    return pl.pallas_call(
        body, name=name, grid=(S // tr, H),
        in_specs=[_head_spec(tr, 0), _head_spec(tr, H), _gain_spec(), _gain_spec(), _tab_spec(tr), _tab_spec(tr), _tab_spec(tr)]
        + [_head_spec(tr, 0)] * 6,
        out_specs=[_head_spec(tr, 0)] * 3 + [_gain_spec(), _gain_spec()], out_shape=[out, out, out, gout, gout],
        compiler_params=_cparams(("arbitrary", "arbitrary")),
    )(proj, proj, q_gain, k_gain, cc, s1, s2, *near, *far)


def _multiplicity(j, t):
    ti = lax.broadcasted_iota(jnp.int32, (t, t), 0)
    si = lax.broadcasted_iota(jnp.int32, (t, t), 1)
    delta = j * t + ti - si
    cnt = jnp.zeros((t, t), F32)
    for window, dil in NEAR_PATTERNS:
        ok = (delta >= 0) & ((delta & (dil - 1)) == 0) & (delta <= window)
        cnt = cnt + ok.astype(F32)
    return cnt


_NEG = -1e30
NEAR_PATTERNS = DILATED_PATTERNS[:2]
NEAR_WINDOW = max(w for w, _ in NEAR_PATTERNS)
FAR_WINDOW, FAR_DIL = DILATED_PATTERNS[2]


def _log_multiplicity_table(t):
    cnt = jnp.stack([_multiplicity(j, t) for j in range(NEAR_WINDOW // t + 1)])
    return jnp.where(cnt > 0.0, jnp.log(jnp.maximum(cnt, 1.0)), _NEG)


def _bias_spec(t):
    return pl.BlockSpec((NEAR_WINDOW // t + 1, t, t), lambda h, i, j: (0, 0, 0))


def _attn_fwd(q, k, proj, v_col0, bias, name):
    S = q.shape[0]
    H = N_ATTN_HEADS
    t = _pick(S, ATTN_TILE)
    nq = S // t
    nj = NEAR_WINDOW // t + 1
    scale = HEAD_DIM ** -0.5

    def body(q_ref, k_ref, v_ref, b_ref, o_ref, lse_ref, m_sc, l_sc, acc_sc):
        qb, j = pl.program_id(1), pl.program_id(2)

        @pl.when(j == 0)
        def _():
            m_sc[...] = jnp.full_like(m_sc, _NEG)
            l_sc[...] = jnp.zeros_like(l_sc)
            acc_sc[...] = jnp.zeros_like(acc_sc)

        @pl.when(qb - j >= 0)
        def _():
            s = _bdot(q_ref[...], k_ref[...], ((1,), (1,))) * scale + b_ref[j]
            m_prev = m_sc[...]
            m_new = jnp.maximum(m_prev, jnp.max(s, axis=-1, keepdims=True))
            alpha = jnp.exp(m_prev - m_new)
            p = jnp.exp(s - m_new)
            l_sc[...] = alpha * l_sc[...] + jnp.sum(p, axis=-1, keepdims=True)
            acc_sc[...] = alpha * acc_sc[...] + _bdot(p, v_ref[...], ((1,), (0,)))
            m_sc[...] = m_new

        @pl.when(j == nj - 1)
        def _():
            o_ref[...] = acc_sc[...] / l_sc[...]
            lse_ref[...] = jnp.broadcast_to(m_sc[...] + jnp.log(l_sc[...]), (t, HEAD_DIM))

    qspec = pl.BlockSpec((t, HEAD_DIM), lambda h, i, j: (i, h))
    kspec = pl.BlockSpec((t, HEAD_DIM), lambda h, i, j: (jnp.maximum(i - j, 0), h))
    vspec = pl.BlockSpec((t, HEAD_DIM), lambda h, i, j: (jnp.maximum(i - j, 0), v_col0 + h))
    return pl.pallas_call(
        body, name=name, grid=(H, nq, nj),
        in_specs=[qspec, kspec, vspec, _bias_spec(t)], out_specs=[qspec, qspec],
        out_shape=[jax.ShapeDtypeStruct((S, H * HEAD_DIM), F32), jax.ShapeDtypeStruct((S, H * HEAD_DIM), F32)],
        scratch_shapes=[pltpu.VMEM((t, 1), F32), pltpu.VMEM((t, 1), F32), pltpu.VMEM((t, HEAD_DIM), F32)],
        compiler_params=_cparams(("parallel", "parallel", "arbitrary")),
    )(q, k, proj, bias)


def _far_rows(r, n):
    return pl.ds(r, n, stride=FAR_DIL)


def _far_band_bias(n):
    i = lax.broadcasted_iota(jnp.int32, (n, n), 0)
    j = lax.broadcasted_iota(jnp.int32, (n, n), 1)
    return jnp.where((i >= j) & (i - j <= FAR_WINDOW // FAR_DIL), 0.0, _NEG)


def _col_spec(S, col0):
    return pl.BlockSpec((S, HEAD_DIM), lambda h: (0, col0 + h))


def _attn_far_fwd(q, k, proj, v_col0, o_near, lse_near, name):
    S = q.shape[0]
    H = N_ATTN_HEADS
    n = S // FAR_DIL
    scale = HEAD_DIM ** -0.5

    def body(q_ref, k_ref, v_ref, on_ref, ln_ref, o_ref, ob_ref, lse_ref):
        bias = _far_band_bias(n)
        for r in range(FAR_DIL):
            rows = _far_rows(r, n)
            s = _bdot(q_ref[rows, :], k_ref[rows, :], _NT) * scale + bias
            m = jnp.max(s, axis=-1, keepdims=True)
            p = jnp.exp(s - m)
            l = jnp.sum(p, axis=-1, keepdims=True)
            o_far = _bdot(p, v_ref[rows, :], _NN) / l
            lse_far = m + jnp.log(l)
            lse_near = jnp.max(ln_ref[rows, :], axis=-1, keepdims=True)
            top = jnp.maximum(lse_near, lse_far)
            lse = top + jnp.log(jnp.exp(lse_near - top) + jnp.exp(lse_far - top))
            o_ref[rows, :] = jnp.exp(lse_near - lse) * on_ref[rows, :] + jnp.exp(lse_far - lse) * o_far
            lse_ref[rows, :] = jnp.broadcast_to(lse, (n, HEAD_DIM))
        ob_ref[...] = o_ref[...].astype(BF16)

    cs = _col_spec(S, 0)
    return pl.pallas_call(
        body, name=name, grid=(H,),
        in_specs=[cs, cs, _col_spec(S, v_col0), cs, cs], out_specs=[cs, cs, cs],
        out_shape=[jax.ShapeDtypeStruct((S, H * HEAD_DIM), F32), jax.ShapeDtypeStruct((S, H * HEAD_DIM), BF16),
                   jax.ShapeDtypeStruct((S, H * HEAD_DIM), F32)],
        compiler_params=_cparams(("parallel",)),
    )(q, k, proj, o_near, lse_near)


def _attn_far_bwd(q, k, proj, v_col0, o, lse, do, do_col0, name):
    S = q.shape[0]
    H = N_ATTN_HEADS
    n = S // FAR_DIL
    scale = HEAD_DIM ** -0.5

    def body(q_ref, k_ref, v_ref, o_ref, lse_ref, do_ref, dq_ref, dk_ref, dv_ref):
        bias = _far_band_bias(n)
        for r in range(FAR_DIL):
            rows = _far_rows(r, n)
            q, k, v, do = q_ref[rows, :], k_ref[rows, :], v_ref[rows, :], do_ref[rows, :]
            dsum = jnp.sum(do * o_ref[rows, :], axis=-1, keepdims=True)
            lse = jnp.max(lse_ref[rows, :], axis=-1, keepdims=True)
            p = jnp.exp(_bdot(q, k, _NT) * scale + bias - lse)
            ds = p * (_bdot(do, v, _NT) - dsum)
            dq_ref[rows, :] = _bdot(ds, k, _NN) * scale
            dk_ref[rows, :] = _bdot(ds, q, _TN) * scale
            dv_ref[rows, :] = _bdot(p, do, _TN)

    cs = _col_spec(S, 0)
    shp = jax.ShapeDtypeStruct((S, H * HEAD_DIM), F32)
    return pl.pallas_call(
        body, name=name, grid=(H,),
        in_specs=[cs, cs, _col_spec(S, v_col0), cs, cs, _col_spec(S, do_col0)], out_specs=[cs, cs, cs], out_shape=[shp, shp, shp],
        compiler_params=_cparams(("parallel",)),
    )(q, k, proj, o, lse, do)


def _attn_probs(q, k, lse, bias_tile, scale):
    return jnp.exp(_bdot(q, k, ((1,), (1,))) * scale + bias_tile - lse)


def _attn_bwd_dq(q, k, proj, v_col0, o, lse, do, do_col0, bias, name):
    S = q.shape[0]
    H = N_ATTN_HEADS
    t = _pick(S, ATTN_TILE)
    nq = S // t
    nj = NEAR_WINDOW // t + 1
    scale = HEAD_DIM ** -0.5

    def body(q_ref, k_ref, v_ref, o_ref, lse_ref, do_ref, b_ref, dq_ref, acc_sc):
        qb, j = pl.program_id(1), pl.program_id(2)

        @pl.when(j == 0)
        def _():
            acc_sc[...] = jnp.zeros_like(acc_sc)

        @pl.when(qb - j >= 0)
        def _():
            do = do_ref[...]
            dsum = jnp.sum(do * o_ref[...], axis=-1, keepdims=True)
            lse = jnp.max(lse_ref[...], axis=-1, keepdims=True)
            p = _attn_probs(q_ref[...], k_ref[...], lse, b_ref[j], scale)
            dp = _bdot(do, v_ref[...], ((1,), (1,)))
            ds = p * (dp - dsum)
            acc_sc[...] += _bdot(ds, k_ref[...], ((1,), (0,))) * scale

        @pl.when(j == nj - 1)
        def _():
            dq_ref[...] = acc_sc[...]

    qspec = pl.BlockSpec((t, HEAD_DIM), lambda h, i, j: (i, h))
    dospec = pl.BlockSpec((t, HEAD_DIM), lambda h, i, j: (i, do_col0 + h))
    kspec = pl.BlockSpec((t, HEAD_DIM), lambda h, i, j: (jnp.maximum(i - j, 0), h))
    vspec = pl.BlockSpec((t, HEAD_DIM), lambda h, i, j: (jnp.maximum(i - j, 0), v_col0 + h))
    return pl.pallas_call(
        body, name=name, grid=(H, nq, nj),
        in_specs=[qspec, kspec, vspec, qspec, qspec, dospec, _bias_spec(t)], out_specs=qspec,
        out_shape=jax.ShapeDtypeStruct((S, H * HEAD_DIM), F32),
        scratch_shapes=[pltpu.VMEM((t, HEAD_DIM), F32)],
        compiler_params=_cparams(("parallel", "parallel", "arbitrary")),
    )(q, k, proj, o, lse, do, bias)


def _attn_bwd_dkv(q, k, proj, v_col0, o, lse, do, do_col0, bias, name):
    S = q.shape[0]
    H = N_ATTN_HEADS
    t = _pick(S, ATTN_TILE)
    nq = S // t
    nj = NEAR_WINDOW // t + 1
    scale = HEAD_DIM ** -0.5

    def body(q_ref, k_ref, v_ref, o_ref, lse_ref, do_ref, b_ref, dk_ref, dv_ref, dk_sc, dv_sc):
        kb, j = pl.program_id(1), pl.program_id(2)

        @pl.when(j == 0)
        def _():
            dk_sc[...] = jnp.zeros_like(dk_sc)
            dv_sc[...] = jnp.zeros_like(dv_sc)

        @pl.when(kb + j < nq)
        def _():
            do = do_ref[...]
            dsum = jnp.sum(do * o_ref[...], axis=-1, keepdims=True)
            lse = jnp.max(lse_ref[...], axis=-1, keepdims=True)
            p = _attn_probs(q_ref[...], k_ref[...], lse, b_ref[j], scale)
            dp = _bdot(do, v_ref[...], ((1,), (1,)))
            ds = p * (dp - dsum)
            dv_sc[...] += _bdot(p, do, ((0,), (0,)))
            dk_sc[...] += _bdot(ds, q_ref[...], ((0,), (0,))) * scale

        @pl.when(j == nj - 1)
        def _():
            dk_ref[...] = dk_sc[...]
            dv_ref[...] = dv_sc[...]

    def qrow(h, i, j):
        return jnp.minimum(i + j, nq - 1)

    qspec = pl.BlockSpec((t, HEAD_DIM), lambda h, i, j: (qrow(h, i, j), h))
    dospec = pl.BlockSpec((t, HEAD_DIM), lambda h, i, j: (qrow(h, i, j), do_col0 + h))
    kspec = pl.BlockSpec((t, HEAD_DIM), lambda h, i, j: (i, h))
    vspec = pl.BlockSpec((t, HEAD_DIM), lambda h, i, j: (i, v_col0 + h))
    return pl.pallas_call(
        body, name=name, grid=(H, nq, nj),
        in_specs=[qspec, kspec, vspec, qspec, qspec, dospec, _bias_spec(t)], out_specs=[kspec, kspec],
        out_shape=[jax.ShapeDtypeStruct((S, H * HEAD_DIM), F32), jax.ShapeDtypeStruct((S, H * HEAD_DIM), F32)],
        scratch_shapes=[pltpu.VMEM((t, HEAD_DIM), F32), pltpu.VMEM((t, HEAD_DIM), F32)],
        compiler_params=_cparams(("parallel", "parallel", "arbitrary")),
    )(q, k, proj, o, lse, do, bias)


def _conv_pre(x_ref, w_ref):
    x = x_ref[...]
    rows = lax.broadcasted_iota(jnp.int32, x.shape, 0)
    shifted = [x]
    acc = x * w_ref[pl.ds(CONV_WIDTH - 1, 1), :]
    for sft in range(1, CONV_WIDTH):
        xs = jnp.where(rows >= sft, pltpu.roll(x, sft, 0), 0.0)
        shifted.append(xs)
        acc = acc + xs * w_ref[pl.ds(CONV_WIDTH - 1 - sft, 1), :]
    return acc, shifted


def _conv_fwd(proj, col0, width, w, name):
    S = proj.shape[0]

    def body(x_ref, w_ref, y_ref):
        acc, _ = _conv_pre(x_ref, w_ref)
        y_ref[...] = _silu(acc)

    return pl.pallas_call(
        body, name=name, grid=(width // LANES,),
        in_specs=[pl.BlockSpec((S, LANES), lambda c: (0, col0 + c)), pl.BlockSpec((CONV_WIDTH, LANES), lambda c: (0, c))],
        out_specs=pl.BlockSpec((S, LANES), lambda c: (0, c)),
        out_shape=jax.ShapeDtypeStruct((S, width), F32),
        compiler_params=_cparams(("parallel",)),
    )(proj, w)


def _conv_bwd(proj, col0, width, w, dy, name):
    S = proj.shape[0]

    def body(x_ref, w_ref, d_ref, dx_ref, dw_ref):
        acc, shifted = _conv_pre(x_ref, w_ref)
        sig = _sigmoid(acc)
        da = d_ref[...] * (sig * (1.0 + acc * (1.0 - sig)))
        rows = lax.broadcasted_iota(jnp.int32, da.shape, 0)
        dx = da * w_ref[pl.ds(CONV_WIDTH - 1, 1), :]
        dw_ref[pl.ds(CONV_WIDTH - 1, 1), :] = jnp.sum(da * shifted[0], axis=0, keepdims=True)
        for sft in range(1, CONV_WIDTH):
            back = jnp.where(rows < S - sft, pltpu.roll(da, S - sft, 0), 0.0)
            dx = dx + back * w_ref[pl.ds(CONV_WIDTH - 1 - sft, 1), :]
            dw_ref[pl.ds(CONV_WIDTH - 1 - sft, 1), :] = jnp.sum(da * shifted[sft], axis=0, keepdims=True)
        dx_ref[...] = dx.astype(BF16)

    return pl.pallas_call(
        body, name=name, grid=(width // LANES,),
        in_specs=[pl.BlockSpec((S, LANES), lambda c: (0, col0 + c)), pl.BlockSpec((CONV_WIDTH, LANES), lambda c: (0, c)),
                  pl.BlockSpec((S, LANES), lambda c: (0, c))],
        out_specs=[pl.BlockSpec((S, LANES), lambda c: (0, c)), pl.BlockSpec((CONV_WIDTH, LANES), lambda c: (0, c))],
        out_shape=[jax.ShapeDtypeStruct((S, width), BF16), jax.ShapeDtypeStruct((CONV_WIDTH, width), F32)],
        compiler_params=_cparams(("parallel",)),
    )(proj, w, dy)


PREP_CHUNKS = 8


def _chunks_prep(qraws, kraws, vs, abs_, alog_row, dtb_row, mask_g, mask_b, t_saved=None):
    n = len(qraws)
    c = qraws[0].shape[0]
    mm_nt, mm_nn = (_mm_nt, _mm_nn) if t_saved is not None else (lambda p, r: _bdot(p, r, _NT), lambda p, r: _bdot(p, r, _NN))
    row = lax.broadcasted_iota(jnp.int32, (c, c), 0)
    col = lax.broadcasted_iota(jnp.int32, (c, c), 1)
    tril, strict, eye = row >= col, row > col, row == col
    eyef = eye.astype(F32)
    neg_rate = -jnp.exp(alog_row)
    q, k, beta, gc_col, gamma, kb, g_last = [], [], [], [], [], [], []
    for i in range(n):
        q.append(_l2(qraws[i]) * (HEAD_DIM ** -0.5))
        k.append(_l2(kraws[i]))
        gfull = neg_rate * _softplus(abs_[i] + dtb_row)
        g = jnp.sum(jnp.where(mask_g, gfull, 0.0), axis=-1, keepdims=True)
        beta.append(jnp.sum(jnp.where(mask_b, _sigmoid(abs_[i]), 0.0), axis=-1, keepdims=True))
        g_row = jnp.sum(jnp.where(eye, g, 0.0), axis=0, keepdims=True)
        gc_col.append(jnp.sum(jnp.where(tril, g_row, 0.0), axis=1, keepdims=True))
        gc_row = jnp.sum(jnp.where(row <= col, g, 0.0), axis=0, keepdims=True)
        gamma.append(jnp.where(tril, jnp.exp(jnp.where(tril, gc_col[i] - gc_row, 0.0)), 0.0))
        kb.append(k[i] * beta[i])
        g_last.append(jnp.sum(g, axis=0, keepdims=True))
    a = [jnp.where(strict, mm_nt(kb[i], k[i]) * gamma[i], 0.0) for i in range(n)]
    if t_saved is None:
        t_inv = [eyef - a[i] for i in range(n)]
        p = a
        for _ in range(int(math.log2(c)) - 1):
            p = [_dot3(p[i], p[i]) for i in range(n)]
            t_inv = [_dot3(t_inv[i], eyef + p[i]) for i in range(n)]
    else:
        t_inv = [_tri_inv_saved(a[i], t_saved[i]) for i in range(n)]
    egc = [jnp.exp(gc_col[i]) for i in range(n)]
    u = [mm_nn(t_inv[i], vs[i] * beta[i]) for i in range(n)]
    w = [mm_nn(t_inv[i], kb[i] * egc[i]) for i in range(n)]
    intra = [mm_nt(q[i], k[i]) * gamma[i] for i in range(n)]
    out = []
    for i in range(n):
        kt = k[i] * jnp.exp(g_last[i] - gc_col[i])
        dec = jnp.broadcast_to(jnp.exp(g_last[i]), (1, HEAD_DIM))
        one = (u[i], w[i], q[i] * egc[i], kt, intra[i], dec)
        out.append(one + (t_inv[i],) if t_saved is None else one)
    return out


def _lane_masks(h):
    lane = lax.broadcasted_iota(jnp.int32, (1, LANES), 1)
    return lane == h, lane == N_DELTA_HEADS + h


def _prep_specs(tr, ab_col):
    H = N_DELTA_HEADS
    return [
        pl.BlockSpec((tr, HEAD_DIM), lambda i, h: (i, h)),
        pl.BlockSpec((tr, HEAD_DIM), lambda i, h: (i, H + h)),
        pl.BlockSpec((tr, HEAD_DIM), lambda i, h: (i, 2 * H + h)),
        pl.BlockSpec((tr, LANES), lambda i, h: (i, ab_col)),
        pl.BlockSpec((1, LANES), lambda i, h: (0, 0)),
        pl.BlockSpec((1, LANES), lambda i, h: (0, 0)),
    ]


def _prep_out_specs(tr):
    nc = tr // CHUNK
    hs = pl.BlockSpec((tr, HEAD_DIM), lambda i, h: (i, h))
    return [hs, hs, hs, hs,
            pl.BlockSpec((None, tr, CHUNK), lambda i, h: (h, i, 0)),
            pl.BlockSpec((None, nc, 1, HEAD_DIM), lambda i, h: (h, i, 0, 0)),
            pl.BlockSpec((None, tr, CHUNK), lambda i, h: (h, i, 0))]


def _prep_out_shapes(S):
    H = N_DELTA_HEADS
    hs = jax.ShapeDtypeStruct((S, H * HEAD_DIM), F32)
    sq = jax.ShapeDtypeStruct((H, S, CHUNK), F32)
    return [hs, hs, hs, hs, sq, jax.ShapeDtypeStruct((H, S // CHUNK, 1, HEAD_DIM), F32), sq]


def _delta_prep_fwd(dqkv, proj, ab_col, alog_row, dtb_row, name):
    S = dqkv.shape[0]
    tr = min(S, PREP_CHUNKS * CHUNK)
    nc = tr // CHUNK

    def body(q_ref, k_ref, v_ref, ab_ref, al_ref, dt_ref, u_ref, w_ref, qd_ref, kt_ref, in_ref, dec_ref, ti_ref):
        mask_g, mask_b = _lane_masks(pl.program_id(1))
        rows = [pl.ds(ci * CHUNK, CHUNK) for ci in range(nc)]
        outs = _chunks_prep([q_ref[rs, :] for rs in rows], [k_ref[rs, :] for rs in rows], [v_ref[rs, :] for rs in rows],
                            [ab_ref[rs, :] for rs in rows], al_ref[...], dt_ref[...], mask_g, mask_b)
        for ci, rs in enumerate(rows):
            u, w, qd, kt, intra, dec, t_inv = outs[ci]
            u_ref[rs, :] = u
            w_ref[rs, :] = w
            qd_ref[rs, :] = qd
            kt_ref[rs, :] = kt
            in_ref[rs, :] = intra
            dec_ref[ci] = dec
            ti_ref[rs, :] = t_inv

    return pl.pallas_call(
        body, name=name, grid=(S // tr, N_DELTA_HEADS),
        in_specs=_prep_specs(tr, ab_col), out_specs=_prep_out_specs(tr), out_shape=_prep_out_shapes(S),
        compiler_params=_cparams(("parallel", "parallel")),
    )(dqkv, dqkv, dqkv, proj, alog_row, dtb_row)


def _delta_prep_bwd(dqkv, proj, ab_col, alog_row, dtb_row, cots, t_inv, name):
    S = dqkv.shape[0]
    H = N_DELTA_HEADS
    tr = min(S, PREP_CHUNKS * CHUNK)
    nc = tr // CHUNK

    def body(q_ref, k_ref, v_ref, ab_ref, al_ref, dt_ref, du_ref, dw_ref, dqd_ref, dkt_ref, din_ref, ddec_ref, ti_ref,
             dq_ref, dk_ref, dv_ref, dab_ref, dal_ref, ddt_ref, dab_sc):
        h = pl.program_id(1)
        mask_g, mask_b = _lane_masks(h)

        @pl.when((pl.program_id(0) == 0) & (h == 0))
        def _():
            dal_ref[...] = jnp.zeros_like(dal_ref)
            ddt_ref[...] = jnp.zeros_like(ddt_ref)

        @pl.when(h == 0)
        def _():
            dab_sc[...] = jnp.zeros_like(dab_sc)

        rows = [pl.ds(ci * CHUNK, CHUNK) for ci in range(nc)]
        fn = functools.partial(_chunks_prep, mask_g=mask_g, mask_b=mask_b, t_saved=[ti_ref[rs, :] for rs in rows])
        _, vjp = jax.vjp(fn, [q_ref[rs, :] for rs in rows], [k_ref[rs, :] for rs in rows], [v_ref[rs, :] for rs in rows],
                         [ab_ref[rs, :] for rs in rows], al_ref[...], dt_ref[...])
        dqs, dks, dvs, dabs, dal, ddt = vjp([(du_ref[rs, :], dw_ref[rs, :], dqd_ref[rs, :], dkt_ref[rs, :], din_ref[rs, :],
                                              ddec_ref[ci]) for ci, rs in enumerate(rows)])
        for ci, rs in enumerate(rows):
            dq_ref[rs, :] = dqs[ci]
            dk_ref[rs, :] = dks[ci]
            dv_ref[rs, :] = dvs[ci]
            dab_sc[rs, :] += dabs[ci]
        dal_ref[...] += dal
        ddt_ref[...] += ddt

        @pl.when(h == H - 1)
        def _():
            dab_ref[...] = dab_sc[...].astype(BF16)

    hs = pl.BlockSpec((tr, HEAD_DIM), lambda i, h: (i, h))
    hshape = jax.ShapeDtypeStruct((S, H * HEAD_DIM), F32)
    row = pl.BlockSpec((1, LANES), lambda i, h: (0, 0))
    rshape = jax.ShapeDtypeStruct((1, LANES), F32)
    return pl.pallas_call(
        body, name=name, grid=(S // tr, H),
        in_specs=_prep_specs(tr, ab_col) + _prep_out_specs(tr),
        out_specs=[hs, hs, hs, pl.BlockSpec((tr, LANES), lambda i, h: (i, 0)), row, row],
        out_shape=[hshape, hshape, hshape, jax.ShapeDtypeStruct((S, LANES), BF16), rshape, rshape],
        scratch_shapes=[pltpu.VMEM((tr, LANES), F32)],
        compiler_params=_cparams(("arbitrary", "arbitrary")),
    )(dqkv, dqkv, dqkv, proj, alog_row, dtb_row, *cots, t_inv)


def _scan_steps(states, us, ws, qds, kts, intras, decs, diff=False):
    nn, tn = (_mm_nn, _mm_tn) if diff else (lambda p, r: _bdot(p, r, _NN), lambda p, r: _bdot(p, r, _TN))
    hs = range(len(states))
    v_new = [us[h] - nn(ws[h], states[h]) for h in hs]
    o_state = [nn(qds[h], states[h]) for h in hs]
    o_intra = [nn(intras[h], v_new[h]) for h in hs]
    grown = [tn(kts[h], v_new[h]) for h in hs]
    return [o_state[h] + o_intra[h] for h in hs], [states[h] * decs[h] + grown[h] for h in hs]


def _scan_specs(rev, n):
    H = N_DELTA_HEADS

    def cix(i):
        return (n - 1 - i) if rev else i

    row = pl.BlockSpec((CHUNK, H * HEAD_DIM), lambda i: (cix(i), 0))
    return row, pl.BlockSpec((H, CHUNK, CHUNK), lambda i: (0, cix(i), 0)), \
        pl.BlockSpec((H, 1, 1, HEAD_DIM), lambda i: (0, cix(i), 0, 0)), \
        pl.BlockSpec((1, H, HEAD_DIM, HEAD_DIM), lambda i: (cix(i), 0, 0, 0))


def _delta_scan_fwd(u, w, qd, kt, intra, dec, name):
    S = u.shape[0]
    H = N_DELTA_HEADS
    n = S // CHUNK
    row, ispec, dspec, sspec = _scan_specs(False, n)

    def body(u_ref, w_ref, qd_ref, kt_ref, in_ref, dec_ref, o_ref, st_ref, s_sc):
        @pl.when(pl.program_id(0) == 0)
        def _():
            s_sc[...] = jnp.zeros_like(s_sc)

        cols = [pl.ds(h * HEAD_DIM, HEAD_DIM) for h in range(H)]
        states = [s_sc[h] for h in range(H)]
        outs, new = _scan_steps(states, [u_ref[:, cs] for cs in cols], [w_ref[:, cs] for cs in cols],
                                [qd_ref[:, cs] for cs in cols], [kt_ref[:, cs] for cs in cols],
                                [in_ref[h] for h in range(H)], [dec_ref[h, 0] for h in range(H)])
        for h, cs in enumerate(cols):
            st_ref[0, h] = states[h]
            o_ref[:, cs] = outs[h]
            s_sc[h] = new[h]

    return pl.pallas_call(
        body, name=name, grid=(n,),
        in_specs=[row, row, row, row, ispec, dspec], out_specs=[row, sspec],
        out_shape=[jax.ShapeDtypeStruct((S, H * HEAD_DIM), F32), jax.ShapeDtypeStruct((n, H, HEAD_DIM, HEAD_DIM), F32)],
        scratch_shapes=[pltpu.VMEM((H, HEAD_DIM, HEAD_DIM), F32)],
        compiler_params=_cparams(("arbitrary",)),
    )(u, w, qd, kt, intra, dec)


def _delta_scan_bwd(u, w, qd, kt, intra, dec, states, do, name):
    S = u.shape[0]
    H = N_DELTA_HEADS
    n = S // CHUNK
    row, ispec, dspec, sspec = _scan_specs(True, n)

    def body(u_ref, w_ref, qd_ref, kt_ref, in_ref, dec_ref, st_ref, do_ref,
             du_ref, dw_ref, dqd_ref, dkt_ref, din_ref, ddec_ref, ds_sc):
        @pl.when(pl.program_id(0) == 0)
        def _():
            ds_sc[...] = jnp.zeros_like(ds_sc)

        cols = [pl.ds(h * HEAD_DIM, HEAD_DIM) for h in range(H)]
        _, vjp = jax.vjp(functools.partial(_scan_steps, diff=True), [st_ref[0, h] for h in range(H)], [u_ref[:, cs] for cs in cols],
                         [w_ref[:, cs] for cs in cols], [qd_ref[:, cs] for cs in cols], [kt_ref[:, cs] for cs in cols],
                         [in_ref[h] for h in range(H)], [dec_ref[h, 0] for h in range(H)])
        dstate, du, dw, dqd, dkt, din, ddec = vjp(([do_ref[:, cs] for cs in cols], [ds_sc[h] for h in range(H)]))
        for h, cs in enumerate(cols):
            du_ref[:, cs] = du[h]
            dw_ref[:, cs] = dw[h]
            dqd_ref[:, cs] = dqd[h]
            dkt_ref[:, cs] = dkt[h]
            din_ref[h] = din[h]
            ddec_ref[h, 0] = ddec[h]
            ds_sc[h] = dstate[h]

    hshape = jax.ShapeDtypeStruct((S, H * HEAD_DIM), F32)
    return pl.pallas_call(
        body, name=name, grid=(n,),
        in_specs=[row, row, row, row, ispec, dspec, sspec, row],
        out_specs=[row, row, row, row, ispec, dspec],
        out_shape=[hshape, hshape, hshape, hshape, jax.ShapeDtypeStruct((H, S, CHUNK), F32),
                   jax.ShapeDtypeStruct((H, n, 1, HEAD_DIM), F32)],
        scratch_shapes=[pltpu.VMEM((H, HEAD_DIM, HEAD_DIM), F32)],
        compiler_params=_cparams(("arbitrary",)),
    )(u, w, qd, kt, intra, dec, states, do)


def _gated_norm(od, z, gain):
    return _rms(od, gain) * _silu(z)


def _post_fwd(od, proj, z_col0, gain, name):
    S = od.shape[0]
    H = N_DELTA_HEADS
    tr = _pick(S, 512)

    def body(od_ref, z_ref, g_ref, o_ref):
        o_ref[...] = _gated_norm(od_ref[...], z_ref[...], g_ref[...]).astype(BF16)

    return pl.pallas_call(
        body, name=name, grid=(S // tr, H),
        in_specs=[_head_spec(tr, 0), _head_spec(tr, z_col0), _gain_spec()],
        out_specs=_head_spec(tr, 0), out_shape=jax.ShapeDtypeStruct((S, H * HEAD_DIM), BF16),
        compiler_params=_cparams(("parallel", "parallel")),
    )(od, proj, gain)


def _post_bwd(od, proj, z_col0, gain, do, do_col0, name):
    S = od.shape[0]
    H = N_DELTA_HEADS
    tr = _pick(S, 512)

    def body(od_ref, z_ref, g_ref, do_ref, dod_ref, dz_ref, dg_ref):
        @pl.when((pl.program_id(0) == 0) & (pl.program_id(1) == 0))
        def _():
            dg_ref[...] = jnp.zeros_like(dg_ref)

        _, vjp = jax.vjp(_gated_norm, od_ref[...], z_ref[...], g_ref[...])
        dod, dz, dg = vjp(do_ref[...])
        dod_ref[...] = dod
        dz_ref[...] = dz.astype(BF16)
        dg_ref[...] += dg

    return pl.pallas_call(
        body, name=name, grid=(S // tr, H),
        in_specs=[_head_spec(tr, 0), _head_spec(tr, z_col0), _gain_spec(), _head_spec(tr, do_col0)],
        out_specs=[_head_spec(tr, 0), _head_spec(tr, 0), _gain_spec()],
        out_shape=[jax.ShapeDtypeStruct((S, H * HEAD_DIM), F32), jax.ShapeDtypeStruct((S, H * HEAD_DIM), BF16),
                   jax.ShapeDtypeStruct((1, HEAD_DIM), F32)],
        compiler_params=_cparams(("arbitrary", "arbitrary")),
    )(od, proj, gain, do)


def _adam_math(w, g, m, v):
    m = ADAM_B1 * m + (1.0 - ADAM_B1) * g
    v = ADAM_B2 * v + (1.0 - ADAM_B2) * (g * g)
    m_hat = m / (1.0 - ADAM_B1 ** ADAM_STEP)
    v_hat = v / (1.0 - ADAM_B2 ** ADAM_STEP)
    delta = -ADAM_LR * (m_hat / (jnp.sqrt(v_hat) + ADAM_EPS) + ADAM_WD * w)
    return delta, m, v


def _adamw(w, g, m, v, name):
    R, C = w.shape
    tr = R if R * C * 4 <= (1 << 20) else _pick8(R, max(8, (1 << 20) // (C * 4)))

    def body(w_ref, g_ref, m_ref, v_ref, d_ref, nm_ref, nv_ref):
        d, nm, nv = _adam_math(w_ref[...], g_ref[...], m_ref[...], v_ref[...])
        d_ref[...] = d
        nm_ref[...] = nm
        nv_ref[...] = nv

    spec = pl.BlockSpec((tr, C), lambda i: (i, 0))
    shp = jax.ShapeDtypeStruct((R, C), F32)
    return pl.pallas_call(
        body, name=name, grid=(R // tr,), in_specs=[spec] * 4, out_specs=[spec] * 3, out_shape=[shp] * 3,
        compiler_params=_cparams(("parallel",)),
    )(w, g, m, v)


def _adamw_halves(w, mine, theirs, col, m, v, name):
    R, C = w.shape
    tr = _pick8(R // 2, max(8, (1 << 20) // (C * 4)))
    nb2 = (R // 2) // tr

    def body(w_ref, a_ref, b_ref, m_ref, v_ref, g_ref, d_ref, nm_ref, nv_ref):
        top = pl.program_id(0) < nb2
        g = jnp.where(top == (lax.axis_index("c") == 0), a_ref[...], b_ref[...])
        d, nm, nv = _adam_math(w_ref[...], g, m_ref[...], v_ref[...])
        g_ref[...] = g
        d_ref[...] = d
        nm_ref[...] = nm
        nv_ref[...] = nv

    spec = pl.BlockSpec((tr, C), lambda i: (i, 0))
    half = pl.BlockSpec((tr, C), lambda i: (i % nb2, col))
    shp = jax.ShapeDtypeStruct((R, C), F32)
    return pl.pallas_call(
        body, name=name, grid=(2 * nb2,), in_specs=[spec, half, half, spec, spec], out_specs=[spec] * 4, out_shape=[shp] * 4,
        compiler_params=_cparams(("parallel",)),
    )(w, mine, theirs, m, v)


def _pick8(dim, pref):
    t = (min(dim, pref) // 8) * 8
    while t >= 8:
        if dim % t == 0:
            return t
        t -= 8
    return dim


def _adamw_outer(w, m, v, cond_t, rhs, name):
    R, C = w.shape
    tr = _pick8(R, 128)
    nb = cond_t.shape[1]
    lhs_t = cond_t

    def body(w_ref, m_ref, v_ref, a_ref, b_ref, g_ref, d_ref, nm_ref, nv_ref):
        g = _dot(_silu(a_ref[...]), b_ref[...])
        d, nm, nv = _adam_math(w_ref[...], g, m_ref[...], v_ref[...])
        g_ref[...] = g
        d_ref[...] = d
        nm_ref[...] = nm
        nv_ref[...] = nv

    spec = pl.BlockSpec((tr, C), lambda i: (i, 0))
    shp = jax.ShapeDtypeStruct((R, C), F32)
    return pl.pallas_call(
        body, name=name, grid=(R // tr,),
        in_specs=[spec, spec, spec, pl.BlockSpec((tr, nb), lambda i: (i, 0)), pl.BlockSpec((nb, C), lambda i: (0, 0))],
        out_specs=[spec] * 4, out_shape=[shp] * 4,
        compiler_params=_cparams(("parallel",)),
    )(w, m, v, lhs_t, rhs)


def _ada_fwd(cond, w, bias, name):
    a = cond
    nb, K = a.shape
    N = w.shape[1]
    tn = _pick(N, 512)

    def body(a_ref, w_ref, b_ref, o_ref):
        o_ref[...] = _dot(_silu(a_ref[...]), w_ref[...]) + b_ref[...]

    return pl.pallas_call(
        body, name=name, grid=(N // tn,),
        in_specs=[pl.BlockSpec((nb, K), lambda j: (0, 0)), pl.BlockSpec((K, tn), lambda j: (0, j)), pl.BlockSpec((1, tn), lambda j: (0, j))],
        out_specs=pl.BlockSpec((nb, tn), lambda j: (0, j)), out_shape=jax.ShapeDtypeStruct((nb, N), F32),
        compiler_params=_cparams(("parallel",)),
    )(a, w, bias)


def _add_cast(parts, out_dtypes, name):
    shape = parts[0].shape
    G, R, C = shape
    tr = _pick8(R, max(8, (1 << 20) // (C * 4)))
    n_in = len(parts)

    def body(*refs):
        acc = refs[0][...].astype(F32)
        for r in refs[1:n_in]:
            acc = acc + r[...].astype(F32)
        for o, dt in zip(refs[n_in:], out_dtypes):
            o[...] = acc.astype(dt)

    spec = pl.BlockSpec((1, tr, C), lambda g, i: (g, i, 0))
    outs = pl.pallas_call(
        body, name=name, grid=(G, R // tr), in_specs=[spec] * n_in, out_specs=[spec] * len(out_dtypes),
        out_shape=[jax.ShapeDtypeStruct(shape, dt) for dt in out_dtypes],
        compiler_params=_cparams(("parallel", "parallel")),
    )(*parts)
    return outs


def _me():
    return lax.axis_index("x"), lax.axis_index("y"), lax.axis_index("c")


def _xor_peer(k):
    x, y, c = _me()
    dx, dy, dc = (k >> 2) & 1, (k >> 1) & 1, k & 1
    return (x ^ dx if dx else x, y ^ dy if dy else y, c ^ dc if dc else c)


ANY = pl.BlockSpec(memory_space=pl.ANY)


def _all_gather_small(v, name, after=None):
    R, C = v.shape
    extra = [] if after is None else [after]

    def body(v_ref, *rest):
        out_ref, send_sems, recv_sems = rest[len(extra):]
        x, y, c = _me()
        mine = 4 * x + 2 * y + c
        out_ref[mine] = v_ref[...]
        copies = []
        for k in range(1, 8):
            cp = pltpu.make_async_remote_copy(src_ref=v_ref, dst_ref=out_ref.at[mine], send_sem=send_sems.at[k - 1],
                                              recv_sem=recv_sems.at[k - 1], device_id=_xor_peer(k), device_id_type=MESH)
            cp.start()
            copies.append(cp)
        for k in range(1, 8):
            px, py, pc = _xor_peer(k)
            pltpu.make_async_remote_copy(src_ref=v_ref, dst_ref=out_ref.at[4 * px + 2 * py + pc], send_sem=send_sems.at[k - 1],
                                         recv_sem=recv_sems.at[k - 1], device_id=_xor_peer(k), device_id_type=MESH).wait_recv()
        for cp in copies:
            cp.wait_send()

    return pl.pallas_call(
        body, name=name, out_shape=jax.ShapeDtypeStruct((8, R, C), F32),
        in_specs=[pl.BlockSpec(memory_space=pltpu.VMEM)] + [ANY] * len(extra), out_specs=pl.BlockSpec(memory_space=pltpu.VMEM),
        scratch_shapes=[pltpu.SemaphoreType.DMA((7,)), pltpu.SemaphoreType.DMA((7,))],
        compiler_params=pltpu.CompilerParams(vmem_limit_bytes=VMEM_LIMIT),
    )(v, *extra)


def _chip_peers():
    x, y, _ = _me()
    return [(1, (x, 1 - y)), (2, (1 - x, y)), (3, (1 - x, 1 - y))]


def _all_gather_shards(shards, name):
    n = len(shards)

    def body(*refs):
        ins, outs = refs[:n], refs[n:2 * n]
        send_sems, recv_sems = refs[2 * n:]
        x, y, c = _me()
        chip = 2 * x + y
        sib = (x, y, 1 - c)
        peers = _chip_peers()
        sends = []
        for t in range(n):
            half = ins[t].shape[0] // 2
            mine = pl.ds(c * half, half)
            for p, (k, (px, py)) in enumerate(peers):
                cp = pltpu.make_async_remote_copy(src_ref=ins[t].at[mine], dst_ref=outs[t].at[chip, mine],
                                                  send_sem=send_sems.at[6 * t + p], recv_sem=recv_sems.at[6 * t + p],
                                                  device_id=(px, py, c), device_id_type=MESH)
                cp.start()
                sends.append(cp)
        for t in range(n):
            half = ins[t].shape[0] // 2
            mine = pl.ds(c * half, half)
            for p, (k, (px, py)) in enumerate(peers):
                src_chip = 2 * px + py
                landed = outs[t].at[src_chip, mine]
                pltpu.make_async_remote_copy(src_ref=landed, dst_ref=landed, send_sem=send_sems.at[6 * t + p],
                                             recv_sem=recv_sems.at[6 * t + p], device_id=(px, py, c), device_id_type=MESH).wait_recv()
                fw = pltpu.make_async_remote_copy(src_ref=landed, dst_ref=landed, send_sem=send_sems.at[6 * t + 3 + p],
                                                  recv_sem=recv_sems.at[6 * t + 3 + p], device_id=sib, device_id_type=MESH)
                fw.start()
                sends.append(fw)
        for t in range(n):
            half = ins[t].shape[0] // 2
            theirs = pl.ds((1 - c) * half, half)
            for p, (k, (px, py)) in enumerate(peers):
                got = outs[t].at[2 * px + py, theirs]
                pltpu.make_async_remote_copy(src_ref=got, dst_ref=got, send_sem=send_sems.at[6 * t + 3 + p],
                                             recv_sem=recv_sems.at[6 * t + 3 + p], device_id=sib, device_id_type=MESH).wait_recv()
        for cp in sends:
            cp.wait_send()

    return pl.pallas_call(
        body, name=name,
        out_shape=[jax.ShapeDtypeStruct((4,) + s.shape, s.dtype) for s in shards],
        in_specs=[ANY] * n, out_specs=[ANY] * n,
        scratch_shapes=[pltpu.SemaphoreType.DMA((6 * n,)), pltpu.SemaphoreType.DMA((6 * n,))],
    )(*shards)


def _swap_halves_with_sibling(slabs, name):
    n = len(slabs)

    def body(*refs):
        ins, outs = refs[:n], refs[n:2 * n]
        send_sems, recv_sems = refs[2 * n:]
        x, y, c = _me()
        sib = (x, y, 1 - c)
        cps = []
        for t in range(n):
            half = ins[t].shape[1] // 2
            cp = pltpu.make_async_remote_copy(src_ref=ins[t].at[:, pl.ds((1 - c) * half, half)], dst_ref=outs[t],
                                              send_sem=send_sems.at[t], recv_sem=recv_sems.at[t], device_id=sib, device_id_type=MESH)
            cp.start()
            cps.append(cp)
        for cp in cps:
            cp.wait()

    return pl.pallas_call(
        body, name=name,
        out_shape=[jax.ShapeDtypeStruct((4, s.shape[1] // 2, s.shape[2]), s.dtype) for s in slabs],
        in_specs=[ANY] * n, out_specs=[ANY] * n,
        scratch_shapes=[pltpu.SemaphoreType.DMA((n,)), pltpu.SemaphoreType.DMA((n,))],
    )(*slabs)


def _send_halves_to_sibling(halves, name):
    n = len(halves)

    def body(*refs):
        ins, outs = refs[:n], refs[n:2 * n]
        send_sems, recv_sems = refs[2 * n:]
        x, y, c = _me()
        sib = (x, y, 1 - c)
        cps = []
        for t in range(n):
            cp = pltpu.make_async_remote_copy(src_ref=ins[t], dst_ref=outs[t], send_sem=send_sems.at[t],
                                              recv_sem=recv_sems.at[t], device_id=sib, device_id_type=MESH)
            cp.start()
            cps.append(cp)
        for cp in cps:
            cp.wait()

    return pl.pallas_call(
        body, name=name,
        out_shape=[jax.ShapeDtypeStruct(s.shape, s.dtype) for s in halves],
        in_specs=[ANY] * n, out_specs=[ANY] * n,
        scratch_shapes=[pltpu.SemaphoreType.DMA((n,)), pltpu.SemaphoreType.DMA((n,))],
    )(*halves)


HBM_SPEC = pl.BlockSpec(memory_space=pltpu.HBM)
SEM_SPEC = pl.BlockSpec(memory_space=pltpu.SEMAPHORE)
DATAFLOW = pltpu.SideEffectType.DATAFLOW_SIDE_EFFECTING


def _plan_gather_direct(src_refs, land_refs):
    x, y, c = _me()
    chip = 2 * x + y
    plan = []
    for s, land in zip(src_refs, land_refs):
        half = s.shape[0] // 2
        for _, (px, py) in _chip_peers():
            for pc in (c, 1 - c):
                plan.append((s.at[pl.ds(c * half, half)], land.at[chip, pl.ds(c * half, half)],
                             land.at[2 * px + py, pl.ds(pc * half, half)], (px, py, pc)))
    return plan


def _plan_scatter_direct(src_refs, land_refs):
    x, y, c = _me()
    plan = []
    for s, land in zip(src_refs, land_refs):
        half = s.shape[1] // 2
        for k in range(1, 8):
            px, py, pc = _xor_peer(k)
            plan.append((s.at[2 * px + py, pl.ds(pc * half, half)], land.at[4 * x + 2 * y + c],
                         land.at[4 * px + 2 * py + pc], (px, py, pc)))
    return plan


def _plan_exchange_chips(src_refs, land_refs):
    x, y, c = _me()
    chip = 2 * x + y
    plan = []
    for s, land in zip(src_refs, land_refs):
        for _, (px, py) in _chip_peers():
            plan.append((s.at[2 * px + py], land.at[chip], land.at[2 * px + py], (px, py, c)))
    return plan


_plan_gather_direct.per_tensor = 6
_plan_scatter_direct.per_tensor = 7
_plan_exchange_chips.per_tensor = 3


def _start_copies(srcs, lands, plan_fn, name, after=None):
    n = len(srcs)
    n_copies = len(srcs) * plan_fn.per_tensor
    extra = [] if after is None else [after]

    def body(*refs):
        refs = refs[:2 * n] + refs[2 * n + len(extra):]
        send_sems, recv_sems, token = refs[2 * n], refs[2 * n + 1], refs[-1]
        for i, (src, dst, _, peer) in enumerate(plan_fn(refs[:n], refs[n:2 * n])):
            pltpu.make_async_remote_copy(src_ref=src, dst_ref=dst, send_sem=send_sems.at[i], recv_sem=recv_sems.at[i],
                                         device_id=peer, device_id_type=MESH).start()
        token[...] = jnp.zeros_like(token)

    arrays = list(srcs) + list(lands)
    outs = pl.pallas_call(
        body, name=name,
        out_shape=(pltpu.SemaphoreType.DMA((n_copies,)), pltpu.SemaphoreType.DMA((n_copies,)),
                   *[pltpu.HBM(a.shape, a.dtype) for a in arrays], jax.ShapeDtypeStruct((8, LANES), F32)),
        in_specs=[HBM_SPEC] * (2 * n) + [ANY] * len(extra),
        out_specs=(SEM_SPEC, SEM_SPEC, *[HBM_SPEC] * (2 * n), pl.BlockSpec(memory_space=pltpu.VMEM)),
        input_output_aliases={i: 2 + i for i in range(2 * n)},
        compiler_params=pltpu.CompilerParams(has_side_effects=DATAFLOW),
    )(*[pltpu.with_memory_space_constraint(a, pltpu.HBM) for a in arrays], *extra)
    return outs[0], outs[1], list(outs[2:2 + n]), list(outs[2 + n:2 + 2 * n]), outs[-1]


def _wait_copies(send_sems, recv_sems, srcs, lands, after, plan_fn, name):
    n = len(srcs)

    def body(*refs):
        send_sems, recv_sems = refs[2 * n], refs[2 * n + 1]
        for i, (src, _, arrival, peer) in enumerate(plan_fn(refs[:n], refs[n:2 * n])):
            cp = pltpu.make_async_remote_copy(src_ref=src, dst_ref=arrival, send_sem=send_sems.at[i], recv_sem=recv_sems.at[i],
                                              device_id=peer, device_id_type=MESH)
            cp.wait_send()
            cp.wait_recv()

    arrays = list(srcs) + list(lands)
    outs = pl.pallas_call(
        body, name=name,
        out_shape=tuple(pltpu.HBM(a.shape, a.dtype) for a in arrays),
        in_specs=[HBM_SPEC] * (2 * n) + [SEM_SPEC, SEM_SPEC, ANY],
        out_specs=tuple([HBM_SPEC] * (2 * n)),
        input_output_aliases={i: i for i in range(2 * n)},
        compiler_params=pltpu.CompilerParams(has_side_effects=DATAFLOW),
    )(*arrays, send_sems, recv_sems, after)
    return list(outs[n:])


def _rope_tables(positions):
    half = ROPE_DIM // 2
    S = positions.shape[0]
    inv_freq = ROPE_THETA ** (-jnp.arange(half, dtype=F32) / half)
    ang = positions.astype(F32)[:, None] * inv_freq
    cos, sin = jnp.cos(ang), jnp.sin(ang)
    zeros = functools.partial(jnp.zeros, dtype=F32)
    cc = jnp.concatenate([cos, cos, jnp.ones((S, HEAD_DIM - ROPE_DIM), F32)], axis=1)
    s1 = jnp.concatenate([-sin, zeros((S, HEAD_DIM - half))], axis=1)
    s2 = jnp.concatenate([zeros((S, half)), sin, zeros((S, HEAD_DIM - ROPE_DIM))], axis=1)
    return cc, s1, s2


def _ffn_fwd(x, gain, shift, scale, gate, w_gu, w_d, fs, tag):
    S = x.shape[0]
    tm = _pick(S, 512)
    h = _pre_fwd(x, gain, shift, scale, tag + "_pre")

    def swiglu_out(tile, ins, outs):
        outs[0][...] = tile
        outs[1][...] = _swiglu_fn(tile[:, :fs], tile[:, fs:]).astype(BF16)

    ab, s = _matmul(h, w_gu, "nn", F32, tag + "_gate_up", tm=tm, tn=2 * fs, tk=4096, epilogue=(swiglu_out, [], [
        ((S, 8 * fs), F32, (tm, 2 * fs), lambda i, j: (i, j)), ((S, 4 * fs), BF16, (tm, fs), lambda i, j: (i, j))]))
    f = _matmul(s, w_d, "nn", F32, tag + "_down", tm=512, tn=1024, tk=8192)
    xn = _residual_fwd(x, gate, f, 0.5, tag + "_res")
    return xn, (x, h, ab, s, f)


def _ffn_bwd(dxn, saved, gain, shift, scale, gate, w_gu, w_d, fs, tag, on_dw_d=None, on_dw_gu=None):
    x, h, ab, s, f = saved
    df, dgate = _residual_bwd(gate, f, dxn, 0.5, tag + "_res_bwd")
    S = x.shape[0]
    tm = _pick(S, 512)
    dw_d = _matmul(s, df, "tn", F32, tag + "_down_dw", tm=1408, tn=1024, tk=2048)
    if on_dw_d is not None:
        shift = shift + on_dw_d(dw_d)

    def swiglu_back(tile, ins, outs):
        _, vjp = jax.vjp(_swiglu_fn, ins[0][:, :fs], ins[0][:, fs:])
        da, db = vjp(tile)
        outs[0][:, :fs] = da.astype(BF16)
        outs[0][:, fs:] = db.astype(BF16)

    (dab,) = _matmul(df, w_d, "nt", F32, tag + "_down_dx", tm=tm, tn=fs, tk=4096, epilogue=(
        swiglu_back, [(ab, (tm, 2 * fs), lambda i, j: (i, j))], [((S, 8 * fs), BF16, (tm, 2 * fs), lambda i, j: (i, j))]))
    dw_gu = _matmul(h, dab, "tn", F32, tag + "_gate_up_dw", tm=512, tn=2 * fs, tk=2048, col_slabs=True)
    tie = on_dw_gu(dw_gu) if on_dw_gu is not None else None
    dh = _matmul(dab, w_gu, "nt", F32, tag + "_gate_up_dx", tm=1024, tn=1024, tk=2816, after=tie)
    dx, dgain, dshift, dscale = _pre_bwd(x, gain, shift, scale, dh, dxn, tag + "_pre_bwd")
    return dx, dw_gu, dw_d, dgain, dshift, dscale, dgate


def _flat_pad(parts, rows, cols):
    flat = jnp.concatenate([p.reshape(-1).astype(F32) for p in parts])
    return jnp.pad(flat, (0, rows * cols - flat.shape[0])).reshape(rows, cols)


def _cols_to_slabs(w, n):
    R, NC = w.shape
    return jnp.transpose(w.reshape(R, n, NC // n), (1, 0, 2))


def kernel(x, c, positions, w_ada, b_ada, ffn1_norm, ffn1_w_gate, ffn1_w_up, ffn1_w_down, mix_norm, w_in, conv_w, q_norm, k_norm, a_log, dt_bias, delta_out_norm, w_out, ffn2_norm, ffn2_w_gate, ffn2_w_up, ffn2_w_down, loss_target, m_w_ada, m_b_ada, m_ffn1_norm, m_ffn1_w_gate, m_ffn1_w_up, m_ffn1_w_down, m_mix_norm, m_w_in, m_conv_w, m_q_norm, m_k_norm, m_a_log, m_dt_bias, m_delta_out_norm, m_w_out, m_ffn2_norm, m_ffn2_w_gate, m_ffn2_w_up, m_ffn2_w_down, v_w_ada, v_b_ada, v_ffn1_norm, v_ffn1_w_gate, v_ffn1_w_up, v_ffn1_w_down, v_mix_norm, v_w_in, v_conv_w, v_q_norm, v_k_norm, v_a_log, v_dt_bias, v_delta_out_norm, v_w_out, v_ffn2_norm, v_ffn2_w_gate, v_ffn2_w_up, v_ffn2_w_down):
    xi, yi, ci = _me()
    chip = 2 * xi + yi
    dev = 2 * chip + ci
    xs = x[0]
    S, D = xs.shape
    HA, HD = N_ATTN_HEADS, N_DELTA_HEADS
    fs = ffn1_w_gate.shape[2]
    n_mod_shard = w_ada.shape[2]
    in_shard = w_in.shape[2]
    in_width = 4 * in_shard
    in_pad = -(-in_width // LANES) * LANES
    conv_shard = conv_w.shape[2]
    conv_width = 4 * conv_shard

    pack0 = jnp.zeros((8, max(D, conv_shard)), F32)
    pack0 = pack0.at[0, :D].set(c[0]).at[1:1 + CONV_WIDTH, :conv_shard].set(conv_w[0])
    got0 = _all_gather_small(pack0, "gather_cond")
    c_all = got0[:, 0, :D]
    conv_full = jnp.transpose(got0[::2, 1:1 + CONV_WIDTH, :conv_shard], (1, 0, 2)).reshape(CONV_WIDTH, conv_width)
    b_ada_mine = lax.dynamic_slice(b_ada, (0, chip * n_mod_shard), (1, n_mod_shard))
    mod_part = _ada_fwd(c_all, w_ada[0], b_ada_mine, "ada_fwd")
    got1 = _all_gather_small(mod_part, "gather_mod")
    mod = lax.dynamic_index_in_dim(got1[::2], dev, axis=1, keepdims=False).reshape(1, 4 * n_mod_shard)
    sh1, sc1, gt1, sh2, sc2, gt2, sh3, sc3, gt3 = [mod[:, i * D:(i + 1) * D] for i in range(N_MOD)]

    shards = [w[0].astype(BF16) for w in (ffn1_w_gate, ffn1_w_up, ffn1_w_down, w_in, w_out, ffn2_w_gate, ffn2_w_up, ffn2_w_down)]
    gathered = _all_gather_shards(shards[:3], "gather_weights")
    g1g, g1u, g1d = [lax.dynamic_update_index_in_dim(g, s, chip, 0) for g, s in zip(gathered, shards[:3])]
    zones = [lax.dynamic_update_index_in_dim(lax.empty((4,) + s.shape, BF16), s, chip, 0) for s in shards[3:]]
    ag_in = _start_copies(shards[3:4], zones[:1], _plan_gather_direct, "gather_in_start")
    sh1 = sh1 + ag_in[4][0, 0]

    def gate_up(gg, gu):
        return jnp.transpose(jnp.concatenate([gg, gu], axis=2), (1, 0, 2)).reshape(D, 8 * fs)

    w_gu1, w_d1 = gate_up(g1g, g1u), g1d.reshape(4 * fs, D)

    x1, saved1 = _ffn_fwd(xs, ffn1_norm, sh1, sc1, gt1, w_gu1, w_d1, fs, "ffn1")
    (gin,) = _wait_copies(ag_in[0], ag_in[1], ag_in[2], ag_in[3], x1, _plan_gather_direct, "gather_in_wait")
    w_in_f = jnp.pad(jnp.transpose(gin, (1, 0, 2)).reshape(D, in_width), ((0, 0), (0, in_pad - in_width)))
    ag_rest = _start_copies(shards[4:], zones[1:], _plan_gather_direct, "gather_rest_start", after=gin)
    sh2 = sh2 + ag_rest[4][0, 0]

    cc, s1, s2 = _rope_tables(positions[0])
    alog_row = jnp.pad(a_log, ((0, 0), (0, LANES - HD)))
    dtb_row = jnp.pad(dt_bias, ((0, 0), (0, LANES - HD)))
    col_k, col_v, col_d, col_z, col_ab = HA, 2 * HA, 3 * HA, 3 * HA + 3 * HD, 3 * HA + 4 * HD
    h2 = _pre_fwd(x1, mix_norm, sh2, sc2, "mix_pre")
    proj = _matmul(h2, w_in_f, "nn", F32, "mix_in_proj", tm=512, tn=2432, tk=4096)
    attn_bias = _log_multiplicity_table(_pick(S, ATTN_TILE))
    qa, ka = _attn_prep_fwd(proj, q_norm, k_norm, cc, s1, s2, "attn_prep")
    o_near, lse_near = _attn_fwd(qa, ka, proj, col_v, attn_bias, "attn_fwd")
    oa, oa_b, lse = _attn_far_fwd(qa, ka, proj, col_v, o_near, lse_near, "attn_far_fwd")
    dqkv = _conv_fwd(proj, col_d, conv_width, conv_full, "conv_fwd")
    *prep, t_inv = _delta_prep_fwd(dqkv, proj, col_ab, alog_row, dtb_row, "delta_prep")
    od_raw, states = _delta_scan_fwd(*prep, "delta_scan")
    od = _post_fwd(od_raw, proj, col_z, delta_out_norm, "delta_post")
    o = jnp.concatenate([oa_b, od], axis=1)
    gout, g2g, g2u, g2d = _wait_copies(ag_rest[0], ag_rest[1], ag_rest[2], ag_rest[3], o, _plan_gather_direct, "gather_rest_wait")
    w_out_f = gout.reshape(-1, D)
    w_gu2, w_d2 = gate_up(g2g, g2u), g2d.reshape(4 * fs, D)
    mo = _matmul(o, w_out_f, "nn", F32, "mix_out_proj", tm=512, tn=1024, tk=4096)
    x2 = _residual_fwd(x1, gt2, mo, 1.0, "mix_res")

    x3, saved3 = _ffn_fwd(x2, ffn2_norm, sh3, sc3, gt3, w_gu2, w_d2, fs, "ffn2")
    loss_part, dy = _loss_head(x3, loss_target[0], "loss_head")
    loss = lax.psum(loss_part[0, 0], ("x", "y", "c"))

    dx2, dw_gu2, dw_d2, dgain3, dsh3, dsc3, dgt3 = _ffn_bwd(dy, saved3, ffn2_norm, sh3, sc3, gt3, w_gu2, w_d2, fs, "ffn2")

    def scatter_start(slabs32, name):
        slabs16 = [s.astype(BF16) for s in slabs32]
        zones = []
        for s in slabs16:
            half = s.shape[1] // 2
            own = lax.dynamic_slice(s, (chip, ci * half, 0), (1, half, s.shape[2]))
            zones.append(lax.dynamic_update_slice(lax.empty((8, half, s.shape[2]), BF16), own, (dev, 0, 0)))
        return _start_copies(slabs16, zones, _plan_scatter_direct, name)

    rs_ffn2 = scatter_start([dw_gu2, dw_d2.reshape(4, fs, D)], "rs_ffn2_start")
    dmo, dgt2 = _residual_bwd(gt2 + rs_ffn2[4][0, 0], mo, dx2, 1.0, "mix_res_bwd")
    do = _matmul(dmo, w_out_f, "nt", F32, "mix_out_dx", tm=1024, tn=1024, tk=4096)
    dw_out = _matmul(o, dmo, "tn", F32, "mix_out_dw", tm=1024, tn=1024, tk=1024)
    dq = _attn_bwd_dq(qa, ka, proj, col_v, oa, lse, do, 0, attn_bias, "attn_bwd_dq")
    dk, dv = _attn_bwd_dkv(qa, ka, proj, col_v, oa, lse, do, 0, attn_bias, "attn_bwd_dkv")
    far = _attn_far_bwd(qa, ka, proj, col_v, oa, lse, do, 0, "attn_far_bwd")
    dpq, dpk, dv, dq_gain, dk_gain = _attn_prep_bwd(proj, q_norm, k_norm, cc, s1, s2, (dq, dk, dv), far, "attn_prep_bwd")
    dod, dz, ddn_gain = _post_bwd(od_raw, proj, col_z, delta_out_norm, do, HA, "delta_post_bwd")
    cots = _delta_scan_bwd(*prep, states, dod, "delta_scan_bwd")
    ddq, ddk, ddv, dab, dalog, ddtb = _delta_prep_bwd(dqkv, proj, col_ab, alog_row, dtb_row, cots, t_inv, "delta_prep_bwd")
    dconv_in, dconv_w = _conv_bwd(proj, col_d, conv_width, conv_full, jnp.concatenate([ddq, ddk, ddv], axis=1), "conv_bwd")
    dproj = jnp.concatenate([dpq, dpk, dv, dconv_in, dz, dab], axis=1)
    dh2 = _matmul(dproj, w_in_f, "nt", F32, "mix_in_dx", tm=1024, tn=1024, tk=2816)
    dw_in = _matmul(h2, dproj, "tn", F32, "mix_in_dw", tm=512, tn=2432, tk=2048)
    rs_mix = scatter_start([_cols_to_slabs(dw_in[:, :in_width], 4), dw_out.reshape(4, -1, D)], "rs_mix_start")
    dx1, dgain2, dsh2, dsc2 = _pre_bwd(x1, mix_norm, sh2, sc2, dh2, dx2, "mix_pre_bwd")

    rs_ffn1d = []

    def send_dw_d1(dw_d):
        rs_ffn1d.extend(scatter_start([dw_d.reshape(4, fs, D)], "rs_ffn1d_start"))
        return rs_ffn1d[4][0, 0]

    rs_gu1 = []

    def send_dw_gu1(slab):
        (other,) = _swap_halves_with_sibling([slab], "rs_sibling_swap")
        half = slab.shape[1] // 2
        mine = lax.dynamic_slice_in_dim(slab, ci * half, half, axis=1)
        p32, p16 = _add_cast([mine, other], [F32, BF16], "rs_chip_sum")
        rs_gu1.append(p32)
        rs_gu1.extend(_start_copies([p16], [lax.empty(p16.shape, BF16)], _plan_exchange_chips, "rs_chip_exchange_start"))
        return rs_gu1[5]

    dx0, dw_gu1, dw_d1, dgain1, dsh1, dsc1, dgt1 = _ffn_bwd(dx1, saved1, ffn1_norm, sh1, sc1, gt1 + rs_mix[4][0, 0], w_gu1, w_d1, fs,
                                                            "ffn1", on_dw_d=send_dw_d1, on_dw_gu=send_dw_gu1)

    (got,) = _wait_copies(rs_gu1[1], rs_gu1[2], rs_gu1[3], rs_gu1[4], dx0, _plan_exchange_chips, "rs_chip_exchange_wait")
    parts = [lax.dynamic_index_in_dim(rs_gu1[0], chip, axis=0, keepdims=True)]
    parts += [lax.dynamic_index_in_dim(got, (chip + k) % 4, axis=0, keepdims=True) for k in (1, 2, 3)]
    halves = [_add_cast(parts, [F32], "rs_total_0")[0][0]]
    arrived = _wait_copies(rs_ffn1d[0], rs_ffn1d[1], rs_ffn1d[2], rs_ffn1d[3], dx0, _plan_scatter_direct, "rs_ffn1d_wait")
    arrived += _wait_copies(rs_mix[0], rs_mix[1], rs_mix[2], rs_mix[3], dx0, _plan_scatter_direct, "rs_mix_wait")
    arrived += _wait_copies(rs_ffn2[0], rs_ffn2[1], rs_ffn2[2], rs_ffn2[3], dx0, _plan_scatter_direct, "rs_ffn2_wait")
    for t, zone in enumerate(arrived):
        halves.append(_add_cast([zone[d:d + 1] for d in range(8)], [F32], "rs_total_%d" % (t + 2))[0][0])
    theirs = _send_halves_to_sibling(halves, "rs_sibling_join")
    h_gu1, h_d1, h_in, h_out, h_gu2, h_d2 = zip(halves, theirs)

    n_small = N_MOD * D + 3 * D + 5 * LANES + CONV_WIDTH * conv_width
    cols_small = -(-n_small // (8 * LANES)) * LANES
    small = _flat_pad([dsh1, dsc1, dgt1, dsh2, dsc2, dgt2, dsh3, dsc3, dgt3, dgain1, dgain2, dgain3,
                       dq_gain, dk_gain, dalog, ddtb, ddn_gain, dconv_w], 8, cols_small)
    got2 = _all_gather_small(small, "gather_small_grads", after=theirs[0])
    small_sum = _add_cast([got2[d:d + 1] for d in range(8)], [F32], "sum_small_grads")[0].reshape(-1)
    dmod_all = got2.reshape(8, -1)[:, :N_MOD * D]
    off = [0]

    def take(n):
        off[0] += n
        return small_sum[off[0] - n:off[0]]

    g_b_ada = take(N_MOD * D).reshape(1, -1)
    g_ffn1_norm, g_mix_norm, g_ffn2_norm = take(D).reshape(1, D), take(D).reshape(1, D), take(D).reshape(1, D)
    g_q_norm, g_k_norm = take(LANES).reshape(1, -1), take(LANES).reshape(1, -1)
    g_a_log, g_dt_bias = take(LANES)[:HD].reshape(1, HD), take(LANES)[:HD].reshape(1, HD)
    g_dn = take(LANES).reshape(1, -1)
    g_conv_full = take(CONV_WIDTH * conv_width).reshape(CONV_WIDTH, conv_width)
    g_conv = lax.dynamic_slice(g_conv_full, (0, chip * conv_shard), (CONV_WIDTH, conv_shard))

    res = {}

    def upd(name, w, pair, col, m, v):
        g, d, nm, nv = _adamw_halves(w[0], pair[0], pair[1], col, m[0], v[0], "adamw_" + name)
        res[name] = (g[None], d[None], nm[None], nv[None])

    upd("ffn1_w_gate", ffn1_w_gate, h_gu1, 0, m_ffn1_w_gate, v_ffn1_w_gate)
    upd("ffn1_w_up", ffn1_w_up, h_gu1, 1, m_ffn1_w_up, v_ffn1_w_up)
    upd("ffn1_w_down", ffn1_w_down, h_d1, 0, m_ffn1_w_down, v_ffn1_w_down)
    upd("w_in", w_in, h_in, 0, m_w_in, v_w_in)
    upd("w_out", w_out, h_out, 0, m_w_out, v_w_out)
    upd("ffn2_w_gate", ffn2_w_gate, h_gu2, 0, m_ffn2_w_gate, v_ffn2_w_gate)
    upd("ffn2_w_up", ffn2_w_up, h_gu2, 1, m_ffn2_w_up, v_ffn2_w_up)
    upd("ffn2_w_down", ffn2_w_down, h_d2, 0, m_ffn2_w_down, v_ffn2_w_down)
    d_cv, nm_cv, nv_cv = _adamw(conv_w[0], g_conv, m_conv_w[0], v_conv_w[0], "adamw_conv_w")
    res["conv_w"] = (g_conv[None], d_cv[None], nm_cv[None], nv_cv[None])

    dmod_mine = lax.dynamic_slice(dmod_all, (0, chip * n_mod_shard), (8, n_mod_shard))
    g, d, nm, nv = _adamw_outer(w_ada[0], m_w_ada[0], v_w_ada[0], jnp.transpose(c_all), dmod_mine, "adamw_w_ada")
    res["w_ada"] = (g[None], d[None], nm[None], nv[None])

    rep = [("b_ada", b_ada, g_b_ada, m_b_ada, v_b_ada), ("ffn1_norm", ffn1_norm, g_ffn1_norm, m_ffn1_norm, v_ffn1_norm),
           ("mix_norm", mix_norm, g_mix_norm, m_mix_norm, v_mix_norm), ("ffn2_norm", ffn2_norm, g_ffn2_norm, m_ffn2_norm, v_ffn2_norm),
           ("q_norm", q_norm, g_q_norm, m_q_norm, v_q_norm), ("k_norm", k_norm, g_k_norm, m_k_norm, v_k_norm),
           ("a_log", a_log, g_a_log, m_a_log, v_a_log), ("dt_bias", dt_bias, g_dt_bias, m_dt_bias, v_dt_bias),
           ("delta_out_norm", delta_out_norm, g_dn, m_delta_out_norm, v_delta_out_norm)]
    n_rep = sum(-(-r[1].shape[1] // LANES) * LANES for r in rep)
    cols_rep = -(-n_rep // (8 * LANES)) * LANES

    def pack_rep(idx):
        return _flat_pad([jnp.pad(r[idx], ((0, 0), (0, -r[idx].shape[1] % LANES))) for r in rep], 8, cols_rep)

    d_rep, nm_rep, nv_rep = [a.reshape(-1) for a in _adamw(pack_rep(1), pack_rep(2), pack_rep(3), pack_rep(4), "adamw_small")]
    o2 = 0
    for name, w, g, _, _ in rep:
        n = w.shape[1]
        res[name] = (g, d_rep[o2:o2 + n].reshape(1, n), nm_rep[o2:o2 + n].reshape(1, n), nv_rep[o2:o2 + n].reshape(1, n))
        o2 += -(-n // LANES) * LANES

    order = ["w_ada", "b_ada", "ffn1_norm", "ffn1_w_gate", "ffn1_w_up", "ffn1_w_down", "mix_norm", "w_in", "conv_w", "q_norm",
             "k_norm", "a_log", "dt_bias", "delta_out_norm", "w_out", "ffn2_norm", "ffn2_w_gate", "ffn2_w_up", "ffn2_w_down"]
    return (loss, dx0[None], *[res[n][0] for n in order], *[res[n][1] for n in order],
            *[res[n][2] for n in order], *[res[n][3] for n in order])
```

```python
import functools
import math

import jax
import jax.numpy as jnp
from jax import lax
from jax.experimental import pallas as pl
from jax.experimental.pallas import tpu as pltpu

F32 = jnp.float32
BF16 = jnp.bfloat16
MESH = pl.DeviceIdType.MESH

HEAD_DIM = 128
N_ATTN_HEADS = 8
N_DELTA_HEADS = 8
DILATED_PATTERNS = ((128, 1), (512, 4), (2048, 16))
ROPE_THETA = 500000.0
ROPE_DIM = HEAD_DIM // 4
CONV_WIDTH = 4
CHUNK = 64
NORM_EPS = 1e-6
N_MOD = 9
ADAM_LR = 0.001
ADAM_B1 = 0.9
ADAM_B2 = 0.999
ADAM_EPS = 1e-08
ADAM_WD = 0.01
ADAM_STEP = 10

LANES = 128
VMEM_LIMIT = 56 * 1024 * 1024
ATTN_TILE = 512
HIGHEST = lax.Precision.HIGHEST


def _cparams(sem=None):
    return pltpu.CompilerParams(dimension_semantics=sem, vmem_limit_bytes=VMEM_LIMIT)


def _pick(dim, pref):
    if dim <= pref:
        return dim
    t = (pref // LANES) * LANES
    while t >= LANES:
        if dim % t == 0:
            return t
        t -= LANES
    return dim


def _sigmoid(x):
    return 1.0 / (1.0 + jnp.exp(-x))


def _silu(x):
    return x * _sigmoid(x)


def _softplus(x):
    return jnp.maximum(x, 0.0) + jnp.log(1.0 + jnp.exp(-jnp.abs(x)))


def _rms(x, gain):
    return x * lax.rsqrt(jnp.mean(x * x, axis=-1, keepdims=True) + NORM_EPS) * gain


def _l2(x):
    return x * lax.rsqrt(jnp.sum(x * x, axis=-1, keepdims=True) + NORM_EPS)


def _modulate(x, gain, shift, scale):
    return _rms(x, gain) * (1.0 + scale) + shift


def _dot(a, b):
    return lax.dot_general(a, b, (((1,), (0,)), ((), ())), precision=HIGHEST, preferred_element_type=F32)


def _bdot(a, b, dims):
    return lax.dot_general(a.astype(BF16), b.astype(BF16), (dims, ((), ())), preferred_element_type=F32)


_NN, _NT, _TN = ((1,), (0,)), ((1,), (1,)), ((0,), (0,))
HIGH = lax.Precision.HIGH


def _dot3(a, b, dims=_NN):
    return lax.dot_general(a, b, (dims, ((), ())), precision=HIGH, preferred_element_type=F32)


@jax.custom_vjp
def _mm_nn(a, b):
    return _bdot(a, b, _NN)


_mm_nn.defvjp(lambda a, b: (_bdot(a, b, _NN), (a, b)),
              lambda res, g: (_bdot(g, res[1], _NT), _bdot(res[0], g, _TN)))


@jax.custom_vjp
def _mm_nt(a, b):
    return _bdot(a, b, _NT)


_mm_nt.defvjp(lambda a, b: (_bdot(a, b, _NT), (a, b)),
              lambda res, g: (_bdot(g, res[1], _NN), _bdot(g, res[0], _TN)))


@jax.custom_vjp
def _mm_tn(a, b):
    return _bdot(a, b, _TN)


_mm_tn.defvjp(lambda a, b: (_bdot(a, b, _TN), (a, b)),
              lambda res, g: (_bdot(res[1], g, _NT), _bdot(res[0], g, _NN)))


@jax.custom_vjp
def _tri_inv_saved(a, t_inv):
    return t_inv


_tri_inv_saved.defvjp(lambda a, t_inv: (t_inv, t_inv),
                      lambda t_inv, g: (-_dot3(t_inv, _dot3(g, t_inv, _NT), _TN), jnp.zeros_like(t_inv)))


_MM_DIMS = {"nn": ((1,), (0,)), "nt": ((1,), (1,)), "tn": ((0,), (0,))}


def _matmul(a, b, mode, out_dtype, name, tm=1024, tn=1024, tk=1024, col_slabs=False, epilogue=None, after=None):
    if mode == "nn":
        (M, K), (_, N) = a.shape, b.shape
    elif mode == "nt":
        (M, K), (N, _) = a.shape, b.shape
    else:
        (K, M), (_, N) = a.shape, b.shape
    tm, tn, tk = _pick(M, tm), _pick(N, tn), _pick(K, tk)
    nk = K // tk
    dims = _MM_DIMS[mode]
    epi_fn, extra_in, outs = epilogue if epilogue is not None else (None, [], None)
    if outs is None:
        if col_slabs:
            outs = [((N // tn, M, tn), out_dtype, (None, tm, tn), lambda i, j: (j, i, 0))]
        else:
            outs = [((M, N), out_dtype, (tm, tn), lambda i, j: (i, j))]
    n_in, n_out = len(extra_in), len(outs)

    tied = [] if after is None else [after]

    def body(a_ref, b_ref, *rest):
        rest = rest[len(tied):]
        in_refs, out_refs = rest[:n_in], rest[n_in:n_in + n_out]
        k = pl.program_id(2)
        p = _bdot(a_ref[...], b_ref[...], dims)

        def finish(tile):
            if epi_fn is None:
                out_refs[0][...] = tile.astype(out_dtype)
            else:
                epi_fn(tile, in_refs, out_refs)

        if nk == 1:
            finish(p)
        else:
            acc_ref = rest[-1]

            @pl.when(k == 0)
            def _():
                acc_ref[...] = p

            @pl.when((k > 0) & (k < nk - 1))
            def _():
                acc_ref[...] += p

            @pl.when(k == nk - 1)
            def _():
                finish(acc_ref[...] + p)

    def ij(index_map):
        return lambda i, j, k: index_map(i, j)

    a_spec = pl.BlockSpec((tk, tm), lambda i, j, k: (k, i)) if mode == "tn" else pl.BlockSpec((tm, tk), lambda i, j, k: (i, k))
    b_spec = pl.BlockSpec((tn, tk), lambda i, j, k: (j, k)) if mode == "nt" else pl.BlockSpec((tk, tn), lambda i, j, k: (k, j))
    res = pl.pallas_call(
        body, name=name, grid=(M // tm, N // tn, nk),
        in_specs=[a_spec, b_spec] + [pl.BlockSpec(memory_space=pl.ANY)] * len(tied) + [pl.BlockSpec(blk, ij(im)) for _, blk, im in extra_in],
        out_specs=[pl.BlockSpec(blk, ij(im)) for _, _, blk, im in outs],
        out_shape=[jax.ShapeDtypeStruct(shp, dt) for shp, dt, _, _ in outs],
        scratch_shapes=[pltpu.VMEM((tm, tn), F32)] if nk > 1 else [],
        compiler_params=_cparams(("parallel", "parallel", "arbitrary")),
    )(a, b, *tied, *[arr for arr, _, _ in extra_in])
    return res if epilogue is not None else res[0]


def _row_spec(tr, d):
    return pl.BlockSpec((tr, d), lambda i: (i, 0))


def _vec_spec(d):
    return pl.BlockSpec((1, d), lambda i: (0, 0))


def _pre_fwd(x, gain, shift, scale, name):
    S, D = x.shape
    tr = _pick(S, 256)

    def body(x_ref, g_ref, sh_ref, sc_ref, h_ref):
        h_ref[...] = _modulate(x_ref[...], g_ref[...], sh_ref[...], sc_ref[...]).astype(BF16)

    return pl.pallas_call(
        body, name=name, grid=(S // tr,),
        in_specs=[_row_spec(tr, D), _vec_spec(D), _vec_spec(D), _vec_spec(D)],
        out_specs=_row_spec(tr, D), out_shape=jax.ShapeDtypeStruct((S, D), BF16),
        compiler_params=_cparams(("parallel",)),
    )(x, gain, shift, scale)


def _pre_bwd(x, gain, shift, scale, dh, dx_in, name):
    S, D = x.shape
    tr = _pick(S, 256)

    def body(x_ref, g_ref, sh_ref, sc_ref, dh_ref, dxin_ref, dx_ref, dg_ref, dsh_ref, dsc_ref):
        _, vjp = jax.vjp(_modulate, x_ref[...], g_ref[...], sh_ref[...], sc_ref[...])
        dx, dg, dsh, dsc = vjp(dh_ref[...])
        dx_ref[...] = dxin_ref[...] + dx

        @pl.when(pl.program_id(0) == 0)
        def _():
            dg_ref[...] = jnp.zeros_like(dg_ref)
            dsh_ref[...] = jnp.zeros_like(dsh_ref)
            dsc_ref[...] = jnp.zeros_like(dsc_ref)

        dg_ref[...] += dg
        dsh_ref[...] += dsh
        dsc_ref[...] += dsc

    vec = jax.ShapeDtypeStruct((1, D), F32)
    return pl.pallas_call(
        body, name=name, grid=(S // tr,),
        in_specs=[_row_spec(tr, D), _vec_spec(D), _vec_spec(D), _vec_spec(D), _row_spec(tr, D), _row_spec(tr, D)],
        out_specs=[_row_spec(tr, D), _vec_spec(D), _vec_spec(D), _vec_spec(D)],
        out_shape=[jax.ShapeDtypeStruct((S, D), F32), vec, vec, vec],
        compiler_params=_cparams(("arbitrary",)),
    )(x, gain, shift, scale, dh, dx_in)


def _residual_fwd(x, gate, f, coef, name):
    S, D = x.shape
    tr = _pick(S, 256)

    def body(x_ref, g_ref, f_ref, o_ref):
        o_ref[...] = x_ref[...] + coef * g_ref[...] * f_ref[...]

    return pl.pallas_call(
        body, name=name, grid=(S // tr,),
        in_specs=[_row_spec(tr, D), _vec_spec(D), _row_spec(tr, D)],
        out_specs=_row_spec(tr, D), out_shape=jax.ShapeDtypeStruct((S, D), F32),
        compiler_params=_cparams(("parallel",)),
    )(x, gate, f)


def _residual_bwd(gate, f, dxn, coef, name):
    S, D = f.shape
    tr = _pick(S, 256)

    def body(g_ref, f_ref, d_ref, df_ref, dg_ref):
        d = d_ref[...]
        df_ref[...] = (coef * g_ref[...] * d).astype(BF16)

        @pl.when(pl.program_id(0) == 0)
        def _():
            dg_ref[...] = jnp.zeros_like(dg_ref)

        dg_ref[...] += jnp.sum(coef * f_ref[...] * d, axis=0, keepdims=True)

    return pl.pallas_call(
        body, name=name, grid=(S // tr,),
        in_specs=[_vec_spec(D), _row_spec(tr, D), _row_spec(tr, D)],
        out_specs=[_row_spec(tr, D), _vec_spec(D)],
        out_shape=[jax.ShapeDtypeStruct((S, D), BF16), jax.ShapeDtypeStruct((1, D), F32)],
        compiler_params=_cparams(("arbitrary",)),
    )(gate, f, dxn)


def _swiglu_fn(a, b):
    return _silu(a) * b


def _loss_head(y, target, name):
    S, D = y.shape
    tr = _pick(S, 256)

    def body(y_ref, t_ref, l_ref, dy_ref):
        e = y_ref[...] - t_ref[...]
        dy_ref[...] = e * (1.0 / D)

        @pl.when(pl.program_id(0) == 0)
        def _():
            l_ref[...] = jnp.zeros_like(l_ref)

        l_ref[...] += jnp.sum(jnp.sum(e * e, axis=-1, keepdims=True), axis=0, keepdims=True) * (0.5 / D)

    return pl.pallas_call(
        body, name=name, grid=(S // tr,),
        in_specs=[_row_spec(tr, D), _row_spec(tr, D)],
        out_specs=[pl.BlockSpec((1, 1), lambda i: (0, 0)), _row_spec(tr, D)],
        out_shape=[jax.ShapeDtypeStruct((1, 1), F32), jax.ShapeDtypeStruct((S, D), F32)],
        compiler_params=_cparams(("arbitrary",)),
    )(y, target)


def _rope(y, cc, s1, s2):
    return y * cc + pltpu.roll(y, LANES - ROPE_DIM // 2, 1) * s1 + pltpu.roll(y, ROPE_DIM // 2, 1) * s2


def _rope_t(d, cc, s1, s2):
    return d * cc + pltpu.roll(d * s1, ROPE_DIM // 2, 1) + pltpu.roll(d * s2, LANES - ROPE_DIM // 2, 1)


def _head_spec(tr, col0):
    return pl.BlockSpec((tr, HEAD_DIM), lambda i, h: (i, col0 + h))


def _tab_spec(tr):
    return pl.BlockSpec((tr, HEAD_DIM), lambda i, h: (i, 0))


def _gain_spec():
    return pl.BlockSpec((1, HEAD_DIM), lambda i, h: (0, 0))


def _attn_prep_fwd(proj, q_gain, k_gain, cc, s1, s2, name):
    S = proj.shape[0]
    H = N_ATTN_HEADS
    tr = _pick(S, 512)

    def body(q_ref, k_ref, qg_ref, kg_ref, cc_ref, s1_ref, s2_ref, qo_ref, ko_ref):
        cc, s1, s2 = cc_ref[...], s1_ref[...], s2_ref[...]
        qo_ref[...] = _rope(_rms(q_ref[...], qg_ref[...]), cc, s1, s2)
        ko_ref[...] = _rope(_rms(k_ref[...], kg_ref[...]), cc, s1, s2)

    out = jax.ShapeDtypeStruct((S, H * HEAD_DIM), F32)
    return pl.pallas_call(
        body, name=name, grid=(S // tr, H),
        in_specs=[_head_spec(tr, 0), _head_spec(tr, H), _gain_spec(), _gain_spec(), _tab_spec(tr), _tab_spec(tr), _tab_spec(tr)],
        out_specs=[_head_spec(tr, 0), _head_spec(tr, 0)], out_shape=[out, out],
        compiler_params=_cparams(("parallel", "parallel")),
    )(proj, proj, q_gain, k_gain, cc, s1, s2)


def _attn_prep_bwd(proj, q_gain, k_gain, cc, s1, s2, near, far, name):
    S = proj.shape[0]
    H = N_ATTN_HEADS
    tr = _pick(S, 512)

    def body(q_ref, k_ref, qg_ref, kg_ref, cc_ref, s1_ref, s2_ref, dqn_ref, dkn_ref, dvn_ref, dqf_ref, dkf_ref, dvf_ref,
             dpq_ref, dpk_ref, dpv_ref, dqg_ref, dkg_ref):
        cc, s1, s2 = cc_ref[...], s1_ref[...], s2_ref[...]
        dq_ref = dqn_ref[...] + dqf_ref[...]
        dk_ref = dkn_ref[...] + dkf_ref[...]
        dpv_ref[...] = (dvn_ref[...] + dvf_ref[...]).astype(BF16)

        @pl.when((pl.program_id(0) == 0) & (pl.program_id(1) == 0))
        def _():
            dqg_ref[...] = jnp.zeros_like(dqg_ref)
            dkg_ref[...] = jnp.zeros_like(dkg_ref)

        _, vjp_q = jax.vjp(_rms, q_ref[...], qg_ref[...])
        dxq, dgq = vjp_q(_rope_t(dq_ref[...], cc, s1, s2))
        _, vjp_k = jax.vjp(_rms, k_ref[...], kg_ref[...])
        dxk, dgk = vjp_k(_rope_t(dk_ref[...], cc, s1, s2))
        dpq_ref[...] = dxq.astype(BF16)
        dpk_ref[...] = dxk.astype(BF16)
        dqg_ref[...] += dgq
        dkg_ref[...] += dgk

    out = jax.ShapeDtypeStruct((S, H * HEAD_DIM), BF16)
    gout = jax.ShapeDtypeStruct((1, HEAD_DIM), F32)
    return pl.pallas_call(
        body, name=name, grid=(S // tr, H),
        in_specs=[_head_spec(tr, 0), _head_spec(tr, H), _gain_spec(), _gain_spec(), _tab_spec(tr), _tab_spec(tr), _tab_spec(tr)]
        + [_head_spec(tr, 0)] * 6,
        out_specs=[_head_spec(tr, 0)] * 3 + [_gain_spec(), _gain_spec()], out_shape=[out, out, out, gout, gout],
        compiler_params=_cparams(("arbitrary", "arbitrary")),
    )(proj, proj, q_gain, k_gain, cc, s1, s2, *near, *far)


def _multiplicity(j, t):
    ti = lax.broadcasted_iota(jnp.int32, (t, t), 0)
    si = lax.broadcasted_iota(jnp.int32, (t, t), 1)
    delta = j * t + ti - si
    cnt = jnp.zeros((t, t), F32)
    for window, dil in NEAR_PATTERNS:
        ok = (delta >= 0) & ((delta & (dil - 1)) == 0) & (delta <= window)
        cnt = cnt + ok.astype(F32)
    return cnt


_NEG = -1e30
NEAR_PATTERNS = DILATED_PATTERNS[:2]
NEAR_WINDOW = max(w for w, _ in NEAR_PATTERNS)
FAR_WINDOW, FAR_DIL = DILATED_PATTERNS[2]


def _log_multiplicity_table(t):
    cnt = jnp.stack([_multiplicity(j, t) for j in range(NEAR_WINDOW // t + 1)])
    return jnp.where(cnt > 0.0, jnp.log(jnp.maximum(cnt, 1.0)), _NEG)


def _bias_spec(t):
    return pl.BlockSpec((NEAR_WINDOW // t + 1, t, t), lambda h, i, j: (0, 0, 0))


def _attn_fwd(q, k, proj, v_col0, bias, name):
    S = q.shape[0]
    H = N_ATTN_HEADS
    t = _pick(S, ATTN_TILE)
    nq = S // t
    nj = NEAR_WINDOW // t + 1
    scale = HEAD_DIM ** -0.5

    def body(q_ref, k_ref, v_ref, b_ref, o_ref, lse_ref, m_sc, l_sc, acc_sc):
        qb, j = pl.program_id(1), pl.program_id(2)

        @pl.when(j == 0)
        def _():
            m_sc[...] = jnp.full_like(m_sc, _NEG)
            l_sc[...] = jnp.zeros_like(l_sc)
            acc_sc[...] = jnp.zeros_like(acc_sc)

        @pl.when(qb - j >= 0)
        def _():
            s = _bdot(q_ref[...], k_ref[...], ((1,), (1,))) * scale + b_ref[j]
            m_prev = m_sc[...]
            m_new = jnp.maximum(m_prev, jnp.max(s, axis=-1, keepdims=True))
            alpha = jnp.exp(m_prev - m_new)
            p = jnp.exp(s - m_new)
            l_sc[...] = alpha * l_sc[...] + jnp.sum(p, axis=-1, keepdims=True)
            acc_sc[...] = alpha * acc_sc[...] + _bdot(p, v_ref[...], ((1,), (0,)))
            m_sc[...] = m_new

        @pl.when(j == nj - 1)
        def _():
            o_ref[...] = acc_sc[...] / l_sc[...]
            lse_ref[...] = jnp.broadcast_to(m_sc[...] + jnp.log(l_sc[...]), (t, HEAD_DIM))

    qspec = pl.BlockSpec((t, HEAD_DIM), lambda h, i, j: (i, h))
    kspec = pl.BlockSpec((t, HEAD_DIM), lambda h, i, j: (jnp.maximum(i - j, 0), h))
    vspec = pl.BlockSpec((t, HEAD_DIM), lambda h, i, j: (jnp.maximum(i - j, 0), v_col0 + h))
    return pl.pallas_call(
        body, name=name, grid=(H, nq, nj),
        in_specs=[qspec, kspec, vspec, _bias_spec(t)], out_specs=[qspec, qspec],
        out_shape=[jax.ShapeDtypeStruct((S, H * HEAD_DIM), F32), jax.ShapeDtypeStruct((S, H * HEAD_DIM), F32)],
        scratch_shapes=[pltpu.VMEM((t, 1), F32), pltpu.VMEM((t, 1), F32), pltpu.VMEM((t, HEAD_DIM), F32)],
        compiler_params=_cparams(("parallel", "parallel", "arbitrary")),
    )(q, k, proj, bias)


def _far_rows(r, n):
    return pl.ds(r, n, stride=FAR_DIL)


def _far_band_bias(n):
    i = lax.broadcasted_iota(jnp.int32, (n, n), 0)
    j = lax.broadcasted_iota(jnp.int32, (n, n), 1)
    return jnp.where((i >= j) & (i - j <= FAR_WINDOW // FAR_DIL), 0.0, _NEG)


def _col_spec(S, col0):
    return pl.BlockSpec((S, HEAD_DIM), lambda h: (0, col0 + h))


def _attn_far_fwd(q, k, proj, v_col0, o_near, lse_near, name):
    S = q.shape[0]
    H = N_ATTN_HEADS
    n = S // FAR_DIL
    scale = HEAD_DIM ** -0.5

    def body(q_ref, k_ref, v_ref, on_ref, ln_ref, o_ref, ob_ref, lse_ref):
        bias = _far_band_bias(n)
        for r in range(FAR_DIL):
            rows = _far_rows(r, n)
            s = _bdot(q_ref[rows, :], k_ref[rows, :], _NT) * scale + bias
            m = jnp.max(s, axis=-1, keepdims=True)
            p = jnp.exp(s - m)
            l = jnp.sum(p, axis=-1, keepdims=True)
            o_far = _bdot(p, v_ref[rows, :], _NN) / l
            lse_far = m + jnp.log(l)
            lse_near = jnp.max(ln_ref[rows, :], axis=-1, keepdims=True)
            top = jnp.maximum(lse_near, lse_far)
            lse = top + jnp.log(jnp.exp(lse_near - top) + jnp.exp(lse_far - top))
            o_ref[rows, :] = jnp.exp(lse_near - lse) * on_ref[rows, :] + jnp.exp(lse_far - lse) * o_far
            lse_ref[rows, :] = jnp.broadcast_to(lse, (n, HEAD_DIM))
        ob_ref[...] = o_ref[...].astype(BF16)

    cs = _col_spec(S, 0)
    return pl.pallas_call(
        body, name=name, grid=(H,),
        in_specs=[cs, cs, _col_spec(S, v_col0), cs, cs], out_specs=[cs, cs, cs],
        out_shape=[jax.ShapeDtypeStruct((S, H * HEAD_DIM), F32), jax.ShapeDtypeStruct((S, H * HEAD_DIM), BF16),
                   jax.ShapeDtypeStruct((S, H * HEAD_DIM), F32)],
        compiler_params=_cparams(("parallel",)),
    )(q, k, proj, o_near, lse_near)


def _attn_far_bwd(q, k, proj, v_col0, o, lse, do, do_col0, name):
    S = q.shape[0]
    H = N_ATTN_HEADS
    n = S // FAR_DIL
    scale = HEAD_DIM ** -0.5

    def body(q_ref, k_ref, v_ref, o_ref, lse_ref, do_ref, dq_ref, dk_ref, dv_ref):
        bias = _far_band_bias(n)
        for r in range(FAR_DIL):
            rows = _far_rows(r, n)
            q, k, v, do = q_ref[rows, :], k_ref[rows, :], v_ref[rows, :], do_ref[rows, :]
            dsum = jnp.sum(do * o_ref[rows, :], axis=-1, keepdims=True)
            lse = jnp.max(lse_ref[rows, :], axis=-1, keepdims=True)
            p = jnp.exp(_bdot(q, k, _NT) * scale + bias - lse)
            ds = p * (_bdot(do, v, _NT) - dsum)
            dq_ref[rows, :] = _bdot(ds, k, _NN) * scale
            dk_ref[rows, :] = _bdot(ds, q, _TN) * scale
            dv_ref[rows, :] = _bdot(p, do, _TN)

    cs = _col_spec(S, 0)
    shp = jax.ShapeDtypeStruct((S, H * HEAD_DIM), F32)
    return pl.pallas_call(
        body, name=name, grid=(H,),
        in_specs=[cs, cs, _col_spec(S, v_col0), cs, cs, _col_spec(S, do_col0)], out_specs=[cs, cs, cs], out_shape=[shp, shp, shp],
        compiler_params=_cparams(("parallel",)),
    )(q, k, proj, o, lse, do)


def _attn_probs(q, k, lse, bias_tile, scale):
    return jnp.exp(_bdot(q, k, ((1,), (1,))) * scale + bias_tile - lse)


def _attn_bwd_dq(q, k, proj, v_col0, o, lse, do, do_col0, bias, name):
    S = q.shape[0]
    H = N_ATTN_HEADS
    t = _pick(S, ATTN_TILE)
    nq = S // t
    nj = NEAR_WINDOW // t + 1
    scale = HEAD_DIM ** -0.5

    def body(q_ref, k_ref, v_ref, o_ref, lse_ref, do_ref, b_ref, dq_ref, acc_sc):
        qb, j = pl.program_id(1), pl.program_id(2)

        @pl.when(j == 0)
        def _():
            acc_sc[...] = jnp.zeros_like(acc_sc)

        @pl.when(qb - j >= 0)
        def _():
            do = do_ref[...]
            dsum = jnp.sum(do * o_ref[...], axis=-1, keepdims=True)
            lse = jnp.max(lse_ref[...], axis=-1, keepdims=True)
            p = _attn_probs(q_ref[...], k_ref[...], lse, b_ref[j], scale)
            dp = _bdot(do, v_ref[...], ((1,), (1,)))
            ds = p * (dp - dsum)
            acc_sc[...] += _bdot(ds, k_ref[...], ((1,), (0,))) * scale

        @pl.when(j == nj - 1)
        def _():
            dq_ref[...] = acc_sc[...]

    qspec = pl.BlockSpec((t, HEAD_DIM), lambda h, i, j: (i, h))
    dospec = pl.BlockSpec((t, HEAD_DIM), lambda h, i, j: (i, do_col0 + h))
    kspec = pl.BlockSpec((t, HEAD_DIM), lambda h, i, j: (jnp.maximum(i - j, 0), h))
    vspec = pl.BlockSpec((t, HEAD_DIM), lambda h, i, j: (jnp.maximum(i - j, 0), v_col0 + h))
    return pl.pallas_call(
        body, name=name, grid=(H, nq, nj),
        in_specs=[qspec, kspec, vspec, qspec, qspec, dospec, _bias_spec(t)], out_specs=qspec,
        out_shape=jax.ShapeDtypeStruct((S, H * HEAD_DIM), F32),
        scratch_shapes=[pltpu.VMEM((t, HEAD_DIM), F32)],
        compiler_params=_cparams(("parallel", "parallel", "arbitrary")),
    )(q, k, proj, o, lse, do, bias)


def _attn_bwd_dkv(q, k, proj, v_col0, o, lse, do, do_col0, bias, name):
    S = q.shape[0]
    H = N_ATTN_HEADS
    t = _pick(S, ATTN_TILE)
    nq = S // t
    nj = NEAR_WINDOW // t + 1
    scale = HEAD_DIM ** -0.5

    def body(q_ref, k_ref, v_ref, o_ref, lse_ref, do_ref, b_ref, dk_ref, dv_ref, dk_sc, dv_sc):
        kb, j = pl.program_id(1), pl.program_id(2)

        @pl.when(j == 0)
        def _():
            dk_sc[...] = jnp.zeros_like(dk_sc)
            dv_sc[...] = jnp.zeros_like(dv_sc)

        @pl.when(kb + j < nq)
        def _():
            do = do_ref[...]
            dsum = jnp.sum(do * o_ref[...], axis=-1, keepdims=True)
            lse = jnp.max(lse_ref[...], axis=-1, keepdims=True)
            p = _attn_probs(q_ref[...], k_ref[...], lse, b_ref[j], scale)
            dp = _bdot(do, v_ref[...], ((1,), (1,)))
            ds = p * (dp - dsum)
            dv_sc[...] += _bdot(p, do, ((0,), (0,)))
            dk_sc[...] += _bdot(ds, q_ref[...], ((0,), (0,))) * scale

        @pl.when(j == nj - 1)
        def _():
            dk_ref[...] = dk_sc[...]
            dv_ref[...] = dv_sc[...]

    def qrow(h, i, j):
        return jnp.minimum(i + j, nq - 1)

    qspec = pl.BlockSpec((t, HEAD_DIM), lambda h, i, j: (qrow(h, i, j), h))
    dospec = pl.BlockSpec((t, HEAD_DIM), lambda h, i, j: (qrow(h, i, j), do_col0 + h))
    kspec = pl.BlockSpec((t, HEAD_DIM), lambda h, i, j: (i, h))
    vspec = pl.BlockSpec((t, HEAD_DIM), lambda h, i, j: (i, v_col0 + h))
    return pl.pallas_call(
        body, name=name, grid=(H, nq, nj),
        in_specs=[qspec, kspec, vspec, qspec, qspec, dospec, _bias_spec(t)], out_specs=[kspec, kspec],
        out_shape=[jax.ShapeDtypeStruct((S, H * HEAD_DIM), F32), jax.ShapeDtypeStruct((S, H * HEAD_DIM), F32)],
        scratch_shapes=[pltpu.VMEM((t, HEAD_DIM), F32), pltpu.VMEM((t, HEAD_DIM), F32)],
        compiler_params=_cparams(("parallel", "parallel", "arbitrary")),
    )(q, k, proj, o, lse, do, bias)


def _conv_pre(x_ref, w_ref):
    x = x_ref[...]
    rows = lax.broadcasted_iota(jnp.int32, x.shape, 0)
    shifted = [x]
    acc = x * w_ref[pl.ds(CONV_WIDTH - 1, 1), :]
    for sft in range(1, CONV_WIDTH):
        xs = jnp.where(rows >= sft, pltpu.roll(x, sft, 0), 0.0)
        shifted.append(xs)
        acc = acc + xs * w_ref[pl.ds(CONV_WIDTH - 1 - sft, 1), :]
    return acc, shifted


def _conv_fwd(proj, col0, width, w, name):
    S = proj.shape[0]

    def body(x_ref, w_ref, y_ref):
        acc, _ = _conv_pre(x_ref, w_ref)
        y_ref[...] = _silu(acc)

    return pl.pallas_call(
        body, name=name, grid=(width // LANES,),
        in_specs=[pl.BlockSpec((S, LANES), lambda c: (0, col0 + c)), pl.BlockSpec((CONV_WIDTH, LANES), lambda c: (0, c))],
        out_specs=pl.BlockSpec((S, LANES), lambda c: (0, c)),
        out_shape=jax.ShapeDtypeStruct((S, width), F32),
        compiler_params=_cparams(("parallel",)),
    )(proj, w)


def _conv_bwd(proj, col0, width, w, dy, name):
    S = proj.shape[0]

    def body(x_ref, w_ref, d_ref, dx_ref, dw_ref):
        acc, shifted = _conv_pre(x_ref, w_ref)
        sig = _sigmoid(acc)
        da = d_ref[...] * (sig * (1.0 + acc * (1.0 - sig)))
        rows = lax.broadcasted_iota(jnp.int32, da.shape, 0)
        dx = da * w_ref[pl.ds(CONV_WIDTH - 1, 1), :]
        dw_ref[pl.ds(CONV_WIDTH - 1, 1), :] = jnp.sum(da * shifted[0], axis=0, keepdims=True)
        for sft in range(1, CONV_WIDTH):
            back = jnp.where(rows < S - sft, pltpu.roll(da, S - sft, 0), 0.0)
            dx = dx + back * w_ref[pl.ds(CONV_WIDTH - 1 - sft, 1), :]
            dw_ref[pl.ds(CONV_WIDTH - 1 - sft, 1), :] = jnp.sum(da * shifted[sft], axis=0, keepdims=True)
        dx_ref[...] = dx.astype(BF16)

    return pl.pallas_call(
        body, name=name, grid=(width // LANES,),
        in_specs=[pl.BlockSpec((S, LANES), lambda c: (0, col0 + c)), pl.BlockSpec((CONV_WIDTH, LANES), lambda c: (0, c)),
                  pl.BlockSpec((S, LANES), lambda c: (0, c))],
        out_specs=[pl.BlockSpec((S, LANES), lambda c: (0, c)), pl.BlockSpec((CONV_WIDTH, LANES), lambda c: (0, c))],
        out_shape=[jax.ShapeDtypeStruct((S, width), BF16), jax.ShapeDtypeStruct((CONV_WIDTH, width), F32)],
        compiler_params=_cparams(("parallel",)),
    )(proj, w, dy)


PREP_CHUNKS = 8


def _chunks_prep(qraws, kraws, vs, abs_, alog_row, dtb_row, mask_g, mask_b, t_saved=None):
    n = len(qraws)
    c = qraws[0].shape[0]
    mm_nt, mm_nn = (_mm_nt, _mm_nn) if t_saved is not None else (lambda p, r: _bdot(p, r, _NT), lambda p, r: _bdot(p, r, _NN))
    row = lax.broadcasted_iota(jnp.int32, (c, c), 0)
    col = lax.broadcasted_iota(jnp.int32, (c, c), 1)
    tril, strict, eye = row >= col, row > col, row == col
    eyef = eye.astype(F32)
    neg_rate = -jnp.exp(alog_row)
    q, k, beta, gc_col, gamma, kb, g_last = [], [], [], [], [], [], []
    for i in range(n):
        q.append(_l2(qraws[i]) * (HEAD_DIM ** -0.5))
        k.append(_l2(kraws[i]))
        gfull = neg_rate * _softplus(abs_[i] + dtb_row)
        g = jnp.sum(jnp.where(mask_g, gfull, 0.0), axis=-1, keepdims=True)
        beta.append(jnp.sum(jnp.where(mask_b, _sigmoid(abs_[i]), 0.0), axis=-1, keepdims=True))
        g_row = jnp.sum(jnp.where(eye, g, 0.0), axis=0, keepdims=True)
        gc_col.append(jnp.sum(jnp.where(tril, g_row, 0.0), axis=1, keepdims=True))
        gc_row = jnp.sum(jnp.where(row <= col, g, 0.0), axis=0, keepdims=True)
        gamma.append(jnp.where(tril, jnp.exp(jnp.where(tril, gc_col[i] - gc_row, 0.0)), 0.0))
        kb.append(k[i] * beta[i])
        g_last.append(jnp.sum(g, axis=0, keepdims=True))
    a = [jnp.where(strict, mm_nt(kb[i], k[i]) * gamma[i], 0.0) for i in range(n)]
    if t_saved is None:
        t_inv = [eyef - a[i] for i in range(n)]
        p = a
        for _ in range(int(math.log2(c)) - 1):
            p = [_dot3(p[i], p[i]) for i in range(n)]
            t_inv = [_dot3(t_inv[i], eyef + p[i]) for i in range(n)]
    else:
        t_inv = [_tri_inv_saved(a[i], t_saved[i]) for i in range(n)]
    egc = [jnp.exp(gc_col[i]) for i in range(n)]
    u = [mm_nn(t_inv[i], vs[i] * beta[i]) for i in range(n)]
    w = [mm_nn(t_inv[i], kb[i] * egc[i]) for i in range(n)]
    intra = [mm_nt(q[i], k[i]) * gamma[i] for i in range(n)]
    out = []
    for i in range(n):
        kt = k[i] * jnp.exp(g_last[i] - gc_col[i])
        dec = jnp.broadcast_to(jnp.exp(g_last[i]), (1, HEAD_DIM))
        one = (u[i], w[i], q[i] * egc[i], kt, intra[i], dec)
        out.append(one + (t_inv[i],) if t_saved is None else one)
    return out


def _lane_masks(h):
    lane = lax.broadcasted_iota(jnp.int32, (1, LANES), 1)
    return lane == h, lane == N_DELTA_HEADS + h


def _prep_specs(tr, ab_col):
    H = N_DELTA_HEADS
    return [
        pl.BlockSpec((tr, HEAD_DIM), lambda i, h: (i, h)),
        pl.BlockSpec((tr, HEAD_DIM), lambda i, h: (i, H + h)),
        pl.BlockSpec((tr, HEAD_DIM), lambda i, h: (i, 2 * H + h)),
        pl.BlockSpec((tr, LANES), lambda i, h: (i, ab_col)),
        pl.BlockSpec((1, LANES), lambda i, h: (0, 0)),
        pl.BlockSpec((1, LANES), lambda i, h: (0, 0)),
    ]


def _prep_out_specs(tr):
    nc = tr // CHUNK
    hs = pl.BlockSpec((tr, HEAD_DIM), lambda i, h: (i, h))
    return [hs, hs, hs, hs,
            pl.BlockSpec((None, tr, CHUNK), lambda i, h: (h, i, 0)),
            pl.BlockSpec((None, nc, 1, HEAD_DIM), lambda i, h: (h, i, 0, 0)),
            pl.BlockSpec((None, tr, CHUNK), lambda i, h: (h, i, 0))]


def _prep_out_shapes(S):
    H = N_DELTA_HEADS
    hs = jax.ShapeDtypeStruct((S, H * HEAD_DIM), F32)
    sq = jax.ShapeDtypeStruct((H, S, CHUNK), F32)
    return [hs, hs, hs, hs, sq, jax.ShapeDtypeStruct((H, S // CHUNK, 1, HEAD_DIM), F32), sq]


def _delta_prep_fwd(dqkv, proj, ab_col, alog_row, dtb_row, name):
    S = dqkv.shape[0]
    tr = min(S, PREP_CHUNKS * CHUNK)
    nc = tr // CHUNK

    def body(q_ref, k_ref, v_ref, ab_ref, al_ref, dt_ref, u_ref, w_ref, qd_ref, kt_ref, in_ref, dec_ref, ti_ref):
        mask_g, mask_b = _lane_masks(pl.program_id(1))
        rows = [pl.ds(ci * CHUNK, CHUNK) for ci in range(nc)]
        outs = _chunks_prep([q_ref[rs, :] for rs in rows], [k_ref[rs, :] for rs in rows], [v_ref[rs, :] for rs in rows],
                            [ab_ref[rs, :] for rs in rows], al_ref[...], dt_ref[...], mask_g, mask_b)
        for ci, rs in enumerate(rows):
            u, w, qd, kt, intra, dec, t_inv = outs[ci]
            u_ref[rs, :] = u
            w_ref[rs, :] = w
            qd_ref[rs, :] = qd
            kt_ref[rs, :] = kt
            in_ref[rs, :] = intra
            dec_ref[ci] = dec
            ti_ref[rs, :] = t_inv

    return pl.pallas_call(
        body, name=name, grid=(S // tr, N_DELTA_HEADS),
        in_specs=_prep_specs(tr, ab_col), out_specs=_prep_out_specs(tr), out_shape=_prep_out_shapes(S),
        compiler_params=_cparams(("parallel", "parallel")),
    )(dqkv, dqkv, dqkv, proj, alog_row, dtb_row)


def _delta_prep_bwd(dqkv, proj, ab_col, alog_row, dtb_row, cots, t_inv, name):
    S = dqkv.shape[0]
    H = N_DELTA_HEADS
    tr = min(S, PREP_CHUNKS * CHUNK)
    nc = tr // CHUNK

    def body(q_ref, k_ref, v_ref, ab_ref, al_ref, dt_ref, du_ref, dw_ref, dqd_ref, dkt_ref, din_ref, ddec_ref, ti_ref,
             dq_ref, dk_ref, dv_ref, dab_ref, dal_ref, ddt_ref, dab_sc):
        h = pl.program_id(1)
        mask_g, mask_b = _lane_masks(h)

        @pl.when((pl.program_id(0) == 0) & (h == 0))
        def _():
            dal_ref[...] = jnp.zeros_like(dal_ref)
            ddt_ref[...] = jnp.zeros_like(ddt_ref)

        @pl.when(h == 0)
        def _():
            dab_sc[...] = jnp.zeros_like(dab_sc)

        rows = [pl.ds(ci * CHUNK, CHUNK) for ci in range(nc)]
        fn = functools.partial(_chunks_prep, mask_g=mask_g, mask_b=mask_b, t_saved=[ti_ref[rs, :] for rs in rows])
        _, vjp = jax.vjp(fn, [q_ref[rs, :] for rs in rows], [k_ref[rs, :] for rs in rows], [v_ref[rs, :] for rs in rows],
                         [ab_ref[rs, :] for rs in rows], al_ref[...], dt_ref[...])
        dqs, dks, dvs, dabs, dal, ddt = vjp([(du_ref[rs, :], dw_ref[rs, :], dqd_ref[rs, :], dkt_ref[rs, :], din_ref[rs, :],
                                              ddec_ref[ci]) for ci, rs in enumerate(rows)])
        for ci, rs in enumerate(rows):
            dq_ref[rs, :] = dqs[ci]
            dk_ref[rs, :] = dks[ci]
            dv_ref[rs, :] = dvs[ci]
            dab_sc[rs, :] += dabs[ci]
        dal_ref[...] += dal
        ddt_ref[...] += ddt

        @pl.when(h == H - 1)
        def _():
            dab_ref[...] = dab_sc[...].astype(BF16)

    hs = pl.BlockSpec((tr, HEAD_DIM), lambda i, h: (i, h))
    hshape = jax.ShapeDtypeStruct((S, H * HEAD_DIM), F32)
    row = pl.BlockSpec((1, LANES), lambda i, h: (0, 0))
    rshape = jax.ShapeDtypeStruct((1, LANES), F32)
    return pl.pallas_call(
        body, name=name, grid=(S // tr, H),
        in_specs=_prep_specs(tr, ab_col) + _prep_out_specs(tr),
        out_specs=[hs, hs, hs, pl.BlockSpec((tr, LANES), lambda i, h: (i, 0)), row, row],
        out_shape=[hshape, hshape, hshape, jax.ShapeDtypeStruct((S, LANES), BF16), rshape, rshape],
        scratch_shapes=[pltpu.VMEM((tr, LANES), F32)],
        compiler_params=_cparams(("arbitrary", "arbitrary")),
    )(dqkv, dqkv, dqkv, proj, alog_row, dtb_row, *cots, t_inv)


def _scan_steps(states, us, ws, qds, kts, intras, decs, diff=False):
    nn, tn = (_mm_nn, _mm_tn) if diff else (lambda p, r: _bdot(p, r, _NN), lambda p, r: _bdot(p, r, _TN))
    hs = range(len(states))
    v_new = [us[h] - nn(ws[h], states[h]) for h in hs]
    o_state = [nn(qds[h], states[h]) for h in hs]
    o_intra = [nn(intras[h], v_new[h]) for h in hs]
    grown = [tn(kts[h], v_new[h]) for h in hs]
    return [o_state[h] + o_intra[h] for h in hs], [states[h] * decs[h] + grown[h] for h in hs]


def _scan_specs(rev, n):
    H = N_DELTA_HEADS

    def cix(i):
        return (n - 1 - i) if rev else i

    row = pl.BlockSpec((CHUNK, H * HEAD_DIM), lambda i: (cix(i), 0))
    return row, pl.BlockSpec((H, CHUNK, CHUNK), lambda i: (0, cix(i), 0)), \
        pl.BlockSpec((H, 1, 1, HEAD_DIM), lambda i: (0, cix(i), 0, 0)), \
        pl.BlockSpec((1, H, HEAD_DIM, HEAD_DIM), lambda i: (cix(i), 0, 0, 0))


def _delta_scan_fwd(u, w, qd, kt, intra, dec, name):
    S = u.shape[0]
    H = N_DELTA_HEADS
    n = S // CHUNK
    row, ispec, dspec, sspec = _scan_specs(False, n)

    def body(u_ref, w_ref, qd_ref, kt_ref, in_ref, dec_ref, o_ref, st_ref, s_sc):
        @pl.when(pl.program_id(0) == 0)
        def _():
            s_sc[...] = jnp.zeros_like(s_sc)

        cols = [pl.ds(h * HEAD_DIM, HEAD_DIM) for h in range(H)]
        states = [s_sc[h] for h in range(H)]
        outs, new = _scan_steps(states, [u_ref[:, cs] for cs in cols], [w_ref[:, cs] for cs in cols],
                                [qd_ref[:, cs] for cs in cols], [kt_ref[:, cs] for cs in cols],
                                [in_ref[h] for h in range(H)], [dec_ref[h, 0] for h in range(H)])
        for h, cs in enumerate(cols):
            st_ref[0, h] = states[h]
            o_ref[:, cs] = outs[h]
            s_sc[h] = new[h]

    return pl.pallas_call(
        body, name=name, grid=(n,),
        in_specs=[row, row, row, row, ispec, dspec], out_specs=[row, sspec],
        out_shape=[jax.ShapeDtypeStruct((S, H * HEAD_DIM), F32), jax.ShapeDtypeStruct((n, H, HEAD_DIM, HEAD_DIM), F32)],
        scratch_shapes=[pltpu.VMEM((H, HEAD_DIM, HEAD_DIM), F32)],
        compiler_params=_cparams(("arbitrary",)),
    )(u, w, qd, kt, intra, dec)


def _delta_scan_bwd(u, w, qd, kt, intra, dec, states, do, name):
    S = u.shape[0]
    H = N_DELTA_HEADS
    n = S // CHUNK
    row, ispec, dspec, sspec = _scan_specs(True, n)

    def body(u_ref, w_ref, qd_ref, kt_ref, in_ref, dec_ref, st_ref, do_ref,
             du_ref, dw_ref, dqd_ref, dkt_ref, din_ref, ddec_ref, ds_sc):
        @pl.when(pl.program_id(0) == 0)
        def _():
            ds_sc[...] = jnp.zeros_like(ds_sc)

        cols = [pl.ds(h * HEAD_DIM, HEAD_DIM) for h in range(H)]
        _, vjp = jax.vjp(functools.partial(_scan_steps, diff=True), [st_ref[0, h] for h in range(H)], [u_ref[:, cs] for cs in cols],
                         [w_ref[:, cs] for cs in cols], [qd_ref[:, cs] for cs in cols], [kt_ref[:, cs] for cs in cols],
                         [in_ref[h] for h in range(H)], [dec_ref[h, 0] for h in range(H)])
        dstate, du, dw, dqd, dkt, din, ddec = vjp(([do_ref[:, cs] for cs in cols], [ds_sc[h] for h in range(H)]))
        for h, cs in enumerate(cols):
            du_ref[:, cs] = du[h]
            dw_ref[:, cs] = dw[h]
            dqd_ref[:, cs] = dqd[h]
            dkt_ref[:, cs] = dkt[h]
            din_ref[h] = din[h]
            ddec_ref[h, 0] = ddec[h]
            ds_sc[h] = dstate[h]

    hshape = jax.ShapeDtypeStruct((S, H * HEAD_DIM), F32)
    return pl.pallas_call(
        body, name=name, grid=(n,),
        in_specs=[row, row, row, row, ispec, dspec, sspec, row],
        out_specs=[row, row, row, row, ispec, dspec],
        out_shape=[hshape, hshape, hshape, hshape, jax.ShapeDtypeStruct((H, S, CHUNK), F32),
                   jax.ShapeDtypeStruct((H, n, 1, HEAD_DIM), F32)],
        scratch_shapes=[pltpu.VMEM((H, HEAD_DIM, HEAD_DIM), F32)],
        compiler_params=_cparams(("arbitrary",)),
    )(u, w, qd, kt, intra, dec, states, do)


def _gated_norm(od, z, gain):
    return _rms(od, gain) * _silu(z)


def _post_fwd(od, proj, z_col0, gain, name):
    S = od.shape[0]
    H = N_DELTA_HEADS
    tr = _pick(S, 512)

    def body(od_ref, z_ref, g_ref, o_ref):
        o_ref[...] = _gated_norm(od_ref[...], z_ref[...], g_ref[...]).astype(BF16)

    return pl.pallas_call(
        body, name=name, grid=(S // tr, H),
        in_specs=[_head_spec(tr, 0), _head_spec(tr, z_col0), _gain_spec()],
        out_specs=_head_spec(tr, 0), out_shape=jax.ShapeDtypeStruct((S, H * HEAD_DIM), BF16),
        compiler_params=_cparams(("parallel", "parallel")),
    )(od, proj, gain)


def _post_bwd(od, proj, z_col0, gain, do, do_col0, name):
    S = od.shape[0]
    H = N_DELTA_HEADS
    tr = _pick(S, 512)

    def body(od_ref, z_ref, g_ref, do_ref, dod_ref, dz_ref, dg_ref):
        @pl.when((pl.program_id(0) == 0) & (pl.program_id(1) == 0))
        def _():
            dg_ref[...] = jnp.zeros_like(dg_ref)

        _, vjp = jax.vjp(_gated_norm, od_ref[...], z_ref[...], g_ref[...])
        dod, dz, dg = vjp(do_ref[...])
        dod_ref[...] = dod
        dz_ref[...] = dz.astype(BF16)
        dg_ref[...] += dg

    return pl.pallas_call(
        body, name=name, grid=(S // tr, H),
        in_specs=[_head_spec(tr, 0), _head_spec(tr, z_col0), _gain_spec(), _head_spec(tr, do_col0)],
        out_specs=[_head_spec(tr, 0), _head_spec(tr, 0), _gain_spec()],
        out_shape=[jax.ShapeDtypeStruct((S, H * HEAD_DIM), F32), jax.ShapeDtypeStruct((S, H * HEAD_DIM), BF16),
                   jax.ShapeDtypeStruct((1, HEAD_DIM), F32)],
        compiler_params=_cparams(("arbitrary", "arbitrary")),
    )(od, proj, gain, do)


def _adam_math(w, g, m, v):
    m = ADAM_B1 * m + (1.0 - ADAM_B1) * g
    v = ADAM_B2 * v + (1.0 - ADAM_B2) * (g * g)
    m_hat = m / (1.0 - ADAM_B1 ** ADAM_STEP)
    v_hat = v / (1.0 - ADAM_B2 ** ADAM_STEP)
    delta = -ADAM_LR * (m_hat / (jnp.sqrt(v_hat) + ADAM_EPS) + ADAM_WD * w)
    return delta, m, v


def _adamw(w, g, m, v, name):
    R, C = w.shape
    tr = R if R * C * 4 <= (1 << 20) else _pick8(R, max(8, (1 << 20) // (C * 4)))

    def body(w_ref, g_ref, m_ref, v_ref, d_ref, nm_ref, nv_ref):
        d, nm, nv = _adam_math(w_ref[...], g_ref[...], m_ref[...], v_ref[...])
        d_ref[...] = d
        nm_ref[...] = nm
        nv_ref[...] = nv

    spec = pl.BlockSpec((tr, C), lambda i: (i, 0))
    shp = jax.ShapeDtypeStruct((R, C), F32)
    return pl.pallas_call(
        body, name=name, grid=(R // tr,), in_specs=[spec] * 4, out_specs=[spec] * 3, out_shape=[shp] * 3,
        compiler_params=_cparams(("parallel",)),
    )(w, g, m, v)


def _adamw_halves(w, mine, theirs, col, m, v, name):
    _, R, C = w.shape
    tr = _pick8(R // 2, max(8, (1 << 20) // (C * 4)))
    nb2 = (R // 2) // tr

    def body(w_ref, a_ref, b_ref, m_ref, v_ref, g_ref, d_ref, nm_ref, nv_ref):
        top = pl.program_id(0) < nb2
        g = jnp.where(top == (lax.axis_index("c") == 0), a_ref[...], b_ref[...])
        d, nm, nv = _adam_math(w_ref[...], g, m_ref[...], v_ref[...])
        g_ref[...] = g
        d_ref[...] = d
        nm_ref[...] = nm
        nv_ref[...] = nv

    spec = pl.BlockSpec((None, tr, C), lambda i: (0, i, 0))
    half = pl.BlockSpec((tr, C), lambda i: (i % nb2, col))
    shp = jax.ShapeDtypeStruct((1, R, C), F32)
    return pl.pallas_call(
        body, name=name, grid=(2 * nb2,), in_specs=[spec, half, half, spec, spec], out_specs=[spec] * 4, out_shape=[shp] * 4,
        compiler_params=_cparams(("parallel",)),
    )(w, mine, theirs, m, v)


def _pick8(dim, pref):
    t = (min(dim, pref) // 8) * 8
    while t >= 8:
        if dim % t == 0:
            return t
        t -= 8
    return dim


def _adamw_outer(w, m, v, cond_t, rhs, name):
    R, C = w.shape
    tr = _pick8(R, 128)
    nb = cond_t.shape[1]
    lhs_t = cond_t

    def body(w_ref, m_ref, v_ref, a_ref, b_ref, g_ref, d_ref, nm_ref, nv_ref):
        g = _dot(_silu(a_ref[...]), b_ref[...])
        d, nm, nv = _adam_math(w_ref[...], g, m_ref[...], v_ref[...])
        g_ref[...] = g
        d_ref[...] = d
        nm_ref[...] = nm
        nv_ref[...] = nv

    spec = pl.BlockSpec((tr, C), lambda i: (i, 0))
    shp = jax.ShapeDtypeStruct((R, C), F32)
    return pl.pallas_call(
        body, name=name, grid=(R // tr,),
        in_specs=[spec, spec, spec, pl.BlockSpec((tr, nb), lambda i: (i, 0)), pl.BlockSpec((nb, C), lambda i: (0, 0))],
        out_specs=[spec] * 4, out_shape=[shp] * 4,
        compiler_params=_cparams(("parallel",)),
    )(w, m, v, lhs_t, rhs)


def _ada_fwd(cond, w, bias, name):
    a = cond
    nb, K = a.shape
    N = w.shape[1]
    tn = _pick(N, 512)

    def body(a_ref, w_ref, b_ref, o_ref):
        o_ref[...] = _dot(_silu(a_ref[...]), w_ref[...]) + b_ref[...]

    return pl.pallas_call(
        body, name=name, grid=(N // tn,),
        in_specs=[pl.BlockSpec((nb, K), lambda j: (0, 0)), pl.BlockSpec((K, tn), lambda j: (0, j)), pl.BlockSpec((1, tn), lambda j: (0, j))],
        out_specs=pl.BlockSpec((nb, tn), lambda j: (0, j)), out_shape=jax.ShapeDtypeStruct((nb, N), F32),
        compiler_params=_cparams(("parallel",)),
    )(a, w, bias)


def _add_cast(parts, out_dtypes, name):
    shape = parts[0].shape
    G, R, C = shape
    tr = _pick8(R, max(8, (1 << 20) // (C * 4)))
    n_in = len(parts)

    def body(*refs):
        acc = refs[0][...].astype(F32)
        for r in refs[1:n_in]:
            acc = acc + r[...].astype(F32)
        for o, dt in zip(refs[n_in:], out_dtypes):
            o[...] = acc.astype(dt)

    spec = pl.BlockSpec((1, tr, C), lambda g, i: (g, i, 0))
    outs = pl.pallas_call(
        body, name=name, grid=(G, R // tr), in_specs=[spec] * n_in, out_specs=[spec] * len(out_dtypes),
        out_shape=[jax.ShapeDtypeStruct(shape, dt) for dt in out_dtypes],
        compiler_params=_cparams(("parallel", "parallel")),
    )(*parts)
    return outs


def _me():
    return lax.axis_index("x"), lax.axis_index("y"), lax.axis_index("c")


def _xor_peer(k):
    x, y, c = _me()
    dx, dy, dc = (k >> 2) & 1, (k >> 1) & 1, k & 1
    return (x ^ dx if dx else x, y ^ dy if dy else y, c ^ dc if dc else c)


ANY = pl.BlockSpec(memory_space=pl.ANY)


def _all_gather_small(v, name, after=None):
    R, C = v.shape
    extra = [] if after is None else [after]

    def body(v_ref, *rest):
        out_ref, send_sems, recv_sems = rest[len(extra):]
        x, y, c = _me()
        mine = 4 * x + 2 * y + c
        out_ref[mine] = v_ref[...]
        copies = []
        for k in range(1, 8):
            cp = pltpu.make_async_remote_copy(src_ref=v_ref, dst_ref=out_ref.at[mine], send_sem=send_sems.at[k - 1],
                                              recv_sem=recv_sems.at[k - 1], device_id=_xor_peer(k), device_id_type=MESH)
            cp.start()
            copies.append(cp)
        for k in range(1, 8):
            px, py, pc = _xor_peer(k)
            pltpu.make_async_remote_copy(src_ref=v_ref, dst_ref=out_ref.at[4 * px + 2 * py + pc], send_sem=send_sems.at[k - 1],
                                         recv_sem=recv_sems.at[k - 1], device_id=_xor_peer(k), device_id_type=MESH).wait_recv()
        for cp in copies:
            cp.wait_send()

    return pl.pallas_call(
        body, name=name, out_shape=jax.ShapeDtypeStruct((8, R, C), F32),
        in_specs=[pl.BlockSpec(memory_space=pltpu.VMEM)] + [ANY] * len(extra), out_specs=pl.BlockSpec(memory_space=pltpu.VMEM),
        scratch_shapes=[pltpu.SemaphoreType.DMA((7,)), pltpu.SemaphoreType.DMA((7,))],
        compiler_params=pltpu.CompilerParams(vmem_limit_bytes=VMEM_LIMIT),
    )(v, *extra)


def _chip_peers():
    x, y, _ = _me()
    return [(1, (x, 1 - y)), (2, (1 - x, y)), (3, (1 - x, 1 - y))]


def _all_gather_shards(shards, name):
    n = len(shards)

    def body(*refs):
        ins, outs = refs[:n], refs[n:2 * n]
        send_sems, recv_sems = refs[2 * n:]
        x, y, c = _me()
        chip = 2 * x + y
        sib = (x, y, 1 - c)
        peers = _chip_peers()
        sends = []
        for t in range(n):
            half = ins[t].shape[0] // 2
            mine = pl.ds(c * half, half)
            for p, (k, (px, py)) in enumerate(peers):
                cp = pltpu.make_async_remote_copy(src_ref=ins[t].at[mine], dst_ref=outs[t].at[chip, mine],
                                                  send_sem=send_sems.at[6 * t + p], recv_sem=recv_sems.at[6 * t + p],
                                                  device_id=(px, py, c), device_id_type=MESH)
                cp.start()
                sends.append(cp)
        for t in range(n):
            half = ins[t].shape[0] // 2
            mine = pl.ds(c * half, half)
            for p, (k, (px, py)) in enumerate(peers):
                src_chip = 2 * px + py
                landed = outs[t].at[src_chip, mine]
                pltpu.make_async_remote_copy(src_ref=landed, dst_ref=landed, send_sem=send_sems.at[6 * t + p],
                                             recv_sem=recv_sems.at[6 * t + p], device_id=(px, py, c), device_id_type=MESH).wait_recv()
                fw = pltpu.make_async_remote_copy(src_ref=landed, dst_ref=landed, send_sem=send_sems.at[6 * t + 3 + p],
                                                  recv_sem=recv_sems.at[6 * t + 3 + p], device_id=sib, device_id_type=MESH)
                fw.start()
                sends.append(fw)
        for t in range(n):
            half = ins[t].shape[0] // 2
            theirs = pl.ds((1 - c) * half, half)
            for p, (k, (px, py)) in enumerate(peers):
                got = outs[t].at[2 * px + py, theirs]
                pltpu.make_async_remote_copy(src_ref=got, dst_ref=got, send_sem=send_sems.at[6 * t + 3 + p],
                                             recv_sem=recv_sems.at[6 * t + 3 + p], device_id=sib, device_id_type=MESH).wait_recv()
        for cp in sends:
            cp.wait_send()

    return pl.pallas_call(
        body, name=name,
        out_shape=[jax.ShapeDtypeStruct((4,) + s.shape, s.dtype) for s in shards],
        in_specs=[ANY] * n, out_specs=[ANY] * n,
        scratch_shapes=[pltpu.SemaphoreType.DMA((6 * n,)), pltpu.SemaphoreType.DMA((6 * n,))],
    )(*shards)


def _swap_halves_with_sibling(slabs, name):
    n = len(slabs)

    def body(*refs):
        ins, outs = refs[:n], refs[n:2 * n]
        send_sems, recv_sems = refs[2 * n:]
        x, y, c = _me()
        sib = (x, y, 1 - c)
        cps = []
        for t in range(n):
            half = ins[t].shape[1] // 2
            cp = pltpu.make_async_remote_copy(src_ref=ins[t].at[:, pl.ds((1 - c) * half, half)], dst_ref=outs[t],
                                              send_sem=send_sems.at[t], recv_sem=recv_sems.at[t], device_id=sib, device_id_type=MESH)
            cp.start()
            cps.append(cp)
        for cp in cps:
            cp.wait()

    return pl.pallas_call(
        body, name=name,
        out_shape=[jax.ShapeDtypeStruct((4, s.shape[1] // 2, s.shape[2]), s.dtype) for s in slabs],
        in_specs=[ANY] * n, out_specs=[ANY] * n,
        scratch_shapes=[pltpu.SemaphoreType.DMA((n,)), pltpu.SemaphoreType.DMA((n,))],
    )(*slabs)


def _send_halves_to_sibling(halves, name):
    n = len(halves)

    def body(*refs):
        ins, outs = refs[:n], refs[n:2 * n]
        send_sems, recv_sems = refs[2 * n:]
        x, y, c = _me()
        sib = (x, y, 1 - c)
        cps = []
        for t in range(n):
            cp = pltpu.make_async_remote_copy(src_ref=ins[t], dst_ref=outs[t], send_sem=send_sems.at[t],
                                              recv_sem=recv_sems.at[t], device_id=sib, device_id_type=MESH)
            cp.start()
            cps.append(cp)
        for cp in cps:
            cp.wait()

    return pl.pallas_call(
        body, name=name,
        out_shape=[jax.ShapeDtypeStruct(s.shape, s.dtype) for s in halves],
        in_specs=[ANY] * n, out_specs=[ANY] * n,
        scratch_shapes=[pltpu.SemaphoreType.DMA((n,)), pltpu.SemaphoreType.DMA((n,))],
    )(*halves)


HBM_SPEC = pl.BlockSpec(memory_space=pltpu.HBM)
SEM_SPEC = pl.BlockSpec(memory_space=pltpu.SEMAPHORE)
DATAFLOW = pltpu.SideEffectType.DATAFLOW_SIDE_EFFECTING


def _plan_gather_direct(src_refs, land_refs):
    x, y, c = _me()
    chip = 2 * x + y
    plan = []
    for s, land in zip(src_refs, land_refs):
        half = s.shape[0] // 2
        for _, (px, py) in _chip_peers():
            for pc in (c, 1 - c):
                plan.append((s.at[pl.ds(c * half, half)], land.at[chip, pl.ds(c * half, half)],
                             land.at[2 * px + py, pl.ds(pc * half, half)], (px, py, pc)))
    return plan


def _plan_scatter_direct(src_refs, land_refs):
    x, y, c = _me()
    plan = []
    for s, land in zip(src_refs, land_refs):
        half = s.shape[1] // 2
        for k in range(1, 8):
            px, py, pc = _xor_peer(k)
            plan.append((s.at[2 * px + py, pl.ds(pc * half, half)], land.at[4 * x + 2 * y + c],
                         land.at[4 * px + 2 * py + pc], (px, py, pc)))
    return plan


def _plan_exchange_chips(src_refs, land_refs):
    x, y, c = _me()
    chip = 2 * x + y
    plan = []
    for s, land in zip(src_refs, land_refs):
        for _, (px, py) in _chip_peers():
            plan.append((s.at[2 * px + py], land.at[chip], land.at[2 * px + py], (px, py, c)))
    return plan


_plan_gather_direct.per_tensor = 6
_plan_scatter_direct.per_tensor = 7
_plan_exchange_chips.per_tensor = 3


def _start_copies(srcs, lands, plan_fn, name, after=None):
    n = len(srcs)
    n_copies = len(srcs) * plan_fn.per_tensor
    extra = [] if after is None else [after]

    def body(*refs):
        refs = refs[:2 * n] + refs[2 * n + len(extra):]
        send_sems, recv_sems, token = refs[2 * n], refs[2 * n + 1], refs[-1]
        for i, (src, dst, _, peer) in enumerate(plan_fn(refs[:n], refs[n:2 * n])):
            pltpu.make_async_remote_copy(src_ref=src, dst_ref=dst, send_sem=send_sems.at[i], recv_sem=recv_sems.at[i],
                                         device_id=peer, device_id_type=MESH).start()
        token[...] = jnp.zeros_like(token)

    arrays = list(srcs) + list(lands)
    outs = pl.pallas_call(
        body, name=name,
        out_shape=(pltpu.SemaphoreType.DMA((n_copies,)), pltpu.SemaphoreType.DMA((n_copies,)),
                   *[pltpu.HBM(a.shape, a.dtype) for a in arrays], jax.ShapeDtypeStruct((8, LANES), F32)),
        in_specs=[HBM_SPEC] * (2 * n) + [ANY] * len(extra),
        out_specs=(SEM_SPEC, SEM_SPEC, *[HBM_SPEC] * (2 * n), pl.BlockSpec(memory_space=pltpu.VMEM)),
        input_output_aliases={i: 2 + i for i in range(2 * n)},
        compiler_params=pltpu.CompilerParams(has_side_effects=DATAFLOW),
    )(*[pltpu.with_memory_space_constraint(a, pltpu.HBM) for a in arrays], *extra)
    return outs[0], outs[1], list(outs[2:2 + n]), list(outs[2 + n:2 + 2 * n]), outs[-1]


def _wait_copies(send_sems, recv_sems, srcs, lands, after, plan_fn, name):
    n = len(srcs)

    def body(*refs):
        send_sems, recv_sems = refs[2 * n], refs[2 * n + 1]
        for i, (src, _, arrival, peer) in enumerate(plan_fn(refs[:n], refs[n:2 * n])):
            cp = pltpu.make_async_remote_copy(src_ref=src, dst_ref=arrival, send_sem=send_sems.at[i], recv_sem=recv_sems.at[i],
                                              device_id=peer, device_id_type=MESH)
            cp.wait_send()
            cp.wait_recv()

    arrays = list(srcs) + list(lands)
    outs = pl.pallas_call(
        body, name=name,
        out_shape=tuple(pltpu.HBM(a.shape, a.dtype) for a in arrays),
        in_specs=[HBM_SPEC] * (2 * n) + [SEM_SPEC, SEM_SPEC, ANY],
        out_specs=tuple([HBM_SPEC] * (2 * n)),
        input_output_aliases={i: i for i in range(2 * n)},
        compiler_params=pltpu.CompilerParams(has_side_effects=DATAFLOW),
    )(*arrays, send_sems, recv_sems, after)
    return list(outs[n:])


def _rope_tables(positions):
    half = ROPE_DIM // 2
    S = positions.shape[0]
    inv_freq = ROPE_THETA ** (-jnp.arange(half, dtype=F32) / half)
    ang = positions.astype(F32)[:, None] * inv_freq
    cos, sin = jnp.cos(ang), jnp.sin(ang)
    zeros = functools.partial(jnp.zeros, dtype=F32)
    cc = jnp.concatenate([cos, cos, jnp.ones((S, HEAD_DIM - ROPE_DIM), F32)], axis=1)
    s1 = jnp.concatenate([-sin, zeros((S, HEAD_DIM - half))], axis=1)
    s2 = jnp.concatenate([zeros((S, half)), sin, zeros((S, HEAD_DIM - ROPE_DIM))], axis=1)
    return cc, s1, s2


def _ffn_fwd(x, gain, shift, scale, gate, w_gu, w_d, fs, tag):
    S = x.shape[0]
    tm = _pick(S, 512)
    h = _pre_fwd(x, gain, shift, scale, tag + "_pre")

    def swiglu_out(tile, ins, outs):
        outs[0][...] = tile
        outs[1][...] = _swiglu_fn(tile[:, :fs], tile[:, fs:]).astype(BF16)

    ab, s = _matmul(h, w_gu, "nn", F32, tag + "_gate_up", tm=tm, tn=2 * fs, tk=4096, epilogue=(swiglu_out, [], [
        ((S, 8 * fs), F32, (tm, 2 * fs), lambda i, j: (i, j)), ((S, 4 * fs), BF16, (tm, fs), lambda i, j: (i, j))]))
    f = _matmul(s, w_d, "nn", F32, tag + "_down", tm=512, tn=1024, tk=8192)
    xn = _residual_fwd(x, gate, f, 0.5, tag + "_res")
    return xn, (x, h, ab, s, f)


def _ffn_bwd(dxn, saved, gain, shift, scale, gate, w_gu, w_d, fs, tag, on_dw_d=None, on_dw_gu=None, dw_gu_dtype=F32):
    x, h, ab, s, f = saved
    df, dgate = _residual_bwd(gate, f, dxn, 0.5, tag + "_res_bwd")
    S = x.shape[0]
    tm = _pick(S, 512)
    dw_d = _matmul(s, df, "tn", BF16, tag + "_down_dw", tm=1408, tn=1024, tk=2048)
    if on_dw_d is not None:
        shift = shift + on_dw_d(dw_d)

    def swiglu_back(tile, ins, outs):
        _, vjp = jax.vjp(_swiglu_fn, ins[0][:, :fs], ins[0][:, fs:])
        da, db = vjp(tile)
        outs[0][:, :fs] = da.astype(BF16)
        outs[0][:, fs:] = db.astype(BF16)

    (dab,) = _matmul(df, w_d, "nt", F32, tag + "_down_dx", tm=tm, tn=fs, tk=4096, epilogue=(
        swiglu_back, [(ab, (tm, 2 * fs), lambda i, j: (i, j))], [((S, 8 * fs), BF16, (tm, 2 * fs), lambda i, j: (i, j))]))
    dw_gu = _matmul(h, dab, "tn", dw_gu_dtype, tag + "_gate_up_dw", tm=512, tn=2 * fs, tk=2048, col_slabs=True)
    tie = on_dw_gu(dw_gu) if on_dw_gu is not None else None
    dh = _matmul(dab, w_gu, "nt", F32, tag + "_gate_up_dx", tm=1024, tn=1024, tk=2816, after=tie)
    dx, dgain, dshift, dscale = _pre_bwd(x, gain, shift, scale, dh, dxn, tag + "_pre_bwd")
    return dx, dw_gu, dw_d, dgain, dshift, dscale, dgate


def _flat_pad(parts, rows, cols):
    flat = jnp.concatenate([p.reshape(-1).astype(F32) for p in parts])
    return jnp.pad(flat, (0, rows * cols - flat.shape[0])).reshape(rows, cols)


def _cols_to_slabs(w, n):
    R, NC = w.shape
    return jnp.transpose(w.reshape(R, n, NC // n), (1, 0, 2))


def kernel(x, c, positions, w_ada, b_ada, ffn1_norm, ffn1_w_gate, ffn1_w_up, ffn1_w_down, mix_norm, w_in, conv_w, q_norm, k_norm, a_log, dt_bias, delta_out_norm, w_out, ffn2_norm, ffn2_w_gate, ffn2_w_up, ffn2_w_down, loss_target, m_w_ada, m_b_ada, m_ffn1_norm, m_ffn1_w_gate, m_ffn1_w_up, m_ffn1_w_down, m_mix_norm, m_w_in, m_conv_w, m_q_norm, m_k_norm, m_a_log, m_dt_bias, m_delta_out_norm, m_w_out, m_ffn2_norm, m_ffn2_w_gate, m_ffn2_w_up, m_ffn2_w_down, v_w_ada, v_b_ada, v_ffn1_norm, v_ffn1_w_gate, v_ffn1_w_up, v_ffn1_w_down, v_mix_norm, v_w_in, v_conv_w, v_q_norm, v_k_norm, v_a_log, v_dt_bias, v_delta_out_norm, v_w_out, v_ffn2_norm, v_ffn2_w_gate, v_ffn2_w_up, v_ffn2_w_down):
    xi, yi, ci = _me()
    chip = 2 * xi + yi
    dev = 2 * chip + ci
    xs = x[0]
    S, D = xs.shape
    HA, HD = N_ATTN_HEADS, N_DELTA_HEADS
    fs = ffn1_w_gate.shape[2]
    n_mod_shard = w_ada.shape[2]
    in_shard = w_in.shape[2]
    in_width = 4 * in_shard
    in_pad = -(-in_width // LANES) * LANES
    conv_shard = conv_w.shape[2]
    conv_width = 4 * conv_shard

    pack0 = jnp.zeros((8, max(D, conv_shard)), F32)
    pack0 = pack0.at[0, :D].set(c[0]).at[1:1 + CONV_WIDTH, :conv_shard].set(conv_w[0])
    got0 = _all_gather_small(pack0, "gather_cond")
    c_all = got0[:, 0, :D]
    conv_full = jnp.transpose(got0[::2, 1:1 + CONV_WIDTH, :conv_shard], (1, 0, 2)).reshape(CONV_WIDTH, conv_width)
    b_ada_mine = lax.dynamic_slice(b_ada, (0, chip * n_mod_shard), (1, n_mod_shard))
    mod_part = _ada_fwd(c_all, w_ada[0], b_ada_mine, "ada_fwd")
    got1 = _all_gather_small(mod_part, "gather_mod")
    mod = lax.dynamic_index_in_dim(got1[::2], dev, axis=1, keepdims=False).reshape(1, 4 * n_mod_shard)
    sh1, sc1, gt1, sh2, sc2, gt2, sh3, sc3, gt3 = [mod[:, i * D:(i + 1) * D] for i in range(N_MOD)]

    shards = [w[0].astype(BF16) for w in (ffn1_w_gate, ffn1_w_up, ffn1_w_down, w_in, w_out, ffn2_w_gate, ffn2_w_up, ffn2_w_down)]
    gathered = _all_gather_shards(shards[:3], "gather_weights")
    g1g, g1u, g1d = [lax.dynamic_update_index_in_dim(g, s, chip, 0) for g, s in zip(gathered, shards[:3])]
    zones = [lax.dynamic_update_index_in_dim(lax.empty((4,) + s.shape, BF16), s, chip, 0) for s in shards[3:]]
    ag_in = _start_copies(shards[3:4], zones[:1], _plan_gather_direct, "gather_in_start")
    sh1 = sh1 + ag_in[4][0, 0]

    def gate_up(gg, gu):
        return jnp.transpose(jnp.concatenate([gg, gu], axis=2), (1, 0, 2)).reshape(D, 8 * fs)

    w_gu1, w_d1 = gate_up(g1g, g1u), g1d.reshape(4 * fs, D)

    x1, saved1 = _ffn_fwd(xs, ffn1_norm, sh1, sc1, gt1, w_gu1, w_d1, fs, "ffn1")
    (gin,) = _wait_copies(ag_in[0], ag_in[1], ag_in[2], ag_in[3], x1, _plan_gather_direct, "gather_in_wait")
    w_in_f = jnp.pad(jnp.transpose(gin, (1, 0, 2)).reshape(D, in_width), ((0, 0), (0, in_pad - in_width)))
    ag_rest = _start_copies(shards[4:], zones[1:], _plan_gather_direct, "gather_rest_start", after=gin)
    sh2 = sh2 + ag_rest[4][0, 0]

    cc, s1, s2 = _rope_tables(positions[0])
    alog_row = jnp.pad(a_log, ((0, 0), (0, LANES - HD)))
    dtb_row = jnp.pad(dt_bias, ((0, 0), (0, LANES - HD)))
    col_k, col_v, col_d, col_z, col_ab = HA, 2 * HA, 3 * HA, 3 * HA + 3 * HD, 3 * HA + 4 * HD
    h2 = _pre_fwd(x1, mix_norm, sh2, sc2, "mix_pre")
    proj = _matmul(h2, w_in_f, "nn", F32, "mix_in_proj", tm=512, tn=2432, tk=4096)
    attn_bias = _log_multiplicity_table(_pick(S, ATTN_TILE))
    qa, ka = _attn_prep_fwd(proj, q_norm, k_norm, cc, s1, s2, "attn_prep")
    o_near, lse_near = _attn_fwd(qa, ka, proj, col_v, attn_bias, "attn_fwd")
    oa, oa_b, lse = _attn_far_fwd(qa, ka, proj, col_v, o_near, lse_near, "attn_far_fwd")
    dqkv = _conv_fwd(proj, col_d, conv_width, conv_full, "conv_fwd")
    *prep, t_inv = _delta_prep_fwd(dqkv, proj, col_ab, alog_row, dtb_row, "delta_prep")
    od_raw, states = _delta_scan_fwd(*prep, "delta_scan")
    od = _post_fwd(od_raw, proj, col_z, delta_out_norm, "delta_post")
    o = jnp.concatenate([oa_b, od], axis=1)
    gout, g2g, g2u, g2d = _wait_copies(ag_rest[0], ag_rest[1], ag_rest[2], ag_rest[3], o, _plan_gather_direct, "gather_rest_wait")
    w_out_f = gout.reshape(-1, D)
    w_gu2, w_d2 = gate_up(g2g, g2u), g2d.reshape(4 * fs, D)
    mo = _matmul(o, w_out_f, "nn", F32, "mix_out_proj", tm=1024, tn=1024, tk=4096)
    x2 = _residual_fwd(x1, gt2, mo, 1.0, "mix_res")

    x3, saved3 = _ffn_fwd(x2, ffn2_norm, sh3, sc3, gt3, w_gu2, w_d2, fs, "ffn2")
    loss_part, dy = _loss_head(x3, loss_target[0], "loss_head")
    loss = lax.psum(loss_part[0, 0], ("x", "y", "c"))

    dx2, dw_gu2, dw_d2, dgain3, dsh3, dsc3, dgt3 = _ffn_bwd(dy, saved3, ffn2_norm, sh3, sc3, gt3, w_gu2, w_d2, fs, "ffn2",
                                                            dw_gu_dtype=BF16)

    def scatter_start(slabs32, name):
        slabs16 = [s.astype(BF16) for s in slabs32]
        zones = []
        for s in slabs16:
            half = s.shape[1] // 2
            own = lax.dynamic_slice(s, (chip, ci * half, 0), (1, half, s.shape[2]))
            zones.append(lax.dynamic_update_slice(lax.empty((8, half, s.shape[2]), BF16), own, (dev, 0, 0)))
        return _start_copies(slabs16, zones, _plan_scatter_direct, name)

    rs_ffn2 = scatter_start([dw_gu2, dw_d2.reshape(4, fs, D)], "rs_ffn2_start")
    dmo, dgt2 = _residual_bwd(gt2 + rs_ffn2[4][0, 0], mo, dx2, 1.0, "mix_res_bwd")
    do = _matmul(dmo, w_out_f, "nt", F32, "mix_out_dx", tm=1024, tn=1024, tk=4096)
    dw_out = _matmul(o, dmo, "tn", BF16, "mix_out_dw", tm=1024, tn=1024, tk=2048)
    dq = _attn_bwd_dq(qa, ka, proj, col_v, oa, lse, do, 0, attn_bias, "attn_bwd_dq")
    dk, dv = _attn_bwd_dkv(qa, ka, proj, col_v, oa, lse, do, 0, attn_bias, "attn_bwd_dkv")
    far = _attn_far_bwd(qa, ka, proj, col_v, oa, lse, do, 0, "attn_far_bwd")
    dpq, dpk, dv, dq_gain, dk_gain = _attn_prep_bwd(proj, q_norm, k_norm, cc, s1, s2, (dq, dk, dv), far, "attn_prep_bwd")
    dod, dz, ddn_gain = _post_bwd(od_raw, proj, col_z, delta_out_norm, do, HA, "delta_post_bwd")
    cots = _delta_scan_bwd(*prep, states, dod, "delta_scan_bwd")
    ddq, ddk, ddv, dab, dalog, ddtb = _delta_prep_bwd(dqkv, proj, col_ab, alog_row, dtb_row, cots, t_inv, "delta_prep_bwd")
    dconv_in, dconv_w = _conv_bwd(proj, col_d, conv_width, conv_full, jnp.concatenate([ddq, ddk, ddv], axis=1), "conv_bwd")
    dproj = jnp.concatenate([dpq, dpk, dv, dconv_in, dz, dab], axis=1)
    dh2 = _matmul(dproj, w_in_f, "nt", F32, "mix_in_dx", tm=1024, tn=1024, tk=2816)
    dw_in = _matmul(h2, dproj, "tn", BF16, "mix_in_dw", tm=512, tn=2432, tk=2048)
    rs_mix = scatter_start([_cols_to_slabs(dw_in[:, :in_width], 4), dw_out.reshape(4, -1, D)], "rs_mix_start")
    dx1, dgain2, dsh2, dsc2 = _pre_bwd(x1, mix_norm, sh2, sc2, dh2, dx2, "mix_pre_bwd")

    rs_ffn1d = []

    def send_dw_d1(dw_d):
        rs_ffn1d.extend(scatter_start([dw_d.reshape(4, fs, D)], "rs_ffn1d_start"))
        return rs_ffn1d[4][0, 0]

    rs_gu1 = []

    def send_dw_gu1(slab):
        (other,) = _swap_halves_with_sibling([slab], "rs_sibling_swap")
        half = slab.shape[1] // 2
        mine = lax.dynamic_slice_in_dim(slab, ci * half, half, axis=1)
        p32, p16 = _add_cast([mine, other], [F32, BF16], "rs_chip_sum")
        rs_gu1.append(p32)
        rs_gu1.extend(_start_copies([p16], [lax.empty(p16.shape, BF16)], _plan_exchange_chips, "rs_chip_exchange_start"))
        return rs_gu1[5]

    dx0, dw_gu1, dw_d1, dgain1, dsh1, dsc1, dgt1 = _ffn_bwd(dx1, saved1, ffn1_norm, sh1, sc1, gt1 + rs_mix[4][0, 0], w_gu1, w_d1, fs,
                                                            "ffn1", on_dw_d=send_dw_d1, on_dw_gu=send_dw_gu1)

    (got,) = _wait_copies(rs_gu1[1], rs_gu1[2], rs_gu1[3], rs_gu1[4], dx0, _plan_exchange_chips, "rs_chip_exchange_wait")
    parts = [lax.dynamic_index_in_dim(rs_gu1[0], chip, axis=0, keepdims=True)]
    parts += [lax.dynamic_index_in_dim(got, (chip + k) % 4, axis=0, keepdims=True) for k in (1, 2, 3)]
    halves = [_add_cast(parts, [F32], "rs_total_0")[0][0]]
    arrived = _wait_copies(rs_ffn1d[0], rs_ffn1d[1], rs_ffn1d[2], rs_ffn1d[3], dx0, _plan_scatter_direct, "rs_ffn1d_wait")
    arrived += _wait_copies(rs_mix[0], rs_mix[1], rs_mix[2], rs_mix[3], dx0, _plan_scatter_direct, "rs_mix_wait")
    arrived += _wait_copies(rs_ffn2[0], rs_ffn2[1], rs_ffn2[2], rs_ffn2[3], dx0, _plan_scatter_direct, "rs_ffn2_wait")
    for t, zone in enumerate(arrived):
        halves.append(_add_cast([zone[d:d + 1] for d in range(8)], [F32], "rs_total_%d" % (t + 2))[0][0])
    theirs = _send_halves_to_sibling(halves, "rs_sibling_join")
    h_gu1, h_d1, h_in, h_out, h_gu2, h_d2 = zip(halves, theirs)

    n_small = N_MOD * D + 3 * D + 5 * LANES + CONV_WIDTH * conv_width
    cols_small = -(-n_small // (8 * LANES)) * LANES
    small = _flat_pad([dsh1, dsc1, dgt1, dsh2, dsc2, dgt2, dsh3, dsc3, dgt3, dgain1, dgain2, dgain3,
                       dq_gain, dk_gain, dalog, ddtb, ddn_gain, dconv_w], 8, cols_small)
    got2 = _all_gather_small(small, "gather_small_grads", after=theirs[0])
    small_sum = _add_cast([got2[d:d + 1] for d in range(8)], [F32], "sum_small_grads")[0].reshape(-1)
    dmod_all = got2.reshape(8, -1)[:, :N_MOD * D]
    off = [0]

    def take(n):
        off[0] += n
        return small_sum[off[0] - n:off[0]]

    g_b_ada = take(N_MOD * D).reshape(1, -1)
    g_ffn1_norm, g_mix_norm, g_ffn2_norm = take(D).reshape(1, D), take(D).reshape(1, D), take(D).reshape(1, D)
    g_q_norm, g_k_norm = take(LANES).reshape(1, -1), take(LANES).reshape(1, -1)
    g_a_log, g_dt_bias = take(LANES)[:HD].reshape(1, HD), take(LANES)[:HD].reshape(1, HD)
    g_dn = take(LANES).reshape(1, -1)
    g_conv_full = take(CONV_WIDTH * conv_width).reshape(CONV_WIDTH, conv_width)
    g_conv = lax.dynamic_slice(g_conv_full, (0, chip * conv_shard), (CONV_WIDTH, conv_shard))

    res = {}

    def upd(name, w, pair, col, m, v):
        res[name] = tuple(_adamw_halves(w, pair[0], pair[1], col, m, v, "adamw_" + name))

    upd("ffn1_w_gate", ffn1_w_gate, h_gu1, 0, m_ffn1_w_gate, v_ffn1_w_gate)
    upd("ffn1_w_up", ffn1_w_up, h_gu1, 1, m_ffn1_w_up, v_ffn1_w_up)
    upd("ffn1_w_down", ffn1_w_down, h_d1, 0, m_ffn1_w_down, v_ffn1_w_down)
    upd("w_in", w_in, h_in, 0, m_w_in, v_w_in)
    upd("w_out", w_out, h_out, 0, m_w_out, v_w_out)
    upd("ffn2_w_gate", ffn2_w_gate, h_gu2, 0, m_ffn2_w_gate, v_ffn2_w_gate)
    upd("ffn2_w_up", ffn2_w_up, h_gu2, 1, m_ffn2_w_up, v_ffn2_w_up)
    upd("ffn2_w_down", ffn2_w_down, h_d2, 0, m_ffn2_w_down, v_ffn2_w_down)
    d_cv, nm_cv, nv_cv = _adamw(conv_w[0], g_conv, m_conv_w[0], v_conv_w[0], "adamw_conv_w")
    res["conv_w"] = (g_conv[None], d_cv[None], nm_cv[None], nv_cv[None])

    dmod_mine = lax.dynamic_slice(dmod_all, (0, chip * n_mod_shard), (8, n_mod_shard))
    g, d, nm, nv = _adamw_outer(w_ada[0], m_w_ada[0], v_w_ada[0], jnp.transpose(c_all), dmod_mine, "adamw_w_ada")
    res["w_ada"] = (g[None], d[None], nm[None], nv[None])

    rep = [("b_ada", b_ada, g_b_ada, m_b_ada, v_b_ada), ("ffn1_norm", ffn1_norm, g_ffn1_norm, m_ffn1_norm, v_ffn1_norm),
           ("mix_norm", mix_norm, g_mix_norm, m_mix_norm, v_mix_norm), ("ffn2_norm", ffn2_norm, g_ffn2_norm, m_ffn2_norm, v_ffn2_norm),
           ("q_norm", q_norm, g_q_norm, m_q_norm, v_q_norm), ("k_norm", k_norm, g_k_norm, m_k_norm, v_k_norm),
           ("a_log", a_log, g_a_log, m_a_log, v_a_log), ("dt_bias", dt_bias, g_dt_bias, m_dt_bias, v_dt_bias),
           ("delta_out_norm", delta_out_norm, g_dn, m_delta_out_norm, v_delta_out_norm)]
    n_rep = sum(-(-r[1].shape[1] // LANES) * LANES for r in rep)
    cols_rep = -(-n_rep // (8 * LANES)) * LANES

    def pack_rep(idx):
        return _flat_pad([jnp.pad(r[idx], ((0, 0), (0, -r[idx].shape[1] % LANES))) for r in rep], 8, cols_rep)

    d_rep, nm_rep, nv_rep = [a.reshape(-1) for a in _adamw(pack_rep(1), pack_rep(2), pack_rep(3), pack_rep(4), "adamw_small")]
    o2 = 0
    for name, w, g, _, _ in rep:
        n = w.shape[1]
        res[name] = (g, d_rep[o2:o2 + n].reshape(1, n), nm_rep[o2:o2 + n].reshape(1, n), nv_rep[o2:o2 + n].reshape(1, n))
        o2 += -(-n // LANES) * LANES

    order = ["w_ada", "b_ada", "ffn1_norm", "ffn1_w_gate", "ffn1_w_up", "ffn1_w_down", "mix_norm", "w_in", "conv_w", "q_norm",
             "k_norm", "a_log", "dt_bias", "delta_out_norm", "w_out", "ffn2_norm", "ffn2_w_gate", "ffn2_w_up", "ffn2_w_down"]
    return (loss, dx0[None], *[res[n][0] for n in order], *[res[n][1] for n in order],
            *[res[n][2] for n in order], *[res[n][3] for n in order])
```

```python
import functools
import math

import jax
import jax.numpy as jnp
from jax import lax
from jax.experimental import pallas as pl
from jax.experimental.pallas import tpu as pltpu

F32 = jnp.float32
BF16 = jnp.bfloat16
MESH = pl.DeviceIdType.MESH

HEAD_DIM = 128
N_ATTN_HEADS = 8
N_DELTA_HEADS = 8
DILATED_PATTERNS = ((128, 1), (512, 4), (2048, 16))
ROPE_THETA = 500000.0
ROPE_DIM = HEAD_DIM // 4
CONV_WIDTH = 4
CHUNK = 64
NORM_EPS = 1e-6
N_MOD = 9
ADAM_LR = 0.001
ADAM_B1 = 0.9
ADAM_B2 = 0.999
ADAM_EPS = 1e-08
ADAM_WD = 0.01
ADAM_STEP = 10

LANES = 128
VMEM_LIMIT = 56 * 1024 * 1024
ATTN_TILE = 512
HIGHEST = lax.Precision.HIGHEST


def _cparams(sem=None):
    return pltpu.CompilerParams(dimension_semantics=sem, vmem_limit_bytes=VMEM_LIMIT)


def _pick(dim, pref):
    if dim <= pref:
        return dim
    t = (pref // LANES) * LANES
    while t >= LANES:
        if dim % t == 0:
            return t
        t -= LANES
    return dim


def _sigmoid(x):
    return 1.0 / (1.0 + jnp.exp(-x))


def _silu(x):
    return x * _sigmoid(x)


def _softplus(x):
    return jnp.maximum(x, 0.0) + jnp.log(1.0 + jnp.exp(-jnp.abs(x)))


def _rms(x, gain):
    return x * lax.rsqrt(jnp.mean(x * x, axis=-1, keepdims=True) + NORM_EPS) * gain


def _l2(x):
    return x * lax.rsqrt(jnp.sum(x * x, axis=-1, keepdims=True) + NORM_EPS)


def _modulate(x, gain, shift, scale):
    return _rms(x, gain) * (1.0 + scale) + shift


def _dot(a, b):
    return lax.dot_general(a, b, (((1,), (0,)), ((), ())), precision=HIGHEST, preferred_element_type=F32)


def _bdot(a, b, dims):
    return lax.dot_general(a.astype(BF16), b.astype(BF16), (dims, ((), ())), preferred_element_type=F32)


_NN, _NT, _TN = ((1,), (0,)), ((1,), (1,)), ((0,), (0,))
HIGH = lax.Precision.HIGH


def _dot3(a, b, dims=_NN):
    return lax.dot_general(a, b, (dims, ((), ())), precision=HIGH, preferred_element_type=F32)


@jax.custom_vjp
def _mm_nn(a, b):
    return _bdot(a, b, _NN)


_mm_nn.defvjp(lambda a, b: (_bdot(a, b, _NN), (a, b)),
              lambda res, g: (_bdot(g, res[1], _NT), _bdot(res[0], g, _TN)))


@jax.custom_vjp
def _mm_nt(a, b):
    return _bdot(a, b, _NT)


_mm_nt.defvjp(lambda a, b: (_bdot(a, b, _NT), (a, b)),
              lambda res, g: (_bdot(g, res[1], _NN), _bdot(g, res[0], _TN)))


@jax.custom_vjp
def _mm_tn(a, b):
    return _bdot(a, b, _TN)


_mm_tn.defvjp(lambda a, b: (_bdot(a, b, _TN), (a, b)),
              lambda res, g: (_bdot(res[1], g, _NT), _bdot(res[0], g, _NN)))


@jax.custom_vjp
def _tri_inv_saved(a, t_inv):
    return t_inv


_tri_inv_saved.defvjp(lambda a, t_inv: (t_inv, t_inv),
                      lambda t_inv, g: (-_dot3(t_inv, _dot3(g, t_inv, _NT), _TN), jnp.zeros_like(t_inv)))


_MM_DIMS = {"nn": ((1,), (0,)), "nt": ((1,), (1,)), "tn": ((0,), (0,))}


def _matmul(a, b, mode, out_dtype, name, tm=1024, tn=1024, tk=1024, col_slabs=False, epilogue=None, after=None):
    if mode == "nn":
        (M, K), (_, N) = a.shape, b.shape
    elif mode == "nt":
        (M, K), (N, _) = a.shape, b.shape
    else:
        (K, M), (_, N) = a.shape, b.shape
    tm, tn, tk = _pick(M, tm), _pick(N, tn), _pick(K, tk)
    nk = K // tk
    dims = _MM_DIMS[mode]
    epi_fn, extra_in, outs = epilogue if epilogue is not None else (None, [], None)
    if outs is None:
        if col_slabs:
            outs = [((N // tn, M, tn), out_dtype, (None, tm, tn), lambda i, j: (j, i, 0))]
        else:
            outs = [((M, N), out_dtype, (tm, tn), lambda i, j: (i, j))]
    n_in, n_out = len(extra_in), len(outs)

    tied = [] if after is None else [after]

    def body(a_ref, b_ref, *rest):
        rest = rest[len(tied):]
        in_refs, out_refs = rest[:n_in], rest[n_in:n_in + n_out]
        k = pl.program_id(2)
        p = _bdot(a_ref[...], b_ref[...], dims)

        def finish(tile):
            if epi_fn is None:
                out_refs[0][...] = tile.astype(out_dtype)
            else:
                epi_fn(tile, in_refs, out_refs)

        if nk == 1:
            finish(p)
        else:
            acc_ref = rest[-1]

            @pl.when(k == 0)
            def _():
                acc_ref[...] = p

            @pl.when((k > 0) & (k < nk - 1))
            def _():
                acc_ref[...] += p

            @pl.when(k == nk - 1)
            def _():
                finish(acc_ref[...] + p)

    def ij(index_map):
        return lambda i, j, k: index_map(i, j)

    a_spec = pl.BlockSpec((tk, tm), lambda i, j, k: (k, i)) if mode == "tn" else pl.BlockSpec((tm, tk), lambda i, j, k: (i, k))
    b_spec = pl.BlockSpec((tn, tk), lambda i, j, k: (j, k)) if mode == "nt" else pl.BlockSpec((tk, tn), lambda i, j, k: (k, j))
    res = pl.pallas_call(
        body, name=name, grid=(M // tm, N // tn, nk),
        in_specs=[a_spec, b_spec] + [pl.BlockSpec(memory_space=pl.ANY)] * len(tied) + [pl.BlockSpec(blk, ij(im)) for _, blk, im in extra_in],
        out_specs=[pl.BlockSpec(blk, ij(im)) for _, _, blk, im in outs],
        out_shape=[jax.ShapeDtypeStruct(shp, dt) for shp, dt, _, _ in outs],
        scratch_shapes=[pltpu.VMEM((tm, tn), F32)] if nk > 1 else [],
        compiler_params=_cparams(("parallel", "parallel", "arbitrary")),
    )(a, b, *tied, *[arr for arr, _, _ in extra_in])
    return res if epilogue is not None else res[0]


def _row_spec(tr, d):
    return pl.BlockSpec((tr, d), lambda i: (i, 0))


def _vec_spec(d):
    return pl.BlockSpec((1, d), lambda i: (0, 0))


def _pre_fwd(x, gain, shift, scale, name):
    S, D = x.shape
    tr = _pick(S, 256)

    def body(x_ref, g_ref, sh_ref, sc_ref, h_ref):
        h_ref[...] = _modulate(x_ref[...], g_ref[...], sh_ref[...], sc_ref[...]).astype(BF16)

    return pl.pallas_call(
        body, name=name, grid=(S // tr,),
        in_specs=[_row_spec(tr, D), _vec_spec(D), _vec_spec(D), _vec_spec(D)],
        out_specs=_row_spec(tr, D), out_shape=jax.ShapeDtypeStruct((S, D), BF16),
        compiler_params=_cparams(("parallel",)),
    )(x, gain, shift, scale)


def _pre_bwd(x, gain, shift, scale, dh, dx_in, name):
    S, D = x.shape
    tr = _pick(S, 256)

    def body(x_ref, g_ref, sh_ref, sc_ref, dh_ref, dxin_ref, dx_ref, dg_ref, dsh_ref, dsc_ref):
        _, vjp = jax.vjp(_modulate, x_ref[...], g_ref[...], sh_ref[...], sc_ref[...])
        dx, dg, dsh, dsc = vjp(dh_ref[...])
        dx_ref[...] = dxin_ref[...] + dx

        @pl.when(pl.program_id(0) == 0)
        def _():
            dg_ref[...] = jnp.zeros_like(dg_ref)
            dsh_ref[...] = jnp.zeros_like(dsh_ref)
            dsc_ref[...] = jnp.zeros_like(dsc_ref)

        dg_ref[...] += dg
        dsh_ref[...] += dsh
        dsc_ref[...] += dsc

    vec = jax.ShapeDtypeStruct((1, D), F32)
    return pl.pallas_call(
        body, name=name, grid=(S // tr,),
        in_specs=[_row_spec(tr, D), _vec_spec(D), _vec_spec(D), _vec_spec(D), _row_spec(tr, D), _row_spec(tr, D)],
        out_specs=[_row_spec(tr, D), _vec_spec(D), _vec_spec(D), _vec_spec(D)],
        out_shape=[jax.ShapeDtypeStruct((S, D), F32), vec, vec, vec],
        compiler_params=_cparams(("arbitrary",)),
    )(x, gain, shift, scale, dh, dx_in)


def _residual_fwd(x, gate, f, coef, name):
    S, D = x.shape
    tr = _pick(S, 256)

    def body(x_ref, g_ref, f_ref, o_ref):
        o_ref[...] = x_ref[...] + coef * g_ref[...] * f_ref[...]

    return pl.pallas_call(
        body, name=name, grid=(S // tr,),
        in_specs=[_row_spec(tr, D), _vec_spec(D), _row_spec(tr, D)],
        out_specs=_row_spec(tr, D), out_shape=jax.ShapeDtypeStruct((S, D), F32),
        compiler_params=_cparams(("parallel",)),
    )(x, gate, f)


def _residual_bwd(gate, f, dxn, coef, name):
    S, D = f.shape
    tr = _pick(S, 256)

    def body(g_ref, f_ref, d_ref, df_ref, dg_ref):
        d = d_ref[...]
        df_ref[...] = (coef * g_ref[...] * d).astype(BF16)

        @pl.when(pl.program_id(0) == 0)
        def _():
            dg_ref[...] = jnp.zeros_like(dg_ref)

        dg_ref[...] += jnp.sum(coef * f_ref[...] * d, axis=0, keepdims=True)

    return pl.pallas_call(
        body, name=name, grid=(S // tr,),
        in_specs=[_vec_spec(D), _row_spec(tr, D), _row_spec(tr, D)],
        out_specs=[_row_spec(tr, D), _vec_spec(D)],
        out_shape=[jax.ShapeDtypeStruct((S, D), BF16), jax.ShapeDtypeStruct((1, D), F32)],
        compiler_params=_cparams(("arbitrary",)),
    )(gate, f, dxn)


def _swiglu_fn(a, b):
    return _silu(a) * b


def _loss_head(y, target, name):
    S, D = y.shape
    tr = _pick(S, 256)

    def body(y_ref, t_ref, l_ref, dy_ref):
        e = y_ref[...] - t_ref[...]
        dy_ref[...] = e * (1.0 / D)

        @pl.when(pl.program_id(0) == 0)
        def _():
            l_ref[...] = jnp.zeros_like(l_ref)

        l_ref[...] += jnp.sum(jnp.sum(e * e, axis=-1, keepdims=True), axis=0, keepdims=True) * (0.5 / D)

    return pl.pallas_call(
        body, name=name, grid=(S // tr,),
        in_specs=[_row_spec(tr, D), _row_spec(tr, D)],
        out_specs=[pl.BlockSpec((1, 1), lambda i: (0, 0)), _row_spec(tr, D)],
        out_shape=[jax.ShapeDtypeStruct((1, 1), F32), jax.ShapeDtypeStruct((S, D), F32)],
        compiler_params=_cparams(("arbitrary",)),
    )(y, target)


def _rope(y, cc, s1, s2):
    return y * cc + pltpu.roll(y, LANES - ROPE_DIM // 2, 1) * s1 + pltpu.roll(y, ROPE_DIM // 2, 1) * s2


def _rope_t(d, cc, s1, s2):
    return d * cc + pltpu.roll(d * s1, ROPE_DIM // 2, 1) + pltpu.roll(d * s2, LANES - ROPE_DIM // 2, 1)


def _head_spec(tr, col0):
    return pl.BlockSpec((tr, HEAD_DIM), lambda i, h: (i, col0 + h))


def _tab_spec(tr):
    return pl.BlockSpec((tr, HEAD_DIM), lambda i, h: (i, 0))


def _gain_spec():
    return pl.BlockSpec((1, HEAD_DIM), lambda i, h: (0, 0))


def _attn_prep_fwd(proj, q_gain, k_gain, cc, s1, s2, name):
    S = proj.shape[0]
    H = N_ATTN_HEADS
    tr = _pick(S, 512)

    def body(q_ref, k_ref, qg_ref, kg_ref, cc_ref, s1_ref, s2_ref, qo_ref, ko_ref):
        cc, s1, s2 = cc_ref[...], s1_ref[...], s2_ref[...]
        qo_ref[...] = _rope(_rms(q_ref[...], qg_ref[...]), cc, s1, s2)
        ko_ref[...] = _rope(_rms(k_ref[...], kg_ref[...]), cc, s1, s2)

    out = jax.ShapeDtypeStruct((S, H * HEAD_DIM), F32)
    return pl.pallas_call(
        body, name=name, grid=(S // tr, H),
        in_specs=[_head_spec(tr, 0), _head_spec(tr, H), _gain_spec(), _gain_spec(), _tab_spec(tr), _tab_spec(tr), _tab_spec(tr)],
        out_specs=[_head_spec(tr, 0), _head_spec(tr, 0)], out_shape=[out, out],
        compiler_params=_cparams(("parallel", "parallel")),
    )(proj, proj, q_gain, k_gain, cc, s1, s2)


def _attn_prep_bwd(proj, q_gain, k_gain, cc, s1, s2, near, far, name):
    S = proj.shape[0]
    H = N_ATTN_HEADS
    tr = _pick(S, 512)

    def body(q_ref, k_ref, qg_ref, kg_ref, cc_ref, s1_ref, s2_ref, dqn_ref, dkn_ref, dvn_ref, dqf_ref, dkf_ref, dvf_ref,
             dpq_ref, dpk_ref, dpv_ref, dqg_ref, dkg_ref):
        cc, s1, s2 = cc_ref[...], s1_ref[...], s2_ref[...]
        dq_ref = dqn_ref[...] + dqf_ref[...]
        dk_ref = dkn_ref[...] + dkf_ref[...]
        dpv_ref[...] = (dvn_ref[...] + dvf_ref[...]).astype(BF16)

        @pl.when((pl.program_id(0) == 0) & (pl.program_id(1) == 0))
        def _():
            dqg_ref[...] = jnp.zeros_like(dqg_ref)
            dkg_ref[...] = jnp.zeros_like(dkg_ref)

        _, vjp_q = jax.vjp(_rms, q_ref[...], qg_ref[...])
        dxq, dgq = vjp_q(_rope_t(dq_ref[...], cc, s1, s2))
        _, vjp_k = jax.vjp(_rms, k_ref[...], kg_ref[...])
        dxk, dgk = vjp_k(_rope_t(dk_ref[...], cc, s1, s2))
        dpq_ref[...] = dxq.astype(BF16)
        dpk_ref[...] = dxk.astype(BF16)
        dqg_ref[...] += dgq
        dkg_ref[...] += dgk

    out = jax.ShapeDtypeStruct((S, H * HEAD_DIM), BF16)
    gout = jax.ShapeDtypeStruct((1, HEAD_DIM), F32)
    return pl.pallas_call(
        body, name=name, grid=(S // tr, H),
        in_specs=[_head_spec(tr, 0), _head_spec(tr, H), _gain_spec(), _gain_spec(), _tab_spec(tr), _tab_spec(tr), _tab_spec(tr)]
        + [_head_spec(tr, 0)] * 6,
        out_specs=[_head_spec(tr, 0)] * 3 + [_gain_spec(), _gain_spec()], out_shape=[out, out, out, gout, gout],
        compiler_params=_cparams(("arbitrary", "arbitrary")),
    )(proj, proj, q_gain, k_gain, cc, s1, s2, *near, *far)


def _multiplicity(j, t):
    ti = lax.broadcasted_iota(jnp.int32, (t, t), 0)
    si = lax.broadcasted_iota(jnp.int32, (t, t), 1)
    delta = j * t + ti - si
    cnt = jnp.zeros((t, t), F32)
    for window, dil in NEAR_PATTERNS:
        ok = (delta >= 0) & ((delta & (dil - 1)) == 0) & (delta <= window)
        cnt = cnt + ok.astype(F32)
    return cnt


_NEG = -1e30
NEAR_PATTERNS = DILATED_PATTERNS[:2]
NEAR_WINDOW = max(w for w, _ in NEAR_PATTERNS)
FAR_WINDOW, FAR_DIL = DILATED_PATTERNS[2]


def _log_multiplicity_table(t):
    cnt = jnp.stack([_multiplicity(j, t) for j in range(NEAR_WINDOW // t + 1)])
    return jnp.where(cnt > 0.0, jnp.log(jnp.maximum(cnt, 1.0)), _NEG)


def _bias_spec(t):
    return pl.BlockSpec((NEAR_WINDOW // t + 1, t, t), lambda h, i, j: (0, 0, 0))


def _attn_fwd(q, k, proj, v_col0, bias, name):
    S = q.shape[0]
    H = N_ATTN_HEADS
    t = _pick(S, ATTN_TILE)
    nq = S // t
    nj = NEAR_WINDOW // t + 1
    scale = HEAD_DIM ** -0.5

    def body(q_ref, k_ref, v_ref, b_ref, o_ref, lse_ref, m_sc, l_sc, acc_sc):
        qb, j = pl.program_id(1), pl.program_id(2)

        @pl.when(j == 0)
        def _():
            m_sc[...] = jnp.full_like(m_sc, _NEG)
            l_sc[...] = jnp.zeros_like(l_sc)
            acc_sc[...] = jnp.zeros_like(acc_sc)

        @pl.when(qb - j >= 0)
        def _():
            s = _bdot(q_ref[...], k_ref[...], ((1,), (1,))) * scale + b_ref[j]
            m_prev = m_sc[...]
            m_new = jnp.maximum(m_prev, jnp.max(s, axis=-1, keepdims=True))
            alpha = jnp.exp(m_prev - m_new)
            p = jnp.exp(s - m_new)
            l_sc[...] = alpha * l_sc[...] + jnp.sum(p, axis=-1, keepdims=True)
            acc_sc[...] = alpha * acc_sc[...] + _bdot(p, v_ref[...], ((1,), (0,)))
            m_sc[...] = m_new

        @pl.when(j == nj - 1)
        def _():
            o_ref[...] = acc_sc[...] / l_sc[...]
            lse_ref[...] = jnp.broadcast_to(m_sc[...] + jnp.log(l_sc[...]), (t, HEAD_DIM))

    qspec = pl.BlockSpec((t, HEAD_DIM), lambda h, i, j: (i, h))
    kspec = pl.BlockSpec((t, HEAD_DIM), lambda h, i, j: (jnp.maximum(i - j, 0), h))
    vspec = pl.BlockSpec((t, HEAD_DIM), lambda h, i, j: (jnp.maximum(i - j, 0), v_col0 + h))
    return pl.pallas_call(
        body, name=name, grid=(H, nq, nj),
        in_specs=[qspec, kspec, vspec, _bias_spec(t)], out_specs=[qspec, qspec],
        out_shape=[jax.ShapeDtypeStruct((S, H * HEAD_DIM), F32), jax.ShapeDtypeStruct((S, H * HEAD_DIM), F32)],
        scratch_shapes=[pltpu.VMEM((t, 1), F32), pltpu.VMEM((t, 1), F32), pltpu.VMEM((t, HEAD_DIM), F32)],
        compiler_params=_cparams(("parallel", "parallel", "arbitrary")),
    )(q, k, proj, bias)


def _far_rows(r, n):
    return pl.ds(r, n, stride=FAR_DIL)


def _far_band_bias(n):
    i = lax.broadcasted_iota(jnp.int32, (n, n), 0)
    j = lax.broadcasted_iota(jnp.int32, (n, n), 1)
    return jnp.where((i >= j) & (i - j <= FAR_WINDOW // FAR_DIL), 0.0, _NEG)


def _col_spec(S, col0):
    return pl.BlockSpec((S, HEAD_DIM), lambda h: (0, col0 + h))


def _attn_far_fwd(q, k, proj, v_col0, o_near, lse_near, name):
    S = q.shape[0]
    H = N_ATTN_HEADS
    n = S // FAR_DIL
    scale = HEAD_DIM ** -0.5

    def body(q_ref, k_ref, v_ref, on_ref, ln_ref, o_ref, ob_ref, lse_ref):
        bias = _far_band_bias(n)
        for r in range(FAR_DIL):
            rows = _far_rows(r, n)
            s = _bdot(q_ref[rows, :], k_ref[rows, :], _NT) * scale + bias
            m = jnp.max(s, axis=-1, keepdims=True)
            p = jnp.exp(s - m)
            l = jnp.sum(p, axis=-1, keepdims=True)
            o_far = _bdot(p, v_ref[rows, :], _NN) / l
            lse_far = m + jnp.log(l)
            lse_near = jnp.max(ln_ref[rows, :], axis=-1, keepdims=True)
            top = jnp.maximum(lse_near, lse_far)
            lse = top + jnp.log(jnp.exp(lse_near - top) + jnp.exp(lse_far - top))
            o_ref[rows, :] = jnp.exp(lse_near - lse) * on_ref[rows, :] + jnp.exp(lse_far - lse) * o_far
            lse_ref[rows, :] = jnp.broadcast_to(lse, (n, HEAD_DIM))
        ob_ref[...] = o_ref[...].astype(BF16)

    cs = _col_spec(S, 0)
    return pl.pallas_call(
        body, name=name, grid=(H,),
        in_specs=[cs, cs, _col_spec(S, v_col0), cs, cs], out_specs=[cs, cs, cs],
        out_shape=[jax.ShapeDtypeStruct((S, H * HEAD_DIM), F32), jax.ShapeDtypeStruct((S, H * HEAD_DIM), BF16),
                   jax.ShapeDtypeStruct((S, H * HEAD_DIM), F32)],
        compiler_params=_cparams(("parallel",)),
    )(q, k, proj, o_near, lse_near)


def _attn_far_bwd(q, k, proj, v_col0, o, lse, do, do_col0, name):
    S = q.shape[0]
    H = N_ATTN_HEADS
    n = S // FAR_DIL
    scale = HEAD_DIM ** -0.5

    def body(q_ref, k_ref, v_ref, o_ref, lse_ref, do_ref, dq_ref, dk_ref, dv_ref):
        bias = _far_band_bias(n)
        for r in range(FAR_DIL):
            rows = _far_rows(r, n)
            q, k, v, do = q_ref[rows, :], k_ref[rows, :], v_ref[rows, :], do_ref[rows, :]
            dsum = jnp.sum(do * o_ref[rows, :], axis=-1, keepdims=True)
            lse = jnp.max(lse_ref[rows, :], axis=-1, keepdims=True)
            p = jnp.exp(_bdot(q, k, _NT) * scale + bias - lse)
            ds = p * (_bdot(do, v, _NT) - dsum)
            dq_ref[rows, :] = _bdot(ds, k, _NN) * scale
            dk_ref[rows, :] = _bdot(ds, q, _TN) * scale
            dv_ref[rows, :] = _bdot(p, do, _TN)

    cs = _col_spec(S, 0)
    shp = jax.ShapeDtypeStruct((S, H * HEAD_DIM), F32)
    return pl.pallas_call(
        body, name=name, grid=(H,),
        in_specs=[cs, cs, _col_spec(S, v_col0), cs, cs, _col_spec(S, do_col0)], out_specs=[cs, cs, cs], out_shape=[shp, shp, shp],
        compiler_params=_cparams(("parallel",)),
    )(q, k, proj, o, lse, do)


def _attn_probs(q, k, lse, bias_tile, scale):
    return jnp.exp(_bdot(q, k, ((1,), (1,))) * scale + bias_tile - lse)


def _attn_bwd_dq(q, k, proj, v_col0, o, lse, do, do_col0, bias, name):
    S = q.shape[0]
    H = N_ATTN_HEADS
    t = _pick(S, ATTN_TILE)
    nq = S // t
    nj = NEAR_WINDOW // t + 1
    scale = HEAD_DIM ** -0.5

    def body(q_ref, k_ref, v_ref, o_ref, lse_ref, do_ref, b_ref, dq_ref, acc_sc):
        qb, j = pl.program_id(1), pl.program_id(2)

        @pl.when(j == 0)
        def _():
            acc_sc[...] = jnp.zeros_like(acc_sc)

        @pl.when(qb - j >= 0)
        def _():
            do = do_ref[...]
            dsum = jnp.sum(do * o_ref[...], axis=-1, keepdims=True)
            lse = jnp.max(lse_ref[...], axis=-1, keepdims=True)
            p = _attn_probs(q_ref[...], k_ref[...], lse, b_ref[j], scale)
            dp = _bdot(do, v_ref[...], ((1,), (1,)))
            ds = p * (dp - dsum)
            acc_sc[...] += _bdot(ds, k_ref[...], ((1,), (0,))) * scale

        @pl.when(j == nj - 1)
        def _():
            dq_ref[...] = acc_sc[...]

    qspec = pl.BlockSpec((t, HEAD_DIM), lambda h, i, j: (i, h))
    dospec = pl.BlockSpec((t, HEAD_DIM), lambda h, i, j: (i, do_col0 + h))
    kspec = pl.BlockSpec((t, HEAD_DIM), lambda h, i, j: (jnp.maximum(i - j, 0), h))
    vspec = pl.BlockSpec((t, HEAD_DIM), lambda h, i, j: (jnp.maximum(i - j, 0), v_col0 + h))
    return pl.pallas_call(
        body, name=name, grid=(H, nq, nj),
        in_specs=[qspec, kspec, vspec, qspec, qspec, dospec, _bias_spec(t)], out_specs=qspec,
        out_shape=jax.ShapeDtypeStruct((S, H * HEAD_DIM), F32),
        scratch_shapes=[pltpu.VMEM((t, HEAD_DIM), F32)],
        compiler_params=_cparams(("parallel", "parallel", "arbitrary")),
    )(q, k, proj, o, lse, do, bias)


def _attn_bwd_dkv(q, k, proj, v_col0, o, lse, do, do_col0, bias, name):
    S = q.shape[0]
    H = N_ATTN_HEADS
    t = _pick(S, ATTN_TILE)
    nq = S // t
    nj = NEAR_WINDOW // t + 1
    scale = HEAD_DIM ** -0.5

    def body(q_ref, k_ref, v_ref, o_ref, lse_ref, do_ref, b_ref, dk_ref, dv_ref, dk_sc, dv_sc):
        kb, j = pl.program_id(1), pl.program_id(2)

        @pl.when(j == 0)
        def _():
            dk_sc[...] = jnp.zeros_like(dk_sc)
            dv_sc[...] = jnp.zeros_like(dv_sc)

        @pl.when(kb + j < nq)
        def _():
            do = do_ref[...]
            dsum = jnp.sum(do * o_ref[...], axis=-1, keepdims=True)
            lse = jnp.max(lse_ref[...], axis=-1, keepdims=True)
            p = _attn_probs(q_ref[...], k_ref[...], lse, b_ref[j], scale)
            dp = _bdot(do, v_ref[...], ((1,), (1,)))
            ds = p * (dp - dsum)
            dv_sc[...] += _bdot(p, do, ((0,), (0,)))
            dk_sc[...] += _bdot(ds, q_ref[...], ((0,), (0,))) * scale

        @pl.when(j == nj - 1)
        def _():
            dk_ref[...] = dk_sc[...]
            dv_ref[...] = dv_sc[...]

    def qrow(h, i, j):
        return jnp.minimum(i + j, nq - 1)

    qspec = pl.BlockSpec((t, HEAD_DIM), lambda h, i, j: (qrow(h, i, j), h))
    dospec = pl.BlockSpec((t, HEAD_DIM), lambda h, i, j: (qrow(h, i, j), do_col0 + h))
    kspec = pl.BlockSpec((t, HEAD_DIM), lambda h, i, j: (i, h))
    vspec = pl.BlockSpec((t, HEAD_DIM), lambda h, i, j: (i, v_col0 + h))
    return pl.pallas_call(
        body, name=name, grid=(H, nq, nj),
        in_specs=[qspec, kspec, vspec, qspec, qspec, dospec, _bias_spec(t)], out_specs=[kspec, kspec],
        out_shape=[jax.ShapeDtypeStruct((S, H * HEAD_DIM), F32), jax.ShapeDtypeStruct((S, H * HEAD_DIM), F32)],
        scratch_shapes=[pltpu.VMEM((t, HEAD_DIM), F32), pltpu.VMEM((t, HEAD_DIM), F32)],
        compiler_params=_cparams(("parallel", "parallel", "arbitrary")),
    )(q, k, proj, o, lse, do, bias)


def _conv_pre(x_ref, w_ref):
    x = x_ref[...]
    rows = lax.broadcasted_iota(jnp.int32, x.shape, 0)
    shifted = [x]
    acc = x * w_ref[pl.ds(CONV_WIDTH - 1, 1), :]
    for sft in range(1, CONV_WIDTH):
        xs = jnp.where(rows >= sft, pltpu.roll(x, sft, 0), 0.0)
        shifted.append(xs)
        acc = acc + xs * w_ref[pl.ds(CONV_WIDTH - 1 - sft, 1), :]
    return acc, shifted


def _conv_fwd(proj, col0, width, w, name):
    S = proj.shape[0]

    def body(x_ref, w_ref, y_ref):
        acc, _ = _conv_pre(x_ref, w_ref)
        y_ref[...] = _silu(acc)

    return pl.pallas_call(
        body, name=name, grid=(width // LANES,),
        in_specs=[pl.BlockSpec((S, LANES), lambda c: (0, col0 + c)), pl.BlockSpec((CONV_WIDTH, LANES), lambda c: (0, c))],
        out_specs=pl.BlockSpec((S, LANES), lambda c: (0, c)),
        out_shape=jax.ShapeDtypeStruct((S, width), F32),
        compiler_params=_cparams(("parallel",)),
    )(proj, w)


def _conv_bwd(proj, col0, width, w, dy, name):
    S = proj.shape[0]

    def body(x_ref, w_ref, d_ref, dx_ref, dw_ref):
        acc, shifted = _conv_pre(x_ref, w_ref)
        sig = _sigmoid(acc)
        da = d_ref[...] * (sig * (1.0 + acc * (1.0 - sig)))
        rows = lax.broadcasted_iota(jnp.int32, da.shape, 0)
        dx = da * w_ref[pl.ds(CONV_WIDTH - 1, 1), :]
        dw_ref[pl.ds(CONV_WIDTH - 1, 1), :] = jnp.sum(da * shifted[0], axis=0, keepdims=True)
        for sft in range(1, CONV_WIDTH):
            back = jnp.where(rows < S - sft, pltpu.roll(da, S - sft, 0), 0.0)
            dx = dx + back * w_ref[pl.ds(CONV_WIDTH - 1 - sft, 1), :]
            dw_ref[pl.ds(CONV_WIDTH - 1 - sft, 1), :] = jnp.sum(da * shifted[sft], axis=0, keepdims=True)
        dx_ref[...] = dx.astype(BF16)

    return pl.pallas_call(
        body, name=name, grid=(width // LANES,),
        in_specs=[pl.BlockSpec((S, LANES), lambda c: (0, col0 + c)), pl.BlockSpec((CONV_WIDTH, LANES), lambda c: (0, c)),
                  pl.BlockSpec((S, LANES), lambda c: (0, c))],
        out_specs=[pl.BlockSpec((S, LANES), lambda c: (0, c)), pl.BlockSpec((CONV_WIDTH, LANES), lambda c: (0, c))],
        out_shape=[jax.ShapeDtypeStruct((S, width), BF16), jax.ShapeDtypeStruct((CONV_WIDTH, width), F32)],
        compiler_params=_cparams(("parallel",)),
    )(proj, w, dy)


PREP_CHUNKS = 8


def _chunks_prep(qraws, kraws, vs, abs_, alog_row, dtb_row, mask_g, mask_b, t_saved=None):
    n = len(qraws)
    c = qraws[0].shape[0]
    mm_nt, mm_nn = (_mm_nt, _mm_nn) if t_saved is not None else (lambda p, r: _bdot(p, r, _NT), lambda p, r: _bdot(p, r, _NN))
    row = lax.broadcasted_iota(jnp.int32, (c, c), 0)
    col = lax.broadcasted_iota(jnp.int32, (c, c), 1)
    tril, strict, eye = row >= col, row > col, row == col
    eyef = eye.astype(F32)
    neg_rate = -jnp.exp(alog_row)
    q, k, beta, gc_col, gamma, kb, g_last = [], [], [], [], [], [], []
    for i in range(n):
        q.append(_l2(qraws[i]) * (HEAD_DIM ** -0.5))
        k.append(_l2(kraws[i]))
        gfull = neg_rate * _softplus(abs_[i] + dtb_row)
        g = jnp.sum(jnp.where(mask_g, gfull, 0.0), axis=-1, keepdims=True)
        beta.append(jnp.sum(jnp.where(mask_b, _sigmoid(abs_[i]), 0.0), axis=-1, keepdims=True))
        g_row = jnp.sum(jnp.where(eye, g, 0.0), axis=0, keepdims=True)
        gc_col.append(jnp.sum(jnp.where(tril, g_row, 0.0), axis=1, keepdims=True))
        gc_row = jnp.sum(jnp.where(row <= col, g, 0.0), axis=0, keepdims=True)
        gamma.append(jnp.where(tril, jnp.exp(jnp.where(tril, gc_col[i] - gc_row, 0.0)), 0.0))
        kb.append(k[i] * beta[i])
        g_last.append(jnp.sum(g, axis=0, keepdims=True))
    a = [jnp.where(strict, mm_nt(kb[i], k[i]) * gamma[i], 0.0) for i in range(n)]
    if t_saved is None:
        t_inv = [eyef - a[i] for i in range(n)]
        p = a
        for _ in range(int(math.log2(c)) - 1):
            p = [_dot3(p[i], p[i]) for i in range(n)]
            t_inv = [_dot3(t_inv[i], eyef + p[i]) for i in range(n)]
    else:
        t_inv = [_tri_inv_saved(a[i], t_saved[i]) for i in range(n)]
    egc = [jnp.exp(gc_col[i]) for i in range(n)]
    u = [mm_nn(t_inv[i], vs[i] * beta[i]) for i in range(n)]
    w = [mm_nn(t_inv[i], kb[i] * egc[i]) for i in range(n)]
    intra = [mm_nt(q[i], k[i]) * gamma[i] for i in range(n)]
    out = []
    for i in range(n):
        kt = k[i] * jnp.exp(g_last[i] - gc_col[i])
        dec = jnp.broadcast_to(jnp.exp(g_last[i]), (1, HEAD_DIM))
        one = (u[i], w[i], q[i] * egc[i], kt, intra[i], dec)
        out.append(one + (t_inv[i],) if t_saved is None else one)
    return out


def _lane_masks(h):
    lane = lax.broadcasted_iota(jnp.int32, (1, LANES), 1)
    return lane == h, lane == N_DELTA_HEADS + h


def _prep_specs(tr, ab_col):
    H = N_DELTA_HEADS
    return [
        pl.BlockSpec((tr, HEAD_DIM), lambda i, h: (i, h)),
        pl.BlockSpec((tr, HEAD_DIM), lambda i, h: (i, H + h)),
        pl.BlockSpec((tr, HEAD_DIM), lambda i, h: (i, 2 * H + h)),
        pl.BlockSpec((tr, LANES), lambda i, h: (i, ab_col)),
        pl.BlockSpec((1, LANES), lambda i, h: (0, 0)),
        pl.BlockSpec((1, LANES), lambda i, h: (0, 0)),
    ]


def _prep_out_specs(tr):
    nc = tr // CHUNK
    hs = pl.BlockSpec((tr, HEAD_DIM), lambda i, h: (i, h))
    return [hs, hs, hs, hs,
            pl.BlockSpec((None, tr, CHUNK), lambda i, h: (h, i, 0)),
            pl.BlockSpec((None, nc, 1, HEAD_DIM), lambda i, h: (h, i, 0, 0)),
            pl.BlockSpec((None, tr, CHUNK), lambda i, h: (h, i, 0))]


def _prep_out_shapes(S):
    H = N_DELTA_HEADS
    hs = jax.ShapeDtypeStruct((S, H * HEAD_DIM), F32)
    sq = jax.ShapeDtypeStruct((H, S, CHUNK), F32)
    return [hs, hs, hs, hs, sq, jax.ShapeDtypeStruct((H, S // CHUNK, 1, HEAD_DIM), F32), sq]


def _delta_prep_fwd(dqkv, proj, ab_col, alog_row, dtb_row, name):
    S = dqkv.shape[0]
    tr = min(S, PREP_CHUNKS * CHUNK)
    nc = tr // CHUNK

    def body(q_ref, k_ref, v_ref, ab_ref, al_ref, dt_ref, u_ref, w_ref, qd_ref, kt_ref, in_ref, dec_ref, ti_ref):
        mask_g, mask_b = _lane_masks(pl.program_id(1))
        rows = [pl.ds(ci * CHUNK, CHUNK) for ci in range(nc)]
        outs = _chunks_prep([q_ref[rs, :] for rs in rows], [k_ref[rs, :] for rs in rows], [v_ref[rs, :] for rs in rows],
                            [ab_ref[rs, :] for rs in rows], al_ref[...], dt_ref[...], mask_g, mask_b)
        for ci, rs in enumerate(rows):
            u, w, qd, kt, intra, dec, t_inv = outs[ci]
            u_ref[rs, :] = u
            w_ref[rs, :] = w
            qd_ref[rs, :] = qd
            kt_ref[rs, :] = kt
            in_ref[rs, :] = intra
            dec_ref[ci] = dec
            ti_ref[rs, :] = t_inv

    return pl.pallas_call(
        body, name=name, grid=(S // tr, N_DELTA_HEADS),
        in_specs=_prep_specs(tr, ab_col), out_specs=_prep_out_specs(tr), out_shape=_prep_out_shapes(S),
        compiler_params=_cparams(("parallel", "parallel")),
    )(dqkv, dqkv, dqkv, proj, alog_row, dtb_row)


def _delta_prep_bwd(dqkv, proj, ab_col, alog_row, dtb_row, cots, t_inv, name):
    S = dqkv.shape[0]
    H = N_DELTA_HEADS
    tr = min(S, PREP_CHUNKS * CHUNK)
    nc = tr // CHUNK

    def body(q_ref, k_ref, v_ref, ab_ref, al_ref, dt_ref, du_ref, dw_ref, dqd_ref, dkt_ref, din_ref, ddec_ref, ti_ref,
             dq_ref, dk_ref, dv_ref, dab_ref, dal_ref, ddt_ref, dab_sc):
        h = pl.program_id(1)
        mask_g, mask_b = _lane_masks(h)

        @pl.when((pl.program_id(0) == 0) & (h == 0))
        def _():
            dal_ref[...] = jnp.zeros_like(dal_ref)
            ddt_ref[...] = jnp.zeros_like(ddt_ref)

        @pl.when(h == 0)
        def _():
            dab_sc[...] = jnp.zeros_like(dab_sc)

        rows = [pl.ds(ci * CHUNK, CHUNK) for ci in range(nc)]
        fn = functools.partial(_chunks_prep, mask_g=mask_g, mask_b=mask_b, t_saved=[ti_ref[rs, :] for rs in rows])
        _, vjp = jax.vjp(fn, [q_ref[rs, :] for rs in rows], [k_ref[rs, :] for rs in rows], [v_ref[rs, :] for rs in rows],
                         [ab_ref[rs, :] for rs in rows], al_ref[...], dt_ref[...])
        dqs, dks, dvs, dabs, dal, ddt = vjp([(du_ref[rs, :], dw_ref[rs, :], dqd_ref[rs, :], dkt_ref[rs, :], din_ref[rs, :],
                                              ddec_ref[ci]) for ci, rs in enumerate(rows)])
        for ci, rs in enumerate(rows):
            dq_ref[rs, :] = dqs[ci]
            dk_ref[rs, :] = dks[ci]
            dv_ref[rs, :] = dvs[ci]
            dab_sc[rs, :] += dabs[ci]
        dal_ref[...] += dal
        ddt_ref[...] += ddt

        @pl.when(h == H - 1)
        def _():
            dab_ref[...] = dab_sc[...].astype(BF16)

    hs = pl.BlockSpec((tr, HEAD_DIM), lambda i, h: (i, h))
    hshape = jax.ShapeDtypeStruct((S, H * HEAD_DIM), F32)
    row = pl.BlockSpec((1, LANES), lambda i, h: (0, 0))
    rshape = jax.ShapeDtypeStruct((1, LANES), F32)
    return pl.pallas_call(
        body, name=name, grid=(S // tr, H),
        in_specs=_prep_specs(tr, ab_col) + _prep_out_specs(tr),
        out_specs=[hs, hs, hs, pl.BlockSpec((tr, LANES), lambda i, h: (i, 0)), row, row],
        out_shape=[hshape, hshape, hshape, jax.ShapeDtypeStruct((S, LANES), BF16), rshape, rshape],
        scratch_shapes=[pltpu.VMEM((tr, LANES), F32)],
        compiler_params=_cparams(("arbitrary", "arbitrary")),
    )(dqkv, dqkv, dqkv, proj, alog_row, dtb_row, *cots, t_inv)


def _scan_steps(states, us, ws, qds, kts, intras, decs, diff=False):
    nn, tn = (_mm_nn, _mm_tn) if diff else (lambda p, r: _bdot(p, r, _NN), lambda p, r: _bdot(p, r, _TN))
    hs = range(len(states))
    v_new = [us[h] - nn(ws[h], states[h]) for h in hs]
    o_state = [nn(qds[h], states[h]) for h in hs]
    o_intra = [nn(intras[h], v_new[h]) for h in hs]
    grown = [tn(kts[h], v_new[h]) for h in hs]
    return [o_state[h] + o_intra[h] for h in hs], [states[h] * decs[h] + grown[h] for h in hs]


def _scan_specs(rev, n):
    H = N_DELTA_HEADS

    def cix(i):
        return (n - 1 - i) if rev else i

    row = pl.BlockSpec((CHUNK, H * HEAD_DIM), lambda i: (cix(i), 0))
    return row, pl.BlockSpec((H, CHUNK, CHUNK), lambda i: (0, cix(i), 0)), \
        pl.BlockSpec((H, 1, 1, HEAD_DIM), lambda i: (0, cix(i), 0, 0)), \
        pl.BlockSpec((1, H, HEAD_DIM, HEAD_DIM), lambda i: (cix(i), 0, 0, 0))


def _delta_scan_fwd(u, w, qd, kt, intra, dec, name):
    S = u.shape[0]
    H = N_DELTA_HEADS
    n = S // CHUNK
    row, ispec, dspec, sspec = _scan_specs(False, n)

    def body(u_ref, w_ref, qd_ref, kt_ref, in_ref, dec_ref, o_ref, st_ref, s_sc):
        @pl.when(pl.program_id(0) == 0)
        def _():
            s_sc[...] = jnp.zeros_like(s_sc)

        cols = [pl.ds(h * HEAD_DIM, HEAD_DIM) for h in range(H)]
        states = [s_sc[h] for h in range(H)]
        outs, new = _scan_steps(states, [u_ref[:, cs] for cs in cols], [w_ref[:, cs] for cs in cols],
                                [qd_ref[:, cs] for cs in cols], [kt_ref[:, cs] for cs in cols],
                                [in_ref[h] for h in range(H)], [dec_ref[h, 0] for h in range(H)])
        for h, cs in enumerate(cols):
            st_ref[0, h] = states[h]
            o_ref[:, cs] = outs[h]
            s_sc[h] = new[h]

    return pl.pallas_call(
        body, name=name, grid=(n,),
        in_specs=[row, row, row, row, ispec, dspec], out_specs=[row, sspec],
        out_shape=[jax.ShapeDtypeStruct((S, H * HEAD_DIM), F32), jax.ShapeDtypeStruct((n, H, HEAD_DIM, HEAD_DIM), F32)],
        scratch_shapes=[pltpu.VMEM((H, HEAD_DIM, HEAD_DIM), F32)],
        compiler_params=_cparams(("arbitrary",)),
    )(u, w, qd, kt, intra, dec)


def _delta_scan_bwd(u, w, qd, kt, intra, dec, states, do, name):
    S = u.shape[0]
    H = N_DELTA_HEADS
    n = S // CHUNK
    row, ispec, dspec, sspec = _scan_specs(True, n)

    def body(u_ref, w_ref, qd_ref, kt_ref, in_ref, dec_ref, st_ref, do_ref,
             du_ref, dw_ref, dqd_ref, dkt_ref, din_ref, ddec_ref, ds_sc):
        @pl.when(pl.program_id(0) == 0)
        def _():
            ds_sc[...] = jnp.zeros_like(ds_sc)

        cols = [pl.ds(h * HEAD_DIM, HEAD_DIM) for h in range(H)]
        _, vjp = jax.vjp(functools.partial(_scan_steps, diff=True), [st_ref[0, h] for h in range(H)], [u_ref[:, cs] for cs in cols],
                         [w_ref[:, cs] for cs in cols], [qd_ref[:, cs] for cs in cols], [kt_ref[:, cs] for cs in cols],
                         [in_ref[h] for h in range(H)], [dec_ref[h, 0] for h in range(H)])
        dstate, du, dw, dqd, dkt, din, ddec = vjp(([do_ref[:, cs] for cs in cols], [ds_sc[h] for h in range(H)]))
        for h, cs in enumerate(cols):
            du_ref[:, cs] = du[h]
            dw_ref[:, cs] = dw[h]
            dqd_ref[:, cs] = dqd[h]
            dkt_ref[:, cs] = dkt[h]
            din_ref[h] = din[h]
            ddec_ref[h, 0] = ddec[h]
            ds_sc[h] = dstate[h]

    hshape = jax.ShapeDtypeStruct((S, H * HEAD_DIM), F32)
    return pl.pallas_call(
        body, name=name, grid=(n,),
        in_specs=[row, row, row, row, ispec, dspec, sspec, row],
        out_specs=[row, row, row, row, ispec, dspec],
        out_shape=[hshape, hshape, hshape, hshape, jax.ShapeDtypeStruct((H, S, CHUNK), F32),
                   jax.ShapeDtypeStruct((H, n, 1, HEAD_DIM), F32)],
        scratch_shapes=[pltpu.VMEM((H, HEAD_DIM, HEAD_DIM), F32)],
        compiler_params=_cparams(("arbitrary",)),
    )(u, w, qd, kt, intra, dec, states, do)


def _gated_norm(od, z, gain):
    return _rms(od, gain) * _silu(z)


def _post_fwd(od, proj, z_col0, gain, name):
    S = od.shape[0]
    H = N_DELTA_HEADS
    tr = _pick(S, 512)

    def body(od_ref, z_ref, g_ref, o_ref):
        o_ref[...] = _gated_norm(od_ref[...], z_ref[...], g_ref[...]).astype(BF16)

    return pl.pallas_call(
        body, name=name, grid=(S // tr, H),
        in_specs=[_head_spec(tr, 0), _head_spec(tr, z_col0), _gain_spec()],
        out_specs=_head_spec(tr, 0), out_shape=jax.ShapeDtypeStruct((S, H * HEAD_DIM), BF16),
        compiler_params=_cparams(("parallel", "parallel")),
    )(od, proj, gain)


def _post_bwd(od, proj, z_col0, gain, do, do_col0, name):
    S = od.shape[0]
    H = N_DELTA_HEADS
    tr = _pick(S, 512)

    def body(od_ref, z_ref, g_ref, do_ref, dod_ref, dz_ref, dg_ref):
        @pl.when((pl.program_id(0) == 0) & (pl.program_id(1) == 0))
        def _():
            dg_ref[...] = jnp.zeros_like(dg_ref)

        _, vjp = jax.vjp(_gated_norm, od_ref[...], z_ref[...], g_ref[...])
        dod, dz, dg = vjp(do_ref[...])
        dod_ref[...] = dod
        dz_ref[...] = dz.astype(BF16)
        dg_ref[...] += dg

    return pl.pallas_call(
        body, name=name, grid=(S // tr, H),
        in_specs=[_head_spec(tr, 0), _head_spec(tr, z_col0), _gain_spec(), _head_spec(tr, do_col0)],
        out_specs=[_head_spec(tr, 0), _head_spec(tr, 0), _gain_spec()],
        out_shape=[jax.ShapeDtypeStruct((S, H * HEAD_DIM), F32), jax.ShapeDtypeStruct((S, H * HEAD_DIM), BF16),
                   jax.ShapeDtypeStruct((1, HEAD_DIM), F32)],
        compiler_params=_cparams(("arbitrary", "arbitrary")),
    )(od, proj, gain, do)


def _adam_math(w, g, m, v):
    m = ADAM_B1 * m + (1.0 - ADAM_B1) * g
    v = ADAM_B2 * v + (1.0 - ADAM_B2) * (g * g)
    m_hat = m / (1.0 - ADAM_B1 ** ADAM_STEP)
    v_hat = v / (1.0 - ADAM_B2 ** ADAM_STEP)
    delta = -ADAM_LR * (m_hat / (jnp.sqrt(v_hat) + ADAM_EPS) + ADAM_WD * w)
    return delta, m, v


def _adamw(w, g, m, v, name):
    R, C = w.shape
    tr = R if R * C * 4 <= (1 << 20) else _pick8(R, max(8, (1 << 20) // (C * 4)))

    def body(w_ref, g_ref, m_ref, v_ref, d_ref, nm_ref, nv_ref):
        d, nm, nv = _adam_math(w_ref[...], g_ref[...], m_ref[...], v_ref[...])
        d_ref[...] = d
        nm_ref[...] = nm
        nv_ref[...] = nv

    spec = pl.BlockSpec((tr, C), lambda i: (i, 0))
    shp = jax.ShapeDtypeStruct((R, C), F32)
    return pl.pallas_call(
        body, name=name, grid=(R // tr,), in_specs=[spec] * 4, out_specs=[spec] * 3, out_shape=[shp] * 3,
        compiler_params=_cparams(("parallel",)),
    )(w, g, m, v)


def _adamw_halves(w, mine, theirs, col, m, v, name):
    _, R, C = w.shape
    tr = _pick8(R // 2, max(8, (1 << 20) // (C * 4)))
    nb2 = (R // 2) // tr

    def body(w_ref, a_ref, b_ref, m_ref, v_ref, g_ref, d_ref, nm_ref, nv_ref):
        top = pl.program_id(0) < nb2
        g = jnp.where(top == (lax.axis_index("c") == 0), a_ref[...], b_ref[...])
        d, nm, nv = _adam_math(w_ref[...], g, m_ref[...], v_ref[...])
        g_ref[...] = g
        d_ref[...] = d
        nm_ref[...] = nm
        nv_ref[...] = nv

    spec = pl.BlockSpec((None, tr, C), lambda i: (0, i, 0))
    half = pl.BlockSpec((tr, C), lambda i: (i % nb2, col))
    shp = jax.ShapeDtypeStruct((1, R, C), F32)
    return pl.pallas_call(
        body, name=name, grid=(2 * nb2,), in_specs=[spec, half, half, spec, spec], out_specs=[spec] * 4, out_shape=[shp] * 4,
        compiler_params=_cparams(("parallel",)),
    )(w, mine, theirs, m, v)


def _adamw_halves_t(w, mine, theirs, m, v, name):
    _, R, C = w.shape
    tc = _pick(R // 2, 256)
    nb2 = (R // 2) // tc

    def body(w_ref, a_ref, b_ref, m_ref, v_ref, g_ref, d_ref, nm_ref, nv_ref):
        left = pl.program_id(0) < nb2
        g = jnp.where(left == (lax.axis_index("c") == 0), a_ref[...], b_ref[...])
        d, nm, nv = _adam_math(w_ref[...], g, m_ref[...], v_ref[...])
        g_ref[...] = g
        d_ref[...] = d
        nm_ref[...] = nm
        nv_ref[...] = nv

    spec = pl.BlockSpec((C, tc), lambda j: (0, j))
    half = pl.BlockSpec((C, tc), lambda j: (0, j % nb2))
    shp = jax.ShapeDtypeStruct((C, R), F32)
    outs = pl.pallas_call(
        body, name=name, grid=(2 * nb2,), in_specs=[spec, half, half, spec, spec], out_specs=[spec] * 4, out_shape=[shp] * 4,
        compiler_params=_cparams(("parallel",)),
    )(jnp.transpose(w[0]), jnp.transpose(mine), jnp.transpose(theirs), jnp.transpose(m[0]), jnp.transpose(v[0]))
    return [jnp.transpose(o)[None] for o in outs]


def _pick8(dim, pref):
    t = (min(dim, pref) // 8) * 8
    while t >= 8:
        if dim % t == 0:
            return t
        t -= 8
    return dim


def _adamw_outer(w, m, v, cond_t, rhs, name):
    R, C = w.shape
    tr = _pick8(R, 128)
    nb = cond_t.shape[1]
    lhs_t = cond_t

    def body(w_ref, m_ref, v_ref, a_ref, b_ref, g_ref, d_ref, nm_ref, nv_ref):
        g = _dot(_silu(a_ref[...]), b_ref[...])
        d, nm, nv = _adam_math(w_ref[...], g, m_ref[...], v_ref[...])
        g_ref[...] = g
        d_ref[...] = d
        nm_ref[...] = nm
        nv_ref[...] = nv

    spec = pl.BlockSpec((tr, C), lambda i: (i, 0))
    shp = jax.ShapeDtypeStruct((R, C), F32)
    return pl.pallas_call(
        body, name=name, grid=(R // tr,),
        in_specs=[spec, spec, spec, pl.BlockSpec((tr, nb), lambda i: (i, 0)), pl.BlockSpec((nb, C), lambda i: (0, 0))],
        out_specs=[spec] * 4, out_shape=[shp] * 4,
        compiler_params=_cparams(("parallel",)),
    )(w, m, v, lhs_t, rhs)


def _ada_fwd(cond, w, bias, name):
    a = cond
    nb, K = a.shape
    N = w.shape[1]
    tn = _pick(N, 512)

    def body(a_ref, w_ref, b_ref, o_ref):
        o_ref[...] = _dot(_silu(a_ref[...]), w_ref[...]) + b_ref[...]

    return pl.pallas_call(
        body, name=name, grid=(N // tn,),
        in_specs=[pl.BlockSpec((nb, K), lambda j: (0, 0)), pl.BlockSpec((K, tn), lambda j: (0, j)), pl.BlockSpec((1, tn), lambda j: (0, j))],
        out_specs=pl.BlockSpec((nb, tn), lambda j: (0, j)), out_shape=jax.ShapeDtypeStruct((nb, N), F32),
        compiler_params=_cparams(("parallel",)),
    )(a, w, bias)


def _add_cast(parts, out_dtypes, name):
    shape = parts[0].shape
    G, R, C = shape
    tr = _pick8(R, max(8, (1 << 20) // (C * 4)))
    n_in = len(parts)

    def body(*refs):
        acc = refs[0][...].astype(F32)
        for r in refs[1:n_in]:
            acc = acc + r[...].astype(F32)
        for o, dt in zip(refs[n_in:], out_dtypes):
            o[...] = acc.astype(dt)

    spec = pl.BlockSpec((1, tr, C), lambda g, i: (g, i, 0))
    outs = pl.pallas_call(
        body, name=name, grid=(G, R // tr), in_specs=[spec] * n_in, out_specs=[spec] * len(out_dtypes),
        out_shape=[jax.ShapeDtypeStruct(shape, dt) for dt in out_dtypes],
        compiler_params=_cparams(("parallel", "parallel")),
    )(*parts)
    return outs


def _me():
    return lax.axis_index("x"), lax.axis_index("y"), lax.axis_index("c")


def _xor_peer(k):
    x, y, c = _me()
    dx, dy, dc = (k >> 2) & 1, (k >> 1) & 1, k & 1
    return (x ^ dx if dx else x, y ^ dy if dy else y, c ^ dc if dc else c)


ANY = pl.BlockSpec(memory_space=pl.ANY)


def _all_gather_small(v, name, after=None):
    R, C = v.shape
    extra = [] if after is None else [after]

    def body(v_ref, *rest):
        out_ref, send_sems, recv_sems = rest[len(extra):]
        x, y, c = _me()
        mine = 4 * x + 2 * y + c
        out_ref[mine] = v_ref[...]
        copies = []
        for k in range(1, 8):
            cp = pltpu.make_async_remote_copy(src_ref=v_ref, dst_ref=out_ref.at[mine], send_sem=send_sems.at[k - 1],
                                              recv_sem=recv_sems.at[k - 1], device_id=_xor_peer(k), device_id_type=MESH)
            cp.start()
            copies.append(cp)
        for k in range(1, 8):
            px, py, pc = _xor_peer(k)
            pltpu.make_async_remote_copy(src_ref=v_ref, dst_ref=out_ref.at[4 * px + 2 * py + pc], send_sem=send_sems.at[k - 1],
                                         recv_sem=recv_sems.at[k - 1], device_id=_xor_peer(k), device_id_type=MESH).wait_recv()
        for cp in copies:
            cp.wait_send()

    return pl.pallas_call(
        body, name=name, out_shape=jax.ShapeDtypeStruct((8, R, C), F32),
        in_specs=[pl.BlockSpec(memory_space=pltpu.VMEM)] + [ANY] * len(extra), out_specs=pl.BlockSpec(memory_space=pltpu.VMEM),
        scratch_shapes=[pltpu.SemaphoreType.DMA((7,)), pltpu.SemaphoreType.DMA((7,))],
        compiler_params=pltpu.CompilerParams(vmem_limit_bytes=VMEM_LIMIT),
    )(v, *extra)


def _chip_peers():
    x, y, _ = _me()
    return [(1, (x, 1 - y)), (2, (1 - x, y)), (3, (1 - x, 1 - y))]


def _all_gather_shards(shards, name):
    n = len(shards)

    def body(*refs):
        ins, outs = refs[:n], refs[n:2 * n]
        send_sems, recv_sems = refs[2 * n:]
        x, y, c = _me()
        chip = 2 * x + y
        sib = (x, y, 1 - c)
        peers = _chip_peers()
        sends = []
        for t in range(n):
            half = ins[t].shape[0] // 2
            mine = pl.ds(c * half, half)
            for p, (k, (px, py)) in enumerate(peers):
                cp = pltpu.make_async_remote_copy(src_ref=ins[t].at[mine], dst_ref=outs[t].at[chip, mine],
                                                  send_sem=send_sems.at[6 * t + p], recv_sem=recv_sems.at[6 * t + p],
                                                  device_id=(px, py, c), device_id_type=MESH)
                cp.start()
                sends.append(cp)
        for t in range(n):
            half = ins[t].shape[0] // 2
            mine = pl.ds(c * half, half)
            for p, (k, (px, py)) in enumerate(peers):
                src_chip = 2 * px + py
                landed = outs[t].at[src_chip, mine]
                pltpu.make_async_remote_copy(src_ref=landed, dst_ref=landed, send_sem=send_sems.at[6 * t + p],
                                             recv_sem=recv_sems.at[6 * t + p], device_id=(px, py, c), device_id_type=MESH).wait_recv()
                fw = pltpu.make_async_remote_copy(src_ref=landed, dst_ref=landed, send_sem=send_sems.at[6 * t + 3 + p],
                                                  recv_sem=recv_sems.at[6 * t + 3 + p], device_id=sib, device_id_type=MESH)
                fw.start()
                sends.append(fw)
        for t in range(n):
            half = ins[t].shape[0] // 2
            theirs = pl.ds((1 - c) * half, half)
            for p, (k, (px, py)) in enumerate(peers):
                got = outs[t].at[2 * px + py, theirs]
                pltpu.make_async_remote_copy(src_ref=got, dst_ref=got, send_sem=send_sems.at[6 * t + 3 + p],
                                             recv_sem=recv_sems.at[6 * t + 3 + p], device_id=sib, device_id_type=MESH).wait_recv()
        for cp in sends:
            cp.wait_send()

    return pl.pallas_call(
        body, name=name,
        out_shape=[jax.ShapeDtypeStruct((4,) + s.shape, s.dtype) for s in shards],
        in_specs=[ANY] * n, out_specs=[ANY] * n,
        scratch_shapes=[pltpu.SemaphoreType.DMA((6 * n,)), pltpu.SemaphoreType.DMA((6 * n,))],
    )(*shards)


def _swap_halves_with_sibling(slabs, name):
    n = len(slabs)

    def body(*refs):
        ins, outs = refs[:n], refs[n:2 * n]
        send_sems, recv_sems = refs[2 * n:]
        x, y, c = _me()
        sib = (x, y, 1 - c)
        cps = []
        for t in range(n):
            half = ins[t].shape[1] // 2
            cp = pltpu.make_async_remote_copy(src_ref=ins[t].at[:, pl.ds((1 - c) * half, half)], dst_ref=outs[t],
                                              send_sem=send_sems.at[t], recv_sem=recv_sems.at[t], device_id=sib, device_id_type=MESH)
            cp.start()
            cps.append(cp)
        for cp in cps:
            cp.wait()

    return pl.pallas_call(
        body, name=name,
        out_shape=[jax.ShapeDtypeStruct((4, s.shape[1] // 2, s.shape[2]), s.dtype) for s in slabs],
        in_specs=[ANY] * n, out_specs=[ANY] * n,
        scratch_shapes=[pltpu.SemaphoreType.DMA((n,)), pltpu.SemaphoreType.DMA((n,))],
    )(*slabs)


def _send_halves_to_sibling(halves, name):
    n = len(halves)

    def body(*refs):
        ins, outs = refs[:n], refs[n:2 * n]
        send_sems, recv_sems = refs[2 * n:]
        x, y, c = _me()
        sib = (x, y, 1 - c)
        cps = []
        for t in range(n):
            cp = pltpu.make_async_remote_copy(src_ref=ins[t], dst_ref=outs[t], send_sem=send_sems.at[t],
                                              recv_sem=recv_sems.at[t], device_id=sib, device_id_type=MESH)
            cp.start()
            cps.append(cp)
        for cp in cps:
            cp.wait()

    return pl.pallas_call(
        body, name=name,
        out_shape=[jax.ShapeDtypeStruct(s.shape, s.dtype) for s in halves],
        in_specs=[ANY] * n, out_specs=[ANY] * n,
        scratch_shapes=[pltpu.SemaphoreType.DMA((n,)), pltpu.SemaphoreType.DMA((n,))],
    )(*halves)


HBM_SPEC = pl.BlockSpec(memory_space=pltpu.HBM)
SEM_SPEC = pl.BlockSpec(memory_space=pltpu.SEMAPHORE)
DATAFLOW = pltpu.SideEffectType.DATAFLOW_SIDE_EFFECTING


def _plan_gather_direct(src_refs, land_refs):
    x, y, c = _me()
    chip = 2 * x + y
    plan = []
    for s, land in zip(src_refs, land_refs):
        half = s.shape[0] // 2
        for _, (px, py) in _chip_peers():
            for pc in (c, 1 - c):
                plan.append((s.at[pl.ds(c * half, half)], land.at[chip, pl.ds(c * half, half)],
                             land.at[2 * px + py, pl.ds(pc * half, half)], (px, py, pc)))
    return plan


def _plan_scatter_direct(src_refs, land_refs):
    x, y, c = _me()
    plan = []
    for s, land in zip(src_refs, land_refs):
        half = s.shape[1] // 2
        for k in range(1, 8):
            px, py, pc = _xor_peer(k)
            plan.append((s.at[2 * px + py, pl.ds(pc * half, half)], land.at[4 * x + 2 * y + c],
                         land.at[4 * px + 2 * py + pc], (px, py, pc)))
    return plan


def _plan_exchange_chips(src_refs, land_refs):
    x, y, c = _me()
    chip = 2 * x + y
    plan = []
    for s, land in zip(src_refs, land_refs):
        for _, (px, py) in _chip_peers():
            plan.append((s.at[2 * px + py], land.at[chip], land.at[2 * px + py], (px, py, c)))
    return plan


_plan_gather_direct.per_tensor = 6
_plan_scatter_direct.per_tensor = 7
_plan_exchange_chips.per_tensor = 3


def _start_copies(srcs, lands, plan_fn, name, after=None):
    n = len(srcs)
    n_copies = len(srcs) * plan_fn.per_tensor
    extra = [] if after is None else [after]

    def body(*refs):
        refs = refs[:2 * n] + refs[2 * n + len(extra):]
        send_sems, recv_sems, token = refs[2 * n], refs[2 * n + 1], refs[-1]
        for i, (src, dst, _, peer) in enumerate(plan_fn(refs[:n], refs[n:2 * n])):
            pltpu.make_async_remote_copy(src_ref=src, dst_ref=dst, send_sem=send_sems.at[i], recv_sem=recv_sems.at[i],
                                         device_id=peer, device_id_type=MESH).start()
        token[...] = jnp.zeros_like(token)

    arrays = list(srcs) + list(lands)
    outs = pl.pallas_call(
        body, name=name,
        out_shape=(pltpu.SemaphoreType.DMA((n_copies,)), pltpu.SemaphoreType.DMA((n_copies,)),
                   *[pltpu.HBM(a.shape, a.dtype) for a in arrays], jax.ShapeDtypeStruct((8, LANES), F32)),
        in_specs=[HBM_SPEC] * (2 * n) + [ANY] * len(extra),
        out_specs=(SEM_SPEC, SEM_SPEC, *[HBM_SPEC] * (2 * n), pl.BlockSpec(memory_space=pltpu.VMEM)),
        input_output_aliases={i: 2 + i for i in range(2 * n)},
        compiler_params=pltpu.CompilerParams(has_side_effects=DATAFLOW),
    )(*[pltpu.with_memory_space_constraint(a, pltpu.HBM) for a in arrays], *extra)
    return outs[0], outs[1], list(outs[2:2 + n]), list(outs[2 + n:2 + 2 * n]), outs[-1]


def _wait_copies(send_sems, recv_sems, srcs, lands, after, plan_fn, name):
    n = len(srcs)

    def body(*refs):
        send_sems, recv_sems = refs[2 * n], refs[2 * n + 1]
        for i, (src, _, arrival, peer) in enumerate(plan_fn(refs[:n], refs[n:2 * n])):
            cp = pltpu.make_async_remote_copy(src_ref=src, dst_ref=arrival, send_sem=send_sems.at[i], recv_sem=recv_sems.at[i],
                                              device_id=peer, device_id_type=MESH)
            cp.wait_send()
            cp.wait_recv()

    arrays = list(srcs) + list(lands)
    outs = pl.pallas_call(
        body, name=name,
        out_shape=tuple(pltpu.HBM(a.shape, a.dtype) for a in arrays),
        in_specs=[HBM_SPEC] * (2 * n) + [SEM_SPEC, SEM_SPEC, ANY],
        out_specs=tuple([HBM_SPEC] * (2 * n)),
        input_output_aliases={i: i for i in range(2 * n)},
        compiler_params=pltpu.CompilerParams(has_side_effects=DATAFLOW),
    )(*arrays, send_sems, recv_sems, after)
    return list(outs[n:])


def _rope_tables(positions):
    half = ROPE_DIM // 2
    S = positions.shape[0]
    inv_freq = ROPE_THETA ** (-jnp.arange(half, dtype=F32) / half)
    ang = positions.astype(F32)[:, None] * inv_freq
    cos, sin = jnp.cos(ang), jnp.sin(ang)
    zeros = functools.partial(jnp.zeros, dtype=F32)
    cc = jnp.concatenate([cos, cos, jnp.ones((S, HEAD_DIM - ROPE_DIM), F32)], axis=1)
    s1 = jnp.concatenate([-sin, zeros((S, HEAD_DIM - half))], axis=1)
    s2 = jnp.concatenate([zeros((S, half)), sin, zeros((S, HEAD_DIM - ROPE_DIM))], axis=1)
    return cc, s1, s2


def _ffn_fwd(x, gain, shift, scale, gate, w_gu, w_d, fs, tag):
    S = x.shape[0]
    tm = _pick(S, 512)
    h = _pre_fwd(x, gain, shift, scale, tag + "_pre")

    def swiglu_out(tile, ins, outs):
        outs[0][...] = tile
        outs[1][...] = _swiglu_fn(tile[:, :fs], tile[:, fs:]).astype(BF16)

    ab, s = _matmul(h, w_gu, "nn", F32, tag + "_gate_up", tm=tm, tn=2 * fs, tk=4096, epilogue=(swiglu_out, [], [
        ((S, 8 * fs), F32, (tm, 2 * fs), lambda i, j: (i, j)), ((S, 4 * fs), BF16, (tm, fs), lambda i, j: (i, j))]))
    f = _matmul(s, w_d, "nn", F32, tag + "_down", tm=512, tn=1024, tk=8192)
    xn = _residual_fwd(x, gate, f, 0.5, tag + "_res")
    return xn, (x, h, ab, s, f)


def _ffn_bwd(dxn, saved, gain, shift, scale, gate, w_gu, w_d, fs, tag, on_dw_d=None, on_dw_gu=None, dw_gu_dtype=F32):
    x, h, ab, s, f = saved
    df, dgate = _residual_bwd(gate, f, dxn, 0.5, tag + "_res_bwd")
    S = x.shape[0]
    tm = _pick(S, 512)
    dw_d = _matmul(s, df, "tn", BF16, tag + "_down_dw", tm=1408, tn=1024, tk=2048)
    if on_dw_d is not None:
        shift = shift + on_dw_d(dw_d)

    def swiglu_back(tile, ins, outs):
        _, vjp = jax.vjp(_swiglu_fn, ins[0][:, :fs], ins[0][:, fs:])
        da, db = vjp(tile)
        outs[0][:, :fs] = da.astype(BF16)
        outs[0][:, fs:] = db.astype(BF16)

    (dab,) = _matmul(df, w_d, "nt", F32, tag + "_down_dx", tm=tm, tn=fs, tk=4096, epilogue=(
        swiglu_back, [(ab, (tm, 2 * fs), lambda i, j: (i, j))], [((S, 8 * fs), BF16, (tm, 2 * fs), lambda i, j: (i, j))]))
    dw_gu = _matmul(h, dab, "tn", dw_gu_dtype, tag + "_gate_up_dw", tm=512, tn=2 * fs, tk=2048, col_slabs=True)
    tie = on_dw_gu(dw_gu) if on_dw_gu is not None else None
    dh = _matmul(dab, w_gu, "nt", F32, tag + "_gate_up_dx", tm=1024, tn=1024, tk=2816, after=tie)
    dx, dgain, dshift, dscale = _pre_bwd(x, gain, shift, scale, dh, dxn, tag + "_pre_bwd")
    return dx, dw_gu, dw_d, dgain, dshift, dscale, dgate


def _flat_pad(parts, rows, cols):
    flat = jnp.concatenate([p.reshape(-1).astype(F32) for p in parts])
    return jnp.pad(flat, (0, rows * cols - flat.shape[0])).reshape(rows, cols)


def _cols_to_slabs(w, n):
    R, NC = w.shape
    return jnp.transpose(w.reshape(R, n, NC // n), (1, 0, 2))


def kernel(x, c, positions, w_ada, b_ada, ffn1_norm, ffn1_w_gate, ffn1_w_up, ffn1_w_down, mix_norm, w_in, conv_w, q_norm, k_norm, a_log, dt_bias, delta_out_norm, w_out, ffn2_norm, ffn2_w_gate, ffn2_w_up, ffn2_w_down, loss_target, m_w_ada, m_b_ada, m_ffn1_norm, m_ffn1_w_gate, m_ffn1_w_up, m_ffn1_w_down, m_mix_norm, m_w_in, m_conv_w, m_q_norm, m_k_norm, m_a_log, m_dt_bias, m_delta_out_norm, m_w_out, m_ffn2_norm, m_ffn2_w_gate, m_ffn2_w_up, m_ffn2_w_down, v_w_ada, v_b_ada, v_ffn1_norm, v_ffn1_w_gate, v_ffn1_w_up, v_ffn1_w_down, v_mix_norm, v_w_in, v_conv_w, v_q_norm, v_k_norm, v_a_log, v_dt_bias, v_delta_out_norm, v_w_out, v_ffn2_norm, v_ffn2_w_gate, v_ffn2_w_up, v_ffn2_w_down):
    xi, yi, ci = _me()
    chip = 2 * xi + yi
    dev = 2 * chip + ci
    xs = x[0]
    S, D = xs.shape
    HA, HD = N_ATTN_HEADS, N_DELTA_HEADS
    fs = ffn1_w_gate.shape[2]
    n_mod_shard = w_ada.shape[2]
    in_shard = w_in.shape[2]
    in_width = 4 * in_shard
    in_pad = -(-in_width // LANES) * LANES
    conv_shard = conv_w.shape[2]
    conv_width = 4 * conv_shard

    pack0 = jnp.zeros((8, max(D, conv_shard)), F32)
    pack0 = pack0.at[0, :D].set(c[0]).at[1:1 + CONV_WIDTH, :conv_shard].set(conv_w[0])
    got0 = _all_gather_small(pack0, "gather_cond")
    c_all = got0[:, 0, :D]
    conv_full = jnp.transpose(got0[::2, 1:1 + CONV_WIDTH, :conv_shard], (1, 0, 2)).reshape(CONV_WIDTH, conv_width)
    b_ada_mine = lax.dynamic_slice(b_ada, (0, chip * n_mod_shard), (1, n_mod_shard))
    mod_part = _ada_fwd(c_all, w_ada[0], b_ada_mine, "ada_fwd")
    got1 = _all_gather_small(mod_part, "gather_mod")
    mod = lax.dynamic_index_in_dim(got1[::2], dev, axis=1, keepdims=False).reshape(1, 4 * n_mod_shard)
    sh1, sc1, gt1, sh2, sc2, gt2, sh3, sc3, gt3 = [mod[:, i * D:(i + 1) * D] for i in range(N_MOD)]

    shards = [w[0].astype(BF16) for w in (ffn1_w_gate, ffn1_w_up, ffn1_w_down, w_in, w_out, ffn2_w_gate, ffn2_w_up, ffn2_w_down)]
    gathered = _all_gather_shards(shards[:3], "gather_weights")
    g1g, g1u, g1d = [lax.dynamic_update_index_in_dim(g, s, chip, 0) for g, s in zip(gathered, shards[:3])]
    zones = [lax.dynamic_update_index_in_dim(lax.empty((4,) + s.shape, BF16), s, chip, 0) for s in shards[3:]]
    ag_in = _start_copies(shards[3:4], zones[:1], _plan_gather_direct, "gather_in_start")
    sh1 = sh1 + ag_in[4][0, 0]

    def gate_up(gg, gu):
        return jnp.transpose(jnp.concatenate([gg, gu], axis=2), (1, 0, 2)).reshape(D, 8 * fs)

    w_gu1, w_d1 = gate_up(g1g, g1u), g1d.reshape(4 * fs, D)

    x1, saved1 = _ffn_fwd(xs, ffn1_norm, sh1, sc1, gt1, w_gu1, w_d1, fs, "ffn1")
    (gin,) = _wait_copies(ag_in[0], ag_in[1], ag_in[2], ag_in[3], x1, _plan_gather_direct, "gather_in_wait")
    w_in_f = jnp.pad(jnp.transpose(gin, (1, 0, 2)).reshape(D, in_width), ((0, 0), (0, in_pad - in_width)))
    ag_rest = _start_copies(shards[4:], zones[1:], _plan_gather_direct, "gather_rest_start", after=gin)
    sh2 = sh2 + ag_rest[4][0, 0]

    cc, s1, s2 = _rope_tables(positions[0])
    alog_row = jnp.pad(a_log, ((0, 0), (0, LANES - HD)))
    dtb_row = jnp.pad(dt_bias, ((0, 0), (0, LANES - HD)))
    col_k, col_v, col_d, col_z, col_ab = HA, 2 * HA, 3 * HA, 3 * HA + 3 * HD, 3 * HA + 4 * HD
    h2 = _pre_fwd(x1, mix_norm, sh2, sc2, "mix_pre")
    proj = _matmul(h2, w_in_f, "nn", F32, "mix_in_proj", tm=512, tn=2432, tk=4096)
    attn_bias = _log_multiplicity_table(_pick(S, ATTN_TILE))
    qa, ka = _attn_prep_fwd(proj, q_norm, k_norm, cc, s1, s2, "attn_prep")
    o_near, lse_near = _attn_fwd(qa, ka, proj, col_v, attn_bias, "attn_fwd")
    oa, oa_b, lse = _attn_far_fwd(qa, ka, proj, col_v, o_near, lse_near, "attn_far_fwd")
    dqkv = _conv_fwd(proj, col_d, conv_width, conv_full, "conv_fwd")
    *prep, t_inv = _delta_prep_fwd(dqkv, proj, col_ab, alog_row, dtb_row, "delta_prep")
    od_raw, states = _delta_scan_fwd(*prep, "delta_scan")
    od = _post_fwd(od_raw, proj, col_z, delta_out_norm, "delta_post")
    o = jnp.concatenate([oa_b, od], axis=1)
    gout, g2g, g2u, g2d = _wait_copies(ag_rest[0], ag_rest[1], ag_rest[2], ag_rest[3], o, _plan_gather_direct, "gather_rest_wait")
    w_out_f = gout.reshape(-1, D)
    w_gu2, w_d2 = gate_up(g2g, g2u), g2d.reshape(4 * fs, D)
    mo = _matmul(o, w_out_f, "nn", F32, "mix_out_proj", tm=1024, tn=1024, tk=4096)
    x2 = _residual_fwd(x1, gt2, mo, 1.0, "mix_res")

    x3, saved3 = _ffn_fwd(x2, ffn2_norm, sh3, sc3, gt3, w_gu2, w_d2, fs, "ffn2")
    loss_part, dy = _loss_head(x3, loss_target[0], "loss_head")
    loss = lax.psum(loss_part[0, 0], ("x", "y", "c"))

    dx2, dw_gu2, dw_d2, dgain3, dsh3, dsc3, dgt3 = _ffn_bwd(dy, saved3, ffn2_norm, sh3, sc3, gt3, w_gu2, w_d2, fs, "ffn2",
                                                            dw_gu_dtype=BF16)

    def scatter_start(slabs32, name):
        slabs16 = [s.astype(BF16) for s in slabs32]
        zones = []
        for s in slabs16:
            half = s.shape[1] // 2
            own = lax.dynamic_slice(s, (chip, ci * half, 0), (1, half, s.shape[2]))
            zones.append(lax.dynamic_update_slice(lax.empty((8, half, s.shape[2]), BF16), own, (dev, 0, 0)))
        return _start_copies(slabs16, zones, _plan_scatter_direct, name)

    rs_ffn2 = scatter_start([dw_gu2, dw_d2.reshape(4, fs, D)], "rs_ffn2_start")
    dmo, dgt2 = _residual_bwd(gt2 + rs_ffn2[4][0, 0], mo, dx2, 1.0, "mix_res_bwd")
    do = _matmul(dmo, w_out_f, "nt", F32, "mix_out_dx", tm=1024, tn=1024, tk=4096)
    dw_out = _matmul(o, dmo, "tn", BF16, "mix_out_dw", tm=1024, tn=1024, tk=2048)
    dq = _attn_bwd_dq(qa, ka, proj, col_v, oa, lse, do, 0, attn_bias, "attn_bwd_dq")
    dk, dv = _attn_bwd_dkv(qa, ka, proj, col_v, oa, lse, do, 0, attn_bias, "attn_bwd_dkv")
    far = _attn_far_bwd(qa, ka, proj, col_v, oa, lse, do, 0, "attn_far_bwd")
    dpq, dpk, dv, dq_gain, dk_gain = _attn_prep_bwd(proj, q_norm, k_norm, cc, s1, s2, (dq, dk, dv), far, "attn_prep_bwd")
    dod, dz, ddn_gain = _post_bwd(od_raw, proj, col_z, delta_out_norm, do, HA, "delta_post_bwd")
    cots = _delta_scan_bwd(*prep, states, dod, "delta_scan_bwd")
    ddq, ddk, ddv, dab, dalog, ddtb = _delta_prep_bwd(dqkv, proj, col_ab, alog_row, dtb_row, cots, t_inv, "delta_prep_bwd")
    dconv_in, dconv_w = _conv_bwd(proj, col_d, conv_width, conv_full, jnp.concatenate([ddq, ddk, ddv], axis=1), "conv_bwd")
    dproj = jnp.concatenate([dpq, dpk, dv, dconv_in, dz, dab], axis=1)
    dh2 = _matmul(dproj, w_in_f, "nt", F32, "mix_in_dx", tm=1024, tn=1024, tk=2816)
    dw_in = _matmul(h2, dproj, "tn", BF16, "mix_in_dw", tm=512, tn=2432, tk=2048)
    rs_mix = scatter_start([_cols_to_slabs(dw_in[:, :in_width], 4), dw_out.reshape(4, -1, D)], "rs_mix_start")
    dx1, dgain2, dsh2, dsc2 = _pre_bwd(x1, mix_norm, sh2, sc2, dh2, dx2, "mix_pre_bwd")

    rs_ffn1d = []

    def send_dw_d1(dw_d):
        rs_ffn1d.extend(scatter_start([dw_d.reshape(4, fs, D)], "rs_ffn1d_start"))
        return rs_ffn1d[4][0, 0]

    rs_gu1 = []

    def send_dw_gu1(slab):
        (other,) = _swap_halves_with_sibling([slab], "rs_sibling_swap")
        half = slab.shape[1] // 2
        mine = lax.dynamic_slice_in_dim(slab, ci * half, half, axis=1)
        p32, p16 = _add_cast([mine, other], [F32, BF16], "rs_chip_sum")
        rs_gu1.append(p32)
        rs_gu1.extend(_start_copies([p16], [lax.empty(p16.shape, BF16)], _plan_exchange_chips, "rs_chip_exchange_start"))
        return rs_gu1[5]

    dx0, dw_gu1, dw_d1, dgain1, dsh1, dsc1, dgt1 = _ffn_bwd(dx1, saved1, ffn1_norm, sh1, sc1, gt1 + rs_mix[4][0, 0], w_gu1, w_d1, fs,
                                                            "ffn1", on_dw_d=send_dw_d1, on_dw_gu=send_dw_gu1)

    (got,) = _wait_copies(rs_gu1[1], rs_gu1[2], rs_gu1[3], rs_gu1[4], dx0, _plan_exchange_chips, "rs_chip_exchange_wait")
    parts = [lax.dynamic_index_in_dim(rs_gu1[0], chip, axis=0, keepdims=True)]
    parts += [lax.dynamic_index_in_dim(got, (chip + k) % 4, axis=0, keepdims=True) for k in (1, 2, 3)]
    halves = [_add_cast(parts, [F32], "rs_total_0")[0][0]]
    arrived = _wait_copies(rs_ffn1d[0], rs_ffn1d[1], rs_ffn1d[2], rs_ffn1d[3], dx0, _plan_scatter_direct, "rs_ffn1d_wait")
    arrived += _wait_copies(rs_mix[0], rs_mix[1], rs_mix[2], rs_mix[3], dx0, _plan_scatter_direct, "rs_mix_wait")
    arrived += _wait_copies(rs_ffn2[0], rs_ffn2[1], rs_ffn2[2], rs_ffn2[3], dx0, _plan_scatter_direct, "rs_ffn2_wait")
    for t, zone in enumerate(arrived):
        halves.append(_add_cast([zone[d:d + 1] for d in range(8)], [F32], "rs_total_%d" % (t + 2))[0][0])
    theirs = _send_halves_to_sibling(halves, "rs_sibling_join")
    h_gu1, h_d1, h_in, h_out, h_gu2, h_d2 = zip(halves, theirs)

    n_small = N_MOD * D + 3 * D + 5 * LANES + CONV_WIDTH * conv_width
    cols_small = -(-n_small // (8 * LANES)) * LANES
    small = _flat_pad([dsh1, dsc1, dgt1, dsh2, dsc2, dgt2, dsh3, dsc3, dgt3, dgain1, dgain2, dgain3,
                       dq_gain, dk_gain, dalog, ddtb, ddn_gain, dconv_w], 8, cols_small)
    got2 = _all_gather_small(small, "gather_small_grads", after=theirs[0])
    small_sum = _add_cast([got2[d:d + 1] for d in range(8)], [F32], "sum_small_grads")[0].reshape(-1)
    dmod_all = got2.reshape(8, -1)[:, :N_MOD * D]
    off = [0]

    def take(n):
        off[0] += n
        return small_sum[off[0] - n:off[0]]

    g_b_ada = take(N_MOD * D).reshape(1, -1)
    g_ffn1_norm, g_mix_norm, g_ffn2_norm = take(D).reshape(1, D), take(D).reshape(1, D), take(D).reshape(1, D)
    g_q_norm, g_k_norm = take(LANES).reshape(1, -1), take(LANES).reshape(1, -1)
    g_a_log, g_dt_bias = take(LANES)[:HD].reshape(1, HD), take(LANES)[:HD].reshape(1, HD)
    g_dn = take(LANES).reshape(1, -1)
    g_conv_full = take(CONV_WIDTH * conv_width).reshape(CONV_WIDTH, conv_width)
    g_conv = lax.dynamic_slice(g_conv_full, (0, chip * conv_shard), (CONV_WIDTH, conv_shard))

    res = {}

    def upd(name, w, pair, col, m, v):
        res[name] = tuple(_adamw_halves(w, pair[0], pair[1], col, m, v, "adamw_" + name))

    upd("ffn1_w_gate", ffn1_w_gate, h_gu1, 0, m_ffn1_w_gate, v_ffn1_w_gate)
    upd("ffn1_w_up", ffn1_w_up, h_gu1, 1, m_ffn1_w_up, v_ffn1_w_up)
    upd("ffn1_w_down", ffn1_w_down, h_d1, 0, m_ffn1_w_down, v_ffn1_w_down)
    res["w_in"] = tuple(_adamw_halves_t(w_in, h_in[0], h_in[1], m_w_in, v_w_in, "adamw_w_in"))
    upd("w_out", w_out, h_out, 0, m_w_out, v_w_out)
    upd("ffn2_w_gate", ffn2_w_gate, h_gu2, 0, m_ffn2_w_gate, v_ffn2_w_gate)
    upd("ffn2_w_up", ffn2_w_up, h_gu2, 1, m_ffn2_w_up, v_ffn2_w_up)
    upd("ffn2_w_down", ffn2_w_down, h_d2, 0, m_ffn2_w_down, v_ffn2_w_down)
    d_cv, nm_cv, nv_cv = _adamw(conv_w[0], g_conv, m_conv_w[0], v_conv_w[0], "adamw_conv_w")
    res["conv_w"] = (g_conv[None], d_cv[None], nm_cv[None], nv_cv[None])

    dmod_mine = lax.dynamic_slice(dmod_all, (0, chip * n_mod_shard), (8, n_mod_shard))
    g, d, nm, nv = _adamw_outer(w_ada[0], m_w_ada[0], v_w_ada[0], jnp.transpose(c_all), dmod_mine, "adamw_w_ada")
    res["w_ada"] = (g[None], d[None], nm[None], nv[None])

    rep = [("b_ada", b_ada, g_b_ada, m_b_ada, v_b_ada), ("ffn1_norm", ffn1_norm, g_ffn1_norm, m_ffn1_norm, v_ffn1_norm),
           ("mix_norm", mix_norm, g_mix_norm, m_mix_norm, v_mix_norm), ("ffn2_norm", ffn2_norm, g_ffn2_norm, m_ffn2_norm, v_ffn2_norm),
           ("q_norm", q_norm, g_q_norm, m_q_norm, v_q_norm), ("k_norm", k_norm, g_k_norm, m_k_norm, v_k_norm),
           ("a_log", a_log, g_a_log, m_a_log, v_a_log), ("dt_bias", dt_bias, g_dt_bias, m_dt_bias, v_dt_bias),
           ("delta_out_norm", delta_out_norm, g_dn, m_delta_out_norm, v_delta_out_norm)]
    n_rep = sum(-(-r[1].shape[1] // LANES) * LANES for r in rep)
    cols_rep = -(-n_rep // (8 * LANES)) * LANES

    def pack_rep(idx):
        return _flat_pad([jnp.pad(r[idx], ((0, 0), (0, -r[idx].shape[1] % LANES))) for r in rep], 8, cols_rep)

    d_rep, nm_rep, nv_rep = [a.reshape(-1) for a in _adamw(pack_rep(1), pack_rep(2), pack_rep(3), pack_rep(4), "adamw_small")]
    o2 = 0
    for name, w, g, _, _ in rep:
        n = w.shape[1]
        res[name] = (g, d_rep[o2:o2 + n].reshape(1, n), nm_rep[o2:o2 + n].reshape(1, n), nv_rep[o2:o2 + n].reshape(1, n))
        o2 += -(-n // LANES) * LANES

    order = ["w_ada", "b_ada", "ffn1_norm", "ffn1_w_gate", "ffn1_w_up", "ffn1_w_down", "mix_norm", "w_in", "conv_w", "q_norm",
             "k_norm", "a_log", "dt_bias", "delta_out_norm", "w_out", "ffn2_norm", "ffn2_w_gate", "ffn2_w_up", "ffn2_w_down"]
    return (loss, dx0[None], *[res[n][0] for n in order], *[res[n][1] for n in order],
            *[res[n][2] for n in order], *[res[n][3] for n in order])
```

```python
import functools
import math

import jax
import jax.numpy as jnp
from jax import lax
from jax.experimental import pallas as pl
from jax.experimental.pallas import tpu as pltpu

F32 = jnp.float32
BF16 = jnp.bfloat16
MESH = pl.DeviceIdType.MESH

HEAD_DIM = 128
N_ATTN_HEADS = 8
N_DELTA_HEADS = 8
DILATED_PATTERNS = ((128, 1), (512, 4), (2048, 16))
ROPE_THETA = 500000.0
ROPE_DIM = HEAD_DIM // 4
CONV_WIDTH = 4
CHUNK = 64
NORM_EPS = 1e-6
N_MOD = 9
ADAM_LR = 0.001
ADAM_B1 = 0.9
ADAM_B2 = 0.999
ADAM_EPS = 1e-08
ADAM_WD = 0.01
ADAM_STEP = 10

LANES = 128
VMEM_LIMIT = 56 * 1024 * 1024
ATTN_TILE = 512
HIGHEST = lax.Precision.HIGHEST


def _cparams(sem=None):
    return pltpu.CompilerParams(dimension_semantics=sem, vmem_limit_bytes=VMEM_LIMIT)


def _pick(dim, pref):
    if dim <= pref:
        return dim
    t = (pref // LANES) * LANES
    while t >= LANES:
        if dim % t == 0:
            return t
        t -= LANES
    return dim


def _sigmoid(x):
    return 1.0 / (1.0 + jnp.exp(-x))


def _silu(x):
    return x * _sigmoid(x)


def _softplus(x):
    return jnp.maximum(x, 0.0) + jnp.log(1.0 + jnp.exp(-jnp.abs(x)))


def _rms(x, gain):
    return x * lax.rsqrt(jnp.mean(x * x, axis=-1, keepdims=True) + NORM_EPS) * gain


def _l2(x):
    return x * lax.rsqrt(jnp.sum(x * x, axis=-1, keepdims=True) + NORM_EPS)


def _modulate(x, gain, shift, scale):
    return _rms(x, gain) * (1.0 + scale) + shift


def _dot(a, b):
    return lax.dot_general(a, b, (((1,), (0,)), ((), ())), precision=HIGHEST, preferred_element_type=F32)


def _bdot(a, b, dims):
    return lax.dot_general(a.astype(BF16), b.astype(BF16), (dims, ((), ())), preferred_element_type=F32)


_NN, _NT, _TN = ((1,), (0,)), ((1,), (1,)), ((0,), (0,))
HIGH = lax.Precision.HIGH


def _dot3(a, b, dims=_NN):
    return lax.dot_general(a, b, (dims, ((), ())), precision=HIGH, preferred_element_type=F32)


@jax.custom_vjp
def _mm_nn(a, b):
    return _bdot(a, b, _NN)


_mm_nn.defvjp(lambda a, b: (_bdot(a, b, _NN), (a, b)),
              lambda res, g: (_bdot(g, res[1], _NT), _bdot(res[0], g, _TN)))


@jax.custom_vjp
def _mm_nt(a, b):
    return _bdot(a, b, _NT)


_mm_nt.defvjp(lambda a, b: (_bdot(a, b, _NT), (a, b)),
              lambda res, g: (_bdot(g, res[1], _NN), _bdot(g, res[0], _TN)))


@jax.custom_vjp
def _mm_tn(a, b):
    return _bdot(a, b, _TN)


_mm_tn.defvjp(lambda a, b: (_bdot(a, b, _TN), (a, b)),
              lambda res, g: (_bdot(res[1], g, _NT), _bdot(res[0], g, _NN)))


@jax.custom_vjp
def _tri_inv_saved(a, t_inv):
    return t_inv


_tri_inv_saved.defvjp(lambda a, t_inv: (t_inv, t_inv),
                      lambda t_inv, g: (-_dot3(t_inv, _dot3(g, t_inv, _NT), _TN), jnp.zeros_like(t_inv)))


_MM_DIMS = {"nn": ((1,), (0,)), "nt": ((1,), (1,)), "tn": ((0,), (0,))}


def _matmul(a, b, mode, out_dtype, name, tm=1024, tn=1024, tk=1024, col_slabs=False, epilogue=None, after=None):
    if mode == "nn":
        (M, K), (_, N) = a.shape, b.shape
    elif mode == "nt":
        (M, K), (N, _) = a.shape, b.shape
    else:
        (K, M), (_, N) = a.shape, b.shape
    tm, tn, tk = _pick(M, tm), _pick(N, tn), _pick(K, tk)
    nk = K // tk
    dims = _MM_DIMS[mode]
    epi_fn, extra_in, outs = epilogue if epilogue is not None else (None, [], None)
    if outs is None:
        if col_slabs:
            outs = [((N // tn, M, tn), out_dtype, (None, tm, tn), lambda i, j: (j, i, 0))]
        else:
            outs = [((M, N), out_dtype, (tm, tn), lambda i, j: (i, j))]
    n_in, n_out = len(extra_in), len(outs)

    tied = [] if after is None else [after]

    def body(a_ref, b_ref, *rest):
        rest = rest[len(tied):]
        in_refs, out_refs = rest[:n_in], rest[n_in:n_in + n_out]
        k = pl.program_id(2)
        p = _bdot(a_ref[...], b_ref[...], dims)

        def finish(tile):
            if epi_fn is None:
                out_refs[0][...] = tile.astype(out_dtype)
            else:
                epi_fn(tile, in_refs, out_refs)

        if nk == 1:
            finish(p)
        else:
            acc_ref = rest[-1]

            @pl.when(k == 0)
            def _():
                acc_ref[...] = p

            @pl.when((k > 0) & (k < nk - 1))
            def _():
                acc_ref[...] += p

            @pl.when(k == nk - 1)
            def _():
                finish(acc_ref[...] + p)

    def ij(index_map):
        return lambda i, j, k: index_map(i, j)

    a_spec = pl.BlockSpec((tk, tm), lambda i, j, k: (k, i)) if mode == "tn" else pl.BlockSpec((tm, tk), lambda i, j, k: (i, k))
    b_spec = pl.BlockSpec((tn, tk), lambda i, j, k: (j, k)) if mode == "nt" else pl.BlockSpec((tk, tn), lambda i, j, k: (k, j))
    res = pl.pallas_call(
        body, name=name, grid=(M // tm, N // tn, nk),
        in_specs=[a_spec, b_spec] + [pl.BlockSpec(memory_space=pl.ANY)] * len(tied) + [pl.BlockSpec(blk, ij(im)) for _, blk, im in extra_in],
        out_specs=[pl.BlockSpec(blk, ij(im)) for _, _, blk, im in outs],
        out_shape=[jax.ShapeDtypeStruct(shp, dt) for shp, dt, _, _ in outs],
        scratch_shapes=[pltpu.VMEM((tm, tn), F32)] if nk > 1 else [],
        compiler_params=_cparams(("parallel", "parallel", "arbitrary")),
    )(a, b, *tied, *[arr for arr, _, _ in extra_in])
    return res if epilogue is not None else res[0]


def _row_spec(tr, d):
    return pl.BlockSpec((tr, d), lambda i: (i, 0))


def _vec_spec(d):
    return pl.BlockSpec((1, d), lambda i: (0, 0))


def _pre_fwd(x, gain, shift, scale, name):
    S, D = x.shape
    tr = _pick(S, 256)

    def body(x_ref, g_ref, sh_ref, sc_ref, h_ref):
        h_ref[...] = _modulate(x_ref[...], g_ref[...], sh_ref[...], sc_ref[...]).astype(BF16)

    return pl.pallas_call(
        body, name=name, grid=(S // tr,),
        in_specs=[_row_spec(tr, D), _vec_spec(D), _vec_spec(D), _vec_spec(D)],
        out_specs=_row_spec(tr, D), out_shape=jax.ShapeDtypeStruct((S, D), BF16),
        compiler_params=_cparams(("parallel",)),
    )(x, gain, shift, scale)


def _pre_bwd(x, gain, shift, scale, dh, dx_in, name):
    S, D = x.shape
    tr = _pick(S, 256)

    def body(x_ref, g_ref, sh_ref, sc_ref, dh_ref, dxin_ref, dx_ref, dg_ref, dsh_ref, dsc_ref):
        _, vjp = jax.vjp(_modulate, x_ref[...], g_ref[...], sh_ref[...], sc_ref[...])
        dx, dg, dsh, dsc = vjp(dh_ref[...])
        dx_ref[...] = dxin_ref[...] + dx

        @pl.when(pl.program_id(0) == 0)
        def _():
            dg_ref[...] = jnp.zeros_like(dg_ref)
            dsh_ref[...] = jnp.zeros_like(dsh_ref)
            dsc_ref[...] = jnp.zeros_like(dsc_ref)

        dg_ref[...] += dg
        dsh_ref[...] += dsh
        dsc_ref[...] += dsc

    vec = jax.ShapeDtypeStruct((1, D), F32)
    return pl.pallas_call(
        body, name=name, grid=(S // tr,),
        in_specs=[_row_spec(tr, D), _vec_spec(D), _vec_spec(D), _vec_spec(D), _row_spec(tr, D), _row_spec(tr, D)],
        out_specs=[_row_spec(tr, D), _vec_spec(D), _vec_spec(D), _vec_spec(D)],
        out_shape=[jax.ShapeDtypeStruct((S, D), F32), vec, vec, vec],
        compiler_params=_cparams(("arbitrary",)),
    )(x, gain, shift, scale, dh, dx_in)


def _residual_fwd(x, gate, f, coef, name):
    S, D = x.shape
    tr = _pick(S, 256)

    def body(x_ref, g_ref, f_ref, o_ref):
        o_ref[...] = x_ref[...] + coef * g_ref[...] * f_ref[...]

    return pl.pallas_call(
        body, name=name, grid=(S // tr,),
        in_specs=[_row_spec(tr, D), _vec_spec(D), _row_spec(tr, D)],
        out_specs=_row_spec(tr, D), out_shape=jax.ShapeDtypeStruct((S, D), F32),
        compiler_params=_cparams(("parallel",)),
    )(x, gate, f)


def _residual_bwd(gate, f, dxn, coef, name):
    S, D = f.shape
    tr = _pick(S, 256)

    def body(g_ref, f_ref, d_ref, df_ref, dg_ref):
        d = d_ref[...]
        df_ref[...] = (coef * g_ref[...] * d).astype(BF16)

        @pl.when(pl.program_id(0) == 0)
        def _():
            dg_ref[...] = jnp.zeros_like(dg_ref)

        dg_ref[...] += jnp.sum(coef * f_ref[...] * d, axis=0, keepdims=True)

    return pl.pallas_call(
        body, name=name, grid=(S // tr,),
        in_specs=[_vec_spec(D), _row_spec(tr, D), _row_spec(tr, D)],
        out_specs=[_row_spec(tr, D), _vec_spec(D)],
        out_shape=[jax.ShapeDtypeStruct((S, D), BF16), jax.ShapeDtypeStruct((1, D), F32)],
        compiler_params=_cparams(("arbitrary",)),
    )(gate, f, dxn)


def _swiglu_fn(a, b):
    return _silu(a) * b


def _loss_head(y, target, name):
    S, D = y.shape
    tr = _pick(S, 256)

    def body(y_ref, t_ref, l_ref, dy_ref):
        e = y_ref[...] - t_ref[...]
        dy_ref[...] = e * (1.0 / D)

        @pl.when(pl.program_id(0) == 0)
        def _():
            l_ref[...] = jnp.zeros_like(l_ref)

        l_ref[...] += jnp.sum(jnp.sum(e * e, axis=-1, keepdims=True), axis=0, keepdims=True) * (0.5 / D)

    return pl.pallas_call(
        body, name=name, grid=(S // tr,),
        in_specs=[_row_spec(tr, D), _row_spec(tr, D)],
        out_specs=[pl.BlockSpec((1, 1), lambda i: (0, 0)), _row_spec(tr, D)],
        out_shape=[jax.ShapeDtypeStruct((1, 1), F32), jax.ShapeDtypeStruct((S, D), F32)],
        compiler_params=_cparams(("arbitrary",)),
    )(y, target)


def _rope(y, cc, s1, s2):
    return y * cc + pltpu.roll(y, LANES - ROPE_DIM // 2, 1) * s1 + pltpu.roll(y, ROPE_DIM // 2, 1) * s2


def _rope_t(d, cc, s1, s2):
    return d * cc + pltpu.roll(d * s1, ROPE_DIM // 2, 1) + pltpu.roll(d * s2, LANES - ROPE_DIM // 2, 1)


def _head_spec(tr, col0):
    return pl.BlockSpec((tr, HEAD_DIM), lambda i, h: (i, col0 + h))


def _tab_spec(tr):
    return pl.BlockSpec((tr, HEAD_DIM), lambda i, h: (i, 0))


def _gain_spec():
    return pl.BlockSpec((1, HEAD_DIM), lambda i, h: (0, 0))


def _attn_prep_fwd(proj, q_gain, k_gain, cc, s1, s2, name):
    S = proj.shape[0]
    H = N_ATTN_HEADS
    tr = _pick(S, 512)

    def body(q_ref, k_ref, qg_ref, kg_ref, cc_ref, s1_ref, s2_ref, qo_ref, ko_ref):
        cc, s1, s2 = cc_ref[...], s1_ref[...], s2_ref[...]
        qo_ref[...] = _rope(_rms(q_ref[...], qg_ref[...]), cc, s1, s2)
        ko_ref[...] = _rope(_rms(k_ref[...], kg_ref[...]), cc, s1, s2)

    out = jax.ShapeDtypeStruct((S, H * HEAD_DIM), F32)
    return pl.pallas_call(
        body, name=name, grid=(S // tr, H),
        in_specs=[_head_spec(tr, 0), _head_spec(tr, H), _gain_spec(), _gain_spec(), _tab_spec(tr), _tab_spec(tr), _tab_spec(tr)],
        out_specs=[_head_spec(tr, 0), _head_spec(tr, 0)], out_shape=[out, out],
        compiler_params=_cparams(("parallel", "parallel")),
    )(proj, proj, q_gain, k_gain, cc, s1, s2)


def _attn_prep_bwd(proj, q_gain, k_gain, cc, s1, s2, near, far, name):
    S = proj.shape[0]
    H = N_ATTN_HEADS
    tr = _pick(S, 512)

    def body(q_ref, k_ref, qg_ref, kg_ref, cc_ref, s1_ref, s2_ref, dqn_ref, dkn_ref, dvn_ref, dqf_ref, dkf_ref, dvf_ref,
             dpq_ref, dpk_ref, dpv_ref, dqg_ref, dkg_ref):
        cc, s1, s2 = cc_ref[...], s1_ref[...], s2_ref[...]
        dq = dqn_ref[...] + dqf_ref[...]
        dk = dkn_ref[...] + dkf_ref[...]
        dpv_ref[...] = (dvn_ref[...] + dvf_ref[...]).astype(BF16)

        @pl.when((pl.program_id(0) == 0) & (pl.program_id(1) == 0))
        def _():
            dqg_ref[...] = jnp.zeros_like(dqg_ref)
            dkg_ref[...] = jnp.zeros_like(dkg_ref)

        _, vjp_q = jax.vjp(_rms, q_ref[...], qg_ref[...])
        dxq, dgq = vjp_q(_rope_t(dq, cc, s1, s2))
        _, vjp_k = jax.vjp(_rms, k_ref[...], kg_ref[...])
        dxk, dgk = vjp_k(_rope_t(dk, cc, s1, s2))
        dpq_ref[...] = dxq.astype(BF16)
        dpk_ref[...] = dxk.astype(BF16)
        dqg_ref[...] += dgq
        dkg_ref[...] += dgk

    out = jax.ShapeDtypeStruct((S, H * HEAD_DIM), BF16)
    gout = jax.ShapeDtypeStruct((1, HEAD_DIM), F32)
    return pl.pallas_call(
        body, name=name, grid=(S // tr, H),
        in_specs=[_head_spec(tr, 0), _head_spec(tr, H), _gain_spec(), _gain_spec(), _tab_spec(tr), _tab_spec(tr), _tab_spec(tr)]
        + [_head_spec(tr, 0)] * 6,
        out_specs=[_head_spec(tr, 0)] * 3 + [_gain_spec(), _gain_spec()], out_shape=[out, out, out, gout, gout],
        compiler_params=_cparams(("arbitrary", "arbitrary")),
    )(proj, proj, q_gain, k_gain, cc, s1, s2, *near, *far)


def _multiplicity(j, t):
    ti = lax.broadcasted_iota(jnp.int32, (t, t), 0)
    si = lax.broadcasted_iota(jnp.int32, (t, t), 1)
    delta = j * t + ti - si
    cnt = jnp.zeros((t, t), F32)
    for window, dil in NEAR_PATTERNS:
        ok = (delta >= 0) & ((delta & (dil - 1)) == 0) & (delta <= window)
        cnt = cnt + ok.astype(F32)
    return cnt


_NEG = -1e30
NEAR_PATTERNS = DILATED_PATTERNS[:2]
NEAR_WINDOW = max(w for w, _ in NEAR_PATTERNS)
FAR_WINDOW, FAR_DIL = DILATED_PATTERNS[2]


def _log_multiplicity_table(t):
    cnt = jnp.stack([_multiplicity(j, t) for j in range(NEAR_WINDOW // t + 1)])
    return jnp.where(cnt > 0.0, jnp.log(jnp.maximum(cnt, 1.0)), _NEG)


def _bias_spec(t):
    return pl.BlockSpec((NEAR_WINDOW // t + 1, t, t), lambda h, i, j: (0, 0, 0))


def _attn_fwd(q, k, proj, v_col0, bias, name):
    S = q.shape[0]
    H = N_ATTN_HEADS
    t = _pick(S, ATTN_TILE)
    nq = S // t
    nj = NEAR_WINDOW // t + 1
    scale = HEAD_DIM ** -0.5

    def body(q_ref, k_ref, v_ref, b_ref, o_ref, lse_ref, m_sc, l_sc, acc_sc):
        qb, j = pl.program_id(1), pl.program_id(2)

        @pl.when(j == 0)
        def _():
            m_sc[...] = jnp.full_like(m_sc, _NEG)
            l_sc[...] = jnp.zeros_like(l_sc)
            acc_sc[...] = jnp.zeros_like(acc_sc)

        @pl.when(qb - j >= 0)
        def _():
            s = _bdot(q_ref[...], k_ref[...], ((1,), (1,))) * scale + b_ref[j]
            m_prev = m_sc[...]
            m_new = jnp.maximum(m_prev, jnp.max(s, axis=-1, keepdims=True))
            alpha = jnp.exp(m_prev - m_new)
            p = jnp.exp(s - m_new)
            l_sc[...] = alpha * l_sc[...] + jnp.sum(p, axis=-1, keepdims=True)
            acc_sc[...] = alpha * acc_sc[...] + _bdot(p, v_ref[...], ((1,), (0,)))
            m_sc[...] = m_new

        @pl.when(j == nj - 1)
        def _():
            o_ref[...] = acc_sc[...] / l_sc[...]
            lse_ref[...] = jnp.broadcast_to(m_sc[...] + jnp.log(l_sc[...]), (t, HEAD_DIM))

    qspec = pl.BlockSpec((t, HEAD_DIM), lambda h, i, j: (i, h))
    kspec = pl.BlockSpec((t, HEAD_DIM), lambda h, i, j: (jnp.maximum(i - j, 0), h))
    vspec = pl.BlockSpec((t, HEAD_DIM), lambda h, i, j: (jnp.maximum(i - j, 0), v_col0 + h))
    return pl.pallas_call(
        body, name=name, grid=(H, nq, nj),
        in_specs=[qspec, kspec, vspec, _bias_spec(t)], out_specs=[qspec, qspec],
        out_shape=[jax.ShapeDtypeStruct((S, H * HEAD_DIM), F32), jax.ShapeDtypeStruct((S, H * HEAD_DIM), F32)],
        scratch_shapes=[pltpu.VMEM((t, 1), F32), pltpu.VMEM((t, 1), F32), pltpu.VMEM((t, HEAD_DIM), F32)],
        compiler_params=_cparams(("parallel", "parallel", "arbitrary")),
    )(q, k, proj, bias)


def _far_rows(r, n):
    return pl.ds(r, n, stride=FAR_DIL)


def _far_band_bias(n):
    i = lax.broadcasted_iota(jnp.int32, (n, n), 0)
    j = lax.broadcasted_iota(jnp.int32, (n, n), 1)
    return jnp.where((i >= j) & (i - j <= FAR_WINDOW // FAR_DIL), 0.0, _NEG)


def _col_spec(S, col0):
    return pl.BlockSpec((S, HEAD_DIM), lambda h: (0, col0 + h))


def _attn_far_fwd(q, k, proj, v_col0, o_near, lse_near, name):
    S = q.shape[0]
    H = N_ATTN_HEADS
    n = S // FAR_DIL
    scale = HEAD_DIM ** -0.5

    def body(q_ref, k_ref, v_ref, on_ref, ln_ref, o_ref, ob_ref, lse_ref):
        bias = _far_band_bias(n)
        for r in range(FAR_DIL):
            rows = _far_rows(r, n)
            s = _bdot(q_ref[rows, :], k_ref[rows, :], _NT) * scale + bias
            m = jnp.max(s, axis=-1, keepdims=True)
            p = jnp.exp(s - m)
            l = jnp.sum(p, axis=-1, keepdims=True)
            o_far = _bdot(p, v_ref[rows, :], _NN) / l
            lse_far = m + jnp.log(l)
            lse_near = jnp.max(ln_ref[rows, :], axis=-1, keepdims=True)
            top = jnp.maximum(lse_near, lse_far)
            lse = top + jnp.log(jnp.exp(lse_near - top) + jnp.exp(lse_far - top))
            o_ref[rows, :] = jnp.exp(lse_near - lse) * on_ref[rows, :] + jnp.exp(lse_far - lse) * o_far
            lse_ref[rows, :] = jnp.broadcast_to(lse, (n, HEAD_DIM))
        ob_ref[...] = o_ref[...].astype(BF16)

    cs = _col_spec(S, 0)
    return pl.pallas_call(
        body, name=name, grid=(H,),
        in_specs=[cs, cs, _col_spec(S, v_col0), cs, cs], out_specs=[cs, cs, cs],
        out_shape=[jax.ShapeDtypeStruct((S, H * HEAD_DIM), F32), jax.ShapeDtypeStruct((S, H * HEAD_DIM), BF16),
                   jax.ShapeDtypeStruct((S, H * HEAD_DIM), F32)],
        compiler_params=_cparams(("parallel",)),
    )(q, k, proj, o_near, lse_near)


def _attn_far_bwd(q, k, proj, v_col0, o, lse, do, do_col0, name):
    S = q.shape[0]
    H = N_ATTN_HEADS
    n = S // FAR_DIL
    scale = HEAD_DIM ** -0.5

    def body(q_ref, k_ref, v_ref, o_ref, lse_ref, do_ref, dq_ref, dk_ref, dv_ref):
        bias = _far_band_bias(n)
        for r in range(FAR_DIL):
            rows = _far_rows(r, n)
            q, k, v, do = q_ref[rows, :], k_ref[rows, :], v_ref[rows, :], do_ref[rows, :]
            dsum = jnp.sum(do * o_ref[rows, :], axis=-1, keepdims=True)
            lse = jnp.max(lse_ref[rows, :], axis=-1, keepdims=True)
            p = jnp.exp(_bdot(q, k, _NT) * scale + bias - lse)
            ds = p * (_bdot(do, v, _NT) - dsum)
            dq_ref[rows, :] = _bdot(ds, k, _NN) * scale
            dk_ref[rows, :] = _bdot(ds, q, _TN) * scale
            dv_ref[rows, :] = _bdot(p, do, _TN)

    cs = _col_spec(S, 0)
    shp = jax.ShapeDtypeStruct((S, H * HEAD_DIM), F32)
    return pl.pallas_call(
        body, name=name, grid=(H,),
        in_specs=[cs, cs, _col_spec(S, v_col0), cs, cs, _col_spec(S, do_col0)], out_specs=[cs, cs, cs], out_shape=[shp, shp, shp],
        compiler_params=_cparams(("parallel",)),
    )(q, k, proj, o, lse, do)


def _attn_probs(q, k, lse, bias_tile, scale):
    return jnp.exp(_bdot(q, k, ((1,), (1,))) * scale + bias_tile - lse)


def _attn_bwd_dq(q, k, proj, v_col0, o, lse, do, do_col0, bias, name):
    S = q.shape[0]
    H = N_ATTN_HEADS
    t = _pick(S, ATTN_TILE)
    nq = S // t
    nj = NEAR_WINDOW // t + 1
    scale = HEAD_DIM ** -0.5

    def body(q_ref, k_ref, v_ref, o_ref, lse_ref, do_ref, b_ref, dq_ref, acc_sc):
        qb, j = pl.program_id(1), pl.program_id(2)

        @pl.when(j == 0)
        def _():
            acc_sc[...] = jnp.zeros_like(acc_sc)

        @pl.when(qb - j >= 0)
        def _():
            do = do_ref[...]
            dsum = jnp.sum(do * o_ref[...], axis=-1, keepdims=True)
            lse = jnp.max(lse_ref[...], axis=-1, keepdims=True)
            p = _attn_probs(q_ref[...], k_ref[...], lse, b_ref[j], scale)
            dp = _bdot(do, v_ref[...], ((1,), (1,)))
            ds = p * (dp - dsum)
            acc_sc[...] += _bdot(ds, k_ref[...], ((1,), (0,))) * scale

        @pl.when(j == nj - 1)
        def _():
            dq_ref[...] = acc_sc[...]

    qspec = pl.BlockSpec((t, HEAD_DIM), lambda h, i, j: (i, h))
    dospec = pl.BlockSpec((t, HEAD_DIM), lambda h, i, j: (i, do_col0 + h))
    kspec = pl.BlockSpec((t, HEAD_DIM), lambda h, i, j: (jnp.maximum(i - j, 0), h))
    vspec = pl.BlockSpec((t, HEAD_DIM), lambda h, i, j: (jnp.maximum(i - j, 0), v_col0 + h))
    return pl.pallas_call(
        body, name=name, grid=(H, nq, nj),
        in_specs=[qspec, kspec, vspec, qspec, qspec, dospec, _bias_spec(t)], out_specs=qspec,
        out_shape=jax.ShapeDtypeStruct((S, H * HEAD_DIM), F32),
        scratch_shapes=[pltpu.VMEM((t, HEAD_DIM), F32)],
        compiler_params=_cparams(("parallel", "parallel", "arbitrary")),
    )(q, k, proj, o, lse, do, bias)


def _attn_bwd_dkv(q, k, proj, v_col0, o, lse, do, do_col0, bias, name):
    S = q.shape[0]
    H = N_ATTN_HEADS
    t = _pick(S, ATTN_TILE)
    nq = S // t
    nj = NEAR_WINDOW // t + 1
    scale = HEAD_DIM ** -0.5

    def body(q_ref, k_ref, v_ref, o_ref, lse_ref, do_ref, b_ref, dk_ref, dv_ref, dk_sc, dv_sc):
        kb, j = pl.program_id(1), pl.program_id(2)

        @pl.when(j == 0)
        def _():
            dk_sc[...] = jnp.zeros_like(dk_sc)
            dv_sc[...] = jnp.zeros_like(dv_sc)

        @pl.when(kb + j < nq)
        def _():
            do = do_ref[...]
            dsum = jnp.sum(do * o_ref[...], axis=-1, keepdims=True)
            lse = jnp.max(lse_ref[...], axis=-1, keepdims=True)
            p = _attn_probs(q_ref[...], k_ref[...], lse, b_ref[j], scale)
            dp = _bdot(do, v_ref[...], ((1,), (1,)))
            ds = p * (dp - dsum)
            dv_sc[...] += _bdot(p, do, ((0,), (0,)))
            dk_sc[...] += _bdot(ds, q_ref[...], ((0,), (0,))) * scale

        @pl.when(j == nj - 1)
        def _():
            dk_ref[...] = dk_sc[...]
            dv_ref[...] = dv_sc[...]

    def qrow(h, i, j):
        return jnp.minimum(i + j, nq - 1)

    qspec = pl.BlockSpec((t, HEAD_DIM), lambda h, i, j: (qrow(h, i, j), h))
    dospec = pl.BlockSpec((t, HEAD_DIM), lambda h, i, j: (qrow(h, i, j), do_col0 + h))
    kspec = pl.BlockSpec((t, HEAD_DIM), lambda h, i, j: (i, h))
    vspec = pl.BlockSpec((t, HEAD_DIM), lambda h, i, j: (i, v_col0 + h))
    return pl.pallas_call(
        body, name=name, grid=(H, nq, nj),
        in_specs=[qspec, kspec, vspec, qspec, qspec, dospec, _bias_spec(t)], out_specs=[kspec, kspec],
        out_shape=[jax.ShapeDtypeStruct((S, H * HEAD_DIM), F32), jax.ShapeDtypeStruct((S, H * HEAD_DIM), F32)],
        scratch_shapes=[pltpu.VMEM((t, HEAD_DIM), F32), pltpu.VMEM((t, HEAD_DIM), F32)],
        compiler_params=_cparams(("parallel", "parallel", "arbitrary")),
    )(q, k, proj, o, lse, do, bias)


def _conv_pre(x_ref, w_ref):
    x = x_ref[...]
    rows = lax.broadcasted_iota(jnp.int32, x.shape, 0)
    shifted = [x]
    acc = x * w_ref[pl.ds(CONV_WIDTH - 1, 1), :]
    for sft in range(1, CONV_WIDTH):
        xs = jnp.where(rows >= sft, pltpu.roll(x, sft, 0), 0.0)
        shifted.append(xs)
        acc = acc + xs * w_ref[pl.ds(CONV_WIDTH - 1 - sft, 1), :]
    return acc, shifted


def _conv_fwd(proj, col0, width, w, name):
    S = proj.shape[0]

    def body(x_ref, w_ref, y_ref):
        acc, _ = _conv_pre(x_ref, w_ref)
        y_ref[...] = _silu(acc)

    return pl.pallas_call(
        body, name=name, grid=(width // LANES,),
        in_specs=[pl.BlockSpec((S, LANES), lambda c: (0, col0 + c)), pl.BlockSpec((CONV_WIDTH, LANES), lambda c: (0, c))],
        out_specs=pl.BlockSpec((S, LANES), lambda c: (0, c)),
        out_shape=jax.ShapeDtypeStruct((S, width), F32),
        compiler_params=_cparams(("parallel",)),
    )(proj, w)


def _conv_bwd(proj, col0, width, w, dy, name):
    S = proj.shape[0]

    def body(x_ref, w_ref, d_ref, dx_ref, dw_ref):
        acc, shifted = _conv_pre(x_ref, w_ref)
        sig = _sigmoid(acc)
        da = d_ref[...] * (sig * (1.0 + acc * (1.0 - sig)))
        rows = lax.broadcasted_iota(jnp.int32, da.shape, 0)
        dx = da * w_ref[pl.ds(CONV_WIDTH - 1, 1), :]
        dw_ref[pl.ds(CONV_WIDTH - 1, 1), :] = jnp.sum(da * shifted[0], axis=0, keepdims=True)
        for sft in range(1, CONV_WIDTH):
            back = jnp.where(rows < S - sft, pltpu.roll(da, S - sft, 0), 0.0)
            dx = dx + back * w_ref[pl.ds(CONV_WIDTH - 1 - sft, 1), :]
            dw_ref[pl.ds(CONV_WIDTH - 1 - sft, 1), :] = jnp.sum(da * shifted[sft], axis=0, keepdims=True)
        dx_ref[...] = dx.astype(BF16)

    return pl.pallas_call(
        body, name=name, grid=(width // LANES,),
        in_specs=[pl.BlockSpec((S, LANES), lambda c: (0, col0 + c)), pl.BlockSpec((CONV_WIDTH, LANES), lambda c: (0, c)),
                  pl.BlockSpec((S, LANES), lambda c: (0, c))],
        out_specs=[pl.BlockSpec((S, LANES), lambda c: (0, c)), pl.BlockSpec((CONV_WIDTH, LANES), lambda c: (0, c))],
        out_shape=[jax.ShapeDtypeStruct((S, width), BF16), jax.ShapeDtypeStruct((CONV_WIDTH, width), F32)],
        compiler_params=_cparams(("parallel",)),
    )(proj, w, dy)


PREP_CHUNKS = 16


def _chunks_prep(qraws, kraws, vs, abs_, alog_row, dtb_row, mask_g, mask_b, t_saved=None):
    n = len(qraws)
    c = qraws[0].shape[0]
    mm_nt, mm_nn = (_mm_nt, _mm_nn) if t_saved is not None else (lambda p, r: _bdot(p, r, _NT), lambda p, r: _bdot(p, r, _NN))
    row = lax.broadcasted_iota(jnp.int32, (c, c), 0)
    col = lax.broadcasted_iota(jnp.int32, (c, c), 1)
    tril, strict, eye = row >= col, row > col, row == col
    eyef = eye.astype(F32)
    neg_rate = -jnp.exp(alog_row)
    q, k, beta, gc_col, gamma, kb, g_last = [], [], [], [], [], [], []
    for i in range(n):
        q.append(_l2(qraws[i]) * (HEAD_DIM ** -0.5))
        k.append(_l2(kraws[i]))
        gfull = neg_rate * _softplus(abs_[i] + dtb_row)
        g = jnp.sum(jnp.where(mask_g, gfull, 0.0), axis=-1, keepdims=True)
        beta.append(jnp.sum(jnp.where(mask_b, _sigmoid(abs_[i]), 0.0), axis=-1, keepdims=True))
        g_row = jnp.sum(jnp.where(eye, g, 0.0), axis=0, keepdims=True)
        gc_col.append(jnp.sum(jnp.where(tril, g_row, 0.0), axis=1, keepdims=True))
        gc_row = jnp.sum(jnp.where(row <= col, g, 0.0), axis=0, keepdims=True)
        gamma.append(jnp.where(tril, jnp.exp(jnp.where(tril, gc_col[i] - gc_row, 0.0)), 0.0))
        kb.append(k[i] * beta[i])
        g_last.append(jnp.sum(g, axis=0, keepdims=True))
    a = [jnp.where(strict, mm_nt(kb[i], k[i]) * gamma[i], 0.0) for i in range(n)]
    if t_saved is None:
        t_inv = [eyef - a[i] for i in range(n)]
        p = a
        for _ in range(int(math.log2(c)) - 1):
            p = [_dot3(p[i], p[i]) for i in range(n)]
            t_inv = [_dot3(t_inv[i], eyef + p[i]) for i in range(n)]
    else:
        t_inv = [_tri_inv_saved(a[i], t_saved[i]) for i in range(n)]
    egc = [jnp.exp(gc_col[i]) for i in range(n)]
    u = [mm_nn(t_inv[i], vs[i] * beta[i]) for i in range(n)]
    w = [mm_nn(t_inv[i], kb[i] * egc[i]) for i in range(n)]
    intra = [mm_nt(q[i], k[i]) * gamma[i] for i in range(n)]
    out = []
    for i in range(n):
        kt = k[i] * jnp.exp(g_last[i] - gc_col[i])
        dec = jnp.broadcast_to(jnp.exp(g_last[i]), (1, HEAD_DIM))
        one = (u[i], w[i], q[i] * egc[i], kt, intra[i], dec)
        out.append(one + (t_inv[i],) if t_saved is None else one)
    return out


def _lane_masks(h):
    lane = lax.broadcasted_iota(jnp.int32, (1, LANES), 1)
    return lane == h, lane == N_DELTA_HEADS + h


def _prep_specs(tr, ab_col):
    H = N_DELTA_HEADS
    return [
        pl.BlockSpec((tr, HEAD_DIM), lambda i, h: (i, h)),
        pl.BlockSpec((tr, HEAD_DIM), lambda i, h: (i, H + h)),
        pl.BlockSpec((tr, HEAD_DIM), lambda i, h: (i, 2 * H + h)),
        pl.BlockSpec((tr, LANES), lambda i, h: (i, ab_col)),
        pl.BlockSpec((1, LANES), lambda i, h: (0, 0)),
        pl.BlockSpec((1, LANES), lambda i, h: (0, 0)),
    ]


def _prep_out_specs(tr):
    nc = tr // CHUNK
    hs = pl.BlockSpec((tr, HEAD_DIM), lambda i, h: (i, h))
    return [hs, hs, hs, hs,
            pl.BlockSpec((None, tr, CHUNK), lambda i, h: (h, i, 0)),
            pl.BlockSpec((None, nc, 1, HEAD_DIM), lambda i, h: (h, i, 0, 0)),
            pl.BlockSpec((None, tr, CHUNK), lambda i, h: (h, i, 0))]


def _prep_out_shapes(S):
    H = N_DELTA_HEADS
    hs = jax.ShapeDtypeStruct((S, H * HEAD_DIM), F32)
    sq = jax.ShapeDtypeStruct((H, S, CHUNK), F32)
    return [hs, hs, hs, hs, sq, jax.ShapeDtypeStruct((H, S // CHUNK, 1, HEAD_DIM), F32), sq]


def _delta_prep_fwd(dqkv, proj, ab_col, alog_row, dtb_row, name):
    S = dqkv.shape[0]
    tr = min(S, PREP_CHUNKS * CHUNK)
    nc = tr // CHUNK

    def body(q_ref, k_ref, v_ref, ab_ref, al_ref, dt_ref, u_ref, w_ref, qd_ref, kt_ref, in_ref, dec_ref, ti_ref):
        mask_g, mask_b = _lane_masks(pl.program_id(1))
        rows = [pl.ds(ci * CHUNK, CHUNK) for ci in range(nc)]
        outs = _chunks_prep([q_ref[rs, :] for rs in rows], [k_ref[rs, :] for rs in rows], [v_ref[rs, :] for rs in rows],
                            [ab_ref[rs, :] for rs in rows], al_ref[...], dt_ref[...], mask_g, mask_b)
        for ci, rs in enumerate(rows):
            u, w, qd, kt, intra, dec, t_inv = outs[ci]
            u_ref[rs, :] = u
            w_ref[rs, :] = w
            qd_ref[rs, :] = qd
            kt_ref[rs, :] = kt
            in_ref[rs, :] = intra
            dec_ref[ci] = dec
            ti_ref[rs, :] = t_inv

    return pl.pallas_call(
        body, name=name, grid=(S // tr, N_DELTA_HEADS),
        in_specs=_prep_specs(tr, ab_col), out_specs=_prep_out_specs(tr), out_shape=_prep_out_shapes(S),
        compiler_params=_cparams(("parallel", "parallel")),
    )(dqkv, dqkv, dqkv, proj, alog_row, dtb_row)


def _delta_prep_bwd(dqkv, proj, ab_col, alog_row, dtb_row, cots, t_inv, name):
    S = dqkv.shape[0]
    H = N_DELTA_HEADS
    tr = min(S, PREP_CHUNKS * CHUNK)
    nc = tr // CHUNK

    def body(q_ref, k_ref, v_ref, ab_ref, al_ref, dt_ref, du_ref, dw_ref, dqd_ref, dkt_ref, din_ref, ddec_ref, ti_ref,
             dq_ref, dk_ref, dv_ref, dab_ref, dal_ref, ddt_ref, dab_sc):
        h = pl.program_id(1)
        mask_g, mask_b = _lane_masks(h)

        @pl.when((pl.program_id(0) == 0) & (h == 0))
        def _():
            dal_ref[...] = jnp.zeros_like(dal_ref)
            ddt_ref[...] = jnp.zeros_like(ddt_ref)

        @pl.when(h == 0)
        def _():
            dab_sc[...] = jnp.zeros_like(dab_sc)

        rows = [pl.ds(ci * CHUNK, CHUNK) for ci in range(nc)]
        fn = functools.partial(_chunks_prep, mask_g=mask_g, mask_b=mask_b, t_saved=[ti_ref[rs, :] for rs in rows])
        _, vjp = jax.vjp(fn, [q_ref[rs, :] for rs in rows], [k_ref[rs, :] for rs in rows], [v_ref[rs, :] for rs in rows],
                         [ab_ref[rs, :] for rs in rows], al_ref[...], dt_ref[...])
        dqs, dks, dvs, dabs, dal, ddt = vjp([(du_ref[rs, :], dw_ref[rs, :], dqd_ref[rs, :], dkt_ref[rs, :], din_ref[rs, :],
                                              ddec_ref[ci]) for ci, rs in enumerate(rows)])
        for ci, rs in enumerate(rows):
            dq_ref[rs, :] = dqs[ci]
            dk_ref[rs, :] = dks[ci]
            dv_ref[rs, :] = dvs[ci]
            dab_sc[rs, :] += dabs[ci]
        dal_ref[...] += dal
        ddt_ref[...] += ddt

        @pl.when(h == H - 1)
        def _():
            dab_ref[...] = dab_sc[...].astype(BF16)

    hs = pl.BlockSpec((tr, HEAD_DIM), lambda i, h: (i, h))
    hshape = jax.ShapeDtypeStruct((S, H * HEAD_DIM), F32)
    row = pl.BlockSpec((1, LANES), lambda i, h: (0, 0))
    rshape = jax.ShapeDtypeStruct((1, LANES), F32)
    return pl.pallas_call(
        body, name=name, grid=(S // tr, H),
        in_specs=_prep_specs(tr, ab_col) + _prep_out_specs(tr),
        out_specs=[hs, hs, hs, pl.BlockSpec((tr, LANES), lambda i, h: (i, 0)), row, row],
        out_shape=[hshape, hshape, hshape, jax.ShapeDtypeStruct((S, LANES), BF16), rshape, rshape],
        scratch_shapes=[pltpu.VMEM((tr, LANES), F32)],
        compiler_params=_cparams(("arbitrary", "arbitrary")),
    )(dqkv, dqkv, dqkv, proj, alog_row, dtb_row, *cots, t_inv)


def _scan_steps(states, us, ws, qds, kts, intras, decs, diff=False):
    nn, tn = (_mm_nn, _mm_tn) if diff else (lambda p, r: _bdot(p, r, _NN), lambda p, r: _bdot(p, r, _TN))
    hs = range(len(states))
    v_new = [us[h] - nn(ws[h], states[h]) for h in hs]
    o_state = [nn(qds[h], states[h]) for h in hs]
    o_intra = [nn(intras[h], v_new[h]) for h in hs]
    grown = [tn(kts[h], v_new[h]) for h in hs]
    return [o_state[h] + o_intra[h] for h in hs], [states[h] * decs[h] + grown[h] for h in hs]


def _scan_specs(rev, n):
    H = N_DELTA_HEADS

    def cix(i):
        return (n - 1 - i) if rev else i

    row = pl.BlockSpec((CHUNK, H * HEAD_DIM), lambda i: (cix(i), 0))
    return row, pl.BlockSpec((H, CHUNK, CHUNK), lambda i: (0, cix(i), 0)), \
        pl.BlockSpec((H, 1, 1, HEAD_DIM), lambda i: (0, cix(i), 0, 0)), \
        pl.BlockSpec((1, H, HEAD_DIM, HEAD_DIM), lambda i: (cix(i), 0, 0, 0))


def _delta_scan_fwd(u, w, qd, kt, intra, dec, name):
    S = u.shape[0]
    H = N_DELTA_HEADS
    n = S // CHUNK
    row, ispec, dspec, sspec = _scan_specs(False, n)

    def body(u_ref, w_ref, qd_ref, kt_ref, in_ref, dec_ref, o_ref, st_ref, s_sc):
        @pl.when(pl.program_id(0) == 0)
        def _():
            s_sc[...] = jnp.zeros_like(s_sc)

        cols = [pl.ds(h * HEAD_DIM, HEAD_DIM) for h in range(H)]
        states = [s_sc[h] for h in range(H)]
        outs, new = _scan_steps(states, [u_ref[:, cs] for cs in cols], [w_ref[:, cs] for cs in cols],
                                [qd_ref[:, cs] for cs in cols], [kt_ref[:, cs] for cs in cols],
                                [in_ref[h] for h in range(H)], [dec_ref[h, 0] for h in range(H)])
        for h, cs in enumerate(cols):
            st_ref[0, h] = states[h]
            o_ref[:, cs] = outs[h]
            s_sc[h] = new[h]

    return pl.pallas_call(
        body, name=name, grid=(n,),
        in_specs=[row, row, row, row, ispec, dspec], out_specs=[row, sspec],
        out_shape=[jax.ShapeDtypeStruct((S, H * HEAD_DIM), F32), jax.ShapeDtypeStruct((n, H, HEAD_DIM, HEAD_DIM), F32)],
        scratch_shapes=[pltpu.VMEM((H, HEAD_DIM, HEAD_DIM), F32)],
        compiler_params=_cparams(("arbitrary",)),
    )(u, w, qd, kt, intra, dec)


def _delta_scan_bwd(u, w, qd, kt, intra, dec, states, do, name):
    S = u.shape[0]
    H = N_DELTA_HEADS
    n = S // CHUNK
    row, ispec, dspec, sspec = _scan_specs(True, n)

    def body(u_ref, w_ref, qd_ref, kt_ref, in_ref, dec_ref, st_ref, do_ref,
             du_ref, dw_ref, dqd_ref, dkt_ref, din_ref, ddec_ref, ds_sc):
        @pl.when(pl.program_id(0) == 0)
        def _():
            ds_sc[...] = jnp.zeros_like(ds_sc)

        cols = [pl.ds(h * HEAD_DIM, HEAD_DIM) for h in range(H)]
        _, vjp = jax.vjp(functools.partial(_scan_steps, diff=True), [st_ref[0, h] for h in range(H)], [u_ref[:, cs] for cs in cols],
                         [w_ref[:, cs] for cs in cols], [qd_ref[:, cs] for cs in cols], [kt_ref[:, cs] for cs in cols],
                         [in_ref[h] for h in range(H)], [dec_ref[h, 0] for h in range(H)])
        dstate, du, dw, dqd, dkt, din, ddec = vjp(([do_ref[:, cs] for cs in cols], [ds_sc[h] for h in range(H)]))
        for h, cs in enumerate(cols):
            du_ref[:, cs] = du[h]
            dw_ref[:, cs] = dw[h]
            dqd_ref[:, cs] = dqd[h]
            dkt_ref[:, cs] = dkt[h]
            din_ref[h] = din[h]
            ddec_ref[h, 0] = ddec[h]
            ds_sc[h] = dstate[h]

    hshape = jax.ShapeDtypeStruct((S, H * HEAD_DIM), F32)
    return pl.pallas_call(
        body, name=name, grid=(n,),
        in_specs=[row, row, row, row, ispec, dspec, sspec, row],
        out_specs=[row, row, row, row, ispec, dspec],
        out_shape=[hshape, hshape, hshape, hshape, jax.ShapeDtypeStruct((H, S, CHUNK), F32),
                   jax.ShapeDtypeStruct((H, n, 1, HEAD_DIM), F32)],
        scratch_shapes=[pltpu.VMEM((H, HEAD_DIM, HEAD_DIM), F32)],
        compiler_params=_cparams(("arbitrary",)),
    )(u, w, qd, kt, intra, dec, states, do)


def _gated_norm(od, z, gain):
    return _rms(od, gain) * _silu(z)


def _post_fwd(od, proj, z_col0, gain, name):
    S = od.shape[0]
    H = N_DELTA_HEADS
    tr = _pick(S, 512)

    def body(od_ref, z_ref, g_ref, o_ref):
        o_ref[...] = _gated_norm(od_ref[...], z_ref[...], g_ref[...]).astype(BF16)

    return pl.pallas_call(
        body, name=name, grid=(S // tr, H),
        in_specs=[_head_spec(tr, 0), _head_spec(tr, z_col0), _gain_spec()],
        out_specs=_head_spec(tr, 0), out_shape=jax.ShapeDtypeStruct((S, H * HEAD_DIM), BF16),
        compiler_params=_cparams(("parallel", "parallel")),
    )(od, proj, gain)


def _post_bwd(od, proj, z_col0, gain, do, do_col0, name):
    S = od.shape[0]
    H = N_DELTA_HEADS
    tr = _pick(S, 512)

    def body(od_ref, z_ref, g_ref, do_ref, dod_ref, dz_ref, dg_ref):
        @pl.when((pl.program_id(0) == 0) & (pl.program_id(1) == 0))
        def _():
            dg_ref[...] = jnp.zeros_like(dg_ref)

        _, vjp = jax.vjp(_gated_norm, od_ref[...], z_ref[...], g_ref[...])
        dod, dz, dg = vjp(do_ref[...])
        dod_ref[...] = dod
        dz_ref[...] = dz.astype(BF16)
        dg_ref[...] += dg

    return pl.pallas_call(
        body, name=name, grid=(S // tr, H),
        in_specs=[_head_spec(tr, 0), _head_spec(tr, z_col0), _gain_spec(), _head_spec(tr, do_col0)],
        out_specs=[_head_spec(tr, 0), _head_spec(tr, 0), _gain_spec()],
        out_shape=[jax.ShapeDtypeStruct((S, H * HEAD_DIM), F32), jax.ShapeDtypeStruct((S, H * HEAD_DIM), BF16),
                   jax.ShapeDtypeStruct((1, HEAD_DIM), F32)],
        compiler_params=_cparams(("arbitrary", "arbitrary")),
    )(od, proj, gain, do)


def _adam_math(w, g, m, v):
    m = ADAM_B1 * m + (1.0 - ADAM_B1) * g
    v = ADAM_B2 * v + (1.0 - ADAM_B2) * (g * g)
    m_hat = m / (1.0 - ADAM_B1 ** ADAM_STEP)
    v_hat = v / (1.0 - ADAM_B2 ** ADAM_STEP)
    delta = -ADAM_LR * (m_hat / (jnp.sqrt(v_hat) + ADAM_EPS) + ADAM_WD * w)
    return delta, m, v


def _adamw(w, g, m, v, name):
    R, C = w.shape
    tr = R if R * C * 4 <= (1 << 20) else _pick8(R, max(8, (1 << 20) // (C * 4)))

    def body(w_ref, g_ref, m_ref, v_ref, d_ref, nm_ref, nv_ref):
        d, nm, nv = _adam_math(w_ref[...], g_ref[...], m_ref[...], v_ref[...])
        d_ref[...] = d
        nm_ref[...] = nm
        nv_ref[...] = nv

    spec = pl.BlockSpec((tr, C), lambda i: (i, 0))
    shp = jax.ShapeDtypeStruct((R, C), F32)
    return pl.pallas_call(
        body, name=name, grid=(R // tr,), in_specs=[spec] * 4, out_specs=[spec] * 3, out_shape=[shp] * 3,
        compiler_params=_cparams(("parallel",)),
    )(w, g, m, v)


def _adamw_halves(w, mine, theirs, col, m, v, name):
    _, R, C = w.shape
    tr = _pick8(R // 2, max(8, (1 << 20) // (C * 4)))
    nb2 = (R // 2) // tr

    def body(w_ref, a_ref, b_ref, m_ref, v_ref, g_ref, d_ref, nm_ref, nv_ref):
        top = pl.program_id(0) < nb2
        g = jnp.where(top == (lax.axis_index("c") == 0), a_ref[...], b_ref[...])
        d, nm, nv = _adam_math(w_ref[...], g, m_ref[...], v_ref[...])
        g_ref[...] = g
        d_ref[...] = d
        nm_ref[...] = nm
        nv_ref[...] = nv

    spec = pl.BlockSpec((None, tr, C), lambda i: (0, i, 0))
    half = pl.BlockSpec((tr, C), lambda i: (i % nb2, col))
    shp = jax.ShapeDtypeStruct((1, R, C), F32)
    return pl.pallas_call(
        body, name=name, grid=(2 * nb2,), in_specs=[spec, half, half, spec, spec], out_specs=[spec] * 4, out_shape=[shp] * 4,
        compiler_params=_cparams(("parallel",)),
    )(w, mine, theirs, m, v)


def _adamw_halves_t(w, mine, theirs, m, v, name):
    _, R, C = w.shape
    tc = _pick(R // 2, 256)
    nb2 = (R // 2) // tc

    def body(w_ref, a_ref, b_ref, m_ref, v_ref, g_ref, d_ref, nm_ref, nv_ref):
        left = pl.program_id(0) < nb2
        g = jnp.where(left == (lax.axis_index("c") == 0), a_ref[...], b_ref[...])
        d, nm, nv = _adam_math(w_ref[...], g, m_ref[...], v_ref[...])
        g_ref[...] = g
        d_ref[...] = d
        nm_ref[...] = nm
        nv_ref[...] = nv

    spec = pl.BlockSpec((C, tc), lambda j: (0, j))
    half = pl.BlockSpec((C, tc), lambda j: (0, j % nb2))
    shp = jax.ShapeDtypeStruct((C, R), F32)
    outs = pl.pallas_call(
        body, name=name, grid=(2 * nb2,), in_specs=[spec, half, half, spec, spec], out_specs=[spec] * 4, out_shape=[shp] * 4,
        compiler_params=_cparams(("parallel",)),
    )(jnp.transpose(w[0]), jnp.transpose(mine), jnp.transpose(theirs), jnp.transpose(m[0]), jnp.transpose(v[0]))
    return [jnp.transpose(o)[None] for o in outs]


def _pick8(dim, pref):
    t = (min(dim, pref) // 8) * 8
    while t >= 8:
        if dim % t == 0:
            return t
        t -= 8
    return dim


def _adamw_outer(w, m, v, cond_t, rhs, name):
    R, C = w.shape
    tr = _pick8(R, 128)
    nb = cond_t.shape[1]
    lhs_t = cond_t

    def body(w_ref, m_ref, v_ref, a_ref, b_ref, g_ref, d_ref, nm_ref, nv_ref):
        g = _dot(_silu(a_ref[...]), b_ref[...])
        d, nm, nv = _adam_math(w_ref[...], g, m_ref[...], v_ref[...])
        g_ref[...] = g
        d_ref[...] = d
        nm_ref[...] = nm
        nv_ref[...] = nv

    spec = pl.BlockSpec((tr, C), lambda i: (i, 0))
    shp = jax.ShapeDtypeStruct((R, C), F32)
    return pl.pallas_call(
        body, name=name, grid=(R // tr,),
        in_specs=[spec, spec, spec, pl.BlockSpec((tr, nb), lambda i: (i, 0)), pl.BlockSpec((nb, C), lambda i: (0, 0))],
        out_specs=[spec] * 4, out_shape=[shp] * 4,
        compiler_params=_cparams(("parallel",)),
    )(w, m, v, lhs_t, rhs)


def _ada_fwd(cond, w, bias, name):
    a = cond
    nb, K = a.shape
    N = w.shape[1]
    tn = _pick(N, 512)

    def body(a_ref, w_ref, b_ref, o_ref):
        o_ref[...] = _dot(_silu(a_ref[...]), w_ref[...]) + b_ref[...]

    return pl.pallas_call(
        body, name=name, grid=(N // tn,),
        in_specs=[pl.BlockSpec((nb, K), lambda j: (0, 0)), pl.BlockSpec((K, tn), lambda j: (0, j)), pl.BlockSpec((1, tn), lambda j: (0, j))],
        out_specs=pl.BlockSpec((nb, tn), lambda j: (0, j)), out_shape=jax.ShapeDtypeStruct((nb, N), F32),
        compiler_params=_cparams(("parallel",)),
    )(a, w, bias)


def _add_cast(parts, out_dtypes, name):
    shape = parts[0].shape
    G, R, C = shape
    tr = _pick8(R, max(8, (1 << 20) // (C * 4)))
    n_in = len(parts)

    def body(*refs):
        acc = refs[0][...].astype(F32)
        for r in refs[1:n_in]:
            acc = acc + r[...].astype(F32)
        for o, dt in zip(refs[n_in:], out_dtypes):
            o[...] = acc.astype(dt)

    spec = pl.BlockSpec((1, tr, C), lambda g, i: (g, i, 0))
    outs = pl.pallas_call(
        body, name=name, grid=(G, R // tr), in_specs=[spec] * n_in, out_specs=[spec] * len(out_dtypes),
        out_shape=[jax.ShapeDtypeStruct(shape, dt) for dt in out_dtypes],
        compiler_params=_cparams(("parallel", "parallel")),
    )(*parts)
    return outs


def _me():
    return lax.axis_index("x"), lax.axis_index("y"), lax.axis_index("c")


def _xor_peer(k):
    x, y, c = _me()
    dx, dy, dc = (k >> 2) & 1, (k >> 1) & 1, k & 1
    return (x ^ dx if dx else x, y ^ dy if dy else y, c ^ dc if dc else c)


ANY = pl.BlockSpec(memory_space=pl.ANY)


def _all_gather_small(v, name, after=None):
    R, C = v.shape
    extra = [] if after is None else [after]

    def body(v_ref, *rest):
        out_ref, send_sems, recv_sems = rest[len(extra):]
        x, y, c = _me()
        mine = 4 * x + 2 * y + c
        out_ref[mine] = v_ref[...]
        copies = []
        for k in range(1, 8):
            cp = pltpu.make_async_remote_copy(src_ref=v_ref, dst_ref=out_ref.at[mine], send_sem=send_sems.at[k - 1],
                                              recv_sem=recv_sems.at[k - 1], device_id=_xor_peer(k), device_id_type=MESH)
            cp.start()
            copies.append(cp)
        for k in range(1, 8):
            px, py, pc = _xor_peer(k)
            pltpu.make_async_remote_copy(src_ref=v_ref, dst_ref=out_ref.at[4 * px + 2 * py + pc], send_sem=send_sems.at[k - 1],
                                         recv_sem=recv_sems.at[k - 1], device_id=_xor_peer(k), device_id_type=MESH).wait_recv()
        for cp in copies:
            cp.wait_send()

    return pl.pallas_call(
        body, name=name, out_shape=jax.ShapeDtypeStruct((8, R, C), F32),
        in_specs=[pl.BlockSpec(memory_space=pltpu.VMEM)] + [ANY] * len(extra), out_specs=pl.BlockSpec(memory_space=pltpu.VMEM),
        scratch_shapes=[pltpu.SemaphoreType.DMA((7,)), pltpu.SemaphoreType.DMA((7,))],
        compiler_params=pltpu.CompilerParams(vmem_limit_bytes=VMEM_LIMIT),
    )(v, *extra)


def _chip_peers():
    x, y, _ = _me()
    return [(1, (x, 1 - y)), (2, (1 - x, y)), (3, (1 - x, 1 - y))]


def _all_gather_shards(shards, name):
    n = len(shards)

    def body(*refs):
        ins, outs = refs[:n], refs[n:2 * n]
        send_sems, recv_sems = refs[2 * n:]
        x, y, c = _me()
        chip = 2 * x + y
        sib = (x, y, 1 - c)
        peers = _chip_peers()
        sends = []
        for t in range(n):
            half = ins[t].shape[0] // 2
            mine = pl.ds(c * half, half)
            for p, (k, (px, py)) in enumerate(peers):
                cp = pltpu.make_async_remote_copy(src_ref=ins[t].at[mine], dst_ref=outs[t].at[chip, mine],
                                                  send_sem=send_sems.at[6 * t + p], recv_sem=recv_sems.at[6 * t + p],
                                                  device_id=(px, py, c), device_id_type=MESH)
                cp.start()
                sends.append(cp)
        for t in range(n):
            half = ins[t].shape[0] // 2
            mine = pl.ds(c * half, half)
            for p, (k, (px, py)) in enumerate(peers):
                src_chip = 2 * px + py
                landed = outs[t].at[src_chip, mine]
                pltpu.make_async_remote_copy(src_ref=landed, dst_ref=landed, send_sem=send_sems.at[6 * t + p],
                                             recv_sem=recv_sems.at[6 * t + p], device_id=(px, py, c), device_id_type=MESH).wait_recv()
                fw = pltpu.make_async_remote_copy(src_ref=landed, dst_ref=landed, send_sem=send_sems.at[6 * t + 3 + p],
                                                  recv_sem=recv_sems.at[6 * t + 3 + p], device_id=sib, device_id_type=MESH)
                fw.start()
                sends.append(fw)
        for t in range(n):
            half = ins[t].shape[0] // 2
            theirs = pl.ds((1 - c) * half, half)
            for p, (k, (px, py)) in enumerate(peers):
                got = outs[t].at[2 * px + py, theirs]
                pltpu.make_async_remote_copy(src_ref=got, dst_ref=got, send_sem=send_sems.at[6 * t + 3 + p],
                                             recv_sem=recv_sems.at[6 * t + 3 + p], device_id=sib, device_id_type=MESH).wait_recv()
        for cp in sends:
            cp.wait_send()

    return pl.pallas_call(
        body, name=name,
        out_shape=[jax.ShapeDtypeStruct((4,) + s.shape, s.dtype) for s in shards],
        in_specs=[ANY] * n, out_specs=[ANY] * n,
        scratch_shapes=[pltpu.SemaphoreType.DMA((6 * n,)), pltpu.SemaphoreType.DMA((6 * n,))],
    )(*shards)


def _swap_halves_with_sibling(slabs, name):
    n = len(slabs)

    def body(*refs):
        ins, outs = refs[:n], refs[n:2 * n]
        send_sems, recv_sems = refs[2 * n:]
        x, y, c = _me()
        sib = (x, y, 1 - c)
        cps = []
        for t in range(n):
            half = ins[t].shape[1] // 2
            cp = pltpu.make_async_remote_copy(src_ref=ins[t].at[:, pl.ds((1 - c) * half, half)], dst_ref=outs[t],
                                              send_sem=send_sems.at[t], recv_sem=recv_sems.at[t], device_id=sib, device_id_type=MESH)
            cp.start()
            cps.append(cp)
        for cp in cps:
            cp.wait()

    return pl.pallas_call(
        body, name=name,
        out_shape=[jax.ShapeDtypeStruct((4, s.shape[1] // 2, s.shape[2]), s.dtype) for s in slabs],
        in_specs=[ANY] * n, out_specs=[ANY] * n,
        scratch_shapes=[pltpu.SemaphoreType.DMA((n,)), pltpu.SemaphoreType.DMA((n,))],
    )(*slabs)


def _send_halves_to_sibling(halves, name):
    n = len(halves)

    def body(*refs):
        ins, outs = refs[:n], refs[n:2 * n]
        send_sems, recv_sems = refs[2 * n:]
        x, y, c = _me()
        sib = (x, y, 1 - c)
        cps = []
        for t in range(n):
            cp = pltpu.make_async_remote_copy(src_ref=ins[t], dst_ref=outs[t], send_sem=send_sems.at[t],
                                              recv_sem=recv_sems.at[t], device_id=sib, device_id_type=MESH)
            cp.start()
            cps.append(cp)
        for cp in cps:
            cp.wait()

    return pl.pallas_call(
        body, name=name,
        out_shape=[jax.ShapeDtypeStruct(s.shape, s.dtype) for s in halves],
        in_specs=[ANY] * n, out_specs=[ANY] * n,
        scratch_shapes=[pltpu.SemaphoreType.DMA((n,)), pltpu.SemaphoreType.DMA((n,))],
    )(*halves)


HBM_SPEC = pl.BlockSpec(memory_space=pltpu.HBM)
SEM_SPEC = pl.BlockSpec(memory_space=pltpu.SEMAPHORE)
DATAFLOW = pltpu.SideEffectType.DATAFLOW_SIDE_EFFECTING


def _plan_gather_direct(src_refs, land_refs):
    x, y, c = _me()
    chip = 2 * x + y
    plan = []
    for s, land in zip(src_refs, land_refs):
        half = s.shape[0] // 2
        for _, (px, py) in _chip_peers():
            for pc in (c, 1 - c):
                plan.append((s.at[pl.ds(c * half, half)], land.at[chip, pl.ds(c * half, half)],
                             land.at[2 * px + py, pl.ds(pc * half, half)], (px, py, pc)))
    return plan


def _plan_scatter_direct(src_refs, land_refs):
    x, y, c = _me()
    plan = []
    for s, land in zip(src_refs, land_refs):
        half = s.shape[1] // 2
        for k in range(1, 8):
            px, py, pc = _xor_peer(k)
            plan.append((s.at[2 * px + py, pl.ds(pc * half, half)], land.at[4 * x + 2 * y + c],
                         land.at[4 * px + 2 * py + pc], (px, py, pc)))
    return plan


def _plan_exchange_chips(src_refs, land_refs):
    x, y, c = _me()
    chip = 2 * x + y
    plan = []
    for s, land in zip(src_refs, land_refs):
        for _, (px, py) in _chip_peers():
            plan.append((s.at[2 * px + py], land.at[chip], land.at[2 * px + py], (px, py, c)))
    return plan


_plan_gather_direct.per_tensor = 6
_plan_scatter_direct.per_tensor = 7
_plan_exchange_chips.per_tensor = 3


def _start_copies(srcs, lands, plan_fn, name, after=None):
    n = len(srcs)
    n_copies = len(srcs) * plan_fn.per_tensor
    extra = [] if after is None else [after]

    def body(*refs):
        refs = refs[:2 * n] + refs[2 * n + len(extra):]
        send_sems, recv_sems, token = refs[2 * n], refs[2 * n + 1], refs[-1]
        for i, (src, dst, _, peer) in enumerate(plan_fn(refs[:n], refs[n:2 * n])):
            pltpu.make_async_remote_copy(src_ref=src, dst_ref=dst, send_sem=send_sems.at[i], recv_sem=recv_sems.at[i],
                                         device_id=peer, device_id_type=MESH).start()
        token[...] = jnp.zeros_like(token)

    arrays = list(srcs) + list(lands)
    outs = pl.pallas_call(
        body, name=name,
        out_shape=(pltpu.SemaphoreType.DMA((n_copies,)), pltpu.SemaphoreType.DMA((n_copies,)),
                   *[pltpu.HBM(a.shape, a.dtype) for a in arrays], jax.ShapeDtypeStruct((8, LANES), F32)),
        in_specs=[HBM_SPEC] * (2 * n) + [ANY] * len(extra),
        out_specs=(SEM_SPEC, SEM_SPEC, *[HBM_SPEC] * (2 * n), pl.BlockSpec(memory_space=pltpu.VMEM)),
        input_output_aliases={i: 2 + i for i in range(2 * n)},
        compiler_params=pltpu.CompilerParams(has_side_effects=DATAFLOW),
    )(*[pltpu.with_memory_space_constraint(a, pltpu.HBM) for a in arrays], *extra)
    return outs[0], outs[1], list(outs[2:2 + n]), list(outs[2 + n:2 + 2 * n]), outs[-1]


def _wait_copies(send_sems, recv_sems, srcs, lands, after, plan_fn, name):
    n = len(srcs)

    def body(*refs):
        send_sems, recv_sems = refs[2 * n], refs[2 * n + 1]
        for i, (src, _, arrival, peer) in enumerate(plan_fn(refs[:n], refs[n:2 * n])):
            cp = pltpu.make_async_remote_copy(src_ref=src, dst_ref=arrival, send_sem=send_sems.at[i], recv_sem=recv_sems.at[i],
                                              device_id=peer, device_id_type=MESH)
            cp.wait_send()
            cp.wait_recv()

    arrays = list(srcs) + list(lands)
    outs = pl.pallas_call(
        body, name=name,
        out_shape=tuple(pltpu.HBM(a.shape, a.dtype) for a in arrays),
        in_specs=[HBM_SPEC] * (2 * n) + [SEM_SPEC, SEM_SPEC, ANY],
        out_specs=tuple([HBM_SPEC] * (2 * n)),
        input_output_aliases={i: i for i in range(2 * n)},
        compiler_params=pltpu.CompilerParams(has_side_effects=DATAFLOW),
    )(*arrays, send_sems, recv_sems, after)
    return list(outs[n:])


def _rope_tables(positions):
    half = ROPE_DIM // 2
    S = positions.shape[0]
    inv_freq = ROPE_THETA ** (-jnp.arange(half, dtype=F32) / half)
    ang = positions.astype(F32)[:, None] * inv_freq
    cos, sin = jnp.cos(ang), jnp.sin(ang)
    zeros = functools.partial(jnp.zeros, dtype=F32)
    cc = jnp.concatenate([cos, cos, jnp.ones((S, HEAD_DIM - ROPE_DIM), F32)], axis=1)
    s1 = jnp.concatenate([-sin, zeros((S, HEAD_DIM - half))], axis=1)
    s2 = jnp.concatenate([zeros((S, half)), sin, zeros((S, HEAD_DIM - ROPE_DIM))], axis=1)
    return cc, s1, s2


def _ffn_fwd(x, gain, shift, scale, gate, w_gu, w_d, fs, tag):
    S = x.shape[0]
    tm = _pick(S, 512)
    h = _pre_fwd(x, gain, shift, scale, tag + "_pre")

    def swiglu_out(tile, ins, outs):
        outs[0][...] = tile
        outs[1][...] = _swiglu_fn(tile[:, :fs], tile[:, fs:]).astype(BF16)

    ab, s = _matmul(h, w_gu, "nn", F32, tag + "_gate_up", tm=tm, tn=2 * fs, tk=4096, epilogue=(swiglu_out, [], [
        ((S, 8 * fs), F32, (tm, 2 * fs), lambda i, j: (i, j)), ((S, 4 * fs), BF16, (tm, fs), lambda i, j: (i, j))]))
    f = _matmul(s, w_d, "nn", F32, tag + "_down", tm=512, tn=1024, tk=8192)
    xn = _residual_fwd(x, gate, f, 0.5, tag + "_res")
    return xn, (x, h, ab, s, f)


def _ffn_bwd(dxn, saved, gain, shift, scale, gate, w_gu, w_d, fs, tag, on_dw_d=None, on_dw_gu=None, dw_gu_dtype=F32):
    x, h, ab, s, f = saved
    df, dgate = _residual_bwd(gate, f, dxn, 0.5, tag + "_res_bwd")
    S = x.shape[0]
    tm = _pick(S, 512)
    dw_d = _matmul(s, df, "tn", BF16, tag + "_down_dw", tm=1408, tn=1024, tk=2048)
    if on_dw_d is not None:
        shift = shift + on_dw_d(dw_d)

    def swiglu_back(tile, ins, outs):
        _, vjp = jax.vjp(_swiglu_fn, ins[0][:, :fs], ins[0][:, fs:])
        da, db = vjp(tile)
        outs[0][:, :fs] = da.astype(BF16)
        outs[0][:, fs:] = db.astype(BF16)

    (dab,) = _matmul(df, w_d, "nt", F32, tag + "_down_dx", tm=tm, tn=fs, tk=4096, epilogue=(
        swiglu_back, [(ab, (tm, 2 * fs), lambda i, j: (i, j))], [((S, 8 * fs), BF16, (tm, 2 * fs), lambda i, j: (i, j))]))
    dw_gu = _matmul(h, dab, "tn", dw_gu_dtype, tag + "_gate_up_dw", tm=512, tn=2 * fs, tk=2048, col_slabs=True)
    tie = on_dw_gu(dw_gu) if on_dw_gu is not None else None
    dh = _matmul(dab, w_gu, "nt", F32, tag + "_gate_up_dx", tm=1024, tn=1024, tk=2816, after=tie)
    dx, dgain, dshift, dscale = _pre_bwd(x, gain, shift, scale, dh, dxn, tag + "_pre_bwd")
    return dx, dw_gu, dw_d, dgain, dshift, dscale, dgate


def _flat_pad(parts, rows, cols):
    flat = jnp.concatenate([p.reshape(-1).astype(F32) for p in parts])
    return jnp.pad(flat, (0, rows * cols - flat.shape[0])).reshape(rows, cols)


def _cols_to_slabs(w, n):
    R, NC = w.shape
    return jnp.transpose(w.reshape(R, n, NC // n), (1, 0, 2))


def kernel(x, c, positions, w_ada, b_ada, ffn1_norm, ffn1_w_gate, ffn1_w_up, ffn1_w_down, mix_norm, w_in, conv_w, q_norm, k_norm, a_log, dt_bias, delta_out_norm, w_out, ffn2_norm, ffn2_w_gate, ffn2_w_up, ffn2_w_down, loss_target, m_w_ada, m_b_ada, m_ffn1_norm, m_ffn1_w_gate, m_ffn1_w_up, m_ffn1_w_down, m_mix_norm, m_w_in, m_conv_w, m_q_norm, m_k_norm, m_a_log, m_dt_bias, m_delta_out_norm, m_w_out, m_ffn2_norm, m_ffn2_w_gate, m_ffn2_w_up, m_ffn2_w_down, v_w_ada, v_b_ada, v_ffn1_norm, v_ffn1_w_gate, v_ffn1_w_up, v_ffn1_w_down, v_mix_norm, v_w_in, v_conv_w, v_q_norm, v_k_norm, v_a_log, v_dt_bias, v_delta_out_norm, v_w_out, v_ffn2_norm, v_ffn2_w_gate, v_ffn2_w_up, v_ffn2_w_down):
    xi, yi, ci = _me()
    chip = 2 * xi + yi
    dev = 2 * chip + ci
    xs = x[0]
    S, D = xs.shape
    HA, HD = N_ATTN_HEADS, N_DELTA_HEADS
    fs = ffn1_w_gate.shape[2]
    n_mod_shard = w_ada.shape[2]
    in_shard = w_in.shape[2]
    in_width = 4 * in_shard
    in_pad = -(-in_width // LANES) * LANES
    conv_shard = conv_w.shape[2]
    conv_width = 4 * conv_shard

    pack0 = jnp.zeros((8, max(D, conv_shard)), F32)
    pack0 = pack0.at[0, :D].set(c[0]).at[1:1 + CONV_WIDTH, :conv_shard].set(conv_w[0])
    got0 = _all_gather_small(pack0, "gather_cond")
    c_all = got0[:, 0, :D]
    conv_full = jnp.transpose(got0[::2, 1:1 + CONV_WIDTH, :conv_shard], (1, 0, 2)).reshape(CONV_WIDTH, conv_width)
    b_ada_mine = lax.dynamic_slice(b_ada, (0, chip * n_mod_shard), (1, n_mod_shard))
    mod_part = _ada_fwd(c_all, w_ada[0], b_ada_mine, "ada_fwd")
    got1 = _all_gather_small(mod_part, "gather_mod")
    mod = lax.dynamic_index_in_dim(got1[::2], dev, axis=1, keepdims=False).reshape(1, 4 * n_mod_shard)
    sh1, sc1, gt1, sh2, sc2, gt2, sh3, sc3, gt3 = [mod[:, i * D:(i + 1) * D] for i in range(N_MOD)]

    shards = [w[0].astype(BF16) for w in (ffn1_w_gate, ffn1_w_up, ffn1_w_down, w_in, w_out, ffn2_w_gate, ffn2_w_up, ffn2_w_down)]
    gathered = _all_gather_shards(shards[:3], "gather_weights")
    g1g, g1u, g1d = [lax.dynamic_update_index_in_dim(g, s, chip, 0) for g, s in zip(gathered, shards[:3])]
    zones = [lax.dynamic_update_index_in_dim(lax.empty((4,) + s.shape, BF16), s, chip, 0) for s in shards[3:]]
    ag_in = _start_copies(shards[3:4], zones[:1], _plan_gather_direct, "gather_in_start")
    sh1 = sh1 + ag_in[4][0, 0]

    def gate_up(gg, gu):
        return jnp.transpose(jnp.concatenate([gg, gu], axis=2), (1, 0, 2)).reshape(D, 8 * fs)

    w_gu1, w_d1 = gate_up(g1g, g1u), g1d.reshape(4 * fs, D)

    x1, saved1 = _ffn_fwd(xs, ffn1_norm, sh1, sc1, gt1, w_gu1, w_d1, fs, "ffn1")
    (gin,) = _wait_copies(ag_in[0], ag_in[1], ag_in[2], ag_in[3], x1, _plan_gather_direct, "gather_in_wait")
    w_in_f = jnp.pad(jnp.transpose(gin, (1, 0, 2)).reshape(D, in_width), ((0, 0), (0, in_pad - in_width)))
    ag_rest = _start_copies(shards[4:], zones[1:], _plan_gather_direct, "gather_rest_start", after=gin)
    sh2 = sh2 + ag_rest[4][0, 0]

    cc, s1, s2 = _rope_tables(positions[0])
    alog_row = jnp.pad(a_log, ((0, 0), (0, LANES - HD)))
    dtb_row = jnp.pad(dt_bias, ((0, 0), (0, LANES - HD)))
    col_k, col_v, col_d, col_z, col_ab = HA, 2 * HA, 3 * HA, 3 * HA + 3 * HD, 3 * HA + 4 * HD
    h2 = _pre_fwd(x1, mix_norm, sh2, sc2, "mix_pre")
    proj = _matmul(h2, w_in_f, "nn", F32, "mix_in_proj", tm=512, tn=2432, tk=4096)
    attn_bias = _log_multiplicity_table(_pick(S, ATTN_TILE))
    qa, ka = _attn_prep_fwd(proj, q_norm, k_norm, cc, s1, s2, "attn_prep")
    o_near, lse_near = _attn_fwd(qa, ka, proj, col_v, attn_bias, "attn_fwd")
    oa, oa_b, lse = _attn_far_fwd(qa, ka, proj, col_v, o_near, lse_near, "attn_far_fwd")
    dqkv = _conv_fwd(proj, col_d, conv_width, conv_full, "conv_fwd")
    *prep, t_inv = _delta_prep_fwd(dqkv, proj, col_ab, alog_row, dtb_row, "delta_prep")
    od_raw, states = _delta_scan_fwd(*prep, "delta_scan")
    od = _post_fwd(od_raw, proj, col_z, delta_out_norm, "delta_post")
    o = jnp.concatenate([oa_b, od], axis=1)
    gout, g2g, g2u, g2d = _wait_copies(ag_rest[0], ag_rest[1], ag_rest[2], ag_rest[3], o, _plan_gather_direct, "gather_rest_wait")
    w_out_f = gout.reshape(-1, D)
    w_gu2, w_d2 = gate_up(g2g, g2u), g2d.reshape(4 * fs, D)
    mo = _matmul(o, w_out_f, "nn", F32, "mix_out_proj", tm=1024, tn=1024, tk=4096)
    x2 = _residual_fwd(x1, gt2, mo, 1.0, "mix_res")

    x3, saved3 = _ffn_fwd(x2, ffn2_norm, sh3, sc3, gt3, w_gu2, w_d2, fs, "ffn2")
    loss_part, dy = _loss_head(x3, loss_target[0], "loss_head")
    loss = lax.psum(loss_part[0, 0], ("x", "y", "c"))

    dx2, dw_gu2, dw_d2, dgain3, dsh3, dsc3, dgt3 = _ffn_bwd(dy, saved3, ffn2_norm, sh3, sc3, gt3, w_gu2, w_d2, fs, "ffn2",
                                                            dw_gu_dtype=BF16)

    def scatter_start(slabs32, name):
        slabs16 = [s.astype(BF16) for s in slabs32]
        zones = []
        for s in slabs16:
            half = s.shape[1] // 2
            own = lax.dynamic_slice(s, (chip, ci * half, 0), (1, half, s.shape[2]))
            zones.append(lax.dynamic_update_slice(lax.empty((8, half, s.shape[2]), BF16), own, (dev, 0, 0)))
        return _start_copies(slabs16, zones, _plan_scatter_direct, name)

    rs_ffn2 = scatter_start([dw_gu2, dw_d2.reshape(4, fs, D)], "rs_ffn2_start")
    dmo, dgt2 = _residual_bwd(gt2 + rs_ffn2[4][0, 0], mo, dx2, 1.0, "mix_res_bwd")
    do = _matmul(dmo, w_out_f, "nt", F32, "mix_out_dx", tm=1024, tn=1024, tk=4096)
    dw_out = _matmul(o, dmo, "tn", BF16, "mix_out_dw", tm=1024, tn=1024, tk=2048)
    dq = _attn_bwd_dq(qa, ka, proj, col_v, oa, lse, do, 0, attn_bias, "attn_bwd_dq")
    dk, dv = _attn_bwd_dkv(qa, ka, proj, col_v, oa, lse, do, 0, attn_bias, "attn_bwd_dkv")
    far = _attn_far_bwd(qa, ka, proj, col_v, oa, lse, do, 0, "attn_far_bwd")
    dpq, dpk, dv, dq_gain, dk_gain = _attn_prep_bwd(proj, q_norm, k_norm, cc, s1, s2, (dq, dk, dv), far, "attn_prep_bwd")
    dod, dz, ddn_gain = _post_bwd(od_raw, proj, col_z, delta_out_norm, do, HA, "delta_post_bwd")
    cots = _delta_scan_bwd(*prep, states, dod, "delta_scan_bwd")
    ddq, ddk, ddv, dab, dalog, ddtb = _delta_prep_bwd(dqkv, proj, col_ab, alog_row, dtb_row, cots, t_inv, "delta_prep_bwd")
    dconv_in, dconv_w = _conv_bwd(proj, col_d, conv_width, conv_full, jnp.concatenate([ddq, ddk, ddv], axis=1), "conv_bwd")
    dproj = jnp.concatenate([dpq, dpk, dv, dconv_in, dz, dab], axis=1)
    dh2 = _matmul(dproj, w_in_f, "nt", F32, "mix_in_dx", tm=1024, tn=1024, tk=2816)
    dw_in = _matmul(h2, dproj, "tn", BF16, "mix_in_dw", tm=512, tn=2432, tk=2048)
    rs_mix = scatter_start([_cols_to_slabs(dw_in[:, :in_width], 4), dw_out.reshape(4, -1, D)], "rs_mix_start")
    dx1, dgain2, dsh2, dsc2 = _pre_bwd(x1, mix_norm, sh2, sc2, dh2, dx2, "mix_pre_bwd")

    rs_ffn1d = []

    def send_dw_d1(dw_d):
        rs_ffn1d.extend(scatter_start([dw_d.reshape(4, fs, D)], "rs_ffn1d_start"))
        return rs_ffn1d[4][0, 0]

    rs_gu1 = []

    def send_dw_gu1(slab):
        (other,) = _swap_halves_with_sibling([slab], "rs_sibling_swap")
        half = slab.shape[1] // 2
        mine = lax.dynamic_slice_in_dim(slab, ci * half, half, axis=1)
        p32, p16 = _add_cast([mine, other], [F32, BF16], "rs_chip_sum")
        rs_gu1.append(p32)
        rs_gu1.extend(_start_copies([p16], [lax.empty(p16.shape, BF16)], _plan_exchange_chips, "rs_chip_exchange_start"))
        return rs_gu1[5]

    dx0, dw_gu1, dw_d1, dgain1, dsh1, dsc1, dgt1 = _ffn_bwd(dx1, saved1, ffn1_norm, sh1, sc1, gt1 + rs_mix[4][0, 0], w_gu1, w_d1, fs,
                                                            "ffn1", on_dw_d=send_dw_d1, on_dw_gu=send_dw_gu1)

    (got,) = _wait_copies(rs_gu1[1], rs_gu1[2], rs_gu1[3], rs_gu1[4], dx0, _plan_exchange_chips, "rs_chip_exchange_wait")
    parts = [lax.dynamic_index_in_dim(rs_gu1[0], chip, axis=0, keepdims=True)]
    parts += [lax.dynamic_index_in_dim(got, (chip + k) % 4, axis=0, keepdims=True) for k in (1, 2, 3)]
    halves = [_add_cast(parts, [F32], "rs_total_0")[0][0]]
    arrived = _wait_copies(rs_ffn1d[0], rs_ffn1d[1], rs_ffn1d[2], rs_ffn1d[3], dx0, _plan_scatter_direct, "rs_ffn1d_wait")
    arrived += _wait_copies(rs_mix[0], rs_mix[1], rs_mix[2], rs_mix[3], dx0, _plan_scatter_direct, "rs_mix_wait")
    arrived += _wait_copies(rs_ffn2[0], rs_ffn2[1], rs_ffn2[2], rs_ffn2[3], dx0, _plan_scatter_direct, "rs_ffn2_wait")
    for t, zone in enumerate(arrived):
        halves.append(_add_cast([zone[d:d + 1] for d in range(8)], [F32], "rs_total_%d" % (t + 2))[0][0])
    theirs = _send_halves_to_sibling(halves, "rs_sibling_join")
    h_gu1, h_d1, h_in, h_out, h_gu2, h_d2 = zip(halves, theirs)

    n_small = N_MOD * D + 3 * D + 5 * LANES + CONV_WIDTH * conv_width
    cols_small = -(-n_small // (8 * LANES)) * LANES
    small = _flat_pad([dsh1, dsc1, dgt1, dsh2, dsc2, dgt2, dsh3, dsc3, dgt3, dgain1, dgain2, dgain3,
                       dq_gain, dk_gain, dalog, ddtb, ddn_gain, dconv_w], 8, cols_small)
    got2 = _all_gather_small(small, "gather_small_grads", after=theirs[0])
    small_sum = _add_cast([got2[d:d + 1] for d in range(8)], [F32], "sum_small_grads")[0].reshape(-1)
    dmod_all = got2.reshape(8, -1)[:, :N_MOD * D]
    off = [0]

    def take(n):
        off[0] += n
        return small_sum[off[0] - n:off[0]]

    g_b_ada = take(N_MOD * D).reshape(1, -1)
    g_ffn1_norm, g_mix_norm, g_ffn2_norm = take(D).reshape(1, D), take(D).reshape(1, D), take(D).reshape(1, D)
    g_q_norm, g_k_norm = take(LANES).reshape(1, -1), take(LANES).reshape(1, -1)
    g_a_log, g_dt_bias = take(LANES)[:HD].reshape(1, HD), take(LANES)[:HD].reshape(1, HD)
    g_dn = take(LANES).reshape(1, -1)
    g_conv_full = take(CONV_WIDTH * conv_width).reshape(CONV_WIDTH, conv_width)
    g_conv = lax.dynamic_slice(g_conv_full, (0, chip * conv_shard), (CONV_WIDTH, conv_shard))

    res = {}

    def upd(name, w, pair, col, m, v):
        res[name] = tuple(_adamw_halves(w, pair[0], pair[1], col, m, v, "adamw_" + name))

    upd("ffn1_w_gate", ffn1_w_gate, h_gu1, 0, m_ffn1_w_gate, v_ffn1_w_gate)
    upd("ffn1_w_up", ffn1_w_up, h_gu1, 1, m_ffn1_w_up, v_ffn1_w_up)
    upd("ffn1_w_down", ffn1_w_down, h_d1, 0, m_ffn1_w_down, v_ffn1_w_down)
    res["w_in"] = tuple(_adamw_halves_t(w_in, h_in[0], h_in[1], m_w_in, v_w_in, "adamw_w_in"))
    upd("w_out", w_out, h_out, 0, m_w_out, v_w_out)
    upd("ffn2_w_gate", ffn2_w_gate, h_gu2, 0, m_ffn2_w_gate, v_ffn2_w_gate)
    upd("ffn2_w_up", ffn2_w_up, h_gu2, 1, m_ffn2_w_up, v_ffn2_w_up)
    upd("ffn2_w_down", ffn2_w_down, h_d2, 0, m_ffn2_w_down, v_ffn2_w_down)
    d_cv, nm_cv, nv_cv = _adamw(conv_w[0], g_conv, m_conv_w[0], v_conv_w[0], "adamw_conv_w")
    res["conv_w"] = (g_conv[None], d_cv[None], nm_cv[None], nv_cv[None])

    dmod_mine = lax.dynamic_slice(dmod_all, (0, chip * n_mod_shard), (8, n_mod_shard))
    g, d, nm, nv = _adamw_outer(w_ada[0], m_w_ada[0], v_w_ada[0], jnp.transpose(c_all), dmod_mine, "adamw_w_ada")
    res["w_ada"] = (g[None], d[None], nm[None], nv[None])

    rep = [("b_ada", b_ada, g_b_ada, m_b_ada, v_b_ada), ("ffn1_norm", ffn1_norm, g_ffn1_norm, m_ffn1_norm, v_ffn1_norm),
           ("mix_norm", mix_norm, g_mix_norm, m_mix_norm, v_mix_norm), ("ffn2_norm", ffn2_norm, g_ffn2_norm, m_ffn2_norm, v_ffn2_norm),
           ("q_norm", q_norm, g_q_norm, m_q_norm, v_q_norm), ("k_norm", k_norm, g_k_norm, m_k_norm, v_k_norm),
           ("a_log", a_log, g_a_log, m_a_log, v_a_log), ("dt_bias", dt_bias, g_dt_bias, m_dt_bias, v_dt_bias),
           ("delta_out_norm", delta_out_norm, g_dn, m_delta_out_norm, v_delta_out_norm)]
    n_rep = sum(-(-r[1].shape[1] // LANES) * LANES for r in rep)
    cols_rep = -(-n_rep // (8 * LANES)) * LANES

    def pack_rep(idx):
        return _flat_pad([jnp.pad(r[idx], ((0, 0), (0, -r[idx].shape[1] % LANES))) for r in rep], 8, cols_rep)

    d_rep, nm_rep, nv_rep = [a.reshape(-1) for a in _adamw(pack_rep(1), pack_rep(2), pack_rep(3), pack_rep(4), "adamw_small")]
    o2 = 0
    for name, w, g, _, _ in rep:
        n = w.shape[1]
        res[name] = (g, d_rep[o2:o2 + n].reshape(1, n), nm_rep[o2:o2 + n].reshape(1, n), nv_rep[o2:o2 + n].reshape(1, n))
        o2 += -(-n // LANES) * LANES

    order = ["w_ada", "b_ada", "ffn1_norm", "ffn1_w_gate", "ffn1_w_up", "ffn1_w_down", "mix_norm", "w_in", "conv_w", "q_norm",
             "k_norm", "a_log", "dt_bias", "delta_out_norm", "w_out", "ffn2_norm", "ffn2_w_gate", "ffn2_w_up", "ffn2_w_down"]
    return (loss, dx0[None], *[res[n][0] for n in order], *[res[n][1] for n in order],
            *[res[n][2] for n in order], *[res[n][3] for n in order])
```

```python
import functools
import math

import jax
import jax.numpy as jnp
from jax import lax
from jax.experimental import pallas as pl
from jax.experimental.pallas import tpu as pltpu

F32 = jnp.float32
BF16 = jnp.bfloat16
MESH = pl.DeviceIdType.MESH

HEAD_DIM = 128
N_ATTN_HEADS = 8
N_DELTA_HEADS = 8
DILATED_PATTERNS = ((128, 1), (512, 4), (2048, 16))
ROPE_THETA = 500000.0
ROPE_DIM = HEAD_DIM // 4
CONV_WIDTH = 4
CHUNK = 64
NORM_EPS = 1e-6
N_MOD = 9
ADAM_LR = 0.001
ADAM_B1 = 0.9
ADAM_B2 = 0.999
ADAM_EPS = 1e-08
ADAM_WD = 0.01
ADAM_STEP = 10

LANES = 128
VMEM_LIMIT = 56 * 1024 * 1024
ATTN_TILE = 512
HIGHEST = lax.Precision.HIGHEST


def _cparams(sem=None):
    return pltpu.CompilerParams(dimension_semantics=sem, vmem_limit_bytes=VMEM_LIMIT)


def _pick(dim, pref):
    if dim <= pref:
        return dim
    t = (pref // LANES) * LANES
    while t >= LANES:
        if dim % t == 0:
            return t
        t -= LANES
    return dim


def _sigmoid(x):
    return 1.0 / (1.0 + jnp.exp(-x))


def _silu(x):
    return x * _sigmoid(x)


def _softplus(x):
    return jnp.maximum(x, 0.0) + jnp.log(1.0 + jnp.exp(-jnp.abs(x)))


def _rms(x, gain):
    return x * lax.rsqrt(jnp.mean(x * x, axis=-1, keepdims=True) + NORM_EPS) * gain


def _l2(x):
    return x * lax.rsqrt(jnp.sum(x * x, axis=-1, keepdims=True) + NORM_EPS)


def _modulate(x, gain, shift, scale):
    return _rms(x, gain) * (1.0 + scale) + shift


def _dot(a, b):
    return lax.dot_general(a, b, (((1,), (0,)), ((), ())), precision=HIGHEST, preferred_element_type=F32)


def _bdot(a, b, dims):
    return lax.dot_general(a.astype(BF16), b.astype(BF16), (dims, ((), ())), preferred_element_type=F32)


_NN, _NT, _TN = ((1,), (0,)), ((1,), (1,)), ((0,), (0,))
HIGH = lax.Precision.HIGH


def _dot3(a, b, dims=_NN):
    return lax.dot_general(a, b, (dims, ((), ())), precision=HIGH, preferred_element_type=F32)


@jax.custom_vjp
def _mm_nn(a, b):
    return _bdot(a, b, _NN)


_mm_nn.defvjp(lambda a, b: (_bdot(a, b, _NN), (a, b)),
              lambda res, g: (_bdot(g, res[1], _NT), _bdot(res[0], g, _TN)))


@jax.custom_vjp
def _mm_nt(a, b):
    return _bdot(a, b, _NT)


_mm_nt.defvjp(lambda a, b: (_bdot(a, b, _NT), (a, b)),
              lambda res, g: (_bdot(g, res[1], _NN), _bdot(g, res[0], _TN)))


@jax.custom_vjp
def _mm_tn(a, b):
    return _bdot(a, b, _TN)


_mm_tn.defvjp(lambda a, b: (_bdot(a, b, _TN), (a, b)),
              lambda res, g: (_bdot(res[1], g, _NT), _bdot(res[0], g, _NN)))


@jax.custom_vjp
def _tri_inv_saved(a, t_inv):
    return t_inv


_tri_inv_saved.defvjp(lambda a, t_inv: (t_inv, t_inv),
                      lambda t_inv, g: (-_dot3(t_inv, _dot3(g, t_inv, _NT), _TN), jnp.zeros_like(t_inv)))


_MM_DIMS = {"nn": ((1,), (0,)), "nt": ((1,), (1,)), "tn": ((0,), (0,))}


def _matmul(a, b, mode, out_dtype, name, tm=1024, tn=1024, tk=1024, col_slabs=False, epilogue=None, after=None):
    if mode == "nn":
        (M, K), (_, N) = a.shape, b.shape
    elif mode == "nt":
        (M, K), (N, _) = a.shape, b.shape
    else:
        (K, M), (_, N) = a.shape, b.shape
    tm, tn, tk = _pick(M, tm), _pick(N, tn), _pick(K, tk)
    nk = K // tk
    dims = _MM_DIMS[mode]
    epi_fn, extra_in, outs = epilogue if epilogue is not None else (None, [], None)
    if outs is None:
        if col_slabs:
            outs = [((N // tn, M, tn), out_dtype, (None, tm, tn), lambda i, j: (j, i, 0))]
        else:
            outs = [((M, N), out_dtype, (tm, tn), lambda i, j: (i, j))]
    n_in, n_out = len(extra_in), len(outs)

    tied = [] if after is None else [after]

    def body(a_ref, b_ref, *rest):
        rest = rest[len(tied):]
        in_refs, out_refs = rest[:n_in], rest[n_in:n_in + n_out]
        k = pl.program_id(2)
        p = _bdot(a_ref[...], b_ref[...], dims)

        def finish(tile):
            if epi_fn is None:
                out_refs[0][...] = tile.astype(out_dtype)
            else:
                epi_fn(tile, in_refs, out_refs)

        if nk == 1:
            finish(p)
        else:
            acc_ref = rest[-1]

            @pl.when(k == 0)
            def _():
                acc_ref[...] = p

            @pl.when((k > 0) & (k < nk - 1))
            def _():
                acc_ref[...] += p

            @pl.when(k == nk - 1)
            def _():
                finish(acc_ref[...] + p)

    def ij(index_map):
        return lambda i, j, k: index_map(i, j)

    a_spec = pl.BlockSpec((tk, tm), lambda i, j, k: (k, i)) if mode == "tn" else pl.BlockSpec((tm, tk), lambda i, j, k: (i, k))
    b_spec = pl.BlockSpec((tn, tk), lambda i, j, k: (j, k)) if mode == "nt" else pl.BlockSpec((tk, tn), lambda i, j, k: (k, j))
    res = pl.pallas_call(
        body, name=name, grid=(M // tm, N // tn, nk),
        in_specs=[a_spec, b_spec] + [pl.BlockSpec(memory_space=pl.ANY)] * len(tied) + [pl.BlockSpec(blk, ij(im)) for _, blk, im in extra_in],
        out_specs=[pl.BlockSpec(blk, ij(im)) for _, _, blk, im in outs],
        out_shape=[jax.ShapeDtypeStruct(shp, dt) for shp, dt, _, _ in outs],
        scratch_shapes=[pltpu.VMEM((tm, tn), F32)] if nk > 1 else [],
        compiler_params=_cparams(("parallel", "parallel", "arbitrary")),
    )(a, b, *tied, *[arr for arr, _, _ in extra_in])
    return res if epilogue is not None else res[0]


def _row_spec(tr, d):
    return pl.BlockSpec((tr, d), lambda i: (i, 0))


def _vec_spec(d):
    return pl.BlockSpec((1, d), lambda i: (0, 0))


def _pre_fwd(x, gain, shift, scale, name):
    S, D = x.shape
    tr = _pick(S, 256)

    def body(x_ref, g_ref, sh_ref, sc_ref, h_ref):
        h_ref[...] = _modulate(x_ref[...], g_ref[...], sh_ref[...], sc_ref[...]).astype(BF16)

    return pl.pallas_call(
        body, name=name, grid=(S // tr,),
        in_specs=[_row_spec(tr, D), _vec_spec(D), _vec_spec(D), _vec_spec(D)],
        out_specs=_row_spec(tr, D), out_shape=jax.ShapeDtypeStruct((S, D), BF16),
        compiler_params=_cparams(("parallel",)),
    )(x, gain, shift, scale)


def _pre_bwd(x, gain, shift, scale, dh, dx_in, name):
    S, D = x.shape
    tr = _pick(S, 256)

    def body(x_ref, g_ref, sh_ref, sc_ref, dh_ref, dxin_ref, dx_ref, dg_ref, dsh_ref, dsc_ref):
        _, vjp = jax.vjp(_modulate, x_ref[...], g_ref[...], sh_ref[...], sc_ref[...])
        dx, dg, dsh, dsc = vjp(dh_ref[...])
        dx_ref[...] = dxin_ref[...] + dx

        @pl.when(pl.program_id(0) == 0)
        def _():
            dg_ref[...] = jnp.zeros_like(dg_ref)
            dsh_ref[...] = jnp.zeros_like(dsh_ref)
            dsc_ref[...] = jnp.zeros_like(dsc_ref)

        dg_ref[...] += dg
        dsh_ref[...] += dsh
        dsc_ref[...] += dsc

    vec = jax.ShapeDtypeStruct((1, D), F32)
    return pl.pallas_call(
        body, name=name, grid=(S // tr,),
        in_specs=[_row_spec(tr, D), _vec_spec(D), _vec_spec(D), _vec_spec(D), _row_spec(tr, D), _row_spec(tr, D)],
        out_specs=[_row_spec(tr, D), _vec_spec(D), _vec_spec(D), _vec_spec(D)],
        out_shape=[jax.ShapeDtypeStruct((S, D), F32), vec, vec, vec],
        compiler_params=_cparams(("arbitrary",)),
    )(x, gain, shift, scale, dh, dx_in)


def _matmul_residual(a, b, x, gate, coef, name):
    S, D = x.shape
    tm, tn = _pick(S, 512), _pick(D, 512)

    def residual_out(tile, ins, outs):
        outs[0][...] = tile
        outs[1][...] = ins[0][...] + coef * ins[1][...] * tile

    blk = ((tm, tn), lambda i, j: (i, j))
    return _matmul(a, b, "nn", F32, name, tm=tm, tn=tn, tk=8192, epilogue=(
        residual_out, [(x,) + blk, (gate, (1, tn), lambda i, j: (0, j))], [((S, D), F32) + blk, ((S, D), F32) + blk]))


def _residual_bwd(gate, f, dxn, coef, name):
    S, D = f.shape
    tr = _pick(S, 256)

    def body(g_ref, f_ref, d_ref, df_ref, dg_ref):
        d = d_ref[...]
        df_ref[...] = (coef * g_ref[...] * d).astype(BF16)

        @pl.when(pl.program_id(0) == 0)
        def _():
            dg_ref[...] = jnp.zeros_like(dg_ref)

        dg_ref[...] += jnp.sum(coef * f_ref[...] * d, axis=0, keepdims=True)

    return pl.pallas_call(
        body, name=name, grid=(S // tr,),
        in_specs=[_vec_spec(D), _row_spec(tr, D), _row_spec(tr, D)],
        out_specs=[_row_spec(tr, D), _vec_spec(D)],
        out_shape=[jax.ShapeDtypeStruct((S, D), BF16), jax.ShapeDtypeStruct((1, D), F32)],
        compiler_params=_cparams(("arbitrary",)),
    )(gate, f, dxn)


def _swiglu_fn(a, b):
    return _silu(a) * b


def _loss_head(y, target, name):
    S, D = y.shape
    tr = _pick(S, 256)

    def body(y_ref, t_ref, l_ref, dy_ref):
        e = y_ref[...] - t_ref[...]
        dy_ref[...] = e * (1.0 / D)

        @pl.when(pl.program_id(0) == 0)
        def _():
            l_ref[...] = jnp.zeros_like(l_ref)

        l_ref[...] += jnp.sum(jnp.sum(e * e, axis=-1, keepdims=True), axis=0, keepdims=True) * (0.5 / D)

    return pl.pallas_call(
        body, name=name, grid=(S // tr,),
        in_specs=[_row_spec(tr, D), _row_spec(tr, D)],
        out_specs=[pl.BlockSpec((1, 1), lambda i: (0, 0)), _row_spec(tr, D)],
        out_shape=[jax.ShapeDtypeStruct((1, 1), F32), jax.ShapeDtypeStruct((S, D), F32)],
        compiler_params=_cparams(("arbitrary",)),
    )(y, target)


def _rope(y, cc, s1, s2):
    return y * cc + pltpu.roll(y, LANES - ROPE_DIM // 2, 1) * s1 + pltpu.roll(y, ROPE_DIM // 2, 1) * s2


def _rope_t(d, cc, s1, s2):
    return d * cc + pltpu.roll(d * s1, ROPE_DIM // 2, 1) + pltpu.roll(d * s2, LANES - ROPE_DIM // 2, 1)


def _head_spec(tr, col0):
    return pl.BlockSpec((tr, HEAD_DIM), lambda i, h: (i, col0 + h))


def _tab_spec(tr):
    return pl.BlockSpec((tr, HEAD_DIM), lambda i, h: (i, 0))


def _gain_spec():
    return pl.BlockSpec((1, HEAD_DIM), lambda i, h: (0, 0))


def _attn_prep_fwd(proj, q_gain, k_gain, cc, s1, s2, name):
    S = proj.shape[0]
    H = N_ATTN_HEADS
    tr = _pick(S, 512)

    def body(q_ref, k_ref, qg_ref, kg_ref, cc_ref, s1_ref, s2_ref, qo_ref, ko_ref):
        cc, s1, s2 = cc_ref[...], s1_ref[...], s2_ref[...]
        qo_ref[...] = _rope(_rms(q_ref[...], qg_ref[...]), cc, s1, s2)
        ko_ref[...] = _rope(_rms(k_ref[...], kg_ref[...]), cc, s1, s2)

    out = jax.ShapeDtypeStruct((S, H * HEAD_DIM), F32)
    return pl.pallas_call(
        body, name=name, grid=(S // tr, H),
        in_specs=[_head_spec(tr, 0), _head_spec(tr, H), _gain_spec(), _gain_spec(), _tab_spec(tr), _tab_spec(tr), _tab_spec(tr)],
        out_specs=[_head_spec(tr, 0), _head_spec(tr, 0)], out_shape=[out, out],
        compiler_params=_cparams(("parallel", "parallel")),
    )(proj, proj, q_gain, k_gain, cc, s1, s2)


def _attn_prep_bwd(proj, q_gain, k_gain, cc, s1, s2, near, far, name):
    S = proj.shape[0]
    H = N_ATTN_HEADS
    tr = _pick(S, 512)

    def body(q_ref, k_ref, qg_ref, kg_ref, cc_ref, s1_ref, s2_ref, dqn_ref, dkn_ref, dvn_ref, dqf_ref, dkf_ref, dvf_ref,
             dpq_ref, dpk_ref, dpv_ref, dqg_ref, dkg_ref):
        cc, s1, s2 = cc_ref[...], s1_ref[...], s2_ref[...]
        dq = dqn_ref[...] + dqf_ref[...]
        dk = dkn_ref[...] + dkf_ref[...]
        dpv_ref[...] = (dvn_ref[...] + dvf_ref[...]).astype(BF16)

        @pl.when((pl.program_id(0) == 0) & (pl.program_id(1) == 0))
        def _():
            dqg_ref[...] = jnp.zeros_like(dqg_ref)
            dkg_ref[...] = jnp.zeros_like(dkg_ref)

        _, vjp_q = jax.vjp(_rms, q_ref[...], qg_ref[...])
        dxq, dgq = vjp_q(_rope_t(dq, cc, s1, s2))
        _, vjp_k = jax.vjp(_rms, k_ref[...], kg_ref[...])
        dxk, dgk = vjp_k(_rope_t(dk, cc, s1, s2))
        dpq_ref[...] = dxq.astype(BF16)
        dpk_ref[...] = dxk.astype(BF16)
        dqg_ref[...] += dgq
        dkg_ref[...] += dgk

    out = jax.ShapeDtypeStruct((S, H * HEAD_DIM), BF16)
    gout = jax.ShapeDtypeStruct((1, HEAD_DIM), F32)
    return pl.pallas_call(
        body, name=name, grid=(S // tr, H),
        in_specs=[_head_spec(tr, 0), _head_spec(tr, H), _gain_spec(), _gain_spec(), _tab_spec(tr), _tab_spec(tr), _tab_spec(tr)]
        + [_head_spec(tr, 0)] * 6,
        out_specs=[_head_spec(tr, 0)] * 3 + [_gain_spec(), _gain_spec()], out_shape=[out, out, out, gout, gout],
        compiler_params=_cparams(("arbitrary", "arbitrary")),
    )(proj, proj, q_gain, k_gain, cc, s1, s2, *near, *far)


def _multiplicity(j, t):
    ti = lax.broadcasted_iota(jnp.int32, (t, t), 0)
    si = lax.broadcasted_iota(jnp.int32, (t, t), 1)
    delta = j * t + ti - si
    cnt = jnp.zeros((t, t), F32)
    for window, dil in NEAR_PATTERNS:
        ok = (delta >= 0) & ((delta & (dil - 1)) == 0) & (delta <= window)
        cnt = cnt + ok.astype(F32)
    return cnt


_NEG = -1e30
NEAR_PATTERNS = DILATED_PATTERNS[:2]
NEAR_WINDOW = max(w for w, _ in NEAR_PATTERNS)
FAR_WINDOW, FAR_DIL = DILATED_PATTERNS[2]


def _log_multiplicity_table(t):
    cnt = jnp.stack([_multiplicity(j, t) for j in range(NEAR_WINDOW // t + 1)])
    return jnp.where(cnt > 0.0, jnp.log(jnp.maximum(cnt, 1.0)), _NEG)


def _bias_spec(t):
    return pl.BlockSpec((NEAR_WINDOW // t + 1, t, t), lambda h, i, j: (0, 0, 0))


def _attn_fwd(q, k, proj, v_col0, bias, name):
    S = q.shape[0]
    H = N_ATTN_HEADS
    t = _pick(S, ATTN_TILE)
    nq = S // t
    nj = NEAR_WINDOW // t + 1
    scale = HEAD_DIM ** -0.5

    def body(q_ref, k_ref, v_ref, b_ref, o_ref, lse_ref, m_sc, l_sc, acc_sc):
        qb, j = pl.program_id(1), pl.program_id(2)

        @pl.when(j == 0)
        def _():
            m_sc[...] = jnp.full_like(m_sc, _NEG)
            l_sc[...] = jnp.zeros_like(l_sc)
            acc_sc[...] = jnp.zeros_like(acc_sc)

        @pl.when(qb - j >= 0)
        def _():
            s = _bdot(q_ref[...], k_ref[...], ((1,), (1,))) * scale + b_ref[j]
            m_prev = m_sc[...]
            m_new = jnp.maximum(m_prev, jnp.max(s, axis=-1, keepdims=True))
            alpha = jnp.exp(m_prev - m_new)
            p = jnp.exp(s - m_new)
            l_sc[...] = alpha * l_sc[...] + jnp.sum(p, axis=-1, keepdims=True)
            acc_sc[...] = alpha * acc_sc[...] + _bdot(p, v_ref[...], ((1,), (0,)))
            m_sc[...] = m_new

        @pl.when(j == nj - 1)
        def _():
            o_ref[...] = acc_sc[...] / l_sc[...]
            lse_ref[...] = jnp.broadcast_to(m_sc[...] + jnp.log(l_sc[...]), (t, HEAD_DIM))

    qspec = pl.BlockSpec((t, HEAD_DIM), lambda h, i, j: (i, h))
    kspec = pl.BlockSpec((t, HEAD_DIM), lambda h, i, j: (jnp.maximum(i - j, 0), h))
    vspec = pl.BlockSpec((t, HEAD_DIM), lambda h, i, j: (jnp.maximum(i - j, 0), v_col0 + h))
    return pl.pallas_call(
        body, name=name, grid=(H, nq, nj),
        in_specs=[qspec, kspec, vspec, _bias_spec(t)], out_specs=[qspec, qspec],
        out_shape=[jax.ShapeDtypeStruct((S, H * HEAD_DIM), F32), jax.ShapeDtypeStruct((S, H * HEAD_DIM), F32)],
        scratch_shapes=[pltpu.VMEM((t, 1), F32), pltpu.VMEM((t, 1), F32), pltpu.VMEM((t, HEAD_DIM), F32)],
        compiler_params=_cparams(("parallel", "parallel", "arbitrary")),
    )(q, k, proj, bias)


def _far_rows(r, n):
    return pl.ds(r, n, stride=FAR_DIL)


def _far_band_bias(n):
    i = lax.broadcasted_iota(jnp.int32, (n, n), 0)
    j = lax.broadcasted_iota(jnp.int32, (n, n), 1)
    return jnp.where((i >= j) & (i - j <= FAR_WINDOW // FAR_DIL), 0.0, _NEG)


def _col_spec(S, col0):
    return pl.BlockSpec((S, HEAD_DIM), lambda h: (0, col0 + h))


def _attn_far_fwd(q, k, proj, v_col0, o_near, lse_near, name):
    S = q.shape[0]
    H = N_ATTN_HEADS
    n = S // FAR_DIL
    scale = HEAD_DIM ** -0.5

    def body(q_ref, k_ref, v_ref, on_ref, ln_ref, o_ref, ob_ref, lse_ref):
        bias = _far_band_bias(n)
        for r in range(FAR_DIL):
            rows = _far_rows(r, n)
            s = _bdot(q_ref[rows, :], k_ref[rows, :], _NT) * scale + bias
            m = jnp.max(s, axis=-1, keepdims=True)
            p = jnp.exp(s - m)
            l = jnp.sum(p, axis=-1, keepdims=True)
            o_far = _bdot(p, v_ref[rows, :], _NN) / l
            lse_far = m + jnp.log(l)
            lse_near = jnp.max(ln_ref[rows, :], axis=-1, keepdims=True)
            top = jnp.maximum(lse_near, lse_far)
            lse = top + jnp.log(jnp.exp(lse_near - top) + jnp.exp(lse_far - top))
            o_ref[rows, :] = jnp.exp(lse_near - lse) * on_ref[rows, :] + jnp.exp(lse_far - lse) * o_far
            lse_ref[rows, :] = jnp.broadcast_to(lse, (n, HEAD_DIM))
        ob_ref[...] = o_ref[...].astype(BF16)

    cs = _col_spec(S, 0)
    return pl.pallas_call(
        body, name=name, grid=(H,),
        in_specs=[cs, cs, _col_spec(S, v_col0), cs, cs], out_specs=[cs, cs, cs],
        out_shape=[jax.ShapeDtypeStruct((S, H * HEAD_DIM), F32), jax.ShapeDtypeStruct((S, H * HEAD_DIM), BF16),
                   jax.ShapeDtypeStruct((S, H * HEAD_DIM), F32)],
        compiler_params=_cparams(("parallel",)),
    )(q, k, proj, o_near, lse_near)


def _attn_far_bwd(q, k, proj, v_col0, o, lse, do, do_col0, name):
    S = q.shape[0]
    H = N_ATTN_HEADS
    n = S // FAR_DIL
    scale = HEAD_DIM ** -0.5

    def body(q_ref, k_ref, v_ref, o_ref, lse_ref, do_ref, dq_ref, dk_ref, dv_ref):
        bias = _far_band_bias(n)
        for r in range(FAR_DIL):
            rows = _far_rows(r, n)
            q, k, v, do = q_ref[rows, :], k_ref[rows, :], v_ref[rows, :], do_ref[rows, :]
            dsum = jnp.sum(do * o_ref[rows, :], axis=-1, keepdims=True)
            lse = jnp.max(lse_ref[rows, :], axis=-1, keepdims=True)
            p = jnp.exp(_bdot(q, k, _NT) * scale + bias - lse)
            ds = p * (_bdot(do, v, _NT) - dsum)
            dq_ref[rows, :] = _bdot(ds, k, _NN) * scale
            dk_ref[rows, :] = _bdot(ds, q, _TN) * scale
            dv_ref[rows, :] = _bdot(p, do, _TN)

    cs = _col_spec(S, 0)
    shp = jax.ShapeDtypeStruct((S, H * HEAD_DIM), F32)
    return pl.pallas_call(
        body, name=name, grid=(H,),
        in_specs=[cs, cs, _col_spec(S, v_col0), cs, cs, _col_spec(S, do_col0)], out_specs=[cs, cs, cs], out_shape=[shp, shp, shp],
        compiler_params=_cparams(("parallel",)),
    )(q, k, proj, o, lse, do)


def _attn_probs(q, k, lse, bias_tile, scale):
    return jnp.exp(_bdot(q, k, ((1,), (1,))) * scale + bias_tile - lse)


def _attn_bwd_dq(q, k, proj, v_col0, o, lse, do, do_col0, bias, name):
    S = q.shape[0]
    H = N_ATTN_HEADS
    t = _pick(S, ATTN_TILE)
    nq = S // t
    nj = NEAR_WINDOW // t + 1
    scale = HEAD_DIM ** -0.5

    def body(q_ref, k_ref, v_ref, o_ref, lse_ref, do_ref, b_ref, dq_ref, acc_sc):
        qb, j = pl.program_id(1), pl.program_id(2)

        @pl.when(j == 0)
        def _():
            acc_sc[...] = jnp.zeros_like(acc_sc)

        @pl.when(qb - j >= 0)
        def _():
            do = do_ref[...]
            dsum = jnp.sum(do * o_ref[...], axis=-1, keepdims=True)
            lse = jnp.max(lse_ref[...], axis=-1, keepdims=True)
            p = _attn_probs(q_ref[...], k_ref[...], lse, b_ref[j], scale)
            dp = _bdot(do, v_ref[...], ((1,), (1,)))
            ds = p * (dp - dsum)
            acc_sc[...] += _bdot(ds, k_ref[...], ((1,), (0,))) * scale

        @pl.when(j == nj - 1)
        def _():
            dq_ref[...] = acc_sc[...]

    qspec = pl.BlockSpec((t, HEAD_DIM), lambda h, i, j: (i, h))
    dospec = pl.BlockSpec((t, HEAD_DIM), lambda h, i, j: (i, do_col0 + h))
    kspec = pl.BlockSpec((t, HEAD_DIM), lambda h, i, j: (jnp.maximum(i - j, 0), h))
    vspec = pl.BlockSpec((t, HEAD_DIM), lambda h, i, j: (jnp.maximum(i - j, 0), v_col0 + h))
    return pl.pallas_call(
        body, name=name, grid=(H, nq, nj),
        in_specs=[qspec, kspec, vspec, qspec, qspec, dospec, _bias_spec(t)], out_specs=qspec,
        out_shape=jax.ShapeDtypeStruct((S, H * HEAD_DIM), F32),
        scratch_shapes=[pltpu.VMEM((t, HEAD_DIM), F32)],
        compiler_params=_cparams(("parallel", "parallel", "arbitrary")),
    )(q, k, proj, o, lse, do, bias)


def _attn_bwd_dkv(q, k, proj, v_col0, o, lse, do, do_col0, bias, name):
    S = q.shape[0]
    H = N_ATTN_HEADS
    t = _pick(S, ATTN_TILE)
    nq = S // t
    nj = NEAR_WINDOW // t + 1
    scale = HEAD_DIM ** -0.5

    def body(q_ref, k_ref, v_ref, o_ref, lse_ref, do_ref, b_ref, dk_ref, dv_ref, dk_sc, dv_sc):
        kb, j = pl.program_id(1), pl.program_id(2)

        @pl.when(j == 0)
        def _():
            dk_sc[...] = jnp.zeros_like(dk_sc)
            dv_sc[...] = jnp.zeros_like(dv_sc)

        @pl.when(kb + j < nq)
        def _():
            do = do_ref[...]
            dsum = jnp.sum(do * o_ref[...], axis=-1, keepdims=True)
            lse = jnp.max(lse_ref[...], axis=-1, keepdims=True)
            p = _attn_probs(q_ref[...], k_ref[...], lse, b_ref[j], scale)
            dp = _bdot(do, v_ref[...], ((1,), (1,)))
            ds = p * (dp - dsum)
            dv_sc[...] += _bdot(p, do, ((0,), (0,)))
            dk_sc[...] += _bdot(ds, q_ref[...], ((0,), (0,))) * scale

        @pl.when(j == nj - 1)
        def _():
            dk_ref[...] = dk_sc[...]
            dv_ref[...] = dv_sc[...]

    def qrow(h, i, j):
        return jnp.minimum(i + j, nq - 1)

    qspec = pl.BlockSpec((t, HEAD_DIM), lambda h, i, j: (qrow(h, i, j), h))
    dospec = pl.BlockSpec((t, HEAD_DIM), lambda h, i, j: (qrow(h, i, j), do_col0 + h))
    kspec = pl.BlockSpec((t, HEAD_DIM), lambda h, i, j: (i, h))
    vspec = pl.BlockSpec((t, HEAD_DIM), lambda h, i, j: (i, v_col0 + h))
    return pl.pallas_call(
        body, name=name, grid=(H, nq, nj),
        in_specs=[qspec, kspec, vspec, qspec, qspec, dospec, _bias_spec(t)], out_specs=[kspec, kspec],
        out_shape=[jax.ShapeDtypeStruct((S, H * HEAD_DIM), F32), jax.ShapeDtypeStruct((S, H * HEAD_DIM), F32)],
        scratch_shapes=[pltpu.VMEM((t, HEAD_DIM), F32), pltpu.VMEM((t, HEAD_DIM), F32)],
        compiler_params=_cparams(("parallel", "parallel", "arbitrary")),
    )(q, k, proj, o, lse, do, bias)


def _conv_pre(x_ref, w_ref):
    x = x_ref[...]
    rows = lax.broadcasted_iota(jnp.int32, x.shape, 0)
    shifted = [x]
    acc = x * w_ref[pl.ds(CONV_WIDTH - 1, 1), :]
    for sft in range(1, CONV_WIDTH):
        xs = jnp.where(rows >= sft, pltpu.roll(x, sft, 0), 0.0)
        shifted.append(xs)
        acc = acc + xs * w_ref[pl.ds(CONV_WIDTH - 1 - sft, 1), :]
    return acc, shifted


def _conv_fwd(proj, col0, width, w, name):
    S = proj.shape[0]

    def body(x_ref, w_ref, y_ref):
        acc, _ = _conv_pre(x_ref, w_ref)
        y_ref[...] = _silu(acc)

    return pl.pallas_call(
        body, name=name, grid=(width // LANES,),
        in_specs=[pl.BlockSpec((S, LANES), lambda c: (0, col0 + c)), pl.BlockSpec((CONV_WIDTH, LANES), lambda c: (0, c))],
        out_specs=pl.BlockSpec((S, LANES), lambda c: (0, c)),
        out_shape=jax.ShapeDtypeStruct((S, width), F32),
        compiler_params=_cparams(("parallel",)),
    )(proj, w)


def _conv_bwd(proj, col0, width, w, dy, name):
    S = proj.shape[0]

    def body(x_ref, w_ref, d_ref, dx_ref, dw_ref):
        acc, shifted = _conv_pre(x_ref, w_ref)
        sig = _sigmoid(acc)
        da = d_ref[...] * (sig * (1.0 + acc * (1.0 - sig)))
        rows = lax.broadcasted_iota(jnp.int32, da.shape, 0)
        dx = da * w_ref[pl.ds(CONV_WIDTH - 1, 1), :]
        dw_ref[pl.ds(CONV_WIDTH - 1, 1), :] = jnp.sum(da * shifted[0], axis=0, keepdims=True)
        for sft in range(1, CONV_WIDTH):
            back = jnp.where(rows < S - sft, pltpu.roll(da, S - sft, 0), 0.0)
            dx = dx + back * w_ref[pl.ds(CONV_WIDTH - 1 - sft, 1), :]
            dw_ref[pl.ds(CONV_WIDTH - 1 - sft, 1), :] = jnp.sum(da * shifted[sft], axis=0, keepdims=True)
        dx_ref[...] = dx.astype(BF16)

    return pl.pallas_call(
        body, name=name, grid=(width // LANES,),
        in_specs=[pl.BlockSpec((S, LANES), lambda c: (0, col0 + c)), pl.BlockSpec((CONV_WIDTH, LANES), lambda c: (0, c)),
                  pl.BlockSpec((S, LANES), lambda c: (0, c))],
        out_specs=[pl.BlockSpec((S, LANES), lambda c: (0, c)), pl.BlockSpec((CONV_WIDTH, LANES), lambda c: (0, c))],
        out_shape=[jax.ShapeDtypeStruct((S, width), BF16), jax.ShapeDtypeStruct((CONV_WIDTH, width), F32)],
        compiler_params=_cparams(("parallel",)),
    )(proj, w, dy)


PREP_CHUNKS = 16


def _chunks_prep(qraws, kraws, vs, abs_, alog_row, dtb_row, mask_g, mask_b, t_saved=None):
    n = len(qraws)
    c = qraws[0].shape[0]
    mm_nt, mm_nn = (_mm_nt, _mm_nn) if t_saved is not None else (lambda p, r: _bdot(p, r, _NT), lambda p, r: _bdot(p, r, _NN))
    row = lax.broadcasted_iota(jnp.int32, (c, c), 0)
    col = lax.broadcasted_iota(jnp.int32, (c, c), 1)
    tril, strict, eye = row >= col, row > col, row == col
    eyef = eye.astype(F32)
    neg_rate = -jnp.exp(alog_row)
    q, k, beta, gc_col, gamma, kb, g_last = [], [], [], [], [], [], []
    for i in range(n):
        q.append(_l2(qraws[i]) * (HEAD_DIM ** -0.5))
        k.append(_l2(kraws[i]))
        gfull = neg_rate * _softplus(abs_[i] + dtb_row)
        g = jnp.sum(jnp.where(mask_g, gfull, 0.0), axis=-1, keepdims=True)
        beta.append(jnp.sum(jnp.where(mask_b, _sigmoid(abs_[i]), 0.0), axis=-1, keepdims=True))
        g_row = jnp.sum(jnp.where(eye, g, 0.0), axis=0, keepdims=True)
        gc_col.append(jnp.sum(jnp.where(tril, g_row, 0.0), axis=1, keepdims=True))
        gc_row = jnp.sum(jnp.where(row <= col, g, 0.0), axis=0, keepdims=True)
        gamma.append(jnp.where(tril, jnp.exp(jnp.where(tril, gc_col[i] - gc_row, 0.0)), 0.0))
        kb.append(k[i] * beta[i])
        g_last.append(jnp.sum(g, axis=0, keepdims=True))
    a = [jnp.where(strict, mm_nt(kb[i], k[i]) * gamma[i], 0.0) for i in range(n)]
    if t_saved is None:
        t_inv = [eyef - a[i] for i in range(n)]
        p = a
        for _ in range(int(math.log2(c)) - 1):
            p = [_dot3(p[i], p[i]) for i in range(n)]
            t_inv = [_dot3(t_inv[i], eyef + p[i]) for i in range(n)]
    else:
        t_inv = [_tri_inv_saved(a[i], t_saved[i]) for i in range(n)]
    egc = [jnp.exp(gc_col[i]) for i in range(n)]
    u = [mm_nn(t_inv[i], vs[i] * beta[i]) for i in range(n)]
    w = [mm_nn(t_inv[i], kb[i] * egc[i]) for i in range(n)]
    intra = [mm_nt(q[i], k[i]) * gamma[i] for i in range(n)]
    out = []
    for i in range(n):
        kt = k[i] * jnp.exp(g_last[i] - gc_col[i])
        dec = jnp.broadcast_to(jnp.exp(g_last[i]), (1, HEAD_DIM))
        one = (u[i], w[i], q[i] * egc[i], kt, intra[i], dec)
        out.append(one + (t_inv[i],) if t_saved is None else one)
    return out


def _lane_masks(h):
    lane = lax.broadcasted_iota(jnp.int32, (1, LANES), 1)
    return lane == h, lane == N_DELTA_HEADS + h


def _prep_specs(tr, ab_col):
    H = N_DELTA_HEADS
    return [
        pl.BlockSpec((tr, HEAD_DIM), lambda i, h: (i, h)),
        pl.BlockSpec((tr, HEAD_DIM), lambda i, h: (i, H + h)),
        pl.BlockSpec((tr, HEAD_DIM), lambda i, h: (i, 2 * H + h)),
        pl.BlockSpec((tr, LANES), lambda i, h: (i, ab_col)),
        pl.BlockSpec((1, LANES), lambda i, h: (0, 0)),
        pl.BlockSpec((1, LANES), lambda i, h: (0, 0)),
    ]


def _prep_out_specs(tr):
    nc = tr // CHUNK
    hs = pl.BlockSpec((tr, HEAD_DIM), lambda i, h: (i, h))
    return [hs, hs, hs, hs,
            pl.BlockSpec((None, tr, CHUNK), lambda i, h: (h, i, 0)),
            pl.BlockSpec((None, nc, 1, HEAD_DIM), lambda i, h: (h, i, 0, 0)),
            pl.BlockSpec((None, tr, CHUNK), lambda i, h: (h, i, 0))]


def _prep_out_shapes(S):
    H = N_DELTA_HEADS
    hs = jax.ShapeDtypeStruct((S, H * HEAD_DIM), F32)
    sq = jax.ShapeDtypeStruct((H, S, CHUNK), F32)
    return [hs, hs, hs, hs, sq, jax.ShapeDtypeStruct((H, S // CHUNK, 1, HEAD_DIM), F32), sq]


def _delta_prep_fwd(dqkv, proj, ab_col, alog_row, dtb_row, name):
    S = dqkv.shape[0]
    tr = min(S, PREP_CHUNKS * CHUNK)
    nc = tr // CHUNK

    def body(q_ref, k_ref, v_ref, ab_ref, al_ref, dt_ref, u_ref, w_ref, qd_ref, kt_ref, in_ref, dec_ref, ti_ref):
        mask_g, mask_b = _lane_masks(pl.program_id(1))
        rows = [pl.ds(ci * CHUNK, CHUNK) for ci in range(nc)]
        outs = _chunks_prep([q_ref[rs, :] for rs in rows], [k_ref[rs, :] for rs in rows], [v_ref[rs, :] for rs in rows],
                            [ab_ref[rs, :] for rs in rows], al_ref[...], dt_ref[...], mask_g, mask_b)
        for ci, rs in enumerate(rows):
            u, w, qd, kt, intra, dec, t_inv = outs[ci]
            u_ref[rs, :] = u
            w_ref[rs, :] = w
            qd_ref[rs, :] = qd
            kt_ref[rs, :] = kt
            in_ref[rs, :] = intra
            dec_ref[ci] = dec
            ti_ref[rs, :] = t_inv

    return pl.pallas_call(
        body, name=name, grid=(S // tr, N_DELTA_HEADS),
        in_specs=_prep_specs(tr, ab_col), out_specs=_prep_out_specs(tr), out_shape=_prep_out_shapes(S),
        compiler_params=_cparams(("parallel", "parallel")),
    )(dqkv, dqkv, dqkv, proj, alog_row, dtb_row)


def _delta_prep_bwd(dqkv, proj, ab_col, alog_row, dtb_row, cots, t_inv, name):
    S = dqkv.shape[0]
    H = N_DELTA_HEADS
    tr = min(S, PREP_CHUNKS * CHUNK)
    nc = tr // CHUNK

    def body(q_ref, k_ref, v_ref, ab_ref, al_ref, dt_ref, du_ref, dw_ref, dqd_ref, dkt_ref, din_ref, ddec_ref, ti_ref,
             dq_ref, dk_ref, dv_ref, dab_ref, dal_ref, ddt_ref, dab_sc):
        h = pl.program_id(1)
        mask_g, mask_b = _lane_masks(h)

        @pl.when((pl.program_id(0) == 0) & (h == 0))
        def _():
            dal_ref[...] = jnp.zeros_like(dal_ref)
            ddt_ref[...] = jnp.zeros_like(ddt_ref)

        @pl.when(h == 0)
        def _():
            dab_sc[...] = jnp.zeros_like(dab_sc)

        rows = [pl.ds(ci * CHUNK, CHUNK) for ci in range(nc)]
        fn = functools.partial(_chunks_prep, mask_g=mask_g, mask_b=mask_b, t_saved=[ti_ref[rs, :] for rs in rows])
        _, vjp = jax.vjp(fn, [q_ref[rs, :] for rs in rows], [k_ref[rs, :] for rs in rows], [v_ref[rs, :] for rs in rows],
                         [ab_ref[rs, :] for rs in rows], al_ref[...], dt_ref[...])
        dqs, dks, dvs, dabs, dal, ddt = vjp([(du_ref[rs, :], dw_ref[rs, :], dqd_ref[rs, :], dkt_ref[rs, :], din_ref[rs, :],
                                              ddec_ref[ci]) for ci, rs in enumerate(rows)])
        for ci, rs in enumerate(rows):
            dq_ref[rs, :] = dqs[ci]
            dk_ref[rs, :] = dks[ci]
            dv_ref[rs, :] = dvs[ci]
            dab_sc[rs, :] += dabs[ci]
        dal_ref[...] += dal
        ddt_ref[...] += ddt

        @pl.when(h == H - 1)
        def _():
            dab_ref[...] = dab_sc[...].astype(BF16)

    hs = pl.BlockSpec((tr, HEAD_DIM), lambda i, h: (i, h))
    hshape = jax.ShapeDtypeStruct((S, H * HEAD_DIM), F32)
    row = pl.BlockSpec((1, LANES), lambda i, h: (0, 0))
    rshape = jax.ShapeDtypeStruct((1, LANES), F32)
    return pl.pallas_call(
        body, name=name, grid=(S // tr, H),
        in_specs=_prep_specs(tr, ab_col) + _prep_out_specs(tr),
        out_specs=[hs, hs, hs, pl.BlockSpec((tr, LANES), lambda i, h: (i, 0)), row, row],
        out_shape=[hshape, hshape, hshape, jax.ShapeDtypeStruct((S, LANES), BF16), rshape, rshape],
        scratch_shapes=[pltpu.VMEM((tr, LANES), F32)],
        compiler_params=_cparams(("arbitrary", "arbitrary")),
    )(dqkv, dqkv, dqkv, proj, alog_row, dtb_row, *cots, t_inv)


def _scan_steps(states, us, ws, qds, kts, intras, decs, diff=False):
    nn, tn = (_mm_nn, _mm_tn) if diff else (lambda p, r: _bdot(p, r, _NN), lambda p, r: _bdot(p, r, _TN))
    hs = range(len(states))
    v_new = [us[h] - nn(ws[h], states[h]) for h in hs]
    o_state = [nn(qds[h], states[h]) for h in hs]
    o_intra = [nn(intras[h], v_new[h]) for h in hs]
    grown = [tn(kts[h], v_new[h]) for h in hs]
    return [o_state[h] + o_intra[h] for h in hs], [states[h] * decs[h] + grown[h] for h in hs]


def _scan_specs(rev, n):
    H = N_DELTA_HEADS

    def cix(i):
        return (n - 1 - i) if rev else i

    row = pl.BlockSpec((CHUNK, H * HEAD_DIM), lambda i: (cix(i), 0))
    return row, pl.BlockSpec((H, CHUNK, CHUNK), lambda i: (0, cix(i), 0)), \
        pl.BlockSpec((H, 1, 1, HEAD_DIM), lambda i: (0, cix(i), 0, 0)), \
        pl.BlockSpec((1, H, HEAD_DIM, HEAD_DIM), lambda i: (cix(i), 0, 0, 0))


def _delta_scan_fwd(u, w, qd, kt, intra, dec, name):
    S = u.shape[0]
    H = N_DELTA_HEADS
    n = S // CHUNK
    row, ispec, dspec, sspec = _scan_specs(False, n)

    def body(u_ref, w_ref, qd_ref, kt_ref, in_ref, dec_ref, o_ref, st_ref, s_sc):
        @pl.when(pl.program_id(0) == 0)
        def _():
            s_sc[...] = jnp.zeros_like(s_sc)

        cols = [pl.ds(h * HEAD_DIM, HEAD_DIM) for h in range(H)]
        states = [s_sc[h] for h in range(H)]
        outs, new = _scan_steps(states, [u_ref[:, cs] for cs in cols], [w_ref[:, cs] for cs in cols],
                                [qd_ref[:, cs] for cs in cols], [kt_ref[:, cs] for cs in cols],
                                [in_ref[h] for h in range(H)], [dec_ref[h, 0] for h in range(H)])
        for h, cs in enumerate(cols):
            st_ref[0, h] = states[h]
            o_ref[:, cs] = outs[h]
            s_sc[h] = new[h]

    return pl.pallas_call(
        body, name=name, grid=(n,),
        in_specs=[row, row, row, row, ispec, dspec], out_specs=[row, sspec],
        out_shape=[jax.ShapeDtypeStruct((S, H * HEAD_DIM), F32), jax.ShapeDtypeStruct((n, H, HEAD_DIM, HEAD_DIM), F32)],
        scratch_shapes=[pltpu.VMEM((H, HEAD_DIM, HEAD_DIM), F32)],
        compiler_params=_cparams(("arbitrary",)),
    )(u, w, qd, kt, intra, dec)


def _delta_scan_bwd(u, w, qd, kt, intra, dec, states, do, name):
    S = u.shape[0]
    H = N_DELTA_HEADS
    n = S // CHUNK
    row, ispec, dspec, sspec = _scan_specs(True, n)

    def body(u_ref, w_ref, qd_ref, kt_ref, in_ref, dec_ref, st_ref, do_ref,
             du_ref, dw_ref, dqd_ref, dkt_ref, din_ref, ddec_ref, ds_sc):
        @pl.when(pl.program_id(0) == 0)
        def _():
            ds_sc[...] = jnp.zeros_like(ds_sc)

        cols = [pl.ds(h * HEAD_DIM, HEAD_DIM) for h in range(H)]
        _, vjp = jax.vjp(functools.partial(_scan_steps, diff=True), [st_ref[0, h] for h in range(H)], [u_ref[:, cs] for cs in cols],
                         [w_ref[:, cs] for cs in cols], [qd_ref[:, cs] for cs in cols], [kt_ref[:, cs] for cs in cols],
                         [in_ref[h] for h in range(H)], [dec_ref[h, 0] for h in range(H)])
        dstate, du, dw, dqd, dkt, din, ddec = vjp(([do_ref[:, cs] for cs in cols], [ds_sc[h] for h in range(H)]))
        for h, cs in enumerate(cols):
            du_ref[:, cs] = du[h]
            dw_ref[:, cs] = dw[h]
            dqd_ref[:, cs] = dqd[h]
            dkt_ref[:, cs] = dkt[h]
            din_ref[h] = din[h]
            ddec_ref[h, 0] = ddec[h]
            ds_sc[h] = dstate[h]

    hshape = jax.ShapeDtypeStruct((S, H * HEAD_DIM), F32)
    return pl.pallas_call(
        body, name=name, grid=(n,),
        in_specs=[row, row, row, row, ispec, dspec, sspec, row],
        out_specs=[row, row, row, row, ispec, dspec],
        out_shape=[hshape, hshape, hshape, hshape, jax.ShapeDtypeStruct((H, S, CHUNK), F32),
                   jax.ShapeDtypeStruct((H, n, 1, HEAD_DIM), F32)],
        scratch_shapes=[pltpu.VMEM((H, HEAD_DIM, HEAD_DIM), F32)],
        compiler_params=_cparams(("arbitrary",)),
    )(u, w, qd, kt, intra, dec, states, do)


def _gated_norm(od, z, gain):
    return _rms(od, gain) * _silu(z)


def _post_fwd(od, proj, z_col0, gain, name):
    S = od.shape[0]
    H = N_DELTA_HEADS
    tr = _pick(S, 512)

    def body(od_ref, z_ref, g_ref, o_ref):
        o_ref[...] = _gated_norm(od_ref[...], z_ref[...], g_ref[...]).astype(BF16)

    return pl.pallas_call(
        body, name=name, grid=(S // tr, H),
        in_specs=[_head_spec(tr, 0), _head_spec(tr, z_col0), _gain_spec()],
        out_specs=_head_spec(tr, 0), out_shape=jax.ShapeDtypeStruct((S, H * HEAD_DIM), BF16),
        compiler_params=_cparams(("parallel", "parallel")),
    )(od, proj, gain)


def _post_bwd(od, proj, z_col0, gain, do, do_col0, name):
    S = od.shape[0]
    H = N_DELTA_HEADS
    tr = _pick(S, 512)

    def body(od_ref, z_ref, g_ref, do_ref, dod_ref, dz_ref, dg_ref):
        @pl.when((pl.program_id(0) == 0) & (pl.program_id(1) == 0))
        def _():
            dg_ref[...] = jnp.zeros_like(dg_ref)

        _, vjp = jax.vjp(_gated_norm, od_ref[...], z_ref[...], g_ref[...])
        dod, dz, dg = vjp(do_ref[...])
        dod_ref[...] = dod
        dz_ref[...] = dz.astype(BF16)
        dg_ref[...] += dg

    return pl.pallas_call(
        body, name=name, grid=(S // tr, H),
        in_specs=[_head_spec(tr, 0), _head_spec(tr, z_col0), _gain_spec(), _head_spec(tr, do_col0)],
        out_specs=[_head_spec(tr, 0), _head_spec(tr, 0), _gain_spec()],
        out_shape=[jax.ShapeDtypeStruct((S, H * HEAD_DIM), F32), jax.ShapeDtypeStruct((S, H * HEAD_DIM), BF16),
                   jax.ShapeDtypeStruct((1, HEAD_DIM), F32)],
        compiler_params=_cparams(("arbitrary", "arbitrary")),
    )(od, proj, gain, do)


def _adam_math(w, g, m, v):
    m = ADAM_B1 * m + (1.0 - ADAM_B1) * g
    v = ADAM_B2 * v + (1.0 - ADAM_B2) * (g * g)
    m_hat = m / (1.0 - ADAM_B1 ** ADAM_STEP)
    v_hat = v / (1.0 - ADAM_B2 ** ADAM_STEP)
    delta = -ADAM_LR * (m_hat / (jnp.sqrt(v_hat) + ADAM_EPS) + ADAM_WD * w)
    return delta, m, v


def _adamw(w, g, m, v, name):
    R, C = w.shape
    tr = R if R * C * 4 <= (1 << 20) else _pick8(R, max(8, (1 << 20) // (C * 4)))

    def body(w_ref, g_ref, m_ref, v_ref, d_ref, nm_ref, nv_ref):
        d, nm, nv = _adam_math(w_ref[...], g_ref[...], m_ref[...], v_ref[...])
        d_ref[...] = d
        nm_ref[...] = nm
        nv_ref[...] = nv

    spec = pl.BlockSpec((tr, C), lambda i: (i, 0))
    shp = jax.ShapeDtypeStruct((R, C), F32)
    return pl.pallas_call(
        body, name=name, grid=(R // tr,), in_specs=[spec] * 4, out_specs=[spec] * 3, out_shape=[shp] * 3,
        compiler_params=_cparams(("parallel",)),
    )(w, g, m, v)


def _adamw_halves(w, mine, theirs, col, m, v, name):
    _, R, C = w.shape
    tr = _pick8(R // 2, max(8, (1 << 20) // (C * 4)))
    nb2 = (R // 2) // tr

    def body(w_ref, a_ref, b_ref, m_ref, v_ref, g_ref, d_ref, nm_ref, nv_ref):
        top = pl.program_id(0) < nb2
        g = jnp.where(top == (lax.axis_index("c") == 0), a_ref[...], b_ref[...])
        d, nm, nv = _adam_math(w_ref[...], g, m_ref[...], v_ref[...])
        g_ref[...] = g
        d_ref[...] = d
        nm_ref[...] = nm
        nv_ref[...] = nv

    spec = pl.BlockSpec((None, tr, C), lambda i: (0, i, 0))
    half = pl.BlockSpec((tr, C), lambda i: (i % nb2, col))
    shp = jax.ShapeDtypeStruct((1, R, C), F32)
    return pl.pallas_call(
        body, name=name, grid=(2 * nb2,), in_specs=[spec, half, half, spec, spec], out_specs=[spec] * 4, out_shape=[shp] * 4,
        compiler_params=_cparams(("parallel",)),
    )(w, mine, theirs, m, v)


def _adamw_halves_t(w, mine, theirs, m, v, name):
    _, R, C = w.shape
    tc = _pick(R // 2, 256)
    nb2 = (R // 2) // tc

    def body(w_ref, a_ref, b_ref, m_ref, v_ref, g_ref, d_ref, nm_ref, nv_ref):
        left = pl.program_id(0) < nb2
        g = jnp.where(left == (lax.axis_index("c") == 0), a_ref[...], b_ref[...])
        d, nm, nv = _adam_math(w_ref[...], g, m_ref[...], v_ref[...])
        g_ref[...] = g
        d_ref[...] = d
        nm_ref[...] = nm
        nv_ref[...] = nv

    spec = pl.BlockSpec((C, tc), lambda j: (0, j))
    half = pl.BlockSpec((C, tc), lambda j: (0, j % nb2))
    shp = jax.ShapeDtypeStruct((C, R), F32)
    outs = pl.pallas_call(
        body, name=name, grid=(2 * nb2,), in_specs=[spec, half, half, spec, spec], out_specs=[spec] * 4, out_shape=[shp] * 4,
        compiler_params=_cparams(("parallel",)),
    )(jnp.transpose(w[0]), jnp.transpose(mine), jnp.transpose(theirs), jnp.transpose(m[0]), jnp.transpose(v[0]))
    return [jnp.transpose(o)[None] for o in outs]


def _pick8(dim, pref):
    t = (min(dim, pref) // 8) * 8
    while t >= 8:
        if dim % t == 0:
            return t
        t -= 8
    return dim


def _adamw_outer(w, m, v, cond_t, rhs, name):
    R, C = w.shape
    tr = _pick8(R, 128)
    nb = cond_t.shape[1]
    lhs_t = cond_t

    def body(w_ref, m_ref, v_ref, a_ref, b_ref, g_ref, d_ref, nm_ref, nv_ref):
        g = _dot(_silu(a_ref[...]), b_ref[...])
        d, nm, nv = _adam_math(w_ref[...], g, m_ref[...], v_ref[...])
        g_ref[...] = g
        d_ref[...] = d
        nm_ref[...] = nm
        nv_ref[...] = nv

    spec = pl.BlockSpec((tr, C), lambda i: (i, 0))
    shp = jax.ShapeDtypeStruct((R, C), F32)
    return pl.pallas_call(
        body, name=name, grid=(R // tr,),
        in_specs=[spec, spec, spec, pl.BlockSpec((tr, nb), lambda i: (i, 0)), pl.BlockSpec((nb, C), lambda i: (0, 0))],
        out_specs=[spec] * 4, out_shape=[shp] * 4,
        compiler_params=_cparams(("parallel",)),
    )(w, m, v, lhs_t, rhs)


def _ada_fwd(cond, w, bias, name):
    a = cond
    nb, K = a.shape
    N = w.shape[1]
    tn = _pick(N, 512)

    def body(a_ref, w_ref, b_ref, o_ref):
        o_ref[...] = _dot(_silu(a_ref[...]), w_ref[...]) + b_ref[...]

    return pl.pallas_call(
        body, name=name, grid=(N // tn,),
        in_specs=[pl.BlockSpec((nb, K), lambda j: (0, 0)), pl.BlockSpec((K, tn), lambda j: (0, j)), pl.BlockSpec((1, tn), lambda j: (0, j))],
        out_specs=pl.BlockSpec((nb, tn), lambda j: (0, j)), out_shape=jax.ShapeDtypeStruct((nb, N), F32),
        compiler_params=_cparams(("parallel",)),
    )(a, w, bias)


def _add_cast(parts, out_dtypes, name):
    shape = parts[0].shape
    G, R, C = shape
    tr = _pick8(R, max(8, (1 << 20) // (C * 4)))
    n_in = len(parts)

    def body(*refs):
        acc = refs[0][...].astype(F32)
        for r in refs[1:n_in]:
            acc = acc + r[...].astype(F32)
        for o, dt in zip(refs[n_in:], out_dtypes):
            o[...] = acc.astype(dt)

    spec = pl.BlockSpec((1, tr, C), lambda g, i: (g, i, 0))
    outs = pl.pallas_call(
        body, name=name, grid=(G, R // tr), in_specs=[spec] * n_in, out_specs=[spec] * len(out_dtypes),
        out_shape=[jax.ShapeDtypeStruct(shape, dt) for dt in out_dtypes],
        compiler_params=_cparams(("parallel", "parallel")),
    )(*parts)
    return outs


def _me():
    return lax.axis_index("x"), lax.axis_index("y"), lax.axis_index("c")


def _xor_peer(k):
    x, y, c = _me()
    dx, dy, dc = (k >> 2) & 1, (k >> 1) & 1, k & 1
    return (x ^ dx if dx else x, y ^ dy if dy else y, c ^ dc if dc else c)


ANY = pl.BlockSpec(memory_space=pl.ANY)


def _all_gather_small(v, name, after=None):
    R, C = v.shape
    extra = [] if after is None else [after]

    def body(v_ref, *rest):
        out_ref, send_sems, recv_sems = rest[len(extra):]
        x, y, c = _me()
        mine = 4 * x + 2 * y + c
        out_ref[mine] = v_ref[...]
        copies = []
        for k in range(1, 8):
            cp = pltpu.make_async_remote_copy(src_ref=v_ref, dst_ref=out_ref.at[mine], send_sem=send_sems.at[k - 1],
                                              recv_sem=recv_sems.at[k - 1], device_id=_xor_peer(k), device_id_type=MESH)
            cp.start()
            copies.append(cp)
        for k in range(1, 8):
            px, py, pc = _xor_peer(k)
            pltpu.make_async_remote_copy(src_ref=v_ref, dst_ref=out_ref.at[4 * px + 2 * py + pc], send_sem=send_sems.at[k - 1],
                                         recv_sem=recv_sems.at[k - 1], device_id=_xor_peer(k), device_id_type=MESH).wait_recv()
        for cp in copies:
            cp.wait_send()

    return pl.pallas_call(
        body, name=name, out_shape=jax.ShapeDtypeStruct((8, R, C), F32),
        in_specs=[pl.BlockSpec(memory_space=pltpu.VMEM)] + [ANY] * len(extra), out_specs=pl.BlockSpec(memory_space=pltpu.VMEM),
        scratch_shapes=[pltpu.SemaphoreType.DMA((7,)), pltpu.SemaphoreType.DMA((7,))],
        compiler_params=pltpu.CompilerParams(vmem_limit_bytes=VMEM_LIMIT),
    )(v, *extra)


def _chip_peers():
    x, y, _ = _me()
    return [(1, (x, 1 - y)), (2, (1 - x, y)), (3, (1 - x, 1 - y))]


def _all_gather_shards(shards, name):
    n = len(shards)

    def body(*refs):
        ins, outs = refs[:n], refs[n:2 * n]
        send_sems, recv_sems = refs[2 * n:]
        x, y, c = _me()
        chip = 2 * x + y
        sib = (x, y, 1 - c)
        peers = _chip_peers()
        sends = []
        for t in range(n):
            half = ins[t].shape[0] // 2
            mine = pl.ds(c * half, half)
            for p, (k, (px, py)) in enumerate(peers):
                cp = pltpu.make_async_remote_copy(src_ref=ins[t].at[mine], dst_ref=outs[t].at[chip, mine],
                                                  send_sem=send_sems.at[6 * t + p], recv_sem=recv_sems.at[6 * t + p],
                                                  device_id=(px, py, c), device_id_type=MESH)
                cp.start()
                sends.append(cp)
        for t in range(n):
            half = ins[t].shape[0] // 2
            mine = pl.ds(c * half, half)
            for p, (k, (px, py)) in enumerate(peers):
                src_chip = 2 * px + py
                landed = outs[t].at[src_chip, mine]
                pltpu.make_async_remote_copy(src_ref=landed, dst_ref=landed, send_sem=send_sems.at[6 * t + p],
                                             recv_sem=recv_sems.at[6 * t + p], device_id=(px, py, c), device_id_type=MESH).wait_recv()
                fw = pltpu.make_async_remote_copy(src_ref=landed, dst_ref=landed, send_sem=send_sems.at[6 * t + 3 + p],
                                                  recv_sem=recv_sems.at[6 * t + 3 + p], device_id=sib, device_id_type=MESH)
                fw.start()
                sends.append(fw)
        for t in range(n):
            half = ins[t].shape[0] // 2
            theirs = pl.ds((1 - c) * half, half)
            for p, (k, (px, py)) in enumerate(peers):
                got = outs[t].at[2 * px + py, theirs]
                pltpu.make_async_remote_copy(src_ref=got, dst_ref=got, send_sem=send_sems.at[6 * t + 3 + p],
                                             recv_sem=recv_sems.at[6 * t + 3 + p], device_id=sib, device_id_type=MESH).wait_recv()
        for cp in sends:
            cp.wait_send()

    return pl.pallas_call(
        body, name=name,
        out_shape=[jax.ShapeDtypeStruct((4,) + s.shape, s.dtype) for s in shards],
        in_specs=[ANY] * n, out_specs=[ANY] * n,
        scratch_shapes=[pltpu.SemaphoreType.DMA((6 * n,)), pltpu.SemaphoreType.DMA((6 * n,))],
    )(*shards)


def _swap_halves_with_sibling(slabs, name):
    n = len(slabs)

    def body(*refs):
        ins, outs = refs[:n], refs[n:2 * n]
        send_sems, recv_sems = refs[2 * n:]
        x, y, c = _me()
        sib = (x, y, 1 - c)
        cps = []
        for t in range(n):
            half = ins[t].shape[1] // 2
            cp = pltpu.make_async_remote_copy(src_ref=ins[t].at[:, pl.ds((1 - c) * half, half)], dst_ref=outs[t],
                                              send_sem=send_sems.at[t], recv_sem=recv_sems.at[t], device_id=sib, device_id_type=MESH)
            cp.start()
            cps.append(cp)
        for cp in cps:
            cp.wait()

    return pl.pallas_call(
        body, name=name,
        out_shape=[jax.ShapeDtypeStruct((4, s.shape[1] // 2, s.shape[2]), s.dtype) for s in slabs],
        in_specs=[ANY] * n, out_specs=[ANY] * n,
        scratch_shapes=[pltpu.SemaphoreType.DMA((n,)), pltpu.SemaphoreType.DMA((n,))],
    )(*slabs)


def _send_halves_to_sibling(halves, name):
    n = len(halves)

    def body(*refs):
        ins, outs = refs[:n], refs[n:2 * n]
        send_sems, recv_sems = refs[2 * n:]
        x, y, c = _me()
        sib = (x, y, 1 - c)
        cps = []
        for t in range(n):
            cp = pltpu.make_async_remote_copy(src_ref=ins[t], dst_ref=outs[t], send_sem=send_sems.at[t],
                                              recv_sem=recv_sems.at[t], device_id=sib, device_id_type=MESH)
            cp.start()
            cps.append(cp)
        for cp in cps:
            cp.wait()

    return pl.pallas_call(
        body, name=name,
        out_shape=[jax.ShapeDtypeStruct(s.shape, s.dtype) for s in halves],
        in_specs=[ANY] * n, out_specs=[ANY] * n,
        scratch_shapes=[pltpu.SemaphoreType.DMA((n,)), pltpu.SemaphoreType.DMA((n,))],
    )(*halves)


HBM_SPEC = pl.BlockSpec(memory_space=pltpu.HBM)
SEM_SPEC = pl.BlockSpec(memory_space=pltpu.SEMAPHORE)
DATAFLOW = pltpu.SideEffectType.DATAFLOW_SIDE_EFFECTING


def _plan_gather_direct(src_refs, land_refs):
    x, y, c = _me()
    chip = 2 * x + y
    plan = []
    for s, land in zip(src_refs, land_refs):
        half = s.shape[0] // 2
        for _, (px, py) in _chip_peers():
            for pc in (c, 1 - c):
                plan.append((s.at[pl.ds(c * half, half)], land.at[chip, pl.ds(c * half, half)],
                             land.at[2 * px + py, pl.ds(pc * half, half)], (px, py, pc)))
    return plan


def _plan_scatter_direct(src_refs, land_refs):
    x, y, c = _me()
    plan = []
    for s, land in zip(src_refs, land_refs):
        half = s.shape[1] // 2
        for k in range(1, 8):
            px, py, pc = _xor_peer(k)
            plan.append((s.at[2 * px + py, pl.ds(pc * half, half)], land.at[4 * x + 2 * y + c],
                         land.at[4 * px + 2 * py + pc], (px, py, pc)))
    return plan


def _plan_exchange_chips(src_refs, land_refs):
    x, y, c = _me()
    chip = 2 * x + y
    plan = []
    for s, land in zip(src_refs, land_refs):
        for _, (px, py) in _chip_peers():
            plan.append((s.at[2 * px + py], land.at[chip], land.at[2 * px + py], (px, py, c)))
    return plan


_plan_gather_direct.per_tensor = 6
_plan_scatter_direct.per_tensor = 7
_plan_exchange_chips.per_tensor = 3


def _start_copies(srcs, lands, plan_fn, name, after=None):
    n = len(srcs)
    n_copies = len(srcs) * plan_fn.per_tensor
    extra = [] if after is None else [after]

    def body(*refs):
        refs = refs[:2 * n] + refs[2 * n + len(extra):]
        send_sems, recv_sems, token = refs[2 * n], refs[2 * n + 1], refs[-1]
        for i, (src, dst, _, peer) in enumerate(plan_fn(refs[:n], refs[n:2 * n])):
            pltpu.make_async_remote_copy(src_ref=src, dst_ref=dst, send_sem=send_sems.at[i], recv_sem=recv_sems.at[i],
                                         device_id=peer, device_id_type=MESH).start()
        token[...] = jnp.zeros_like(token)

    arrays = list(srcs) + list(lands)
    outs = pl.pallas_call(
        body, name=name,
        out_shape=(pltpu.SemaphoreType.DMA((n_copies,)), pltpu.SemaphoreType.DMA((n_copies,)),
                   *[pltpu.HBM(a.shape, a.dtype) for a in arrays], jax.ShapeDtypeStruct((8, LANES), F32)),
        in_specs=[HBM_SPEC] * (2 * n) + [ANY] * len(extra),
        out_specs=(SEM_SPEC, SEM_SPEC, *[HBM_SPEC] * (2 * n), pl.BlockSpec(memory_space=pltpu.VMEM)),
        input_output_aliases={i: 2 + i for i in range(2 * n)},
        compiler_params=pltpu.CompilerParams(has_side_effects=DATAFLOW),
    )(*[pltpu.with_memory_space_constraint(a, pltpu.HBM) for a in arrays], *extra)
    return outs[0], outs[1], list(outs[2:2 + n]), list(outs[2 + n:2 + 2 * n]), outs[-1]


def _wait_copies(send_sems, recv_sems, srcs, lands, after, plan_fn, name):
    n = len(srcs)

    def body(*refs):
        send_sems, recv_sems = refs[2 * n], refs[2 * n + 1]
        for i, (src, _, arrival, peer) in enumerate(plan_fn(refs[:n], refs[n:2 * n])):
            cp = pltpu.make_async_remote_copy(src_ref=src, dst_ref=arrival, send_sem=send_sems.at[i], recv_sem=recv_sems.at[i],
                                              device_id=peer, device_id_type=MESH)
            cp.wait_send()
            cp.wait_recv()

    arrays = list(srcs) + list(lands)
    outs = pl.pallas_call(
        body, name=name,
        out_shape=tuple(pltpu.HBM(a.shape, a.dtype) for a in arrays),
        in_specs=[HBM_SPEC] * (2 * n) + [SEM_SPEC, SEM_SPEC, ANY],
        out_specs=tuple([HBM_SPEC] * (2 * n)),
        input_output_aliases={i: i for i in range(2 * n)},
        compiler_params=pltpu.CompilerParams(has_side_effects=DATAFLOW),
    )(*arrays, send_sems, recv_sems, after)
    return list(outs[n:])


def _rope_tables(positions):
    half = ROPE_DIM // 2
    S = positions.shape[0]
    inv_freq = ROPE_THETA ** (-jnp.arange(half, dtype=F32) / half)
    ang = positions.astype(F32)[:, None] * inv_freq
    cos, sin = jnp.cos(ang), jnp.sin(ang)
    zeros = functools.partial(jnp.zeros, dtype=F32)
    cc = jnp.concatenate([cos, cos, jnp.ones((S, HEAD_DIM - ROPE_DIM), F32)], axis=1)
    s1 = jnp.concatenate([-sin, zeros((S, HEAD_DIM - half))], axis=1)
    s2 = jnp.concatenate([zeros((S, half)), sin, zeros((S, HEAD_DIM - ROPE_DIM))], axis=1)
    return cc, s1, s2


def _ffn_fwd(x, gain, shift, scale, gate, w_gu, w_d, fs, tag):
    S = x.shape[0]
    tm = _pick(S, 512)
    h = _pre_fwd(x, gain, shift, scale, tag + "_pre")

    def swiglu_out(tile, ins, outs):
        outs[0][...] = tile
        outs[1][...] = _swiglu_fn(tile[:, :fs], tile[:, fs:]).astype(BF16)

    ab, s = _matmul(h, w_gu, "nn", F32, tag + "_gate_up", tm=tm, tn=2 * fs, tk=4096, epilogue=(swiglu_out, [], [
        ((S, 8 * fs), F32, (tm, 2 * fs), lambda i, j: (i, j)), ((S, 4 * fs), BF16, (tm, fs), lambda i, j: (i, j))]))
    f, xn = _matmul_residual(s, w_d, x, gate, 0.5, tag + "_down")
    return xn, (x, h, ab, s, f)


def _ffn_bwd(dxn, saved, gain, shift, scale, gate, w_gu, w_d, fs, tag, on_dw_d=None, on_dw_gu=None, dw_gu_dtype=F32):
    x, h, ab, s, f = saved
    df, dgate = _residual_bwd(gate, f, dxn, 0.5, tag + "_res_bwd")
    S = x.shape[0]
    tm = _pick(S, 512)
    dw_d = _matmul(s, df, "tn", BF16, tag + "_down_dw", tm=1408, tn=1024, tk=2048)
    if on_dw_d is not None:
        shift = shift + on_dw_d(dw_d)

    def swiglu_back(tile, ins, outs):
        _, vjp = jax.vjp(_swiglu_fn, ins[0][:, :fs], ins[0][:, fs:])
        da, db = vjp(tile)
        outs[0][:, :fs] = da.astype(BF16)
        outs[0][:, fs:] = db.astype(BF16)

    (dab,) = _matmul(df, w_d, "nt", F32, tag + "_down_dx", tm=tm, tn=fs, tk=4096, epilogue=(
        swiglu_back, [(ab, (tm, 2 * fs), lambda i, j: (i, j))], [((S, 8 * fs), BF16, (tm, 2 * fs), lambda i, j: (i, j))]))
    dw_gu = _matmul(h, dab, "tn", dw_gu_dtype, tag + "_gate_up_dw", tm=512, tn=2 * fs, tk=2048, col_slabs=True)
    tie = on_dw_gu(dw_gu) if on_dw_gu is not None else None
    dh = _matmul(dab, w_gu, "nt", F32, tag + "_gate_up_dx", tm=1024, tn=1024, tk=2816, after=tie)
    dx, dgain, dshift, dscale = _pre_bwd(x, gain, shift, scale, dh, dxn, tag + "_pre_bwd")
    return dx, dw_gu, dw_d, dgain, dshift, dscale, dgate


def _flat_pad(parts, rows, cols):
    flat = jnp.concatenate([p.reshape(-1).astype(F32) for p in parts])
    return jnp.pad(flat, (0, rows * cols - flat.shape[0])).reshape(rows, cols)


def _cols_to_slabs(w, n):
    R, NC = w.shape
    return jnp.transpose(w.reshape(R, n, NC // n), (1, 0, 2))


def kernel(x, c, positions, w_ada, b_ada, ffn1_norm, ffn1_w_gate, ffn1_w_up, ffn1_w_down, mix_norm, w_in, conv_w, q_norm, k_norm, a_log, dt_bias, delta_out_norm, w_out, ffn2_norm, ffn2_w_gate, ffn2_w_up, ffn2_w_down, loss_target, m_w_ada, m_b_ada, m_ffn1_norm, m_ffn1_w_gate, m_ffn1_w_up, m_ffn1_w_down, m_mix_norm, m_w_in, m_conv_w, m_q_norm, m_k_norm, m_a_log, m_dt_bias, m_delta_out_norm, m_w_out, m_ffn2_norm, m_ffn2_w_gate, m_ffn2_w_up, m_ffn2_w_down, v_w_ada, v_b_ada, v_ffn1_norm, v_ffn1_w_gate, v_ffn1_w_up, v_ffn1_w_down, v_mix_norm, v_w_in, v_conv_w, v_q_norm, v_k_norm, v_a_log, v_dt_bias, v_delta_out_norm, v_w_out, v_ffn2_norm, v_ffn2_w_gate, v_ffn2_w_up, v_ffn2_w_down):
    xi, yi, ci = _me()
    chip = 2 * xi + yi
    dev = 2 * chip + ci
    xs = x[0]
    S, D = xs.shape
    HA, HD = N_ATTN_HEADS, N_DELTA_HEADS
    fs = ffn1_w_gate.shape[2]
    n_mod_shard = w_ada.shape[2]
    in_shard = w_in.shape[2]
    in_width = 4 * in_shard
    in_pad = -(-in_width // LANES) * LANES
    conv_shard = conv_w.shape[2]
    conv_width = 4 * conv_shard

    pack0 = jnp.zeros((8, max(D, conv_shard)), F32)
    pack0 = pack0.at[0, :D].set(c[0]).at[1:1 + CONV_WIDTH, :conv_shard].set(conv_w[0])
    got0 = _all_gather_small(pack0, "gather_cond")
    c_all = got0[:, 0, :D]
    conv_full = jnp.transpose(got0[::2, 1:1 + CONV_WIDTH, :conv_shard], (1, 0, 2)).reshape(CONV_WIDTH, conv_width)
    b_ada_mine = lax.dynamic_slice(b_ada, (0, chip * n_mod_shard), (1, n_mod_shard))
    mod_part = _ada_fwd(c_all, w_ada[0], b_ada_mine, "ada_fwd")
    got1 = _all_gather_small(mod_part, "gather_mod")
    mod = lax.dynamic_index_in_dim(got1[::2], dev, axis=1, keepdims=False).reshape(1, 4 * n_mod_shard)
    sh1, sc1, gt1, sh2, sc2, gt2, sh3, sc3, gt3 = [mod[:, i * D:(i + 1) * D] for i in range(N_MOD)]

    shards = [w[0].astype(BF16) for w in (ffn1_w_gate, ffn1_w_up, ffn1_w_down, w_in, w_out, ffn2_w_gate, ffn2_w_up, ffn2_w_down)]
    gathered = _all_gather_shards(shards[:3], "gather_weights")
    g1g, g1u, g1d = [lax.dynamic_update_index_in_dim(g, s, chip, 0) for g, s in zip(gathered, shards[:3])]
    zones = [lax.dynamic_update_index_in_dim(lax.empty((4,) + s.shape, BF16), s, chip, 0) for s in shards[3:]]
    ag_in = _start_copies(shards[3:4], zones[:1], _plan_gather_direct, "gather_in_start")
    sh1 = sh1 + ag_in[4][0, 0]

    def gate_up(gg, gu):
        return jnp.transpose(jnp.concatenate([gg, gu], axis=2), (1, 0, 2)).reshape(D, 8 * fs)

    w_gu1, w_d1 = gate_up(g1g, g1u), g1d.reshape(4 * fs, D)

    x1, saved1 = _ffn_fwd(xs, ffn1_norm, sh1, sc1, gt1, w_gu1, w_d1, fs, "ffn1")
    (gin,) = _wait_copies(ag_in[0], ag_in[1], ag_in[2], ag_in[3], x1, _plan_gather_direct, "gather_in_wait")
    w_in_f = jnp.pad(jnp.transpose(gin, (1, 0, 2)).reshape(D, in_width), ((0, 0), (0, in_pad - in_width)))
    ag_rest = _start_copies(shards[4:], zones[1:], _plan_gather_direct, "gather_rest_start", after=gin)
    sh2 = sh2 + ag_rest[4][0, 0]

    cc, s1, s2 = _rope_tables(positions[0])
    alog_row = jnp.pad(a_log, ((0, 0), (0, LANES - HD)))
    dtb_row = jnp.pad(dt_bias, ((0, 0), (0, LANES - HD)))
    col_k, col_v, col_d, col_z, col_ab = HA, 2 * HA, 3 * HA, 3 * HA + 3 * HD, 3 * HA + 4 * HD
    h2 = _pre_fwd(x1, mix_norm, sh2, sc2, "mix_pre")
    proj = _matmul(h2, w_in_f, "nn", F32, "mix_in_proj", tm=512, tn=2432, tk=4096)
    attn_bias = _log_multiplicity_table(_pick(S, ATTN_TILE))
    qa, ka = _attn_prep_fwd(proj, q_norm, k_norm, cc, s1, s2, "attn_prep")
    o_near, lse_near = _attn_fwd(qa, ka, proj, col_v, attn_bias, "attn_fwd")
    oa, oa_b, lse = _attn_far_fwd(qa, ka, proj, col_v, o_near, lse_near, "attn_far_fwd")
    dqkv = _conv_fwd(proj, col_d, conv_width, conv_full, "conv_fwd")
    *prep, t_inv = _delta_prep_fwd(dqkv, proj, col_ab, alog_row, dtb_row, "delta_prep")
    od_raw, states = _delta_scan_fwd(*prep, "delta_scan")
    od = _post_fwd(od_raw, proj, col_z, delta_out_norm, "delta_post")
    o = jnp.concatenate([oa_b, od], axis=1)
    gout, g2g, g2u, g2d = _wait_copies(ag_rest[0], ag_rest[1], ag_rest[2], ag_rest[3], o, _plan_gather_direct, "gather_rest_wait")
    w_out_f = gout.reshape(-1, D)
    w_gu2, w_d2 = gate_up(g2g, g2u), g2d.reshape(4 * fs, D)
    mo, x2 = _matmul_residual(o, w_out_f, x1, gt2, 1.0, "mix_out_proj")

    x3, saved3 = _ffn_fwd(x2, ffn2_norm, sh3, sc3, gt3, w_gu2, w_d2, fs, "ffn2")
    loss_part, dy = _loss_head(x3, loss_target[0], "loss_head")
    loss = lax.psum(loss_part[0, 0], ("x", "y", "c"))

    dx2, dw_gu2, dw_d2, dgain3, dsh3, dsc3, dgt3 = _ffn_bwd(dy, saved3, ffn2_norm, sh3, sc3, gt3, w_gu2, w_d2, fs, "ffn2",
                                                            dw_gu_dtype=BF16)

    def scatter_start(slabs32, name):
        slabs16 = [s.astype(BF16) for s in slabs32]
        zones = []
        for s in slabs16:
            half = s.shape[1] // 2
            own = lax.dynamic_slice(s, (chip, ci * half, 0), (1, half, s.shape[2]))
            zones.append(lax.dynamic_update_slice(lax.empty((8, half, s.shape[2]), BF16), own, (dev, 0, 0)))
        return _start_copies(slabs16, zones, _plan_scatter_direct, name)

    rs_ffn2 = scatter_start([dw_gu2, dw_d2.reshape(4, fs, D)], "rs_ffn2_start")
    dmo, dgt2 = _residual_bwd(gt2 + rs_ffn2[4][0, 0], mo, dx2, 1.0, "mix_res_bwd")
    do = _matmul(dmo, w_out_f, "nt", F32, "mix_out_dx", tm=1024, tn=1024, tk=4096)
    dw_out = _matmul(o, dmo, "tn", BF16, "mix_out_dw", tm=1024, tn=1024, tk=2048)
    dq = _attn_bwd_dq(qa, ka, proj, col_v, oa, lse, do, 0, attn_bias, "attn_bwd_dq")
    dk, dv = _attn_bwd_dkv(qa, ka, proj, col_v, oa, lse, do, 0, attn_bias, "attn_bwd_dkv")
    far = _attn_far_bwd(qa, ka, proj, col_v, oa, lse, do, 0, "attn_far_bwd")
    dpq, dpk, dv, dq_gain, dk_gain = _attn_prep_bwd(proj, q_norm, k_norm, cc, s1, s2, (dq, dk, dv), far, "attn_prep_bwd")
    dod, dz, ddn_gain = _post_bwd(od_raw, proj, col_z, delta_out_norm, do, HA, "delta_post_bwd")
    cots = _delta_scan_bwd(*prep, states, dod, "delta_scan_bwd")
    ddq, ddk, ddv, dab, dalog, ddtb = _delta_prep_bwd(dqkv, proj, col_ab, alog_row, dtb_row, cots, t_inv, "delta_prep_bwd")
    dconv_in, dconv_w = _conv_bwd(proj, col_d, conv_width, conv_full, jnp.concatenate([ddq, ddk, ddv], axis=1), "conv_bwd")
    dproj = jnp.concatenate([dpq, dpk, dv, dconv_in, dz, dab], axis=1)
    dh2 = _matmul(dproj, w_in_f, "nt", F32, "mix_in_dx", tm=1024, tn=1024, tk=2816)
    dw_in = _matmul(h2, dproj, "tn", BF16, "mix_in_dw", tm=512, tn=2432, tk=2048)
    rs_mix = scatter_start([_cols_to_slabs(dw_in[:, :in_width], 4), dw_out.reshape(4, -1, D)], "rs_mix_start")
    dx1, dgain2, dsh2, dsc2 = _pre_bwd(x1, mix_norm, sh2, sc2, dh2, dx2, "mix_pre_bwd")

    rs_ffn1d = []

    def send_dw_d1(dw_d):
        rs_ffn1d.extend(scatter_start([dw_d.reshape(4, fs, D)], "rs_ffn1d_start"))
        return rs_ffn1d[4][0, 0]

    rs_gu1 = []

    def send_dw_gu1(slab):
        (other,) = _swap_halves_with_sibling([slab], "rs_sibling_swap")
        half = slab.shape[1] // 2
        mine = lax.dynamic_slice_in_dim(slab, ci * half, half, axis=1)
        p32, p16 = _add_cast([mine, other], [F32, BF16], "rs_chip_sum")
        rs_gu1.append(p32)
        rs_gu1.extend(_start_copies([p16], [lax.empty(p16.shape, BF16)], _plan_exchange_chips, "rs_chip_exchange_start"))
        return rs_gu1[5]

    dx0, dw_gu1, dw_d1, dgain1, dsh1, dsc1, dgt1 = _ffn_bwd(dx1, saved1, ffn1_norm, sh1, sc1, gt1 + rs_mix[4][0, 0], w_gu1, w_d1, fs,
                                                            "ffn1", on_dw_d=send_dw_d1, on_dw_gu=send_dw_gu1)

    (got,) = _wait_copies(rs_gu1[1], rs_gu1[2], rs_gu1[3], rs_gu1[4], dx0, _plan_exchange_chips, "rs_chip_exchange_wait")
    parts = [lax.dynamic_index_in_dim(rs_gu1[0], chip, axis=0, keepdims=True)]
    parts += [lax.dynamic_index_in_dim(got, (chip + k) % 4, axis=0, keepdims=True) for k in (1, 2, 3)]
    halves = [_add_cast(parts, [F32], "rs_total_0")[0][0]]
    arrived = _wait_copies(rs_ffn1d[0], rs_ffn1d[1], rs_ffn1d[2], rs_ffn1d[3], dx0, _plan_scatter_direct, "rs_ffn1d_wait")
    arrived += _wait_copies(rs_mix[0], rs_mix[1], rs_mix[2], rs_mix[3], dx0, _plan_scatter_direct, "rs_mix_wait")
    arrived += _wait_copies(rs_ffn2[0], rs_ffn2[1], rs_ffn2[2], rs_ffn2[3], dx0, _plan_scatter_direct, "rs_ffn2_wait")
    for t, zone in enumerate(arrived):
        halves.append(_add_cast([zone[d:d + 1] for d in range(8)], [F32], "rs_total_%d" % (t + 2))[0][0])
    theirs = _send_halves_to_sibling(halves, "rs_sibling_join")
    h_gu1, h_d1, h_in, h_out, h_gu2, h_d2 = zip(halves, theirs)

    n_small = N_MOD * D + 3 * D + 5 * LANES + CONV_WIDTH * conv_width
    cols_small = -(-n_small // (8 * LANES)) * LANES
    small = _flat_pad([dsh1, dsc1, dgt1, dsh2, dsc2, dgt2, dsh3, dsc3, dgt3, dgain1, dgain2, dgain3,
                       dq_gain, dk_gain, dalog, ddtb, ddn_gain, dconv_w], 8, cols_small)
    got2 = _all_gather_small(small, "gather_small_grads", after=theirs[0])
    small_sum = _add_cast([got2[d:d + 1] for d in range(8)], [F32], "sum_small_grads")[0].reshape(-1)
    dmod_all = got2.reshape(8, -1)[:, :N_MOD * D]
    off = [0]

    def take(n):
        off[0] += n
        return small_sum[off[0] - n:off[0]]

    g_b_ada = take(N_MOD * D).reshape(1, -1)
    g_ffn1_norm, g_mix_norm, g_ffn2_norm = take(D).reshape(1, D), take(D).reshape(1, D), take(D).reshape(1, D)
    g_q_norm, g_k_norm = take(LANES).reshape(1, -1), take(LANES).reshape(1, -1)
    g_a_log, g_dt_bias = take(LANES)[:HD].reshape(1, HD), take(LANES)[:HD].reshape(1, HD)
    g_dn = take(LANES).reshape(1, -1)
    g_conv_full = take(CONV_WIDTH * conv_width).reshape(CONV_WIDTH, conv_width)
    g_conv = lax.dynamic_slice(g_conv_full, (0, chip * conv_shard), (CONV_WIDTH, conv_shard))

    res = {}

    def upd(name, w, pair, col, m, v):
        res[name] = tuple(_adamw_halves(w, pair[0], pair[1], col, m, v, "adamw_" + name))

    upd("ffn1_w_gate", ffn1_w_gate, h_gu1, 0, m_ffn1_w_gate, v_ffn1_w_gate)
    upd("ffn1_w_up", ffn1_w_up, h_gu1, 1, m_ffn1_w_up, v_ffn1_w_up)
    upd("ffn1_w_down", ffn1_w_down, h_d1, 0, m_ffn1_w_down, v_ffn1_w_down)
    res["w_in"] = tuple(_adamw_halves_t(w_in, h_in[0], h_in[1], m_w_in, v_w_in, "adamw_w_in"))
    upd("w_out", w_out, h_out, 0, m_w_out, v_w_out)
    upd("ffn2_w_gate", ffn2_w_gate, h_gu2, 0, m_ffn2_w_gate, v_ffn2_w_gate)
    upd("ffn2_w_up", ffn2_w_up, h_gu2, 1, m_ffn2_w_up, v_ffn2_w_up)
    upd("ffn2_w_down", ffn2_w_down, h_d2, 0, m_ffn2_w_down, v_ffn2_w_down)
    d_cv, nm_cv, nv_cv = _adamw(conv_w[0], g_conv, m_conv_w[0], v_conv_w[0], "adamw_conv_w")
    res["conv_w"] = (g_conv[None], d_cv[None], nm_cv[None], nv_cv[None])

    dmod_mine = lax.dynamic_slice(dmod_all, (0, chip * n_mod_shard), (8, n_mod_shard))
    g, d, nm, nv = _adamw_outer(w_ada[0], m_w_ada[0], v_w_ada[0], jnp.transpose(c_all), dmod_mine, "adamw_w_ada")
    res["w_ada"] = (g[None], d[None], nm[None], nv[None])

    rep = [("b_ada", b_ada, g_b_ada, m_b_ada, v_b_ada), ("ffn1_norm", ffn1_norm, g_ffn1_norm, m_ffn1_norm, v_ffn1_norm),
           ("mix_norm", mix_norm, g_mix_norm, m_mix_norm, v_mix_norm), ("ffn2_norm", ffn2_norm, g_ffn2_norm, m_ffn2_norm, v_ffn2_norm),
           ("q_norm", q_norm, g_q_norm, m_q_norm, v_q_norm), ("k_norm", k_norm, g_k_norm, m_k_norm, v_k_norm),
           ("a_log", a_log, g_a_log, m_a_log, v_a_log), ("dt_bias", dt_bias, g_dt_bias, m_dt_bias, v_dt_bias),
           ("delta_out_norm", delta_out_norm, g_dn, m_delta_out_norm, v_delta_out_norm)]
    n_rep = sum(-(-r[1].shape[1] // LANES) * LANES for r in rep)
    cols_rep = -(-n_rep // (8 * LANES)) * LANES

    def pack_rep(idx):
        return _flat_pad([jnp.pad(r[idx], ((0, 0), (0, -r[idx].shape[1] % LANES))) for r in rep], 8, cols_rep)

    d_rep, nm_rep, nv_rep = [a.reshape(-1) for a in _adamw(pack_rep(1), pack_rep(2), pack_rep(3), pack_rep(4), "adamw_small")]
    o2 = 0
    for name, w, g, _, _ in rep:
        n = w.shape[1]
        res[name] = (g, d_rep[o2:o2 + n].reshape(1, n), nm_rep[o2:o2 + n].reshape(1, n), nv_rep[o2:o2 + n].reshape(1, n))
        o2 += -(-n // LANES) * LANES

    order = ["w_ada", "b_ada", "ffn1_norm", "ffn1_w_gate", "ffn1_w_up", "ffn1_w_down", "mix_norm", "w_in", "conv_w", "q_norm",
             "k_norm", "a_log", "dt_bias", "delta_out_norm", "w_out", "ffn2_norm", "ffn2_w_gate", "ffn2_w_up", "ffn2_w_down"]
    return (loss, dx0[None], *[res[n][0] for n in order], *[res[n][1] for n in order],
            *[res[n][2] for n in order], *[res[n][3] for n in order])
```

```python
import functools
import math

import jax
import jax.numpy as jnp
from jax import lax
from jax.experimental import pallas as pl
from jax.experimental.pallas import tpu as pltpu

F32 = jnp.float32
BF16 = jnp.bfloat16
MESH = pl.DeviceIdType.MESH

HEAD_DIM = 128
N_ATTN_HEADS = 8
N_DELTA_HEADS = 8
DILATED_PATTERNS = ((128, 1), (512, 4), (2048, 16))
ROPE_THETA = 500000.0
ROPE_DIM = HEAD_DIM // 4
CONV_WIDTH = 4
CHUNK = 64
NORM_EPS = 1e-6
N_MOD = 9
ADAM_LR = 0.001
ADAM_B1 = 0.9
ADAM_B2 = 0.999
ADAM_EPS = 1e-08
ADAM_WD = 0.01
ADAM_STEP = 10

LANES = 128
VMEM_LIMIT = 56 * 1024 * 1024
ATTN_TILE = 512
HIGHEST = lax.Precision.HIGHEST


def _cparams(sem=None):
    return pltpu.CompilerParams(dimension_semantics=sem, vmem_limit_bytes=VMEM_LIMIT)


def _pick(dim, pref):
    if dim <= pref:
        return dim
    t = (pref // LANES) * LANES
    while t >= LANES:
        if dim % t == 0:
            return t
        t -= LANES
    return dim


def _sigmoid(x):
    return 1.0 / (1.0 + jnp.exp(-x))


def _silu(x):
    return x * _sigmoid(x)


def _softplus(x):
    return jnp.maximum(x, 0.0) + jnp.log(1.0 + jnp.exp(-jnp.abs(x)))


def _rms(x, gain):
    return x * lax.rsqrt(jnp.mean(x * x, axis=-1, keepdims=True) + NORM_EPS) * gain


def _l2(x):
    return x * lax.rsqrt(jnp.sum(x * x, axis=-1, keepdims=True) + NORM_EPS)


def _modulate(x, gain, shift, scale):
    return _rms(x, gain) * (1.0 + scale) + shift


def _dot(a, b):
    return lax.dot_general(a, b, (((1,), (0,)), ((), ())), precision=HIGHEST, preferred_element_type=F32)


def _bdot(a, b, dims):
    return lax.dot_general(a.astype(BF16), b.astype(BF16), (dims, ((), ())), preferred_element_type=F32)


_NN, _NT, _TN = ((1,), (0,)), ((1,), (1,)), ((0,), (0,))
HIGH = lax.Precision.HIGH


def _dot3(a, b, dims=_NN):
    return lax.dot_general(a, b, (dims, ((), ())), precision=HIGH, preferred_element_type=F32)


@jax.custom_vjp
def _mm_nn(a, b):
    return _bdot(a, b, _NN)


_mm_nn.defvjp(lambda a, b: (_bdot(a, b, _NN), (a, b)),
              lambda res, g: (_bdot(g, res[1], _NT), _bdot(res[0], g, _TN)))


@jax.custom_vjp
def _mm_nt(a, b):
    return _bdot(a, b, _NT)


_mm_nt.defvjp(lambda a, b: (_bdot(a, b, _NT), (a, b)),
              lambda res, g: (_bdot(g, res[1], _NN), _bdot(g, res[0], _TN)))


@jax.custom_vjp
def _mm_tn(a, b):
    return _bdot(a, b, _TN)


_mm_tn.defvjp(lambda a, b: (_bdot(a, b, _TN), (a, b)),
              lambda res, g: (_bdot(res[1], g, _NT), _bdot(res[0], g, _NN)))


@jax.custom_vjp
def _tri_inv_saved(a, t_inv):
    return t_inv


_tri_inv_saved.defvjp(lambda a, t_inv: (t_inv, t_inv),
                      lambda t_inv, g: (-_dot3(t_inv, _dot3(g, t_inv, _NT), _TN), jnp.zeros_like(t_inv)))


_MM_DIMS = {"nn": ((1,), (0,)), "nt": ((1,), (1,)), "tn": ((0,), (0,))}


def _matmul(a, b, mode, out_dtype, name, tm=1024, tn=1024, tk=1024, col_slabs=False, epilogue=None, after=None,
            b_stationary=False):
    if mode == "nn":
        (M, K), (_, N) = a.shape, b.shape
    elif mode == "nt":
        (M, K), (N, _) = a.shape, b.shape
    else:
        (K, M), (_, N) = a.shape, b.shape
    tm, tn, tk = _pick(M, tm), _pick(N, tn), _pick(K, tk)
    nk = K // tk
    dims = _MM_DIMS[mode]
    epi_fn, extra_in, outs = epilogue if epilogue is not None else (None, [], None)
    if outs is None:
        if col_slabs:
            outs = [((N // tn, M, tn), out_dtype, (None, tm, tn), lambda i, j: (j, i, 0))]
        else:
            outs = [((M, N), out_dtype, (tm, tn), lambda i, j: (i, j))]
    n_in, n_out = len(extra_in), len(outs)

    tied = [] if after is None else [after]

    def body(a_ref, b_ref, *rest):
        rest = rest[len(tied):]
        in_refs, out_refs = rest[:n_in], rest[n_in:n_in + n_out]
        k = pl.program_id(2)
        p = _bdot(a_ref[...], b_ref[...], dims)

        def finish(tile):
            if epi_fn is None:
                out_refs[0][...] = tile.astype(out_dtype)
            else:
                epi_fn(tile, in_refs, out_refs)

        if nk == 1:
            finish(p)
        else:
            acc_ref = rest[-1]

            @pl.when(k == 0)
            def _():
                acc_ref[...] = p

            @pl.when((k > 0) & (k < nk - 1))
            def _():
                acc_ref[...] += p

            @pl.when(k == nk - 1)
            def _():
                finish(acc_ref[...] + p)

    def order(f):
        return (lambda g0, g1, k: f(g1, g0, k)) if b_stationary else f

    def ij(index_map):
        return order(lambda i, j, k: index_map(i, j))

    a_spec = pl.BlockSpec((tk, tm), order(lambda i, j, k: (k, i))) if mode == "tn" else pl.BlockSpec((tm, tk), order(lambda i, j, k: (i, k)))
    b_spec = pl.BlockSpec((tn, tk), order(lambda i, j, k: (j, k))) if mode == "nt" else pl.BlockSpec((tk, tn), order(lambda i, j, k: (k, j)))
    res = pl.pallas_call(
        body, name=name, grid=(N // tn, M // tm, nk) if b_stationary else (M // tm, N // tn, nk),
        in_specs=[a_spec, b_spec] + [pl.BlockSpec(memory_space=pl.ANY)] * len(tied) + [pl.BlockSpec(blk, ij(im)) for _, blk, im in extra_in],
        out_specs=[pl.BlockSpec(blk, ij(im)) for _, _, blk, im in outs],
        out_shape=[jax.ShapeDtypeStruct(shp, dt) for shp, dt, _, _ in outs],
        scratch_shapes=[pltpu.VMEM((tm, tn), F32)] if nk > 1 else [],
        compiler_params=_cparams(("parallel", "parallel", "arbitrary")),
    )(a, b, *tied, *[arr for arr, _, _ in extra_in])
    return res if epilogue is not None else res[0]


def _row_spec(tr, d):
    return pl.BlockSpec((tr, d), lambda i: (i, 0))


def _vec_spec(d):
    return pl.BlockSpec((1, d), lambda i: (0, 0))


def _pre_fwd(x, gain, shift, scale, name):
    S, D = x.shape
    tr = _pick(S, 256)

    def body(x_ref, g_ref, sh_ref, sc_ref, h_ref):
        h_ref[...] = _modulate(x_ref[...], g_ref[...], sh_ref[...], sc_ref[...]).astype(BF16)

    return pl.pallas_call(
        body, name=name, grid=(S // tr,),
        in_specs=[_row_spec(tr, D), _vec_spec(D), _vec_spec(D), _vec_spec(D)],
        out_specs=_row_spec(tr, D), out_shape=jax.ShapeDtypeStruct((S, D), BF16),
        compiler_params=_cparams(("parallel",)),
    )(x, gain, shift, scale)


def _pre_bwd(x, gain, shift, scale, dh, dx_in, name):
    S, D = x.shape
    tr = _pick(S, 256)

    def body(x_ref, g_ref, sh_ref, sc_ref, dh_ref, dxin_ref, dx_ref, dg_ref, dsh_ref, dsc_ref):
        _, vjp = jax.vjp(_modulate, x_ref[...], g_ref[...], sh_ref[...], sc_ref[...])
        dx, dg, dsh, dsc = vjp(dh_ref[...])
        dx_ref[...] = dxin_ref[...] + dx

        @pl.when(pl.program_id(0) == 0)
        def _():
            dg_ref[...] = jnp.zeros_like(dg_ref)
            dsh_ref[...] = jnp.zeros_like(dsh_ref)
            dsc_ref[...] = jnp.zeros_like(dsc_ref)

        dg_ref[...] += dg
        dsh_ref[...] += dsh
        dsc_ref[...] += dsc

    vec = jax.ShapeDtypeStruct((1, D), F32)
    return pl.pallas_call(
        body, name=name, grid=(S // tr,),
        in_specs=[_row_spec(tr, D), _vec_spec(D), _vec_spec(D), _vec_spec(D), _row_spec(tr, D), _row_spec(tr, D)],
        out_specs=[_row_spec(tr, D), _vec_spec(D), _vec_spec(D), _vec_spec(D)],
        out_shape=[jax.ShapeDtypeStruct((S, D), F32), vec, vec, vec],
        compiler_params=_cparams(("arbitrary",)),
    )(x, gain, shift, scale, dh, dx_in)


def _matmul_residual(a, b, x, gate, coef, name):
    S, D = x.shape
    tm, tn = _pick(S, 512), _pick(D, 512)

    def residual_out(tile, ins, outs):
        outs[0][...] = tile
        outs[1][...] = ins[0][...] + coef * ins[1][...] * tile

    blk = ((tm, tn), lambda i, j: (i, j))
    return _matmul(a, b, "nn", F32, name, tm=tm, tn=tn, tk=8192, epilogue=(
        residual_out, [(x,) + blk, (gate, (1, tn), lambda i, j: (0, j))], [((S, D), F32) + blk, ((S, D), F32) + blk]))


def _residual_bwd(gate, f, dxn, coef, name):
    S, D = f.shape
    tr = _pick(S, 256)

    def body(g_ref, f_ref, d_ref, df_ref, dg_ref):
        d = d_ref[...]
        df_ref[...] = (coef * g_ref[...] * d).astype(BF16)

        @pl.when(pl.program_id(0) == 0)
        def _():
            dg_ref[...] = jnp.zeros_like(dg_ref)

        dg_ref[...] += jnp.sum(coef * f_ref[...] * d, axis=0, keepdims=True)

    return pl.pallas_call(
        body, name=name, grid=(S // tr,),
        in_specs=[_vec_spec(D), _row_spec(tr, D), _row_spec(tr, D)],
        out_specs=[_row_spec(tr, D), _vec_spec(D)],
        out_shape=[jax.ShapeDtypeStruct((S, D), BF16), jax.ShapeDtypeStruct((1, D), F32)],
        compiler_params=_cparams(("arbitrary",)),
    )(gate, f, dxn)


def _swiglu_fn(a, b):
    return _silu(a) * b


def _loss_head(y, target, name):
    S, D = y.shape
    tr = _pick(S, 256)

    def body(y_ref, t_ref, l_ref, dy_ref):
        e = y_ref[...] - t_ref[...]
        dy_ref[...] = e * (1.0 / D)

        @pl.when(pl.program_id(0) == 0)
        def _():
            l_ref[...] = jnp.zeros_like(l_ref)

        l_ref[...] += jnp.sum(jnp.sum(e * e, axis=-1, keepdims=True), axis=0, keepdims=True) * (0.5 / D)

    return pl.pallas_call(
        body, name=name, grid=(S // tr,),
        in_specs=[_row_spec(tr, D), _row_spec(tr, D)],
        out_specs=[pl.BlockSpec((1, 1), lambda i: (0, 0)), _row_spec(tr, D)],
        out_shape=[jax.ShapeDtypeStruct((1, 1), F32), jax.ShapeDtypeStruct((S, D), F32)],
        compiler_params=_cparams(("arbitrary",)),
    )(y, target)


def _rope(y, cc, s1, s2):
    return y * cc + pltpu.roll(y, LANES - ROPE_DIM // 2, 1) * s1 + pltpu.roll(y, ROPE_DIM // 2, 1) * s2


def _rope_t(d, cc, s1, s2):
    return d * cc + pltpu.roll(d * s1, ROPE_DIM // 2, 1) + pltpu.roll(d * s2, LANES - ROPE_DIM // 2, 1)


def _head_spec(tr, col0):
    return pl.BlockSpec((tr, HEAD_DIM), lambda i, h: (i, col0 + h))


def _tab_spec(tr):
    return pl.BlockSpec((tr, HEAD_DIM), lambda i, h: (i, 0))


def _gain_spec():
    return pl.BlockSpec((1, HEAD_DIM), lambda i, h: (0, 0))


def _attn_prep_fwd(proj, q_gain, k_gain, cc, s1, s2, name):
    S = proj.shape[0]
    H = N_ATTN_HEADS
    tr = _pick(S, 512)

    def body(q_ref, k_ref, qg_ref, kg_ref, cc_ref, s1_ref, s2_ref, qo_ref, ko_ref):
        cc, s1, s2 = cc_ref[...], s1_ref[...], s2_ref[...]
        qo_ref[...] = _rope(_rms(q_ref[...], qg_ref[...]), cc, s1, s2)
        ko_ref[...] = _rope(_rms(k_ref[...], kg_ref[...]), cc, s1, s2)

    out = jax.ShapeDtypeStruct((S, H * HEAD_DIM), F32)
    return pl.pallas_call(
        body, name=name, grid=(S // tr, H),
        in_specs=[_head_spec(tr, 0), _head_spec(tr, H), _gain_spec(), _gain_spec(), _tab_spec(tr), _tab_spec(tr), _tab_spec(tr)],
        out_specs=[_head_spec(tr, 0), _head_spec(tr, 0)], out_shape=[out, out],
        compiler_params=_cparams(("parallel", "parallel")),
    )(proj, proj, q_gain, k_gain, cc, s1, s2)


def _attn_prep_bwd(proj, q_gain, k_gain, cc, s1, s2, near, far, name):
    S = proj.shape[0]
    H = N_ATTN_HEADS
    tr = _pick(S, 512)

    def body(q_ref, k_ref, qg_ref, kg_ref, cc_ref, s1_ref, s2_ref, dqn_ref, dkn_ref, dvn_ref, dqf_ref, dkf_ref, dvf_ref,
             dpq_ref, dpk_ref, dpv_ref, dqg_ref, dkg_ref):
        cc, s1, s2 = cc_ref[...], s1_ref[...], s2_ref[...]
        dq = dqn_ref[...] + dqf_ref[...]
        dk = dkn_ref[...] + dkf_ref[...]
        dpv_ref[...] = (dvn_ref[...] + dvf_ref[...]).astype(BF16)

        @pl.when((pl.program_id(0) == 0) & (pl.program_id(1) == 0))
        def _():
            dqg_ref[...] = jnp.zeros_like(dqg_ref)
            dkg_ref[...] = jnp.zeros_like(dkg_ref)

        _, vjp_q = jax.vjp(_rms, q_ref[...], qg_ref[...])
        dxq, dgq = vjp_q(_rope_t(dq, cc, s1, s2))
        _, vjp_k = jax.vjp(_rms, k_ref[...], kg_ref[...])
        dxk, dgk = vjp_k(_rope_t(dk, cc, s1, s2))
        dpq_ref[...] = dxq.astype(BF16)
        dpk_ref[...] = dxk.astype(BF16)
        dqg_ref[...] += dgq
        dkg_ref[...] += dgk

    out = jax.ShapeDtypeStruct((S, H * HEAD_DIM), BF16)
    gout = jax.ShapeDtypeStruct((1, HEAD_DIM), F32)
    return pl.pallas_call(
        body, name=name, grid=(S // tr, H),
        in_specs=[_head_spec(tr, 0), _head_spec(tr, H), _gain_spec(), _gain_spec(), _tab_spec(tr), _tab_spec(tr), _tab_spec(tr)]
        + [_head_spec(tr, 0)] * 6,
        out_specs=[_head_spec(tr, 0)] * 3 + [_gain_spec(), _gain_spec()], out_shape=[out, out, out, gout, gout],
        compiler_params=_cparams(("arbitrary", "arbitrary")),
    )(proj, proj, q_gain, k_gain, cc, s1, s2, *near, *far)


def _multiplicity(j, t):
    ti = lax.broadcasted_iota(jnp.int32, (t, t), 0)
    si = lax.broadcasted_iota(jnp.int32, (t, t), 1)
    delta = j * t + ti - si
    cnt = jnp.zeros((t, t), F32)
    for window, dil in NEAR_PATTERNS:
        ok = (delta >= 0) & ((delta & (dil - 1)) == 0) & (delta <= window)
        cnt = cnt + ok.astype(F32)
    return cnt


_NEG = -1e30
NEAR_PATTERNS = DILATED_PATTERNS[:2]
NEAR_WINDOW = max(w for w, _ in NEAR_PATTERNS)
FAR_WINDOW, FAR_DIL = DILATED_PATTERNS[2]


def _log_multiplicity_table(t):
    cnt = jnp.stack([_multiplicity(j, t) for j in range(NEAR_WINDOW // t + 1)])
    return jnp.where(cnt > 0.0, jnp.log(jnp.maximum(cnt, 1.0)), _NEG)


def _bias_spec(t):
    return pl.BlockSpec((NEAR_WINDOW // t + 1, t, t), lambda h, i, j: (0, 0, 0))


def _attn_fwd(q, k, proj, v_col0, bias, name):
    S = q.shape[0]
    H = N_ATTN_HEADS
    t = _pick(S, ATTN_TILE)
    nq = S // t
    nj = NEAR_WINDOW // t + 1
    scale = HEAD_DIM ** -0.5

    def body(q_ref, k_ref, v_ref, b_ref, o_ref, lse_ref, m_sc, l_sc, acc_sc):
        qb, j = pl.program_id(1), pl.program_id(2)

        @pl.when(j == 0)
        def _():
            m_sc[...] = jnp.full_like(m_sc, _NEG)
            l_sc[...] = jnp.zeros_like(l_sc)
            acc_sc[...] = jnp.zeros_like(acc_sc)

        @pl.when(qb - j >= 0)
        def _():
            s = _bdot(q_ref[...], k_ref[...], ((1,), (1,))) * scale + b_ref[j]
            m_prev = m_sc[...]
            m_new = jnp.maximum(m_prev, jnp.max(s, axis=-1, keepdims=True))
            alpha = jnp.exp(m_prev - m_new)
            p = jnp.exp(s - m_new)
            l_sc[...] = alpha * l_sc[...] + jnp.sum(p, axis=-1, keepdims=True)
            acc_sc[...] = alpha * acc_sc[...] + _bdot(p, v_ref[...], ((1,), (0,)))
            m_sc[...] = m_new

        @pl.when(j == nj - 1)
        def _():
            o_ref[...] = acc_sc[...] / l_sc[...]
            lse_ref[...] = jnp.broadcast_to(m_sc[...] + jnp.log(l_sc[...]), (t, HEAD_DIM))

    qspec = pl.BlockSpec((t, HEAD_DIM), lambda h, i, j: (i, h))
    kspec = pl.BlockSpec((t, HEAD_DIM), lambda h, i, j: (jnp.maximum(i - j, 0), h))
    vspec = pl.BlockSpec((t, HEAD_DIM), lambda h, i, j: (jnp.maximum(i - j, 0), v_col0 + h))
    return pl.pallas_call(
        body, name=name, grid=(H, nq, nj),
        in_specs=[qspec, kspec, vspec, _bias_spec(t)], out_specs=[qspec, qspec],
        out_shape=[jax.ShapeDtypeStruct((S, H * HEAD_DIM), F32), jax.ShapeDtypeStruct((S, H * HEAD_DIM), F32)],
        scratch_shapes=[pltpu.VMEM((t, 1), F32), pltpu.VMEM((t, 1), F32), pltpu.VMEM((t, HEAD_DIM), F32)],
        compiler_params=_cparams(("parallel", "parallel", "arbitrary")),
    )(q, k, proj, bias)


def _far_rows(r, n):
    return pl.ds(r, n, stride=FAR_DIL)


def _far_band_bias(n):
    i = lax.broadcasted_iota(jnp.int32, (n, n), 0)
    j = lax.broadcasted_iota(jnp.int32, (n, n), 1)
    return jnp.where((i >= j) & (i - j <= FAR_WINDOW // FAR_DIL), 0.0, _NEG)


def _col_spec(S, col0):
    return pl.BlockSpec((S, HEAD_DIM), lambda h: (0, col0 + h))


def _attn_far_fwd(q, k, proj, v_col0, o_near, lse_near, name):
    S = q.shape[0]
    H = N_ATTN_HEADS
    n = S // FAR_DIL
    scale = HEAD_DIM ** -0.5

    def body(q_ref, k_ref, v_ref, on_ref, ln_ref, o_ref, ob_ref, lse_ref):
        bias = _far_band_bias(n)
        for r in range(FAR_DIL):
            rows = _far_rows(r, n)
            s = _bdot(q_ref[rows, :], k_ref[rows, :], _NT) * scale + bias
            m = jnp.max(s, axis=-1, keepdims=True)
            p = jnp.exp(s - m)
            l = jnp.sum(p, axis=-1, keepdims=True)
            o_far = _bdot(p, v_ref[rows, :], _NN) / l
            lse_far = m + jnp.log(l)
            lse_near = jnp.max(ln_ref[rows, :], axis=-1, keepdims=True)
            top = jnp.maximum(lse_near, lse_far)
            lse = top + jnp.log(jnp.exp(lse_near - top) + jnp.exp(lse_far - top))
            o_ref[rows, :] = jnp.exp(lse_near - lse) * on_ref[rows, :] + jnp.exp(lse_far - lse) * o_far
            lse_ref[rows, :] = jnp.broadcast_to(lse, (n, HEAD_DIM))
        ob_ref[...] = o_ref[...].astype(BF16)

    cs = _col_spec(S, 0)
    return pl.pallas_call(
        body, name=name, grid=(H,),
        in_specs=[cs, cs, _col_spec(S, v_col0), cs, cs], out_specs=[cs, cs, cs],
        out_shape=[jax.ShapeDtypeStruct((S, H * HEAD_DIM), F32), jax.ShapeDtypeStruct((S, H * HEAD_DIM), BF16),
                   jax.ShapeDtypeStruct((S, H * HEAD_DIM), F32)],
        compiler_params=_cparams(("parallel",)),
    )(q, k, proj, o_near, lse_near)


def _attn_far_bwd(q, k, proj, v_col0, o, lse, do, do_col0, name):
    S = q.shape[0]
    H = N_ATTN_HEADS
    n = S // FAR_DIL
    scale = HEAD_DIM ** -0.5

    def body(q_ref, k_ref, v_ref, o_ref, lse_ref, do_ref, dq_ref, dk_ref, dv_ref):
        bias = _far_band_bias(n)
        for r in range(FAR_DIL):
            rows = _far_rows(r, n)
            q, k, v, do = q_ref[rows, :], k_ref[rows, :], v_ref[rows, :], do_ref[rows, :]
            dsum = jnp.sum(do * o_ref[rows, :], axis=-1, keepdims=True)
            lse = jnp.max(lse_ref[rows, :], axis=-1, keepdims=True)
            p = jnp.exp(_bdot(q, k, _NT) * scale + bias - lse)
            ds = p * (_bdot(do, v, _NT) - dsum)
            dq_ref[rows, :] = _bdot(ds, k, _NN) * scale
            dk_ref[rows, :] = _bdot(ds, q, _TN) * scale
            dv_ref[rows, :] = _bdot(p, do, _TN)

    cs = _col_spec(S, 0)
    shp = jax.ShapeDtypeStruct((S, H * HEAD_DIM), F32)
    return pl.pallas_call(
        body, name=name, grid=(H,),
        in_specs=[cs, cs, _col_spec(S, v_col0), cs, cs, _col_spec(S, do_col0)], out_specs=[cs, cs, cs], out_shape=[shp, shp, shp],
        compiler_params=_cparams(("parallel",)),
    )(q, k, proj, o, lse, do)


def _attn_probs(q, k, lse, bias_tile, scale):
    return jnp.exp(_bdot(q, k, ((1,), (1,))) * scale + bias_tile - lse)


def _attn_bwd_dq(q, k, proj, v_col0, o, lse, do, do_col0, bias, name):
    S = q.shape[0]
    H = N_ATTN_HEADS
    t = _pick(S, ATTN_TILE)
    nq = S // t
    nj = NEAR_WINDOW // t + 1
    scale = HEAD_DIM ** -0.5

    def body(q_ref, k_ref, v_ref, o_ref, lse_ref, do_ref, b_ref, dq_ref, acc_sc):
        qb, j = pl.program_id(1), pl.program_id(2)

        @pl.when(j == 0)
        def _():
            acc_sc[...] = jnp.zeros_like(acc_sc)

        @pl.when(qb - j >= 0)
        def _():
            do = do_ref[...]
            dsum = jnp.sum(do * o_ref[...], axis=-1, keepdims=True)
            lse = jnp.max(lse_ref[...], axis=-1, keepdims=True)
            p = _attn_probs(q_ref[...], k_ref[...], lse, b_ref[j], scale)
            dp = _bdot(do, v_ref[...], ((1,), (1,)))
            ds = p * (dp - dsum)
            acc_sc[...] += _bdot(ds, k_ref[...], ((1,), (0,))) * scale

        @pl.when(j == nj - 1)
        def _():
            dq_ref[...] = acc_sc[...]

    qspec = pl.BlockSpec((t, HEAD_DIM), lambda h, i, j: (i, h))
    dospec = pl.BlockSpec((t, HEAD_DIM), lambda h, i, j: (i, do_col0 + h))
    kspec = pl.BlockSpec((t, HEAD_DIM), lambda h, i, j: (jnp.maximum(i - j, 0), h))
    vspec = pl.BlockSpec((t, HEAD_DIM), lambda h, i, j: (jnp.maximum(i - j, 0), v_col0 + h))
    return pl.pallas_call(
        body, name=name, grid=(H, nq, nj),
        in_specs=[qspec, kspec, vspec, qspec, qspec, dospec, _bias_spec(t)], out_specs=qspec,
        out_shape=jax.ShapeDtypeStruct((S, H * HEAD_DIM), F32),
        scratch_shapes=[pltpu.VMEM((t, HEAD_DIM), F32)],
        compiler_params=_cparams(("parallel", "parallel", "arbitrary")),
    )(q, k, proj, o, lse, do, bias)


def _attn_bwd_dkv(q, k, proj, v_col0, o, lse, do, do_col0, bias, name):
    S = q.shape[0]
    H = N_ATTN_HEADS
    t = _pick(S, ATTN_TILE)
    nq = S // t
    nj = NEAR_WINDOW // t + 1
    scale = HEAD_DIM ** -0.5

    def body(q_ref, k_ref, v_ref, o_ref, lse_ref, do_ref, b_ref, dk_ref, dv_ref, dk_sc, dv_sc):
        kb, j = pl.program_id(1), pl.program_id(2)

        @pl.when(j == 0)
        def _():
            dk_sc[...] = jnp.zeros_like(dk_sc)
            dv_sc[...] = jnp.zeros_like(dv_sc)

        @pl.when(kb + j < nq)
        def _():
            do = do_ref[...]
            dsum = jnp.sum(do * o_ref[...], axis=-1, keepdims=True)
            lse = jnp.max(lse_ref[...], axis=-1, keepdims=True)
            p = _attn_probs(q_ref[...], k_ref[...], lse, b_ref[j], scale)
            dp = _bdot(do, v_ref[...], ((1,), (1,)))
            ds = p * (dp - dsum)
            dv_sc[...] += _bdot(p, do, ((0,), (0,)))
            dk_sc[...] += _bdot(ds, q_ref[...], ((0,), (0,))) * scale

        @pl.when(j == nj - 1)
        def _():
            dk_ref[...] = dk_sc[...]
            dv_ref[...] = dv_sc[...]

    def qrow(h, i, j):
        return jnp.minimum(i + j, nq - 1)

    qspec = pl.BlockSpec((t, HEAD_DIM), lambda h, i, j: (qrow(h, i, j), h))
    dospec = pl.BlockSpec((t, HEAD_DIM), lambda h, i, j: (qrow(h, i, j), do_col0 + h))
    kspec = pl.BlockSpec((t, HEAD_DIM), lambda h, i, j: (i, h))
    vspec = pl.BlockSpec((t, HEAD_DIM), lambda h, i, j: (i, v_col0 + h))
    return pl.pallas_call(
        body, name=name, grid=(H, nq, nj),
        in_specs=[qspec, kspec, vspec, qspec, qspec, dospec, _bias_spec(t)], out_specs=[kspec, kspec],
        out_shape=[jax.ShapeDtypeStruct((S, H * HEAD_DIM), F32), jax.ShapeDtypeStruct((S, H * HEAD_DIM), F32)],
        scratch_shapes=[pltpu.VMEM((t, HEAD_DIM), F32), pltpu.VMEM((t, HEAD_DIM), F32)],
        compiler_params=_cparams(("parallel", "parallel", "arbitrary")),
    )(q, k, proj, o, lse, do, bias)


def _conv_pre(x_ref, w_ref):
    x = x_ref[...]
    rows = lax.broadcasted_iota(jnp.int32, x.shape, 0)
    shifted = [x]
    acc = x * w_ref[pl.ds(CONV_WIDTH - 1, 1), :]
    for sft in range(1, CONV_WIDTH):
        xs = jnp.where(rows >= sft, pltpu.roll(x, sft, 0), 0.0)
        shifted.append(xs)
        acc = acc + xs * w_ref[pl.ds(CONV_WIDTH - 1 - sft, 1), :]
    return acc, shifted


def _conv_fwd(proj, col0, width, w, name):
    S = proj.shape[0]

    def body(x_ref, w_ref, y_ref):
        acc, _ = _conv_pre(x_ref, w_ref)
        y_ref[...] = _silu(acc)

    return pl.pallas_call(
        body, name=name, grid=(width // LANES,),
        in_specs=[pl.BlockSpec((S, LANES), lambda c: (0, col0 + c)), pl.BlockSpec((CONV_WIDTH, LANES), lambda c: (0, c))],
        out_specs=pl.BlockSpec((S, LANES), lambda c: (0, c)),
        out_shape=jax.ShapeDtypeStruct((S, width), F32),
        compiler_params=_cparams(("parallel",)),
    )(proj, w)


def _conv_bwd(proj, col0, width, w, dy, name):
    S = proj.shape[0]

    def body(x_ref, w_ref, d_ref, dx_ref, dw_ref):
        acc, shifted = _conv_pre(x_ref, w_ref)
        sig = _sigmoid(acc)
        da = d_ref[...] * (sig * (1.0 + acc * (1.0 - sig)))
        rows = lax.broadcasted_iota(jnp.int32, da.shape, 0)
        dx = da * w_ref[pl.ds(CONV_WIDTH - 1, 1), :]
        dw_ref[pl.ds(CONV_WIDTH - 1, 1), :] = jnp.sum(da * shifted[0], axis=0, keepdims=True)
        for sft in range(1, CONV_WIDTH):
            back = jnp.where(rows < S - sft, pltpu.roll(da, S - sft, 0), 0.0)
            dx = dx + back * w_ref[pl.ds(CONV_WIDTH - 1 - sft, 1), :]
            dw_ref[pl.ds(CONV_WIDTH - 1 - sft, 1), :] = jnp.sum(da * shifted[sft], axis=0, keepdims=True)
        dx_ref[...] = dx.astype(BF16)

    return pl.pallas_call(
        body, name=name, grid=(width // LANES,),
        in_specs=[pl.BlockSpec((S, LANES), lambda c: (0, col0 + c)), pl.BlockSpec((CONV_WIDTH, LANES), lambda c: (0, c)),
                  pl.BlockSpec((S, LANES), lambda c: (0, c))],
        out_specs=[pl.BlockSpec((S, LANES), lambda c: (0, c)), pl.BlockSpec((CONV_WIDTH, LANES), lambda c: (0, c))],
        out_shape=[jax.ShapeDtypeStruct((S, width), BF16), jax.ShapeDtypeStruct((CONV_WIDTH, width), F32)],
        compiler_params=_cparams(("parallel",)),
    )(proj, w, dy)


PREP_CHUNKS = 16


def _chunks_prep(qraws, kraws, vs, abs_, alog_row, dtb_row, mask_g, mask_b, t_saved=None):
    n = len(qraws)
    c = qraws[0].shape[0]
    mm_nt, mm_nn = (_mm_nt, _mm_nn) if t_saved is not None else (lambda p, r: _bdot(p, r, _NT), lambda p, r: _bdot(p, r, _NN))
    row = lax.broadcasted_iota(jnp.int32, (c, c), 0)
    col = lax.broadcasted_iota(jnp.int32, (c, c), 1)
    tril, strict, eye = row >= col, row > col, row == col
    eyef = eye.astype(F32)
    neg_rate = -jnp.exp(alog_row)
    q, k, beta, gc_col, gamma, kb, g_last = [], [], [], [], [], [], []
    for i in range(n):
        q.append(_l2(qraws[i]) * (HEAD_DIM ** -0.5))
        k.append(_l2(kraws[i]))
        gfull = neg_rate * _softplus(abs_[i] + dtb_row)
        g = jnp.sum(jnp.where(mask_g, gfull, 0.0), axis=-1, keepdims=True)
        beta.append(jnp.sum(jnp.where(mask_b, _sigmoid(abs_[i]), 0.0), axis=-1, keepdims=True))
        g_row = jnp.sum(jnp.where(eye, g, 0.0), axis=0, keepdims=True)
        gc_col.append(jnp.sum(jnp.where(tril, g_row, 0.0), axis=1, keepdims=True))
        gc_row = jnp.sum(jnp.where(row <= col, g, 0.0), axis=0, keepdims=True)
        gamma.append(jnp.where(tril, jnp.exp(jnp.where(tril, gc_col[i] - gc_row, 0.0)), 0.0))
        kb.append(k[i] * beta[i])
        g_last.append(jnp.sum(g, axis=0, keepdims=True))
    a = [jnp.where(strict, mm_nt(kb[i], k[i]) * gamma[i], 0.0) for i in range(n)]
    if t_saved is None:
        t_inv = [eyef - a[i] for i in range(n)]
        p = a
        for _ in range(int(math.log2(c)) - 1):
            p = [_dot3(p[i], p[i]) for i in range(n)]
            t_inv = [_dot3(t_inv[i], eyef + p[i]) for i in range(n)]
    else:
        t_inv = [_tri_inv_saved(a[i], t_saved[i]) for i in range(n)]
    egc = [jnp.exp(gc_col[i]) for i in range(n)]
    u = [mm_nn(t_inv[i], vs[i] * beta[i]) for i in range(n)]
    w = [mm_nn(t_inv[i], kb[i] * egc[i]) for i in range(n)]
    intra = [mm_nt(q[i], k[i]) * gamma[i] for i in range(n)]
    out = []
    for i in range(n):
        kt = k[i] * jnp.exp(g_last[i] - gc_col[i])
        dec = jnp.broadcast_to(jnp.exp(g_last[i]), (1, HEAD_DIM))
        one = (u[i], w[i], q[i] * egc[i], kt, intra[i], dec)
        out.append(one + (t_inv[i],) if t_saved is None else one)
    return out


def _lane_masks(h):
    lane = lax.broadcasted_iota(jnp.int32, (1, LANES), 1)
    return lane == h, lane == N_DELTA_HEADS + h


def _prep_specs(tr, ab_col):
    H = N_DELTA_HEADS
    return [
        pl.BlockSpec((tr, HEAD_DIM), lambda i, h: (i, h)),
        pl.BlockSpec((tr, HEAD_DIM), lambda i, h: (i, H + h)),
        pl.BlockSpec((tr, HEAD_DIM), lambda i, h: (i, 2 * H + h)),
        pl.BlockSpec((tr, LANES), lambda i, h: (i, ab_col)),
        pl.BlockSpec((1, LANES), lambda i, h: (0, 0)),
        pl.BlockSpec((1, LANES), lambda i, h: (0, 0)),
    ]


def _prep_out_specs(tr):
    nc = tr // CHUNK
    hs = pl.BlockSpec((tr, HEAD_DIM), lambda i, h: (i, h))
    return [hs, hs, hs, hs,
            pl.BlockSpec((None, tr, CHUNK), lambda i, h: (h, i, 0)),
            pl.BlockSpec((None, nc, 1, HEAD_DIM), lambda i, h: (h, i, 0, 0)),
            pl.BlockSpec((None, tr, CHUNK), lambda i, h: (h, i, 0))]


def _prep_out_shapes(S):
    H = N_DELTA_HEADS
    hs = jax.ShapeDtypeStruct((S, H * HEAD_DIM), F32)
    sq = jax.ShapeDtypeStruct((H, S, CHUNK), F32)
    return [hs, hs, hs, hs, sq, jax.ShapeDtypeStruct((H, S // CHUNK, 1, HEAD_DIM), F32), sq]


def _delta_prep_fwd(dqkv, proj, ab_col, alog_row, dtb_row, name):
    S = dqkv.shape[0]
    tr = min(S, PREP_CHUNKS * CHUNK)
    nc = tr // CHUNK

    def body(q_ref, k_ref, v_ref, ab_ref, al_ref, dt_ref, u_ref, w_ref, qd_ref, kt_ref, in_ref, dec_ref, ti_ref):
        mask_g, mask_b = _lane_masks(pl.program_id(1))
        rows = [pl.ds(ci * CHUNK, CHUNK) for ci in range(nc)]
        outs = _chunks_prep([q_ref[rs, :] for rs in rows], [k_ref[rs, :] for rs in rows], [v_ref[rs, :] for rs in rows],
                            [ab_ref[rs, :] for rs in rows], al_ref[...], dt_ref[...], mask_g, mask_b)
        for ci, rs in enumerate(rows):
            u, w, qd, kt, intra, dec, t_inv = outs[ci]
            u_ref[rs, :] = u
            w_ref[rs, :] = w
            qd_ref[rs, :] = qd
            kt_ref[rs, :] = kt
            in_ref[rs, :] = intra
            dec_ref[ci] = dec
            ti_ref[rs, :] = t_inv

    return pl.pallas_call(
        body, name=name, grid=(S // tr, N_DELTA_HEADS),
        in_specs=_prep_specs(tr, ab_col), out_specs=_prep_out_specs(tr), out_shape=_prep_out_shapes(S),
        compiler_params=_cparams(("parallel", "parallel")),
    )(dqkv, dqkv, dqkv, proj, alog_row, dtb_row)


def _delta_prep_bwd(dqkv, proj, ab_col, alog_row, dtb_row, cots, t_inv, name):
    S = dqkv.shape[0]
    H = N_DELTA_HEADS
    tr = min(S, PREP_CHUNKS * CHUNK)
    nc = tr // CHUNK

    def body(q_ref, k_ref, v_ref, ab_ref, al_ref, dt_ref, du_ref, dw_ref, dqd_ref, dkt_ref, din_ref, ddec_ref, ti_ref,
             dq_ref, dk_ref, dv_ref, dab_ref, dal_ref, ddt_ref, dab_sc):
        h = pl.program_id(1)
        mask_g, mask_b = _lane_masks(h)

        @pl.when((pl.program_id(0) == 0) & (h == 0))
        def _():
            dal_ref[...] = jnp.zeros_like(dal_ref)
            ddt_ref[...] = jnp.zeros_like(ddt_ref)

        @pl.when(h == 0)
        def _():
            dab_sc[...] = jnp.zeros_like(dab_sc)

        rows = [pl.ds(ci * CHUNK, CHUNK) for ci in range(nc)]
        fn = functools.partial(_chunks_prep, mask_g=mask_g, mask_b=mask_b, t_saved=[ti_ref[rs, :] for rs in rows])
        _, vjp = jax.vjp(fn, [q_ref[rs, :] for rs in rows], [k_ref[rs, :] for rs in rows], [v_ref[rs, :] for rs in rows],
                         [ab_ref[rs, :] for rs in rows], al_ref[...], dt_ref[...])
        dqs, dks, dvs, dabs, dal, ddt = vjp([(du_ref[rs, :], dw_ref[rs, :], dqd_ref[rs, :], dkt_ref[rs, :], din_ref[rs, :],
                                              ddec_ref[ci]) for ci, rs in enumerate(rows)])
        for ci, rs in enumerate(rows):
            dq_ref[rs, :] = dqs[ci]
            dk_ref[rs, :] = dks[ci]
            dv_ref[rs, :] = dvs[ci]
            dab_sc[rs, :] += dabs[ci]
        dal_ref[...] += dal
        ddt_ref[...] += ddt

        @pl.when(h == H - 1)
        def _():
            dab_ref[...] = dab_sc[...].astype(BF16)

    hs = pl.BlockSpec((tr, HEAD_DIM), lambda i, h: (i, h))
    hshape = jax.ShapeDtypeStruct((S, H * HEAD_DIM), F32)
    row = pl.BlockSpec((1, LANES), lambda i, h: (0, 0))
    rshape = jax.ShapeDtypeStruct((1, LANES), F32)
    return pl.pallas_call(
        body, name=name, grid=(S // tr, H),
        in_specs=_prep_specs(tr, ab_col) + _prep_out_specs(tr),
        out_specs=[hs, hs, hs, pl.BlockSpec((tr, LANES), lambda i, h: (i, 0)), row, row],
        out_shape=[hshape, hshape, hshape, jax.ShapeDtypeStruct((S, LANES), BF16), rshape, rshape],
        scratch_shapes=[pltpu.VMEM((tr, LANES), F32)],
        compiler_params=_cparams(("arbitrary", "arbitrary")),
    )(dqkv, dqkv, dqkv, proj, alog_row, dtb_row, *cots, t_inv)


def _scan_steps(states, us, ws, qds, kts, intras, decs, diff=False):
    nn, tn = (_mm_nn, _mm_tn) if diff else (lambda p, r: _bdot(p, r, _NN), lambda p, r: _bdot(p, r, _TN))
    hs = range(len(states))
    v_new = [us[h] - nn(ws[h], states[h]) for h in hs]
    o_state = [nn(qds[h], states[h]) for h in hs]
    o_intra = [nn(intras[h], v_new[h]) for h in hs]
    grown = [tn(kts[h], v_new[h]) for h in hs]
    return [o_state[h] + o_intra[h] for h in hs], [states[h] * decs[h] + grown[h] for h in hs]


def _scan_specs(rev, n):
    H = N_DELTA_HEADS

    def cix(i):
        return (n - 1 - i) if rev else i

    row = pl.BlockSpec((CHUNK, H * HEAD_DIM), lambda i: (cix(i), 0))
    return row, pl.BlockSpec((H, CHUNK, CHUNK), lambda i: (0, cix(i), 0)), \
        pl.BlockSpec((H, 1, 1, HEAD_DIM), lambda i: (0, cix(i), 0, 0)), \
        pl.BlockSpec((1, H, HEAD_DIM, HEAD_DIM), lambda i: (cix(i), 0, 0, 0))


def _delta_scan_fwd(u, w, qd, kt, intra, dec, name):
    S = u.shape[0]
    H = N_DELTA_HEADS
    n = S // CHUNK
    row, ispec, dspec, sspec = _scan_specs(False, n)

    def body(u_ref, w_ref, qd_ref, kt_ref, in_ref, dec_ref, o_ref, st_ref, s_sc):
        @pl.when(pl.program_id(0) == 0)
        def _():
            s_sc[...] = jnp.zeros_like(s_sc)

        cols = [pl.ds(h * HEAD_DIM, HEAD_DIM) for h in range(H)]
        states = [s_sc[h] for h in range(H)]
        outs, new = _scan_steps(states, [u_ref[:, cs] for cs in cols], [w_ref[:, cs] for cs in cols],
                                [qd_ref[:, cs] for cs in cols], [kt_ref[:, cs] for cs in cols],
                                [in_ref[h] for h in range(H)], [dec_ref[h, 0] for h in range(H)])
        for h, cs in enumerate(cols):
            st_ref[0, h] = states[h]
            o_ref[:, cs] = outs[h]
            s_sc[h] = new[h]

    return pl.pallas_call(
        body, name=name, grid=(n,),
        in_specs=[row, row, row, row, ispec, dspec], out_specs=[row, sspec],
        out_shape=[jax.ShapeDtypeStruct((S, H * HEAD_DIM), F32), jax.ShapeDtypeStruct((n, H, HEAD_DIM, HEAD_DIM), F32)],
        scratch_shapes=[pltpu.VMEM((H, HEAD_DIM, HEAD_DIM), F32)],
        compiler_params=_cparams(("arbitrary",)),
    )(u, w, qd, kt, intra, dec)


def _delta_scan_bwd(u, w, qd, kt, intra, dec, states, do, name):
    S = u.shape[0]
    H = N_DELTA_HEADS
    n = S // CHUNK
    row, ispec, dspec, sspec = _scan_specs(True, n)

    def body(u_ref, w_ref, qd_ref, kt_ref, in_ref, dec_ref, st_ref, do_ref,
             du_ref, dw_ref, dqd_ref, dkt_ref, din_ref, ddec_ref, ds_sc):
        @pl.when(pl.program_id(0) == 0)
        def _():
            ds_sc[...] = jnp.zeros_like(ds_sc)

        cols = [pl.ds(h * HEAD_DIM, HEAD_DIM) for h in range(H)]
        _, vjp = jax.vjp(functools.partial(_scan_steps, diff=True), [st_ref[0, h] for h in range(H)], [u_ref[:, cs] for cs in cols],
                         [w_ref[:, cs] for cs in cols], [qd_ref[:, cs] for cs in cols], [kt_ref[:, cs] for cs in cols],
                         [in_ref[h] for h in range(H)], [dec_ref[h, 0] for h in range(H)])
        dstate, du, dw, dqd, dkt, din, ddec = vjp(([do_ref[:, cs] for cs in cols], [ds_sc[h] for h in range(H)]))
        for h, cs in enumerate(cols):
            du_ref[:, cs] = du[h]
            dw_ref[:, cs] = dw[h]
            dqd_ref[:, cs] = dqd[h]
            dkt_ref[:, cs] = dkt[h]
            din_ref[h] = din[h]
            ddec_ref[h, 0] = ddec[h]
            ds_sc[h] = dstate[h]

    hshape = jax.ShapeDtypeStruct((S, H * HEAD_DIM), F32)
    return pl.pallas_call(
        body, name=name, grid=(n,),
        in_specs=[row, row, row, row, ispec, dspec, sspec, row],
        out_specs=[row, row, row, row, ispec, dspec],
        out_shape=[hshape, hshape, hshape, hshape, jax.ShapeDtypeStruct((H, S, CHUNK), F32),
                   jax.ShapeDtypeStruct((H, n, 1, HEAD_DIM), F32)],
        scratch_shapes=[pltpu.VMEM((H, HEAD_DIM, HEAD_DIM), F32)],
        compiler_params=_cparams(("arbitrary",)),
    )(u, w, qd, kt, intra, dec, states, do)


def _gated_norm(od, z, gain):
    return _rms(od, gain) * _silu(z)


def _post_fwd(od, proj, z_col0, gain, name):
    S = od.shape[0]
    H = N_DELTA_HEADS
    tr = _pick(S, 512)

    def body(od_ref, z_ref, g_ref, o_ref):
        o_ref[...] = _gated_norm(od_ref[...], z_ref[...], g_ref[...]).astype(BF16)

    return pl.pallas_call(
        body, name=name, grid=(S // tr, H),
        in_specs=[_head_spec(tr, 0), _head_spec(tr, z_col0), _gain_spec()],
        out_specs=_head_spec(tr, 0), out_shape=jax.ShapeDtypeStruct((S, H * HEAD_DIM), BF16),
        compiler_params=_cparams(("parallel", "parallel")),
    )(od, proj, gain)


def _post_bwd(od, proj, z_col0, gain, do, do_col0, name):
    S = od.shape[0]
    H = N_DELTA_HEADS
    tr = _pick(S, 512)

    def body(od_ref, z_ref, g_ref, do_ref, dod_ref, dz_ref, dg_ref):
        @pl.when((pl.program_id(0) == 0) & (pl.program_id(1) == 0))
        def _():
            dg_ref[...] = jnp.zeros_like(dg_ref)

        _, vjp = jax.vjp(_gated_norm, od_ref[...], z_ref[...], g_ref[...])
        dod, dz, dg = vjp(do_ref[...])
        dod_ref[...] = dod
        dz_ref[...] = dz.astype(BF16)
        dg_ref[...] += dg

    return pl.pallas_call(
        body, name=name, grid=(S // tr, H),
        in_specs=[_head_spec(tr, 0), _head_spec(tr, z_col0), _gain_spec(), _head_spec(tr, do_col0)],
        out_specs=[_head_spec(tr, 0), _head_spec(tr, 0), _gain_spec()],
        out_shape=[jax.ShapeDtypeStruct((S, H * HEAD_DIM), F32), jax.ShapeDtypeStruct((S, H * HEAD_DIM), BF16),
                   jax.ShapeDtypeStruct((1, HEAD_DIM), F32)],
        compiler_params=_cparams(("arbitrary", "arbitrary")),
    )(od, proj, gain, do)


def _adam_math(w, g, m, v):
    m = ADAM_B1 * m + (1.0 - ADAM_B1) * g
    v = ADAM_B2 * v + (1.0 - ADAM_B2) * (g * g)
    m_hat = m / (1.0 - ADAM_B1 ** ADAM_STEP)
    v_hat = v / (1.0 - ADAM_B2 ** ADAM_STEP)
    delta = -ADAM_LR * (m_hat / (jnp.sqrt(v_hat) + ADAM_EPS) + ADAM_WD * w)
    return delta, m, v


def _adamw(w, g, m, v, name):
    R, C = w.shape
    tr = R if R * C * 4 <= (1 << 20) else _pick8(R, max(8, (1 << 20) // (C * 4)))

    def body(w_ref, g_ref, m_ref, v_ref, d_ref, nm_ref, nv_ref):
        d, nm, nv = _adam_math(w_ref[...], g_ref[...], m_ref[...], v_ref[...])
        d_ref[...] = d
        nm_ref[...] = nm
        nv_ref[...] = nv

    spec = pl.BlockSpec((tr, C), lambda i: (i, 0))
    shp = jax.ShapeDtypeStruct((R, C), F32)
    return pl.pallas_call(
        body, name=name, grid=(R // tr,), in_specs=[spec] * 4, out_specs=[spec] * 3, out_shape=[shp] * 3,
        compiler_params=_cparams(("parallel",)),
    )(w, g, m, v)


def _adamw_halves(w, mine, theirs, col, m, v, name):
    _, R, C = w.shape
    tr = _pick8(R // 2, max(8, (1 << 20) // (C * 4)))
    nb2 = (R // 2) // tr

    def body(w_ref, a_ref, b_ref, m_ref, v_ref, g_ref, d_ref, nm_ref, nv_ref):
        top = pl.program_id(0) < nb2
        g = jnp.where(top == (lax.axis_index("c") == 0), a_ref[...], b_ref[...])
        d, nm, nv = _adam_math(w_ref[...], g, m_ref[...], v_ref[...])
        g_ref[...] = g
        d_ref[...] = d
        nm_ref[...] = nm
        nv_ref[...] = nv

    spec = pl.BlockSpec((None, tr, C), lambda i: (0, i, 0))
    half = pl.BlockSpec((tr, C), lambda i: (i % nb2, col))
    shp = jax.ShapeDtypeStruct((1, R, C), F32)
    return pl.pallas_call(
        body, name=name, grid=(2 * nb2,), in_specs=[spec, half, half, spec, spec], out_specs=[spec] * 4, out_shape=[shp] * 4,
        compiler_params=_cparams(("parallel",)),
    )(w, mine, theirs, m, v)


def _adamw_halves_t(w, mine, theirs, m, v, name):
    _, R, C = w.shape
    tc = _pick(R // 2, 256)
    nb2 = (R // 2) // tc

    def body(w_ref, a_ref, b_ref, m_ref, v_ref, g_ref, d_ref, nm_ref, nv_ref):
        left = pl.program_id(0) < nb2
        g = jnp.where(left == (lax.axis_index("c") == 0), a_ref[...], b_ref[...])
        d, nm, nv = _adam_math(w_ref[...], g, m_ref[...], v_ref[...])
        g_ref[...] = g
        d_ref[...] = d
        nm_ref[...] = nm
        nv_ref[...] = nv

    spec = pl.BlockSpec((C, tc), lambda j: (0, j))
    half = pl.BlockSpec((C, tc), lambda j: (0, j % nb2))
    shp = jax.ShapeDtypeStruct((C, R), F32)
    outs = pl.pallas_call(
        body, name=name, grid=(2 * nb2,), in_specs=[spec, half, half, spec, spec], out_specs=[spec] * 4, out_shape=[shp] * 4,
        compiler_params=_cparams(("parallel",)),
    )(jnp.transpose(w[0]), jnp.transpose(mine), jnp.transpose(theirs), jnp.transpose(m[0]), jnp.transpose(v[0]))
    return [jnp.transpose(o)[None] for o in outs]


def _pick8(dim, pref):
    t = (min(dim, pref) // 8) * 8
    while t >= 8:
        if dim % t == 0:
            return t
        t -= 8
    return dim


def _adamw_outer(w, m, v, cond_t, rhs, name):
    R, C = w.shape
    tr = _pick8(R, 128)
    nb = cond_t.shape[1]
    lhs_t = cond_t

    def body(w_ref, m_ref, v_ref, a_ref, b_ref, g_ref, d_ref, nm_ref, nv_ref):
        g = _dot(_silu(a_ref[...]), b_ref[...])
        d, nm, nv = _adam_math(w_ref[...], g, m_ref[...], v_ref[...])
        g_ref[...] = g
        d_ref[...] = d
        nm_ref[...] = nm
        nv_ref[...] = nv

    spec = pl.BlockSpec((tr, C), lambda i: (i, 0))
    shp = jax.ShapeDtypeStruct((R, C), F32)
    return pl.pallas_call(
        body, name=name, grid=(R // tr,),
        in_specs=[spec, spec, spec, pl.BlockSpec((tr, nb), lambda i: (i, 0)), pl.BlockSpec((nb, C), lambda i: (0, 0))],
        out_specs=[spec] * 4, out_shape=[shp] * 4,
        compiler_params=_cparams(("parallel",)),
    )(w, m, v, lhs_t, rhs)


def _ada_fwd(cond, w, bias, name):
    a = cond
    nb, K = a.shape
    N = w.shape[1]
    tn = _pick(N, 512)

    def body(a_ref, w_ref, b_ref, o_ref):
        o_ref[...] = _dot(_silu(a_ref[...]), w_ref[...]) + b_ref[...]

    return pl.pallas_call(
        body, name=name, grid=(N // tn,),
        in_specs=[pl.BlockSpec((nb, K), lambda j: (0, 0)), pl.BlockSpec((K, tn), lambda j: (0, j)), pl.BlockSpec((1, tn), lambda j: (0, j))],
        out_specs=pl.BlockSpec((nb, tn), lambda j: (0, j)), out_shape=jax.ShapeDtypeStruct((nb, N), F32),
        compiler_params=_cparams(("parallel",)),
    )(a, w, bias)


def _add_cast(parts, out_dtypes, name):
    shape = parts[0].shape
    G, R, C = shape
    tr = _pick8(R, max(8, (1 << 20) // (C * 4)))
    n_in = len(parts)

    def body(*refs):
        acc = refs[0][...].astype(F32)
        for r in refs[1:n_in]:
            acc = acc + r[...].astype(F32)
        for o, dt in zip(refs[n_in:], out_dtypes):
            o[...] = acc.astype(dt)

    spec = pl.BlockSpec((1, tr, C), lambda g, i: (g, i, 0))
    outs = pl.pallas_call(
        body, name=name, grid=(G, R // tr), in_specs=[spec] * n_in, out_specs=[spec] * len(out_dtypes),
        out_shape=[jax.ShapeDtypeStruct(shape, dt) for dt in out_dtypes],
        compiler_params=_cparams(("parallel", "parallel")),
    )(*parts)
    return outs


def _me():
    return lax.axis_index("x"), lax.axis_index("y"), lax.axis_index("c")


def _xor_peer(k):
    x, y, c = _me()
    dx, dy, dc = (k >> 2) & 1, (k >> 1) & 1, k & 1
    return (x ^ dx if dx else x, y ^ dy if dy else y, c ^ dc if dc else c)


ANY = pl.BlockSpec(memory_space=pl.ANY)


def _all_gather_small(v, name, after=None):
    R, C = v.shape
    extra = [] if after is None else [after]

    def body(v_ref, *rest):
        out_ref, send_sems, recv_sems = rest[len(extra):]
        x, y, c = _me()
        mine = 4 * x + 2 * y + c
        out_ref[mine] = v_ref[...]
        copies = []
        for k in range(1, 8):
            cp = pltpu.make_async_remote_copy(src_ref=v_ref, dst_ref=out_ref.at[mine], send_sem=send_sems.at[k - 1],
                                              recv_sem=recv_sems.at[k - 1], device_id=_xor_peer(k), device_id_type=MESH)
            cp.start()
            copies.append(cp)
        for k in range(1, 8):
            px, py, pc = _xor_peer(k)
            pltpu.make_async_remote_copy(src_ref=v_ref, dst_ref=out_ref.at[4 * px + 2 * py + pc], send_sem=send_sems.at[k - 1],
                                         recv_sem=recv_sems.at[k - 1], device_id=_xor_peer(k), device_id_type=MESH).wait_recv()
        for cp in copies:
            cp.wait_send()

    return pl.pallas_call(
        body, name=name, out_shape=jax.ShapeDtypeStruct((8, R, C), F32),
        in_specs=[pl.BlockSpec(memory_space=pltpu.VMEM)] + [ANY] * len(extra), out_specs=pl.BlockSpec(memory_space=pltpu.VMEM),
        scratch_shapes=[pltpu.SemaphoreType.DMA((7,)), pltpu.SemaphoreType.DMA((7,))],
        compiler_params=pltpu.CompilerParams(vmem_limit_bytes=VMEM_LIMIT),
    )(v, *extra)


def _chip_peers():
    x, y, _ = _me()
    return [(1, (x, 1 - y)), (2, (1 - x, y)), (3, (1 - x, 1 - y))]


def _all_gather_shards(shards, name):
    n = len(shards)

    def body(*refs):
        ins, outs = refs[:n], refs[n:2 * n]
        send_sems, recv_sems = refs[2 * n:]
        x, y, c = _me()
        chip = 2 * x + y
        sib = (x, y, 1 - c)
        peers = _chip_peers()
        sends = []
        for t in range(n):
            half = ins[t].shape[0] // 2
            mine = pl.ds(c * half, half)
            for p, (k, (px, py)) in enumerate(peers):
                cp = pltpu.make_async_remote_copy(src_ref=ins[t].at[mine], dst_ref=outs[t].at[chip, mine],
                                                  send_sem=send_sems.at[6 * t + p], recv_sem=recv_sems.at[6 * t + p],
                                                  device_id=(px, py, c), device_id_type=MESH)
                cp.start()
                sends.append(cp)
        for t in range(n):
            half = ins[t].shape[0] // 2
            mine = pl.ds(c * half, half)
            for p, (k, (px, py)) in enumerate(peers):
                src_chip = 2 * px + py
                landed = outs[t].at[src_chip, mine]
                pltpu.make_async_remote_copy(src_ref=landed, dst_ref=landed, send_sem=send_sems.at[6 * t + p],
                                             recv_sem=recv_sems.at[6 * t + p], device_id=(px, py, c), device_id_type=MESH).wait_recv()
                fw = pltpu.make_async_remote_copy(src_ref=landed, dst_ref=landed, send_sem=send_sems.at[6 * t + 3 + p],
                                                  recv_sem=recv_sems.at[6 * t + 3 + p], device_id=sib, device_id_type=MESH)
                fw.start()
                sends.append(fw)
        for t in range(n):
            half = ins[t].shape[0] // 2
            theirs = pl.ds((1 - c) * half, half)
            for p, (k, (px, py)) in enumerate(peers):
                got = outs[t].at[2 * px + py, theirs]
                pltpu.make_async_remote_copy(src_ref=got, dst_ref=got, send_sem=send_sems.at[6 * t + 3 + p],
                                             recv_sem=recv_sems.at[6 * t + 3 + p], device_id=sib, device_id_type=MESH).wait_recv()
        for cp in sends:
            cp.wait_send()

    return pl.pallas_call(
        body, name=name,
        out_shape=[jax.ShapeDtypeStruct((4,) + s.shape, s.dtype) for s in shards],
        in_specs=[ANY] * n, out_specs=[ANY] * n,
        scratch_shapes=[pltpu.SemaphoreType.DMA((6 * n,)), pltpu.SemaphoreType.DMA((6 * n,))],
    )(*shards)


def _swap_halves_with_sibling(slabs, name):
    n = len(slabs)

    def body(*refs):
        ins, outs = refs[:n], refs[n:2 * n]
        send_sems, recv_sems = refs[2 * n:]
        x, y, c = _me()
        sib = (x, y, 1 - c)
        cps = []
        for t in range(n):
            half = ins[t].shape[1] // 2
            cp = pltpu.make_async_remote_copy(src_ref=ins[t].at[:, pl.ds((1 - c) * half, half)], dst_ref=outs[t],
                                              send_sem=send_sems.at[t], recv_sem=recv_sems.at[t], device_id=sib, device_id_type=MESH)
            cp.start()
            cps.append(cp)
        for cp in cps:
            cp.wait()

    return pl.pallas_call(
        body, name=name,
        out_shape=[jax.ShapeDtypeStruct((4, s.shape[1] // 2, s.shape[2]), s.dtype) for s in slabs],
        in_specs=[ANY] * n, out_specs=[ANY] * n,
        scratch_shapes=[pltpu.SemaphoreType.DMA((n,)), pltpu.SemaphoreType.DMA((n,))],
    )(*slabs)


def _send_halves_to_sibling(halves, name):
    n = len(halves)

    def body(*refs):
        ins, outs = refs[:n], refs[n:2 * n]
        send_sems, recv_sems = refs[2 * n:]
        x, y, c = _me()
        sib = (x, y, 1 - c)
        cps = []
        for t in range(n):
            cp = pltpu.make_async_remote_copy(src_ref=ins[t], dst_ref=outs[t], send_sem=send_sems.at[t],
                                              recv_sem=recv_sems.at[t], device_id=sib, device_id_type=MESH)
            cp.start()
            cps.append(cp)
        for cp in cps:
            cp.wait()

    return pl.pallas_call(
        body, name=name,
        out_shape=[jax.ShapeDtypeStruct(s.shape, s.dtype) for s in halves],
        in_specs=[ANY] * n, out_specs=[ANY] * n,
        scratch_shapes=[pltpu.SemaphoreType.DMA((n,)), pltpu.SemaphoreType.DMA((n,))],
    )(*halves)


HBM_SPEC = pl.BlockSpec(memory_space=pltpu.HBM)
SEM_SPEC = pl.BlockSpec(memory_space=pltpu.SEMAPHORE)
DATAFLOW = pltpu.SideEffectType.DATAFLOW_SIDE_EFFECTING


def _plan_gather_direct(src_refs, land_refs):
    x, y, c = _me()
    chip = 2 * x + y
    plan = []
    for s, land in zip(src_refs, land_refs):
        half = s.shape[0] // 2
        for _, (px, py) in _chip_peers():
            for pc in (c, 1 - c):
                plan.append((s.at[pl.ds(c * half, half)], land.at[chip, pl.ds(c * half, half)],
                             land.at[2 * px + py, pl.ds(pc * half, half)], (px, py, pc)))
    return plan


def _plan_scatter_direct(src_refs, land_refs):
    x, y, c = _me()
    plan = []
    for s, land in zip(src_refs, land_refs):
        half = s.shape[1] // 2
        for k in range(1, 8):
            px, py, pc = _xor_peer(k)
            plan.append((s.at[2 * px + py, pl.ds(pc * half, half)], land.at[4 * x + 2 * y + c],
                         land.at[4 * px + 2 * py + pc], (px, py, pc)))
    return plan


def _plan_exchange_chips(src_refs, land_refs):
    x, y, c = _me()
    chip = 2 * x + y
    plan = []
    for s, land in zip(src_refs, land_refs):
        for _, (px, py) in _chip_peers():
            plan.append((s.at[2 * px + py], land.at[chip], land.at[2 * px + py], (px, py, c)))
    return plan


_plan_gather_direct.per_tensor = 6
_plan_scatter_direct.per_tensor = 7
_plan_exchange_chips.per_tensor = 3


def _start_copies(srcs, lands, plan_fn, name, after=None):
    n = len(srcs)
    n_copies = len(srcs) * plan_fn.per_tensor
    extra = [] if after is None else [after]

    def body(*refs):
        refs = refs[:2 * n] + refs[2 * n + len(extra):]
        send_sems, recv_sems, token = refs[2 * n], refs[2 * n + 1], refs[-1]
        for i, (src, dst, _, peer) in enumerate(plan_fn(refs[:n], refs[n:2 * n])):
            pltpu.make_async_remote_copy(src_ref=src, dst_ref=dst, send_sem=send_sems.at[i], recv_sem=recv_sems.at[i],
                                         device_id=peer, device_id_type=MESH).start()
        token[...] = jnp.zeros_like(token)

    arrays = list(srcs) + list(lands)
    outs = pl.pallas_call(
        body, name=name,
        out_shape=(pltpu.SemaphoreType.DMA((n_copies,)), pltpu.SemaphoreType.DMA((n_copies,)),
                   *[pltpu.HBM(a.shape, a.dtype) for a in arrays], jax.ShapeDtypeStruct((8, LANES), F32)),
        in_specs=[HBM_SPEC] * (2 * n) + [ANY] * len(extra),
        out_specs=(SEM_SPEC, SEM_SPEC, *[HBM_SPEC] * (2 * n), pl.BlockSpec(memory_space=pltpu.VMEM)),
        input_output_aliases={i: 2 + i for i in range(2 * n)},
        compiler_params=pltpu.CompilerParams(has_side_effects=DATAFLOW),
    )(*[pltpu.with_memory_space_constraint(a, pltpu.HBM) for a in arrays], *extra)
    return outs[0], outs[1], list(outs[2:2 + n]), list(outs[2 + n:2 + 2 * n]), outs[-1]


def _wait_copies(send_sems, recv_sems, srcs, lands, after, plan_fn, name):
    n = len(srcs)

    def body(*refs):
        send_sems, recv_sems = refs[2 * n], refs[2 * n + 1]
        for i, (src, _, arrival, peer) in enumerate(plan_fn(refs[:n], refs[n:2 * n])):
            cp = pltpu.make_async_remote_copy(src_ref=src, dst_ref=arrival, send_sem=send_sems.at[i], recv_sem=recv_sems.at[i],
                                              device_id=peer, device_id_type=MESH)
            cp.wait_send()
            cp.wait_recv()

    arrays = list(srcs) + list(lands)
    outs = pl.pallas_call(
        body, name=name,
        out_shape=tuple(pltpu.HBM(a.shape, a.dtype) for a in arrays),
        in_specs=[HBM_SPEC] * (2 * n) + [SEM_SPEC, SEM_SPEC, ANY],
        out_specs=tuple([HBM_SPEC] * (2 * n)),
        input_output_aliases={i: i for i in range(2 * n)},
        compiler_params=pltpu.CompilerParams(has_side_effects=DATAFLOW),
    )(*arrays, send_sems, recv_sems, after)
    return list(outs[n:])


def _rope_tables(positions):
    half = ROPE_DIM // 2
    S = positions.shape[0]
    inv_freq = ROPE_THETA ** (-jnp.arange(half, dtype=F32) / half)
    ang = positions.astype(F32)[:, None] * inv_freq
    cos, sin = jnp.cos(ang), jnp.sin(ang)
    zeros = functools.partial(jnp.zeros, dtype=F32)
    cc = jnp.concatenate([cos, cos, jnp.ones((S, HEAD_DIM - ROPE_DIM), F32)], axis=1)
    s1 = jnp.concatenate([-sin, zeros((S, HEAD_DIM - half))], axis=1)
    s2 = jnp.concatenate([zeros((S, half)), sin, zeros((S, HEAD_DIM - ROPE_DIM))], axis=1)
    return cc, s1, s2


def _ffn_fwd(x, gain, shift, scale, gate, w_gu, w_d, fs, tag):
    S = x.shape[0]
    tm = _pick(S, 512)
    h = _pre_fwd(x, gain, shift, scale, tag + "_pre")

    def swiglu_out(tile, ins, outs):
        outs[0][...] = tile
        outs[1][...] = _swiglu_fn(tile[:, :fs], tile[:, fs:]).astype(BF16)

    ab, s = _matmul(h, w_gu, "nn", F32, tag + "_gate_up", tm=tm, tn=2 * fs, tk=4096, b_stationary=True, epilogue=(swiglu_out, [], [
        ((S, 8 * fs), F32, (tm, 2 * fs), lambda i, j: (i, j)), ((S, 4 * fs), BF16, (tm, fs), lambda i, j: (i, j))]))
    f, xn = _matmul_residual(s, w_d, x, gate, 0.5, tag + "_down")
    return xn, (x, h, ab, s, f)


def _ffn_bwd(dxn, saved, gain, shift, scale, gate, w_gu, w_d, fs, tag, on_dw_d=None, on_dw_gu=None, dw_gu_dtype=F32):
    x, h, ab, s, f = saved
    df, dgate = _residual_bwd(gate, f, dxn, 0.5, tag + "_res_bwd")
    S = x.shape[0]
    tm = _pick(S, 512)
    dw_d = _matmul(s, df, "tn", BF16, tag + "_down_dw", tm=1408, tn=1024, tk=2048)
    if on_dw_d is not None:
        shift = shift + on_dw_d(dw_d)

    def swiglu_back(tile, ins, outs):
        _, vjp = jax.vjp(_swiglu_fn, ins[0][:, :fs], ins[0][:, fs:])
        da, db = vjp(tile)
        outs[0][:, :fs] = da.astype(BF16)
        outs[0][:, fs:] = db.astype(BF16)

    (dab,) = _matmul(df, w_d, "nt", F32, tag + "_down_dx", tm=tm, tn=fs, tk=4096, b_stationary=True, epilogue=(
        swiglu_back, [(ab, (tm, 2 * fs), lambda i, j: (i, j))], [((S, 8 * fs), BF16, (tm, 2 * fs), lambda i, j: (i, j))]))
    dw_gu = _matmul(h, dab, "tn", dw_gu_dtype, tag + "_gate_up_dw", tm=512, tn=2 * fs, tk=2048, col_slabs=True)
    tie = on_dw_gu(dw_gu) if on_dw_gu is not None else None
    dh = _matmul(dab, w_gu, "nt", F32, tag + "_gate_up_dx", tm=1024, tn=1024, tk=2816, after=tie)
    dx, dgain, dshift, dscale = _pre_bwd(x, gain, shift, scale, dh, dxn, tag + "_pre_bwd")
    return dx, dw_gu, dw_d, dgain, dshift, dscale, dgate


def _flat_pad(parts, rows, cols):
    flat = jnp.concatenate([p.reshape(-1).astype(F32) for p in parts])
    return jnp.pad(flat, (0, rows * cols - flat.shape[0])).reshape(rows, cols)


def _cols_to_slabs(w, n):
    R, NC = w.shape
    return jnp.transpose(w.reshape(R, n, NC // n), (1, 0, 2))


def kernel(x, c, positions, w_ada, b_ada, ffn1_norm, ffn1_w_gate, ffn1_w_up, ffn1_w_down, mix_norm, w_in, conv_w, q_norm, k_norm, a_log, dt_bias, delta_out_norm, w_out, ffn2_norm, ffn2_w_gate, ffn2_w_up, ffn2_w_down, loss_target, m_w_ada, m_b_ada, m_ffn1_norm, m_ffn1_w_gate, m_ffn1_w_up, m_ffn1_w_down, m_mix_norm, m_w_in, m_conv_w, m_q_norm, m_k_norm, m_a_log, m_dt_bias, m_delta_out_norm, m_w_out, m_ffn2_norm, m_ffn2_w_gate, m_ffn2_w_up, m_ffn2_w_down, v_w_ada, v_b_ada, v_ffn1_norm, v_ffn1_w_gate, v_ffn1_w_up, v_ffn1_w_down, v_mix_norm, v_w_in, v_conv_w, v_q_norm, v_k_norm, v_a_log, v_dt_bias, v_delta_out_norm, v_w_out, v_ffn2_norm, v_ffn2_w_gate, v_ffn2_w_up, v_ffn2_w_down):
    xi, yi, ci = _me()
    chip = 2 * xi + yi
    dev = 2 * chip + ci
    xs = x[0]
    S, D = xs.shape
    HA, HD = N_ATTN_HEADS, N_DELTA_HEADS
    fs = ffn1_w_gate.shape[2]
    n_mod_shard = w_ada.shape[2]
    in_shard = w_in.shape[2]
    in_width = 4 * in_shard
    in_pad = -(-in_width // LANES) * LANES
    conv_shard = conv_w.shape[2]
    conv_width = 4 * conv_shard

    pack0 = jnp.zeros((8, max(D, conv_shard)), F32)
    pack0 = pack0.at[0, :D].set(c[0]).at[1:1 + CONV_WIDTH, :conv_shard].set(conv_w[0])
    got0 = _all_gather_small(pack0, "gather_cond")
    c_all = got0[:, 0, :D]
    conv_full = jnp.transpose(got0[::2, 1:1 + CONV_WIDTH, :conv_shard], (1, 0, 2)).reshape(CONV_WIDTH, conv_width)
    b_ada_mine = lax.dynamic_slice(b_ada, (0, chip * n_mod_shard), (1, n_mod_shard))
    mod_part = _ada_fwd(c_all, w_ada[0], b_ada_mine, "ada_fwd")
    got1 = _all_gather_small(mod_part, "gather_mod")
    mod = lax.dynamic_index_in_dim(got1[::2], dev, axis=1, keepdims=False).reshape(1, 4 * n_mod_shard)
    sh1, sc1, gt1, sh2, sc2, gt2, sh3, sc3, gt3 = [mod[:, i * D:(i + 1) * D] for i in range(N_MOD)]

    shards = [w[0].astype(BF16) for w in (ffn1_w_gate, ffn1_w_up, ffn1_w_down, w_in, w_out, ffn2_w_gate, ffn2_w_up, ffn2_w_down)]
    gathered = _all_gather_shards(shards[:3], "gather_weights")
    g1g, g1u, g1d = [lax.dynamic_update_index_in_dim(g, s, chip, 0) for g, s in zip(gathered, shards[:3])]
    zones = [lax.dynamic_update_index_in_dim(lax.empty((4,) + s.shape, BF16), s, chip, 0) for s in shards[3:]]
    ag_in = _start_copies(shards[3:4], zones[:1], _plan_gather_direct, "gather_in_start")
    sh1 = sh1 + ag_in[4][0, 0]

    def gate_up(gg, gu):
        return jnp.transpose(jnp.concatenate([gg, gu], axis=2), (1, 0, 2)).reshape(D, 8 * fs)

    w_gu1, w_d1 = gate_up(g1g, g1u), g1d.reshape(4 * fs, D)

    x1, saved1 = _ffn_fwd(xs, ffn1_norm, sh1, sc1, gt1, w_gu1, w_d1, fs, "ffn1")
    (gin,) = _wait_copies(ag_in[0], ag_in[1], ag_in[2], ag_in[3], x1, _plan_gather_direct, "gather_in_wait")
    w_in_f = jnp.pad(jnp.transpose(gin, (1, 0, 2)).reshape(D, in_width), ((0, 0), (0, in_pad - in_width)))
    ag_rest = _start_copies(shards[4:], zones[1:], _plan_gather_direct, "gather_rest_start", after=gin)
    sh2 = sh2 + ag_rest[4][0, 0]

    cc, s1, s2 = _rope_tables(positions[0])
    alog_row = jnp.pad(a_log, ((0, 0), (0, LANES - HD)))
    dtb_row = jnp.pad(dt_bias, ((0, 0), (0, LANES - HD)))
    col_k, col_v, col_d, col_z, col_ab = HA, 2 * HA, 3 * HA, 3 * HA + 3 * HD, 3 * HA + 4 * HD
    h2 = _pre_fwd(x1, mix_norm, sh2, sc2, "mix_pre")
    proj = _matmul(h2, w_in_f, "nn", F32, "mix_in_proj", tm=512, tn=2432, tk=4096, b_stationary=True)
    attn_bias = _log_multiplicity_table(_pick(S, ATTN_TILE))
    qa, ka = _attn_prep_fwd(proj, q_norm, k_norm, cc, s1, s2, "attn_prep")
    o_near, lse_near = _attn_fwd(qa, ka, proj, col_v, attn_bias, "attn_fwd")
    oa, oa_b, lse = _attn_far_fwd(qa, ka, proj, col_v, o_near, lse_near, "attn_far_fwd")
    dqkv = _conv_fwd(proj, col_d, conv_width, conv_full, "conv_fwd")
    *prep, t_inv = _delta_prep_fwd(dqkv, proj, col_ab, alog_row, dtb_row, "delta_prep")
    od_raw, states = _delta_scan_fwd(*prep, "delta_scan")
    od = _post_fwd(od_raw, proj, col_z, delta_out_norm, "delta_post")
    o = jnp.concatenate([oa_b, od], axis=1)
    gout, g2g, g2u, g2d = _wait_copies(ag_rest[0], ag_rest[1], ag_rest[2], ag_rest[3], o, _plan_gather_direct, "gather_rest_wait")
    w_out_f = gout.reshape(-1, D)
    w_gu2, w_d2 = gate_up(g2g, g2u), g2d.reshape(4 * fs, D)
    mo, x2 = _matmul_residual(o, w_out_f, x1, gt2, 1.0, "mix_out_proj")

    x3, saved3 = _ffn_fwd(x2, ffn2_norm, sh3, sc3, gt3, w_gu2, w_d2, fs, "ffn2")
    loss_part, dy = _loss_head(x3, loss_target[0], "loss_head")
    loss = lax.psum(loss_part[0, 0], ("x", "y", "c"))

    dx2, dw_gu2, dw_d2, dgain3, dsh3, dsc3, dgt3 = _ffn_bwd(dy, saved3, ffn2_norm, sh3, sc3, gt3, w_gu2, w_d2, fs, "ffn2",
                                                            dw_gu_dtype=BF16)

    def scatter_start(slabs32, name):
        slabs16 = [s.astype(BF16) for s in slabs32]
        zones = []
        for s in slabs16:
            half = s.shape[1] // 2
            own = lax.dynamic_slice(s, (chip, ci * half, 0), (1, half, s.shape[2]))
            zones.append(lax.dynamic_update_slice(lax.empty((8, half, s.shape[2]), BF16), own, (dev, 0, 0)))
        return _start_copies(slabs16, zones, _plan_scatter_direct, name)

    rs_ffn2 = scatter_start([dw_gu2, dw_d2.reshape(4, fs, D)], "rs_ffn2_start")
    dmo, dgt2 = _residual_bwd(gt2 + rs_ffn2[4][0, 0], mo, dx2, 1.0, "mix_res_bwd")
    do = _matmul(dmo, w_out_f, "nt", F32, "mix_out_dx", tm=1024, tn=1024, tk=4096)
    dw_out = _matmul(o, dmo, "tn", BF16, "mix_out_dw", tm=1024, tn=1024, tk=2048)
    dq = _attn_bwd_dq(qa, ka, proj, col_v, oa, lse, do, 0, attn_bias, "attn_bwd_dq")
    dk, dv = _attn_bwd_dkv(qa, ka, proj, col_v, oa, lse, do, 0, attn_bias, "attn_bwd_dkv")
    far = _attn_far_bwd(qa, ka, proj, col_v, oa, lse, do, 0, "attn_far_bwd")
    dpq, dpk, dv, dq_gain, dk_gain = _attn_prep_bwd(proj, q_norm, k_norm, cc, s1, s2, (dq, dk, dv), far, "attn_prep_bwd")
    dod, dz, ddn_gain = _post_bwd(od_raw, proj, col_z, delta_out_norm, do, HA, "delta_post_bwd")
    cots = _delta_scan_bwd(*prep, states, dod, "delta_scan_bwd")
    ddq, ddk, ddv, dab, dalog, ddtb = _delta_prep_bwd(dqkv, proj, col_ab, alog_row, dtb_row, cots, t_inv, "delta_prep_bwd")
    dconv_in, dconv_w = _conv_bwd(proj, col_d, conv_width, conv_full, jnp.concatenate([ddq, ddk, ddv], axis=1), "conv_bwd")
    dproj = jnp.concatenate([dpq, dpk, dv, dconv_in, dz, dab], axis=1)
    dh2 = _matmul(dproj, w_in_f, "nt", F32, "mix_in_dx", tm=1024, tn=1024, tk=2816)
    dw_in = _matmul(h2, dproj, "tn", BF16, "mix_in_dw", tm=512, tn=2432, tk=2048)
    rs_mix = scatter_start([_cols_to_slabs(dw_in[:, :in_width], 4), dw_out.reshape(4, -1, D)], "rs_mix_start")
    dx1, dgain2, dsh2, dsc2 = _pre_bwd(x1, mix_norm, sh2, sc2, dh2, dx2, "mix_pre_bwd")

    rs_ffn1d = []

    def send_dw_d1(dw_d):
        rs_ffn1d.extend(scatter_start([dw_d.reshape(4, fs, D)], "rs_ffn1d_start"))
        return rs_ffn1d[4][0, 0]

    rs_gu1 = []

    def send_dw_gu1(slab):
        (other,) = _swap_halves_with_sibling([slab], "rs_sibling_swap")
        half = slab.shape[1] // 2
        mine = lax.dynamic_slice_in_dim(slab, ci * half, half, axis=1)
        p32, p16 = _add_cast([mine, other], [F32, BF16], "rs_chip_sum")
        rs_gu1.append(p32)
        rs_gu1.extend(_start_copies([p16], [lax.empty(p16.shape, BF16)], _plan_exchange_chips, "rs_chip_exchange_start"))
        return rs_gu1[5]

    dx0, dw_gu1, dw_d1, dgain1, dsh1, dsc1, dgt1 = _ffn_bwd(dx1, saved1, ffn1_norm, sh1, sc1, gt1 + rs_mix[4][0, 0], w_gu1, w_d1, fs,
                                                            "ffn1", on_dw_d=send_dw_d1, on_dw_gu=send_dw_gu1)

    (got,) = _wait_copies(rs_gu1[1], rs_gu1[2], rs_gu1[3], rs_gu1[4], dx0, _plan_exchange_chips, "rs_chip_exchange_wait")
    parts = [lax.dynamic_index_in_dim(rs_gu1[0], chip, axis=0, keepdims=True)]
    parts += [lax.dynamic_index_in_dim(got, (chip + k) % 4, axis=0, keepdims=True) for k in (1, 2, 3)]
    halves = [_add_cast(parts, [F32], "rs_total_0")[0][0]]
    arrived = _wait_copies(rs_ffn1d[0], rs_ffn1d[1], rs_ffn1d[2], rs_ffn1d[3], dx0, _plan_scatter_direct, "rs_ffn1d_wait")
    arrived += _wait_copies(rs_mix[0], rs_mix[1], rs_mix[2], rs_mix[3], dx0, _plan_scatter_direct, "rs_mix_wait")
    arrived += _wait_copies(rs_ffn2[0], rs_ffn2[1], rs_ffn2[2], rs_ffn2[3], dx0, _plan_scatter_direct, "rs_ffn2_wait")
    for t, zone in enumerate(arrived):
        halves.append(_add_cast([zone[d:d + 1] for d in range(8)], [F32], "rs_total_%d" % (t + 2))[0][0])
    theirs = _send_halves_to_sibling(halves, "rs_sibling_join")
    h_gu1, h_d1, h_in, h_out, h_gu2, h_d2 = zip(halves, theirs)

    n_small = N_MOD * D + 3 * D + 5 * LANES + CONV_WIDTH * conv_width
    cols_small = -(-n_small // (8 * LANES)) * LANES
    small = _flat_pad([dsh1, dsc1, dgt1, dsh2, dsc2, dgt2, dsh3, dsc3, dgt3, dgain1, dgain2, dgain3,
                       dq_gain, dk_gain, dalog, ddtb, ddn_gain, dconv_w], 8, cols_small)
    got2 = _all_gather_small(small, "gather_small_grads", after=theirs[0])
    small_sum = _add_cast([got2[d:d + 1] for d in range(8)], [F32], "sum_small_grads")[0].reshape(-1)
    dmod_all = got2.reshape(8, -1)[:, :N_MOD * D]
    off = [0]

    def take(n):
        off[0] += n
        return small_sum[off[0] - n:off[0]]

    g_b_ada = take(N_MOD * D).reshape(1, -1)
    g_ffn1_norm, g_mix_norm, g_ffn2_norm = take(D).reshape(1, D), take(D).reshape(1, D), take(D).reshape(1, D)
    g_q_norm, g_k_norm = take(LANES).reshape(1, -1), take(LANES).reshape(1, -1)
    g_a_log, g_dt_bias = take(LANES)[:HD].reshape(1, HD), take(LANES)[:HD].reshape(1, HD)
    g_dn = take(LANES).reshape(1, -1)
    g_conv_full = take(CONV_WIDTH * conv_width).reshape(CONV_WIDTH, conv_width)
    g_conv = lax.dynamic_slice(g_conv_full, (0, chip * conv_shard), (CONV_WIDTH, conv_shard))

    res = {}

    def upd(name, w, pair, col, m, v):
        res[name] = tuple(_adamw_halves(w, pair[0], pair[1], col, m, v, "adamw_" + name))

    upd("ffn1_w_gate", ffn1_w_gate, h_gu1, 0, m_ffn1_w_gate, v_ffn1_w_gate)
    upd("ffn1_w_up", ffn1_w_up, h_gu1, 1, m_ffn1_w_up, v_ffn1_w_up)
    upd("ffn1_w_down", ffn1_w_down, h_d1, 0, m_ffn1_w_down, v_ffn1_w_down)
    res["w_in"] = tuple(_adamw_halves_t(w_in, h_in[0], h_in[1], m_w_in, v_w_in, "adamw_w_in"))
    upd("w_out", w_out, h_out, 0, m_w_out, v_w_out)
    upd("ffn2_w_gate", ffn2_w_gate, h_gu2, 0, m_ffn2_w_gate, v_ffn2_w_gate)
    upd("ffn2_w_up", ffn2_w_up, h_gu2, 1, m_ffn2_w_up, v_ffn2_w_up)
    upd("ffn2_w_down", ffn2_w_down, h_d2, 0, m_ffn2_w_down, v_ffn2_w_down)
    d_cv, nm_cv, nv_cv = _adamw(conv_w[0], g_conv, m_conv_w[0], v_conv_w[0], "adamw_conv_w")
    res["conv_w"] = (g_conv[None], d_cv[None], nm_cv[None], nv_cv[None])

    dmod_mine = lax.dynamic_slice(dmod_all, (0, chip * n_mod_shard), (8, n_mod_shard))
    g, d, nm, nv = _adamw_outer(w_ada[0], m_w_ada[0], v_w_ada[0], jnp.transpose(c_all), dmod_mine, "adamw_w_ada")
    res["w_ada"] = (g[None], d[None], nm[None], nv[None])

    rep = [("b_ada", b_ada, g_b_ada, m_b_ada, v_b_ada), ("ffn1_norm", ffn1_norm, g_ffn1_norm, m_ffn1_norm, v_ffn1_norm),
           ("mix_norm", mix_norm, g_mix_norm, m_mix_norm, v_mix_norm), ("ffn2_norm", ffn2_norm, g_ffn2_norm, m_ffn2_norm, v_ffn2_norm),
           ("q_norm", q_norm, g_q_norm, m_q_norm, v_q_norm), ("k_norm", k_norm, g_k_norm, m_k_norm, v_k_norm),
           ("a_log", a_log, g_a_log, m_a_log, v_a_log), ("dt_bias", dt_bias, g_dt_bias, m_dt_bias, v_dt_bias),
           ("delta_out_norm", delta_out_norm, g_dn, m_delta_out_norm, v_delta_out_norm)]
    n_rep = sum(-(-r[1].shape[1] // LANES) * LANES for r in rep)
    cols_rep = -(-n_rep // (8 * LANES)) * LANES

    def pack_rep(idx):
        return _flat_pad([jnp.pad(r[idx], ((0, 0), (0, -r[idx].shape[1] % LANES))) for r in rep], 8, cols_rep)

    d_rep, nm_rep, nv_rep = [a.reshape(-1) for a in _adamw(pack_rep(1), pack_rep(2), pack_rep(3), pack_rep(4), "adamw_small")]
    o2 = 0
    for name, w, g, _, _ in rep:
        n = w.shape[1]
        res[name] = (g, d_rep[o2:o2 + n].reshape(1, n), nm_rep[o2:o2 + n].reshape(1, n), nv_rep[o2:o2 + n].reshape(1, n))
        o2 += -(-n // LANES) * LANES

    order = ["w_ada", "b_ada", "ffn1_norm", "ffn1_w_gate", "ffn1_w_up", "ffn1_w_down", "mix_norm", "w_in", "conv_w", "q_norm",
             "k_norm", "a_log", "dt_bias", "delta_out_norm", "w_out", "ffn2_norm", "ffn2_w_gate", "ffn2_w_up", "ffn2_w_down"]
    return (loss, dx0[None], *[res[n][0] for n in order], *[res[n][1] for n in order],
            *[res[n][2] for n in order], *[res[n][3] for n in order])
```
